```python
import jax, jax.numpy as jnp
from jax import lax
import numpy as np

D_MODEL = 1024
BATCH = 8
SEQ = 8192
DEPTH = 2

N_META = 16
CHUNK = 64
META_PAD = CHUNK - N_META
D_MIX = D_MODEL
LRU_WIDTH = D_MIX // 2
LRU_HEADS = 8
LRU_HEAD_DIM = LRU_WIDTH // LRU_HEADS
LRU_C = 8.0
CONV_WIDTH = 4
CONV_LEFT = 2
CONV_RIGHT = CONV_WIDTH - 1 - CONV_LEFT
GLA_WIDTH = D_MIX - LRU_WIDTH
GLA_HEADS = 4
GLA_DV = GLA_WIDTH // GLA_HEADS
GLA_DK = GLA_DV // 2
GLA_RANK = 16
GLA_GATE_NORM = 16.0
D_FF = 4 * D_MODEL
EPS = 1e-6

IN_WIDTHS = [LRU_WIDTH, LRU_WIDTH,
             GLA_HEADS * GLA_DK, GLA_HEADS * GLA_DK,
             GLA_WIDTH, GLA_WIDTH,
             GLA_RANK, GLA_RANK]
D_IN = int(sum(IN_WIDTHS))
IN_SPLITS = [int(s) for s in np.cumsum(IN_WIDTHS)[:-1]]

kernel_name = "bidir_hymba_rglru_gla_block"


def rmsnorm(x, g):
    xf = x.astype(jnp.float32)
    y = xf * lax.rsqrt(jnp.mean(xf * xf, axis=-1, keepdims=True) + EPS)
    return y.astype(x.dtype) * g


def centred_dwconv(x, w, b):
    y = lax.conv_general_dilated(x, w[:, None, :], window_strides=(1,),
                                 padding=[(CONV_LEFT, CONV_RIGHT)],
                                 dimension_numbers=('NWC', 'WIO', 'NWC'),
                                 feature_group_count=x.shape[-1])
    return y + b


def _lin_combine(c1, c2):
    a1, b1 = c1
    a2, b2 = c2
    return a1 * a2, a2 * b1 + b2


def rglru_direction(xc, wa, ba, wx, bx, lam, reverse):
    B, T, _ = xc.shape
    xh = xc.reshape(B, T, LRU_HEADS, LRU_HEAD_DIM)
    r = jax.nn.sigmoid(jnp.einsum('bthi,hij->bthj', xh, wa).reshape(B, T, LRU_WIDTH) + ba)
    i = jax.nn.sigmoid(jnp.einsum('bthi,hij->bthj', xh, wx).reshape(B, T, LRU_WIDTH) + bx)
    log_a = -LRU_C * r * jax.nn.softplus(-lam)
    a = jnp.exp(log_a)
    u = jnp.sqrt(-jnp.expm1(2.0 * log_a)) * (i * xc)
    _, h = lax.associative_scan(_lin_combine, (a, u), axis=1, reverse=reverse)
    return h


def rglru_group(x_br, gate_br, conv_w, conv_b,
                wa_f, ba_f, wx_f, bx_f, lam_f, wa_b, ba_b, wx_b, bx_b, lam_b):
    xc = centred_dwconv(x_br, conv_w, conv_b).astype(jnp.float32)
    h = (rglru_direction(xc, wa_f, ba_f, wx_f, bx_f, lam_f, reverse=False)
         + rglru_direction(xc, wa_b, ba_b, wx_b, bx_b, lam_b, reverse=True))
    return h.astype(x_br.dtype) * jax.nn.gelu(gate_br)


def insert_pad(t):
    z = jnp.zeros((t.shape[0], META_PAD, t.shape[2]), t.dtype)
    return jnp.concatenate([t[:, :N_META], z, t[:, N_META:]], axis=1)


def remove_pad(t):
    return jnp.concatenate([t[:, :N_META], t[:, CHUNK:]], axis=1)


def to_chunks(t, d):
    B, Tp = t.shape[0], t.shape[1]
    return t.reshape(B, Tp // CHUNK, CHUNK, GLA_HEADS, d).transpose(1, 0, 3, 2, 4)


def from_chunks(o):
    N, B = o.shape[0], o.shape[1]
    return o.transpose(1, 0, 3, 2, 4).reshape(B, N * CHUNK, GLA_HEADS * GLA_DV)


def gla_chunk_scan(q, k, v, g, strict):
    B, H = q.shape[1], q.shape[2]
    b = jnp.cumsum(g, axis=3)
    mask = jnp.tril(jnp.ones((CHUNK, CHUNK), dtype=bool), k=-1 if strict else 0)

    def step(S, inp):
        qc, kc, vc, bc = inp
        diff = bc[:, :, :, None, :] - bc[:, :, None, :, :]
        decay = jnp.where(mask[:, :, None], jnp.exp(jnp.minimum(diff, 0.0)), 0.0)
        scores = jnp.einsum('bhtc,bhtsc,bhsc->bhts', qc, decay, kc)
        intra = jnp.einsum('bhts,bhsv->bhtv', scores, vc)
        inter = jnp.einsum('bhtc,bhcv->bhtv', qc * jnp.exp(bc), S)
        b_last = bc[:, :, -1:, :]
        S_new = (jnp.exp(b_last)[:, :, 0, :, None] * S
                 + jnp.einsum('bhsc,bhsv->bhcv', kc * jnp.exp(b_last - bc), vc))
        return S_new, intra + inter

    S0 = jnp.zeros((B, H, GLA_DK, GLA_DV), jnp.float32)
    _, o = lax.scan(step, S0, (q, k, v, b))
    return o


def gla_bidirectional(q, k, v, g_f, g_b):
    qp, kp, vp, gfp, gbp = (insert_pad(t) for t in (q, k, v, g_f, g_b))
    o_f = gla_chunk_scan(to_chunks(qp, GLA_DK), to_chunks(kp, GLA_DK),
                         to_chunks(vp, GLA_DV), to_chunks(gfp, GLA_DK), strict=False)
    flip = lambda t: t[:, ::-1]
    o_b = gla_chunk_scan(to_chunks(flip(qp), GLA_DK), to_chunks(flip(kp), GLA_DK),
                         to_chunks(flip(vp), GLA_DV), to_chunks(flip(gbp), GLA_DK), strict=True)
    o = from_chunks(o_f) + flip(from_chunks(o_b))
    return remove_pad(o)


def gla_group(q, k, v, g_out, zg_f, zg_b, wg_f, bg_f, wg_b, bg_b, head_norm):
    B, T = q.shape[0], q.shape[1]
    in_dtype = q.dtype
    q = q.astype(jnp.float32) * (GLA_DK ** -0.5)
    g_f = jax.nn.log_sigmoid((zg_f @ wg_f + bg_f).astype(jnp.float32)) / GLA_GATE_NORM
    g_b = jax.nn.log_sigmoid((zg_b @ wg_b + bg_b).astype(jnp.float32)) / GLA_GATE_NORM
    o = gla_bidirectional(q, k.astype(jnp.float32), v.astype(jnp.float32), g_f, g_b)
    o = o.reshape(B, T, GLA_HEADS, GLA_DV)
    o = o * lax.rsqrt(jnp.mean(o * o, axis=-1, keepdims=True) + EPS)
    o = o.reshape(B, T, GLA_WIDTH).astype(in_dtype) * head_norm
    return o * jax.nn.silu(g_out)


def hybrid_mixer(h, w_in, conv_w, conv_b,
                 lru_wa_f, lru_ba_f, lru_wx_f, lru_bx_f, lru_lambda_f,
                 lru_wa_b, lru_ba_b, lru_wx_b, lru_bx_b, lru_lambda_b,
                 gla_wg_f, gla_bg_f, gla_wg_b, gla_bg_b, gla_head_norm, w_out):
    z = h @ w_in
    x_br, gate_br, q, k, v, g_out, zg_f, zg_b = jnp.split(z, IN_SPLITS, axis=-1)
    y_lru = rglru_group(x_br, gate_br, conv_w, conv_b,
                        lru_wa_f, lru_ba_f, lru_wx_f, lru_bx_f, lru_lambda_f,
                        lru_wa_b, lru_ba_b, lru_wx_b, lru_bx_b, lru_lambda_b)
    y_gla = gla_group(q, k, v, g_out, zg_f, zg_b,
                      gla_wg_f, gla_bg_f, gla_wg_b, gla_bg_b, gla_head_norm)
    return jnp.concatenate([y_lru, y_gla], axis=-1) @ w_out


def squared_relu_mlp(h, w_up, w_down):
    return jnp.square(jax.nn.relu(h @ w_up)) @ w_down


def _fwd_setup_inputs(seed: int = 0) -> dict:
    key = jax.random.key(seed)
    ks = iter(jax.random.split(key, 40))
    nrm = lambda shape, scale: jax.random.normal(next(ks), shape, jnp.float32) * scale
    gain = lambda shape: 1.0 + nrm(shape, 0.05)

    def lam(shape):
        u = jax.random.uniform(next(ks), shape, jnp.float32, 0.9, 0.999)
        s = u ** (1.0 / LRU_C)
        return jnp.log(s) - jnp.log1p(-s)

    L = DEPTH
    return {
        "x": nrm((BATCH, SEQ, D_MODEL), 1.0),
        "meta_tokens": nrm((N_META, D_MODEL), 1.0),
        "norm_mix_pre": gain((L, D_MODEL)),
        "norm_mix_post": gain((L, D_MODEL)),
        "norm_mlp_pre": gain((L, D_MODEL)),
        "norm_mlp_post": gain((L, D_MODEL)),
        "w_in": nrm((L, D_MODEL, D_IN), D_MODEL ** -0.5),
        "conv_w": nrm((L, CONV_WIDTH, LRU_WIDTH), CONV_WIDTH ** -0.5),
        "conv_b": nrm((L, LRU_WIDTH), 0.01),
        "lru_wa_f": nrm((L, LRU_HEADS, LRU_HEAD_DIM, LRU_HEAD_DIM), LRU_HEAD_DIM ** -0.5),
        "lru_ba_f": nrm((L, LRU_WIDTH), 0.01),
        "lru_wx_f": nrm((L, LRU_HEADS, LRU_HEAD_DIM, LRU_HEAD_DIM), LRU_HEAD_DIM ** -0.5),
        "lru_bx_f": nrm((L, LRU_WIDTH), 0.01),
        "lru_lambda_f": lam((L, LRU_WIDTH)),
        "lru_wa_b": nrm((L, LRU_HEADS, LRU_HEAD_DIM, LRU_HEAD_DIM), LRU_HEAD_DIM ** -0.5),
        "lru_ba_b": nrm((L, LRU_WIDTH), 0.01),
        "lru_wx_b": nrm((L, LRU_HEADS, LRU_HEAD_DIM, LRU_HEAD_DIM), LRU_HEAD_DIM ** -0.5),
        "lru_bx_b": nrm((L, LRU_WIDTH), 0.01),
        "lru_lambda_b": lam((L, LRU_WIDTH)),
        "gla_wg_f": nrm((L, GLA_RANK, GLA_HEADS * GLA_DK), GLA_RANK ** -0.5),
        "gla_bg_f": nrm((L, GLA_HEADS * GLA_DK), 0.1),
        "gla_wg_b": nrm((L, GLA_RANK, GLA_HEADS * GLA_DK), GLA_RANK ** -0.5),
        "gla_bg_b": nrm((L, GLA_HEADS * GLA_DK), 0.1),
        "gla_head_norm": gain((L, GLA_WIDTH)),
        "w_out": nrm((L, D_MIX, D_MODEL), D_MIX ** -0.5),
        "w_mlp_up": nrm((L, D_MODEL, D_FF), D_MODEL ** -0.5),
        "w_mlp_down": nrm((L, D_FF, D_MODEL), D_FF ** -0.5),
    }


def _fwd_reference(x, meta_tokens, norm_mix_pre, norm_mix_post, norm_mlp_pre, norm_mlp_post,
              w_in, conv_w, conv_b,
              lru_wa_f, lru_ba_f, lru_wx_f, lru_bx_f, lru_lambda_f,
              lru_wa_b, lru_ba_b, lru_wx_b, lru_bx_b, lru_lambda_b,
              gla_wg_f, gla_bg_f, gla_wg_b, gla_bg_b, gla_head_norm,
              w_out, w_mlp_up, w_mlp_down):
    B = x.shape[0]
    meta = jnp.broadcast_to(meta_tokens.astype(x.dtype)[None], (B, N_META, D_MODEL))
    h = jnp.concatenate([meta, x], axis=1)
    for l in range(DEPTH):
        mix = hybrid_mixer(rmsnorm(h, norm_mix_pre[l]), w_in[l], conv_w[l], conv_b[l],
                           lru_wa_f[l], lru_ba_f[l], lru_wx_f[l], lru_bx_f[l], lru_lambda_f[l],
                           lru_wa_b[l], lru_ba_b[l], lru_wx_b[l], lru_bx_b[l], lru_lambda_b[l],
                           gla_wg_f[l], gla_bg_f[l], gla_wg_b[l], gla_bg_b[l], gla_head_norm[l],
                           w_out[l])
        h = h + rmsnorm(mix, norm_mix_post[l])
        ff = squared_relu_mlp(rmsnorm(h, norm_mlp_pre[l]), w_mlp_up[l], w_mlp_down[l])
        h = h + rmsnorm(ff, norm_mlp_post[l])
    return h[:, N_META:]


import jax as _jax
import jax.numpy as _jnp

TWIN_FORMAT = 'train_step'
FWD_PARAMS = ['x', 'meta_tokens', 'norm_mix_pre', 'norm_mix_post', 'norm_mlp_pre', 'norm_mlp_post', 'w_in', 'conv_w', 'conv_b', 'lru_wa_f', 'lru_ba_f', 'lru_wx_f', 'lru_bx_f', 'lru_lambda_f', 'lru_wa_b', 'lru_ba_b', 'lru_wx_b', 'lru_bx_b', 'lru_lambda_b', 'gla_wg_f', 'gla_bg_f', 'gla_wg_b', 'gla_bg_b', 'gla_head_norm', 'w_out', 'w_mlp_up', 'w_mlp_down']
TWIN_WEIGHTS = ['meta_tokens', 'norm_mix_pre', 'norm_mix_post', 'norm_mlp_pre', 'norm_mlp_post', 'w_in', 'conv_w', 'conv_b', 'lru_wa_f', 'lru_ba_f', 'lru_wx_f', 'lru_bx_f', 'lru_lambda_f', 'lru_wa_b', 'lru_ba_b', 'lru_wx_b', 'lru_bx_b', 'lru_lambda_b', 'gla_wg_f', 'gla_bg_f', 'gla_wg_b', 'gla_bg_b', 'gla_head_norm', 'w_out', 'w_mlp_up', 'w_mlp_down']
TWIN_DIFF_INPUT = 'x'
TWIN_INPUTS = ['x', 'meta_tokens', 'norm_mix_pre', 'norm_mix_post', 'norm_mlp_pre', 'norm_mlp_post', 'w_in', 'conv_w', 'conv_b', 'lru_wa_f', 'lru_ba_f', 'lru_wx_f', 'lru_bx_f', 'lru_lambda_f', 'lru_wa_b', 'lru_ba_b', 'lru_wx_b', 'lru_bx_b', 'lru_lambda_b', 'gla_wg_f', 'gla_bg_f', 'gla_wg_b', 'gla_bg_b', 'gla_head_norm', 'w_out', 'w_mlp_up', 'w_mlp_down', 'loss_target', 'm_meta_tokens', 'm_norm_mix_pre', 'm_norm_mix_post', 'm_norm_mlp_pre', 'm_norm_mlp_post', 'm_w_in', 'm_conv_w', 'm_conv_b', 'm_lru_wa_f', 'm_lru_ba_f', 'm_lru_wx_f', 'm_lru_bx_f', 'm_lru_lambda_f', 'm_lru_wa_b', 'm_lru_ba_b', 'm_lru_wx_b', 'm_lru_bx_b', 'm_lru_lambda_b', 'm_gla_wg_f', 'm_gla_bg_f', 'm_gla_wg_b', 'm_gla_bg_b', 'm_gla_head_norm', 'm_w_out', 'm_w_mlp_up', 'm_w_mlp_down', 'v_meta_tokens', 'v_norm_mix_pre', 'v_norm_mix_post', 'v_norm_mlp_pre', 'v_norm_mlp_post', 'v_w_in', 'v_conv_w', 'v_conv_b', 'v_lru_wa_f', 'v_lru_ba_f', 'v_lru_wx_f', 'v_lru_bx_f', 'v_lru_lambda_f', 'v_lru_wa_b', 'v_lru_ba_b', 'v_lru_wx_b', 'v_lru_bx_b', 'v_lru_lambda_b', 'v_gla_wg_f', 'v_gla_bg_f', 'v_gla_wg_b', 'v_gla_bg_b', 'v_gla_head_norm', 'v_w_out', 'v_w_mlp_up', 'v_w_mlp_down']
TWIN_OUTPUTS = ['loss', 'grad_x', 'grad_meta_tokens', 'grad_norm_mix_pre', 'grad_norm_mix_post', 'grad_norm_mlp_pre', 'grad_norm_mlp_post', 'grad_w_in', 'grad_conv_w', 'grad_conv_b', 'grad_lru_wa_f', 'grad_lru_ba_f', 'grad_lru_wx_f', 'grad_lru_bx_f', 'grad_lru_lambda_f', 'grad_lru_wa_b', 'grad_lru_ba_b', 'grad_lru_wx_b', 'grad_lru_bx_b', 'grad_lru_lambda_b', 'grad_gla_wg_f', 'grad_gla_bg_f', 'grad_gla_wg_b', 'grad_gla_bg_b', 'grad_gla_head_norm', 'grad_w_out', 'grad_w_mlp_up', 'grad_w_mlp_down', 'delta_meta_tokens', 'delta_norm_mix_pre', 'delta_norm_mix_post', 'delta_norm_mlp_pre', 'delta_norm_mlp_post', 'delta_w_in', 'delta_conv_w', 'delta_conv_b', 'delta_lru_wa_f', 'delta_lru_ba_f', 'delta_lru_wx_f', 'delta_lru_bx_f', 'delta_lru_lambda_f', 'delta_lru_wa_b', 'delta_lru_ba_b', 'delta_lru_wx_b', 'delta_lru_bx_b', 'delta_lru_lambda_b', 'delta_gla_wg_f', 'delta_gla_bg_f', 'delta_gla_wg_b', 'delta_gla_bg_b', 'delta_gla_head_norm', 'delta_w_out', 'delta_w_mlp_up', 'delta_w_mlp_down', 'new_m_meta_tokens', 'new_m_norm_mix_pre', 'new_m_norm_mix_post', 'new_m_norm_mlp_pre', 'new_m_norm_mlp_post', 'new_m_w_in', 'new_m_conv_w', 'new_m_conv_b', 'new_m_lru_wa_f', 'new_m_lru_ba_f', 'new_m_lru_wx_f', 'new_m_lru_bx_f', 'new_m_lru_lambda_f', 'new_m_lru_wa_b', 'new_m_lru_ba_b', 'new_m_lru_wx_b', 'new_m_lru_bx_b', 'new_m_lru_lambda_b', 'new_m_gla_wg_f', 'new_m_gla_bg_f', 'new_m_gla_wg_b', 'new_m_gla_bg_b', 'new_m_gla_head_norm', 'new_m_w_out', 'new_m_w_mlp_up', 'new_m_w_mlp_down', 'new_v_meta_tokens', 'new_v_norm_mix_pre', 'new_v_norm_mix_post', 'new_v_norm_mlp_pre', 'new_v_norm_mlp_post', 'new_v_w_in', 'new_v_conv_w', 'new_v_conv_b', 'new_v_lru_wa_f', 'new_v_lru_ba_f', 'new_v_lru_wx_f', 'new_v_lru_bx_f', 'new_v_lru_lambda_f', 'new_v_lru_wa_b', 'new_v_lru_ba_b', 'new_v_lru_wx_b', 'new_v_lru_bx_b', 'new_v_lru_lambda_b', 'new_v_gla_wg_f', 'new_v_gla_bg_f', 'new_v_gla_wg_b', 'new_v_gla_bg_b', 'new_v_gla_head_norm', 'new_v_w_out', 'new_v_w_mlp_up', 'new_v_w_mlp_down']
TWIN_LEAF_KINDS = {'loss': 'loss', 'grad_x': 'grad_x', 'grad_meta_tokens': 'grad_w', 'grad_norm_mix_pre': 'grad_w', 'grad_norm_mix_post': 'grad_w', 'grad_norm_mlp_pre': 'grad_w', 'grad_norm_mlp_post': 'grad_w', 'grad_w_in': 'grad_w', 'grad_conv_w': 'grad_w', 'grad_conv_b': 'grad_w', 'grad_lru_wa_f': 'grad_w', 'grad_lru_ba_f': 'grad_w', 'grad_lru_wx_f': 'grad_w', 'grad_lru_bx_f': 'grad_w', 'grad_lru_lambda_f': 'grad_w', 'grad_lru_wa_b': 'grad_w', 'grad_lru_ba_b': 'grad_w', 'grad_lru_wx_b': 'grad_w', 'grad_lru_bx_b': 'grad_w', 'grad_lru_lambda_b': 'grad_w', 'grad_gla_wg_f': 'grad_w', 'grad_gla_bg_f': 'grad_w', 'grad_gla_wg_b': 'grad_w', 'grad_gla_bg_b': 'grad_w', 'grad_gla_head_norm': 'grad_w', 'grad_w_out': 'grad_w', 'grad_w_mlp_up': 'grad_w', 'grad_w_mlp_down': 'grad_w', 'delta_meta_tokens': 'delta_w', 'delta_norm_mix_pre': 'delta_w', 'delta_norm_mix_post': 'delta_w', 'delta_norm_mlp_pre': 'delta_w', 'delta_norm_mlp_post': 'delta_w', 'delta_w_in': 'delta_w', 'delta_conv_w': 'delta_w', 'delta_conv_b': 'delta_w', 'delta_lru_wa_f': 'delta_w', 'delta_lru_ba_f': 'delta_w', 'delta_lru_wx_f': 'delta_w', 'delta_lru_bx_f': 'delta_w', 'delta_lru_lambda_f': 'delta_w', 'delta_lru_wa_b': 'delta_w', 'delta_lru_ba_b': 'delta_w', 'delta_lru_wx_b': 'delta_w', 'delta_lru_bx_b': 'delta_w', 'delta_lru_lambda_b': 'delta_w', 'delta_gla_wg_f': 'delta_w', 'delta_gla_bg_f': 'delta_w', 'delta_gla_wg_b': 'delta_w', 'delta_gla_bg_b': 'delta_w', 'delta_gla_head_norm': 'delta_w', 'delta_w_out': 'delta_w', 'delta_w_mlp_up': 'delta_w', 'delta_w_mlp_down': 'delta_w', 'new_m_meta_tokens': 'new_m', 'new_m_norm_mix_pre': 'new_m', 'new_m_norm_mix_post': 'new_m', 'new_m_norm_mlp_pre': 'new_m', 'new_m_norm_mlp_post': 'new_m', 'new_m_w_in': 'new_m', 'new_m_conv_w': 'new_m', 'new_m_conv_b': 'new_m', 'new_m_lru_wa_f': 'new_m', 'new_m_lru_ba_f': 'new_m', 'new_m_lru_wx_f': 'new_m', 'new_m_lru_bx_f': 'new_m', 'new_m_lru_lambda_f': 'new_m', 'new_m_lru_wa_b': 'new_m', 'new_m_lru_ba_b': 'new_m', 'new_m_lru_wx_b': 'new_m', 'new_m_lru_bx_b': 'new_m', 'new_m_lru_lambda_b': 'new_m', 'new_m_gla_wg_f': 'new_m', 'new_m_gla_bg_f': 'new_m', 'new_m_gla_wg_b': 'new_m', 'new_m_gla_bg_b': 'new_m', 'new_m_gla_head_norm': 'new_m', 'new_m_w_out': 'new_m', 'new_m_w_mlp_up': 'new_m', 'new_m_w_mlp_down': 'new_m', 'new_v_meta_tokens': 'new_v', 'new_v_norm_mix_pre': 'new_v', 'new_v_norm_mix_post': 'new_v', 'new_v_norm_mlp_pre': 'new_v', 'new_v_norm_mlp_post': 'new_v', 'new_v_w_in': 'new_v', 'new_v_conv_w': 'new_v', 'new_v_conv_b': 'new_v', 'new_v_lru_wa_f': 'new_v', 'new_v_lru_ba_f': 'new_v', 'new_v_lru_wx_f': 'new_v', 'new_v_lru_bx_f': 'new_v', 'new_v_lru_lambda_f': 'new_v', 'new_v_lru_wa_b': 'new_v', 'new_v_lru_ba_b': 'new_v', 'new_v_lru_wx_b': 'new_v', 'new_v_lru_bx_b': 'new_v', 'new_v_lru_lambda_b': 'new_v', 'new_v_gla_wg_f': 'new_v', 'new_v_gla_bg_f': 'new_v', 'new_v_gla_wg_b': 'new_v', 'new_v_gla_bg_b': 'new_v', 'new_v_gla_head_norm': 'new_v', 'new_v_w_out': 'new_v', 'new_v_w_mlp_up': 'new_v', 'new_v_w_mlp_down': 'new_v'}


def _forward(args):
    return _fwd_reference(*[args[k] for k in FWD_PARAMS])


def _output_shape():
    def fwd():
        inp = _fwd_setup_inputs(0)
        return _fwd_reference(*[inp[k] for k in FWD_PARAMS])
    out = _jax.eval_shape(fwd)
    return out.shape, out.dtype

N_MICROBATCH = 1
ADAM_LR = 0.001
ADAM_B1 = 0.9
ADAM_B2 = 0.999
ADAM_EPS = 1e-08
ADAM_WD = 0.01
ADAM_STEP = 10
PER_EXAMPLE_BATCH_AXIS = {'x': 0, 'loss_target': 0}
SHARED_INPUTS = []
_WEIGHT_DTYPES = {'meta_tokens': _jnp.float32, 'norm_mix_pre': _jnp.float32, 'norm_mix_post': _jnp.float32, 'norm_mlp_pre': _jnp.float32, 'norm_mlp_post': _jnp.float32, 'w_in': _jnp.float32, 'conv_w': _jnp.float32, 'conv_b': _jnp.float32, 'lru_wa_f': _jnp.float32, 'lru_ba_f': _jnp.float32, 'lru_wx_f': _jnp.float32, 'lru_bx_f': _jnp.float32, 'lru_lambda_f': _jnp.float32, 'lru_wa_b': _jnp.float32, 'lru_ba_b': _jnp.float32, 'lru_wx_b': _jnp.float32, 'lru_bx_b': _jnp.float32, 'lru_lambda_b': _jnp.float32, 'gla_wg_f': _jnp.float32, 'gla_bg_f': _jnp.float32, 'gla_wg_b': _jnp.float32, 'gla_bg_b': _jnp.float32, 'gla_head_norm': _jnp.float32, 'w_out': _jnp.float32, 'w_mlp_up': _jnp.float32, 'w_mlp_down': _jnp.float32}
MOMENT_SCALE = {'meta_tokens': 9.976003e-02, 'norm_mix_pre': 9.885960e+00, 'norm_mix_post': 6.624959e+01, 'norm_mlp_pre': 7.069938e+00, 'norm_mlp_post': 6.975350e+01, 'w_in': 6.480567e+00, 'conv_w': 1.955532e+01, 'conv_b': 1.943604e+02, 'lru_wa_f': 3.336240e+00, 'lru_ba_f': 2.022174e+00, 'lru_wx_f': 6.147936e+00, 'lru_bx_f': 3.477553e+00, 'lru_lambda_f': 3.829594e+00, 'lru_wa_b': 3.285917e+00, 'lru_ba_b': 2.041850e+00, 'lru_wx_b': 6.130486e+00, 'lru_bx_b': 3.724765e+00, 'lru_lambda_b': 3.213872e+00, 'gla_wg_f': 1.056863e-01, 'gla_bg_f': 3.896167e-01, 'gla_wg_b': 9.525829e-02, 'gla_bg_b': 3.936334e-01, 'gla_head_norm': 1.709311e+00, 'w_out': 1.475303e+01, 'w_mlp_up': 3.725959e+00, 'w_mlp_down': 2.062267e+01}


def _to_microbatches(a, axis):
    t = _jnp.moveaxis(a, axis, 0)
    t = t.reshape((N_MICROBATCH, t.shape[0] // N_MICROBATCH) + t.shape[1:])
    return _jnp.moveaxis(t, 1, axis + 1)


def setup_inputs(seed: int = 0) -> dict:
    inp = _fwd_setup_inputs(seed)
    key = _jax.random.fold_in(_jax.random.key(seed), 7919)
    shape, _ = _output_shape()
    out = dict(inp)
    out["loss_target"] = _jax.random.normal(_jax.random.fold_in(key, 0), shape, _jnp.float32)
    for i, name in enumerate(TWIN_WEIGHTS):
        w = inp[name].astype(_jnp.float32)
        if MOMENT_SCALE is None:
            s = _jnp.sqrt(_jnp.mean(_jnp.square(w)) + 1e-30)
        else:
            s = MOMENT_SCALE[name]
        km, kv = _jax.random.split(_jax.random.fold_in(key, i + 1))
        out[name] = w
        out["m_" + name] = s * _jax.random.normal(km, w.shape, _jnp.float32)
        out["v_" + name] = (s * s) * _jax.random.uniform(kv, w.shape, _jnp.float32, 0.5, 1.5)
    if N_MICROBATCH > 1:
        for name, axis in PER_EXAMPLE_BATCH_AXIS.items():
            out[name] = _to_microbatches(out[name], axis)
    return {'x': out['x'], 'meta_tokens': out['meta_tokens'], 'norm_mix_pre': out['norm_mix_pre'], 'norm_mix_post': out['norm_mix_post'], 'norm_mlp_pre': out['norm_mlp_pre'], 'norm_mlp_post': out['norm_mlp_post'], 'w_in': out['w_in'], 'conv_w': out['conv_w'], 'conv_b': out['conv_b'], 'lru_wa_f': out['lru_wa_f'], 'lru_ba_f': out['lru_ba_f'], 'lru_wx_f': out['lru_wx_f'], 'lru_bx_f': out['lru_bx_f'], 'lru_lambda_f': out['lru_lambda_f'], 'lru_wa_b': out['lru_wa_b'], 'lru_ba_b': out['lru_ba_b'], 'lru_wx_b': out['lru_wx_b'], 'lru_bx_b': out['lru_bx_b'], 'lru_lambda_b': out['lru_lambda_b'], 'gla_wg_f': out['gla_wg_f'], 'gla_bg_f': out['gla_bg_f'], 'gla_wg_b': out['gla_wg_b'], 'gla_bg_b': out['gla_bg_b'], 'gla_head_norm': out['gla_head_norm'], 'w_out': out['w_out'], 'w_mlp_up': out['w_mlp_up'], 'w_mlp_down': out['w_mlp_down'], 'loss_target': out['loss_target'], 'm_meta_tokens': out['m_meta_tokens'], 'm_norm_mix_pre': out['m_norm_mix_pre'], 'm_norm_mix_post': out['m_norm_mix_post'], 'm_norm_mlp_pre': out['m_norm_mlp_pre'], 'm_norm_mlp_post': out['m_norm_mlp_post'], 'm_w_in': out['m_w_in'], 'm_conv_w': out['m_conv_w'], 'm_conv_b': out['m_conv_b'], 'm_lru_wa_f': out['m_lru_wa_f'], 'm_lru_ba_f': out['m_lru_ba_f'], 'm_lru_wx_f': out['m_lru_wx_f'], 'm_lru_bx_f': out['m_lru_bx_f'], 'm_lru_lambda_f': out['m_lru_lambda_f'], 'm_lru_wa_b': out['m_lru_wa_b'], 'm_lru_ba_b': out['m_lru_ba_b'], 'm_lru_wx_b': out['m_lru_wx_b'], 'm_lru_bx_b': out['m_lru_bx_b'], 'm_lru_lambda_b': out['m_lru_lambda_b'], 'm_gla_wg_f': out['m_gla_wg_f'], 'm_gla_bg_f': out['m_gla_bg_f'], 'm_gla_wg_b': out['m_gla_wg_b'], 'm_gla_bg_b': out['m_gla_bg_b'], 'm_gla_head_norm': out['m_gla_head_norm'], 'm_w_out': out['m_w_out'], 'm_w_mlp_up': out['m_w_mlp_up'], 'm_w_mlp_down': out['m_w_mlp_down'], 'v_meta_tokens': out['v_meta_tokens'], 'v_norm_mix_pre': out['v_norm_mix_pre'], 'v_norm_mix_post': out['v_norm_mix_post'], 'v_norm_mlp_pre': out['v_norm_mlp_pre'], 'v_norm_mlp_post': out['v_norm_mlp_post'], 'v_w_in': out['v_w_in'], 'v_conv_w': out['v_conv_w'], 'v_conv_b': out['v_conv_b'], 'v_lru_wa_f': out['v_lru_wa_f'], 'v_lru_ba_f': out['v_lru_ba_f'], 'v_lru_wx_f': out['v_lru_wx_f'], 'v_lru_bx_f': out['v_lru_bx_f'], 'v_lru_lambda_f': out['v_lru_lambda_f'], 'v_lru_wa_b': out['v_lru_wa_b'], 'v_lru_ba_b': out['v_lru_ba_b'], 'v_lru_wx_b': out['v_lru_wx_b'], 'v_lru_bx_b': out['v_lru_bx_b'], 'v_lru_lambda_b': out['v_lru_lambda_b'], 'v_gla_wg_f': out['v_gla_wg_f'], 'v_gla_bg_f': out['v_gla_bg_f'], 'v_gla_wg_b': out['v_gla_wg_b'], 'v_gla_bg_b': out['v_gla_bg_b'], 'v_gla_head_norm': out['v_gla_head_norm'], 'v_w_out': out['v_w_out'], 'v_w_mlp_up': out['v_w_mlp_up'], 'v_w_mlp_down': out['v_w_mlp_down']}


def _loss(weights, diff, rest, loss_target):
    with _jax.named_scope("forward"):
        args = {**rest, TWIN_DIFF_INPUT: diff, **{k: w.astype(_WEIGHT_DTYPES[k]) for k, w in weights.items()}}
        y = _forward(args)
    with _jax.named_scope("loss_head"):
        err = _jnp.square(y.astype(_jnp.float32) - loss_target)
        return 0.5 * _jnp.sum(_jnp.mean(err, axis=-1)) if err.ndim else 0.5 * err


def _adamw(w, g, m, v):
    m = ADAM_B1 * m + (1.0 - ADAM_B1) * g
    v = ADAM_B2 * v + (1.0 - ADAM_B2) * _jnp.square(g)
    m_hat = m / (1.0 - ADAM_B1 ** ADAM_STEP)
    v_hat = v / (1.0 - ADAM_B2 ** ADAM_STEP)
    delta = -ADAM_LR * (m_hat / (_jnp.sqrt(v_hat) + ADAM_EPS) + ADAM_WD * w)
    return delta, m, v


def reference(x, meta_tokens, norm_mix_pre, norm_mix_post, norm_mlp_pre, norm_mlp_post, w_in, conv_w, conv_b, lru_wa_f, lru_ba_f, lru_wx_f, lru_bx_f, lru_lambda_f, lru_wa_b, lru_ba_b, lru_wx_b, lru_bx_b, lru_lambda_b, gla_wg_f, gla_bg_f, gla_wg_b, gla_bg_b, gla_head_norm, w_out, w_mlp_up, w_mlp_down, loss_target, m_meta_tokens, m_norm_mix_pre, m_norm_mix_post, m_norm_mlp_pre, m_norm_mlp_post, m_w_in, m_conv_w, m_conv_b, m_lru_wa_f, m_lru_ba_f, m_lru_wx_f, m_lru_bx_f, m_lru_lambda_f, m_lru_wa_b, m_lru_ba_b, m_lru_wx_b, m_lru_bx_b, m_lru_lambda_b, m_gla_wg_f, m_gla_bg_f, m_gla_wg_b, m_gla_bg_b, m_gla_head_norm, m_w_out, m_w_mlp_up, m_w_mlp_down, v_meta_tokens, v_norm_mix_pre, v_norm_mix_post, v_norm_mlp_pre, v_norm_mlp_post, v_w_in, v_conv_w, v_conv_b, v_lru_wa_f, v_lru_ba_f, v_lru_wx_f, v_lru_bx_f, v_lru_lambda_f, v_lru_wa_b, v_lru_ba_b, v_lru_wx_b, v_lru_bx_b, v_lru_lambda_b, v_gla_wg_f, v_gla_bg_f, v_gla_wg_b, v_gla_bg_b, v_gla_head_norm, v_w_out, v_w_mlp_up, v_w_mlp_down):
    given = dict(x=x, meta_tokens=meta_tokens, norm_mix_pre=norm_mix_pre, norm_mix_post=norm_mix_post, norm_mlp_pre=norm_mlp_pre, norm_mlp_post=norm_mlp_post, w_in=w_in, conv_w=conv_w, conv_b=conv_b, lru_wa_f=lru_wa_f, lru_ba_f=lru_ba_f, lru_wx_f=lru_wx_f, lru_bx_f=lru_bx_f, lru_lambda_f=lru_lambda_f, lru_wa_b=lru_wa_b, lru_ba_b=lru_ba_b, lru_wx_b=lru_wx_b, lru_bx_b=lru_bx_b, lru_lambda_b=lru_lambda_b, gla_wg_f=gla_wg_f, gla_bg_f=gla_bg_f, gla_wg_b=gla_wg_b, gla_bg_b=gla_bg_b, gla_head_norm=gla_head_norm, w_out=w_out, w_mlp_up=w_mlp_up, w_mlp_down=w_mlp_down, loss_target=loss_target, m_meta_tokens=m_meta_tokens, m_norm_mix_pre=m_norm_mix_pre, m_norm_mix_post=m_norm_mix_post, m_norm_mlp_pre=m_norm_mlp_pre, m_norm_mlp_post=m_norm_mlp_post, m_w_in=m_w_in, m_conv_w=m_conv_w, m_conv_b=m_conv_b, m_lru_wa_f=m_lru_wa_f, m_lru_ba_f=m_lru_ba_f, m_lru_wx_f=m_lru_wx_f, m_lru_bx_f=m_lru_bx_f, m_lru_lambda_f=m_lru_lambda_f, m_lru_wa_b=m_lru_wa_b, m_lru_ba_b=m_lru_ba_b, m_lru_wx_b=m_lru_wx_b, m_lru_bx_b=m_lru_bx_b, m_lru_lambda_b=m_lru_lambda_b, m_gla_wg_f=m_gla_wg_f, m_gla_bg_f=m_gla_bg_f, m_gla_wg_b=m_gla_wg_b, m_gla_bg_b=m_gla_bg_b, m_gla_head_norm=m_gla_head_norm, m_w_out=m_w_out, m_w_mlp_up=m_w_mlp_up, m_w_mlp_down=m_w_mlp_down, v_meta_tokens=v_meta_tokens, v_norm_mix_pre=v_norm_mix_pre, v_norm_mix_post=v_norm_mix_post, v_norm_mlp_pre=v_norm_mlp_pre, v_norm_mlp_post=v_norm_mlp_post, v_w_in=v_w_in, v_conv_w=v_conv_w, v_conv_b=v_conv_b, v_lru_wa_f=v_lru_wa_f, v_lru_ba_f=v_lru_ba_f, v_lru_wx_f=v_lru_wx_f, v_lru_bx_f=v_lru_bx_f, v_lru_lambda_f=v_lru_lambda_f, v_lru_wa_b=v_lru_wa_b, v_lru_ba_b=v_lru_ba_b, v_lru_wx_b=v_lru_wx_b, v_lru_bx_b=v_lru_bx_b, v_lru_lambda_b=v_lru_lambda_b, v_gla_wg_f=v_gla_wg_f, v_gla_bg_f=v_gla_bg_f, v_gla_wg_b=v_gla_wg_b, v_gla_bg_b=v_gla_bg_b, v_gla_head_norm=v_gla_head_norm, v_w_out=v_w_out, v_w_mlp_up=v_w_mlp_up, v_w_mlp_down=v_w_mlp_down)
    weights = {n: given[n] for n in TWIN_WEIGHTS}
    shared = {n: given[n] for n in SHARED_INPUTS}
    per_example = {n: given[n] for n in ['x']}
    grad_fn = _jax.value_and_grad(_loss, argnums=(0, 1))

    def one_microbatch(ex, loss_target):
        ex = dict(ex)
        diff = ex.pop(TWIN_DIFF_INPUT)
        return grad_fn(weights, diff, {**shared, **ex}, loss_target)

    if N_MICROBATCH == 1:
        loss, (grad_w, grad_x) = one_microbatch(per_example, given["loss_target"])
    else:
        def body(carry, xs):
            loss_sum, grad_sum = carry
            l_k, (gw_k, gx_k) = one_microbatch(xs[0], xs[1])
            with _jax.named_scope("update"):
                return (loss_sum + l_k, _jax.tree.map(_jnp.add, grad_sum, gw_k)), gx_k

        init = (_jnp.zeros((), _jnp.float32), _jax.tree.map(_jnp.zeros_like, weights))
        (loss, grad_w), grad_x = _jax.lax.scan(body, init, (per_example, given["loss_target"]))
    with _jax.named_scope("update"):
        delta_w, new_m, new_v = {}, {}, {}
        for n in TWIN_WEIGHTS:
            delta_w[n], new_m[n], new_v[n] = _adamw(weights[n], grad_w[n], given["m_" + n], given["v_" + n])
    return (loss, grad_x, *[grad_w[n] for n in TWIN_WEIGHTS], *[delta_w[n] for n in TWIN_WEIGHTS],
            *[new_m[n] for n in TWIN_WEIGHTS], *[new_v[n] for n in TWIN_WEIGHTS])
```

```python
import functools

import jax
import jax.numpy as jnp
from jax import lax
from jax.experimental import pallas as pl
from jax.experimental.pallas import tpu as pltpu

F32 = jnp.float32
BF16 = jnp.bfloat16

N_DEV = 8
D_MODEL = 1024
N_META = 16
ROW_BLOCK = 256
PAD_ROWS = ROW_BLOCK - N_META
CHUNK = 64
LRU_W = 512
LRU_HEADS = 8
LRU_HD = 64
LRU_C = 8.0
GLA_HEADS = 4
GLA_DK = 64
GLA_DV = 128
GLA_QK = GLA_HEADS * GLA_DK
GLA_W = GLA_HEADS * GLA_DV
GLA_RANK = 16
GATE_NORM = 16.0
D_FF = 4096
D_IN = 2592
Z_W = 2688
ZG_COL_BLOCK = 2560 // 128
EPS = 1e-6
LANES = 128

ADAM_LR = 0.001
ADAM_B1 = 0.9
ADAM_B2 = 0.999
ADAM_EPS = 1e-08
ADAM_WD = 0.01
ADAM_STEP = 10
ADAM_ROWS = 512

VMEM_SPEC = pl.BlockSpec(memory_space=pltpu.VMEM)
ANY_SPEC = pl.BlockSpec(memory_space=pl.ANY)
MESH_ID = pl.DeviceIdType.MESH


def _sds(shape, dtype):
    return jax.ShapeDtypeStruct(shape, dtype)


def _params(sem=None, vmem_mb=None):
    kw = {}
    if sem is not None:
        kw["dimension_semantics"] = sem
    if vmem_mb is not None:
        kw["vmem_limit_bytes"] = vmem_mb * 2 ** 20
    return pltpu.CompilerParams(**kw)


def _row_tile(n):
    for t in (768, 512, 256):
        if n % t == 0:
            return t
    raise ValueError(n)


def _col_tile(k):
    for t in (1024, 896, 768, 640, 512, 384, 256, 128):
        if k % t == 0:
            return t
    raise ValueError(k)


def _sigmoid(x):
    return 1.0 / (1.0 + jnp.exp(-x))


def _gelu_and_grad(x):
    c = 0.7978845608028654
    inner = c * (x + 0.044715 * x * x * x)
    t = jnp.tanh(inner)
    gelu = 0.5 * x * (1.0 + t)
    dgelu = 0.5 * (1.0 + t) + 0.5 * x * (1.0 - t * t) * c * (1.0 + 3.0 * 0.044715 * x * x)
    return gelu, dgelu


def _neg_expm1(y):
    series = -y * (1.0 + y * (0.5 + y * (1.0 / 6.0 + y * (1.0 / 24.0 + y * (1.0 / 120.0 + y * (1.0 / 720.0))))))
    return jnp.where(y > -0.25, series, 1.0 - jnp.exp(y))


def _rms_fwd(x, g):
    rs = lax.rsqrt(jnp.mean(x * x, axis=-1, keepdims=True) + EPS)
    return x * rs * g


def _rms_bwd(x, g, dy):
    rs = lax.rsqrt(jnp.mean(x * x, axis=-1, keepdims=True) + EPS)
    xh = x * rs
    dyg = dy * g
    dx = rs * (dyg - xh * jnp.mean(dyg * xh, axis=-1, keepdims=True))
    return dx, jnp.sum(dy * xh, axis=0, keepdims=True)


def _dot(a, b):
    return jnp.dot(a.astype(BF16), b.astype(BF16), preferred_element_type=F32)


def _dot_nt(a, b):
    return lax.dot_general(a.astype(BF16), b.astype(BF16), (((1,), (1,)), ((), ())), preferred_element_type=F32)


def _dot_tn(a, b):
    return lax.dot_general(a.astype(BF16), b.astype(BF16), (((0,), (0,)), ((), ())), preferred_element_type=F32)


def _row_ids(rows, block_index):
    return block_index * rows + lax.broadcasted_iota(jnp.int32, (rows, 1), 0)


def _accumulate(ref, value, first):
    @pl.when(first)
    def _():
        ref[...] = value

    @pl.when(jnp.logical_not(first))
    def _():
        ref[...] += value


def _norm_in_proj(h, g, w):
    n, d = h.shape
    zw = w.shape[1]
    tr = ROW_BLOCK

    def body(h_ref, g_ref, w_ref, hn_ref, z_ref):
        hn = _rms_fwd(h_ref[...], g_ref[...]).astype(BF16)
        hn_ref[...] = hn
        z_ref[...] = jnp.dot(hn, w_ref[...], preferred_element_type=F32)

    return pl.pallas_call(
        body, name="norm_in_proj", grid=(n // tr,),
        in_specs=[pl.BlockSpec((tr, d), lambda i: (i, 0)), pl.BlockSpec((1, d), lambda i: (0, 0)), VMEM_SPEC],
        out_specs=[pl.BlockSpec((tr, d), lambda i: (i, 0)), pl.BlockSpec((tr, zw), lambda i: (i, 0))],
        out_shape=[_sds((n, d), BF16), _sds((n, zw), F32)],
        compiler_params=_params(("parallel",), 48),
    )(h, g, w)


def _halo_specs(width, nb, col=0):
    per = ROW_BLOCK // 8
    prev = pl.BlockSpec((8, width), lambda i: (jnp.maximum(i * per - 1, 0), col))
    nxt = pl.BlockSpec((8, width), lambda i: (jnp.minimum((i + 1) * per, nb * per - 1), col))
    return prev, nxt


def _shift_down(x, prev8, d):
    n = x.shape[0]
    r = pltpu.roll(x, d, 0)
    p = pltpu.roll(prev8, d, 0)
    row8 = lax.broadcasted_iota(jnp.int32, (8, 1), 0)
    head = jnp.where(row8 < d, p, r[0:8])
    return jnp.concatenate([head, r[8:]], axis=0)


def _shift_up(x, next8, d):
    n = x.shape[0]
    r = pltpu.roll(x, n - d, 0)
    q = pltpu.roll(next8, 8 - d, 0)
    row8 = lax.broadcasted_iota(jnp.int32, (8, 1), 0)
    tail = jnp.where(row8 >= 8 - d, q, r[n - 8:])
    return jnp.concatenate([r[:n - 8], tail], axis=0)


def _conv_fwd(z, conv_w, conv_b):
    n = z.shape[0]
    nb = n // ROW_BLOCK
    prev_spec, next_spec = _halo_specs(LRU_W, nb)

    def body(cur_ref, prev_ref, next_ref, w_ref, b_ref, xc_ref):
        i = pl.program_id(0)
        cur = cur_ref[...]
        prev8 = prev_ref[...] * jnp.where(i > 0, 1.0, 0.0)
        next8 = next_ref[...] * jnp.where(i < nb - 1, 1.0, 0.0)
        w = [w_ref[pl.ds(k, 1), :] for k in range(4)]
        xc = (w[0] * _shift_down(cur, prev8, 2) + w[1] * _shift_down(cur, prev8, 1)
              + w[2] * cur + w[3] * _shift_up(cur, next8, 1) + b_ref[...])
        xc_ref[...] = xc

    return pl.pallas_call(
        body, name="conv_fwd", grid=(nb,),
        in_specs=[pl.BlockSpec((ROW_BLOCK, LRU_W), lambda i: (i, 0)), prev_spec, next_spec,
                  pl.BlockSpec((4, LRU_W), lambda i: (0, 0)), pl.BlockSpec((1, LRU_W), lambda i: (0, 0))],
        out_specs=pl.BlockSpec((ROW_BLOCK, LRU_W), lambda i: (i, 0)),
        out_shape=_sds((n, LRU_W), F32),
        compiler_params=_params(("parallel",)),
    )(z, z, z, conv_w, conv_b)


def _conv_bwd(dxc_f, dxc_b, z, conv_w):
    n = z.shape[0]
    nb = n // ROW_BLOCK
    prev_spec, next_spec = _halo_specs(LRU_W, nb)
    row_spec = pl.BlockSpec((ROW_BLOCK, LRU_W), lambda i: (i, 0))

    def body(df_ref, dfp_ref, dfn_ref, db_ref, dbp_ref, dbn_ref, x_ref, xp_ref, xn_ref, w_ref,
             dx_ref, dw_ref, dbias_ref):
        i = pl.program_id(0)
        has_prev = jnp.where(i > 0, 1.0, 0.0)
        has_next = jnp.where(i < nb - 1, 1.0, 0.0)
        dxc = df_ref[...] + db_ref[...]
        dprev = (dfp_ref[...] + dbp_ref[...]) * has_prev
        dnext = (dfn_ref[...] + dbn_ref[...]) * has_next
        x = x_ref[...]
        xprev = xp_ref[...] * has_prev
        xnext = xn_ref[...] * has_next
        w = [w_ref[pl.ds(k, 1), :] for k in range(4)]
        dx_ref[...] = (w[0] * _shift_up(dxc, dnext, 2) + w[1] * _shift_up(dxc, dnext, 1)
                       + w[2] * dxc + w[3] * _shift_down(dxc, dprev, 1))
        dw = jnp.concatenate([
            jnp.sum(dxc * _shift_down(x, xprev, 2), axis=0, keepdims=True),
            jnp.sum(dxc * _shift_down(x, xprev, 1), axis=0, keepdims=True),
            jnp.sum(dxc * x, axis=0, keepdims=True),
            jnp.sum(dxc * _shift_up(x, xnext, 1), axis=0, keepdims=True),
            jnp.zeros((4, LRU_W), F32)], axis=0)
        _accumulate(dw_ref, dw, i == 0)
        _accumulate(dbias_ref, jnp.sum(dxc, axis=0, keepdims=True), i == 0)

    dx, dw, dbias = pl.pallas_call(
        body, name="conv_bwd", grid=(nb,),
        in_specs=[row_spec, prev_spec, next_spec, row_spec, prev_spec, next_spec, row_spec, prev_spec, next_spec,
                  pl.BlockSpec((4, LRU_W), lambda i: (0, 0))],
        out_specs=[row_spec, pl.BlockSpec((8, LRU_W), lambda i: (0, 0)), pl.BlockSpec((1, LRU_W), lambda i: (0, 0))],
        out_shape=[_sds((n, LRU_W), F32), _sds((8, LRU_W), F32), _sds((1, LRU_W), F32)],
        compiler_params=_params(("arbitrary",)),
    )(dxc_f, dxc_f, dxc_f, dxc_b, dxc_b, dxc_b, z, z, z, conv_w)
    return dx, dw[:4], dbias


def _mix_epilogue(h_f, h_b, o_f, o_b, z, head_norm):
    n = z.shape[0]
    tr = ROW_BLOCK
    spec = pl.BlockSpec((tr, 512), lambda i: (i, 0))

    def body(hf_ref, hb_ref, of_ref, ob_ref, gate_ref, gout_ref, w_ref, y_ref):
        gelu, _ = _gelu_and_grad(gate_ref[...])
        y_ref[:, 0:LRU_W] = ((hf_ref[...] + hb_ref[...]) * gelu).astype(BF16)
        o = of_ref[...] + ob_ref[...]
        gout = gout_ref[...]
        silu = gout * _sigmoid(gout)
        w = w_ref[...]
        for hd in range(GLA_HEADS):
            cs = slice(hd * GLA_DV, (hd + 1) * GLA_DV)
            oh = o[:, cs]
            on = oh * lax.rsqrt(jnp.mean(oh * oh, axis=-1, keepdims=True) + EPS)
            y_ref[:, LRU_W + hd * GLA_DV:LRU_W + (hd + 1) * GLA_DV] = (on * w[:, cs] * silu[:, cs]).astype(BF16)

    return pl.pallas_call(
        body, name="mix_epilogue", grid=(n // tr,),
        in_specs=[spec, spec, spec, spec, pl.BlockSpec((tr, 512), lambda i: (i, 1)),
                  pl.BlockSpec((tr, 512), lambda i: (i, 4)), pl.BlockSpec((1, GLA_W), lambda i: (0, 0))],
        out_specs=pl.BlockSpec((tr, D_MODEL), lambda i: (i, 0)),
        out_shape=_sds((n, D_MODEL), BF16),
        compiler_params=_params(("parallel",)),
    )(h_f, h_b, o_f, o_b, z, z, head_norm)


def _mix_epilogue_bwd(dymix, h_f, h_b, o_f, o_b, z, head_norm):
    n = z.shape[0]
    tr = ROW_BLOCK
    spec = pl.BlockSpec((tr, 512), lambda i: (i, 0))

    def body(dyl_ref, dyg_ref, hf_ref, hb_ref, of_ref, ob_ref, gate_ref, gout_ref, w_ref,
             dhs_ref, dgate_ref, do_ref, dgout_ref, dw_ref):
        i = pl.program_id(0)
        dyl = dyl_ref[...]
        gelu, dgelu = _gelu_and_grad(gate_ref[...])
        dhs_ref[...] = dyl * gelu
        dgate_ref[...] = dyl * (hf_ref[...] + hb_ref[...]) * dgelu
        dyg = dyg_ref[...]
        o = of_ref[...] + ob_ref[...]
        gout = gout_ref[...]
        sg = _sigmoid(gout)
        silu = gout * sg
        dsilu = sg * (1.0 + gout * (1.0 - sg))
        w = w_ref[...]
        dws = []
        for hd in range(GLA_HEADS):
            cs = slice(hd * GLA_DV, (hd + 1) * GLA_DV)
            oh = o[:, cs]
            rs = lax.rsqrt(jnp.mean(oh * oh, axis=-1, keepdims=True) + EPS)
            on = oh * rs
            dy = dyg[:, cs]
            dgout_ref[:, cs] = dy * on * w[:, cs] * dsilu[:, cs]
            dys = dy * silu[:, cs]
            dws.append(jnp.sum(dys * on, axis=0, keepdims=True))
            don = dys * w[:, cs]
            do_ref[:, cs] = rs * (don - on * jnp.mean(don * on, axis=-1, keepdims=True))
        _accumulate(dw_ref, jnp.concatenate(dws, axis=1), i == 0)

    return pl.pallas_call(
        body, name="mix_epilogue_bwd", grid=(n // tr,),
        in_specs=[pl.BlockSpec((tr, 512), lambda i: (i, 0)), pl.BlockSpec((tr, 512), lambda i: (i, 1)),
                  spec, spec, spec, spec, pl.BlockSpec((tr, 512), lambda i: (i, 1)),
                  pl.BlockSpec((tr, 512), lambda i: (i, 4)), pl.BlockSpec((1, GLA_W), lambda i: (0, 0))],
        out_specs=[spec, spec, spec, spec, pl.BlockSpec((1, GLA_W), lambda i: (0, 0))],
        out_shape=[_sds((n, 512), F32)] * 4 + [_sds((1, GLA_W), F32)],
        compiler_params=_params(("arbitrary",)),
    )(dymix, dymix, h_f, h_b, o_f, o_b, z, z, head_norm)


def _out_proj(ymix, w_out, h, g):
    n, d = h.shape
    tr = ROW_BLOCK
    spec = pl.BlockSpec((tr, d), lambda i: (i, 0))

    def body(y_ref, w_ref, h_ref, g_ref, mix_ref, hmid_ref):
        mix = jnp.dot(y_ref[...], w_ref[...], preferred_element_type=F32)
        mix_ref[...] = mix
        hmid_ref[...] = h_ref[...] + _rms_fwd(mix, g_ref[...])

    return pl.pallas_call(
        body, name="out_proj", grid=(n // tr,),
        in_specs=[spec, VMEM_SPEC, spec, pl.BlockSpec((1, d), lambda i: (0, 0))],
        out_specs=[spec, spec],
        out_shape=[_sds((n, d), F32), _sds((n, d), F32)],
        compiler_params=_params(("parallel",), 32),
    )(ymix, w_out, h, g)


def _out_proj_bwd(dh_mid, mix, g, w_out):
    n, d = mix.shape
    tr = ROW_BLOCK
    spec = pl.BlockSpec((tr, d), lambda i: (i, 0))

    def body(dh_ref, mix_ref, g_ref, w_ref, dmix_ref, dy_ref, dg_ref):
        i = pl.program_id(0)
        dmix, dg = _rms_bwd(mix_ref[...], g_ref[...], dh_ref[...])
        dmix = dmix.astype(BF16)
        dmix_ref[...] = dmix
        dy_ref[...] = _dot_nt(dmix, w_ref[...])
        _accumulate(dg_ref, dg, i == 0)

    return pl.pallas_call(
        body, name="out_proj_bwd", grid=(n // tr,),
        in_specs=[spec, spec, pl.BlockSpec((1, d), lambda i: (0, 0)), VMEM_SPEC],
        out_specs=[spec, spec, pl.BlockSpec((1, d), lambda i: (0, 0))],
        out_shape=[_sds((n, d), BF16), _sds((n, d), F32), _sds((1, d), F32)],
        compiler_params=_params(("arbitrary",), 32),
    )(dh_mid, mix, g, w_out)


FF_CHUNK = 1024


def _mlp_fwd(h_mid, g_pre, w_up, w_down, g_post):
    n, d = h_mid.shape
    tr = ROW_BLOCK
    spec = pl.BlockSpec((tr, d), lambda i: (i, 0))

    def body(h_ref, gpre_ref, wup_ref, wdn_ref, gpost_ref, hn_ref, up_ref, ff_ref, hout_ref):
        h = h_ref[...]
        hn = _rms_fwd(h, gpre_ref[...]).astype(BF16)
        hn_ref[...] = hn
        ff = jnp.zeros((tr, d), F32)
        for j in range(D_FF // FF_CHUNK):
            cs = slice(j * FF_CHUNK, (j + 1) * FF_CHUNK)
            up = jnp.dot(hn, wup_ref[:, cs], preferred_element_type=F32)
            up_ref[:, cs] = up
            act = jnp.square(jnp.maximum(up, 0.0)).astype(BF16)
            ff = ff + jnp.dot(act, wdn_ref[cs, :], preferred_element_type=F32)
        ff_ref[...] = ff
        hout_ref[...] = h + _rms_fwd(ff, gpost_ref[...])

    return pl.pallas_call(
        body, name="mlp_fwd", grid=(n // tr,),
        in_specs=[spec, pl.BlockSpec((1, d), lambda i: (0, 0)), VMEM_SPEC, VMEM_SPEC, pl.BlockSpec((1, d), lambda i: (0, 0))],
        out_specs=[spec, pl.BlockSpec((tr, D_FF), lambda i: (i, 0)), spec, spec],
        out_shape=[_sds((n, d), BF16), _sds((n, D_FF), F32), _sds((n, d), F32), _sds((n, d), F32)],
        compiler_params=_params(("parallel",), 52),
    )(h_mid, g_pre, w_up, w_down, g_post)


def _mlp_bwd(dh, ff, up, h_mid, g_pre, w_up, w_down, g_post):
    n, d = h_mid.shape
    tr = ROW_BLOCK
    spec = pl.BlockSpec((tr, d), lambda i: (i, 0))
    wide = pl.BlockSpec((tr, D_FF), lambda i: (i, 0))
    gspec = pl.BlockSpec((1, d), lambda i: (0, 0))

    def body(dh_ref, ff_ref, up_ref, h_ref, gpre_ref, wup_ref, wdn_ref, gpost_ref,
             dff_ref, dup_ref, act_ref, dhmid_ref, dgpost_ref, dgpre_ref):
        i = pl.program_id(0)
        dh = dh_ref[...]
        dff, dgpost = _rms_bwd(ff_ref[...], gpost_ref[...], dh)
        dff = dff.astype(BF16)
        dff_ref[...] = dff
        dhn = jnp.zeros((tr, d), F32)
        for j in range(D_FF // FF_CHUNK):
            cs = slice(j * FF_CHUNK, (j + 1) * FF_CHUNK)
            relu = jnp.maximum(up_ref[:, cs], 0.0)
            act_ref[:, cs] = jnp.square(relu).astype(BF16)
            dact = _dot_nt(dff, wdn_ref[cs, :])
            dup = (dact * 2.0 * relu).astype(BF16)
            dup_ref[:, cs] = dup
            dhn = dhn + _dot_nt(dup, wup_ref[:, cs])
        dx, dgpre = _rms_bwd(h_ref[...], gpre_ref[...], dhn)
        dhmid_ref[...] = dh + dx
        _accumulate(dgpost_ref, dgpost, i == 0)
        _accumulate(dgpre_ref, dgpre, i == 0)

    return pl.pallas_call(
        body, name="mlp_bwd", grid=(n // tr,),
        in_specs=[spec, spec, wide, spec, gspec, VMEM_SPEC, VMEM_SPEC, gspec],
        out_specs=[spec, wide, wide, spec, gspec, gspec],
        out_shape=[_sds((n, d), BF16), _sds((n, D_FF), BF16), _sds((n, D_FF), BF16), _sds((n, d), F32),
                   _sds((1, d), F32), _sds((1, d), F32)],
        compiler_params=_params(("arbitrary",), 56),
    )(dh, ff, up, h_mid, g_pre, w_up, w_down, g_post)


def _in_proj_bwd(pieces, w_in, h, g, dh_mid):
    dxbr, dgate, dqk_f, dqk_b, dv_f, dv_b, dgout, dzg_f, dzg_b = pieces
    n, d = h.shape
    tr = ROW_BLOCK
    spec = pl.BlockSpec((tr, d), lambda i: (i, 0))
    s512 = pl.BlockSpec((tr, 512), lambda i: (i, 0))
    s128 = pl.BlockSpec((tr, LANES), lambda i: (i, 0))

    def body(a_ref, b_ref, cf_ref, cb_ref, df_ref, db_ref, e_ref, ff_ref, fb_ref, w_ref, h_ref, g_ref, dhm_ref,
             dz_ref, dh_ref, dg_ref):
        i = pl.program_id(0)
        real = (_row_ids(tr, i) >= PAD_ROWS).astype(F32)
        dz = jnp.concatenate([a_ref[...], b_ref[...], cf_ref[...] + cb_ref[...], df_ref[...] + db_ref[...],
                              e_ref[...], ff_ref[...] + fb_ref[...]], axis=1) * real
        dz = dz.astype(BF16)
        dz_ref[...] = dz
        dhn = _dot_nt(dz, w_ref[...])
        dx, dg = _rms_bwd(h_ref[...], g_ref[...], dhn)
        dh_ref[...] = (dhm_ref[...] + dx) * real
        _accumulate(dg_ref, dg, i == 0)

    return pl.pallas_call(
        body, name="in_proj_bwd", grid=(n // tr,),
        in_specs=[s512, s512, s512, s512, s512, s512, s512, s128, s128, VMEM_SPEC, spec,
                  pl.BlockSpec((1, d), lambda i: (0, 0)), spec],
        out_specs=[pl.BlockSpec((tr, Z_W), lambda i: (i, 0)), spec, pl.BlockSpec((1, d), lambda i: (0, 0))],
        out_shape=[_sds((n, Z_W), BF16), _sds((n, d), F32), _sds((1, d), F32)],
        compiler_params=_params(("arbitrary",), 48),
    )(dxbr, dgate, dqk_f, dqk_b, dv_f, dv_b, dgout, dzg_f, dzg_b, w_in, h, g, dh_mid)


def _matmul_tn(a, b, name):
    n, m = a.shape
    k = b.shape[1]
    tr, tm, tk = _row_tile(n), _col_tile(m), _col_tile(k)

    def body(a_ref, b_ref, o_ref):
        _accumulate(o_ref, _dot_tn(a_ref[...], b_ref[...]), pl.program_id(2) == 0)

    return pl.pallas_call(
        body, name=name, grid=(m // tm, k // tk, n // tr),
        in_specs=[pl.BlockSpec((tr, tm), lambda mi, ki, r: (r, mi)), pl.BlockSpec((tr, tk), lambda mi, ki, r: (r, ki))],
        out_specs=pl.BlockSpec((tm, tk), lambda mi, ki, r: (mi, ki)),
        out_shape=_sds((m, k), F32),
        compiler_params=_params(("parallel", "parallel", "arbitrary"), 40),
    )(a, b)


def _loss_and_grad(h_out, target):
    n, d = h_out.shape
    tr = ROW_BLOCK
    first = (PAD_ROWS + N_META) // tr

    def body(h_ref, t_ref, dh_ref, loss_ref):
        i = pl.program_id(0)
        real = jnp.where(i >= first, 1.0, 0.0)
        diff = (h_ref[...] - t_ref[...]) * real
        dh_ref[...] = diff * (1.0 / d)
        part = 0.5 * jnp.sum(jnp.mean(diff * diff, axis=-1, keepdims=True), axis=0, keepdims=True)
        _accumulate(loss_ref, jnp.broadcast_to(part, (1, LANES)), i == 0)

    return pl.pallas_call(
        body, name="loss_and_grad", grid=(n // tr,),
        in_specs=[pl.BlockSpec((tr, d), lambda i: (i, 0)), pl.BlockSpec((tr, d), lambda i: (jnp.maximum(i - first, 0), 0))],
        out_specs=[pl.BlockSpec((tr, d), lambda i: (i, 0)), pl.BlockSpec((1, LANES), lambda i: (0, 0))],
        out_shape=[_sds((n, d), F32), _sds((1, LANES), F32)],
        compiler_params=_params(("arbitrary",)),
    )(h_out, target)


def _scan_rows(a, u, reverse):
    n = a.shape[0]
    row = lax.broadcasted_iota(jnp.int32, (n, 1), 0)
    d = 1
    while d < n:
        shift = n - d if reverse else d
        keep = (row < n - d) if reverse else (row >= d)
        a_s = pltpu.roll(a, shift, 0)
        u_s = pltpu.roll(u, shift, 0)
        u = jnp.where(keep, a * u_s + u, u)
        a = jnp.where(keep, a * a_s, a)
        d *= 2
    return a, u


def _lru_gates(xc, wcat_ref, bias_ref, lam_ref):
    nl = -lam_ref[...]
    nsp = -LRU_C * (jnp.maximum(nl, 0.0) + jnp.log(1.0 + jnp.exp(-jnp.abs(nl))))
    pre = _dot(xc, wcat_ref[...]) + bias_ref[...]
    r = _sigmoid(pre[:, :LRU_W])
    ig = _sigmoid(pre[:, LRU_W:])
    log_a = r * nsp
    a = jnp.exp(log_a)
    m = jnp.sqrt(_neg_expm1(2.0 * log_a))
    return r, ig, a, m, nsp


def _lru_scan(xc, wcat, bias, lam, reverse):
    n = xc.shape[0]
    nb = n // ROW_BLOCK
    order = (lambda i: nb - 1 - i) if reverse else (lambda i: i)
    spec = pl.BlockSpec((ROW_BLOCK, LRU_W), lambda i: (order(i), 0))
    edge = 0 if reverse else ROW_BLOCK - 1

    def body(xc_ref, wcat_ref, bias_ref, lam_ref, h_ref, carry_ref):
        i = pl.program_id(0)

        @pl.when(i == 0)
        def _():
            carry_ref[...] = jnp.zeros_like(carry_ref)

        xc = xc_ref[...]
        r, ig, a, m, _ = _lru_gates(xc, wcat_ref, bias_ref, lam_ref)
        u = jnp.where(_row_ids(ROW_BLOCK, order(i)) >= PAD_ROWS, m * (ig * xc), 0.0)
        big_a, big_u = _scan_rows(a, u, reverse)
        h_ref[...] = big_a * carry_ref[0:1, :] + big_u
        carry_ref[0:1, :] = h_ref[pl.ds(edge, 1), :]

    return pl.pallas_call(
        body, name="lru_scan_b" if reverse else "lru_scan_f", grid=(nb,),
        in_specs=[spec, VMEM_SPEC, pl.BlockSpec((1, 2 * LRU_W), lambda i: (0, 0)), pl.BlockSpec((1, LRU_W), lambda i: (0, 0))],
        out_specs=spec,
        out_shape=_sds((n, LRU_W), F32),
        scratch_shapes=[pltpu.VMEM((8, LRU_W), F32)],
        compiler_params=_params(("arbitrary",)),
    )(xc, wcat, bias, lam)


def _lru_scan_bwd(dhs, xc, h, wcat, bias, lam, reverse):
    n = xc.shape[0]
    nb = n // ROW_BLOCK
    per = ROW_BLOCK // 8
    order = (lambda i: i) if reverse else (lambda i: nb - 1 - i)
    spec = pl.BlockSpec((ROW_BLOCK, LRU_W), lambda i: (order(i), 0))
    if reverse:
        halo = pl.BlockSpec((8, LRU_W), lambda i: (jnp.minimum((order(i) + 1) * per, nb * per - 1), 0))
    else:
        halo = pl.BlockSpec((8, LRU_W), lambda i: (jnp.maximum(order(i) * per - 1, 0), 0))
    edge = ROW_BLOCK - 1 if reverse else 0

    def body(dhs_ref, xc_ref, h_ref, halo_ref, wcat_ref, bias_ref, lam_ref,
             dxc_ref, dw_ref, db_ref, dlam_ref, cdh_ref, ca_ref, tmp_ref):
        i = pl.program_id(0)
        ib = order(i)

        @pl.when(i == 0)
        def _():
            cdh_ref[...] = jnp.zeros_like(cdh_ref)
            ca_ref[...] = jnp.zeros_like(ca_ref)

        xc = xc_ref[...]
        r, ig, a, m, nsp = _lru_gates(xc, wcat_ref, bias_ref, lam_ref)
        row = lax.broadcasted_iota(jnp.int32, (ROW_BLOCK, 1), 0)
        if reverse:
            coef = jnp.where(row == 0, ca_ref[0:1, :], pltpu.roll(a, 1, 0))
            h_nb = jnp.where(row == ROW_BLOCK - 1, halo_ref[0:1, :] * jnp.where(ib < nb - 1, 1.0, 0.0),
                             pltpu.roll(h_ref[...], ROW_BLOCK - 1, 0))
        else:
            coef = jnp.where(row == ROW_BLOCK - 1, ca_ref[0:1, :], pltpu.roll(a, ROW_BLOCK - 1, 0))
            h_nb = jnp.where(row == 0, halo_ref[7:8, :] * jnp.where(ib > 0, 1.0, 0.0), pltpu.roll(h_ref[...], 1, 0))
        big_c, big_v = _scan_rows(coef, dhs_ref[...], not reverse)
        dh = big_c * cdh_ref[0:1, :] + big_v
        tmp_ref[...] = dh
        cdh_ref[0:1, :] = tmp_ref[pl.ds(edge, 1), :]
        tmp_ref[...] = a
        ca_ref[0:1, :] = tmp_ref[pl.ds(edge, 1), :]

        du = jnp.where(_row_ids(ROW_BLOCK, ib) >= PAD_ROWS, dh, 0.0)
        da = dh * h_nb
        dm = du * (ig * xc)
        di = du * (m * xc)
        dlog_a = da * a - dm * (a * a) / m
        dr = dlog_a * nsp
        dpre = jnp.concatenate([dr * r * (1.0 - r), di * ig * (1.0 - ig)], axis=1)
        dxc_ref[...] = du * (m * ig) + _dot_nt(dpre, wcat_ref[...])
        _accumulate(dw_ref, _dot_tn(xc, dpre), i == 0)
        _accumulate(db_ref, jnp.sum(dpre, axis=0, keepdims=True), i == 0)
        _accumulate(dlam_ref, jnp.sum(dlog_a * r, axis=0, keepdims=True), i == 0)

        @pl.when(i == nb - 1)
        def _():
            dlam_ref[...] = dlam_ref[...] * (LRU_C * _sigmoid(-lam_ref[...]))

    return pl.pallas_call(
        body, name="lru_scan_bwd_b" if reverse else "lru_scan_bwd_f", grid=(nb,),
        in_specs=[spec, spec, spec, halo, VMEM_SPEC, pl.BlockSpec((1, 2 * LRU_W), lambda i: (0, 0)),
                  pl.BlockSpec((1, LRU_W), lambda i: (0, 0))],
        out_specs=[spec, pl.BlockSpec((LRU_W, 2 * LRU_W), lambda i: (0, 0)),
                   pl.BlockSpec((1, 2 * LRU_W), lambda i: (0, 0)), pl.BlockSpec((1, LRU_W), lambda i: (0, 0))],
        out_shape=[_sds((n, LRU_W), F32), _sds((LRU_W, 2 * LRU_W), F32), _sds((1, 2 * LRU_W), F32), _sds((1, LRU_W), F32)],
        scratch_shapes=[pltpu.VMEM((8, LRU_W), F32), pltpu.VMEM((8, LRU_W), F32), pltpu.VMEM((ROW_BLOCK, LRU_W), F32)],
        compiler_params=_params(("arbitrary",)),
    )(dhs, xc, h, h, wcat, bias, lam)


def _gla_masks(reverse):
    t = lax.broadcasted_iota(jnp.int32, (CHUNK, CHUNK), 0)
    s = lax.broadcasted_iota(jnp.int32, (CHUNK, CHUNK), 1)
    if reverse:
        return (s >= t).astype(F32), s > t
    return (s <= t).astype(F32), s <= t


def _gla_gate(zg, wg_ref, bg_ref):
    pre = _dot(zg, wg_ref[...]) + bg_ref[...]
    g = (jnp.minimum(pre, 0.0) - jnp.log(1.0 + jnp.exp(-jnp.abs(pre)))) * (1.0 / GATE_NORM)
    return pre, g


def _gla_decays(gc, tri):
    b = jnp.dot(tri, gc, precision=lax.Precision.HIGHEST, preferred_element_type=F32)
    b_last = jnp.sum(gc, axis=0, keepdims=True)
    return jnp.exp(b), jnp.exp(-b), jnp.exp(b_last - b), jnp.exp(b_last)


def _gla_scan(z, wg, bg, reverse):
    n = z.shape[0]
    nb = n // ROW_BLOCK
    cpb = ROW_BLOCK // CHUNK
    order = (lambda i: nb - 1 - i) if reverse else (lambda i: i)
    chunks = range(cpb - 1, -1, -1) if reverse else range(cpb)

    def body(qk_ref, v_ref, zg_ref, wg_ref, bg_ref, o_ref, sall_ref, s_ref):
        i = pl.program_id(0)

        @pl.when(i == 0)
        def _():
            s_ref[...] = jnp.zeros_like(s_ref)

        tri, mask = _gla_masks(reverse)
        _, g = _gla_gate(zg_ref[...], wg_ref, bg_ref)
        for c in chunks:
            rows = slice(c * CHUNK, (c + 1) * CHUNK)
            eb, enb, ebl, el = _gla_decays(g[rows], tri)
            qk = qk_ref[rows, :]
            qh = qk[:, :GLA_QK] * (GLA_DK ** -0.5) * eb
            kh = qk[:, GLA_QK:] * enb
            kb = qk[:, GLA_QK:] * ebl
            v = v_ref[rows, :]
            state = s_ref[...]
            sall_ref[c] = state
            outs, new_state = [], []
            for hd in range(GLA_HEADS):
                ks = slice(hd * GLA_DK, (hd + 1) * GLA_DK)
                vs = slice(hd * GLA_DV, (hd + 1) * GLA_DV)
                p = jnp.where(mask, _dot_nt(qh[:, ks], kh[:, ks]), 0.0)
                outs.append(_dot(p, v[:, vs]) + _dot_nt(qh[:, ks], state[:, ks]))
                new_state.append(state[:, ks] * el[:, ks] + _dot_tn(v[:, vs], kb[:, ks]))
            o_ref[rows, :] = jnp.concatenate(outs, axis=1)
            s_ref[...] = jnp.concatenate(new_state, axis=1)

    return pl.pallas_call(
        body, name="gla_scan_b" if reverse else "gla_scan_f", grid=(nb,),
        in_specs=[pl.BlockSpec((ROW_BLOCK, 512), lambda i: (order(i), 2)), pl.BlockSpec((ROW_BLOCK, 512), lambda i: (order(i), 3)),
                  pl.BlockSpec((ROW_BLOCK, LANES), lambda i: (order(i), ZG_COL_BLOCK)), VMEM_SPEC,
                  pl.BlockSpec((1, GLA_QK), lambda i: (0, 0))],
        out_specs=[pl.BlockSpec((ROW_BLOCK, GLA_W), lambda i: (order(i), 0)),
                   pl.BlockSpec((cpb, GLA_DV, GLA_QK), lambda i: (order(i), 0, 0))],
        out_shape=[_sds((n, GLA_W), F32), _sds((n // CHUNK, GLA_DV, GLA_QK), F32)],
        scratch_shapes=[pltpu.VMEM((GLA_DV, GLA_QK), F32)],
        compiler_params=_params(("arbitrary",)),
    )(z, z, z, wg, bg)


def _gla_scan_bwd(do, z, states, wg, bg, reverse):
    n = z.shape[0]
    nb = n // ROW_BLOCK
    cpb = ROW_BLOCK // CHUNK
    order = (lambda i: i) if reverse else (lambda i: nb - 1 - i)
    chunks = range(cpb) if reverse else range(cpb - 1, -1, -1)

    def body(do_ref, qk_ref, v_ref, zg_ref, sall_ref, wg_ref, bg_ref,
             dqk_ref, dv_ref, dzg_ref, dwg_ref, dbg_ref, ds_ref):
        i = pl.program_id(0)

        @pl.when(i == 0)
        def _():
            ds_ref[...] = jnp.zeros_like(ds_ref)

        tri, mask = _gla_masks(reverse)
        tri_t, _ = _gla_masks(not reverse)
        zg = zg_ref[...]
        pre, g = _gla_gate(zg, wg_ref, bg_ref)
        dgs = [None] * cpb
        for c in chunks:
            rows = slice(c * CHUNK, (c + 1) * CHUNK)
            eb, enb, ebl, el = _gla_decays(g[rows], tri)
            qk = qk_ref[rows, :]
            qh = qk[:, :GLA_QK] * (GLA_DK ** -0.5) * eb
            kh = qk[:, GLA_QK:] * enb
            kb = qk[:, GLA_QK:] * ebl
            v = v_ref[rows, :]
            dout = do_ref[rows, :]
            state = sall_ref[c]
            dstate = ds_ref[...]
            dqh, dkh, dkb, dvs, new_ds = [], [], [], [], []
            for hd in range(GLA_HEADS):
                ks = slice(hd * GLA_DK, (hd + 1) * GLA_DK)
                vs = slice(hd * GLA_DV, (hd + 1) * GLA_DV)
                p = jnp.where(mask, _dot_nt(qh[:, ks], kh[:, ks]), 0.0)
                dp = jnp.where(mask, _dot_nt(dout[:, vs], v[:, vs]), 0.0)
                dvs.append(_dot_tn(p, dout[:, vs]) + _dot_nt(kb[:, ks], dstate[:, ks]))
                dqh.append(_dot(dp, kh[:, ks]) + _dot(dout[:, vs], state[:, ks]))
                dkh.append(_dot_tn(dp, qh[:, ks]))
                dkb.append(_dot(v[:, vs], dstate[:, ks]))
                new_ds.append(dstate[:, ks] * el[:, ks] + _dot_tn(dout[:, vs], qh[:, ks]))
            dqh = jnp.concatenate(dqh, axis=1)
            dkh = jnp.concatenate(dkh, axis=1)
            dkb = jnp.concatenate(dkb, axis=1)
            dv_ref[rows, :] = jnp.concatenate(dvs, axis=1)
            ds_ref[...] = jnp.concatenate(new_ds, axis=1)
            dqk_ref[rows, :] = jnp.concatenate([dqh * eb * (GLA_DK ** -0.5), dkh * enb + dkb * ebl], axis=1)
            dkb_kb = dkb * kb
            db = dqh * qh - dkh * kh - dkb_kb
            db_last = el * jnp.sum(state * dstate, axis=0, keepdims=True) + jnp.sum(dkb_kb, axis=0, keepdims=True)
            dgs[c] = jnp.dot(tri_t, db, precision=lax.Precision.HIGHEST, preferred_element_type=F32) + db_last
        dg = jnp.concatenate(dgs, axis=0)
        dpre = dg * _sigmoid(-pre) * (1.0 / GATE_NORM)
        dzg_ref[...] = _dot_nt(dpre, wg_ref[...])
        _accumulate(dwg_ref, _dot_tn(zg, dpre), i == 0)
        _accumulate(dbg_ref, jnp.sum(dpre, axis=0, keepdims=True), i == 0)

    return pl.pallas_call(
        body, name="gla_scan_bwd_b" if reverse else "gla_scan_bwd_f", grid=(nb,),
        in_specs=[pl.BlockSpec((ROW_BLOCK, GLA_W), lambda i: (order(i), 0)),
                  pl.BlockSpec((ROW_BLOCK, 512), lambda i: (order(i), 2)), pl.BlockSpec((ROW_BLOCK, 512), lambda i: (order(i), 3)),
                  pl.BlockSpec((ROW_BLOCK, LANES), lambda i: (order(i), ZG_COL_BLOCK)),
                  pl.BlockSpec((cpb, GLA_DV, GLA_QK), lambda i: (order(i), 0, 0)), VMEM_SPEC,
                  pl.BlockSpec((1, GLA_QK), lambda i: (0, 0))],
        out_specs=[pl.BlockSpec((ROW_BLOCK, 512), lambda i: (order(i), 0)), pl.BlockSpec((ROW_BLOCK, 512), lambda i: (order(i), 0)),
                   pl.BlockSpec((ROW_BLOCK, LANES), lambda i: (order(i), 0)),
                   pl.BlockSpec((LANES, GLA_QK), lambda i: (0, 0)), pl.BlockSpec((1, GLA_QK), lambda i: (0, 0))],
        out_shape=[_sds((n, 512), F32), _sds((n, 512), F32), _sds((n, LANES), F32), _sds((LANES, GLA_QK), F32),
                   _sds((1, GLA_QK), F32)],
        scratch_shapes=[pltpu.VMEM((GLA_DV, GLA_QK), F32)],
        compiler_params=_params(("arbitrary",)),
    )(do, z, z, z, states, wg, bg)


def _block_diag(w):
    eye = jnp.eye(LRU_HEADS, dtype=w.dtype)
    return jnp.einsum("hij,hg->higj", w, eye).reshape(LRU_W, LRU_W)


def _diag_blocks(w):
    w4 = w.reshape(LRU_HEADS, LRU_HD, LRU_HEADS, LRU_HD)
    return jnp.stack([w4[h, :, h, :] for h in range(LRU_HEADS)])


def _layer_params(p):
    q = dict(p)
    q["w_in_p"] = jnp.pad(p["w_in"], ((0, 0), (0, Z_W - D_IN)))
    for d in "fb":
        q["wcat_" + d] = jnp.concatenate([_block_diag(p["lru_wa_" + d]), _block_diag(p["lru_wx_" + d])], axis=1).astype(BF16)
        q["bias_" + d] = jnp.concatenate([p["lru_ba_" + d], p["lru_bx_" + d]])[None]
        q["lam_" + d] = p["lru_lambda_" + d][None]
        lo = 0 if d == "f" else GLA_RANK
        q["wg_" + d] = jnp.zeros((LANES, GLA_QK), F32).at[lo:lo + GLA_RANK].set(p["gla_wg_" + d]).astype(BF16)
        q["bg_" + d] = p["gla_bg_" + d][None]
    return q


def _layer_fwd(h, q):
    hn, z = _norm_in_proj(h, q["norm_mix_pre"][None], q["w_in_p"])
    xc = _conv_fwd(z, q["conv_w"], q["conv_b"][None])
    h_f = _lru_scan(xc, q["wcat_f"], q["bias_f"], q["lam_f"], False)
    h_b = _lru_scan(xc, q["wcat_b"], q["bias_b"], q["lam_b"], True)
    o_f, s_f = _gla_scan(z, q["wg_f"], q["bg_f"], False)
    o_b, s_b = _gla_scan(z, q["wg_b"], q["bg_b"], True)
    ymix = _mix_epilogue(h_f, h_b, o_f, o_b, z, q["gla_head_norm"][None])
    mix, h_mid = _out_proj(ymix, q["w_out"], h, q["norm_mix_post"][None])
    hn2, up, ff, h_out = _mlp_fwd(h_mid, q["norm_mlp_pre"][None], q["w_mlp_up"], q["w_mlp_down"], q["norm_mlp_post"][None])
    saved = dict(h=h, hn=hn, z=z, xc=xc, h_f=h_f, h_b=h_b, o_f=o_f, o_b=o_b, s_f=s_f, s_b=s_b, ymix=ymix, mix=mix,
                 h_mid=h_mid, hn2=hn2, up=up, ff=ff)
    return h_out, saved


def _layer_bwd(dh_out, q, s):
    g = {}
    d_ff, dup, act, dh_mid, g["norm_mlp_post"], g["norm_mlp_pre"] = _mlp_bwd(
        dh_out, s["ff"], s["up"], s["h_mid"], q["norm_mlp_pre"][None], q["w_mlp_up"], q["w_mlp_down"], q["norm_mlp_post"][None])
    g["w_mlp_down"] = _matmul_tn(act, d_ff, "grad_w_down")
    g["w_mlp_up"] = _matmul_tn(s["hn2"], dup, "grad_w_up")
    dmix, dymix, g["norm_mix_post"] = _out_proj_bwd(dh_mid, s["mix"], q["norm_mix_post"][None], q["w_out"])
    g["w_out"] = _matmul_tn(s["ymix"], dmix, "grad_w_out")
    dhs, dgate, do, dgout, g["gla_head_norm"] = _mix_epilogue_bwd(
        dymix, s["h_f"], s["h_b"], s["o_f"], s["o_b"], s["z"], q["gla_head_norm"][None])
    dqk, dv, dzg = {}, {}, {}
    dxc = {}
    for d, rev in (("f", False), ("b", True)):
        dqk[d], dv[d], dzg[d], dwg, dbg = _gla_scan_bwd(do, s["z"], s["s_" + d], q["wg_" + d], q["bg_" + d], rev)
        lo = 0 if d == "f" else GLA_RANK
        g["gla_wg_" + d] = dwg[lo:lo + GLA_RANK]
        g["gla_bg_" + d] = dbg
        dxc[d], dwcat, dbias, g["lru_lambda_" + d] = _lru_scan_bwd(
            dhs, s["xc"], s["h_" + d], q["wcat_" + d], q["bias_" + d], q["lam_" + d], rev)
        g["lru_wa_" + d] = _diag_blocks(dwcat[:, :LRU_W])
        g["lru_wx_" + d] = _diag_blocks(dwcat[:, LRU_W:])
        g["lru_ba_" + d] = dbias[:, :LRU_W]
        g["lru_bx_" + d] = dbias[:, LRU_W:]
    dxbr, g["conv_w"], g["conv_b"] = _conv_bwd(dxc["f"], dxc["b"], s["z"], q["conv_w"])
    dz, dh_in, g["norm_mix_pre"] = _in_proj_bwd(
        (dxbr, dgate, dqk["f"], dqk["b"], dv["f"], dv["b"], dgout, dzg["f"], dzg["b"]),
        q["w_in_p"], s["h"], q["norm_mix_pre"][None], dh_mid)
    g["w_in"] = _matmul_tn(s["hn"], dz, "grad_w_in")[:, :D_IN]
    return dh_in, g


def _my_index():
    return 4 * lax.axis_index("x") + 2 * lax.axis_index("y") + lax.axis_index("c")


def _peer(k):
    x, y, c = lax.axis_index("x"), lax.axis_index("y"), lax.axis_index("c")
    px = x ^ ((k >> 2) & 1)
    py = y ^ ((k >> 1) & 1)
    pc = c ^ (k & 1)
    return (px, py, pc), 4 * px + 2 * py + pc


def _all_gather(arrays):
    count = len(arrays)

    def body(*refs):
        ins, outs = refs[:count], refs[count:2 * count]
        send_sems, recv_sems, local_sems = refs[2 * count:]
        me = _my_index()
        local = [pltpu.make_async_copy(ins[a], outs[a].at[me], local_sems.at[a]) for a in range(count)]
        for cp in local:
            cp.start()
        sends = []
        for k in range(1, N_DEV):
            peer, _ = _peer(k)
            for a in range(count):
                sem = a * (N_DEV - 1) + k - 1
                sends.append(pltpu.make_async_remote_copy(
                    src_ref=ins[a], dst_ref=outs[a].at[me], send_sem=send_sems.at[sem], recv_sem=recv_sems.at[sem],
                    device_id=peer, device_id_type=MESH_ID))
        for cp in sends:
            cp.start()
        for k in range(1, N_DEV):
            peer, peer_index = _peer(k)
            for a in range(count):
                sem = a * (N_DEV - 1) + k - 1
                pltpu.make_async_remote_copy(
                    src_ref=ins[a], dst_ref=outs[a].at[peer_index], send_sem=send_sems.at[sem], recv_sem=recv_sems.at[sem],
                    device_id=peer, device_id_type=MESH_ID).wait_recv()
        for cp in sends:
            cp.wait_send()
        for cp in local:
            cp.wait()

    nsem = count * (N_DEV - 1)
    return pl.pallas_call(
        body, name="all_gather",
        in_specs=[ANY_SPEC] * count, out_specs=[ANY_SPEC] * count,
        out_shape=[_sds((N_DEV,) + a.shape, a.dtype) for a in arrays],
        scratch_shapes=[pltpu.SemaphoreType.DMA((nsem,)), pltpu.SemaphoreType.DMA((nsem,)), pltpu.SemaphoreType.DMA((count,))],
        compiler_params=pltpu.CompilerParams(has_side_effects=True),
    )(*arrays)


def _exchange_grads(slabs, replicated):
    def body(slab_ref, rep_ref, slab_out, rep_out, send_sems, recv_sems, local_sems):
        me = _my_index()
        local = [pltpu.make_async_copy(slab_ref.at[me], slab_out.at[me], local_sems.at[0]),
                 pltpu.make_async_copy(rep_ref, rep_out.at[me], local_sems.at[1])]
        for cp in local:
            cp.start()
        sends = []
        for k in range(1, N_DEV):
            peer, peer_index = _peer(k)
            sends.append(pltpu.make_async_remote_copy(
                src_ref=slab_ref.at[peer_index], dst_ref=slab_out.at[me], send_sem=send_sems.at[k - 1],
                recv_sem=recv_sems.at[k - 1], device_id=peer, device_id_type=MESH_ID))
            sends.append(pltpu.make_async_remote_copy(
                src_ref=rep_ref, dst_ref=rep_out.at[me], send_sem=send_sems.at[N_DEV - 1 + k - 1],
                recv_sem=recv_sems.at[N_DEV - 1 + k - 1], device_id=peer, device_id_type=MESH_ID))
        for cp in sends:
            cp.start()
        for k in range(1, N_DEV):
            peer, peer_index = _peer(k)
            pltpu.make_async_remote_copy(
                src_ref=slab_ref.at[peer_index], dst_ref=slab_out.at[peer_index], send_sem=send_sems.at[k - 1],
                recv_sem=recv_sems.at[k - 1], device_id=peer, device_id_type=MESH_ID).wait_recv()
            pltpu.make_async_remote_copy(
                src_ref=rep_ref, dst_ref=rep_out.at[peer_index], send_sem=send_sems.at[N_DEV - 1 + k - 1],
                recv_sem=recv_sems.at[N_DEV - 1 + k - 1], device_id=peer, device_id_type=MESH_ID).wait_recv()
        for cp in sends:
            cp.wait_send()
        for cp in local:
            cp.wait()

    nsem = 2 * (N_DEV - 1)
    return pl.pallas_call(
        body, name="exchange_grads",
        in_specs=[ANY_SPEC, ANY_SPEC], out_specs=[ANY_SPEC, ANY_SPEC],
        out_shape=[_sds(slabs.shape, slabs.dtype), _sds((N_DEV,) + replicated.shape, replicated.dtype)],
        scratch_shapes=[pltpu.SemaphoreType.DMA((nsem,)), pltpu.SemaphoreType.DMA((nsem,)), pltpu.SemaphoreType.DMA((2,))],
        compiler_params=pltpu.CompilerParams(has_side_effects=True),
    )(slabs, replicated)


def _sum_and_adamw(parts, w, m, v, name):
    rows = w.shape[0]
    tr = ADAM_ROWS
    assert rows % tr == 0
    spec = pl.BlockSpec((tr, LANES), lambda i: (i, 0))

    def body(p_ref, w_ref, m_ref, v_ref, g_ref, d_ref, nm_ref, nv_ref):
        g = p_ref[0]
        for j in range(1, N_DEV):
            g = g + p_ref[j]
        g_ref[...] = g
        nm = ADAM_B1 * m_ref[...] + (1.0 - ADAM_B1) * g
        nv = ADAM_B2 * v_ref[...] + (1.0 - ADAM_B2) * jnp.square(g)
        nm_ref[...] = nm
        nv_ref[...] = nv
        m_hat = nm / (1.0 - ADAM_B1 ** ADAM_STEP)
        v_hat = nv / (1.0 - ADAM_B2 ** ADAM_STEP)
        d_ref[...] = -ADAM_LR * (m_hat / (jnp.sqrt(v_hat) + ADAM_EPS) + ADAM_WD * w_ref[...])

    return pl.pallas_call(
        body, name=name, grid=(rows // tr,),
        in_specs=[pl.BlockSpec((N_DEV, tr, LANES), lambda i: (0, i, 0)), spec, spec, spec],
        out_specs=[spec, spec, spec, spec],
        out_shape=[_sds((rows, LANES), F32)] * 4,
        compiler_params=_params(("parallel",)),
    )(parts, w, m, v)


SHARDED = {"meta_tokens": 1, "w_in": 2, "conv_w": 2, "gla_wg_f": 2, "gla_wg_b": 2, "w_out": 1, "w_mlp_up": 2, "w_mlp_down": 1}
MATMUL_WEIGHTS = ("w_in", "w_out", "w_mlp_up", "w_mlp_down")
SMALL_SHARDED = ("meta_tokens", "conv_w", "gla_wg_f", "gla_wg_b")
WEIGHT_NAMES = ("meta_tokens", "norm_mix_pre", "norm_mix_post", "norm_mlp_pre", "norm_mlp_post", "w_in", "conv_w", "conv_b",
                "lru_wa_f", "lru_ba_f", "lru_wx_f", "lru_bx_f", "lru_lambda_f", "lru_wa_b", "lru_ba_b", "lru_wx_b",
                "lru_bx_b", "lru_lambda_b", "gla_wg_f", "gla_bg_f", "gla_wg_b", "gla_bg_b", "gla_head_norm", "w_out",
                "w_mlp_up", "w_mlp_down")
REPLICATED = tuple(nm for nm in WEIGHT_NAMES if nm not in SHARDED)


def _pack(flat_list, row_multiple):
    sizes = [a.shape[-1] for a in flat_list]
    total = sum(sizes)
    unit = row_multiple * LANES
    padded = -(-total // unit) * unit
    lead = flat_list[0].shape[:-1]
    parts = list(flat_list)
    if padded > total:
        parts.append(jnp.zeros(lead + (padded - total,), flat_list[0].dtype))
    buf = jnp.concatenate(parts, axis=-1).reshape(lead + (padded // LANES, LANES))
    return buf, sizes


def _unpack(buf, sizes, shapes):
    flat = buf.reshape(buf.shape[:-2] + (-1,))
    out, off = [], 0
    for size, shape in zip(sizes, shapes):
        out.append(flat[..., off:off + size].reshape(flat.shape[:-1] + tuple(shape)))
        off += size
    return out


def _full_from_shards(gathered, axis):
    return jnp.concatenate([gathered[j] for j in range(N_DEV)], axis=axis)


def _shards_of_full(full, axis):
    return jnp.stack([s.reshape(-1) for s in jnp.split(full, N_DEV, axis=axis)])


def kernel(x, meta_tokens, norm_mix_pre, norm_mix_post, norm_mlp_pre, norm_mlp_post, w_in, conv_w, conv_b, lru_wa_f, lru_ba_f, lru_wx_f, lru_bx_f, lru_lambda_f, lru_wa_b, lru_ba_b, lru_wx_b, lru_bx_b, lru_lambda_b, gla_wg_f, gla_bg_f, gla_wg_b, gla_bg_b, gla_head_norm, w_out, w_mlp_up, w_mlp_down, loss_target, m_meta_tokens, m_norm_mix_pre, m_norm_mix_post, m_norm_mlp_pre, m_norm_mlp_post, m_w_in, m_conv_w, m_conv_b, m_lru_wa_f, m_lru_ba_f, m_lru_wx_f, m_lru_bx_f, m_lru_lambda_f, m_lru_wa_b, m_lru_ba_b, m_lru_wx_b, m_lru_bx_b, m_lru_lambda_b, m_gla_wg_f, m_gla_bg_f, m_gla_wg_b, m_gla_bg_b, m_gla_head_norm, m_w_out, m_w_mlp_up, m_w_mlp_down, v_meta_tokens, v_norm_mix_pre, v_norm_mix_post, v_norm_mlp_pre, v_norm_mlp_post, v_w_in, v_conv_w, v_conv_b, v_lru_wa_f, v_lru_ba_f, v_lru_wx_f, v_lru_bx_f, v_lru_lambda_f, v_lru_wa_b, v_lru_ba_b, v_lru_wx_b, v_lru_bx_b, v_lru_lambda_b, v_gla_wg_f, v_gla_bg_f, v_gla_wg_b, v_gla_bg_b, v_gla_head_norm, v_w_out, v_w_mlp_up, v_w_mlp_down):
    args = locals()
    w = {nm: args[nm] for nm in WEIGHT_NAMES}
    m = {nm: args["m_" + nm] for nm in WEIGHT_NAMES}
    v = {nm: args["v_" + nm] for nm in WEIGHT_NAMES}
    depth = w_in.shape[0]

    big, big_sizes = _pack([w[nm].astype(BF16).reshape(-1) for nm in MATMUL_WEIGHTS], 16)
    small, small_sizes = _pack([w[nm].reshape(-1) for nm in SMALL_SHARDED], 8)
    big_all, small_all = _all_gather([big, small])
    full = dict(w)
    for nm, part in zip(MATMUL_WEIGHTS, _unpack(big_all, big_sizes, [w[nm].shape for nm in MATMUL_WEIGHTS])):
        full[nm] = _full_from_shards(part, SHARDED[nm])
    for nm, part in zip(SMALL_SHARDED, _unpack(small_all, small_sizes, [w[nm].shape for nm in SMALL_SHARDED])):
        full[nm] = _full_from_shards(part, SHARDED[nm])

    h = jnp.concatenate([jnp.zeros((PAD_ROWS, D_MODEL), F32), full["meta_tokens"], x[0]], axis=0)
    layers = []
    for l in range(depth):
        q = _layer_params({nm: full[nm][l] for nm in WEIGHT_NAMES if nm != "meta_tokens"})
        h, saved = _layer_fwd(h, q)
        layers.append((q, saved))
    dh, loss_part = _loss_and_grad(h, loss_target[0])
    loss = lax.psum(loss_part[0, 0], ("x", "y", "c"))

    grads = [None] * depth
    for l in reversed(range(depth)):
        q, saved = layers[l]
        dh, grads[l] = _layer_bwd(dh, q, saved)
    grad_x = dh[PAD_ROWS + N_META:][None]
    full_grad = {nm: jnp.stack([grads[l][nm].reshape(w[nm].shape[1:]) if nm not in SHARDED
                                else grads[l][nm] for l in range(depth)])
                 for nm in WEIGHT_NAMES if nm != "meta_tokens"}
    full_grad["meta_tokens"] = dh[PAD_ROWS:PAD_ROWS + N_META]

    sharded_names = tuple(SHARDED)
    slabs, slab_sizes = _pack([_shards_of_full(full_grad[nm], SHARDED[nm]) for nm in sharded_names], ADAM_ROWS)
    rep, rep_sizes = _pack([full_grad[nm].reshape(-1) for nm in REPLICATED], ADAM_ROWS)
    slabs_in, rep_in = _exchange_grads(slabs, rep)

    def packed(tree, names):
        return _pack([tree[nm].reshape(-1) for nm in names], ADAM_ROWS)[0]

    out_sharded = _sum_and_adamw(slabs_in, packed(w, sharded_names), packed(m, sharded_names), packed(v, sharded_names),
                                 "adamw_sharded")
    out_rep = _sum_and_adamw(rep_in, packed(w, REPLICATED), packed(m, REPLICATED), packed(v, REPLICATED), "adamw_replicated")
    results = []
    for kind in range(4):
        tree = dict(zip(sharded_names, _unpack(out_sharded[kind], slab_sizes, [w[nm].shape for nm in sharded_names])))
        tree.update(zip(REPLICATED, _unpack(out_rep[kind], rep_sizes, [w[nm].shape for nm in REPLICATED])))
        results.append(tree)
    return (loss, grad_x, *[results[k][nm] for k in range(4) for nm in WEIGHT_NAMES])
```

```python
import functools

import jax
import jax.numpy as jnp
from jax import lax
from jax.experimental import pallas as pl
from jax.experimental.pallas import tpu as pltpu

F32 = jnp.float32
BF16 = jnp.bfloat16

N_DEV = 8
D_MODEL = 1024
N_META = 16
ROW_BLOCK = 256
PAD_ROWS = ROW_BLOCK - N_META
CHUNK = 64
LRU_W = 512
LRU_HEADS = 8
LRU_HD = 64
LRU_C = 8.0
GLA_HEADS = 4
GLA_DK = 64
GLA_DV = 128
GLA_QK = GLA_HEADS * GLA_DK
GLA_W = GLA_HEADS * GLA_DV
GLA_RANK = 16
GATE_NORM = 16.0
D_FF = 4096
D_IN = 2592
Z_W = 2688
ZG_COL_BLOCK = 2560 // 128
EPS = 1e-6
LANES = 128

ADAM_LR = 0.001
ADAM_B1 = 0.9
ADAM_B2 = 0.999
ADAM_EPS = 1e-08
ADAM_WD = 0.01
ADAM_STEP = 10
ADAM_ROWS = 512

VMEM_SPEC = pl.BlockSpec(memory_space=pltpu.VMEM)
ANY_SPEC = pl.BlockSpec(memory_space=pl.ANY)
MESH_ID = pl.DeviceIdType.MESH


def _sds(shape, dtype):
    return jax.ShapeDtypeStruct(shape, dtype)


def _params(sem=None, vmem_mb=None):
    kw = {}
    if sem is not None:
        kw["dimension_semantics"] = sem
    if vmem_mb is not None:
        kw["vmem_limit_bytes"] = vmem_mb * 2 ** 20
    return pltpu.CompilerParams(**kw)


def _row_tile(n):
    for t in (768, 512, 256):
        if n % t == 0:
            return t
    raise ValueError(n)


def _col_tile(k):
    for t in (1024, 896, 768, 640, 512, 384, 256, 128):
        if k % t == 0:
            return t
    raise ValueError(k)


def _sigmoid(x):
    return 1.0 / (1.0 + jnp.exp(-x))


def _gelu_and_grad(x):
    c = 0.7978845608028654
    inner = c * (x + 0.044715 * x * x * x)
    t = jnp.tanh(inner)
    gelu = 0.5 * x * (1.0 + t)
    dgelu = 0.5 * (1.0 + t) + 0.5 * x * (1.0 - t * t) * c * (1.0 + 3.0 * 0.044715 * x * x)
    return gelu, dgelu


def _neg_expm1(y):
    series = -y * (1.0 + y * (0.5 + y * (1.0 / 6.0 + y * (1.0 / 24.0 + y * (1.0 / 120.0 + y * (1.0 / 720.0))))))
    return jnp.where(y > -0.25, series, 1.0 - jnp.exp(y))


def _rms_fwd(x, g):
    rs = lax.rsqrt(jnp.mean(x * x, axis=-1, keepdims=True) + EPS)
    return x * rs * g


def _rms_bwd(x, g, dy):
    rs = lax.rsqrt(jnp.mean(x * x, axis=-1, keepdims=True) + EPS)
    xh = x * rs
    dyg = dy * g
    dx = rs * (dyg - xh * jnp.mean(dyg * xh, axis=-1, keepdims=True))
    return dx, jnp.sum(dy * xh, axis=0, keepdims=True)


def _dot(a, b):
    return jnp.dot(a.astype(BF16), b.astype(BF16), preferred_element_type=F32)


def _dot_nt(a, b):
    return lax.dot_general(a.astype(BF16), b.astype(BF16), (((1,), (1,)), ((), ())), preferred_element_type=F32)


def _dot_tn(a, b):
    return lax.dot_general(a.astype(BF16), b.astype(BF16), (((0,), (0,)), ((), ())), preferred_element_type=F32)


class _LayerParam:
    def __init__(self, array, *index):
        self.array = array
        self.index = index

    @property
    def spec(self):
        lead = len(self.index)
        tail = self.array.shape[lead:]
        index = self.index
        return pl.BlockSpec((None,) * lead + tail, lambda *_: index + (0,) * len(tail))


def _row_ids(rows, block_index):
    return block_index * rows + lax.broadcasted_iota(jnp.int32, (rows, 1), 0)


def _accumulate(ref, value, first):
    @pl.when(first)
    def _():
        ref[...] = value

    @pl.when(jnp.logical_not(first))
    def _():
        ref[...] += value


def _norm_in_proj(h, g, w):
    n, d = h.shape
    zw = w.shape[1]
    tr = ROW_BLOCK

    def body(h_ref, g_ref, w_ref, hn_ref, z_ref):
        hn = _rms_fwd(h_ref[...], g_ref[...]).astype(BF16)
        hn_ref[...] = hn
        z_ref[...] = jnp.dot(hn, w_ref[...], preferred_element_type=F32)

    return pl.pallas_call(
        body, name="norm_in_proj", grid=(n // tr,),
        in_specs=[pl.BlockSpec((tr, d), lambda i: (i, 0)), g.spec, VMEM_SPEC],
        out_specs=[pl.BlockSpec((tr, d), lambda i: (i, 0)), pl.BlockSpec((tr, zw), lambda i: (i, 0))],
        out_shape=[_sds((n, d), BF16), _sds((n, zw), F32)],
        compiler_params=_params(("parallel",), 48),
    )(h, g.array, w)


def _halo_specs(width, nb, col=0):
    per = ROW_BLOCK // 8
    prev = pl.BlockSpec((8, width), lambda i: (jnp.maximum(i * per - 1, 0), col))
    nxt = pl.BlockSpec((8, width), lambda i: (jnp.minimum((i + 1) * per, nb * per - 1), col))
    return prev, nxt


def _shift_down(x, prev8, d):
    n = x.shape[0]
    r = pltpu.roll(x, d, 0)
    p = pltpu.roll(prev8, d, 0)
    row8 = lax.broadcasted_iota(jnp.int32, (8, 1), 0)
    head = jnp.where(row8 < d, p, r[0:8])
    return jnp.concatenate([head, r[8:]], axis=0)


def _shift_up(x, next8, d):
    n = x.shape[0]
    r = pltpu.roll(x, n - d, 0)
    q = pltpu.roll(next8, 8 - d, 0)
    row8 = lax.broadcasted_iota(jnp.int32, (8, 1), 0)
    tail = jnp.where(row8 >= 8 - d, q, r[n - 8:])
    return jnp.concatenate([r[:n - 8], tail], axis=0)


def _conv_fwd(z, conv_w, conv_b):
    n = z.shape[0]
    nb = n // ROW_BLOCK
    prev_spec, next_spec = _halo_specs(LRU_W, nb)

    def body(cur_ref, prev_ref, next_ref, w_ref, b_ref, xc_ref):
        i = pl.program_id(0)
        cur = cur_ref[...]
        prev8 = prev_ref[...] * jnp.where(i > 0, 1.0, 0.0)
        next8 = next_ref[...] * jnp.where(i < nb - 1, 1.0, 0.0)
        w = [w_ref[pl.ds(k, 1), :] for k in range(4)]
        xc = (w[0] * _shift_down(cur, prev8, 2) + w[1] * _shift_down(cur, prev8, 1)
              + w[2] * cur + w[3] * _shift_up(cur, next8, 1) + b_ref[...])
        xc_ref[...] = xc

    return pl.pallas_call(
        body, name="conv_fwd", grid=(nb,),
        in_specs=[pl.BlockSpec((ROW_BLOCK, LRU_W), lambda i: (i, 0)), prev_spec, next_spec, conv_w.spec, conv_b.spec],
        out_specs=pl.BlockSpec((ROW_BLOCK, LRU_W), lambda i: (i, 0)),
        out_shape=_sds((n, LRU_W), F32),
        compiler_params=_params(("parallel",)),
    )(z, z, z, conv_w.array, conv_b.array)


def _conv_bwd(dxc_f, dxc_b, z, conv_w):
    n = z.shape[0]
    nb = n // ROW_BLOCK
    prev_spec, next_spec = _halo_specs(LRU_W, nb)
    row_spec = pl.BlockSpec((ROW_BLOCK, LRU_W), lambda i: (i, 0))

    def body(df_ref, dfp_ref, dfn_ref, db_ref, dbp_ref, dbn_ref, x_ref, xp_ref, xn_ref, w_ref,
             dx_ref, dw_ref, dbias_ref):
        i = pl.program_id(0)
        has_prev = jnp.where(i > 0, 1.0, 0.0)
        has_next = jnp.where(i < nb - 1, 1.0, 0.0)
        dxc = df_ref[...] + db_ref[...]
        dprev = (dfp_ref[...] + dbp_ref[...]) * has_prev
        dnext = (dfn_ref[...] + dbn_ref[...]) * has_next
        x = x_ref[...]
        xprev = xp_ref[...] * has_prev
        xnext = xn_ref[...] * has_next
        w = [w_ref[pl.ds(k, 1), :] for k in range(4)]
        dx_ref[...] = (w[0] * _shift_up(dxc, dnext, 2) + w[1] * _shift_up(dxc, dnext, 1)
                       + w[2] * dxc + w[3] * _shift_down(dxc, dprev, 1))
        dw = jnp.concatenate([
            jnp.sum(dxc * _shift_down(x, xprev, 2), axis=0, keepdims=True),
            jnp.sum(dxc * _shift_down(x, xprev, 1), axis=0, keepdims=True),
            jnp.sum(dxc * x, axis=0, keepdims=True),
            jnp.sum(dxc * _shift_up(x, xnext, 1), axis=0, keepdims=True),
            jnp.zeros((4, LRU_W), F32)], axis=0)
        _accumulate(dw_ref, dw, i == 0)
        _accumulate(dbias_ref, jnp.sum(dxc, axis=0, keepdims=True), i == 0)

    dx, dw, dbias = pl.pallas_call(
        body, name="conv_bwd", grid=(nb,),
        in_specs=[row_spec, prev_spec, next_spec, row_spec, prev_spec, next_spec, row_spec, prev_spec, next_spec,
                  conv_w.spec],
        out_specs=[row_spec, pl.BlockSpec((8, LRU_W), lambda i: (0, 0)), pl.BlockSpec((1, LRU_W), lambda i: (0, 0))],
        out_shape=[_sds((n, LRU_W), F32), _sds((8, LRU_W), F32), _sds((1, LRU_W), F32)],
        compiler_params=_params(("arbitrary",)),
    )(dxc_f, dxc_f, dxc_f, dxc_b, dxc_b, dxc_b, z, z, z, conv_w.array)
    return dx, dw, dbias


def _mix_epilogue(h_f, h_b, o_f, o_b, z, head_norm):
    n = z.shape[0]
    tr = ROW_BLOCK
    spec = pl.BlockSpec((tr, 512), lambda i: (i, 0))

    def body(hf_ref, hb_ref, of_ref, ob_ref, gate_ref, gout_ref, w_ref, y_ref):
        gelu, _ = _gelu_and_grad(gate_ref[...])
        y_ref[:, 0:LRU_W] = ((hf_ref[...] + hb_ref[...]) * gelu).astype(BF16)
        o = of_ref[...] + ob_ref[...]
        gout = gout_ref[...]
        silu = gout * _sigmoid(gout)
        w = w_ref[...]
        for hd in range(GLA_HEADS):
            cs = slice(hd * GLA_DV, (hd + 1) * GLA_DV)
            oh = o[:, cs]
            on = oh * lax.rsqrt(jnp.mean(oh * oh, axis=-1, keepdims=True) + EPS)
            y_ref[:, LRU_W + hd * GLA_DV:LRU_W + (hd + 1) * GLA_DV] = (on * w[:, cs] * silu[:, cs]).astype(BF16)

    return pl.pallas_call(
        body, name="mix_epilogue", grid=(n // tr,),
        in_specs=[spec, spec, spec, spec, pl.BlockSpec((tr, 512), lambda i: (i, 1)),
                  pl.BlockSpec((tr, 512), lambda i: (i, 4)), head_norm.spec],
        out_specs=pl.BlockSpec((tr, D_MODEL), lambda i: (i, 0)),
        out_shape=_sds((n, D_MODEL), BF16),
        compiler_params=_params(("parallel",)),
    )(h_f, h_b, o_f, o_b, z, z, head_norm.array)


def _mix_epilogue_bwd(dymix, h_f, h_b, o_f, o_b, z, head_norm):
    n = z.shape[0]
    tr = ROW_BLOCK
    spec = pl.BlockSpec((tr, 512), lambda i: (i, 0))

    def body(dyl_ref, dyg_ref, hf_ref, hb_ref, of_ref, ob_ref, gate_ref, gout_ref, w_ref,
             dhs_ref, dgate_ref, do_ref, dgout_ref, dw_ref):
        i = pl.program_id(0)
        dyl = dyl_ref[...]
        gelu, dgelu = _gelu_and_grad(gate_ref[...])
        dhs_ref[...] = dyl * gelu
        dgate_ref[...] = dyl * (hf_ref[...] + hb_ref[...]) * dgelu
        dyg = dyg_ref[...]
        o = of_ref[...] + ob_ref[...]
        gout = gout_ref[...]
        sg = _sigmoid(gout)
        silu = gout * sg
        dsilu = sg * (1.0 + gout * (1.0 - sg))
        w = w_ref[...]
        dws = []
        for hd in range(GLA_HEADS):
            cs = slice(hd * GLA_DV, (hd + 1) * GLA_DV)
            oh = o[:, cs]
            rs = lax.rsqrt(jnp.mean(oh * oh, axis=-1, keepdims=True) + EPS)
            on = oh * rs
            dy = dyg[:, cs]
            dgout_ref[:, cs] = dy * on * w[:, cs] * dsilu[:, cs]
            dys = dy * silu[:, cs]
            dws.append(jnp.sum(dys * on, axis=0, keepdims=True))
            don = dys * w[:, cs]
            do_ref[:, cs] = rs * (don - on * jnp.mean(don * on, axis=-1, keepdims=True))
        _accumulate(dw_ref, jnp.concatenate(dws, axis=1), i == 0)

    return pl.pallas_call(
        body, name="mix_epilogue_bwd", grid=(n // tr,),
        in_specs=[pl.BlockSpec((tr, 512), lambda i: (i, 0)), pl.BlockSpec((tr, 512), lambda i: (i, 1)),
                  spec, spec, spec, spec, pl.BlockSpec((tr, 512), lambda i: (i, 1)),
                  pl.BlockSpec((tr, 512), lambda i: (i, 4)), head_norm.spec],
        out_specs=[spec, spec, spec, spec, pl.BlockSpec((1, GLA_W), lambda i: (0, 0))],
        out_shape=[_sds((n, 512), F32)] * 4 + [_sds((1, GLA_W), F32)],
        compiler_params=_params(("arbitrary",)),
    )(dymix, dymix, h_f, h_b, o_f, o_b, z, z, head_norm.array)


def _out_proj(ymix, w_out, h, g):
    n, d = h.shape
    tr = ROW_BLOCK
    spec = pl.BlockSpec((tr, d), lambda i: (i, 0))

    def body(y_ref, w_ref, h_ref, g_ref, mix_ref, hmid_ref):
        mix = jnp.dot(y_ref[...], w_ref[...], preferred_element_type=F32)
        mix_ref[...] = mix
        hmid_ref[...] = h_ref[...] + _rms_fwd(mix, g_ref[...])

    return pl.pallas_call(
        body, name="out_proj", grid=(n // tr,),
        in_specs=[spec, VMEM_SPEC, spec, g.spec],
        out_specs=[spec, spec],
        out_shape=[_sds((n, d), F32), _sds((n, d), F32)],
        compiler_params=_params(("parallel",), 32),
    )(ymix, w_out, h, g.array)


def _out_proj_bwd(dh_mid, mix, g, w_out):
    n, d = mix.shape
    tr = ROW_BLOCK
    spec = pl.BlockSpec((tr, d), lambda i: (i, 0))

    def body(dh_ref, mix_ref, g_ref, w_ref, dmix_ref, dy_ref, dg_ref):
        i = pl.program_id(0)
        dmix, dg = _rms_bwd(mix_ref[...], g_ref[...], dh_ref[...])
        dmix = dmix.astype(BF16)
        dmix_ref[...] = dmix
        dy_ref[...] = _dot_nt(dmix, w_ref[...])
        _accumulate(dg_ref, dg, i == 0)

    return pl.pallas_call(
        body, name="out_proj_bwd", grid=(n // tr,),
        in_specs=[spec, spec, g.spec, VMEM_SPEC],
        out_specs=[spec, spec, pl.BlockSpec((1, d), lambda i: (0, 0))],
        out_shape=[_sds((n, d), BF16), _sds((n, d), F32), _sds((1, d), F32)],
        compiler_params=_params(("arbitrary",), 32),
    )(dh_mid, mix, g.array, w_out)


FF_SLAB = D_FF // N_DEV


def _mlp_fwd(h_mid, g_pre, w_up, w_down, g_post):
    n, d = h_mid.shape
    tr = ROW_BLOCK
    spec = pl.BlockSpec((tr, d), lambda i: (i, 0))

    def body(h_ref, gpre_ref, wup_ref, wdn_ref, gpost_ref, hn_ref, up_ref, ff_ref, hout_ref):
        h = h_ref[...]
        hn = _rms_fwd(h, gpre_ref[...]).astype(BF16)
        hn_ref[...] = hn
        ff = jnp.zeros((tr, d), F32)
        for j in range(N_DEV):
            cs = slice(j * FF_SLAB, (j + 1) * FF_SLAB)
            up = jnp.dot(hn, wup_ref[j], preferred_element_type=F32)
            up_ref[:, cs] = up
            act = jnp.square(jnp.maximum(up, 0.0)).astype(BF16)
            ff = ff + jnp.dot(act, wdn_ref[cs, :], preferred_element_type=F32)
        ff_ref[...] = ff
        hout_ref[...] = h + _rms_fwd(ff, gpost_ref[...])

    return pl.pallas_call(
        body, name="mlp_fwd", grid=(n // tr,),
        in_specs=[spec, g_pre.spec, VMEM_SPEC, VMEM_SPEC, g_post.spec],
        out_specs=[spec, pl.BlockSpec((tr, D_FF), lambda i: (i, 0)), spec, spec],
        out_shape=[_sds((n, d), BF16), _sds((n, D_FF), F32), _sds((n, d), F32), _sds((n, d), F32)],
        compiler_params=_params(("parallel",), 52),
    )(h_mid, g_pre.array, w_up, w_down, g_post.array)


def _mlp_bwd(dh, ff, up, h_mid, g_pre, w_up, w_down, g_post):
    n, d = h_mid.shape
    tr = ROW_BLOCK
    spec = pl.BlockSpec((tr, d), lambda i: (i, 0))
    wide = pl.BlockSpec((tr, D_FF), lambda i: (i, 0))
    gspec = pl.BlockSpec((1, d), lambda i: (0, 0))

    def body(dh_ref, ff_ref, up_ref, h_ref, gpre_ref, wup_ref, wdn_ref, gpost_ref,
             dff_ref, dup_ref, act_ref, dhmid_ref, dgpost_ref, dgpre_ref):
        i = pl.program_id(0)
        dh = dh_ref[...]
        dff, dgpost = _rms_bwd(ff_ref[...], gpost_ref[...], dh)
        dff = dff.astype(BF16)
        dff_ref[...] = dff
        dhn = jnp.zeros((tr, d), F32)
        for j in range(N_DEV):
            cs = slice(j * FF_SLAB, (j + 1) * FF_SLAB)
            relu = jnp.maximum(up_ref[:, cs], 0.0)
            act_ref[:, cs] = jnp.square(relu).astype(BF16)
            dact = _dot_nt(dff, wdn_ref[cs, :])
            dup = (dact * 2.0 * relu).astype(BF16)
            dup_ref[:, cs] = dup
            dhn = dhn + _dot_nt(dup, wup_ref[j])
        dx, dgpre = _rms_bwd(h_ref[...], gpre_ref[...], dhn)
        dhmid_ref[...] = dh + dx
        _accumulate(dgpost_ref, dgpost, i == 0)
        _accumulate(dgpre_ref, dgpre, i == 0)

    return pl.pallas_call(
        body, name="mlp_bwd", grid=(n // tr,),
        in_specs=[spec, spec, wide, spec, g_pre.spec, VMEM_SPEC, VMEM_SPEC, g_post.spec],
        out_specs=[spec, wide, wide, spec, gspec, gspec],
        out_shape=[_sds((n, d), BF16), _sds((n, D_FF), BF16), _sds((n, D_FF), BF16), _sds((n, d), F32),
                   _sds((1, d), F32), _sds((1, d), F32)],
        compiler_params=_params(("arbitrary",), 56),
    )(dh, ff, up, h_mid, g_pre.array, w_up, w_down, g_post.array)


def _in_proj_bwd(pieces, w_in, h, g, dh_mid):
    dxbr, dgate, dqk_f, dqk_b, dv_f, dv_b, dgout, dzg_f, dzg_b = pieces
    n, d = h.shape
    tr = ROW_BLOCK
    spec = pl.BlockSpec((tr, d), lambda i: (i, 0))
    s512 = pl.BlockSpec((tr, 512), lambda i: (i, 0))
    s128 = pl.BlockSpec((tr, LANES), lambda i: (i, 0))

    def body(a_ref, b_ref, cf_ref, cb_ref, df_ref, db_ref, e_ref, ff_ref, fb_ref, w_ref, h_ref, g_ref, dhm_ref,
             dz_ref, dh_ref, dg_ref):
        i = pl.program_id(0)
        real = (_row_ids(tr, i) >= PAD_ROWS).astype(F32)
        dz = jnp.concatenate([a_ref[...], b_ref[...], cf_ref[...] + cb_ref[...], df_ref[...] + db_ref[...],
                              e_ref[...], ff_ref[...] + fb_ref[...]], axis=1) * real
        dz = dz.astype(BF16)
        dz_ref[...] = dz
        dhn = _dot_nt(dz, w_ref[...])
        dx, dg = _rms_bwd(h_ref[...], g_ref[...], dhn)
        dh_ref[...] = (dhm_ref[...] + dx) * real
        _accumulate(dg_ref, dg, i == 0)

    return pl.pallas_call(
        body, name="in_proj_bwd", grid=(n // tr,),
        in_specs=[s512, s512, s512, s512, s512, s512, s512, s128, s128, VMEM_SPEC, spec, g.spec, spec],
        out_specs=[pl.BlockSpec((tr, Z_W), lambda i: (i, 0)), spec, pl.BlockSpec((1, d), lambda i: (0, 0))],
        out_shape=[_sds((n, Z_W), BF16), _sds((n, d), F32), _sds((1, d), F32)],
        compiler_params=_params(("arbitrary",), 48),
    )(dxbr, dgate, dqk_f, dqk_b, dv_f, dv_b, dgout, dzg_f, dzg_b, w_in, h, g.array, dh_mid)


def _matmul_tn(a, b, name, column_slabs=False):
    n, m = a.shape
    k = b.shape[1]
    tr, tm = _row_tile(n), _col_tile(m)
    tk = k // N_DEV if column_slabs else _col_tile(k)

    def body(a_ref, b_ref, o_ref):
        _accumulate(o_ref, _dot_tn(a_ref[...], b_ref[...]), pl.program_id(2) == 0)

    if column_slabs:
        out_spec = pl.BlockSpec((None, tm, tk), lambda mi, ki, r: (ki, mi, 0))
        out_shape = _sds((N_DEV, m, tk), F32)
    else:
        out_spec = pl.BlockSpec((tm, tk), lambda mi, ki, r: (mi, ki))
        out_shape = _sds((m, k), F32)
    return pl.pallas_call(
        body, name=name, grid=(m // tm, k // tk, n // tr),
        in_specs=[pl.BlockSpec((tr, tm), lambda mi, ki, r: (r, mi)), pl.BlockSpec((tr, tk), lambda mi, ki, r: (r, ki))],
        out_specs=out_spec, out_shape=out_shape,
        compiler_params=_params(("parallel", "parallel", "arbitrary"), 40),
    )(a, b)


def _loss_and_grad(h_out, target):
    n, d = h_out.shape
    tr = ROW_BLOCK
    first = (PAD_ROWS + N_META) // tr

    def body(h_ref, t_ref, dh_ref, loss_ref):
        i = pl.program_id(0)
        real = jnp.where(i >= first, 1.0, 0.0)
        diff = (h_ref[...] - t_ref[...]) * real
        dh_ref[...] = diff * (1.0 / d)
        part = 0.5 * jnp.sum(jnp.mean(diff * diff, axis=-1, keepdims=True), axis=0, keepdims=True)
        _accumulate(loss_ref, jnp.broadcast_to(part, (1, LANES)), i == 0)

    return pl.pallas_call(
        body, name="loss_and_grad", grid=(n // tr,),
        in_specs=[pl.BlockSpec((tr, d), lambda i: (i, 0)), pl.BlockSpec((tr, d), lambda i: (jnp.maximum(i - first, 0), 0))],
        out_specs=[pl.BlockSpec((tr, d), lambda i: (i, 0)), pl.BlockSpec((1, LANES), lambda i: (0, 0))],
        out_shape=[_sds((n, d), F32), _sds((1, LANES), F32)],
        compiler_params=_params(("arbitrary",)),
    )(h_out, target)


def _scan_rows(a, u, reverse):
    n = a.shape[0]
    row = lax.broadcasted_iota(jnp.int32, (n, 1), 0)
    d = 1
    while d < n:
        shift = n - d if reverse else d
        keep = (row < n - d) if reverse else (row >= d)
        a_s = pltpu.roll(a, shift, 0)
        u_s = pltpu.roll(u, shift, 0)
        u = jnp.where(keep, a * u_s + u, u)
        a = jnp.where(keep, a * a_s, a)
        d *= 2
    return a, u


def _lru_gates(xc, wcat_ref, bias_ref, lam_ref):
    nl = -lam_ref[...]
    nsp = -LRU_C * (jnp.maximum(nl, 0.0) + jnp.log(1.0 + jnp.exp(-jnp.abs(nl))))
    pre = _dot(xc, wcat_ref[...]) + bias_ref[...]
    r = _sigmoid(pre[:, :LRU_W])
    ig = _sigmoid(pre[:, LRU_W:])
    log_a = r * nsp
    a = jnp.exp(log_a)
    m = jnp.sqrt(_neg_expm1(2.0 * log_a))
    return r, ig, a, m, nsp


def _lru_scan(xc, wcat, bias, lam, reverse):
    n = xc.shape[0]
    nb = n // ROW_BLOCK
    order = (lambda i: nb - 1 - i) if reverse else (lambda i: i)
    spec = pl.BlockSpec((ROW_BLOCK, LRU_W), lambda i: (order(i), 0))
    edge = 0 if reverse else ROW_BLOCK - 1

    def body(xc_ref, wcat_ref, bias_ref, lam_ref, h_ref, carry_ref):
        i = pl.program_id(0)

        @pl.when(i == 0)
        def _():
            carry_ref[...] = jnp.zeros_like(carry_ref)

        xc = xc_ref[...]
        r, ig, a, m, _ = _lru_gates(xc, wcat_ref, bias_ref, lam_ref)
        u = jnp.where(_row_ids(ROW_BLOCK, order(i)) >= PAD_ROWS, m * (ig * xc), 0.0)
        big_a, big_u = _scan_rows(a, u, reverse)
        h_ref[...] = big_a * carry_ref[0:1, :] + big_u
        carry_ref[0:1, :] = h_ref[pl.ds(edge, 1), :]

    return pl.pallas_call(
        body, name="lru_scan_b" if reverse else "lru_scan_f", grid=(nb,),
        in_specs=[spec, wcat.spec, bias.spec, lam.spec],
        out_specs=spec,
        out_shape=_sds((n, LRU_W), F32),
        scratch_shapes=[pltpu.VMEM((8, LRU_W), F32)],
        compiler_params=_params(("arbitrary",)),
    )(xc, wcat.array, bias.array, lam.array)


def _lru_scan_bwd(dhs, xc, h, wcat, bias, lam, reverse):
    n = xc.shape[0]
    nb = n // ROW_BLOCK
    per = ROW_BLOCK // 8
    order = (lambda i: i) if reverse else (lambda i: nb - 1 - i)
    spec = pl.BlockSpec((ROW_BLOCK, LRU_W), lambda i: (order(i), 0))
    if reverse:
        halo = pl.BlockSpec((8, LRU_W), lambda i: (jnp.minimum((order(i) + 1) * per, nb * per - 1), 0))
    else:
        halo = pl.BlockSpec((8, LRU_W), lambda i: (jnp.maximum(order(i) * per - 1, 0), 0))
    edge = ROW_BLOCK - 1 if reverse else 0

    def body(dhs_ref, xc_ref, h_ref, halo_ref, wcat_ref, bias_ref, lam_ref,
             dxc_ref, dw_ref, db_ref, dlam_ref, cdh_ref, ca_ref, tmp_ref):
        i = pl.program_id(0)
        ib = order(i)

        @pl.when(i == 0)
        def _():
            cdh_ref[...] = jnp.zeros_like(cdh_ref)
            ca_ref[...] = jnp.zeros_like(ca_ref)

        xc = xc_ref[...]
        r, ig, a, m, nsp = _lru_gates(xc, wcat_ref, bias_ref, lam_ref)
        row = lax.broadcasted_iota(jnp.int32, (ROW_BLOCK, 1), 0)
        if reverse:
            coef = jnp.where(row == 0, ca_ref[0:1, :], pltpu.roll(a, 1, 0))
            h_nb = jnp.where(row == ROW_BLOCK - 1, halo_ref[0:1, :] * jnp.where(ib < nb - 1, 1.0, 0.0),
                             pltpu.roll(h_ref[...], ROW_BLOCK - 1, 0))
        else:
            coef = jnp.where(row == ROW_BLOCK - 1, ca_ref[0:1, :], pltpu.roll(a, ROW_BLOCK - 1, 0))
            h_nb = jnp.where(row == 0, halo_ref[7:8, :] * jnp.where(ib > 0, 1.0, 0.0), pltpu.roll(h_ref[...], 1, 0))
        big_c, big_v = _scan_rows(coef, dhs_ref[...], not reverse)
        dh = big_c * cdh_ref[0:1, :] + big_v
        tmp_ref[...] = dh
        cdh_ref[0:1, :] = tmp_ref[pl.ds(edge, 1), :]
        tmp_ref[...] = a
        ca_ref[0:1, :] = tmp_ref[pl.ds(edge, 1), :]

        du = jnp.where(_row_ids(ROW_BLOCK, ib) >= PAD_ROWS, dh, 0.0)
        da = dh * h_nb
        dm = du * (ig * xc)
        di = du * (m * xc)
        dlog_a = da * a - dm * (a * a) / m
        dr = dlog_a * nsp
        dpre = jnp.concatenate([dr * r * (1.0 - r), di * ig * (1.0 - ig)], axis=1)
        dxc_ref[...] = du * (m * ig) + _dot_nt(dpre, wcat_ref[...])
        _accumulate(dw_ref, _dot_tn(xc, dpre), i == 0)
        _accumulate(db_ref, jnp.sum(dpre, axis=0, keepdims=True), i == 0)
        _accumulate(dlam_ref, jnp.sum(dlog_a * r, axis=0, keepdims=True), i == 0)

        @pl.when(i == nb - 1)
        def _():
            dlam_ref[...] = dlam_ref[...] * (LRU_C * _sigmoid(-lam_ref[...]))

    return pl.pallas_call(
        body, name="lru_scan_bwd_b" if reverse else "lru_scan_bwd_f", grid=(nb,),
        in_specs=[spec, spec, spec, halo, wcat.spec, bias.spec, lam.spec],
        out_specs=[spec, pl.BlockSpec((LRU_W, 2 * LRU_W), lambda i: (0, 0)),
                   pl.BlockSpec((1, 2 * LRU_W), lambda i: (0, 0)), pl.BlockSpec((1, LRU_W), lambda i: (0, 0))],
        out_shape=[_sds((n, LRU_W), F32), _sds((LRU_W, 2 * LRU_W), F32), _sds((1, 2 * LRU_W), F32), _sds((1, LRU_W), F32)],
        scratch_shapes=[pltpu.VMEM((8, LRU_W), F32), pltpu.VMEM((8, LRU_W), F32), pltpu.VMEM((ROW_BLOCK, LRU_W), F32)],
        compiler_params=_params(("arbitrary",)),
    )(dhs, xc, h, h, wcat.array, bias.array, lam.array)


def _gla_masks(reverse):
    t = lax.broadcasted_iota(jnp.int32, (CHUNK, CHUNK), 0)
    s = lax.broadcasted_iota(jnp.int32, (CHUNK, CHUNK), 1)
    if reverse:
        return (s >= t).astype(F32), s > t
    return (s <= t).astype(F32), s <= t


def _gla_gate(zg, wg_ref, bg_ref):
    pre = _dot(zg, wg_ref[...]) + bg_ref[...]
    g = (jnp.minimum(pre, 0.0) - jnp.log(1.0 + jnp.exp(-jnp.abs(pre)))) * (1.0 / GATE_NORM)
    return pre, g


def _gla_decays(gc, tri):
    b = jnp.dot(tri, gc, precision=lax.Precision.HIGHEST, preferred_element_type=F32)
    b_last = jnp.sum(gc, axis=0, keepdims=True)
    return jnp.exp(b), jnp.exp(-b), jnp.exp(b_last - b), jnp.exp(b_last)


def _gla_scan(z, wg, bg, reverse):
    n = z.shape[0]
    nb = n // ROW_BLOCK
    cpb = ROW_BLOCK // CHUNK
    order = (lambda i: nb - 1 - i) if reverse else (lambda i: i)
    chunks = range(cpb - 1, -1, -1) if reverse else range(cpb)

    def body(qk_ref, v_ref, zg_ref, wg_ref, bg_ref, o_ref, sall_ref, s_ref):
        i = pl.program_id(0)

        @pl.when(i == 0)
        def _():
            s_ref[...] = jnp.zeros_like(s_ref)

        tri, mask = _gla_masks(reverse)
        _, g = _gla_gate(zg_ref[...], wg_ref, bg_ref)
        for c in chunks:
            rows = slice(c * CHUNK, (c + 1) * CHUNK)
            eb, enb, ebl, el = _gla_decays(g[rows], tri)
            qk = qk_ref[rows, :]
            qh = qk[:, :GLA_QK] * (GLA_DK ** -0.5) * eb
            kh = qk[:, GLA_QK:] * enb
            kb = qk[:, GLA_QK:] * ebl
            v = v_ref[rows, :]
            state = s_ref[...]
            sall_ref[c] = state
            outs, new_state = [], []
            for hd in range(GLA_HEADS):
                ks = slice(hd * GLA_DK, (hd + 1) * GLA_DK)
                vs = slice(hd * GLA_DV, (hd + 1) * GLA_DV)
                p = jnp.where(mask, _dot_nt(qh[:, ks], kh[:, ks]), 0.0)
                outs.append(_dot(p, v[:, vs]) + _dot_nt(qh[:, ks], state[:, ks]))
                new_state.append(state[:, ks] * el[:, ks] + _dot_tn(v[:, vs], kb[:, ks]))
            o_ref[rows, :] = jnp.concatenate(outs, axis=1)
            s_ref[...] = jnp.concatenate(new_state, axis=1)

    return pl.pallas_call(
        body, name="gla_scan_b" if reverse else "gla_scan_f", grid=(nb,),
        in_specs=[pl.BlockSpec((ROW_BLOCK, 512), lambda i: (order(i), 2)), pl.BlockSpec((ROW_BLOCK, 512), lambda i: (order(i), 3)),
                  pl.BlockSpec((ROW_BLOCK, LANES), lambda i: (order(i), ZG_COL_BLOCK)), wg.spec, bg.spec],
        out_specs=[pl.BlockSpec((ROW_BLOCK, GLA_W), lambda i: (order(i), 0)),
                   pl.BlockSpec((cpb, GLA_DV, GLA_QK), lambda i: (order(i), 0, 0))],
        out_shape=[_sds((n, GLA_W), F32), _sds((n // CHUNK, GLA_DV, GLA_QK), F32)],
        scratch_shapes=[pltpu.VMEM((GLA_DV, GLA_QK), F32)],
        compiler_params=_params(("arbitrary",)),
    )(z, z, z, wg.array, bg.array)


def _gla_scan_bwd(do, z, states, wg, bg, reverse):
    n = z.shape[0]
    nb = n // ROW_BLOCK
    cpb = ROW_BLOCK // CHUNK
    order = (lambda i: i) if reverse else (lambda i: nb - 1 - i)
    chunks = range(cpb) if reverse else range(cpb - 1, -1, -1)

    def body(do_ref, qk_ref, v_ref, zg_ref, sall_ref, wg_ref, bg_ref,
             dqk_ref, dv_ref, dzg_ref, dwg_ref, dbg_ref, ds_ref):
        i = pl.program_id(0)

        @pl.when(i == 0)
        def _():
            ds_ref[...] = jnp.zeros_like(ds_ref)

        tri, mask = _gla_masks(reverse)
        tri_t, _ = _gla_masks(not reverse)
        zg = zg_ref[...]
        pre, g = _gla_gate(zg, wg_ref, bg_ref)
        dgs = [None] * cpb
        for c in chunks:
            rows = slice(c * CHUNK, (c + 1) * CHUNK)
            eb, enb, ebl, el = _gla_decays(g[rows], tri)
            qk = qk_ref[rows, :]
            qh = qk[:, :GLA_QK] * (GLA_DK ** -0.5) * eb
            kh = qk[:, GLA_QK:] * enb
            kb = qk[:, GLA_QK:] * ebl
            v = v_ref[rows, :]
            dout = do_ref[rows, :]
            state = sall_ref[c]
            dstate = ds_ref[...]
            dqh, dkh, dkb, dvs, new_ds = [], [], [], [], []
            for hd in range(GLA_HEADS):
                ks = slice(hd * GLA_DK, (hd + 1) * GLA_DK)
                vs = slice(hd * GLA_DV, (hd + 1) * GLA_DV)
                p = jnp.where(mask, _dot_nt(qh[:, ks], kh[:, ks]), 0.0)
                dp = jnp.where(mask, _dot_nt(dout[:, vs], v[:, vs]), 0.0)
                dvs.append(_dot_tn(p, dout[:, vs]) + _dot_nt(kb[:, ks], dstate[:, ks]))
                dqh.append(_dot(dp, kh[:, ks]) + _dot(dout[:, vs], state[:, ks]))
                dkh.append(_dot_tn(dp, qh[:, ks]))
                dkb.append(_dot(v[:, vs], dstate[:, ks]))
                new_ds.append(dstate[:, ks] * el[:, ks] + _dot_tn(dout[:, vs], qh[:, ks]))
            dqh = jnp.concatenate(dqh, axis=1)
            dkh = jnp.concatenate(dkh, axis=1)
            dkb = jnp.concatenate(dkb, axis=1)
            dv_ref[rows, :] = jnp.concatenate(dvs, axis=1)
            ds_ref[...] = jnp.concatenate(new_ds, axis=1)
            dqk_ref[rows, :] = jnp.concatenate([dqh * eb * (GLA_DK ** -0.5), dkh * enb + dkb * ebl], axis=1)
            dkb_kb = dkb * kb
            db = dqh * qh - dkh * kh - dkb_kb
            db_last = el * jnp.sum(state * dstate, axis=0, keepdims=True) + jnp.sum(dkb_kb, axis=0, keepdims=True)
            dgs[c] = jnp.dot(tri_t, db, precision=lax.Precision.HIGHEST, preferred_element_type=F32) + db_last
        dg = jnp.concatenate(dgs, axis=0)
        dpre = dg * _sigmoid(-pre) * (1.0 / GATE_NORM)
        dzg_ref[...] = _dot_nt(dpre, wg_ref[...])
        _accumulate(dwg_ref, _dot_tn(zg, dpre), i == 0)
        _accumulate(dbg_ref, jnp.sum(dpre, axis=0, keepdims=True), i == 0)

    return pl.pallas_call(
        body, name="gla_scan_bwd_b" if reverse else "gla_scan_bwd_f", grid=(nb,),
        in_specs=[pl.BlockSpec((ROW_BLOCK, GLA_W), lambda i: (order(i), 0)),
                  pl.BlockSpec((ROW_BLOCK, 512), lambda i: (order(i), 2)), pl.BlockSpec((ROW_BLOCK, 512), lambda i: (order(i), 3)),
                  pl.BlockSpec((ROW_BLOCK, LANES), lambda i: (order(i), ZG_COL_BLOCK)),
                  pl.BlockSpec((cpb, GLA_DV, GLA_QK), lambda i: (order(i), 0, 0)), wg.spec, bg.spec],
        out_specs=[pl.BlockSpec((ROW_BLOCK, 512), lambda i: (order(i), 0)), pl.BlockSpec((ROW_BLOCK, 512), lambda i: (order(i), 0)),
                   pl.BlockSpec((ROW_BLOCK, LANES), lambda i: (order(i), 0)),
                   pl.BlockSpec((LANES, GLA_QK), lambda i: (0, 0)), pl.BlockSpec((1, GLA_QK), lambda i: (0, 0))],
        out_shape=[_sds((n, 512), F32), _sds((n, 512), F32), _sds((n, LANES), F32), _sds((LANES, GLA_QK), F32),
                   _sds((1, GLA_QK), F32)],
        scratch_shapes=[pltpu.VMEM((GLA_DV, GLA_QK), F32)],
        compiler_params=_params(("arbitrary",)),
    )(do, z, z, z, states, wg.array, bg.array)


NORM_NAMES = ("norm_mix_pre", "norm_mix_post", "norm_mlp_pre", "norm_mlp_post")
VEC512_NAMES = ("conv_b", "lru_ba_f", "lru_bx_f", "lru_lambda_f", "lru_ba_b", "lru_bx_b", "lru_lambda_b", "gla_head_norm")
VEC256_NAMES = ("gla_bg_f", "gla_bg_b")
LRU_MAT_NAMES = ("lru_wa_f", "lru_wx_f", "lru_wa_b", "lru_wx_b")
DIRS = ("f", "b")


def _prepare_params(w, gathered, depth):
    row_names = NORM_NAMES + ("conv_b", "gla_head_norm")
    ins = ([w[nm] for nm in row_names] + [w["lru_ba_" + d] for d in DIRS] + [w["lru_bx_" + d] for d in DIRS]
           + [w["lru_lambda_" + d] for d in DIRS] + [w["gla_bg_" + d] for d in DIRS]
           + [w["lru_wa_" + d].reshape(depth, LRU_W, LRU_HD) for d in DIRS]
           + [w["lru_wx_" + d].reshape(depth, LRU_W, LRU_HD) for d in DIRS]
           + [gathered["conv_w"], gathered["gla_wg_f"], gathered["gla_wg_b"], gathered["meta_tokens"]])
    n_rows = len(row_names)

    def body(*refs):
        rows_in = refs[:n_rows]
        ba, bx, lam, bg, wa, wx = (refs[n_rows + 2 * t:n_rows + 2 * t + 2] for t in range(6))
        convw_g, wgf_g, wgb_g, meta_g = refs[n_rows + 12:n_rows + 16]
        outs = refs[n_rows + 16:]
        rows_out = outs[:n_rows]
        convw, wcat, bias, lam_o, wg, bg_o, meta = outs[n_rows:]
        for l in range(depth):
            for src, dst in zip(rows_in, rows_out):
                dst[l] = src[pl.ds(l, 1), :]
            convw[l] = jnp.zeros((8, LRU_W), F32)
            for j in range(N_DEV):
                convw[l, 0:4, j * 64:(j + 1) * 64] = convw_g[j, l]
            for d in range(2):
                wcat[l, d] = jnp.zeros((LRU_W, 2 * LRU_W), BF16)
                for hd in range(LRU_HEADS):
                    rs = slice(hd * LRU_HD, (hd + 1) * LRU_HD)
                    wcat[l, d, rs, hd * LRU_HD:(hd + 1) * LRU_HD] = wa[d][l, rs, :].astype(BF16)
                    wcat[l, d, rs, LRU_W + hd * LRU_HD:LRU_W + (hd + 1) * LRU_HD] = wx[d][l, rs, :].astype(BF16)
                bias[l, d, :, 0:LRU_W] = ba[d][pl.ds(l, 1), :]
                bias[l, d, :, LRU_W:2 * LRU_W] = bx[d][pl.ds(l, 1), :]
                lam_o[l, d] = lam[d][pl.ds(l, 1), :]
                bg_o[l, d] = bg[d][pl.ds(l, 1), :]
                wg[l, d] = jnp.zeros((LANES, GLA_QK), BF16)
                src = wgf_g if d == 0 else wgb_g
                for j in range(N_DEV):
                    wg[l, d, d * GLA_RANK:(d + 1) * GLA_RANK, j * 32:(j + 1) * 32] = src[j, l].astype(BF16)
        for j in range(N_DEV):
            meta[:, j * LANES:(j + 1) * LANES] = meta_g[j]

    out_shape = ([_sds((depth, 1, w[nm].shape[1]), F32) for nm in row_names]
                 + [_sds((depth, 8, LRU_W), F32), _sds((depth, 2, LRU_W, 2 * LRU_W), BF16), _sds((depth, 2, 1, 2 * LRU_W), F32),
                    _sds((depth, 2, 1, LRU_W), F32), _sds((depth, 2, LANES, GLA_QK), BF16), _sds((depth, 2, 1, GLA_QK), F32),
                    _sds((N_META, D_MODEL), F32)])
    outs = pl.pallas_call(
        body, name="prepare_params", in_specs=[VMEM_SPEC] * len(ins), out_specs=[VMEM_SPEC] * len(out_shape),
        out_shape=out_shape, compiler_params=_params(None, 32),
    )(*ins)
    prepared = dict(zip(row_names, outs[:n_rows]))
    prepared.update(zip(("conv_w", "wcat", "lru_bias", "lru_lam", "wg", "gla_bg", "meta_tokens"), outs[n_rows:]))
    return prepared


def _layer_fwd(h, l, p):
    lp = lambda name, *index: _LayerParam(p[name], l, *index)
    hn, z = _norm_in_proj(h, lp("norm_mix_pre"), p["w_in"][l])
    xc = _conv_fwd(z, lp("conv_w"), lp("conv_b"))
    s = dict(h=h, hn=hn, z=z, xc=xc)
    for d, name in enumerate(DIRS):
        s["h_" + name] = _lru_scan(xc, lp("wcat", d), lp("lru_bias", d), lp("lru_lam", d), d == 1)
        s["o_" + name], s["s_" + name] = _gla_scan(z, lp("wg", d), lp("gla_bg", d), d == 1)
    s["ymix"] = _mix_epilogue(s["h_f"], s["h_b"], s["o_f"], s["o_b"], z, lp("gla_head_norm"))
    s["mix"], s["h_mid"] = _out_proj(s["ymix"], p["w_out"][l], h, lp("norm_mix_post"))
    s["hn2"], s["up"], s["ff"], h_out = _mlp_fwd(s["h_mid"], lp("norm_mlp_pre"), p["w_mlp_up"][l], p["w_mlp_down"][l],
                                                  lp("norm_mlp_post"))
    return h_out, s


def _layer_bwd(dh_out, l, p, s):
    lp = lambda name, *index: _LayerParam(p[name], l, *index)
    g = {}
    d_ff, dup, act, dh_mid, g["norm_mlp_post"], g["norm_mlp_pre"] = _mlp_bwd(
        dh_out, s["ff"], s["up"], s["h_mid"], lp("norm_mlp_pre"), p["w_mlp_up"][l], p["w_mlp_down"][l], lp("norm_mlp_post"))
    g["w_mlp_down"] = _matmul_tn(act, d_ff, "grad_w_down")
    g["w_mlp_up"] = _matmul_tn(s["hn2"], dup, "grad_w_up", column_slabs=True)
    dmix, dymix, g["norm_mix_post"] = _out_proj_bwd(dh_mid, s["mix"], lp("norm_mix_post"), p["w_out"][l])
    g["w_out"] = _matmul_tn(s["ymix"], dmix, "grad_w_out")
    dhs, dgate, do, dgout, g["gla_head_norm"] = _mix_epilogue_bwd(
        dymix, s["h_f"], s["h_b"], s["o_f"], s["o_b"], s["z"], lp("gla_head_norm"))
    dqk, dv, dzg, dxc = {}, {}, {}, {}
    for d, name in enumerate(DIRS):
        dqk[name], dv[name], dzg[name], g["wg_" + name], g["gla_bg_" + name] = _gla_scan_bwd(
            do, s["z"], s["s_" + name], lp("wg", d), lp("gla_bg", d), d == 1)
        dxc[name], g["wcat_" + name], g["lru_bias_" + name], g["lru_lambda_" + name] = _lru_scan_bwd(
            dhs, s["xc"], s["h_" + name], lp("wcat", d), lp("lru_bias", d), lp("lru_lam", d), d == 1)
    dxbr, g["conv_w"], g["conv_b"] = _conv_bwd(dxc["f"], dxc["b"], s["z"], lp("conv_w"))
    dz, dh_in, g["norm_mix_pre"] = _in_proj_bwd(
        (dxbr, dgate, dqk["f"], dqk["b"], dv["f"], dv["b"], dgout, dzg["f"], dzg["b"]),
        p["w_in"][l], s["h"], lp("norm_mix_pre"), dh_mid)
    g["w_in"] = _matmul_tn(s["hn"], dz, "grad_w_in")
    return dh_in, g


def _pack_small_grads(grads, dh0, depth):
    per_layer = ("norm_mix_pre", "norm_mix_post", "norm_mlp_pre", "norm_mlp_post", "conv_b", "gla_head_norm",
                 "lru_bias_f", "lru_bias_b", "lru_lambda_f", "lru_lambda_b", "gla_bg_f", "gla_bg_b",
                 "wcat_f", "wcat_b", "conv_w", "wg_f", "wg_b")
    ins = [grads[l][nm] for l in range(depth) for nm in per_layer]
    k = len(per_layer)
    meta_rows = PAD_ROWS // N_META

    def body(*refs):
        g = [dict(zip(per_layer, refs[l * k:(l + 1) * k])) for l in range(depth)]
        dh0_ref = refs[depth * k]
        norms, v512, v256, mats, convw, wgf, wgb, meta = refs[depth * k + 1:]
        v256[...] = jnp.zeros_like(v256)
        for l in range(depth):
            for p_, nm in enumerate(NORM_NAMES):
                norms[pl.ds(2 * p_ + l, 1), :] = g[l][nm][...]
            rows512 = [g[l]["conv_b"][...], g[l]["lru_bias_f"][:, 0:LRU_W], g[l]["lru_bias_f"][:, LRU_W:2 * LRU_W],
                       g[l]["lru_lambda_f"][...], g[l]["lru_bias_b"][:, 0:LRU_W], g[l]["lru_bias_b"][:, LRU_W:2 * LRU_W],
                       g[l]["lru_lambda_b"][...], g[l]["gla_head_norm"][...]]
            for p_, row in enumerate(rows512):
                v512[pl.ds(2 * p_ + l, 1), :] = row
            for p_, nm in enumerate(("gla_bg_f", "gla_bg_b")):
                v256[pl.ds(2 * p_ + l, 1), :] = g[l][nm][...]
            for d, name in enumerate(DIRS):
                for hd in range(LRU_HEADS):
                    rs = slice(hd * LRU_HD, (hd + 1) * LRU_HD)
                    mats[2 * d, l, rs, :] = g[l]["wcat_" + name][rs, hd * LRU_HD:(hd + 1) * LRU_HD]
                    mats[2 * d + 1, l, rs, :] = g[l]["wcat_" + name][rs, LRU_W + hd * LRU_HD:LRU_W + (hd + 1) * LRU_HD]
            for j in range(N_DEV):
                convw[j, l] = g[l]["conv_w"][0:4, j * 64:(j + 1) * 64]
                wgf[j, l] = g[l]["wg_f"][0:GLA_RANK, j * 32:(j + 1) * 32]
                wgb[j, l] = g[l]["wg_b"][GLA_RANK:2 * GLA_RANK, j * 32:(j + 1) * 32]
        for j in range(N_DEV):
            meta[j] = dh0_ref[:, j * LANES:(j + 1) * LANES]

    out_shape = [_sds((8, D_MODEL), F32), _sds((16, LRU_W), F32), _sds((8, GLA_QK), F32), _sds((4, depth, LRU_W, LRU_HD), F32),
                 _sds((N_DEV, depth, 4, 64), F32), _sds((N_DEV, depth, GLA_RANK, 32), F32), _sds((N_DEV, depth, GLA_RANK, 32), F32),
                 _sds((N_DEV, N_META, LANES), F32)]
    return pl.pallas_call(
        body, name="pack_small_grads", grid=(1,),
        in_specs=[VMEM_SPEC] * (depth * k) + [pl.BlockSpec((N_META, D_MODEL), lambda i: (meta_rows, 0))],
        out_specs=[VMEM_SPEC] * len(out_shape), out_shape=out_shape, compiler_params=_params(("arbitrary",), 32),
    )(*ins, dh0)


def _my_index():
    return 4 * lax.axis_index("x") + 2 * lax.axis_index("y") + lax.axis_index("c")


def _peer(k):
    x, y, c = lax.axis_index("x"), lax.axis_index("y"), lax.axis_index("c")
    px = x ^ ((k >> 2) & 1)
    py = y ^ ((k >> 1) & 1)
    pc = c ^ (k & 1)
    return (px, py, pc), 4 * px + 2 * py + pc


def _exchange(name, inputs, out_shapes, transfers):
    n_in, n_out, n_tr = len(inputs), len(out_shapes), len(transfers)

    def body(*refs):
        ins, outs = refs[:n_in], refs[n_in:n_in + n_out]
        send_sems, recv_sems, local_sems = refs[n_in + n_out:]
        me = _my_index()
        local = [pltpu.make_async_copy(src(ins[a], me), dst(outs[b], me), local_sems.at[t])
                 for t, (a, src, b, dst) in enumerate(transfers)]
        for cp in local:
            cp.start()

        def copy(t, k, landing):
            a, src, b, dst = transfers[t]
            peer, peer_index = _peer(k)
            sem = t * (N_DEV - 1) + k - 1
            return pltpu.make_async_remote_copy(
                src_ref=src(ins[a], peer_index), dst_ref=dst(outs[b], me if landing is None else peer_index),
                send_sem=send_sems.at[sem], recv_sem=recv_sems.at[sem], device_id=peer, device_id_type=MESH_ID)

        sends = [copy(t, k, None) for k in range(1, N_DEV) for t in range(n_tr)]
        for cp in sends:
            cp.start()
        for k in range(1, N_DEV):
            for t in range(n_tr):
                copy(t, k, "here").wait_recv()
        for cp in sends:
            cp.wait_send()
        for cp in local:
            cp.wait()

    nsem = n_tr * (N_DEV - 1)
    return pl.pallas_call(
        body, name=name,
        in_specs=[ANY_SPEC] * n_in, out_specs=[ANY_SPEC] * n_out, out_shape=out_shapes,
        scratch_shapes=[pltpu.SemaphoreType.DMA((nsem,)), pltpu.SemaphoreType.DMA((nsem,)), pltpu.SemaphoreType.DMA((n_tr,))],
        compiler_params=pltpu.CompilerParams(has_side_effects=True),
    )(*inputs)


def _whole(ref, j):
    return ref


def _slab(ref, j):
    return ref.at[j]


def _layer_of(l):
    return lambda ref, j: ref.at[l]


def _slab_layer(l):
    return lambda ref, j: ref.at[j, l]


def _adamw(g, w, m, v):
    nm = ADAM_B1 * m + (1.0 - ADAM_B1) * g
    nv = ADAM_B2 * v + (1.0 - ADAM_B2) * jnp.square(g)
    m_hat = nm / (1.0 - ADAM_B1 ** ADAM_STEP)
    v_hat = nv / (1.0 - ADAM_B2 ** ADAM_STEP)
    return -ADAM_LR * (m_hat / (jnp.sqrt(v_hat) + ADAM_EPS) + ADAM_WD * w), nm, nv


def _sum_parts(p_ref):
    g = p_ref[0]
    for j in range(1, N_DEV):
        g = g + p_ref[j]
    return g


def _adamw_sharded(parts, w, m, v, name):
    shape = w.shape
    lead, (rows, cols) = shape[:-2], shape[-2:]
    tr = min(rows, ROW_BLOCK)
    assert rows % tr == 0
    nl = len(lead)
    spec = pl.BlockSpec((None,) * nl + (tr, cols), lambda *idx: idx + (0,))
    part_spec = pl.BlockSpec((N_DEV,) + (None,) * nl + (tr, cols), lambda *idx: (0,) + idx + (0,))

    def body(p_ref, w_ref, m_ref, v_ref, g_ref, d_ref, nm_ref, nv_ref):
        g = _sum_parts(p_ref)
        g_ref[...] = g
        d_ref[...], nm_ref[...], nv_ref[...] = _adamw(g, w_ref[...], m_ref[...], v_ref[...])

    return pl.pallas_call(
        body, name=name, grid=lead + (rows // tr,),
        in_specs=[part_spec, spec, spec, spec], out_specs=[spec] * 4, out_shape=[_sds(shape, F32)] * 4,
        compiler_params=_params(("parallel",) * (nl + 1)),
    )(parts, w, m, v)


def _adamw_replicated(gathered, w, m, v, depth):
    names = NORM_NAMES + VEC512_NAMES + VEC256_NAMES + LRU_MAT_NAMES
    count = len(names)

    def body(*refs):
        norms, v512, v256, mats = refs[:4]
        w_refs, m_refs, v_refs = (refs[4 + t * count:4 + (t + 1) * count] for t in range(3))
        outs = refs[4 + 3 * count:4 + 7 * count]
        sum_norms, sum_512, sum_256 = refs[4 + 7 * count:]
        sum_norms[...] = _sum_parts(norms)
        sum_512[...] = _sum_parts(v512)
        sum_256[...] = _sum_parts(v256)
        for n_, nm in enumerate(names):
            if nm in NORM_NAMES:
                g = sum_norms[pl.ds(depth * NORM_NAMES.index(nm), depth), :]
            elif nm in VEC512_NAMES:
                g = sum_512[pl.ds(depth * VEC512_NAMES.index(nm), depth), :]
            elif nm in VEC256_NAMES:
                g = sum_256[pl.ds(depth * VEC256_NAMES.index(nm), depth), :]
            else:
                p_ = LRU_MAT_NAMES.index(nm)
                g = mats[0, p_]
                for j in range(1, N_DEV):
                    g = g + mats[j, p_]
            delta, nm_, nv_ = _adamw(g, w_refs[n_][...], m_refs[n_][...], v_refs[n_][...])
            outs[n_][...] = g
            outs[count + n_][...] = delta
            outs[2 * count + n_][...] = nm_
            outs[3 * count + n_][...] = nv_

    shapes = [_sds(w[nm].shape, F32) for nm in names]
    ins = list(gathered) + [t[nm] for t in (w, m, v) for nm in names]
    outs = pl.pallas_call(
        body, name="adamw_replicated", in_specs=[VMEM_SPEC] * len(ins), out_specs=[VMEM_SPEC] * (4 * count),
        out_shape=shapes * 4,
        scratch_shapes=[pltpu.VMEM(gathered[0].shape[1:], F32), pltpu.VMEM(gathered[1].shape[1:], F32),
                        pltpu.VMEM(gathered[2].shape[1:], F32)],
        compiler_params=_params(None, 48),
    )(*ins)
    return [dict(zip(names, outs[t * count:(t + 1) * count])) for t in range(4)]


WEIGHT_NAMES = ("meta_tokens", "norm_mix_pre", "norm_mix_post", "norm_mlp_pre", "norm_mlp_post", "w_in", "conv_w", "conv_b",
                "lru_wa_f", "lru_ba_f", "lru_wx_f", "lru_bx_f", "lru_lambda_f", "lru_wa_b", "lru_ba_b", "lru_wx_b",
                "lru_bx_b", "lru_lambda_b", "gla_wg_f", "gla_bg_f", "gla_wg_b", "gla_bg_b", "gla_head_norm", "w_out",
                "w_mlp_up", "w_mlp_down")
MATMUL_WEIGHTS = ("w_in", "w_out", "w_mlp_up", "w_mlp_down")
SMALL_SHARDED = ("conv_w", "gla_wg_f", "gla_wg_b", "meta_tokens")


def kernel(x, meta_tokens, norm_mix_pre, norm_mix_post, norm_mlp_pre, norm_mlp_post, w_in, conv_w, conv_b, lru_wa_f, lru_ba_f, lru_wx_f, lru_bx_f, lru_lambda_f, lru_wa_b, lru_ba_b, lru_wx_b, lru_bx_b, lru_lambda_b, gla_wg_f, gla_bg_f, gla_wg_b, gla_bg_b, gla_head_norm, w_out, w_mlp_up, w_mlp_down, loss_target, m_meta_tokens, m_norm_mix_pre, m_norm_mix_post, m_norm_mlp_pre, m_norm_mlp_post, m_w_in, m_conv_w, m_conv_b, m_lru_wa_f, m_lru_ba_f, m_lru_wx_f, m_lru_bx_f, m_lru_lambda_f, m_lru_wa_b, m_lru_ba_b, m_lru_wx_b, m_lru_bx_b, m_lru_lambda_b, m_gla_wg_f, m_gla_bg_f, m_gla_wg_b, m_gla_bg_b, m_gla_head_norm, m_w_out, m_w_mlp_up, m_w_mlp_down, v_meta_tokens, v_norm_mix_pre, v_norm_mix_post, v_norm_mlp_pre, v_norm_mlp_post, v_w_in, v_conv_w, v_conv_b, v_lru_wa_f, v_lru_ba_f, v_lru_wx_f, v_lru_bx_f, v_lru_lambda_f, v_lru_wa_b, v_lru_ba_b, v_lru_wx_b, v_lru_bx_b, v_lru_lambda_b, v_gla_wg_f, v_gla_bg_f, v_gla_wg_b, v_gla_bg_b, v_gla_head_norm, v_w_out, v_w_mlp_up, v_w_mlp_down):
    args = locals()
    w = {nm: args[nm] for nm in WEIGHT_NAMES}
    m = {nm: args["m_" + nm] for nm in WEIGHT_NAMES}
    v = {nm: args["v_" + nm] for nm in WEIGHT_NAMES}
    depth = w_in.shape[0]

    sources = [w[nm].astype(BF16) for nm in MATMUL_WEIGHTS] + [w[nm] for nm in SMALL_SHARDED]
    out_shapes, transfers = [], []
    for a, nm in enumerate(MATMUL_WEIGHTS):
        for l in range(depth):
            transfers.append((a, _layer_of(l), len(out_shapes), _slab))
            out_shapes.append(_sds((N_DEV,) + w[nm].shape[1:], BF16))
    for a, nm in enumerate(SMALL_SHARDED):
        transfers.append((len(MATMUL_WEIGHTS) + a, _whole, len(out_shapes), _slab))
        out_shapes.append(_sds((N_DEV,) + w[nm].shape, F32))
    gathered = _exchange("all_gather", sources, out_shapes, transfers)
    big = {nm: gathered[a * depth:(a + 1) * depth] for a, nm in enumerate(MATMUL_WEIGHTS)}
    p = _prepare_params(w, dict(zip(SMALL_SHARDED, gathered[len(MATMUL_WEIGHTS) * depth:])), depth)
    p["w_in"] = [jnp.pad(jnp.concatenate([g[j] for j in range(N_DEV)], axis=1), ((0, 0), (0, Z_W - D_IN))) for g in big["w_in"]]
    p["w_out"] = [g.reshape(D_MODEL, D_MODEL) for g in big["w_out"]]
    p["w_mlp_up"] = big["w_mlp_up"]
    p["w_mlp_down"] = [g.reshape(D_FF, D_MODEL) for g in big["w_mlp_down"]]

    h = jnp.concatenate([jnp.zeros((PAD_ROWS, D_MODEL), F32), p["meta_tokens"], x[0]], axis=0)
    saved = []
    for l in range(depth):
        h, s = _layer_fwd(h, l, p)
        saved.append(s)
    dh, loss_part = _loss_and_grad(h, loss_target[0])
    loss = lax.psum(loss_part[0, 0], ("x", "y", "c"))

    grads = [None] * depth
    for l in reversed(range(depth)):
        dh, grads[l] = _layer_bwd(dh, l, p, saved[l])
    grad_x = dh[PAD_ROWS + N_META:][None]

    small = _pack_small_grads(grads, dh, depth)
    rep_bufs, small_slabs = small[:4], small[4:]
    inputs, out_shapes, transfers = [], [], []
    for nm in MATMUL_WEIGHTS:
        out_index = len(out_shapes)
        out_shapes.append(_sds((N_DEV,) + w[nm].shape, F32))
        for l in range(depth):
            g = grads[l][nm]
            if nm == "w_in":
                g = jnp.stack([g[:, j * (D_IN // N_DEV):(j + 1) * (D_IN // N_DEV)] for j in range(N_DEV)])
            elif nm != "w_mlp_up":
                g = g.reshape((N_DEV,) + w[nm].shape[1:])
            transfers.append((len(inputs), _slab, out_index, _slab_layer(l)))
            inputs.append(g)
    for g in small_slabs:
        transfers.append((len(inputs), _slab, len(out_shapes), _slab))
        inputs.append(g)
        out_shapes.append(_sds(g.shape, F32))
    for g in rep_bufs:
        transfers.append((len(inputs), _whole, len(out_shapes), _slab))
        inputs.append(g)
        out_shapes.append(_sds((N_DEV,) + g.shape, F32))
    received = _exchange("exchange_grads", inputs, out_shapes, transfers)

    results = [{}, {}, {}, {}]
    for nm, parts in zip(MATMUL_WEIGHTS + SMALL_SHARDED, received[:8]):
        for t, out in enumerate(_adamw_sharded(parts, w[nm], m[nm], v[nm], "adamw_" + nm)):
            results[t][nm] = out

    def kernel_side(tree):
        return {nm: tree[nm].reshape(depth, LRU_W, LRU_HD) if nm in LRU_MAT_NAMES else tree[nm]
                for nm in NORM_NAMES + VEC512_NAMES + VEC256_NAMES + LRU_MAT_NAMES}

    for t, tree in enumerate(_adamw_replicated(received[8:], kernel_side(w), kernel_side(m), kernel_side(v), depth)):
        for nm, out in tree.items():
            results[t][nm] = out.reshape(w[nm].shape)
    return (loss, grad_x, *[results[t][nm] for t in range(4) for nm in WEIGHT_NAMES])
```

```python
import functools

import jax
import jax.numpy as jnp
from jax import lax
from jax.experimental import pallas as pl
from jax.experimental.pallas import tpu as pltpu

F32 = jnp.float32
BF16 = jnp.bfloat16

N_DEV = 8
D_MODEL = 1024
N_META = 16
ROW_BLOCK = 256
PAD_ROWS = ROW_BLOCK - N_META
CHUNK = 64
LRU_W = 512
LRU_HEADS = 8
LRU_HD = 64
LRU_C = 8.0
GLA_HEADS = 4
GLA_DK = 64
GLA_DV = 128
GLA_QK = GLA_HEADS * GLA_DK
GLA_W = GLA_HEADS * GLA_DV
GLA_RANK = 16
GATE_NORM = 16.0
D_FF = 4096
D_IN = 2592
Z_W = 2688
ZG_COL_BLOCK = 2560 // 128
EPS = 1e-6
LANES = 128

ADAM_LR = 0.001
ADAM_B1 = 0.9
ADAM_B2 = 0.999
ADAM_EPS = 1e-08
ADAM_WD = 0.01
ADAM_STEP = 10
ADAM_ROWS = 512

VMEM_SPEC = pl.BlockSpec(memory_space=pltpu.VMEM)
ANY_SPEC = pl.BlockSpec(memory_space=pl.ANY)
MESH_ID = pl.DeviceIdType.MESH


def _sds(shape, dtype):
    return jax.ShapeDtypeStruct(shape, dtype)


def _params(sem=None, vmem_mb=None):
    kw = {}
    if sem is not None:
        kw["dimension_semantics"] = sem
    if vmem_mb is not None:
        kw["vmem_limit_bytes"] = vmem_mb * 2 ** 20
    return pltpu.CompilerParams(**kw)


def _row_tile(n):
    for t in (768, 512, 256):
        if n % t == 0:
            return t
    raise ValueError(n)


def _col_tile(k):
    for t in (1024, 896, 768, 640, 512, 384, 256, 128):
        if k % t == 0:
            return t
    raise ValueError(k)


def _sigmoid(x):
    return 1.0 / (1.0 + jnp.exp(-x))


def _gelu_and_grad(x):
    c = 0.7978845608028654
    inner = c * (x + 0.044715 * x * x * x)
    t = jnp.tanh(inner)
    gelu = 0.5 * x * (1.0 + t)
    dgelu = 0.5 * (1.0 + t) + 0.5 * x * (1.0 - t * t) * c * (1.0 + 3.0 * 0.044715 * x * x)
    return gelu, dgelu


def _neg_expm1(y):
    series = -y * (1.0 + y * (0.5 + y * (1.0 / 6.0 + y * (1.0 / 24.0 + y * (1.0 / 120.0 + y * (1.0 / 720.0))))))
    return jnp.where(y > -0.25, series, 1.0 - jnp.exp(y))


def _rms_fwd(x, g):
    rs = lax.rsqrt(jnp.mean(x * x, axis=-1, keepdims=True) + EPS)
    return x * rs * g


def _rms_bwd(x, g, dy):
    rs = lax.rsqrt(jnp.mean(x * x, axis=-1, keepdims=True) + EPS)
    xh = x * rs
    dyg = dy * g
    dx = rs * (dyg - xh * jnp.mean(dyg * xh, axis=-1, keepdims=True))
    return dx, jnp.sum(dy * xh, axis=0, keepdims=True)


def _dot(a, b):
    return jnp.dot(a.astype(BF16), b.astype(BF16), preferred_element_type=F32)


def _dot_nt(a, b):
    return lax.dot_general(a.astype(BF16), b.astype(BF16), (((1,), (1,)), ((), ())), preferred_element_type=F32)


def _dot_tn(a, b):
    return lax.dot_general(a.astype(BF16), b.astype(BF16), (((0,), (0,)), ((), ())), preferred_element_type=F32)


class _LayerParam:
    def __init__(self, array, *index):
        self.array = array
        self.index = index

    @property
    def spec(self):
        lead = len(self.index)
        tail = self.array.shape[lead:]
        index = self.index
        return pl.BlockSpec((None,) * lead + tail, lambda *_: index + (0,) * len(tail))


def _row_ids(rows, block_index):
    return block_index * rows + lax.broadcasted_iota(jnp.int32, (rows, 1), 0)


def _accumulate(ref, value, first):
    @pl.when(first)
    def _():
        ref[...] = value

    @pl.when(jnp.logical_not(first))
    def _():
        ref[...] += value


def _norm_in_proj(h, g, w):
    n, d = h.shape
    zw = w.shape[1]
    tr = ROW_BLOCK

    def body(h_ref, g_ref, w_ref, hn_ref, z_ref):
        hn = _rms_fwd(h_ref[...], g_ref[...]).astype(BF16)
        hn_ref[...] = hn
        z_ref[...] = jnp.dot(hn, w_ref[...], preferred_element_type=F32)

    return pl.pallas_call(
        body, name="norm_in_proj", grid=(n // tr,),
        in_specs=[pl.BlockSpec((tr, d), lambda i: (i, 0)), g.spec, VMEM_SPEC],
        out_specs=[pl.BlockSpec((tr, d), lambda i: (i, 0)), pl.BlockSpec((tr, zw), lambda i: (i, 0))],
        out_shape=[_sds((n, d), BF16), _sds((n, zw), F32)],
        compiler_params=_params(("parallel",), 48),
    )(h, g.array, w)


def _halo_specs(width, nb, col=0):
    per = ROW_BLOCK // 8
    prev = pl.BlockSpec((8, width), lambda i: (jnp.maximum(i * per - 1, 0), col))
    nxt = pl.BlockSpec((8, width), lambda i: (jnp.minimum((i + 1) * per, nb * per - 1), col))
    return prev, nxt


def _shift_down(x, prev8, d):
    n = x.shape[0]
    r = pltpu.roll(x, d, 0)
    p = pltpu.roll(prev8, d, 0)
    row8 = lax.broadcasted_iota(jnp.int32, (8, 1), 0)
    head = jnp.where(row8 < d, p, r[0:8])
    return jnp.concatenate([head, r[8:]], axis=0)


def _shift_up(x, next8, d):
    n = x.shape[0]
    r = pltpu.roll(x, n - d, 0)
    q = pltpu.roll(next8, 8 - d, 0)
    row8 = lax.broadcasted_iota(jnp.int32, (8, 1), 0)
    tail = jnp.where(row8 >= 8 - d, q, r[n - 8:])
    return jnp.concatenate([r[:n - 8], tail], axis=0)


def _conv_fwd(z, conv_w, conv_b):
    n = z.shape[0]
    nb = n // ROW_BLOCK
    prev_spec, next_spec = _halo_specs(LRU_W, nb)

    def body(cur_ref, prev_ref, next_ref, w_ref, b_ref, xc_ref):
        i = pl.program_id(0)
        cur = cur_ref[...]
        prev8 = prev_ref[...] * jnp.where(i > 0, 1.0, 0.0)
        next8 = next_ref[...] * jnp.where(i < nb - 1, 1.0, 0.0)
        w = [w_ref[pl.ds(k, 1), :] for k in range(4)]
        xc = (w[0] * _shift_down(cur, prev8, 2) + w[1] * _shift_down(cur, prev8, 1)
              + w[2] * cur + w[3] * _shift_up(cur, next8, 1) + b_ref[...])
        xc_ref[...] = xc

    return pl.pallas_call(
        body, name="conv_fwd", grid=(nb,),
        in_specs=[pl.BlockSpec((ROW_BLOCK, LRU_W), lambda i: (i, 0)), prev_spec, next_spec, conv_w.spec, conv_b.spec],
        out_specs=pl.BlockSpec((ROW_BLOCK, LRU_W), lambda i: (i, 0)),
        out_shape=_sds((n, LRU_W), F32),
        compiler_params=_params(("parallel",)),
    )(z, z, z, conv_w.array, conv_b.array)


def _conv_bwd(dxc_f, dxc_b, z, conv_w):
    n = z.shape[0]
    nb = n // ROW_BLOCK
    prev_spec, next_spec = _halo_specs(LRU_W, nb)
    row_spec = pl.BlockSpec((ROW_BLOCK, LRU_W), lambda i: (i, 0))

    def body(df_ref, dfp_ref, dfn_ref, db_ref, dbp_ref, dbn_ref, x_ref, xp_ref, xn_ref, w_ref,
             dx_ref, dw_ref, dbias_ref):
        i = pl.program_id(0)
        has_prev = jnp.where(i > 0, 1.0, 0.0)
        has_next = jnp.where(i < nb - 1, 1.0, 0.0)
        dxc = df_ref[...] + db_ref[...]
        dprev = (dfp_ref[...] + dbp_ref[...]) * has_prev
        dnext = (dfn_ref[...] + dbn_ref[...]) * has_next
        x = x_ref[...]
        xprev = xp_ref[...] * has_prev
        xnext = xn_ref[...] * has_next
        w = [w_ref[pl.ds(k, 1), :] for k in range(4)]
        dx_ref[...] = (w[0] * _shift_up(dxc, dnext, 2) + w[1] * _shift_up(dxc, dnext, 1)
                       + w[2] * dxc + w[3] * _shift_down(dxc, dprev, 1))
        dw = jnp.concatenate([
            jnp.sum(dxc * _shift_down(x, xprev, 2), axis=0, keepdims=True),
            jnp.sum(dxc * _shift_down(x, xprev, 1), axis=0, keepdims=True),
            jnp.sum(dxc * x, axis=0, keepdims=True),
            jnp.sum(dxc * _shift_up(x, xnext, 1), axis=0, keepdims=True),
            jnp.zeros((4, LRU_W), F32)], axis=0)
        _accumulate(dw_ref, dw, i == 0)
        _accumulate(dbias_ref, jnp.sum(dxc, axis=0, keepdims=True), i == 0)

    dx, dw, dbias = pl.pallas_call(
        body, name="conv_bwd", grid=(nb,),
        in_specs=[row_spec, prev_spec, next_spec, row_spec, prev_spec, next_spec, row_spec, prev_spec, next_spec,
                  conv_w.spec],
        out_specs=[row_spec, pl.BlockSpec((8, LRU_W), lambda i: (0, 0)), pl.BlockSpec((1, LRU_W), lambda i: (0, 0))],
        out_shape=[_sds((n, LRU_W), F32), _sds((8, LRU_W), F32), _sds((1, LRU_W), F32)],
        compiler_params=_params(("arbitrary",)),
    )(dxc_f, dxc_f, dxc_f, dxc_b, dxc_b, dxc_b, z, z, z, conv_w.array)
    return dx, dw, dbias


def _mix_epilogue(h_f, h_b, o_f, o_b, z, head_norm):
    n = z.shape[0]
    tr = ROW_BLOCK
    spec = pl.BlockSpec((tr, 512), lambda i: (i, 0))

    def body(hf_ref, hb_ref, of_ref, ob_ref, gate_ref, gout_ref, w_ref, y_ref):
        gelu, _ = _gelu_and_grad(gate_ref[...])
        y_ref[:, 0:LRU_W] = ((hf_ref[...] + hb_ref[...]) * gelu).astype(BF16)
        o = of_ref[...] + ob_ref[...]
        gout = gout_ref[...]
        silu = gout * _sigmoid(gout)
        w = w_ref[...]
        for hd in range(GLA_HEADS):
            cs = slice(hd * GLA_DV, (hd + 1) * GLA_DV)
            oh = o[:, cs]
            on = oh * lax.rsqrt(jnp.mean(oh * oh, axis=-1, keepdims=True) + EPS)
            y_ref[:, LRU_W + hd * GLA_DV:LRU_W + (hd + 1) * GLA_DV] = (on * w[:, cs] * silu[:, cs]).astype(BF16)

    return pl.pallas_call(
        body, name="mix_epilogue", grid=(n // tr,),
        in_specs=[spec, spec, spec, spec, pl.BlockSpec((tr, 512), lambda i: (i, 1)),
                  pl.BlockSpec((tr, 512), lambda i: (i, 4)), head_norm.spec],
        out_specs=pl.BlockSpec((tr, D_MODEL), lambda i: (i, 0)),
        out_shape=_sds((n, D_MODEL), BF16),
        compiler_params=_params(("parallel",)),
    )(h_f, h_b, o_f, o_b, z, z, head_norm.array)


def _mix_epilogue_bwd(dymix, h_f, h_b, o_f, o_b, z, head_norm):
    n = z.shape[0]
    tr = ROW_BLOCK
    spec = pl.BlockSpec((tr, 512), lambda i: (i, 0))

    def body(dyl_ref, dyg_ref, hf_ref, hb_ref, of_ref, ob_ref, gate_ref, gout_ref, w_ref,
             dhs_ref, dgate_ref, do_ref, dgout_ref, dw_ref):
        i = pl.program_id(0)
        dyl = dyl_ref[...]
        gelu, dgelu = _gelu_and_grad(gate_ref[...])
        dhs_ref[...] = dyl * gelu
        dgate_ref[...] = dyl * (hf_ref[...] + hb_ref[...]) * dgelu
        dyg = dyg_ref[...]
        o = of_ref[...] + ob_ref[...]
        gout = gout_ref[...]
        sg = _sigmoid(gout)
        silu = gout * sg
        dsilu = sg * (1.0 + gout * (1.0 - sg))
        w = w_ref[...]
        dws = []
        for hd in range(GLA_HEADS):
            cs = slice(hd * GLA_DV, (hd + 1) * GLA_DV)
            oh = o[:, cs]
            rs = lax.rsqrt(jnp.mean(oh * oh, axis=-1, keepdims=True) + EPS)
            on = oh * rs
            dy = dyg[:, cs]
            dgout_ref[:, cs] = dy * on * w[:, cs] * dsilu[:, cs]
            dys = dy * silu[:, cs]
            dws.append(jnp.sum(dys * on, axis=0, keepdims=True))
            don = dys * w[:, cs]
            do_ref[:, cs] = rs * (don - on * jnp.mean(don * on, axis=-1, keepdims=True))
        _accumulate(dw_ref, jnp.concatenate(dws, axis=1), i == 0)

    return pl.pallas_call(
        body, name="mix_epilogue_bwd", grid=(n // tr,),
        in_specs=[pl.BlockSpec((tr, 512), lambda i: (i, 0)), pl.BlockSpec((tr, 512), lambda i: (i, 1)),
                  spec, spec, spec, spec, pl.BlockSpec((tr, 512), lambda i: (i, 1)),
                  pl.BlockSpec((tr, 512), lambda i: (i, 4)), head_norm.spec],
        out_specs=[spec, spec, spec, spec, pl.BlockSpec((1, GLA_W), lambda i: (0, 0))],
        out_shape=[_sds((n, 512), F32)] * 4 + [_sds((1, GLA_W), F32)],
        compiler_params=_params(("arbitrary",)),
    )(dymix, dymix, h_f, h_b, o_f, o_b, z, z, head_norm.array)


def _out_proj(ymix, w_out, h, g):
    n, d = h.shape
    tr = ROW_BLOCK
    spec = pl.BlockSpec((tr, d), lambda i: (i, 0))

    def body(y_ref, w_ref, h_ref, g_ref, mix_ref, hmid_ref):
        mix = jnp.dot(y_ref[...], w_ref[...], preferred_element_type=F32)
        mix_ref[...] = mix
        hmid_ref[...] = h_ref[...] + _rms_fwd(mix, g_ref[...])

    return pl.pallas_call(
        body, name="out_proj", grid=(n // tr,),
        in_specs=[spec, VMEM_SPEC, spec, g.spec],
        out_specs=[spec, spec],
        out_shape=[_sds((n, d), F32), _sds((n, d), F32)],
        compiler_params=_params(("parallel",), 32),
    )(ymix, w_out, h, g.array)


def _out_proj_bwd(dh_mid, mix, g, w_out):
    n, d = mix.shape
    tr = ROW_BLOCK
    spec = pl.BlockSpec((tr, d), lambda i: (i, 0))

    def body(dh_ref, mix_ref, g_ref, w_ref, dmix_ref, dy_ref, dg_ref):
        i = pl.program_id(0)
        dmix, dg = _rms_bwd(mix_ref[...], g_ref[...], dh_ref[...])
        dmix = dmix.astype(BF16)
        dmix_ref[...] = dmix
        dy_ref[...] = _dot_nt(dmix, w_ref[...])
        _accumulate(dg_ref, dg, i == 0)

    return pl.pallas_call(
        body, name="out_proj_bwd", grid=(n // tr,),
        in_specs=[spec, spec, g.spec, VMEM_SPEC],
        out_specs=[spec, spec, pl.BlockSpec((1, d), lambda i: (0, 0))],
        out_shape=[_sds((n, d), BF16), _sds((n, d), F32), _sds((1, d), F32)],
        compiler_params=_params(("arbitrary",), 32),
    )(dh_mid, mix, g.array, w_out)


FF_SLAB = D_FF // N_DEV


def _mlp_fwd(h_mid, g_pre, w_up, w_down, g_post, exchange=None):
    n, d = h_mid.shape
    tr = ROW_BLOCK
    spec = pl.BlockSpec((tr, d), lambda i: (i, 0))

    def body(h_ref, gpre_ref, wup_ref, wdn_ref, gpost_ref, hn_ref, up_ref, ff_ref, hout_ref):
        h = h_ref[...]
        hn = _rms_fwd(h, gpre_ref[...]).astype(BF16)
        hn_ref[...] = hn
        ff = jnp.zeros((tr, d), F32)
        for j in range(N_DEV):
            cs = slice(j * FF_SLAB, (j + 1) * FF_SLAB)
            up = jnp.dot(hn, wup_ref[j], preferred_element_type=F32)
            up_ref[:, cs] = up
            act = jnp.square(jnp.maximum(up, 0.0)).astype(BF16)
            ff = ff + jnp.dot(act, wdn_ref[cs, :], preferred_element_type=F32)
        ff_ref[...] = ff
        hout_ref[...] = h + _rms_fwd(ff, gpost_ref[...])

    return _hosting_call(
        exchange, body, name="mlp_fwd", grid=(n // tr,),
        in_specs=[spec, g_pre.spec, VMEM_SPEC, VMEM_SPEC, g_post.spec],
        out_specs=[spec, pl.BlockSpec((tr, D_FF), lambda i: (i, 0)), spec, spec],
        out_shape=[_sds((n, d), BF16), _sds((n, D_FF), F32), _sds((n, d), F32), _sds((n, d), F32)],
        scratch_shapes=[], compiler_params=_params(("arbitrary",), 52),
    )(h_mid, g_pre.array, w_up, w_down, g_post.array)


def _mlp_bwd(dh, ff, up, h_mid, g_pre, w_up, w_down, g_post, exchange=None):
    n, d = h_mid.shape
    tr = ROW_BLOCK
    spec = pl.BlockSpec((tr, d), lambda i: (i, 0))
    wide = pl.BlockSpec((tr, D_FF), lambda i: (i, 0))
    gspec = pl.BlockSpec((1, d), lambda i: (0, 0))

    def body(dh_ref, ff_ref, up_ref, h_ref, gpre_ref, wup_ref, wdn_ref, gpost_ref,
             dff_ref, dup_ref, act_ref, dhmid_ref, dgpost_ref, dgpre_ref):
        i = pl.program_id(0)
        dh = dh_ref[...]
        dff, dgpost = _rms_bwd(ff_ref[...], gpost_ref[...], dh)
        dff = dff.astype(BF16)
        dff_ref[...] = dff
        dhn = jnp.zeros((tr, d), F32)
        for j in range(N_DEV):
            cs = slice(j * FF_SLAB, (j + 1) * FF_SLAB)
            relu = jnp.maximum(up_ref[:, cs], 0.0)
            act_ref[:, cs] = jnp.square(relu).astype(BF16)
            dact = _dot_nt(dff, wdn_ref[cs, :])
            dup = (dact * 2.0 * relu).astype(BF16)
            dup_ref[:, cs] = dup
            dhn = dhn + _dot_nt(dup, wup_ref[j])
        dx, dgpre = _rms_bwd(h_ref[...], gpre_ref[...], dhn)
        dhmid_ref[...] = dh + dx
        _accumulate(dgpost_ref, dgpost, i == 0)
        _accumulate(dgpre_ref, dgpre, i == 0)

    return _hosting_call(
        exchange, body, name="mlp_bwd", grid=(n // tr,),
        in_specs=[spec, spec, wide, spec, g_pre.spec, VMEM_SPEC, VMEM_SPEC, g_post.spec],
        out_specs=[spec, wide, wide, spec, gspec, gspec],
        out_shape=[_sds((n, d), BF16), _sds((n, D_FF), BF16), _sds((n, D_FF), BF16), _sds((n, d), F32),
                   _sds((1, d), F32), _sds((1, d), F32)],
        scratch_shapes=[], compiler_params=_params(("arbitrary",), 56),
    )(dh, ff, up, h_mid, g_pre.array, w_up, w_down, g_post.array)


def _in_proj_bwd(pieces, w_in, h, g, dh_mid):
    dxbr, dgate, dqk_f, dqk_b, dv_f, dv_b, dgout, dzg_f, dzg_b = pieces
    n, d = h.shape
    tr = ROW_BLOCK
    spec = pl.BlockSpec((tr, d), lambda i: (i, 0))
    s512 = pl.BlockSpec((tr, 512), lambda i: (i, 0))
    s128 = pl.BlockSpec((tr, LANES), lambda i: (i, 0))

    def body(a_ref, b_ref, cf_ref, cb_ref, df_ref, db_ref, e_ref, ff_ref, fb_ref, w_ref, h_ref, g_ref, dhm_ref,
             dz_ref, dh_ref, dg_ref):
        i = pl.program_id(0)
        real = (_row_ids(tr, i) >= PAD_ROWS).astype(F32)
        dz = jnp.concatenate([a_ref[...], b_ref[...], cf_ref[...] + cb_ref[...], df_ref[...] + db_ref[...],
                              e_ref[...], ff_ref[...] + fb_ref[...]], axis=1) * real
        dz = dz.astype(BF16)
        dz_ref[...] = dz
        dhn = _dot_nt(dz, w_ref[...])
        dx, dg = _rms_bwd(h_ref[...], g_ref[...], dhn)
        dh_ref[...] = (dhm_ref[...] + dx) * real
        _accumulate(dg_ref, dg, i == 0)

    return pl.pallas_call(
        body, name="in_proj_bwd", grid=(n // tr,),
        in_specs=[s512, s512, s512, s512, s512, s512, s512, s128, s128, VMEM_SPEC, spec, g.spec, spec],
        out_specs=[pl.BlockSpec((tr, Z_W), lambda i: (i, 0)), spec, pl.BlockSpec((1, d), lambda i: (0, 0))],
        out_shape=[_sds((n, Z_W), BF16), _sds((n, d), F32), _sds((1, d), F32)],
        compiler_params=_params(("arbitrary",), 48),
    )(dxbr, dgate, dqk_f, dqk_b, dv_f, dv_b, dgout, dzg_f, dzg_b, w_in, h, g.array, dh_mid)


def _matmul_tn(a, b, name, column_slabs=False):
    n, m = a.shape
    k = b.shape[1]
    tr, tm = _row_tile(n), _col_tile(m)
    tk = k // N_DEV if column_slabs else _col_tile(k)
    steps = n // tr

    def body(a_ref, b_ref, o_ref, acc_ref):
        r = pl.program_id(2)
        _accumulate(acc_ref, _dot_tn(a_ref[...], b_ref[...]), r == 0)

        @pl.when(r == steps - 1)
        def _():
            o_ref[...] = acc_ref[...].astype(BF16)

    if column_slabs:
        out_spec = pl.BlockSpec((None, tm, tk), lambda mi, ki, r: (ki, mi, 0))
        out_shape = _sds((N_DEV, m, tk), BF16)
    else:
        out_spec = pl.BlockSpec((tm, tk), lambda mi, ki, r: (mi, ki))
        out_shape = _sds((m, k), BF16)
    return pl.pallas_call(
        body, name=name, grid=(m // tm, k // tk, steps),
        in_specs=[pl.BlockSpec((tr, tm), lambda mi, ki, r: (r, mi)), pl.BlockSpec((tr, tk), lambda mi, ki, r: (r, ki))],
        out_specs=out_spec, out_shape=out_shape, scratch_shapes=[pltpu.VMEM((tm, tk), F32)],
        compiler_params=_params(("parallel", "parallel", "arbitrary"), 40),
    )(a, b)


def _loss_and_grad(h_out, target):
    n, d = h_out.shape
    tr = ROW_BLOCK
    first = (PAD_ROWS + N_META) // tr

    def body(h_ref, t_ref, dh_ref, loss_ref):
        i = pl.program_id(0)
        real = jnp.where(i >= first, 1.0, 0.0)
        diff = (h_ref[...] - t_ref[...]) * real
        dh_ref[...] = diff * (1.0 / d)
        part = 0.5 * jnp.sum(jnp.mean(diff * diff, axis=-1, keepdims=True), axis=0, keepdims=True)
        _accumulate(loss_ref, jnp.broadcast_to(part, (1, LANES)), i == 0)

    return pl.pallas_call(
        body, name="loss_and_grad", grid=(n // tr,),
        in_specs=[pl.BlockSpec((tr, d), lambda i: (i, 0)), pl.BlockSpec((tr, d), lambda i: (jnp.maximum(i - first, 0), 0))],
        out_specs=[pl.BlockSpec((tr, d), lambda i: (i, 0)), pl.BlockSpec((1, LANES), lambda i: (0, 0))],
        out_shape=[_sds((n, d), F32), _sds((1, LANES), F32)],
        compiler_params=_params(("arbitrary",)),
    )(h_out, target)


def _scan_rows(a, u, reverse):
    n = a.shape[0]
    row = lax.broadcasted_iota(jnp.int32, (n, 1), 0)
    d = 1
    while d < n:
        shift = n - d if reverse else d
        keep = (row < n - d) if reverse else (row >= d)
        a_s = pltpu.roll(a, shift, 0)
        u_s = pltpu.roll(u, shift, 0)
        u = jnp.where(keep, a * u_s + u, u)
        a = jnp.where(keep, a * a_s, a)
        d *= 2
    return a, u


def _lru_gates(xc, wcat_ref, bias_ref, lam_ref):
    nl = -lam_ref[...]
    nsp = -LRU_C * (jnp.maximum(nl, 0.0) + jnp.log(1.0 + jnp.exp(-jnp.abs(nl))))
    pre = _dot(xc, wcat_ref[...]) + bias_ref[...]
    r = _sigmoid(pre[:, :LRU_W])
    ig = _sigmoid(pre[:, LRU_W:])
    log_a = r * nsp
    a = jnp.exp(log_a)
    m = jnp.sqrt(_neg_expm1(2.0 * log_a))
    return r, ig, a, m, nsp


def _lru_scan(xc, wcat, bias, lam, reverse):
    n = xc.shape[0]
    nb = n // ROW_BLOCK
    order = (lambda i: nb - 1 - i) if reverse else (lambda i: i)
    spec = pl.BlockSpec((ROW_BLOCK, LRU_W), lambda i: (order(i), 0))
    edge = 0 if reverse else ROW_BLOCK - 1

    def body(xc_ref, wcat_ref, bias_ref, lam_ref, h_ref, carry_ref):
        i = pl.program_id(0)

        @pl.when(i == 0)
        def _():
            carry_ref[...] = jnp.zeros_like(carry_ref)

        xc = xc_ref[...]
        r, ig, a, m, _ = _lru_gates(xc, wcat_ref, bias_ref, lam_ref)
        u = jnp.where(_row_ids(ROW_BLOCK, order(i)) >= PAD_ROWS, m * (ig * xc), 0.0)
        big_a, big_u = _scan_rows(a, u, reverse)
        h_ref[...] = big_a * carry_ref[0:1, :] + big_u
        carry_ref[0:1, :] = h_ref[pl.ds(edge, 1), :]

    return pl.pallas_call(
        body, name="lru_scan_b" if reverse else "lru_scan_f", grid=(nb,),
        in_specs=[spec, wcat.spec, bias.spec, lam.spec],
        out_specs=spec,
        out_shape=_sds((n, LRU_W), F32),
        scratch_shapes=[pltpu.VMEM((8, LRU_W), F32)],
        compiler_params=_params(("arbitrary",)),
    )(xc, wcat.array, bias.array, lam.array)


def _lru_scan_bwd(dhs, xc, h, wcat, bias, lam, reverse):
    n = xc.shape[0]
    nb = n // ROW_BLOCK
    per = ROW_BLOCK // 8
    order = (lambda i: i) if reverse else (lambda i: nb - 1 - i)
    spec = pl.BlockSpec((ROW_BLOCK, LRU_W), lambda i: (order(i), 0))
    if reverse:
        halo = pl.BlockSpec((8, LRU_W), lambda i: (jnp.minimum((order(i) + 1) * per, nb * per - 1), 0))
    else:
        halo = pl.BlockSpec((8, LRU_W), lambda i: (jnp.maximum(order(i) * per - 1, 0), 0))
    edge = ROW_BLOCK - 1 if reverse else 0

    def body(dhs_ref, xc_ref, h_ref, halo_ref, wcat_ref, bias_ref, lam_ref,
             dxc_ref, dw_ref, db_ref, dlam_ref, cdh_ref, ca_ref, tmp_ref):
        i = pl.program_id(0)
        ib = order(i)

        @pl.when(i == 0)
        def _():
            cdh_ref[...] = jnp.zeros_like(cdh_ref)
            ca_ref[...] = jnp.zeros_like(ca_ref)

        xc = xc_ref[...]
        r, ig, a, m, nsp = _lru_gates(xc, wcat_ref, bias_ref, lam_ref)
        row = lax.broadcasted_iota(jnp.int32, (ROW_BLOCK, 1), 0)
        if reverse:
            coef = jnp.where(row == 0, ca_ref[0:1, :], pltpu.roll(a, 1, 0))
            h_nb = jnp.where(row == ROW_BLOCK - 1, halo_ref[0:1, :] * jnp.where(ib < nb - 1, 1.0, 0.0),
                             pltpu.roll(h_ref[...], ROW_BLOCK - 1, 0))
        else:
            coef = jnp.where(row == ROW_BLOCK - 1, ca_ref[0:1, :], pltpu.roll(a, ROW_BLOCK - 1, 0))
            h_nb = jnp.where(row == 0, halo_ref[7:8, :] * jnp.where(ib > 0, 1.0, 0.0), pltpu.roll(h_ref[...], 1, 0))
        big_c, big_v = _scan_rows(coef, dhs_ref[...], not reverse)
        dh = big_c * cdh_ref[0:1, :] + big_v
        tmp_ref[...] = dh
        cdh_ref[0:1, :] = tmp_ref[pl.ds(edge, 1), :]
        tmp_ref[...] = a
        ca_ref[0:1, :] = tmp_ref[pl.ds(edge, 1), :]

        du = jnp.where(_row_ids(ROW_BLOCK, ib) >= PAD_ROWS, dh, 0.0)
        da = dh * h_nb
        dm = du * (ig * xc)
        di = du * (m * xc)
        dlog_a = da * a - dm * (a * a) / m
        dr = dlog_a * nsp
        dpre = jnp.concatenate([dr * r * (1.0 - r), di * ig * (1.0 - ig)], axis=1)
        dxc_ref[...] = du * (m * ig) + _dot_nt(dpre, wcat_ref[...])
        _accumulate(dw_ref, _dot_tn(xc, dpre), i == 0)
        _accumulate(db_ref, jnp.sum(dpre, axis=0, keepdims=True), i == 0)
        _accumulate(dlam_ref, jnp.sum(dlog_a * r, axis=0, keepdims=True), i == 0)

        @pl.when(i == nb - 1)
        def _():
            dlam_ref[...] = dlam_ref[...] * (LRU_C * _sigmoid(-lam_ref[...]))

    return pl.pallas_call(
        body, name="lru_scan_bwd_b" if reverse else "lru_scan_bwd_f", grid=(nb,),
        in_specs=[spec, spec, spec, halo, wcat.spec, bias.spec, lam.spec],
        out_specs=[spec, pl.BlockSpec((LRU_W, 2 * LRU_W), lambda i: (0, 0)),
                   pl.BlockSpec((1, 2 * LRU_W), lambda i: (0, 0)), pl.BlockSpec((1, LRU_W), lambda i: (0, 0))],
        out_shape=[_sds((n, LRU_W), F32), _sds((LRU_W, 2 * LRU_W), F32), _sds((1, 2 * LRU_W), F32), _sds((1, LRU_W), F32)],
        scratch_shapes=[pltpu.VMEM((8, LRU_W), F32), pltpu.VMEM((8, LRU_W), F32), pltpu.VMEM((ROW_BLOCK, LRU_W), F32)],
        compiler_params=_params(("arbitrary",)),
    )(dhs, xc, h, h, wcat.array, bias.array, lam.array)


def _gla_masks(reverse):
    t = lax.broadcasted_iota(jnp.int32, (CHUNK, CHUNK), 0)
    s = lax.broadcasted_iota(jnp.int32, (CHUNK, CHUNK), 1)
    if reverse:
        return (s >= t).astype(F32), s > t
    return (s <= t).astype(F32), s <= t


def _gla_gate(zg, wg_ref, bg_ref):
    pre = _dot(zg, wg_ref[...]) + bg_ref[...]
    g = (jnp.minimum(pre, 0.0) - jnp.log(1.0 + jnp.exp(-jnp.abs(pre)))) * (1.0 / GATE_NORM)
    return pre, g


def _gla_decays(gc, tri):
    b = jnp.dot(tri, gc, precision=lax.Precision.HIGHEST, preferred_element_type=F32)
    b_last = jnp.sum(gc, axis=0, keepdims=True)
    return jnp.exp(b), jnp.exp(-b), jnp.exp(b_last - b), jnp.exp(b_last)


def _gla_scan(z, wg, bg, reverse, exchange=None):
    n = z.shape[0]
    nb = n // ROW_BLOCK
    cpb = ROW_BLOCK // CHUNK
    order = (lambda i: nb - 1 - i) if reverse else (lambda i: i)
    chunks = range(cpb - 1, -1, -1) if reverse else range(cpb)

    def body(qk_ref, v_ref, zg_ref, wg_ref, bg_ref, o_ref, sall_ref, s_ref):
        i = pl.program_id(0)

        @pl.when(i == 0)
        def _():
            s_ref[...] = jnp.zeros_like(s_ref)

        tri, mask = _gla_masks(reverse)
        _, g = _gla_gate(zg_ref[...], wg_ref, bg_ref)
        for c in chunks:
            rows = slice(c * CHUNK, (c + 1) * CHUNK)
            eb, enb, ebl, el = _gla_decays(g[rows], tri)
            qk = qk_ref[rows, :]
            qh = qk[:, :GLA_QK] * (GLA_DK ** -0.5) * eb
            kh = qk[:, GLA_QK:] * enb
            kb = qk[:, GLA_QK:] * ebl
            v = v_ref[rows, :]
            state = s_ref[...]
            sall_ref[c] = state
            outs, new_state = [], []
            for hd in range(GLA_HEADS):
                ks = slice(hd * GLA_DK, (hd + 1) * GLA_DK)
                vs = slice(hd * GLA_DV, (hd + 1) * GLA_DV)
                p = jnp.where(mask, _dot_nt(qh[:, ks], kh[:, ks]), 0.0)
                outs.append(_dot(p, v[:, vs]) + _dot_nt(qh[:, ks], state[:, ks]))
                new_state.append(state[:, ks] * el[:, ks] + _dot_tn(v[:, vs], kb[:, ks]))
            o_ref[rows, :] = jnp.concatenate(outs, axis=1)
            s_ref[...] = jnp.concatenate(new_state, axis=1)

    return _hosting_call(
        exchange, body, name="gla_scan_b" if reverse else "gla_scan_f", grid=(nb,),
        in_specs=[pl.BlockSpec((ROW_BLOCK, 512), lambda i: (order(i), 2)), pl.BlockSpec((ROW_BLOCK, 512), lambda i: (order(i), 3)),
                  pl.BlockSpec((ROW_BLOCK, LANES), lambda i: (order(i), ZG_COL_BLOCK)), wg.spec, bg.spec],
        out_specs=[pl.BlockSpec((ROW_BLOCK, GLA_W), lambda i: (order(i), 0)),
                   pl.BlockSpec((cpb, GLA_DV, GLA_QK), lambda i: (order(i), 0, 0))],
        out_shape=[_sds((n, GLA_W), F32), _sds((n // CHUNK, GLA_DV, GLA_QK), F32)],
        scratch_shapes=[pltpu.VMEM((GLA_DV, GLA_QK), F32)],
        compiler_params=_params(("arbitrary",)),
    )(z, z, z, wg.array, bg.array)


def _gla_scan_bwd(do, z, states, wg, bg, reverse, exchange=None):
    n = z.shape[0]
    nb = n // ROW_BLOCK
    cpb = ROW_BLOCK // CHUNK
    order = (lambda i: i) if reverse else (lambda i: nb - 1 - i)
    chunks = range(cpb) if reverse else range(cpb - 1, -1, -1)

    def body(do_ref, qk_ref, v_ref, zg_ref, sall_ref, wg_ref, bg_ref,
             dqk_ref, dv_ref, dzg_ref, dwg_ref, dbg_ref, ds_ref):
        i = pl.program_id(0)

        @pl.when(i == 0)
        def _():
            ds_ref[...] = jnp.zeros_like(ds_ref)

        tri, mask = _gla_masks(reverse)
        tri_t, _ = _gla_masks(not reverse)
        zg = zg_ref[...]
        pre, g = _gla_gate(zg, wg_ref, bg_ref)
        dgs = [None] * cpb
        for c in chunks:
            rows = slice(c * CHUNK, (c + 1) * CHUNK)
            eb, enb, ebl, el = _gla_decays(g[rows], tri)
            qk = qk_ref[rows, :]
            qh = qk[:, :GLA_QK] * (GLA_DK ** -0.5) * eb
            kh = qk[:, GLA_QK:] * enb
            kb = qk[:, GLA_QK:] * ebl
            v = v_ref[rows, :]
            dout = do_ref[rows, :]
            state = sall_ref[c]
            dstate = ds_ref[...]
            dqh, dkh, dkb, dvs, new_ds = [], [], [], [], []
            for hd in range(GLA_HEADS):
                ks = slice(hd * GLA_DK, (hd + 1) * GLA_DK)
                vs = slice(hd * GLA_DV, (hd + 1) * GLA_DV)
                p = jnp.where(mask, _dot_nt(qh[:, ks], kh[:, ks]), 0.0)
                dp = jnp.where(mask, _dot_nt(dout[:, vs], v[:, vs]), 0.0)
                dvs.append(_dot_tn(p, dout[:, vs]) + _dot_nt(kb[:, ks], dstate[:, ks]))
                dqh.append(_dot(dp, kh[:, ks]) + _dot(dout[:, vs], state[:, ks]))
                dkh.append(_dot_tn(dp, qh[:, ks]))
                dkb.append(_dot(v[:, vs], dstate[:, ks]))
                new_ds.append(dstate[:, ks] * el[:, ks] + _dot_tn(dout[:, vs], qh[:, ks]))
            dqh = jnp.concatenate(dqh, axis=1)
            dkh = jnp.concatenate(dkh, axis=1)
            dkb = jnp.concatenate(dkb, axis=1)
            dv_ref[rows, :] = jnp.concatenate(dvs, axis=1)
            ds_ref[...] = jnp.concatenate(new_ds, axis=1)
            dqk_ref[rows, :] = jnp.concatenate([dqh * eb * (GLA_DK ** -0.5), dkh * enb + dkb * ebl], axis=1)
            dkb_kb = dkb * kb
            db = dqh * qh - dkh * kh - dkb_kb
            db_last = el * jnp.sum(state * dstate, axis=0, keepdims=True) + jnp.sum(dkb_kb, axis=0, keepdims=True)
            dgs[c] = jnp.dot(tri_t, db, precision=lax.Precision.HIGHEST, preferred_element_type=F32) + db_last
        dg = jnp.concatenate(dgs, axis=0)
        dpre = dg * _sigmoid(-pre) * (1.0 / GATE_NORM)
        dzg_ref[...] = _dot_nt(dpre, wg_ref[...])
        _accumulate(dwg_ref, _dot_tn(zg, dpre), i == 0)
        _accumulate(dbg_ref, jnp.sum(dpre, axis=0, keepdims=True), i == 0)

    return _hosting_call(
        exchange, body, name="gla_scan_bwd_b" if reverse else "gla_scan_bwd_f", grid=(nb,),
        in_specs=[pl.BlockSpec((ROW_BLOCK, GLA_W), lambda i: (order(i), 0)),
                  pl.BlockSpec((ROW_BLOCK, 512), lambda i: (order(i), 2)), pl.BlockSpec((ROW_BLOCK, 512), lambda i: (order(i), 3)),
                  pl.BlockSpec((ROW_BLOCK, LANES), lambda i: (order(i), ZG_COL_BLOCK)),
                  pl.BlockSpec((cpb, GLA_DV, GLA_QK), lambda i: (order(i), 0, 0)), wg.spec, bg.spec],
        out_specs=[pl.BlockSpec((ROW_BLOCK, 512), lambda i: (order(i), 0)), pl.BlockSpec((ROW_BLOCK, 512), lambda i: (order(i), 0)),
                   pl.BlockSpec((ROW_BLOCK, LANES), lambda i: (order(i), 0)),
                   pl.BlockSpec((LANES, GLA_QK), lambda i: (0, 0)), pl.BlockSpec((1, GLA_QK), lambda i: (0, 0))],
        out_shape=[_sds((n, 512), F32), _sds((n, 512), F32), _sds((n, LANES), F32), _sds((LANES, GLA_QK), F32),
                   _sds((1, GLA_QK), F32)],
        scratch_shapes=[pltpu.VMEM((GLA_DV, GLA_QK), F32)],
        compiler_params=_params(("arbitrary",)),
    )(do, z, z, z, states, wg.array, bg.array)


NORM_NAMES = ("norm_mix_pre", "norm_mix_post", "norm_mlp_pre", "norm_mlp_post")
VEC512_NAMES = ("conv_b", "lru_ba_f", "lru_bx_f", "lru_lambda_f", "lru_ba_b", "lru_bx_b", "lru_lambda_b", "gla_head_norm")
VEC256_NAMES = ("gla_bg_f", "gla_bg_b")
LRU_MAT_NAMES = ("lru_wa_f", "lru_wx_f", "lru_wa_b", "lru_wx_b")
DIRS = ("f", "b")


def _prepare_params(w, gathered, depth):
    row_names = NORM_NAMES + ("conv_b", "gla_head_norm")
    ins = ([w[nm] for nm in row_names] + [w["lru_ba_" + d] for d in DIRS] + [w["lru_bx_" + d] for d in DIRS]
           + [w["lru_lambda_" + d] for d in DIRS] + [w["gla_bg_" + d] for d in DIRS]
           + [w["lru_wa_" + d].reshape(depth, LRU_W, LRU_HD) for d in DIRS]
           + [w["lru_wx_" + d].reshape(depth, LRU_W, LRU_HD) for d in DIRS]
           + [gathered["conv_w"], gathered["gla_wg_f"], gathered["gla_wg_b"], gathered["meta_tokens"]])
    n_rows = len(row_names)

    def body(*refs):
        rows_in = refs[:n_rows]
        ba, bx, lam, bg, wa, wx = (refs[n_rows + 2 * t:n_rows + 2 * t + 2] for t in range(6))
        convw_g, wgf_g, wgb_g, meta_g = refs[n_rows + 12:n_rows + 16]
        outs = refs[n_rows + 16:]
        rows_out = outs[:n_rows]
        convw, wcat, bias, lam_o, wg, bg_o, meta = outs[n_rows:]
        for l in range(depth):
            for src, dst in zip(rows_in, rows_out):
                dst[l] = src[pl.ds(l, 1), :]
            convw[l] = jnp.zeros((8, LRU_W), F32)
            for j in range(N_DEV):
                convw[l, 0:4, j * 64:(j + 1) * 64] = convw_g[j, l]
            for d in range(2):
                wcat[l, d] = jnp.zeros((LRU_W, 2 * LRU_W), BF16)
                for hd in range(LRU_HEADS):
                    rs = slice(hd * LRU_HD, (hd + 1) * LRU_HD)
                    wcat[l, d, rs, hd * LRU_HD:(hd + 1) * LRU_HD] = wa[d][l, rs, :].astype(BF16)
                    wcat[l, d, rs, LRU_W + hd * LRU_HD:LRU_W + (hd + 1) * LRU_HD] = wx[d][l, rs, :].astype(BF16)
                bias[l, d, :, 0:LRU_W] = ba[d][pl.ds(l, 1), :]
                bias[l, d, :, LRU_W:2 * LRU_W] = bx[d][pl.ds(l, 1), :]
                lam_o[l, d] = lam[d][pl.ds(l, 1), :]
                bg_o[l, d] = bg[d][pl.ds(l, 1), :]
                wg[l, d] = jnp.zeros((LANES, GLA_QK), BF16)
                src = wgf_g if d == 0 else wgb_g
                for j in range(N_DEV):
                    wg[l, d, d * GLA_RANK:(d + 1) * GLA_RANK, j * 32:(j + 1) * 32] = src[j, l].astype(BF16)
        for j in range(N_DEV):
            meta[:, j * LANES:(j + 1) * LANES] = meta_g[j]

    out_shape = ([_sds((depth, 1, w[nm].shape[1]), F32) for nm in row_names]
                 + [_sds((depth, 8, LRU_W), F32), _sds((depth, 2, LRU_W, 2 * LRU_W), BF16), _sds((depth, 2, 1, 2 * LRU_W), F32),
                    _sds((depth, 2, 1, LRU_W), F32), _sds((depth, 2, LANES, GLA_QK), BF16), _sds((depth, 2, 1, GLA_QK), F32),
                    _sds((N_META, D_MODEL), F32)])
    outs = pl.pallas_call(
        body, name="prepare_params", in_specs=[VMEM_SPEC] * len(ins), out_specs=[VMEM_SPEC] * len(out_shape),
        out_shape=out_shape, compiler_params=_params(None, 32),
    )(*ins)
    prepared = dict(zip(row_names, outs[:n_rows]))
    prepared.update(zip(("conv_w", "wcat", "lru_bias", "lru_lam", "wg", "gla_bg", "meta_tokens"), outs[n_rows:]))
    return prepared


class _WeightGather:
    def __init__(self, shards, p):
        self.shards, self.p = shards, p

    def exchange(self, items):
        ex = _Exchange()
        for nm, l in items:
            ex.add(self.shards[nm], _layer_of(l), _sds((N_DEV,) + self.shards[nm].shape[1:], BF16), _slab)
        return ex

    def install(self, items, landed):
        for (nm, l), g in zip(items, landed):
            if nm == "w_in":
                g = jnp.pad(jnp.concatenate([g[j] for j in range(N_DEV)], axis=1), ((0, 0), (0, Z_W - D_IN)))
            elif nm == "w_out":
                g = g.reshape(D_MODEL, D_MODEL)
            elif nm == "w_mlp_down":
                g = g.reshape(D_FF, D_MODEL)
            self.p.setdefault(nm, {})[l] = g


class _GradOutbox:
    def __init__(self):
        self.pending, self.received = [], {}

    def put(self, nm, l, slabs):
        self.pending.append((nm, l, slabs))

    def exchange(self, ex=None):
        ex = ex or _Exchange()
        keys = []
        for nm, l, slabs in self.pending:
            ex.add(slabs, _slab, _sds(slabs.shape, slabs.dtype), _slab)
            keys.append((nm, l))
        self.pending = []
        return ex, keys

    def store(self, keys, landed):
        self.received.update(zip(keys, landed))


def _layer_fwd(h, l, p, gather, depth):
    lp = lambda name, *index: _LayerParam(p[name], l, *index)
    hn, z = _norm_in_proj(h, lp("norm_mix_pre"), p["w_in"][l])
    xc = _conv_fwd(z, lp("conv_w"), lp("conv_b"))
    s = dict(h=h, hn=hn, z=z, xc=xc)
    for d, name in enumerate(DIRS):
        s["h_" + name] = _lru_scan(xc, lp("wcat", d), lp("lru_bias", d), lp("lru_lam", d), d == 1)
        items = [(("w_mlp_up", "w_mlp_down")[d], l)]
        s["o_" + name], s["s_" + name], *landed = _gla_scan(z, lp("wg", d), lp("gla_bg", d), d == 1, gather.exchange(items))
        gather.install(items, landed)
    s["ymix"] = _mix_epilogue(s["h_f"], s["h_b"], s["o_f"], s["o_b"], z, lp("gla_head_norm"))
    s["mix"], s["h_mid"] = _out_proj(s["ymix"], p["w_out"][l], h, lp("norm_mix_post"))
    items = [("w_in", l + 1), ("w_out", l + 1)] if l + 1 < depth else []
    s["hn2"], s["up"], s["ff"], h_out, *landed = _mlp_fwd(
        s["h_mid"], lp("norm_mlp_pre"), p["w_mlp_up"][l], p["w_mlp_down"][l], lp("norm_mlp_post"), gather.exchange(items))
    gather.install(items, landed)
    return h_out, s


def _layer_bwd(dh_out, l, p, s, outbox):
    lp = lambda name, *index: _LayerParam(p[name], l, *index)
    g = {}
    ex, keys = outbox.exchange()
    d_ff, dup, act, dh_mid, g["norm_mlp_post"], g["norm_mlp_pre"], *landed = _mlp_bwd(
        dh_out, s["ff"], s["up"], s["h_mid"], lp("norm_mlp_pre"), p["w_mlp_up"][l], p["w_mlp_down"][l], lp("norm_mlp_post"), ex)
    outbox.store(keys, landed)
    outbox.put("w_mlp_down", l, _matmul_tn(act, d_ff, "grad_w_down").reshape(N_DEV, D_FF // N_DEV, D_MODEL))
    outbox.put("w_mlp_up", l, _matmul_tn(s["hn2"], dup, "grad_w_up", column_slabs=True))
    dmix, dymix, g["norm_mix_post"] = _out_proj_bwd(dh_mid, s["mix"], lp("norm_mix_post"), p["w_out"][l])
    grad_w_out = _matmul_tn(s["ymix"], dmix, "grad_w_out").reshape(N_DEV, D_MODEL // N_DEV, D_MODEL)
    dhs, dgate, do, dgout, g["gla_head_norm"] = _mix_epilogue_bwd(
        dymix, s["h_f"], s["h_b"], s["o_f"], s["o_b"], s["z"], lp("gla_head_norm"))
    dqk, dv, dzg, dxc = {}, {}, {}, {}
    for d, name in enumerate(DIRS):
        ex, keys = outbox.exchange()
        dqk[name], dv[name], dzg[name], g["wg_" + name], g["gla_bg_" + name], *landed = _gla_scan_bwd(
            do, s["z"], s["s_" + name], lp("wg", d), lp("gla_bg", d), d == 1, ex)
        outbox.store(keys, landed)
        if d == 0:
            outbox.put("w_out", l, grad_w_out)
        dxc[name], g["wcat_" + name], g["lru_bias_" + name], g["lru_lambda_" + name] = _lru_scan_bwd(
            dhs, s["xc"], s["h_" + name], lp("wcat", d), lp("lru_bias", d), lp("lru_lam", d), d == 1)
    dxbr, g["conv_w"], g["conv_b"] = _conv_bwd(dxc["f"], dxc["b"], s["z"], lp("conv_w"))
    dz, dh_in, g["norm_mix_pre"] = _in_proj_bwd(
        (dxbr, dgate, dqk["f"], dqk["b"], dv["f"], dv["b"], dgout, dzg["f"], dzg["b"]),
        p["w_in"][l], s["h"], lp("norm_mix_pre"), dh_mid)
    grad_w_in = _matmul_tn(s["hn"], dz, "grad_w_in")
    shard = D_IN // N_DEV
    outbox.put("w_in", l, jnp.stack([grad_w_in[:, j * shard:(j + 1) * shard] for j in range(N_DEV)]))
    return dh_in, g


def _pack_small_grads(grads, dh0, depth):
    per_layer = ("norm_mix_pre", "norm_mix_post", "norm_mlp_pre", "norm_mlp_post", "conv_b", "gla_head_norm",
                 "lru_bias_f", "lru_bias_b", "lru_lambda_f", "lru_lambda_b", "gla_bg_f", "gla_bg_b",
                 "wcat_f", "wcat_b", "conv_w", "wg_f", "wg_b")
    ins = [grads[l][nm] for l in range(depth) for nm in per_layer]
    k = len(per_layer)
    meta_rows = PAD_ROWS // N_META

    def body(*refs):
        g = [dict(zip(per_layer, refs[l * k:(l + 1) * k])) for l in range(depth)]
        dh0_ref = refs[depth * k]
        norms, v512, v256, mats, convw, wgf, wgb, meta = refs[depth * k + 1:]
        v256[...] = jnp.zeros_like(v256)
        for l in range(depth):
            for p_, nm in enumerate(NORM_NAMES):
                norms[pl.ds(2 * p_ + l, 1), :] = g[l][nm][...]
            rows512 = [g[l]["conv_b"][...], g[l]["lru_bias_f"][:, 0:LRU_W], g[l]["lru_bias_f"][:, LRU_W:2 * LRU_W],
                       g[l]["lru_lambda_f"][...], g[l]["lru_bias_b"][:, 0:LRU_W], g[l]["lru_bias_b"][:, LRU_W:2 * LRU_W],
                       g[l]["lru_lambda_b"][...], g[l]["gla_head_norm"][...]]
            for p_, row in enumerate(rows512):
                v512[pl.ds(2 * p_ + l, 1), :] = row
            for p_, nm in enumerate(("gla_bg_f", "gla_bg_b")):
                v256[pl.ds(2 * p_ + l, 1), :] = g[l][nm][...]
            for d, name in enumerate(DIRS):
                for hd in range(LRU_HEADS):
                    rs = slice(hd * LRU_HD, (hd + 1) * LRU_HD)
                    mats[2 * d, l, rs, :] = g[l]["wcat_" + name][rs, hd * LRU_HD:(hd + 1) * LRU_HD]
                    mats[2 * d + 1, l, rs, :] = g[l]["wcat_" + name][rs, LRU_W + hd * LRU_HD:LRU_W + (hd + 1) * LRU_HD]
            for j in range(N_DEV):
                convw[j, l] = g[l]["conv_w"][0:4, j * 64:(j + 1) * 64]
                wgf[j, l] = g[l]["wg_f"][0:GLA_RANK, j * 32:(j + 1) * 32]
                wgb[j, l] = g[l]["wg_b"][GLA_RANK:2 * GLA_RANK, j * 32:(j + 1) * 32]
        for j in range(N_DEV):
            meta[j] = dh0_ref[:, j * LANES:(j + 1) * LANES]

    out_shape = [_sds((8, D_MODEL), F32), _sds((16, LRU_W), F32), _sds((8, GLA_QK), F32), _sds((4, depth, LRU_W, LRU_HD), F32),
                 _sds((N_DEV, depth, 4, 64), F32), _sds((N_DEV, depth, GLA_RANK, 32), F32), _sds((N_DEV, depth, GLA_RANK, 32), F32),
                 _sds((N_DEV, N_META, LANES), F32)]
    return pl.pallas_call(
        body, name="pack_small_grads", grid=(1,),
        in_specs=[VMEM_SPEC] * (depth * k) + [pl.BlockSpec((N_META, D_MODEL), lambda i: (meta_rows, 0))],
        out_specs=[VMEM_SPEC] * len(out_shape), out_shape=out_shape, compiler_params=_params(("arbitrary",), 32),
    )(*ins, dh0)


def _my_index():
    return 4 * lax.axis_index("x") + 2 * lax.axis_index("y") + lax.axis_index("c")


def _peer(k):
    x, y, c = lax.axis_index("x"), lax.axis_index("y"), lax.axis_index("c")
    px = x ^ ((k >> 2) & 1)
    py = y ^ ((k >> 1) & 1)
    pc = c ^ (k & 1)
    return (px, py, pc), 4 * px + 2 * py + pc


class _Exchange:
    def __init__(self):
        self.inputs, self.out_shapes, self.transfers = [], [], []

    def add(self, array, src, out_shape, dst):
        self.transfers.append((len(self.inputs), src, len(self.out_shapes), dst))
        self.inputs.append(array)
        self.out_shapes.append(out_shape)
        return len(self.out_shapes) - 1

    def sem_shapes(self):
        nsem = len(self.transfers) * (N_DEV - 1)
        return [pltpu.SemaphoreType.DMA((nsem,)), pltpu.SemaphoreType.DMA((nsem,)),
                pltpu.SemaphoreType.DMA((len(self.transfers),))]

    def _local(self, ins, outs, sems):
        me = _my_index()
        return [pltpu.make_async_copy(src(ins[a], me), dst(outs[b], me), sems[2].at[t])
                for t, (a, src, b, dst) in enumerate(self.transfers)]

    def _remote(self, ins, outs, sems, t, k, sending):
        a, src, b, dst = self.transfers[t]
        peer, peer_index = _peer(k)
        sem = t * (N_DEV - 1) + k - 1
        return pltpu.make_async_remote_copy(
            src_ref=src(ins[a], peer_index), dst_ref=dst(outs[b], _my_index() if sending else peer_index),
            send_sem=sems[0].at[sem], recv_sem=sems[1].at[sem], device_id=peer, device_id_type=MESH_ID)

    def start(self, ins, outs, sems):
        for cp in self._local(ins, outs, sems):
            cp.start()
        for k in range(1, N_DEV):
            for t in range(len(self.transfers)):
                self._remote(ins, outs, sems, t, k, True).start()

    def wait(self, ins, outs, sems):
        for k in range(1, N_DEV):
            for t in range(len(self.transfers)):
                self._remote(ins, outs, sems, t, k, False).wait_recv()
        for k in range(1, N_DEV):
            for t in range(len(self.transfers)):
                self._remote(ins, outs, sems, t, k, True).wait_send()
        for cp in self._local(ins, outs, sems):
            cp.wait()

    def run(self, name):
        n_in, n_out = len(self.inputs), len(self.out_shapes)

        def body(*refs):
            ins, outs, sems = refs[:n_in], refs[n_in:n_in + n_out], refs[n_in + n_out:]
            self.start(ins, outs, sems)
            self.wait(ins, outs, sems)

        return pl.pallas_call(
            body, name=name, in_specs=[ANY_SPEC] * n_in, out_specs=[ANY_SPEC] * n_out, out_shape=self.out_shapes,
            scratch_shapes=self.sem_shapes(), compiler_params=pltpu.CompilerParams(has_side_effects=True),
        )(*self.inputs)


def _hosting_call(exchange, body, *, name, grid, in_specs, out_specs, out_shape, scratch_shapes, compiler_params):
    if exchange is None or not exchange.transfers:
        return pl.pallas_call(body, name=name, grid=grid, in_specs=in_specs, out_specs=out_specs, out_shape=out_shape,
                              scratch_shapes=scratch_shapes, compiler_params=compiler_params)
    n_in, n_out, n_scr = len(in_specs), len(out_specs), len(scratch_shapes)
    x_in, x_out = len(exchange.inputs), len(exchange.out_shapes)

    def hosted(*refs):
        ins, x_ins = refs[:n_in], refs[n_in:n_in + x_in]
        o0 = n_in + x_in
        outs, x_outs = refs[o0:o0 + n_out], refs[o0 + n_out:o0 + n_out + x_out]
        s0 = o0 + n_out + x_out
        scratch, sems = refs[s0:s0 + n_scr], refs[s0 + n_scr:]
        ids = [pl.program_id(a) for a in range(len(grid))]
        first = functools.reduce(jnp.logical_and, [i == 0 for i in ids])
        last = functools.reduce(jnp.logical_and, [i == g - 1 for i, g in zip(ids, grid)])

        @pl.when(first)
        def _():
            exchange.start(x_ins, x_outs, sems)

        body(*ins, *outs, *scratch)

        @pl.when(last)
        def _():
            exchange.wait(x_ins, x_outs, sems)

    call = pl.pallas_call(
        hosted, name=name, grid=grid, in_specs=list(in_specs) + [ANY_SPEC] * x_in,
        out_specs=list(out_specs) + [ANY_SPEC] * x_out, out_shape=list(out_shape) + list(exchange.out_shapes),
        scratch_shapes=list(scratch_shapes) + exchange.sem_shapes(), compiler_params=compiler_params)
    return lambda *operands: call(*operands, *exchange.inputs)


def _whole(ref, j):
    return ref


def _slab(ref, j):
    return ref.at[j]


def _layer_of(l):
    return lambda ref, j: ref.at[l]


def _slab_layer(l):
    return lambda ref, j: ref.at[j, l]


def _adamw(g, w, m, v):
    nm = ADAM_B1 * m + (1.0 - ADAM_B1) * g
    nv = ADAM_B2 * v + (1.0 - ADAM_B2) * jnp.square(g)
    m_hat = nm / (1.0 - ADAM_B1 ** ADAM_STEP)
    v_hat = nv / (1.0 - ADAM_B2 ** ADAM_STEP)
    return -ADAM_LR * (m_hat / (jnp.sqrt(v_hat) + ADAM_EPS) + ADAM_WD * w), nm, nv


def _sum_parts(p_ref):
    g = p_ref[0].astype(F32)
    for j in range(1, N_DEV):
        g = g + p_ref[j].astype(F32)
    return g


def _adamw_sharded(parts, w, m, v, name):
    shape = w.shape
    lead, (rows, cols) = shape[:-2], shape[-2:]
    tr = min(rows, ROW_BLOCK)
    assert rows % tr == 0
    steps = rows // tr
    nl = len(lead)
    spec = pl.BlockSpec((None,) * nl + (tr, cols), lambda *idx: idx + (0,))
    per_layer = isinstance(parts, (list, tuple))
    if per_layer:
        def part_spec(l):
            return pl.BlockSpec((N_DEV, tr, cols), lambda li, r: (0, jnp.where(li == l, r, jnp.where(li < l, 0, steps - 1)), 0))
        part_specs = [part_spec(l) for l in range(len(parts))]
    else:
        parts = [parts]
        part_specs = [pl.BlockSpec((N_DEV,) + (None,) * nl + (tr, cols), lambda *idx: (0,) + idx + (0,))]
    count = len(parts)

    def body(*refs):
        p_refs = refs[:count]
        w_ref, m_ref, v_ref, g_ref, d_ref, nm_ref, nv_ref = refs[count:]

        def update(p_ref):
            g = _sum_parts(p_ref)
            g_ref[...] = g
            d_ref[...], nm_ref[...], nv_ref[...] = _adamw(g, w_ref[...], m_ref[...], v_ref[...])

        if per_layer:
            for l in range(count):
                pl.when(pl.program_id(0) == l)(functools.partial(update, p_refs[l]))
        else:
            update(p_refs[0])

    return pl.pallas_call(
        body, name=name, grid=lead + (steps,),
        in_specs=part_specs + [spec, spec, spec], out_specs=[spec] * 4, out_shape=[_sds(shape, F32)] * 4,
        compiler_params=_params(("arbitrary",) * (nl + 1)),
    )(*parts, w, m, v)


def _adamw_replicated(gathered, w, m, v, depth):
    names = NORM_NAMES + VEC512_NAMES + VEC256_NAMES + LRU_MAT_NAMES
    count = len(names)

    def body(*refs):
        norms, v512, v256, mats = refs[:4]
        w_refs, m_refs, v_refs = (refs[4 + t * count:4 + (t + 1) * count] for t in range(3))
        outs = refs[4 + 3 * count:4 + 7 * count]
        sum_norms, sum_512, sum_256 = refs[4 + 7 * count:]
        sum_norms[...] = _sum_parts(norms)
        sum_512[...] = _sum_parts(v512)
        sum_256[...] = _sum_parts(v256)
        for n_, nm in enumerate(names):
            if nm in NORM_NAMES:
                g = sum_norms[pl.ds(depth * NORM_NAMES.index(nm), depth), :]
            elif nm in VEC512_NAMES:
                g = sum_512[pl.ds(depth * VEC512_NAMES.index(nm), depth), :]
            elif nm in VEC256_NAMES:
                g = sum_256[pl.ds(depth * VEC256_NAMES.index(nm), depth), :]
            else:
                p_ = LRU_MAT_NAMES.index(nm)
                g = mats[0, p_]
                for j in range(1, N_DEV):
                    g = g + mats[j, p_]
            delta, nm_, nv_ = _adamw(g, w_refs[n_][...], m_refs[n_][...], v_refs[n_][...])
            outs[n_][...] = g
            outs[count + n_][...] = delta
            outs[2 * count + n_][...] = nm_
            outs[3 * count + n_][...] = nv_

    shapes = [_sds(w[nm].shape, F32) for nm in names]
    ins = list(gathered) + [t[nm] for t in (w, m, v) for nm in names]
    outs = pl.pallas_call(
        body, name="adamw_replicated", in_specs=[VMEM_SPEC] * len(ins), out_specs=[VMEM_SPEC] * (4 * count),
        out_shape=shapes * 4,
        scratch_shapes=[pltpu.VMEM(gathered[0].shape[1:], F32), pltpu.VMEM(gathered[1].shape[1:], F32),
                        pltpu.VMEM(gathered[2].shape[1:], F32)],
        compiler_params=_params(None, 48),
    )(*ins)
    return [dict(zip(names, outs[t * count:(t + 1) * count])) for t in range(4)]


WEIGHT_NAMES = ("meta_tokens", "norm_mix_pre", "norm_mix_post", "norm_mlp_pre", "norm_mlp_post", "w_in", "conv_w", "conv_b",
                "lru_wa_f", "lru_ba_f", "lru_wx_f", "lru_bx_f", "lru_lambda_f", "lru_wa_b", "lru_ba_b", "lru_wx_b",
                "lru_bx_b", "lru_lambda_b", "gla_wg_f", "gla_bg_f", "gla_wg_b", "gla_bg_b", "gla_head_norm", "w_out",
                "w_mlp_up", "w_mlp_down")
MATMUL_WEIGHTS = ("w_in", "w_out", "w_mlp_up", "w_mlp_down")
SMALL_SHARDED = ("conv_w", "gla_wg_f", "gla_wg_b", "meta_tokens")


def kernel(x, meta_tokens, norm_mix_pre, norm_mix_post, norm_mlp_pre, norm_mlp_post, w_in, conv_w, conv_b, lru_wa_f, lru_ba_f, lru_wx_f, lru_bx_f, lru_lambda_f, lru_wa_b, lru_ba_b, lru_wx_b, lru_bx_b, lru_lambda_b, gla_wg_f, gla_bg_f, gla_wg_b, gla_bg_b, gla_head_norm, w_out, w_mlp_up, w_mlp_down, loss_target, m_meta_tokens, m_norm_mix_pre, m_norm_mix_post, m_norm_mlp_pre, m_norm_mlp_post, m_w_in, m_conv_w, m_conv_b, m_lru_wa_f, m_lru_ba_f, m_lru_wx_f, m_lru_bx_f, m_lru_lambda_f, m_lru_wa_b, m_lru_ba_b, m_lru_wx_b, m_lru_bx_b, m_lru_lambda_b, m_gla_wg_f, m_gla_bg_f, m_gla_wg_b, m_gla_bg_b, m_gla_head_norm, m_w_out, m_w_mlp_up, m_w_mlp_down, v_meta_tokens, v_norm_mix_pre, v_norm_mix_post, v_norm_mlp_pre, v_norm_mlp_post, v_w_in, v_conv_w, v_conv_b, v_lru_wa_f, v_lru_ba_f, v_lru_wx_f, v_lru_bx_f, v_lru_lambda_f, v_lru_wa_b, v_lru_ba_b, v_lru_wx_b, v_lru_bx_b, v_lru_lambda_b, v_gla_wg_f, v_gla_bg_f, v_gla_wg_b, v_gla_bg_b, v_gla_head_norm, v_w_out, v_w_mlp_up, v_w_mlp_down):
    args = locals()
    w = {nm: args[nm] for nm in WEIGHT_NAMES}
    m = {nm: args["m_" + nm] for nm in WEIGHT_NAMES}
    v = {nm: args["v_" + nm] for nm in WEIGHT_NAMES}
    depth = w_in.shape[0]

    shards = {nm: w[nm].astype(BF16) for nm in MATMUL_WEIGHTS}
    first_items = [("w_in", 0), ("w_out", 0)]
    p = {}
    gather = _WeightGather(shards, p)
    ex = gather.exchange(first_items)
    for nm in SMALL_SHARDED:
        ex.add(w[nm], _whole, _sds((N_DEV,) + w[nm].shape, F32), _slab)
    landed = ex.run("all_gather")
    gather.install(first_items, landed[:len(first_items)])
    p.update(_prepare_params(w, dict(zip(SMALL_SHARDED, landed[len(first_items):])), depth))

    h = jnp.concatenate([jnp.zeros((PAD_ROWS, D_MODEL), F32), p["meta_tokens"], x[0]], axis=0)
    saved = []
    for l in range(depth):
        h, s = _layer_fwd(h, l, p, gather, depth)
        saved.append(s)
    dh, loss_part = _loss_and_grad(h, loss_target[0])
    loss = lax.psum(loss_part[0, 0], ("x", "y", "c"))

    outbox = _GradOutbox()
    grads = [None] * depth
    for l in reversed(range(depth)):
        dh, grads[l] = _layer_bwd(dh, l, p, saved[l], outbox)
    grad_x = dh[PAD_ROWS + N_META:][None]

    small = _pack_small_grads(grads, dh, depth)
    rep_bufs, small_slabs = small[:4], small[4:]
    ex, keys = outbox.exchange()
    for g in small_slabs:
        ex.add(g, _slab, _sds(g.shape, F32), _slab)
    for g in rep_bufs:
        ex.add(g, _whole, _sds((N_DEV,) + g.shape, F32), _slab)
    landed = ex.run("exchange_grads")
    outbox.store(keys, landed[:len(keys)])
    small_received = landed[len(keys):len(keys) + len(small_slabs)]
    rep_received = landed[len(keys) + len(small_slabs):]

    results = [{}, {}, {}, {}]
    for nm in MATMUL_WEIGHTS:
        parts = [outbox.received[(nm, l)] for l in range(depth)]
        for t, out in enumerate(_adamw_sharded(parts, w[nm], m[nm], v[nm], "adamw_" + nm)):
            results[t][nm] = out
    for nm, parts in zip(SMALL_SHARDED, small_received):
        for t, out in enumerate(_adamw_sharded(parts, w[nm], m[nm], v[nm], "adamw_" + nm)):
            results[t][nm] = out

    def kernel_side(tree):
        return {nm: tree[nm].reshape(depth, LRU_W, LRU_HD) if nm in LRU_MAT_NAMES else tree[nm]
                for nm in NORM_NAMES + VEC512_NAMES + VEC256_NAMES + LRU_MAT_NAMES}

    for t, tree in enumerate(_adamw_replicated(rep_received, kernel_side(w), kernel_side(m), kernel_side(v), depth)):
        for nm, out in tree.items():
            results[t][nm] = out.reshape(w[nm].shape)
    return (loss, grad_x, *[results[t][nm] for t in range(4) for nm in WEIGHT_NAMES])
```

```python
import functools

import jax
import jax.numpy as jnp
from jax import lax
from jax.experimental import pallas as pl
from jax.experimental.pallas import tpu as pltpu

F32 = jnp.float32
BF16 = jnp.bfloat16

N_DEV = 8
D_MODEL = 1024
N_META = 16
ROW_BLOCK = 256
PAD_ROWS = ROW_BLOCK - N_META
CHUNK = 64
LRU_W = 512
LRU_HEADS = 8
LRU_HD = 64
LRU_C = 8.0
GLA_HEADS = 4
GLA_DK = 64
GLA_DV = 128
GLA_QK = GLA_HEADS * GLA_DK
GLA_W = GLA_HEADS * GLA_DV
GLA_RANK = 16
GATE_NORM = 16.0
D_FF = 4096
D_IN = 2592
Z_W = 2688
ZG_COL_BLOCK = 2560 // 128
EPS = 1e-6
LANES = 128

ADAM_LR = 0.001
ADAM_B1 = 0.9
ADAM_B2 = 0.999
ADAM_EPS = 1e-08
ADAM_WD = 0.01
ADAM_STEP = 10
ADAM_ROWS = 512

VMEM_SPEC = pl.BlockSpec(memory_space=pltpu.VMEM)
ANY_SPEC = pl.BlockSpec(memory_space=pl.ANY)
MESH_ID = pl.DeviceIdType.MESH


def _sds(shape, dtype):
    return jax.ShapeDtypeStruct(shape, dtype)


def _params(sem=None, vmem_mb=None):
    kw = {}
    if sem is not None:
        kw["dimension_semantics"] = sem
    if vmem_mb is not None:
        kw["vmem_limit_bytes"] = vmem_mb * 2 ** 20
    return pltpu.CompilerParams(**kw)


def _row_tile(n):
    for t in (768, 512, 256):
        if n % t == 0:
            return t
    raise ValueError(n)


def _col_tile(k):
    for t in (1024, 896, 768, 640, 512, 384, 256, 128):
        if k % t == 0:
            return t
    raise ValueError(k)


def _sigmoid(x):
    return 1.0 / (1.0 + jnp.exp(-x))


def _gelu_and_grad(x):
    c = 0.7978845608028654
    inner = c * (x + 0.044715 * x * x * x)
    t = jnp.tanh(inner)
    gelu = 0.5 * x * (1.0 + t)
    dgelu = 0.5 * (1.0 + t) + 0.5 * x * (1.0 - t * t) * c * (1.0 + 3.0 * 0.044715 * x * x)
    return gelu, dgelu


def _neg_expm1(y):
    series = -y * (1.0 + y * (0.5 + y * (1.0 / 6.0 + y * (1.0 / 24.0 + y * (1.0 / 120.0 + y * (1.0 / 720.0))))))
    return jnp.where(y > -0.25, series, 1.0 - jnp.exp(y))


def _rms_fwd(x, g):
    rs = lax.rsqrt(jnp.mean(x * x, axis=-1, keepdims=True) + EPS)
    return x * rs * g


def _rms_bwd(x, g, dy):
    rs = lax.rsqrt(jnp.mean(x * x, axis=-1, keepdims=True) + EPS)
    xh = x * rs
    dyg = dy * g
    dx = rs * (dyg - xh * jnp.mean(dyg * xh, axis=-1, keepdims=True))
    return dx, jnp.sum(dy * xh, axis=0, keepdims=True)


def _dot(a, b):
    return jnp.dot(a.astype(BF16), b.astype(BF16), preferred_element_type=F32)


def _dot_nt(a, b):
    return lax.dot_general(a.astype(BF16), b.astype(BF16), (((1,), (1,)), ((), ())), preferred_element_type=F32)


def _dot_tn(a, b):
    return lax.dot_general(a.astype(BF16), b.astype(BF16), (((0,), (0,)), ((), ())), preferred_element_type=F32)


class _LayerParam:
    def __init__(self, array, *index):
        self.array = array
        self.index = index

    @property
    def spec(self):
        lead = len(self.index)
        tail = self.array.shape[lead:]
        index = self.index
        return pl.BlockSpec((None,) * lead + tail, lambda *_: index + (0,) * len(tail))


def _row_ids(rows, block_index):
    return block_index * rows + lax.broadcasted_iota(jnp.int32, (rows, 1), 0)


def _accumulate(ref, value, first):
    @pl.when(first)
    def _():
        ref[...] = value

    @pl.when(jnp.logical_not(first))
    def _():
        ref[...] += value


def _norm_in_proj(h, g, w):
    n, d = h.shape
    zw = w.shape[1]
    tr = ROW_BLOCK

    def body(h_ref, g_ref, w_ref, hn_ref, z_ref):
        hn = _rms_fwd(h_ref[...], g_ref[...]).astype(BF16)
        hn_ref[...] = hn
        z_ref[...] = jnp.dot(hn, w_ref[...], preferred_element_type=F32)

    return pl.pallas_call(
        body, name="norm_in_proj", grid=(n // tr,),
        in_specs=[pl.BlockSpec((tr, d), lambda i: (i, 0)), g.spec, VMEM_SPEC],
        out_specs=[pl.BlockSpec((tr, d), lambda i: (i, 0)), pl.BlockSpec((tr, zw), lambda i: (i, 0))],
        out_shape=[_sds((n, d), BF16), _sds((n, zw), F32)],
        compiler_params=_params(("parallel",), 48),
    )(h, g.array, w)


def _halo_specs(width, nb, col=0):
    per = ROW_BLOCK // 8
    prev = pl.BlockSpec((8, width), lambda i: (jnp.maximum(i * per - 1, 0), col))
    nxt = pl.BlockSpec((8, width), lambda i: (jnp.minimum((i + 1) * per, nb * per - 1), col))
    return prev, nxt


def _shift_down(x, prev8, d):
    n = x.shape[0]
    r = pltpu.roll(x, d, 0)
    p = pltpu.roll(prev8, d, 0)
    row8 = lax.broadcasted_iota(jnp.int32, (8, 1), 0)
    head = jnp.where(row8 < d, p, r[0:8])
    return jnp.concatenate([head, r[8:]], axis=0)


def _shift_up(x, next8, d):
    n = x.shape[0]
    r = pltpu.roll(x, n - d, 0)
    q = pltpu.roll(next8, 8 - d, 0)
    row8 = lax.broadcasted_iota(jnp.int32, (8, 1), 0)
    tail = jnp.where(row8 >= 8 - d, q, r[n - 8:])
    return jnp.concatenate([r[:n - 8], tail], axis=0)


def _conv_fwd(z, conv_w, conv_b):
    n = z.shape[0]
    nb = n // ROW_BLOCK
    prev_spec, next_spec = _halo_specs(LRU_W, nb)

    def body(cur_ref, prev_ref, next_ref, w_ref, b_ref, xc_ref):
        i = pl.program_id(0)
        cur = cur_ref[...]
        prev8 = prev_ref[...] * jnp.where(i > 0, 1.0, 0.0)
        next8 = next_ref[...] * jnp.where(i < nb - 1, 1.0, 0.0)
        w = [w_ref[pl.ds(k, 1), :] for k in range(4)]
        xc = (w[0] * _shift_down(cur, prev8, 2) + w[1] * _shift_down(cur, prev8, 1)
              + w[2] * cur + w[3] * _shift_up(cur, next8, 1) + b_ref[...])
        xc_ref[...] = xc

    return pl.pallas_call(
        body, name="conv_fwd", grid=(nb,),
        in_specs=[pl.BlockSpec((ROW_BLOCK, LRU_W), lambda i: (i, 0)), prev_spec, next_spec, conv_w.spec, conv_b.spec],
        out_specs=pl.BlockSpec((ROW_BLOCK, LRU_W), lambda i: (i, 0)),
        out_shape=_sds((n, LRU_W), F32),
        compiler_params=_params(("parallel",)),
    )(z, z, z, conv_w.array, conv_b.array)


def _conv_bwd(dxc_f, dxc_b, z, conv_w):
    n = z.shape[0]
    nb = n // ROW_BLOCK
    prev_spec, next_spec = _halo_specs(LRU_W, nb)
    row_spec = pl.BlockSpec((ROW_BLOCK, LRU_W), lambda i: (i, 0))

    def body(df_ref, dfp_ref, dfn_ref, db_ref, dbp_ref, dbn_ref, x_ref, xp_ref, xn_ref, w_ref,
             dx_ref, dw_ref, dbias_ref):
        i = pl.program_id(0)
        has_prev = jnp.where(i > 0, 1.0, 0.0)
        has_next = jnp.where(i < nb - 1, 1.0, 0.0)
        dxc = df_ref[...] + db_ref[...]
        dprev = (dfp_ref[...] + dbp_ref[...]) * has_prev
        dnext = (dfn_ref[...] + dbn_ref[...]) * has_next
        x = x_ref[...]
        xprev = xp_ref[...] * has_prev
        xnext = xn_ref[...] * has_next
        w = [w_ref[pl.ds(k, 1), :] for k in range(4)]
        dx_ref[...] = (w[0] * _shift_up(dxc, dnext, 2) + w[1] * _shift_up(dxc, dnext, 1)
                       + w[2] * dxc + w[3] * _shift_down(dxc, dprev, 1))
        dw = jnp.concatenate([
            jnp.sum(dxc * _shift_down(x, xprev, 2), axis=0, keepdims=True),
            jnp.sum(dxc * _shift_down(x, xprev, 1), axis=0, keepdims=True),
            jnp.sum(dxc * x, axis=0, keepdims=True),
            jnp.sum(dxc * _shift_up(x, xnext, 1), axis=0, keepdims=True),
            jnp.zeros((4, LRU_W), F32)], axis=0)
        _accumulate(dw_ref, dw, i == 0)
        _accumulate(dbias_ref, jnp.sum(dxc, axis=0, keepdims=True), i == 0)

    dx, dw, dbias = pl.pallas_call(
        body, name="conv_bwd", grid=(nb,),
        in_specs=[row_spec, prev_spec, next_spec, row_spec, prev_spec, next_spec, row_spec, prev_spec, next_spec,
                  conv_w.spec],
        out_specs=[row_spec, pl.BlockSpec((8, LRU_W), lambda i: (0, 0)), pl.BlockSpec((1, LRU_W), lambda i: (0, 0))],
        out_shape=[_sds((n, LRU_W), F32), _sds((8, LRU_W), F32), _sds((1, LRU_W), F32)],
        compiler_params=_params(("arbitrary",)),
    )(dxc_f, dxc_f, dxc_f, dxc_b, dxc_b, dxc_b, z, z, z, conv_w.array)
    return dx, dw, dbias


def _mix_epilogue(h_f, h_b, o_f, o_b, z, head_norm):
    n = z.shape[0]
    tr = ROW_BLOCK
    spec = pl.BlockSpec((tr, 512), lambda i: (i, 0))

    def body(hf_ref, hb_ref, of_ref, ob_ref, gate_ref, gout_ref, w_ref, y_ref):
        gelu, _ = _gelu_and_grad(gate_ref[...])
        y_ref[:, 0:LRU_W] = ((hf_ref[...] + hb_ref[...]) * gelu).astype(BF16)
        o = of_ref[...] + ob_ref[...]
        gout = gout_ref[...]
        silu = gout * _sigmoid(gout)
        w = w_ref[...]
        for hd in range(GLA_HEADS):
            cs = slice(hd * GLA_DV, (hd + 1) * GLA_DV)
            oh = o[:, cs]
            on = oh * lax.rsqrt(jnp.mean(oh * oh, axis=-1, keepdims=True) + EPS)
            y_ref[:, LRU_W + hd * GLA_DV:LRU_W + (hd + 1) * GLA_DV] = (on * w[:, cs] * silu[:, cs]).astype(BF16)

    return pl.pallas_call(
        body, name="mix_epilogue", grid=(n // tr,),
        in_specs=[spec, spec, spec, spec, pl.BlockSpec((tr, 512), lambda i: (i, 1)),
                  pl.BlockSpec((tr, 512), lambda i: (i, 4)), head_norm.spec],
        out_specs=pl.BlockSpec((tr, D_MODEL), lambda i: (i, 0)),
        out_shape=_sds((n, D_MODEL), BF16),
        compiler_params=_params(("parallel",)),
    )(h_f, h_b, o_f, o_b, z, z, head_norm.array)


def _mix_epilogue_bwd(dymix, h_f, h_b, o_f, o_b, z, head_norm):
    n = z.shape[0]
    tr = ROW_BLOCK
    spec = pl.BlockSpec((tr, 512), lambda i: (i, 0))

    def body(dyl_ref, dyg_ref, hf_ref, hb_ref, of_ref, ob_ref, gate_ref, gout_ref, w_ref,
             dhs_ref, dgate_ref, do_ref, dgout_ref, dw_ref):
        i = pl.program_id(0)
        dyl = dyl_ref[...]
        gelu, dgelu = _gelu_and_grad(gate_ref[...])
        dhs_ref[...] = dyl * gelu
        dgate_ref[...] = dyl * (hf_ref[...] + hb_ref[...]) * dgelu
        dyg = dyg_ref[...]
        o = of_ref[...] + ob_ref[...]
        gout = gout_ref[...]
        sg = _sigmoid(gout)
        silu = gout * sg
        dsilu = sg * (1.0 + gout * (1.0 - sg))
        w = w_ref[...]
        dws = []
        for hd in range(GLA_HEADS):
            cs = slice(hd * GLA_DV, (hd + 1) * GLA_DV)
            oh = o[:, cs]
            rs = lax.rsqrt(jnp.mean(oh * oh, axis=-1, keepdims=True) + EPS)
            on = oh * rs
            dy = dyg[:, cs]
            dgout_ref[:, cs] = dy * on * w[:, cs] * dsilu[:, cs]
            dys = dy * silu[:, cs]
            dws.append(jnp.sum(dys * on, axis=0, keepdims=True))
            don = dys * w[:, cs]
            do_ref[:, cs] = rs * (don - on * jnp.mean(don * on, axis=-1, keepdims=True))
        _accumulate(dw_ref, jnp.concatenate(dws, axis=1), i == 0)

    return pl.pallas_call(
        body, name="mix_epilogue_bwd", grid=(n // tr,),
        in_specs=[pl.BlockSpec((tr, 512), lambda i: (i, 0)), pl.BlockSpec((tr, 512), lambda i: (i, 1)),
                  spec, spec, spec, spec, pl.BlockSpec((tr, 512), lambda i: (i, 1)),
                  pl.BlockSpec((tr, 512), lambda i: (i, 4)), head_norm.spec],
        out_specs=[spec, spec, spec, spec, pl.BlockSpec((1, GLA_W), lambda i: (0, 0))],
        out_shape=[_sds((n, 512), F32)] * 4 + [_sds((1, GLA_W), F32)],
        compiler_params=_params(("arbitrary",)),
    )(dymix, dymix, h_f, h_b, o_f, o_b, z, z, head_norm.array)


def _out_proj(ymix, w_out, h, g):
    n, d = h.shape
    tr = ROW_BLOCK
    spec = pl.BlockSpec((tr, d), lambda i: (i, 0))

    def body(y_ref, w_ref, h_ref, g_ref, mix_ref, hmid_ref):
        mix = jnp.dot(y_ref[...], w_ref[...], preferred_element_type=F32)
        mix_ref[...] = mix
        hmid_ref[...] = h_ref[...] + _rms_fwd(mix, g_ref[...])

    return pl.pallas_call(
        body, name="out_proj", grid=(n // tr,),
        in_specs=[spec, VMEM_SPEC, spec, g.spec],
        out_specs=[spec, spec],
        out_shape=[_sds((n, d), F32), _sds((n, d), F32)],
        compiler_params=_params(("parallel",), 32),
    )(ymix, w_out, h, g.array)


def _out_proj_bwd(dh_mid, mix, g, w_out):
    n, d = mix.shape
    tr = ROW_BLOCK
    spec = pl.BlockSpec((tr, d), lambda i: (i, 0))

    def body(dh_ref, mix_ref, g_ref, w_ref, dmix_ref, dy_ref, dg_ref):
        i = pl.program_id(0)
        dmix, dg = _rms_bwd(mix_ref[...], g_ref[...], dh_ref[...])
        dmix = dmix.astype(BF16)
        dmix_ref[...] = dmix
        dy_ref[...] = _dot_nt(dmix, w_ref[...])
        _accumulate(dg_ref, dg, i == 0)

    return pl.pallas_call(
        body, name="out_proj_bwd", grid=(n // tr,),
        in_specs=[spec, spec, g.spec, VMEM_SPEC],
        out_specs=[spec, spec, pl.BlockSpec((1, d), lambda i: (0, 0))],
        out_shape=[_sds((n, d), BF16), _sds((n, d), F32), _sds((1, d), F32)],
        compiler_params=_params(("arbitrary",), 32),
    )(dh_mid, mix, g.array, w_out)


FF_SLAB = D_FF // N_DEV


def _mlp_fwd(h_mid, g_pre, w_up, w_down, g_post, exchange=None):
    n, d = h_mid.shape
    tr = ROW_BLOCK
    spec = pl.BlockSpec((tr, d), lambda i: (i, 0))

    def body(h_ref, gpre_ref, wup_ref, wdn_ref, gpost_ref, hn_ref, up_ref, ff_ref, hout_ref):
        h = h_ref[...]
        hn = _rms_fwd(h, gpre_ref[...]).astype(BF16)
        hn_ref[...] = hn
        ff = jnp.zeros((tr, d), F32)
        for j in range(N_DEV):
            cs = slice(j * FF_SLAB, (j + 1) * FF_SLAB)
            up = jnp.dot(hn, wup_ref[j], preferred_element_type=F32)
            up_ref[:, cs] = up
            act = jnp.square(jnp.maximum(up, 0.0)).astype(BF16)
            ff = ff + jnp.dot(act, wdn_ref[cs, :], preferred_element_type=F32)
        ff_ref[...] = ff
        hout_ref[...] = h + _rms_fwd(ff, gpost_ref[...])

    return _hosting_call(
        exchange, body, name="mlp_fwd", grid=(n // tr,),
        in_specs=[spec, g_pre.spec, VMEM_SPEC, VMEM_SPEC, g_post.spec],
        out_specs=[spec, pl.BlockSpec((tr, D_FF), lambda i: (i, 0)), spec, spec],
        out_shape=[_sds((n, d), BF16), _sds((n, D_FF), F32), _sds((n, d), F32), _sds((n, d), F32)],
        scratch_shapes=[], compiler_params=_params(("arbitrary",), 52),
    )(h_mid, g_pre.array, w_up, w_down, g_post.array)


def _mlp_bwd(dh, ff, up, h_mid, g_pre, w_up, w_down, g_post, exchange=None):
    n, d = h_mid.shape
    tr = ROW_BLOCK
    spec = pl.BlockSpec((tr, d), lambda i: (i, 0))
    wide = pl.BlockSpec((tr, D_FF), lambda i: (i, 0))
    gspec = pl.BlockSpec((1, d), lambda i: (0, 0))

    def body(dh_ref, ff_ref, up_ref, h_ref, gpre_ref, wup_ref, wdn_ref, gpost_ref,
             dff_ref, dup_ref, act_ref, dhmid_ref, dgpost_ref, dgpre_ref):
        i = pl.program_id(0)
        dh = dh_ref[...]
        dff, dgpost = _rms_bwd(ff_ref[...], gpost_ref[...], dh)
        dff = dff.astype(BF16)
        dff_ref[...] = dff
        dhn = jnp.zeros((tr, d), F32)
        for j in range(N_DEV):
            cs = slice(j * FF_SLAB, (j + 1) * FF_SLAB)
            relu = jnp.maximum(up_ref[:, cs], 0.0)
            act_ref[:, cs] = jnp.square(relu).astype(BF16)
            dact = _dot_nt(dff, wdn_ref[cs, :])
            dup = (dact * 2.0 * relu).astype(BF16)
            dup_ref[:, cs] = dup
            dhn = dhn + _dot_nt(dup, wup_ref[j])
        dx, dgpre = _rms_bwd(h_ref[...], gpre_ref[...], dhn)
        dhmid_ref[...] = dh + dx
        _accumulate(dgpost_ref, dgpost, i == 0)
        _accumulate(dgpre_ref, dgpre, i == 0)

    return _hosting_call(
        exchange, body, name="mlp_bwd", grid=(n // tr,),
        in_specs=[spec, spec, wide, spec, g_pre.spec, VMEM_SPEC, VMEM_SPEC, g_post.spec],
        out_specs=[spec, wide, wide, spec, gspec, gspec],
        out_shape=[_sds((n, d), BF16), _sds((n, D_FF), BF16), _sds((n, D_FF), BF16), _sds((n, d), F32),
                   _sds((1, d), F32), _sds((1, d), F32)],
        scratch_shapes=[], compiler_params=_params(("arbitrary",), 56),
    )(dh, ff, up, h_mid, g_pre.array, w_up, w_down, g_post.array)


def _in_proj_bwd(pieces, w_in, h, g, dh_mid):
    dxbr, dgate, dqk_f, dqk_b, dv_f, dv_b, dgout, dzg_f, dzg_b = pieces
    n, d = h.shape
    tr = ROW_BLOCK
    spec = pl.BlockSpec((tr, d), lambda i: (i, 0))
    s512 = pl.BlockSpec((tr, 512), lambda i: (i, 0))
    s128 = pl.BlockSpec((tr, LANES), lambda i: (i, 0))

    def body(a_ref, b_ref, cf_ref, cb_ref, df_ref, db_ref, e_ref, ff_ref, fb_ref, w_ref, h_ref, g_ref, dhm_ref,
             dz_ref, dh_ref, dg_ref):
        i = pl.program_id(0)
        real = (_row_ids(tr, i) >= PAD_ROWS).astype(F32)
        dz = jnp.concatenate([a_ref[...], b_ref[...], cf_ref[...] + cb_ref[...], df_ref[...] + db_ref[...],
                              e_ref[...], ff_ref[...] + fb_ref[...]], axis=1) * real
        dz = dz.astype(BF16)
        dz_ref[...] = dz
        dhn = _dot_nt(dz, w_ref[...])
        dx, dg = _rms_bwd(h_ref[...], g_ref[...], dhn)
        dh_ref[...] = (dhm_ref[...] + dx) * real
        _accumulate(dg_ref, dg, i == 0)

    return pl.pallas_call(
        body, name="in_proj_bwd", grid=(n // tr,),
        in_specs=[s512, s512, s512, s512, s512, s512, s512, s128, s128, VMEM_SPEC, spec, g.spec, spec],
        out_specs=[pl.BlockSpec((tr, Z_W), lambda i: (i, 0)), spec, pl.BlockSpec((1, d), lambda i: (0, 0))],
        out_shape=[_sds((n, Z_W), BF16), _sds((n, d), F32), _sds((1, d), F32)],
        compiler_params=_params(("arbitrary",), 48),
    )(dxbr, dgate, dqk_f, dqk_b, dv_f, dv_b, dgout, dzg_f, dzg_b, w_in, h, g.array, dh_mid)


def _matmul_tn(a, b, name, column_slabs=False, exchange=None):
    n, m = a.shape
    k = b.shape[1]
    tr, tm = _row_tile(n), _col_tile(m)
    tk = k // N_DEV if column_slabs else _col_tile(k)
    steps = n // tr

    def body(a_ref, b_ref, o_ref, acc_ref):
        r = pl.program_id(2)
        _accumulate(acc_ref, _dot_tn(a_ref[...], b_ref[...]), r == 0)

        @pl.when(r == steps - 1)
        def _():
            o_ref[...] = acc_ref[...].astype(BF16)

    if column_slabs:
        out_spec = pl.BlockSpec((None, tm, tk), lambda mi, ki, r: (ki, mi, 0))
        out_shape = _sds((N_DEV, m, tk), BF16)
    else:
        out_spec = pl.BlockSpec((tm, tk), lambda mi, ki, r: (mi, ki))
        out_shape = _sds((m, k), BF16)
    outs = _hosting_call(
        exchange, body, name=name, grid=(m // tm, k // tk, steps),
        in_specs=[pl.BlockSpec((tr, tm), lambda mi, ki, r: (r, mi)), pl.BlockSpec((tr, tk), lambda mi, ki, r: (r, ki))],
        out_specs=[out_spec], out_shape=[out_shape], scratch_shapes=[pltpu.VMEM((tm, tk), F32)],
        compiler_params=_params(("arbitrary", "arbitrary", "arbitrary"), 40),
    )(a, b)
    return outs[0] if exchange is None else outs


def _loss_and_grad(h_out, target):
    n, d = h_out.shape
    tr = ROW_BLOCK
    first = (PAD_ROWS + N_META) // tr

    def body(h_ref, t_ref, dh_ref, loss_ref):
        i = pl.program_id(0)
        real = jnp.where(i >= first, 1.0, 0.0)
        diff = (h_ref[...] - t_ref[...]) * real
        dh_ref[...] = diff * (1.0 / d)
        part = 0.5 * jnp.sum(jnp.mean(diff * diff, axis=-1, keepdims=True), axis=0, keepdims=True)
        _accumulate(loss_ref, jnp.broadcast_to(part, (1, LANES)), i == 0)

    return pl.pallas_call(
        body, name="loss_and_grad", grid=(n // tr,),
        in_specs=[pl.BlockSpec((tr, d), lambda i: (i, 0)), pl.BlockSpec((tr, d), lambda i: (jnp.maximum(i - first, 0), 0))],
        out_specs=[pl.BlockSpec((tr, d), lambda i: (i, 0)), pl.BlockSpec((1, LANES), lambda i: (0, 0))],
        out_shape=[_sds((n, d), F32), _sds((1, LANES), F32)],
        compiler_params=_params(("arbitrary",)),
    )(h_out, target)


def _scan_rows(a, u, reverse):
    n = a.shape[0]
    row = lax.broadcasted_iota(jnp.int32, (n, 1), 0)
    d = 1
    while d < n:
        shift = n - d if reverse else d
        keep = (row < n - d) if reverse else (row >= d)
        a_s = pltpu.roll(a, shift, 0)
        u_s = pltpu.roll(u, shift, 0)
        u = jnp.where(keep, a * u_s + u, u)
        a = jnp.where(keep, a * a_s, a)
        d *= 2
    return a, u


def _lru_gates(xc, wcat_ref, bias_ref, lam_ref):
    nl = -lam_ref[...]
    nsp = -LRU_C * (jnp.maximum(nl, 0.0) + jnp.log(1.0 + jnp.exp(-jnp.abs(nl))))
    pre = _dot(xc, wcat_ref[...]) + bias_ref[...]
    r = _sigmoid(pre[:, :LRU_W])
    ig = _sigmoid(pre[:, LRU_W:])
    log_a = r * nsp
    a = jnp.exp(log_a)
    m = jnp.sqrt(_neg_expm1(2.0 * log_a))
    return r, ig, a, m, nsp


def _lru_scan(xc, wcat, bias, lam, reverse):
    n = xc.shape[0]
    nb = n // ROW_BLOCK
    order = (lambda i: nb - 1 - i) if reverse else (lambda i: i)
    spec = pl.BlockSpec((ROW_BLOCK, LRU_W), lambda i: (order(i), 0))
    edge = 0 if reverse else ROW_BLOCK - 1

    def body(xc_ref, wcat_ref, bias_ref, lam_ref, h_ref, carry_ref):
        i = pl.program_id(0)

        @pl.when(i == 0)
        def _():
            carry_ref[...] = jnp.zeros_like(carry_ref)

        xc = xc_ref[...]
        r, ig, a, m, _ = _lru_gates(xc, wcat_ref, bias_ref, lam_ref)
        u = jnp.where(_row_ids(ROW_BLOCK, order(i)) >= PAD_ROWS, m * (ig * xc), 0.0)
        big_a, big_u = _scan_rows(a, u, reverse)
        h_ref[...] = big_a * carry_ref[0:1, :] + big_u
        carry_ref[0:1, :] = h_ref[pl.ds(edge, 1), :]

    return pl.pallas_call(
        body, name="lru_scan_b" if reverse else "lru_scan_f", grid=(nb,),
        in_specs=[spec, wcat.spec, bias.spec, lam.spec],
        out_specs=spec,
        out_shape=_sds((n, LRU_W), F32),
        scratch_shapes=[pltpu.VMEM((8, LRU_W), F32)],
        compiler_params=_params(("arbitrary",)),
    )(xc, wcat.array, bias.array, lam.array)


def _lru_scan_bwd(dhs, xc, h, wcat, bias, lam, reverse):
    n = xc.shape[0]
    nb = n // ROW_BLOCK
    per = ROW_BLOCK // 8
    order = (lambda i: i) if reverse else (lambda i: nb - 1 - i)
    spec = pl.BlockSpec((ROW_BLOCK, LRU_W), lambda i: (order(i), 0))
    if reverse:
        halo = pl.BlockSpec((8, LRU_W), lambda i: (jnp.minimum((order(i) + 1) * per, nb * per - 1), 0))
    else:
        halo = pl.BlockSpec((8, LRU_W), lambda i: (jnp.maximum(order(i) * per - 1, 0), 0))
    edge = ROW_BLOCK - 1 if reverse else 0

    def body(dhs_ref, xc_ref, h_ref, halo_ref, wcat_ref, bias_ref, lam_ref,
             dxc_ref, dw_ref, db_ref, dlam_ref, cdh_ref, ca_ref, tmp_ref):
        i = pl.program_id(0)
        ib = order(i)

        @pl.when(i == 0)
        def _():
            cdh_ref[...] = jnp.zeros_like(cdh_ref)
            ca_ref[...] = jnp.zeros_like(ca_ref)

        xc = xc_ref[...]
        r, ig, a, m, nsp = _lru_gates(xc, wcat_ref, bias_ref, lam_ref)
        row = lax.broadcasted_iota(jnp.int32, (ROW_BLOCK, 1), 0)
        if reverse:
            coef = jnp.where(row == 0, ca_ref[0:1, :], pltpu.roll(a, 1, 0))
            h_nb = jnp.where(row == ROW_BLOCK - 1, halo_ref[0:1, :] * jnp.where(ib < nb - 1, 1.0, 0.0),
                             pltpu.roll(h_ref[...], ROW_BLOCK - 1, 0))
        else:
            coef = jnp.where(row == ROW_BLOCK - 1, ca_ref[0:1, :], pltpu.roll(a, ROW_BLOCK - 1, 0))
            h_nb = jnp.where(row == 0, halo_ref[7:8, :] * jnp.where(ib > 0, 1.0, 0.0), pltpu.roll(h_ref[...], 1, 0))
        big_c, big_v = _scan_rows(coef, dhs_ref[...], not reverse)
        dh = big_c * cdh_ref[0:1, :] + big_v
        tmp_ref[...] = dh
        cdh_ref[0:1, :] = tmp_ref[pl.ds(edge, 1), :]
        tmp_ref[...] = a
        ca_ref[0:1, :] = tmp_ref[pl.ds(edge, 1), :]

        du = jnp.where(_row_ids(ROW_BLOCK, ib) >= PAD_ROWS, dh, 0.0)
        da = dh * h_nb
        dm = du * (ig * xc)
        di = du * (m * xc)
        dlog_a = da * a - dm * (a * a) / m
        dr = dlog_a * nsp
        dpre = jnp.concatenate([dr * r * (1.0 - r), di * ig * (1.0 - ig)], axis=1)
        dxc_ref[...] = du * (m * ig) + _dot_nt(dpre, wcat_ref[...])
        _accumulate(dw_ref, _dot_tn(xc, dpre), i == 0)
        _accumulate(db_ref, jnp.sum(dpre, axis=0, keepdims=True), i == 0)
        _accumulate(dlam_ref, jnp.sum(dlog_a * r, axis=0, keepdims=True), i == 0)

        @pl.when(i == nb - 1)
        def _():
            dlam_ref[...] = dlam_ref[...] * (LRU_C * _sigmoid(-lam_ref[...]))

    return pl.pallas_call(
        body, name="lru_scan_bwd_b" if reverse else "lru_scan_bwd_f", grid=(nb,),
        in_specs=[spec, spec, spec, halo, wcat.spec, bias.spec, lam.spec],
        out_specs=[spec, pl.BlockSpec((LRU_W, 2 * LRU_W), lambda i: (0, 0)),
                   pl.BlockSpec((1, 2 * LRU_W), lambda i: (0, 0)), pl.BlockSpec((1, LRU_W), lambda i: (0, 0))],
        out_shape=[_sds((n, LRU_W), F32), _sds((LRU_W, 2 * LRU_W), F32), _sds((1, 2 * LRU_W), F32), _sds((1, LRU_W), F32)],
        scratch_shapes=[pltpu.VMEM((8, LRU_W), F32), pltpu.VMEM((8, LRU_W), F32), pltpu.VMEM((ROW_BLOCK, LRU_W), F32)],
        compiler_params=_params(("arbitrary",)),
    )(dhs, xc, h, h, wcat.array, bias.array, lam.array)


def _gla_masks(reverse):
    t = lax.broadcasted_iota(jnp.int32, (CHUNK, CHUNK), 0)
    s = lax.broadcasted_iota(jnp.int32, (CHUNK, CHUNK), 1)
    if reverse:
        return (s >= t).astype(F32), s > t
    return (s <= t).astype(F32), s <= t


def _gla_gate(zg, wg_ref, bg_ref):
    pre = _dot(zg, wg_ref[...]) + bg_ref[...]
    g = (jnp.minimum(pre, 0.0) - jnp.log(1.0 + jnp.exp(-jnp.abs(pre)))) * (1.0 / GATE_NORM)
    return pre, g


def _gla_decays(gc, tri):
    b = jnp.dot(tri, gc, precision=lax.Precision.HIGHEST, preferred_element_type=F32)
    b_last = jnp.sum(gc, axis=0, keepdims=True)
    return jnp.exp(b), jnp.exp(-b), jnp.exp(b_last - b), jnp.exp(b_last)


def _gla_scan(z, wg, bg, reverse, exchange=None):
    n = z.shape[0]
    nb = n // ROW_BLOCK
    cpb = ROW_BLOCK // CHUNK
    order = (lambda i: nb - 1 - i) if reverse else (lambda i: i)
    chunks = range(cpb - 1, -1, -1) if reverse else range(cpb)

    def body(qk_ref, v_ref, zg_ref, wg_ref, bg_ref, o_ref, sall_ref, s_ref):
        i = pl.program_id(0)

        @pl.when(i == 0)
        def _():
            s_ref[...] = jnp.zeros_like(s_ref)

        tri, mask = _gla_masks(reverse)
        _, g = _gla_gate(zg_ref[...], wg_ref, bg_ref)
        heads = range(GLA_HEADS)
        ks = [slice(hd * GLA_DK, (hd + 1) * GLA_DK) for hd in heads]
        vs = [slice(hd * GLA_DV, (hd + 1) * GLA_DV) for hd in heads]
        qh, kb, v, el, p, intra, kv = {}, {}, {}, {}, {}, {}, {}
        for c in chunks:
            rows = slice(c * CHUNK, (c + 1) * CHUNK)
            eb, enb, ebl, el[c] = _gla_decays(g[rows], tri)
            qk = qk_ref[rows, :]
            q_all = (qk[:, :GLA_QK] * (GLA_DK ** -0.5) * eb).astype(BF16)
            k_all = (qk[:, GLA_QK:] * enb).astype(BF16)
            kb_all = (qk[:, GLA_QK:] * ebl).astype(BF16)
            v_all = v_ref[rows, :].astype(BF16)
            for hd in heads:
                qh[c, hd], kb[c, hd], v[c, hd] = q_all[:, ks[hd]], kb_all[:, ks[hd]], v_all[:, vs[hd]]
                p[c, hd] = _dot_nt(qh[c, hd], k_all[:, ks[hd]])
        for c in chunks:
            for hd in heads:
                intra[c, hd] = _dot(jnp.where(mask, p[c, hd], 0.0), v[c, hd])
                kv[c, hd] = _dot_tn(v[c, hd], kb[c, hd])
        state = [s_ref[:, ks[hd]] for hd in heads]
        for c in chunks:
            rows = slice(c * CHUNK, (c + 1) * CHUNK)
            for hd in heads:
                sall_ref[c, :, ks[hd]] = state[hd]
                o_ref[rows, vs[hd]] = intra[c, hd] + _dot_nt(qh[c, hd], state[hd])
                state[hd] = state[hd] * el[c][:, ks[hd]] + kv[c, hd]
        for hd in heads:
            s_ref[:, ks[hd]] = state[hd]

    return _hosting_call(
        exchange, body, name="gla_scan_b" if reverse else "gla_scan_f", grid=(nb,),
        in_specs=[pl.BlockSpec((ROW_BLOCK, 512), lambda i: (order(i), 2)), pl.BlockSpec((ROW_BLOCK, 512), lambda i: (order(i), 3)),
                  pl.BlockSpec((ROW_BLOCK, LANES), lambda i: (order(i), ZG_COL_BLOCK)), wg.spec, bg.spec],
        out_specs=[pl.BlockSpec((ROW_BLOCK, GLA_W), lambda i: (order(i), 0)),
                   pl.BlockSpec((cpb, GLA_DV, GLA_QK), lambda i: (order(i), 0, 0))],
        out_shape=[_sds((n, GLA_W), F32), _sds((n // CHUNK, GLA_DV, GLA_QK), F32)],
        scratch_shapes=[pltpu.VMEM((GLA_DV, GLA_QK), F32)],
        compiler_params=_params(("arbitrary",)),
    )(z, z, z, wg.array, bg.array)


def _gla_scan_bwd(do, z, states, wg, bg, reverse, exchange=None):
    n = z.shape[0]
    nb = n // ROW_BLOCK
    cpb = ROW_BLOCK // CHUNK
    order = (lambda i: i) if reverse else (lambda i: nb - 1 - i)
    chunks = range(cpb) if reverse else range(cpb - 1, -1, -1)

    def body(do_ref, qk_ref, v_ref, zg_ref, sall_ref, wg_ref, bg_ref,
             dqk_ref, dv_ref, dzg_ref, dwg_ref, dbg_ref, ds_ref):
        i = pl.program_id(0)

        @pl.when(i == 0)
        def _():
            ds_ref[...] = jnp.zeros_like(ds_ref)

        tri, mask = _gla_masks(reverse)
        tri_t, _ = _gla_masks(not reverse)
        zg = zg_ref[...]
        pre, g = _gla_gate(zg, wg_ref, bg_ref)
        heads = range(GLA_HEADS)
        ks = [slice(hd * GLA_DK, (hd + 1) * GLA_DK) for hd in heads]
        vs = [slice(hd * GLA_DV, (hd + 1) * GLA_DV) for hd in heads]
        dec, full, qh, kh, kb, v, dout, p, dp = {}, {}, {}, {}, {}, {}, {}, {}, {}
        for c in chunks:
            rows = slice(c * CHUNK, (c + 1) * CHUNK)
            dec[c] = _gla_decays(g[rows], tri)
            eb, enb, ebl, _ = dec[c]
            qk = qk_ref[rows, :]
            q_f = qk[:, :GLA_QK] * (GLA_DK ** -0.5) * eb
            k_f = qk[:, GLA_QK:] * enb
            kb_f = qk[:, GLA_QK:] * ebl
            full[c] = (q_f, k_f, kb_f)
            q_all, k_all, kb_all = q_f.astype(BF16), k_f.astype(BF16), kb_f.astype(BF16)
            v_all, do_all = v_ref[rows, :].astype(BF16), do_ref[rows, :].astype(BF16)
            for hd in heads:
                qh[c, hd], kh[c, hd], kb[c, hd] = q_all[:, ks[hd]], k_all[:, ks[hd]], kb_all[:, ks[hd]]
                v[c, hd], dout[c, hd] = v_all[:, vs[hd]], do_all[:, vs[hd]]
                p[c, hd] = _dot_nt(qh[c, hd], kh[c, hd])
                dp[c, hd] = _dot_nt(dout[c, hd], v[c, hd])
        dv_i, dqh, dkh, dsq, state = {}, {}, {}, {}, {}
        for c in chunks:
            for hd in heads:
                pm = jnp.where(mask, p[c, hd], 0.0).astype(BF16)
                dpm = jnp.where(mask, dp[c, hd], 0.0).astype(BF16)
                state[c, hd] = sall_ref[c, :, ks[hd]]
                dv_i[c, hd] = _dot_tn(pm, dout[c, hd])
                dqh[c, hd] = _dot(dpm, kh[c, hd]) + _dot(dout[c, hd], state[c, hd])
                dkh[c, hd] = _dot_tn(dpm, qh[c, hd])
                dsq[c, hd] = _dot_tn(dout[c, hd], qh[c, hd])
        dstate = [ds_ref[:, ks[hd]] for hd in heads]
        dkb, sds = {}, {}
        for c in chunks:
            rows = slice(c * CHUNK, (c + 1) * CHUNK)
            el = dec[c][3]
            for hd in heads:
                dv_ref[rows, vs[hd]] = dv_i[c, hd] + _dot_nt(kb[c, hd], dstate[hd])
                dkb[c, hd] = _dot(v[c, hd], dstate[hd])
                sds[c, hd] = jnp.sum(state[c, hd] * dstate[hd], axis=0, keepdims=True)
                dstate[hd] = dstate[hd] * el[:, ks[hd]] + dsq[c, hd]
        for hd in heads:
            ds_ref[:, ks[hd]] = dstate[hd]
        dgs = [None] * cpb
        for c in chunks:
            rows = slice(c * CHUNK, (c + 1) * CHUNK)
            eb, enb, ebl, el = dec[c]
            q_f, k_f, kb_f = full[c]
            dqh_c = jnp.concatenate([dqh[c, hd] for hd in heads], axis=1)
            dkh_c = jnp.concatenate([dkh[c, hd] for hd in heads], axis=1)
            dkb_c = jnp.concatenate([dkb[c, hd] for hd in heads], axis=1)
            sds_c = jnp.concatenate([sds[c, hd] for hd in heads], axis=1)
            dqk_ref[rows, :] = jnp.concatenate([dqh_c * eb * (GLA_DK ** -0.5), dkh_c * enb + dkb_c * ebl], axis=1)
            dkb_kb = dkb_c * kb_f
            db = dqh_c * q_f - dkh_c * k_f - dkb_kb
            db_last = el * sds_c + jnp.sum(dkb_kb, axis=0, keepdims=True)
            dgs[c] = jnp.dot(tri_t, db, precision=lax.Precision.HIGHEST, preferred_element_type=F32) + db_last
        dg = jnp.concatenate(dgs, axis=0)
        dpre = dg * _sigmoid(-pre) * (1.0 / GATE_NORM)
        dzg_ref[...] = _dot_nt(dpre, wg_ref[...])
        _accumulate(dwg_ref, _dot_tn(zg, dpre), i == 0)
        _accumulate(dbg_ref, jnp.sum(dpre, axis=0, keepdims=True), i == 0)

    return _hosting_call(
        exchange, body, name="gla_scan_bwd_b" if reverse else "gla_scan_bwd_f", grid=(nb,),
        in_specs=[pl.BlockSpec((ROW_BLOCK, GLA_W), lambda i: (order(i), 0)),
                  pl.BlockSpec((ROW_BLOCK, 512), lambda i: (order(i), 2)), pl.BlockSpec((ROW_BLOCK, 512), lambda i: (order(i), 3)),
                  pl.BlockSpec((ROW_BLOCK, LANES), lambda i: (order(i), ZG_COL_BLOCK)),
                  pl.BlockSpec((cpb, GLA_DV, GLA_QK), lambda i: (order(i), 0, 0)), wg.spec, bg.spec],
        out_specs=[pl.BlockSpec((ROW_BLOCK, 512), lambda i: (order(i), 0)), pl.BlockSpec((ROW_BLOCK, 512), lambda i: (order(i), 0)),
                   pl.BlockSpec((ROW_BLOCK, LANES), lambda i: (order(i), 0)),
                   pl.BlockSpec((LANES, GLA_QK), lambda i: (0, 0)), pl.BlockSpec((1, GLA_QK), lambda i: (0, 0))],
        out_shape=[_sds((n, 512), F32), _sds((n, 512), F32), _sds((n, LANES), F32), _sds((LANES, GLA_QK), F32),
                   _sds((1, GLA_QK), F32)],
        scratch_shapes=[pltpu.VMEM((GLA_DV, GLA_QK), F32)],
        compiler_params=_params(("arbitrary",)),
    )(do, z, z, z, states, wg.array, bg.array)


NORM_NAMES = ("norm_mix_pre", "norm_mix_post", "norm_mlp_pre", "norm_mlp_post")
VEC512_NAMES = ("conv_b", "lru_ba_f", "lru_bx_f", "lru_lambda_f", "lru_ba_b", "lru_bx_b", "lru_lambda_b", "gla_head_norm")
VEC256_NAMES = ("gla_bg_f", "gla_bg_b")
LRU_MAT_NAMES = ("lru_wa_f", "lru_wx_f", "lru_wa_b", "lru_wx_b")
DIRS = ("f", "b")


def _prepare_params(w, gathered, depth):
    row_names = NORM_NAMES + ("conv_b", "gla_head_norm")
    ins = ([w[nm] for nm in row_names] + [w["lru_ba_" + d] for d in DIRS] + [w["lru_bx_" + d] for d in DIRS]
           + [w["lru_lambda_" + d] for d in DIRS] + [w["gla_bg_" + d] for d in DIRS]
           + [w["lru_wa_" + d].reshape(depth, LRU_W, LRU_HD) for d in DIRS]
           + [w["lru_wx_" + d].reshape(depth, LRU_W, LRU_HD) for d in DIRS]
           + [gathered["conv_w"], gathered["gla_wg_f"], gathered["gla_wg_b"], gathered["meta_tokens"]])
    n_rows = len(row_names)

    def body(*refs):
        rows_in = refs[:n_rows]
        ba, bx, lam, bg, wa, wx = (refs[n_rows + 2 * t:n_rows + 2 * t + 2] for t in range(6))
        convw_g, wgf_g, wgb_g, meta_g = refs[n_rows + 12:n_rows + 16]
        outs = refs[n_rows + 16:]
        rows_out = outs[:n_rows]
        convw, wcat, bias, lam_o, wg, bg_o, meta = outs[n_rows:]
        for l in range(depth):
            for src, dst in zip(rows_in, rows_out):
                dst[l] = src[pl.ds(l, 1), :]
            convw[l] = jnp.zeros((8, LRU_W), F32)
            for j in range(N_DEV):
                convw[l, 0:4, j * 64:(j + 1) * 64] = convw_g[j, l]
            for d in range(2):
                wcat[l, d] = jnp.zeros((LRU_W, 2 * LRU_W), BF16)
                for hd in range(LRU_HEADS):
                    rs = slice(hd * LRU_HD, (hd + 1) * LRU_HD)
                    wcat[l, d, rs, hd * LRU_HD:(hd + 1) * LRU_HD] = wa[d][l, rs, :].astype(BF16)
                    wcat[l, d, rs, LRU_W + hd * LRU_HD:LRU_W + (hd + 1) * LRU_HD] = wx[d][l, rs, :].astype(BF16)
                bias[l, d, :, 0:LRU_W] = ba[d][pl.ds(l, 1), :]
                bias[l, d, :, LRU_W:2 * LRU_W] = bx[d][pl.ds(l, 1), :]
                lam_o[l, d] = lam[d][pl.ds(l, 1), :]
                bg_o[l, d] = bg[d][pl.ds(l, 1), :]
                wg[l, d] = jnp.zeros((LANES, GLA_QK), BF16)
                src = wgf_g if d == 0 else wgb_g
                for j in range(N_DEV):
                    wg[l, d, d * GLA_RANK:(d + 1) * GLA_RANK, j * 32:(j + 1) * 32] = src[j, l].astype(BF16)
        for j in range(N_DEV):
            meta[:, j * LANES:(j + 1) * LANES] = meta_g[j]

    out_shape = ([_sds((depth, 1, w[nm].shape[1]), F32) for nm in row_names]
                 + [_sds((depth, 8, LRU_W), F32), _sds((depth, 2, LRU_W, 2 * LRU_W), BF16), _sds((depth, 2, 1, 2 * LRU_W), F32),
                    _sds((depth, 2, 1, LRU_W), F32), _sds((depth, 2, LANES, GLA_QK), BF16), _sds((depth, 2, 1, GLA_QK), F32),
                    _sds((N_META, D_MODEL), F32)])
    outs = pl.pallas_call(
        body, name="prepare_params", in_specs=[VMEM_SPEC] * len(ins), out_specs=[VMEM_SPEC] * len(out_shape),
        out_shape=out_shape, compiler_params=_params(None, 32),
    )(*ins)
    prepared = dict(zip(row_names, outs[:n_rows]))
    prepared.update(zip(("conv_w", "wcat", "lru_bias", "lru_lam", "wg", "gla_bg", "meta_tokens"), outs[n_rows:]))
    return prepared


class _WeightGather:
    def __init__(self, shards, p):
        self.shards, self.p = shards, p

    def exchange(self, items):
        ex = _Exchange()
        for nm, l in items:
            ex.add(self.shards[nm], _layer_of(l), _sds((N_DEV,) + self.shards[nm].shape[1:], BF16), _slab)
        return ex

    def install(self, items, landed):
        for (nm, l), g in zip(items, landed):
            if nm == "w_in":
                g = jnp.pad(jnp.concatenate([g[j] for j in range(N_DEV)], axis=1), ((0, 0), (0, Z_W - D_IN)))
            elif nm == "w_out":
                g = g.reshape(D_MODEL, D_MODEL)
            elif nm == "w_mlp_down":
                g = g.reshape(D_FF, D_MODEL)
            self.p.setdefault(nm, {})[l] = g


class _GradOutbox:
    def __init__(self):
        self.pending, self.received = [], {}

    def put(self, nm, l, slabs):
        self.pending.append((nm, l, slabs))

    def exchange(self, ex=None):
        ex = ex or _Exchange()
        keys = []
        for nm, l, slabs in self.pending:
            ex.add(slabs, _slab, _sds(slabs.shape, slabs.dtype), _slab)
            keys.append((nm, l))
        self.pending = []
        return ex, keys

    def store(self, keys, landed):
        self.received.update(zip(keys, landed))


def _layer_fwd(h, l, p, gather, depth):
    lp = lambda name, *index: _LayerParam(p[name], l, *index)
    hn, z = _norm_in_proj(h, lp("norm_mix_pre"), p["w_in"][l])
    xc = _conv_fwd(z, lp("conv_w"), lp("conv_b"))
    s = dict(h=h, hn=hn, z=z, xc=xc)
    for d, name in enumerate(DIRS):
        s["h_" + name] = _lru_scan(xc, lp("wcat", d), lp("lru_bias", d), lp("lru_lam", d), d == 1)
        items = [(("w_mlp_up", "w_mlp_down")[d], l)]
        s["o_" + name], s["s_" + name], *landed = _gla_scan(z, lp("wg", d), lp("gla_bg", d), d == 1, gather.exchange(items))
        gather.install(items, landed)
    s["ymix"] = _mix_epilogue(s["h_f"], s["h_b"], s["o_f"], s["o_b"], z, lp("gla_head_norm"))
    s["mix"], s["h_mid"] = _out_proj(s["ymix"], p["w_out"][l], h, lp("norm_mix_post"))
    items = [("w_in", l + 1), ("w_out", l + 1)] if l + 1 < depth else []
    s["hn2"], s["up"], s["ff"], h_out, *landed = _mlp_fwd(
        s["h_mid"], lp("norm_mlp_pre"), p["w_mlp_up"][l], p["w_mlp_down"][l], lp("norm_mlp_post"), gather.exchange(items))
    gather.install(items, landed)
    return h_out, s


def _layer_bwd(dh_out, l, p, s, outbox):
    lp = lambda name, *index: _LayerParam(p[name], l, *index)
    g = {}
    ex, keys = outbox.exchange()
    d_ff, dup, act, dh_mid, g["norm_mlp_post"], g["norm_mlp_pre"], *landed = _mlp_bwd(
        dh_out, s["ff"], s["up"], s["h_mid"], lp("norm_mlp_pre"), p["w_mlp_up"][l], p["w_mlp_down"][l], lp("norm_mlp_post"), ex)
    outbox.store(keys, landed)
    outbox.put("w_mlp_down", l, _matmul_tn(act, d_ff, "grad_w_down").reshape(N_DEV, D_FF // N_DEV, D_MODEL))
    outbox.put("w_mlp_up", l, _matmul_tn(s["hn2"], dup, "grad_w_up", column_slabs=True))
    dmix, dymix, g["norm_mix_post"] = _out_proj_bwd(dh_mid, s["mix"], lp("norm_mix_post"), p["w_out"][l])
    grad_w_out = _matmul_tn(s["ymix"], dmix, "grad_w_out").reshape(N_DEV, D_MODEL // N_DEV, D_MODEL)
    dhs, dgate, do, dgout, g["gla_head_norm"] = _mix_epilogue_bwd(
        dymix, s["h_f"], s["h_b"], s["o_f"], s["o_b"], s["z"], lp("gla_head_norm"))
    dqk, dv, dzg, dxc = {}, {}, {}, {}
    for d, name in enumerate(DIRS):
        ex, keys = outbox.exchange()
        dqk[name], dv[name], dzg[name], g["wg_" + name], g["gla_bg_" + name], *landed = _gla_scan_bwd(
            do, s["z"], s["s_" + name], lp("wg", d), lp("gla_bg", d), d == 1, ex)
        outbox.store(keys, landed)
        if d == 0:
            outbox.put("w_out", l, grad_w_out)
        dxc[name], g["wcat_" + name], g["lru_bias_" + name], g["lru_lambda_" + name] = _lru_scan_bwd(
            dhs, s["xc"], s["h_" + name], lp("wcat", d), lp("lru_bias", d), lp("lru_lam", d), d == 1)
    dxbr, g["conv_w"], g["conv_b"] = _conv_bwd(dxc["f"], dxc["b"], s["z"], lp("conv_w"))
    dz, dh_in, g["norm_mix_pre"] = _in_proj_bwd(
        (dxbr, dgate, dqk["f"], dqk["b"], dv["f"], dv["b"], dgout, dzg["f"], dzg["b"]),
        p["w_in"][l], s["h"], lp("norm_mix_pre"), dh_mid)
    return dh_in, g, dz


def _w_in_slabs(grad_w_in):
    shard = D_IN // N_DEV
    return jnp.stack([grad_w_in[:, j * shard:(j + 1) * shard] for j in range(N_DEV)])


def _pack_small_grads(grads, dh0, depth):
    per_layer = ("norm_mix_pre", "norm_mix_post", "norm_mlp_pre", "norm_mlp_post", "conv_b", "gla_head_norm",
                 "lru_bias_f", "lru_bias_b", "lru_lambda_f", "lru_lambda_b", "gla_bg_f", "gla_bg_b",
                 "wcat_f", "wcat_b", "conv_w", "wg_f", "wg_b")
    ins = [grads[l][nm] for l in range(depth) for nm in per_layer]
    k = len(per_layer)
    meta_rows = PAD_ROWS // N_META

    def body(*refs):
        g = [dict(zip(per_layer, refs[l * k:(l + 1) * k])) for l in range(depth)]
        dh0_ref = refs[depth * k]
        norms, v512, v256, mats, convw, wgf, wgb, meta = refs[depth * k + 1:]
        v256[...] = jnp.zeros_like(v256)
        for l in range(depth):
            for p_, nm in enumerate(NORM_NAMES):
                norms[pl.ds(2 * p_ + l, 1), :] = g[l][nm][...]
            rows512 = [g[l]["conv_b"][...], g[l]["lru_bias_f"][:, 0:LRU_W], g[l]["lru_bias_f"][:, LRU_W:2 * LRU_W],
                       g[l]["lru_lambda_f"][...], g[l]["lru_bias_b"][:, 0:LRU_W], g[l]["lru_bias_b"][:, LRU_W:2 * LRU_W],
                       g[l]["lru_lambda_b"][...], g[l]["gla_head_norm"][...]]
            for p_, row in enumerate(rows512):
                v512[pl.ds(2 * p_ + l, 1), :] = row
            for p_, nm in enumerate(("gla_bg_f", "gla_bg_b")):
                v256[pl.ds(2 * p_ + l, 1), :] = g[l][nm][...]
            for d, name in enumerate(DIRS):
                for hd in range(LRU_HEADS):
                    rs = slice(hd * LRU_HD, (hd + 1) * LRU_HD)
                    mats[2 * d, l, rs, :] = g[l]["wcat_" + name][rs, hd * LRU_HD:(hd + 1) * LRU_HD]
                    mats[2 * d + 1, l, rs, :] = g[l]["wcat_" + name][rs, LRU_W + hd * LRU_HD:LRU_W + (hd + 1) * LRU_HD]
            for j in range(N_DEV):
                convw[j, l] = g[l]["conv_w"][0:4, j * 64:(j + 1) * 64]
                wgf[j, l] = g[l]["wg_f"][0:GLA_RANK, j * 32:(j + 1) * 32]
                wgb[j, l] = g[l]["wg_b"][GLA_RANK:2 * GLA_RANK, j * 32:(j + 1) * 32]
        for j in range(N_DEV):
            meta[j] = dh0_ref[:, j * LANES:(j + 1) * LANES]

    out_shape = [_sds((8, D_MODEL), F32), _sds((16, LRU_W), F32), _sds((8, GLA_QK), F32), _sds((4, depth, LRU_W, LRU_HD), F32),
                 _sds((N_DEV, depth, 4, 64), F32), _sds((N_DEV, depth, GLA_RANK, 32), F32), _sds((N_DEV, depth, GLA_RANK, 32), F32),
                 _sds((N_DEV, N_META, LANES), F32)]
    return pl.pallas_call(
        body, name="pack_small_grads", grid=(1,),
        in_specs=[VMEM_SPEC] * (depth * k) + [pl.BlockSpec((N_META, D_MODEL), lambda i: (meta_rows, 0))],
        out_specs=[VMEM_SPEC] * len(out_shape), out_shape=out_shape, compiler_params=_params(("arbitrary",), 32),
    )(*ins, dh0)


def _my_index():
    return 4 * lax.axis_index("x") + 2 * lax.axis_index("y") + lax.axis_index("c")


def _peer(k):
    x, y, c = lax.axis_index("x"), lax.axis_index("y"), lax.axis_index("c")
    px = x ^ ((k >> 2) & 1)
    py = y ^ ((k >> 1) & 1)
    pc = c ^ (k & 1)
    return (px, py, pc), 4 * px + 2 * py + pc


class _Exchange:
    def __init__(self):
        self.inputs, self.out_shapes, self.transfers = [], [], []

    def add(self, array, src, out_shape, dst):
        self.transfers.append((len(self.inputs), src, len(self.out_shapes), dst))
        self.inputs.append(array)
        self.out_shapes.append(out_shape)
        return len(self.out_shapes) - 1

    def sem_shapes(self):
        nsem = len(self.transfers) * (N_DEV - 1)
        return [pltpu.SemaphoreType.DMA((nsem,)), pltpu.SemaphoreType.DMA((nsem,)),
                pltpu.SemaphoreType.DMA((len(self.transfers),))]

    def _local(self, ins, outs, sems):
        me = _my_index()
        return [pltpu.make_async_copy(src(ins[a], me), dst(outs[b], me), sems[2].at[t])
                for t, (a, src, b, dst) in enumerate(self.transfers)]

    def _remote(self, ins, outs, sems, t, k, sending):
        a, src, b, dst = self.transfers[t]
        peer, peer_index = _peer(k)
        sem = t * (N_DEV - 1) + k - 1
        return pltpu.make_async_remote_copy(
            src_ref=src(ins[a], peer_index), dst_ref=dst(outs[b], _my_index() if sending else peer_index),
            send_sem=sems[0].at[sem], recv_sem=sems[1].at[sem], device_id=peer, device_id_type=MESH_ID)

    def start(self, ins, outs, sems):
        for cp in self._local(ins, outs, sems):
            cp.start()
        for k in range(1, N_DEV):
            for t in range(len(self.transfers)):
                self._remote(ins, outs, sems, t, k, True).start()

    def wait(self, ins, outs, sems):
        for k in range(1, N_DEV):
            for t in range(len(self.transfers)):
                self._remote(ins, outs, sems, t, k, False).wait_recv()
        for k in range(1, N_DEV):
            for t in range(len(self.transfers)):
                self._remote(ins, outs, sems, t, k, True).wait_send()
        for cp in self._local(ins, outs, sems):
            cp.wait()

    def run(self, name):
        n_in, n_out = len(self.inputs), len(self.out_shapes)

        def body(*refs):
            ins, outs, sems = refs[:n_in], refs[n_in:n_in + n_out], refs[n_in + n_out:]
            self.start(ins, outs, sems)
            self.wait(ins, outs, sems)

        return pl.pallas_call(
            body, name=name, in_specs=[ANY_SPEC] * n_in, out_specs=[ANY_SPEC] * n_out, out_shape=self.out_shapes,
            scratch_shapes=self.sem_shapes(), compiler_params=pltpu.CompilerParams(has_side_effects=True),
        )(*self.inputs)


def _hosting_call(exchange, body, *, name, grid, in_specs, out_specs, out_shape, scratch_shapes, compiler_params):
    if exchange is None or not exchange.transfers:
        return pl.pallas_call(body, name=name, grid=grid, in_specs=in_specs, out_specs=out_specs, out_shape=out_shape,
                              scratch_shapes=scratch_shapes, compiler_params=compiler_params)
    n_in, n_out, n_scr = len(in_specs), len(out_specs), len(scratch_shapes)
    x_in, x_out = len(exchange.inputs), len(exchange.out_shapes)

    def hosted(*refs):
        ins, x_ins = refs[:n_in], refs[n_in:n_in + x_in]
        o0 = n_in + x_in
        outs, x_outs = refs[o0:o0 + n_out], refs[o0 + n_out:o0 + n_out + x_out]
        s0 = o0 + n_out + x_out
        scratch, sems = refs[s0:s0 + n_scr], refs[s0 + n_scr:]
        ids = [pl.program_id(a) for a in range(len(grid))]
        first = functools.reduce(jnp.logical_and, [i == 0 for i in ids])
        last = functools.reduce(jnp.logical_and, [i == g - 1 for i, g in zip(ids, grid)])

        @pl.when(first)
        def _():
            exchange.start(x_ins, x_outs, sems)

        body(*ins, *outs, *scratch)

        @pl.when(last)
        def _():
            exchange.wait(x_ins, x_outs, sems)

    call = pl.pallas_call(
        hosted, name=name, grid=grid, in_specs=list(in_specs) + [ANY_SPEC] * x_in,
        out_specs=list(out_specs) + [ANY_SPEC] * x_out, out_shape=list(out_shape) + list(exchange.out_shapes),
        scratch_shapes=list(scratch_shapes) + exchange.sem_shapes(), compiler_params=compiler_params)
    return lambda *operands: call(*operands, *exchange.inputs)


def _whole(ref, j):
    return ref


def _slab(ref, j):
    return ref.at[j]


def _layer_of(l):
    return lambda ref, j: ref.at[l]


def _slab_layer(l):
    return lambda ref, j: ref.at[j, l]


def _adamw(g, w, m, v):
    nm = ADAM_B1 * m + (1.0 - ADAM_B1) * g
    nv = ADAM_B2 * v + (1.0 - ADAM_B2) * jnp.square(g)
    m_hat = nm / (1.0 - ADAM_B1 ** ADAM_STEP)
    v_hat = nv / (1.0 - ADAM_B2 ** ADAM_STEP)
    return -ADAM_LR * (m_hat / (jnp.sqrt(v_hat) + ADAM_EPS) + ADAM_WD * w), nm, nv


def _sum_parts(p_ref):
    g = p_ref[0].astype(F32)
    for j in range(1, N_DEV):
        g = g + p_ref[j].astype(F32)
    return g


def _adamw_sharded(parts, w, m, v, name):
    shape = w.shape
    lead, (rows, cols) = shape[:-2], shape[-2:]
    tr = min(rows, ROW_BLOCK)
    assert rows % tr == 0
    steps = rows // tr
    nl = len(lead)
    spec = pl.BlockSpec((None,) * nl + (tr, cols), lambda *idx: idx + (0,))
    per_layer = isinstance(parts, (list, tuple))
    if per_layer:
        def part_spec(l):
            return pl.BlockSpec((N_DEV, tr, cols), lambda li, r: (0, jnp.where(li == l, r, jnp.where(li < l, 0, steps - 1)), 0))
        part_specs = [part_spec(l) for l in range(len(parts))]
    else:
        parts = [parts]
        part_specs = [pl.BlockSpec((N_DEV,) + (None,) * nl + (tr, cols), lambda *idx: (0,) + idx + (0,))]
    count = len(parts)

    def body(*refs):
        p_refs = refs[:count]
        w_ref, m_ref, v_ref, g_ref, d_ref, nm_ref, nv_ref = refs[count:]

        def update(p_ref):
            g = _sum_parts(p_ref)
            g_ref[...] = g
            d_ref[...], nm_ref[...], nv_ref[...] = _adamw(g, w_ref[...], m_ref[...], v_ref[...])

        if per_layer:
            for l in range(count):
                pl.when(pl.program_id(0) == l)(functools.partial(update, p_refs[l]))
        else:
            update(p_refs[0])

    return pl.pallas_call(
        body, name=name, grid=lead + (steps,),
        in_specs=part_specs + [spec, spec, spec], out_specs=[spec] * 4, out_shape=[_sds(shape, F32)] * 4,
        compiler_params=_params(("arbitrary",) * (nl + 1)),
    )(*parts, w, m, v)


def _adamw_replicated(gathered, w, m, v, depth):
    names = NORM_NAMES + VEC512_NAMES + VEC256_NAMES + LRU_MAT_NAMES
    count = len(names)

    def body(*refs):
        norms, v512, v256, mats = refs[:4]
        w_refs, m_refs, v_refs = (refs[4 + t * count:4 + (t + 1) * count] for t in range(3))
        outs = refs[4 + 3 * count:4 + 7 * count]
        sum_norms, sum_512, sum_256 = refs[4 + 7 * count:]
        sum_norms[...] = _sum_parts(norms)
        sum_512[...] = _sum_parts(v512)
        sum_256[...] = _sum_parts(v256)
        for n_, nm in enumerate(names):
            if nm in NORM_NAMES:
                g = sum_norms[pl.ds(depth * NORM_NAMES.index(nm), depth), :]
            elif nm in VEC512_NAMES:
                g = sum_512[pl.ds(depth * VEC512_NAMES.index(nm), depth), :]
            elif nm in VEC256_NAMES:
                g = sum_256[pl.ds(depth * VEC256_NAMES.index(nm), depth), :]
            else:
                p_ = LRU_MAT_NAMES.index(nm)
                g = mats[0, p_]
                for j in range(1, N_DEV):
                    g = g + mats[j, p_]
            delta, nm_, nv_ = _adamw(g, w_refs[n_][...], m_refs[n_][...], v_refs[n_][...])
            outs[n_][...] = g
            outs[count + n_][...] = delta
            outs[2 * count + n_][...] = nm_
            outs[3 * count + n_][...] = nv_

    shapes = [_sds(w[nm].shape, F32) for nm in names]
    ins = list(gathered) + [t[nm] for t in (w, m, v) for nm in names]
    outs = pl.pallas_call(
        body, name="adamw_replicated", in_specs=[VMEM_SPEC] * len(ins), out_specs=[VMEM_SPEC] * (4 * count),
        out_shape=shapes * 4,
        scratch_shapes=[pltpu.VMEM(gathered[0].shape[1:], F32), pltpu.VMEM(gathered[1].shape[1:], F32),
                        pltpu.VMEM(gathered[2].shape[1:], F32)],
        compiler_params=_params(None, 48),
    )(*ins)
    return [dict(zip(names, outs[t * count:(t + 1) * count])) for t in range(4)]


WEIGHT_NAMES = ("meta_tokens", "norm_mix_pre", "norm_mix_post", "norm_mlp_pre", "norm_mlp_post", "w_in", "conv_w", "conv_b",
                "lru_wa_f", "lru_ba_f", "lru_wx_f", "lru_bx_f", "lru_lambda_f", "lru_wa_b", "lru_ba_b", "lru_wx_b",
                "lru_bx_b", "lru_lambda_b", "gla_wg_f", "gla_bg_f", "gla_wg_b", "gla_bg_b", "gla_head_norm", "w_out",
                "w_mlp_up", "w_mlp_down")
MATMUL_WEIGHTS = ("w_in", "w_out", "w_mlp_up", "w_mlp_down")
SMALL_SHARDED = ("conv_w", "gla_wg_f", "gla_wg_b", "meta_tokens")


def kernel(x, meta_tokens, norm_mix_pre, norm_mix_post, norm_mlp_pre, norm_mlp_post, w_in, conv_w, conv_b, lru_wa_f, lru_ba_f, lru_wx_f, lru_bx_f, lru_lambda_f, lru_wa_b, lru_ba_b, lru_wx_b, lru_bx_b, lru_lambda_b, gla_wg_f, gla_bg_f, gla_wg_b, gla_bg_b, gla_head_norm, w_out, w_mlp_up, w_mlp_down, loss_target, m_meta_tokens, m_norm_mix_pre, m_norm_mix_post, m_norm_mlp_pre, m_norm_mlp_post, m_w_in, m_conv_w, m_conv_b, m_lru_wa_f, m_lru_ba_f, m_lru_wx_f, m_lru_bx_f, m_lru_lambda_f, m_lru_wa_b, m_lru_ba_b, m_lru_wx_b, m_lru_bx_b, m_lru_lambda_b, m_gla_wg_f, m_gla_bg_f, m_gla_wg_b, m_gla_bg_b, m_gla_head_norm, m_w_out, m_w_mlp_up, m_w_mlp_down, v_meta_tokens, v_norm_mix_pre, v_norm_mix_post, v_norm_mlp_pre, v_norm_mlp_post, v_w_in, v_conv_w, v_conv_b, v_lru_wa_f, v_lru_ba_f, v_lru_wx_f, v_lru_bx_f, v_lru_lambda_f, v_lru_wa_b, v_lru_ba_b, v_lru_wx_b, v_lru_bx_b, v_lru_lambda_b, v_gla_wg_f, v_gla_bg_f, v_gla_wg_b, v_gla_bg_b, v_gla_head_norm, v_w_out, v_w_mlp_up, v_w_mlp_down):
    args = locals()
    w = {nm: args[nm] for nm in WEIGHT_NAMES}
    m = {nm: args["m_" + nm] for nm in WEIGHT_NAMES}
    v = {nm: args["v_" + nm] for nm in WEIGHT_NAMES}
    depth = w_in.shape[0]

    shards = {nm: w[nm].astype(BF16) for nm in MATMUL_WEIGHTS}
    first_items = [("w_in", 0), ("w_out", 0)]
    p = {}
    gather = _WeightGather(shards, p)
    ex = gather.exchange(first_items)
    for nm in SMALL_SHARDED:
        ex.add(w[nm], _whole, _sds((N_DEV,) + w[nm].shape, F32), _slab)
    landed = ex.run("all_gather")
    gather.install(first_items, landed[:len(first_items)])
    p.update(_prepare_params(w, dict(zip(SMALL_SHARDED, landed[len(first_items):])), depth))

    h = jnp.concatenate([jnp.zeros((PAD_ROWS, D_MODEL), F32), p["meta_tokens"], x[0]], axis=0)
    saved = []
    for l in range(depth):
        h, s = _layer_fwd(h, l, p, gather, depth)
        saved.append(s)
    dh, loss_part = _loss_and_grad(h, loss_target[0])
    loss = lax.psum(loss_part[0, 0], ("x", "y", "c"))

    outbox = _GradOutbox()
    grads = [None] * depth
    for l in reversed(range(depth)):
        dh, grads[l], dz = _layer_bwd(dh, l, p, saved[l], outbox)
        if l > 0:
            outbox.put("w_in", l, _w_in_slabs(_matmul_tn(saved[l]["hn"], dz, "grad_w_in")))
    grad_x = dh[PAD_ROWS + N_META:][None]

    small = _pack_small_grads(grads, dh, depth)
    rep_bufs, small_slabs = small[:4], small[4:]
    ex, keys = outbox.exchange()
    for g in small_slabs:
        ex.add(g, _slab, _sds(g.shape, F32), _slab)
    for g in rep_bufs:
        ex.add(g, _whole, _sds((N_DEV,) + g.shape, F32), _slab)
    grad_w_in, *landed = _matmul_tn(saved[0]["hn"], dz, "grad_w_in", exchange=ex)
    outbox.store(keys, landed[:len(keys)])
    small_received = landed[len(keys):len(keys) + len(small_slabs)]
    rep_received = landed[len(keys) + len(small_slabs):]
    outbox.put("w_in", 0, _w_in_slabs(grad_w_in))
    ex, keys = outbox.exchange()
    outbox.store(keys, ex.run("exchange_grads"))

    results = [{}, {}, {}, {}]
    for nm in MATMUL_WEIGHTS:
        parts = [outbox.received[(nm, l)] for l in range(depth)]
        for t, out in enumerate(_adamw_sharded(parts, w[nm], m[nm], v[nm], "adamw_" + nm)):
            results[t][nm] = out
    for nm, parts in zip(SMALL_SHARDED, small_received):
        for t, out in enumerate(_adamw_sharded(parts, w[nm], m[nm], v[nm], "adamw_" + nm)):
            results[t][nm] = out

    def kernel_side(tree):
        return {nm: tree[nm].reshape(depth, LRU_W, LRU_HD) if nm in LRU_MAT_NAMES else tree[nm]
                for nm in NORM_NAMES + VEC512_NAMES + VEC256_NAMES + LRU_MAT_NAMES}

    for t, tree in enumerate(_adamw_replicated(rep_received, kernel_side(w), kernel_side(m), kernel_side(v), depth)):
        for nm, out in tree.items():
            results[t][nm] = out.reshape(w[nm].shape)
    return (loss, grad_x, *[results[t][nm] for t in range(4) for nm in WEIGHT_NAMES])
```

```python
import functools

import jax
import jax.numpy as jnp
from jax import lax
from jax.experimental import pallas as pl
from jax.experimental.pallas import tpu as pltpu

F32 = jnp.float32
BF16 = jnp.bfloat16

N_DEV = 8
D_MODEL = 1024
N_META = 16
ROW_BLOCK = 256
PAD_ROWS = ROW_BLOCK - N_META
CHUNK = 64
LRU_W = 512
LRU_HEADS = 8
LRU_HD = 64
LRU_C = 8.0
GLA_HEADS = 4
GLA_DK = 64
GLA_DV = 128
GLA_QK = GLA_HEADS * GLA_DK
GLA_W = GLA_HEADS * GLA_DV
GLA_RANK = 16
GATE_NORM = 16.0
D_FF = 4096
D_IN = 2592
Z_W = 2688
ZG_COL_BLOCK = 2560 // 128
EPS = 1e-6
LANES = 128

ADAM_LR = 0.001
ADAM_B1 = 0.9
ADAM_B2 = 0.999
ADAM_EPS = 1e-08
ADAM_WD = 0.01
ADAM_STEP = 10
ADAM_ROWS = 512

VMEM_SPEC = pl.BlockSpec(memory_space=pltpu.VMEM)
ANY_SPEC = pl.BlockSpec(memory_space=pl.ANY)
MESH_ID = pl.DeviceIdType.MESH


def _sds(shape, dtype):
    return jax.ShapeDtypeStruct(shape, dtype)


def _params(sem=None, vmem_mb=None):
    kw = {}
    if sem is not None:
        kw["dimension_semantics"] = sem
    if vmem_mb is not None:
        kw["vmem_limit_bytes"] = vmem_mb * 2 ** 20
    return pltpu.CompilerParams(**kw)


def _row_tile(n):
    for t in (768, 512, 256):
        if n % t == 0:
            return t
    raise ValueError(n)


def _col_tile(k):
    for t in (1024, 896, 768, 640, 512, 384, 256, 128):
        if k % t == 0:
            return t
    raise ValueError(k)


def _sigmoid(x):
    return 1.0 / (1.0 + jnp.exp(-x))


def _gelu_and_grad(x):
    c = 0.7978845608028654
    inner = c * (x + 0.044715 * x * x * x)
    t = jnp.tanh(inner)
    gelu = 0.5 * x * (1.0 + t)
    dgelu = 0.5 * (1.0 + t) + 0.5 * x * (1.0 - t * t) * c * (1.0 + 3.0 * 0.044715 * x * x)
    return gelu, dgelu


def _neg_expm1(y):
    series = -y * (1.0 + y * (0.5 + y * (1.0 / 6.0 + y * (1.0 / 24.0 + y * (1.0 / 120.0 + y * (1.0 / 720.0))))))
    return jnp.where(y > -0.25, series, 1.0 - jnp.exp(y))


def _rms_fwd(x, g):
    rs = lax.rsqrt(jnp.mean(x * x, axis=-1, keepdims=True) + EPS)
    return x * rs * g


def _rms_bwd(x, g, dy):
    rs = lax.rsqrt(jnp.mean(x * x, axis=-1, keepdims=True) + EPS)
    xh = x * rs
    dyg = dy * g
    dx = rs * (dyg - xh * jnp.mean(dyg * xh, axis=-1, keepdims=True))
    return dx, jnp.sum(dy * xh, axis=0, keepdims=True)


def _dot(a, b):
    return jnp.dot(a.astype(BF16), b.astype(BF16), preferred_element_type=F32)


def _dot_nt(a, b):
    return lax.dot_general(a.astype(BF16), b.astype(BF16), (((1,), (1,)), ((), ())), preferred_element_type=F32)


def _dot_tn(a, b):
    return lax.dot_general(a.astype(BF16), b.astype(BF16), (((0,), (0,)), ((), ())), preferred_element_type=F32)


class _LayerParam:
    def __init__(self, array, *index):
        self.array = array
        self.index = index

    @property
    def spec(self):
        lead = len(self.index)
        tail = self.array.shape[lead:]
        index = self.index
        return pl.BlockSpec((None,) * lead + tail, lambda *_: index + (0,) * len(tail))


def _row_ids(rows, block_index):
    return block_index * rows + lax.broadcasted_iota(jnp.int32, (rows, 1), 0)


def _accumulate(ref, value, first):
    @pl.when(first)
    def _():
        ref[...] = value

    @pl.when(jnp.logical_not(first))
    def _():
        ref[...] += value


def _norm_in_proj(h, g, w):
    n, d = h.shape
    zw = w.shape[1]
    tr = ROW_BLOCK

    def body(h_ref, g_ref, w_ref, hn_ref, z_ref):
        hn = _rms_fwd(h_ref[...], g_ref[...]).astype(BF16)
        hn_ref[...] = hn
        z_ref[...] = jnp.dot(hn, w_ref[...], preferred_element_type=F32)

    return pl.pallas_call(
        body, name="norm_in_proj", grid=(n // tr,),
        in_specs=[pl.BlockSpec((tr, d), lambda i: (i, 0)), g.spec, VMEM_SPEC],
        out_specs=[pl.BlockSpec((tr, d), lambda i: (i, 0)), pl.BlockSpec((tr, zw), lambda i: (i, 0))],
        out_shape=[_sds((n, d), BF16), _sds((n, zw), F32)],
        compiler_params=_params(("parallel",), 48),
    )(h, g.array, w)


def _halo_specs(width, nb, col=0):
    per = ROW_BLOCK // 8
    prev = pl.BlockSpec((8, width), lambda i: (jnp.maximum(i * per - 1, 0), col))
    nxt = pl.BlockSpec((8, width), lambda i: (jnp.minimum((i + 1) * per, nb * per - 1), col))
    return prev, nxt


def _shift_down(x, prev8, d):
    n = x.shape[0]
    r = pltpu.roll(x, d, 0)
    p = pltpu.roll(prev8, d, 0)
    row8 = lax.broadcasted_iota(jnp.int32, (8, 1), 0)
    head = jnp.where(row8 < d, p, r[0:8])
    return jnp.concatenate([head, r[8:]], axis=0)


def _shift_up(x, next8, d):
    n = x.shape[0]
    r = pltpu.roll(x, n - d, 0)
    q = pltpu.roll(next8, 8 - d, 0)
    row8 = lax.broadcasted_iota(jnp.int32, (8, 1), 0)
    tail = jnp.where(row8 >= 8 - d, q, r[n - 8:])
    return jnp.concatenate([r[:n - 8], tail], axis=0)


def _conv_fwd(z, conv_w, conv_b):
    n = z.shape[0]
    nb = n // ROW_BLOCK
    prev_spec, next_spec = _halo_specs(LRU_W, nb)

    def body(cur_ref, prev_ref, next_ref, w_ref, b_ref, xc_ref):
        i = pl.program_id(0)
        cur = cur_ref[...]
        prev8 = prev_ref[...] * jnp.where(i > 0, 1.0, 0.0)
        next8 = next_ref[...] * jnp.where(i < nb - 1, 1.0, 0.0)
        w = [w_ref[pl.ds(k, 1), :] for k in range(4)]
        xc = (w[0] * _shift_down(cur, prev8, 2) + w[1] * _shift_down(cur, prev8, 1)
              + w[2] * cur + w[3] * _shift_up(cur, next8, 1) + b_ref[...])
        xc_ref[...] = xc

    return pl.pallas_call(
        body, name="conv_fwd", grid=(nb,),
        in_specs=[pl.BlockSpec((ROW_BLOCK, LRU_W), lambda i: (i, 0)), prev_spec, next_spec, conv_w.spec, conv_b.spec],
        out_specs=pl.BlockSpec((ROW_BLOCK, LRU_W), lambda i: (i, 0)),
        out_shape=_sds((n, LRU_W), F32),
        compiler_params=_params(("parallel",)),
    )(z, z, z, conv_w.array, conv_b.array)


def _conv_bwd(dxc_f, dxc_b, z, conv_w):
    n = z.shape[0]
    nb = n // ROW_BLOCK
    prev_spec, next_spec = _halo_specs(LRU_W, nb)
    row_spec = pl.BlockSpec((ROW_BLOCK, LRU_W), lambda i: (i, 0))

    def body(df_ref, dfp_ref, dfn_ref, db_ref, dbp_ref, dbn_ref, x_ref, xp_ref, xn_ref, w_ref,
             dx_ref, dw_ref, dbias_ref):
        i = pl.program_id(0)
        has_prev = jnp.where(i > 0, 1.0, 0.0)
        has_next = jnp.where(i < nb - 1, 1.0, 0.0)
        dxc = df_ref[...] + db_ref[...]
        dprev = (dfp_ref[...] + dbp_ref[...]) * has_prev
        dnext = (dfn_ref[...] + dbn_ref[...]) * has_next
        x = x_ref[...]
        xprev = xp_ref[...] * has_prev
        xnext = xn_ref[...] * has_next
        w = [w_ref[pl.ds(k, 1), :] for k in range(4)]
        dx_ref[...] = (w[0] * _shift_up(dxc, dnext, 2) + w[1] * _shift_up(dxc, dnext, 1)
                       + w[2] * dxc + w[3] * _shift_down(dxc, dprev, 1)).astype(BF16)
        dw = jnp.concatenate([
            jnp.sum(dxc * _shift_down(x, xprev, 2), axis=0, keepdims=True),
            jnp.sum(dxc * _shift_down(x, xprev, 1), axis=0, keepdims=True),
            jnp.sum(dxc * x, axis=0, keepdims=True),
            jnp.sum(dxc * _shift_up(x, xnext, 1), axis=0, keepdims=True),
            jnp.zeros((4, LRU_W), F32)], axis=0)
        _accumulate(dw_ref, dw, i == 0)
        _accumulate(dbias_ref, jnp.sum(dxc, axis=0, keepdims=True), i == 0)

    dx, dw, dbias = pl.pallas_call(
        body, name="conv_bwd", grid=(nb,),
        in_specs=[row_spec, prev_spec, next_spec, row_spec, prev_spec, next_spec, row_spec, prev_spec, next_spec,
                  conv_w.spec],
        out_specs=[row_spec, pl.BlockSpec((8, LRU_W), lambda i: (0, 0)), pl.BlockSpec((1, LRU_W), lambda i: (0, 0))],
        out_shape=[_sds((n, LRU_W), BF16), _sds((8, LRU_W), F32), _sds((1, LRU_W), F32)],
        compiler_params=_params(("arbitrary",)),
    )(dxc_f, dxc_f, dxc_f, dxc_b, dxc_b, dxc_b, z, z, z, conv_w.array)
    return dx, dw, dbias


def _mix_epilogue(h_f, h_b, o_f, o_b, z, head_norm):
    n = z.shape[0]
    tr = ROW_BLOCK
    spec = pl.BlockSpec((tr, 512), lambda i: (i, 0))

    def body(hf_ref, hb_ref, of_ref, ob_ref, gate_ref, gout_ref, w_ref, y_ref):
        gelu, _ = _gelu_and_grad(gate_ref[...])
        y_ref[:, 0:LRU_W] = ((hf_ref[...] + hb_ref[...]) * gelu).astype(BF16)
        o = of_ref[...] + ob_ref[...]
        gout = gout_ref[...]
        silu = gout * _sigmoid(gout)
        w = w_ref[...]
        for hd in range(GLA_HEADS):
            cs = slice(hd * GLA_DV, (hd + 1) * GLA_DV)
            oh = o[:, cs]
            on = oh * lax.rsqrt(jnp.mean(oh * oh, axis=-1, keepdims=True) + EPS)
            y_ref[:, LRU_W + hd * GLA_DV:LRU_W + (hd + 1) * GLA_DV] = (on * w[:, cs] * silu[:, cs]).astype(BF16)

    return pl.pallas_call(
        body, name="mix_epilogue", grid=(n // tr,),
        in_specs=[spec, spec, spec, spec, pl.BlockSpec((tr, 512), lambda i: (i, 1)),
                  pl.BlockSpec((tr, 512), lambda i: (i, 4)), head_norm.spec],
        out_specs=pl.BlockSpec((tr, D_MODEL), lambda i: (i, 0)),
        out_shape=_sds((n, D_MODEL), BF16),
        compiler_params=_params(("parallel",)),
    )(h_f, h_b, o_f, o_b, z, z, head_norm.array)


def _mix_epilogue_bwd(dymix, h_f, h_b, o_f, o_b, z, head_norm):
    n = z.shape[0]
    tr = ROW_BLOCK
    spec = pl.BlockSpec((tr, 512), lambda i: (i, 0))

    def body(dyl_ref, dyg_ref, hf_ref, hb_ref, of_ref, ob_ref, gate_ref, gout_ref, w_ref,
             dhs_ref, dgate_ref, do_ref, dgout_ref, dw_ref):
        i = pl.program_id(0)
        dyl = dyl_ref[...]
        gelu, dgelu = _gelu_and_grad(gate_ref[...])
        dhs_ref[...] = dyl * gelu
        dgate_ref[...] = (dyl * (hf_ref[...] + hb_ref[...]) * dgelu).astype(BF16)
        dyg = dyg_ref[...]
        o = of_ref[...] + ob_ref[...]
        gout = gout_ref[...]
        sg = _sigmoid(gout)
        silu = gout * sg
        dsilu = sg * (1.0 + gout * (1.0 - sg))
        w = w_ref[...]
        dws = []
        for hd in range(GLA_HEADS):
            cs = slice(hd * GLA_DV, (hd + 1) * GLA_DV)
            oh = o[:, cs]
            rs = lax.rsqrt(jnp.mean(oh * oh, axis=-1, keepdims=True) + EPS)
            on = oh * rs
            dy = dyg[:, cs]
            dgout_ref[:, cs] = (dy * on * w[:, cs] * dsilu[:, cs]).astype(BF16)
            dys = dy * silu[:, cs]
            dws.append(jnp.sum(dys * on, axis=0, keepdims=True))
            don = dys * w[:, cs]
            do_ref[:, cs] = (rs * (don - on * jnp.mean(don * on, axis=-1, keepdims=True))).astype(BF16)
        _accumulate(dw_ref, jnp.concatenate(dws, axis=1), i == 0)

    return pl.pallas_call(
        body, name="mix_epilogue_bwd", grid=(n // tr,),
        in_specs=[pl.BlockSpec((tr, 512), lambda i: (i, 0)), pl.BlockSpec((tr, 512), lambda i: (i, 1)),
                  spec, spec, spec, spec, pl.BlockSpec((tr, 512), lambda i: (i, 1)),
                  pl.BlockSpec((tr, 512), lambda i: (i, 4)), head_norm.spec],
        out_specs=[spec, spec, spec, spec, pl.BlockSpec((1, GLA_W), lambda i: (0, 0))],
        out_shape=[_sds((n, 512), F32)] + [_sds((n, 512), BF16)] * 3 + [_sds((1, GLA_W), F32)],
        compiler_params=_params(("arbitrary",)),
    )(dymix, dymix, h_f, h_b, o_f, o_b, z, z, head_norm.array)


def _out_proj(ymix, w_out, h, g):
    n, d = h.shape
    tr = ROW_BLOCK
    spec = pl.BlockSpec((tr, d), lambda i: (i, 0))

    def body(y_ref, w_ref, h_ref, g_ref, mix_ref, hmid_ref):
        mix = jnp.dot(y_ref[...], w_ref[...], preferred_element_type=F32)
        mix_ref[...] = mix
        hmid_ref[...] = h_ref[...] + _rms_fwd(mix, g_ref[...])

    return pl.pallas_call(
        body, name="out_proj", grid=(n // tr,),
        in_specs=[spec, VMEM_SPEC, spec, g.spec],
        out_specs=[spec, spec],
        out_shape=[_sds((n, d), F32), _sds((n, d), F32)],
        compiler_params=_params(("parallel",), 32),
    )(ymix, w_out, h, g.array)


def _out_proj_bwd(dh_mid, mix, g, w_out):
    n, d = mix.shape
    tr = ROW_BLOCK
    spec = pl.BlockSpec((tr, d), lambda i: (i, 0))

    def body(dh_ref, mix_ref, g_ref, w_ref, dmix_ref, dy_ref, dg_ref):
        i = pl.program_id(0)
        dmix, dg = _rms_bwd(mix_ref[...], g_ref[...], dh_ref[...])
        dmix = dmix.astype(BF16)
        dmix_ref[...] = dmix
        dy_ref[...] = _dot_nt(dmix, w_ref[...])
        _accumulate(dg_ref, dg, i == 0)

    return pl.pallas_call(
        body, name="out_proj_bwd", grid=(n // tr,),
        in_specs=[spec, spec, g.spec, VMEM_SPEC],
        out_specs=[spec, spec, pl.BlockSpec((1, d), lambda i: (0, 0))],
        out_shape=[_sds((n, d), BF16), _sds((n, d), F32), _sds((1, d), F32)],
        compiler_params=_params(("arbitrary",), 32),
    )(dh_mid, mix, g.array, w_out)


FF_SLAB = D_FF // N_DEV


def _relu_squared(up):
    return jnp.square(jnp.maximum(up.astype(F32), 0.0)).astype(BF16)


def _mlp_fwd(h_mid, g_pre, w_up, w_down, g_post, exchange=None):
    n, d = h_mid.shape
    tr = ROW_BLOCK
    spec = pl.BlockSpec((tr, d), lambda i: (i, 0))

    def body(h_ref, gpre_ref, wup_ref, wdn_ref, gpost_ref, hn_ref, up_ref, ff_ref, hout_ref):
        h = h_ref[...]
        hn = _rms_fwd(h, gpre_ref[...]).astype(BF16)
        hn_ref[...] = hn
        ff = jnp.zeros((tr, d), F32)
        for j in range(N_DEV):
            cs = slice(j * FF_SLAB, (j + 1) * FF_SLAB)
            up = jnp.dot(hn, wup_ref[j], preferred_element_type=F32).astype(BF16)
            up_ref[:, cs] = up
            ff = ff + jnp.dot(_relu_squared(up), wdn_ref[cs, :], preferred_element_type=F32)
        ff_ref[...] = ff
        hout_ref[...] = h + _rms_fwd(ff, gpost_ref[...])

    return _hosting_call(
        exchange, body, name="mlp_fwd", grid=(n // tr,),
        in_specs=[spec, g_pre.spec, VMEM_SPEC, VMEM_SPEC, g_post.spec],
        out_specs=[spec, pl.BlockSpec((tr, D_FF), lambda i: (i, 0)), spec, spec],
        out_shape=[_sds((n, d), BF16), _sds((n, D_FF), BF16), _sds((n, d), F32), _sds((n, d), F32)],
        scratch_shapes=[], compiler_params=_params(("arbitrary",), 52),
    )(h_mid, g_pre.array, w_up, w_down, g_post.array)


def _mlp_bwd(dh, ff, up, h_mid, g_pre, w_up, w_down, g_post, exchange=None):
    n, d = h_mid.shape
    tr = ROW_BLOCK
    spec = pl.BlockSpec((tr, d), lambda i: (i, 0))
    wide = pl.BlockSpec((tr, D_FF), lambda i: (i, 0))
    gspec = pl.BlockSpec((1, d), lambda i: (0, 0))

    def body(dh_ref, ff_ref, up_ref, h_ref, gpre_ref, wup_ref, wdn_ref, gpost_ref,
             dff_ref, dup_ref, dhmid_ref, dgpost_ref, dgpre_ref):
        i = pl.program_id(0)
        dh = dh_ref[...]
        dff, dgpost = _rms_bwd(ff_ref[...], gpost_ref[...], dh)
        dff = dff.astype(BF16)
        dff_ref[...] = dff
        dhn = jnp.zeros((tr, d), F32)
        for j in range(N_DEV):
            cs = slice(j * FF_SLAB, (j + 1) * FF_SLAB)
            relu = jnp.maximum(up_ref[:, cs].astype(F32), 0.0)
            dact = _dot_nt(dff, wdn_ref[cs, :])
            dup = (dact * 2.0 * relu).astype(BF16)
            dup_ref[:, cs] = dup
            dhn = dhn + _dot_nt(dup, wup_ref[j])
        dx, dgpre = _rms_bwd(h_ref[...], gpre_ref[...], dhn)
        dhmid_ref[...] = dh + dx
        _accumulate(dgpost_ref, dgpost, i == 0)
        _accumulate(dgpre_ref, dgpre, i == 0)

    return _hosting_call(
        exchange, body, name="mlp_bwd", grid=(n // tr,),
        in_specs=[spec, spec, wide, spec, g_pre.spec, VMEM_SPEC, VMEM_SPEC, g_post.spec],
        out_specs=[spec, wide, spec, gspec, gspec],
        out_shape=[_sds((n, d), BF16), _sds((n, D_FF), BF16), _sds((n, d), F32), _sds((1, d), F32), _sds((1, d), F32)],
        scratch_shapes=[], compiler_params=_params(("arbitrary",), 56),
    )(dh, ff, up, h_mid, g_pre.array, w_up, w_down, g_post.array)


def _in_proj_bwd(pieces, w_in, h, g, dh_mid):
    dxbr, dgate, dqk_f, dqk_b, dv_f, dv_b, dgout, dzg_f, dzg_b = pieces
    n, d = h.shape
    tr = ROW_BLOCK
    spec = pl.BlockSpec((tr, d), lambda i: (i, 0))
    s512 = pl.BlockSpec((tr, 512), lambda i: (i, 0))
    s128 = pl.BlockSpec((tr, LANES), lambda i: (i, 0))

    def body(a_ref, b_ref, cf_ref, cb_ref, df_ref, db_ref, e_ref, ff_ref, fb_ref, w_ref, h_ref, g_ref, dhm_ref,
             dz_ref, dh_ref, dg_ref):
        i = pl.program_id(0)
        real = (_row_ids(tr, i) >= PAD_ROWS).astype(F32)
        f32 = lambda ref: ref[...].astype(F32)
        dz = jnp.concatenate([f32(a_ref), f32(b_ref), f32(cf_ref) + f32(cb_ref), f32(df_ref) + f32(db_ref),
                              f32(e_ref), f32(ff_ref) + f32(fb_ref)], axis=1) * real
        dz = dz.astype(BF16)
        dz_ref[...] = dz
        dhn = _dot_nt(dz, w_ref[...])
        dx, dg = _rms_bwd(h_ref[...], g_ref[...], dhn)
        dh_ref[...] = (dhm_ref[...] + dx) * real
        _accumulate(dg_ref, dg, i == 0)

    return pl.pallas_call(
        body, name="in_proj_bwd", grid=(n // tr,),
        in_specs=[s512, s512, s512, s512, s512, s512, s512, s128, s128, VMEM_SPEC, spec, g.spec, spec],
        out_specs=[pl.BlockSpec((tr, Z_W), lambda i: (i, 0)), spec, pl.BlockSpec((1, d), lambda i: (0, 0))],
        out_shape=[_sds((n, Z_W), BF16), _sds((n, d), F32), _sds((1, d), F32)],
        compiler_params=_params(("arbitrary",), 48),
    )(dxbr, dgate, dqk_f, dqk_b, dv_f, dv_b, dgout, dzg_f, dzg_b, w_in, h, g.array, dh_mid)


def _matmul_tn(a, b, name, column_slabs=False, exchange=None, a_map=None):
    n, m = a.shape
    k = b.shape[1]
    tr, tm = _row_tile(n), _col_tile(m)
    tk = k // N_DEV if column_slabs else _col_tile(k)
    steps = n // tr

    def body(a_ref, b_ref, o_ref, acc_ref):
        r = pl.program_id(2)
        a_blk = a_ref[...] if a_map is None else a_map(a_ref[...])
        _accumulate(acc_ref, _dot_tn(a_blk, b_ref[...]), r == 0)

        @pl.when(r == steps - 1)
        def _():
            o_ref[...] = acc_ref[...].astype(BF16)

    if column_slabs:
        out_spec = pl.BlockSpec((None, tm, tk), lambda mi, ki, r: (ki, mi, 0))
        out_shape = _sds((N_DEV, m, tk), BF16)
    else:
        out_spec = pl.BlockSpec((tm, tk), lambda mi, ki, r: (mi, ki))
        out_shape = _sds((m, k), BF16)
    outs = _hosting_call(
        exchange, body, name=name, grid=(m // tm, k // tk, steps),
        in_specs=[pl.BlockSpec((tr, tm), lambda mi, ki, r: (r, mi)), pl.BlockSpec((tr, tk), lambda mi, ki, r: (r, ki))],
        out_specs=[out_spec], out_shape=[out_shape], scratch_shapes=[pltpu.VMEM((tm, tk), F32)],
        compiler_params=_params(("arbitrary", "arbitrary", "arbitrary"), 40),
    )(a, b)
    return outs[0] if exchange is None else outs


def _loss_and_grad(h_out, target):
    n, d = h_out.shape
    tr = ROW_BLOCK
    first = (PAD_ROWS + N_META) // tr

    def body(h_ref, t_ref, dh_ref, loss_ref):
        i = pl.program_id(0)
        real = jnp.where(i >= first, 1.0, 0.0)
        diff = (h_ref[...] - t_ref[...]) * real
        dh_ref[...] = diff * (1.0 / d)
        part = 0.5 * jnp.sum(jnp.mean(diff * diff, axis=-1, keepdims=True), axis=0, keepdims=True)
        _accumulate(loss_ref, jnp.broadcast_to(part, (1, LANES)), i == 0)

    return pl.pallas_call(
        body, name="loss_and_grad", grid=(n // tr,),
        in_specs=[pl.BlockSpec((tr, d), lambda i: (i, 0)), pl.BlockSpec((tr, d), lambda i: (jnp.maximum(i - first, 0), 0))],
        out_specs=[pl.BlockSpec((tr, d), lambda i: (i, 0)), pl.BlockSpec((1, LANES), lambda i: (0, 0))],
        out_shape=[_sds((n, d), F32), _sds((1, LANES), F32)],
        compiler_params=_params(("arbitrary",)),
    )(h_out, target)


def _scan_rows(a, u, reverse):
    n = a.shape[0]
    row = lax.broadcasted_iota(jnp.int32, (n, 1), 0)
    d = 1
    while d < n:
        shift = n - d if reverse else d
        keep = (row < n - d) if reverse else (row >= d)
        a_s = pltpu.roll(a, shift, 0)
        u_s = pltpu.roll(u, shift, 0)
        u = jnp.where(keep, a * u_s + u, u)
        a = jnp.where(keep, a * a_s, a)
        d *= 2
    return a, u


def _lru_gates(xc, wcat_ref, bias_ref, lam_ref):
    nl = -lam_ref[...]
    nsp = -LRU_C * (jnp.maximum(nl, 0.0) + jnp.log(1.0 + jnp.exp(-jnp.abs(nl))))
    pre = _dot(xc, wcat_ref[...]) + bias_ref[...]
    r = _sigmoid(pre[:, :LRU_W])
    ig = _sigmoid(pre[:, LRU_W:])
    log_a = r * nsp
    a = jnp.exp(log_a)
    m = jnp.sqrt(_neg_expm1(2.0 * log_a))
    return r, ig, a, m, nsp


def _lru_scan(xc, wcat, bias, lam, reverse):
    n = xc.shape[0]
    nb = n // ROW_BLOCK
    order = (lambda i: nb - 1 - i) if reverse else (lambda i: i)
    spec = pl.BlockSpec((ROW_BLOCK, LRU_W), lambda i: (order(i), 0))
    edge = 0 if reverse else ROW_BLOCK - 1

    def body(xc_ref, wcat_ref, bias_ref, lam_ref, h_ref, carry_ref):
        i = pl.program_id(0)

        @pl.when(i == 0)
        def _():
            carry_ref[...] = jnp.zeros_like(carry_ref)

        xc = xc_ref[...]
        r, ig, a, m, _ = _lru_gates(xc, wcat_ref, bias_ref, lam_ref)
        u = jnp.where(_row_ids(ROW_BLOCK, order(i)) >= PAD_ROWS, m * (ig * xc), 0.0)
        big_a, big_u = _scan_rows(a, u, reverse)
        h_ref[...] = big_a * carry_ref[0:1, :] + big_u
        carry_ref[0:1, :] = h_ref[pl.ds(edge, 1), :]

    return pl.pallas_call(
        body, name="lru_scan_b" if reverse else "lru_scan_f", grid=(nb,),
        in_specs=[spec, wcat.spec, bias.spec, lam.spec],
        out_specs=spec,
        out_shape=_sds((n, LRU_W), F32),
        scratch_shapes=[pltpu.VMEM((8, LRU_W), F32)],
        compiler_params=_params(("arbitrary",)),
    )(xc, wcat.array, bias.array, lam.array)


def _lru_scan_bwd(dhs, xc, h, wcat, bias, lam, reverse):
    n = xc.shape[0]
    nb = n // ROW_BLOCK
    per = ROW_BLOCK // 8
    order = (lambda i: i) if reverse else (lambda i: nb - 1 - i)
    spec = pl.BlockSpec((ROW_BLOCK, LRU_W), lambda i: (order(i), 0))
    if reverse:
        halo = pl.BlockSpec((8, LRU_W), lambda i: (jnp.minimum((order(i) + 1) * per, nb * per - 1), 0))
    else:
        halo = pl.BlockSpec((8, LRU_W), lambda i: (jnp.maximum(order(i) * per - 1, 0), 0))
    edge = ROW_BLOCK - 1 if reverse else 0

    def body(dhs_ref, xc_ref, h_ref, halo_ref, wcat_ref, bias_ref, lam_ref,
             dxc_ref, dw_ref, db_ref, dlam_ref, cdh_ref, ca_ref, tmp_ref):
        i = pl.program_id(0)
        ib = order(i)

        @pl.when(i == 0)
        def _():
            cdh_ref[...] = jnp.zeros_like(cdh_ref)
            ca_ref[...] = jnp.zeros_like(ca_ref)

        xc = xc_ref[...]
        r, ig, a, m, nsp = _lru_gates(xc, wcat_ref, bias_ref, lam_ref)
        row = lax.broadcasted_iota(jnp.int32, (ROW_BLOCK, 1), 0)
        if reverse:
            coef = jnp.where(row == 0, ca_ref[0:1, :], pltpu.roll(a, 1, 0))
            h_nb = jnp.where(row == ROW_BLOCK - 1, halo_ref[0:1, :] * jnp.where(ib < nb - 1, 1.0, 0.0),
                             pltpu.roll(h_ref[...], ROW_BLOCK - 1, 0))
        else:
            coef = jnp.where(row == ROW_BLOCK - 1, ca_ref[0:1, :], pltpu.roll(a, ROW_BLOCK - 1, 0))
            h_nb = jnp.where(row == 0, halo_ref[7:8, :] * jnp.where(ib > 0, 1.0, 0.0), pltpu.roll(h_ref[...], 1, 0))
        big_c, big_v = _scan_rows(coef, dhs_ref[...], not reverse)
        dh = big_c * cdh_ref[0:1, :] + big_v
        tmp_ref[...] = dh
        cdh_ref[0:1, :] = tmp_ref[pl.ds(edge, 1), :]
        tmp_ref[...] = a
        ca_ref[0:1, :] = tmp_ref[pl.ds(edge, 1), :]

        du = jnp.where(_row_ids(ROW_BLOCK, ib) >= PAD_ROWS, dh, 0.0)
        da = dh * h_nb
        dm = du * (ig * xc)
        di = du * (m * xc)
        dlog_a = da * a - dm * (a * a) / m
        dr = dlog_a * nsp
        dpre = jnp.concatenate([dr * r * (1.0 - r), di * ig * (1.0 - ig)], axis=1)
        dxc_ref[...] = du * (m * ig) + _dot_nt(dpre, wcat_ref[...])
        _accumulate(dw_ref, _dot_tn(xc, dpre), i == 0)
        _accumulate(db_ref, jnp.sum(dpre, axis=0, keepdims=True), i == 0)
        _accumulate(dlam_ref, jnp.sum(dlog_a * r, axis=0, keepdims=True), i == 0)

        @pl.when(i == nb - 1)
        def _():
            dlam_ref[...] = dlam_ref[...] * (LRU_C * _sigmoid(-lam_ref[...]))

    return pl.pallas_call(
        body, name="lru_scan_bwd_b" if reverse else "lru_scan_bwd_f", grid=(nb,),
        in_specs=[spec, spec, spec, halo, wcat.spec, bias.spec, lam.spec],
        out_specs=[spec, pl.BlockSpec((LRU_W, 2 * LRU_W), lambda i: (0, 0)),
                   pl.BlockSpec((1, 2 * LRU_W), lambda i: (0, 0)), pl.BlockSpec((1, LRU_W), lambda i: (0, 0))],
        out_shape=[_sds((n, LRU_W), F32), _sds((LRU_W, 2 * LRU_W), F32), _sds((1, 2 * LRU_W), F32), _sds((1, LRU_W), F32)],
        scratch_shapes=[pltpu.VMEM((8, LRU_W), F32), pltpu.VMEM((8, LRU_W), F32), pltpu.VMEM((ROW_BLOCK, LRU_W), F32)],
        compiler_params=_params(("arbitrary",)),
    )(dhs, xc, h, h, wcat.array, bias.array, lam.array)


def _gla_masks(reverse):
    t = lax.broadcasted_iota(jnp.int32, (CHUNK, CHUNK), 0)
    s = lax.broadcasted_iota(jnp.int32, (CHUNK, CHUNK), 1)
    if reverse:
        return (s >= t).astype(F32), s > t
    return (s <= t).astype(F32), s <= t


def _gla_gate(zg, wg_ref, bg_ref):
    pre = _dot(zg, wg_ref[...]) + bg_ref[...]
    g = (jnp.minimum(pre, 0.0) - jnp.log(1.0 + jnp.exp(-jnp.abs(pre)))) * (1.0 / GATE_NORM)
    return pre, g


def _gla_decays(gc, tri):
    b = jnp.dot(tri, gc, precision=lax.Precision.HIGHEST, preferred_element_type=F32)
    b_last = jnp.sum(gc, axis=0, keepdims=True)
    return jnp.exp(b), jnp.exp(-b), jnp.exp(b_last - b), jnp.exp(b_last)


def _gla_scan(z, wg, bg, reverse, exchange=None):
    n = z.shape[0]
    nb = n // ROW_BLOCK
    cpb = ROW_BLOCK // CHUNK
    order = (lambda i: nb - 1 - i) if reverse else (lambda i: i)
    chunks = range(cpb - 1, -1, -1) if reverse else range(cpb)

    def body(qk_ref, v_ref, zg_ref, wg_ref, bg_ref, o_ref, sall_ref, s_ref):
        i = pl.program_id(0)

        @pl.when(i == 0)
        def _():
            s_ref[...] = jnp.zeros_like(s_ref)

        tri, mask = _gla_masks(reverse)
        _, g = _gla_gate(zg_ref[...], wg_ref, bg_ref)
        heads = range(GLA_HEADS)
        ks = [slice(hd * GLA_DK, (hd + 1) * GLA_DK) for hd in heads]
        vs = [slice(hd * GLA_DV, (hd + 1) * GLA_DV) for hd in heads]
        qh, kb, v, el, p, intra, kv = {}, {}, {}, {}, {}, {}, {}
        for c in chunks:
            rows = slice(c * CHUNK, (c + 1) * CHUNK)
            eb, enb, ebl, el[c] = _gla_decays(g[rows], tri)
            qk = qk_ref[rows, :]
            q_all = (qk[:, :GLA_QK] * (GLA_DK ** -0.5) * eb).astype(BF16)
            k_all = (qk[:, GLA_QK:] * enb).astype(BF16)
            kb_all = (qk[:, GLA_QK:] * ebl).astype(BF16)
            v_all = v_ref[rows, :].astype(BF16)
            for hd in heads:
                qh[c, hd], kb[c, hd], v[c, hd] = q_all[:, ks[hd]], kb_all[:, ks[hd]], v_all[:, vs[hd]]
                p[c, hd] = _dot_nt(qh[c, hd], k_all[:, ks[hd]])
        for c in chunks:
            for hd in heads:
                intra[c, hd] = _dot(jnp.where(mask, p[c, hd], 0.0), v[c, hd])
                kv[c, hd] = _dot_tn(v[c, hd], kb[c, hd])
        state = [s_ref[:, ks[hd]] for hd in heads]
        for c in chunks:
            rows = slice(c * CHUNK, (c + 1) * CHUNK)
            for hd in heads:
                sall_ref[c, :, ks[hd]] = state[hd]
                o_ref[rows, vs[hd]] = intra[c, hd] + _dot_nt(qh[c, hd], state[hd])
                state[hd] = state[hd] * el[c][:, ks[hd]] + kv[c, hd]
        for hd in heads:
            s_ref[:, ks[hd]] = state[hd]

    return _hosting_call(
        exchange, body, name="gla_scan_b" if reverse else "gla_scan_f", grid=(nb,),
        in_specs=[pl.BlockSpec((ROW_BLOCK, 512), lambda i: (order(i), 2)), pl.BlockSpec((ROW_BLOCK, 512), lambda i: (order(i), 3)),
                  pl.BlockSpec((ROW_BLOCK, LANES), lambda i: (order(i), ZG_COL_BLOCK)), wg.spec, bg.spec],
        out_specs=[pl.BlockSpec((ROW_BLOCK, GLA_W), lambda i: (order(i), 0)),
                   pl.BlockSpec((cpb, GLA_DV, GLA_QK), lambda i: (order(i), 0, 0))],
        out_shape=[_sds((n, GLA_W), F32), _sds((n // CHUNK, GLA_DV, GLA_QK), F32)],
        scratch_shapes=[pltpu.VMEM((GLA_DV, GLA_QK), F32)],
        compiler_params=_params(("arbitrary",)),
    )(z, z, z, wg.array, bg.array)


def _gla_scan_bwd(do, z, states, wg, bg, reverse, exchange=None):
    n = z.shape[0]
    nb = n // ROW_BLOCK
    cpb = ROW_BLOCK // CHUNK
    order = (lambda i: i) if reverse else (lambda i: nb - 1 - i)
    chunks = range(cpb) if reverse else range(cpb - 1, -1, -1)

    def body(do_ref, qk_ref, v_ref, zg_ref, sall_ref, wg_ref, bg_ref,
             dqk_ref, dv_ref, dzg_ref, dwg_ref, dbg_ref, ds_ref):
        i = pl.program_id(0)

        @pl.when(i == 0)
        def _():
            ds_ref[...] = jnp.zeros_like(ds_ref)

        tri, mask = _gla_masks(reverse)
        tri_t, _ = _gla_masks(not reverse)
        zg = zg_ref[...]
        pre, g = _gla_gate(zg, wg_ref, bg_ref)
        heads = range(GLA_HEADS)
        ks = [slice(hd * GLA_DK, (hd + 1) * GLA_DK) for hd in heads]
        vs = [slice(hd * GLA_DV, (hd + 1) * GLA_DV) for hd in heads]
        dec, full, qh, kh, kb, v, dout, p, dp = {}, {}, {}, {}, {}, {}, {}, {}, {}
        for c in chunks:
            rows = slice(c * CHUNK, (c + 1) * CHUNK)
            dec[c] = _gla_decays(g[rows], tri)
            eb, enb, ebl, _ = dec[c]
            qk = qk_ref[rows, :]
            q_f = qk[:, :GLA_QK] * (GLA_DK ** -0.5) * eb
            k_f = qk[:, GLA_QK:] * enb
            kb_f = qk[:, GLA_QK:] * ebl
            full[c] = (q_f, k_f, kb_f)
            q_all, k_all, kb_all = q_f.astype(BF16), k_f.astype(BF16), kb_f.astype(BF16)
            v_all, do_all = v_ref[rows, :].astype(BF16), do_ref[rows, :].astype(BF16)
            for hd in heads:
                qh[c, hd], kh[c, hd], kb[c, hd] = q_all[:, ks[hd]], k_all[:, ks[hd]], kb_all[:, ks[hd]]
                v[c, hd], dout[c, hd] = v_all[:, vs[hd]], do_all[:, vs[hd]]
                p[c, hd] = _dot_nt(qh[c, hd], kh[c, hd])
                dp[c, hd] = _dot_nt(dout[c, hd], v[c, hd])
        dv_i, dqh, dkh, dsq, state = {}, {}, {}, {}, {}
        for c in chunks:
            for hd in heads:
                pm = jnp.where(mask, p[c, hd], 0.0).astype(BF16)
                dpm = jnp.where(mask, dp[c, hd], 0.0).astype(BF16)
                state[c, hd] = sall_ref[c, :, ks[hd]]
                dv_i[c, hd] = _dot_tn(pm, dout[c, hd])
                dqh[c, hd] = _dot(dpm, kh[c, hd]) + _dot(dout[c, hd], state[c, hd])
                dkh[c, hd] = _dot_tn(dpm, qh[c, hd])
                dsq[c, hd] = _dot_tn(dout[c, hd], qh[c, hd])
        dstate = [ds_ref[:, ks[hd]] for hd in heads]
        dkb, sds = {}, {}
        for c in chunks:
            rows = slice(c * CHUNK, (c + 1) * CHUNK)
            el = dec[c][3]
            for hd in heads:
                dv_ref[rows, vs[hd]] = (dv_i[c, hd] + _dot_nt(kb[c, hd], dstate[hd])).astype(BF16)
                dkb[c, hd] = _dot(v[c, hd], dstate[hd])
                sds[c, hd] = jnp.sum(state[c, hd] * dstate[hd], axis=0, keepdims=True)
                dstate[hd] = dstate[hd] * el[:, ks[hd]] + dsq[c, hd]
        for hd in heads:
            ds_ref[:, ks[hd]] = dstate[hd]
        dgs = [None] * cpb
        for c in chunks:
            rows = slice(c * CHUNK, (c + 1) * CHUNK)
            eb, enb, ebl, el = dec[c]
            q_f, k_f, kb_f = full[c]
            dqh_c = jnp.concatenate([dqh[c, hd] for hd in heads], axis=1)
            dkh_c = jnp.concatenate([dkh[c, hd] for hd in heads], axis=1)
            dkb_c = jnp.concatenate([dkb[c, hd] for hd in heads], axis=1)
            sds_c = jnp.concatenate([sds[c, hd] for hd in heads], axis=1)
            dqk_ref[rows, :] = jnp.concatenate([dqh_c * eb * (GLA_DK ** -0.5), dkh_c * enb + dkb_c * ebl], axis=1).astype(BF16)
            dkb_kb = dkb_c * kb_f
            db = dqh_c * q_f - dkh_c * k_f - dkb_kb
            db_last = el * sds_c + jnp.sum(dkb_kb, axis=0, keepdims=True)
            dgs[c] = jnp.dot(tri_t, db, precision=lax.Precision.HIGHEST, preferred_element_type=F32) + db_last
        dg = jnp.concatenate(dgs, axis=0)
        dpre = dg * _sigmoid(-pre) * (1.0 / GATE_NORM)
        dzg_ref[...] = _dot_nt(dpre, wg_ref[...]).astype(BF16)
        _accumulate(dwg_ref, _dot_tn(zg, dpre), i == 0)
        _accumulate(dbg_ref, jnp.sum(dpre, axis=0, keepdims=True), i == 0)

    return _hosting_call(
        exchange, body, name="gla_scan_bwd_b" if reverse else "gla_scan_bwd_f", grid=(nb,),
        in_specs=[pl.BlockSpec((ROW_BLOCK, GLA_W), lambda i: (order(i), 0)),
                  pl.BlockSpec((ROW_BLOCK, 512), lambda i: (order(i), 2)), pl.BlockSpec((ROW_BLOCK, 512), lambda i: (order(i), 3)),
                  pl.BlockSpec((ROW_BLOCK, LANES), lambda i: (order(i), ZG_COL_BLOCK)),
                  pl.BlockSpec((cpb, GLA_DV, GLA_QK), lambda i: (order(i), 0, 0)), wg.spec, bg.spec],
        out_specs=[pl.BlockSpec((ROW_BLOCK, 512), lambda i: (order(i), 0)), pl.BlockSpec((ROW_BLOCK, 512), lambda i: (order(i), 0)),
                   pl.BlockSpec((ROW_BLOCK, LANES), lambda i: (order(i), 0)),
                   pl.BlockSpec((LANES, GLA_QK), lambda i: (0, 0)), pl.BlockSpec((1, GLA_QK), lambda i: (0, 0))],
        out_shape=[_sds((n, 512), BF16), _sds((n, 512), BF16), _sds((n, LANES), BF16), _sds((LANES, GLA_QK), F32),
                   _sds((1, GLA_QK), F32)],
        scratch_shapes=[pltpu.VMEM((GLA_DV, GLA_QK), F32)],
        compiler_params=_params(("arbitrary",)),
    )(do, z, z, z, states, wg.array, bg.array)


NORM_NAMES = ("norm_mix_pre", "norm_mix_post", "norm_mlp_pre", "norm_mlp_post")
VEC512_NAMES = ("conv_b", "lru_ba_f", "lru_bx_f", "lru_lambda_f", "lru_ba_b", "lru_bx_b", "lru_lambda_b", "gla_head_norm")
VEC256_NAMES = ("gla_bg_f", "gla_bg_b")
LRU_MAT_NAMES = ("lru_wa_f", "lru_wx_f", "lru_wa_b", "lru_wx_b")
DIRS = ("f", "b")


def _prepare_params(w, gathered, depth):
    row_names = NORM_NAMES + ("conv_b", "gla_head_norm")
    ins = ([w[nm] for nm in row_names] + [w["lru_ba_" + d] for d in DIRS] + [w["lru_bx_" + d] for d in DIRS]
           + [w["lru_lambda_" + d] for d in DIRS] + [w["gla_bg_" + d] for d in DIRS]
           + [w["lru_wa_" + d].reshape(depth, LRU_W, LRU_HD) for d in DIRS]
           + [w["lru_wx_" + d].reshape(depth, LRU_W, LRU_HD) for d in DIRS]
           + [gathered["conv_w"], gathered["gla_wg_f"], gathered["gla_wg_b"], gathered["meta_tokens"]])
    n_rows = len(row_names)

    def body(*refs):
        rows_in = refs[:n_rows]
        ba, bx, lam, bg, wa, wx = (refs[n_rows + 2 * t:n_rows + 2 * t + 2] for t in range(6))
        convw_g, wgf_g, wgb_g, meta_g = refs[n_rows + 12:n_rows + 16]
        outs = refs[n_rows + 16:]
        rows_out = outs[:n_rows]
        convw, wcat, bias, lam_o, wg, bg_o, meta = outs[n_rows:]
        for l in range(depth):
            for src, dst in zip(rows_in, rows_out):
                dst[l] = src[pl.ds(l, 1), :]
            convw[l] = jnp.zeros((8, LRU_W), F32)
            for j in range(N_DEV):
                convw[l, 0:4, j * 64:(j + 1) * 64] = convw_g[j, l]
            for d in range(2):
                wcat[l, d] = jnp.zeros((LRU_W, 2 * LRU_W), BF16)
                for hd in range(LRU_HEADS):
                    rs = slice(hd * LRU_HD, (hd + 1) * LRU_HD)
                    wcat[l, d, rs, hd * LRU_HD:(hd + 1) * LRU_HD] = wa[d][l, rs, :].astype(BF16)
                    wcat[l, d, rs, LRU_W + hd * LRU_HD:LRU_W + (hd + 1) * LRU_HD] = wx[d][l, rs, :].astype(BF16)
                bias[l, d, :, 0:LRU_W] = ba[d][pl.ds(l, 1), :]
                bias[l, d, :, LRU_W:2 * LRU_W] = bx[d][pl.ds(l, 1), :]
                lam_o[l, d] = lam[d][pl.ds(l, 1), :]
                bg_o[l, d] = bg[d][pl.ds(l, 1), :]
                wg[l, d] = jnp.zeros((LANES, GLA_QK), BF16)
                src = wgf_g if d == 0 else wgb_g
                for j in range(N_DEV):
                    wg[l, d, d * GLA_RANK:(d + 1) * GLA_RANK, j * 32:(j + 1) * 32] = src[j, l].astype(BF16)
        for j in range(N_DEV):
            meta[:, j * LANES:(j + 1) * LANES] = meta_g[j]

    out_shape = ([_sds((depth, 1, w[nm].shape[1]), F32) for nm in row_names]
                 + [_sds((depth, 8, LRU_W), F32), _sds((depth, 2, LRU_W, 2 * LRU_W), BF16), _sds((depth, 2, 1, 2 * LRU_W), F32),
                    _sds((depth, 2, 1, LRU_W), F32), _sds((depth, 2, LANES, GLA_QK), BF16), _sds((depth, 2, 1, GLA_QK), F32),
                    _sds((N_META, D_MODEL), F32)])
    outs = pl.pallas_call(
        body, name="prepare_params", in_specs=[VMEM_SPEC] * len(ins), out_specs=[VMEM_SPEC] * len(out_shape),
        out_shape=out_shape, compiler_params=_params(None, 32),
    )(*ins)
    prepared = dict(zip(row_names, outs[:n_rows]))
    prepared.update(zip(("conv_w", "wcat", "lru_bias", "lru_lam", "wg", "gla_bg", "meta_tokens"), outs[n_rows:]))
    return prepared


class _WeightGather:
    def __init__(self, shards, p):
        self.shards, self.p = shards, p

    def exchange(self, items):
        ex = _Exchange()
        for nm, l in items:
            ex.add(self.shards[nm], _layer_of(l), _sds((N_DEV,) + self.shards[nm].shape[1:], BF16), _slab)
        return ex

    def install(self, items, landed):
        for (nm, l), g in zip(items, landed):
            if nm == "w_in":
                g = jnp.pad(jnp.concatenate([g[j] for j in range(N_DEV)], axis=1), ((0, 0), (0, Z_W - D_IN)))
            elif nm == "w_out":
                g = g.reshape(D_MODEL, D_MODEL)
            elif nm == "w_mlp_down":
                g = g.reshape(D_FF, D_MODEL)
            self.p.setdefault(nm, {})[l] = g


class _GradOutbox:
    def __init__(self):
        self.pending, self.received = [], {}

    def put(self, nm, l, slabs):
        self.pending.append((nm, l, slabs))

    def exchange(self, only=None):
        ex, keys, rest = _Exchange(), [], []
        for nm, l, slabs in self.pending:
            if only is None or nm == only:
                ex.add(slabs, _slab, _sds(slabs.shape, slabs.dtype), _slab)
                keys.append((nm, l))
            else:
                rest.append((nm, l, slabs))
        self.pending = rest
        return ex, keys

    def store(self, keys, landed):
        self.received.update(zip(keys, landed))


def _layer_fwd(h, l, p, gather, depth):
    lp = lambda name, *index: _LayerParam(p[name], l, *index)
    hn, z = _norm_in_proj(h, lp("norm_mix_pre"), p["w_in"][l])
    xc = _conv_fwd(z, lp("conv_w"), lp("conv_b"))
    s = dict(h=h, hn=hn, z=z, xc=xc)
    for d, name in enumerate(DIRS):
        s["h_" + name] = _lru_scan(xc, lp("wcat", d), lp("lru_bias", d), lp("lru_lam", d), d == 1)
        items = [(("w_mlp_up", "w_mlp_down")[d], l)]
        s["o_" + name], s["s_" + name], *landed = _gla_scan(z, lp("wg", d), lp("gla_bg", d), d == 1, gather.exchange(items))
        gather.install(items, landed)
    s["ymix"] = _mix_epilogue(s["h_f"], s["h_b"], s["o_f"], s["o_b"], z, lp("gla_head_norm"))
    s["mix"], s["h_mid"] = _out_proj(s["ymix"], p["w_out"][l], h, lp("norm_mix_post"))
    items = [("w_in", l + 1), ("w_out", l + 1)] if l + 1 < depth else []
    s["hn2"], s["up"], s["ff"], h_out, *landed = _mlp_fwd(
        s["h_mid"], lp("norm_mlp_pre"), p["w_mlp_up"][l], p["w_mlp_down"][l], lp("norm_mlp_post"), gather.exchange(items))
    gather.install(items, landed)
    return h_out, s


def _layer_bwd(dh_out, l, p, s, outbox):
    lp = lambda name, *index: _LayerParam(p[name], l, *index)
    g = {}
    ex, keys = outbox.exchange()
    d_ff, dup, dh_mid, g["norm_mlp_post"], g["norm_mlp_pre"], *landed = _mlp_bwd(
        dh_out, s["ff"], s["up"], s["h_mid"], lp("norm_mlp_pre"), p["w_mlp_up"][l], p["w_mlp_down"][l], lp("norm_mlp_post"), ex)
    outbox.store(keys, landed)
    outbox.put("w_mlp_down", l, _matmul_tn(s["up"], d_ff, "grad_w_down", a_map=_relu_squared)
               .reshape(N_DEV, D_FF // N_DEV, D_MODEL))
    outbox.put("w_mlp_up", l, _matmul_tn(s["hn2"], dup, "grad_w_up", column_slabs=True))
    dmix, dymix, g["norm_mix_post"] = _out_proj_bwd(dh_mid, s["mix"], lp("norm_mix_post"), p["w_out"][l])
    outbox.put("w_out", l, _matmul_tn(s["ymix"], dmix, "grad_w_out").reshape(N_DEV, D_MODEL // N_DEV, D_MODEL))
    dhs, dgate, do, dgout, g["gla_head_norm"] = _mix_epilogue_bwd(
        dymix, s["h_f"], s["h_b"], s["o_f"], s["o_b"], s["z"], lp("gla_head_norm"))
    dqk, dv, dzg, dxc = {}, {}, {}, {}
    for d, name in enumerate(DIRS):
        ex, keys = outbox.exchange(only=("w_mlp_down", "w_mlp_up")[d])
        dqk[name], dv[name], dzg[name], g["wg_" + name], g["gla_bg_" + name], *landed = _gla_scan_bwd(
            do, s["z"], s["s_" + name], lp("wg", d), lp("gla_bg", d), d == 1, ex)
        outbox.store(keys, landed)
        dxc[name], g["wcat_" + name], g["lru_bias_" + name], g["lru_lambda_" + name] = _lru_scan_bwd(
            dhs, s["xc"], s["h_" + name], lp("wcat", d), lp("lru_bias", d), lp("lru_lam", d), d == 1)
    dxbr, g["conv_w"], g["conv_b"] = _conv_bwd(dxc["f"], dxc["b"], s["z"], lp("conv_w"))
    dz, dh_in, g["norm_mix_pre"] = _in_proj_bwd(
        (dxbr, dgate, dqk["f"], dqk["b"], dv["f"], dv["b"], dgout, dzg["f"], dzg["b"]),
        p["w_in"][l], s["h"], lp("norm_mix_pre"), dh_mid)
    return dh_in, g, dz


def _w_in_slabs(grad_w_in):
    shard = D_IN // N_DEV
    return jnp.stack([grad_w_in[:, j * shard:(j + 1) * shard] for j in range(N_DEV)])


def _folded_block(hd):
    return slice((hd // 2) * LRU_HD, (hd // 2 + 1) * LRU_HD), slice((hd % 2) * LRU_HD, (hd % 2 + 1) * LRU_HD)


def _pack_small_grads(grads, dh0, depth):
    per_layer = ("norm_mix_pre", "norm_mix_post", "norm_mlp_pre", "norm_mlp_post", "conv_b", "gla_head_norm",
                 "lru_bias_f", "lru_bias_b", "lru_lambda_f", "lru_lambda_b", "gla_bg_f", "gla_bg_b",
                 "wcat_f", "wcat_b", "conv_w", "wg_f", "wg_b")
    ins = [grads[l][nm] for l in range(depth) for nm in per_layer]
    k = len(per_layer)
    meta_rows = PAD_ROWS // N_META

    def body(*refs):
        g = [dict(zip(per_layer, refs[l * k:(l + 1) * k])) for l in range(depth)]
        dh0_ref = refs[depth * k]
        norms, v512, v256, mats, convw, wgf, wgb, meta = refs[depth * k + 1:]
        v256[...] = jnp.zeros_like(v256)
        for l in range(depth):
            for p_, nm in enumerate(NORM_NAMES):
                norms[pl.ds(2 * p_ + l, 1), :] = g[l][nm][...]
            rows512 = [g[l]["conv_b"][...], g[l]["lru_bias_f"][:, 0:LRU_W], g[l]["lru_bias_f"][:, LRU_W:2 * LRU_W],
                       g[l]["lru_lambda_f"][...], g[l]["lru_bias_b"][:, 0:LRU_W], g[l]["lru_bias_b"][:, LRU_W:2 * LRU_W],
                       g[l]["lru_lambda_b"][...], g[l]["gla_head_norm"][...]]
            for p_, row in enumerate(rows512):
                v512[pl.ds(2 * p_ + l, 1), :] = row
            for p_, nm in enumerate(("gla_bg_f", "gla_bg_b")):
                v256[pl.ds(2 * p_ + l, 1), :] = g[l][nm][...]
            for d, name in enumerate(DIRS):
                for hd in range(LRU_HEADS):
                    rs = slice(hd * LRU_HD, (hd + 1) * LRU_HD)
                    dst_rows, dst_cols = _folded_block(hd)
                    mats[2 * d, l, dst_rows, dst_cols] = g[l]["wcat_" + name][rs, hd * LRU_HD:(hd + 1) * LRU_HD].astype(BF16)
                    mats[2 * d + 1, l, dst_rows, dst_cols] = (
                        g[l]["wcat_" + name][rs, LRU_W + hd * LRU_HD:LRU_W + (hd + 1) * LRU_HD].astype(BF16))
            for j in range(N_DEV):
                convw[j, l] = g[l]["conv_w"][0:4, j * 64:(j + 1) * 64]
                wgf[j, l] = g[l]["wg_f"][0:GLA_RANK, j * 32:(j + 1) * 32]
                wgb[j, l] = g[l]["wg_b"][GLA_RANK:2 * GLA_RANK, j * 32:(j + 1) * 32]
        for j in range(N_DEV):
            meta[j] = dh0_ref[:, j * LANES:(j + 1) * LANES]

    out_shape = [_sds((8, D_MODEL), F32), _sds((16, LRU_W), F32), _sds((8, GLA_QK), F32),
                 _sds((4, depth, LRU_W // 2, 2 * LRU_HD), BF16),
                 _sds((N_DEV, depth, 4, 64), F32), _sds((N_DEV, depth, GLA_RANK, 32), F32), _sds((N_DEV, depth, GLA_RANK, 32), F32),
                 _sds((N_DEV, N_META, LANES), F32)]
    return pl.pallas_call(
        body, name="pack_small_grads", grid=(1,),
        in_specs=[VMEM_SPEC] * (depth * k) + [pl.BlockSpec((N_META, D_MODEL), lambda i: (meta_rows, 0))],
        out_specs=[VMEM_SPEC] * len(out_shape), out_shape=out_shape, compiler_params=_params(("arbitrary",), 32),
    )(*ins, dh0)


def _my_index():
    return 4 * lax.axis_index("x") + 2 * lax.axis_index("y") + lax.axis_index("c")


def _peer(k):
    x, y, c = lax.axis_index("x"), lax.axis_index("y"), lax.axis_index("c")
    px = x ^ ((k >> 2) & 1)
    py = y ^ ((k >> 1) & 1)
    pc = c ^ (k & 1)
    return (px, py, pc), 4 * px + 2 * py + pc


class _Exchange:
    def __init__(self):
        self.inputs, self.out_shapes, self.transfers = [], [], []

    def add(self, array, src, out_shape, dst):
        self.transfers.append((len(self.inputs), src, len(self.out_shapes), dst))
        self.inputs.append(array)
        self.out_shapes.append(out_shape)
        return len(self.out_shapes) - 1

    def sem_shapes(self):
        nsem = len(self.transfers) * (N_DEV - 1)
        return [pltpu.SemaphoreType.DMA((nsem,)), pltpu.SemaphoreType.DMA((nsem,)),
                pltpu.SemaphoreType.DMA((len(self.transfers),))]

    def _local(self, ins, outs, sems):
        me = _my_index()
        return [pltpu.make_async_copy(src(ins[a], me), dst(outs[b], me), sems[2].at[t])
                for t, (a, src, b, dst) in enumerate(self.transfers)]

    def _remote(self, ins, outs, sems, t, k, sending):
        a, src, b, dst = self.transfers[t]
        peer, peer_index = _peer(k)
        sem = t * (N_DEV - 1) + k - 1
        return pltpu.make_async_remote_copy(
            src_ref=src(ins[a], peer_index), dst_ref=dst(outs[b], _my_index() if sending else peer_index),
            send_sem=sems[0].at[sem], recv_sem=sems[1].at[sem], device_id=peer, device_id_type=MESH_ID)

    def start(self, ins, outs, sems):
        for cp in self._local(ins, outs, sems):
            cp.start()
        for k in range(1, N_DEV):
            for t in range(len(self.transfers)):
                self._remote(ins, outs, sems, t, k, True).start()

    def wait(self, ins, outs, sems):
        for k in range(1, N_DEV):
            for t in range(len(self.transfers)):
                self._remote(ins, outs, sems, t, k, False).wait_recv()
        for k in range(1, N_DEV):
            for t in range(len(self.transfers)):
                self._remote(ins, outs, sems, t, k, True).wait_send()
        for cp in self._local(ins, outs, sems):
            cp.wait()

    def run(self, name):
        n_in, n_out = len(self.inputs), len(self.out_shapes)

        def body(*refs):
            ins, outs, sems = refs[:n_in], refs[n_in:n_in + n_out], refs[n_in + n_out:]
            self.start(ins, outs, sems)
            self.wait(ins, outs, sems)

        return pl.pallas_call(
            body, name=name, in_specs=[ANY_SPEC] * n_in, out_specs=[ANY_SPEC] * n_out, out_shape=self.out_shapes,
            scratch_shapes=self.sem_shapes(), compiler_params=pltpu.CompilerParams(has_side_effects=True),
        )(*self.inputs)


def _hosting_call(exchange, body, *, name, grid, in_specs, out_specs, out_shape, scratch_shapes, compiler_params):
    if exchange is None or not exchange.transfers:
        return pl.pallas_call(body, name=name, grid=grid, in_specs=in_specs, out_specs=out_specs, out_shape=out_shape,
                              scratch_shapes=scratch_shapes, compiler_params=compiler_params)
    n_in, n_out, n_scr = len(in_specs), len(out_specs), len(scratch_shapes)
    x_in, x_out = len(exchange.inputs), len(exchange.out_shapes)

    def hosted(*refs):
        ins, x_ins = refs[:n_in], refs[n_in:n_in + x_in]
        o0 = n_in + x_in
        outs, x_outs = refs[o0:o0 + n_out], refs[o0 + n_out:o0 + n_out + x_out]
        s0 = o0 + n_out + x_out
        scratch, sems = refs[s0:s0 + n_scr], refs[s0 + n_scr:]
        ids = [pl.program_id(a) for a in range(len(grid))]
        first = functools.reduce(jnp.logical_and, [i == 0 for i in ids])
        last = functools.reduce(jnp.logical_and, [i == g - 1 for i, g in zip(ids, grid)])

        @pl.when(first)
        def _():
            exchange.start(x_ins, x_outs, sems)

        body(*ins, *outs, *scratch)

        @pl.when(last)
        def _():
            exchange.wait(x_ins, x_outs, sems)

    call = pl.pallas_call(
        hosted, name=name, grid=grid, in_specs=list(in_specs) + [ANY_SPEC] * x_in,
        out_specs=list(out_specs) + [ANY_SPEC] * x_out, out_shape=list(out_shape) + list(exchange.out_shapes),
        scratch_shapes=list(scratch_shapes) + exchange.sem_shapes(), compiler_params=compiler_params)
    return lambda *operands: call(*operands, *exchange.inputs)


def _whole(ref, j):
    return ref


def _slab(ref, j):
    return ref.at[j]


def _layer_of(l):
    return lambda ref, j: ref.at[l]


def _slab_layer(l):
    return lambda ref, j: ref.at[j, l]


def _adamw(g, w, m, v):
    nm = ADAM_B1 * m + (1.0 - ADAM_B1) * g
    nv = ADAM_B2 * v + (1.0 - ADAM_B2) * jnp.square(g)
    m_hat = nm / (1.0 - ADAM_B1 ** ADAM_STEP)
    v_hat = nv / (1.0 - ADAM_B2 ** ADAM_STEP)
    return -ADAM_LR * (m_hat / (jnp.sqrt(v_hat) + ADAM_EPS) + ADAM_WD * w), nm, nv


def _sum_parts(p_ref):
    g = p_ref[0].astype(F32)
    for j in range(1, N_DEV):
        g = g + p_ref[j].astype(F32)
    return g


def _adamw_sharded(parts, w, m, v, name):
    shape = w.shape
    lead, (rows, cols) = shape[:-2], shape[-2:]
    tr = min(rows, ROW_BLOCK)
    assert rows % tr == 0
    steps = rows // tr
    nl = len(lead)
    spec = pl.BlockSpec((None,) * nl + (tr, cols), lambda *idx: idx + (0,))
    per_layer = isinstance(parts, (list, tuple))
    if per_layer:
        def part_spec(l):
            return pl.BlockSpec((N_DEV, tr, cols), lambda li, r: (0, jnp.where(li == l, r, jnp.where(li < l, 0, steps - 1)), 0))
        part_specs = [part_spec(l) for l in range(len(parts))]
    else:
        parts = [parts]
        part_specs = [pl.BlockSpec((N_DEV,) + (None,) * nl + (tr, cols), lambda *idx: (0,) + idx + (0,))]
    count = len(parts)

    def body(*refs):
        p_refs = refs[:count]
        w_ref, m_ref, v_ref, g_ref, d_ref, nm_ref, nv_ref = refs[count:]

        def update(p_ref):
            g = _sum_parts(p_ref)
            g_ref[...] = g
            d_ref[...], nm_ref[...], nv_ref[...] = _adamw(g, w_ref[...], m_ref[...], v_ref[...])

        if per_layer:
            for l in range(count):
                pl.when(pl.program_id(0) == l)(functools.partial(update, p_refs[l]))
        else:
            update(p_refs[0])

    return pl.pallas_call(
        body, name=name, grid=lead + (steps,),
        in_specs=part_specs + [spec, spec, spec], out_specs=[spec] * 4, out_shape=[_sds(shape, F32)] * 4,
        compiler_params=_params(("arbitrary",) * (nl + 1)),
    )(*parts, w, m, v)


def _adamw_replicated(gathered, w, m, v, depth):
    names = NORM_NAMES + VEC512_NAMES + VEC256_NAMES + LRU_MAT_NAMES
    count = len(names)

    def body(*refs):
        norms, v512, v256, mats = refs[:4]
        w_refs, m_refs, v_refs = (refs[4 + t * count:4 + (t + 1) * count] for t in range(3))
        outs = refs[4 + 3 * count:4 + 7 * count]
        sum_norms, sum_512, sum_256, unfolded = refs[4 + 7 * count:]
        sum_norms[...] = _sum_parts(norms)
        sum_512[...] = _sum_parts(v512)
        sum_256[...] = _sum_parts(v256)
        for n_, nm in enumerate(names):
            if nm in NORM_NAMES:
                g = sum_norms[pl.ds(depth * NORM_NAMES.index(nm), depth), :]
            elif nm in VEC512_NAMES:
                g = sum_512[pl.ds(depth * VEC512_NAMES.index(nm), depth), :]
            elif nm in VEC256_NAMES:
                g = sum_256[pl.ds(depth * VEC256_NAMES.index(nm), depth), :]
            else:
                p_ = LRU_MAT_NAMES.index(nm)
                folded = mats[0, p_].astype(F32)
                for j in range(1, N_DEV):
                    folded = folded + mats[j, p_].astype(F32)
                for hd in range(LRU_HEADS):
                    src_rows, src_cols = _folded_block(hd)
                    unfolded[:, hd * LRU_HD:(hd + 1) * LRU_HD, :] = folded[:, src_rows, src_cols]
                g = unfolded[...]
            delta, nm_, nv_ = _adamw(g, w_refs[n_][...], m_refs[n_][...], v_refs[n_][...])
            outs[n_][...] = g
            outs[count + n_][...] = delta
            outs[2 * count + n_][...] = nm_
            outs[3 * count + n_][...] = nv_

    shapes = [_sds(w[nm].shape, F32) for nm in names]
    ins = list(gathered) + [t[nm] for t in (w, m, v) for nm in names]
    outs = pl.pallas_call(
        body, name="adamw_replicated", in_specs=[VMEM_SPEC] * len(ins), out_specs=[VMEM_SPEC] * (4 * count),
        out_shape=shapes * 4,
        scratch_shapes=[pltpu.VMEM(gathered[0].shape[1:], F32), pltpu.VMEM(gathered[1].shape[1:], F32),
                        pltpu.VMEM(gathered[2].shape[1:], F32), pltpu.VMEM((depth, LRU_W, LRU_HD), F32)],
        compiler_params=_params(None, 48),
    )(*ins)
    return [dict(zip(names, outs[t * count:(t + 1) * count])) for t in range(4)]


WEIGHT_NAMES = ("meta_tokens", "norm_mix_pre", "norm_mix_post", "norm_mlp_pre", "norm_mlp_post", "w_in", "conv_w", "conv_b",
                "lru_wa_f", "lru_ba_f", "lru_wx_f", "lru_bx_f", "lru_lambda_f", "lru_wa_b", "lru_ba_b", "lru_wx_b",
                "lru_bx_b", "lru_lambda_b", "gla_wg_f", "gla_bg_f", "gla_wg_b", "gla_bg_b", "gla_head_norm", "w_out",
                "w_mlp_up", "w_mlp_down")
MATMUL_WEIGHTS = ("w_in", "w_out", "w_mlp_up", "w_mlp_down")
SMALL_SHARDED = ("conv_w", "gla_wg_f", "gla_wg_b", "meta_tokens")


def kernel(x, meta_tokens, norm_mix_pre, norm_mix_post, norm_mlp_pre, norm_mlp_post, w_in, conv_w, conv_b, lru_wa_f, lru_ba_f, lru_wx_f, lru_bx_f, lru_lambda_f, lru_wa_b, lru_ba_b, lru_wx_b, lru_bx_b, lru_lambda_b, gla_wg_f, gla_bg_f, gla_wg_b, gla_bg_b, gla_head_norm, w_out, w_mlp_up, w_mlp_down, loss_target, m_meta_tokens, m_norm_mix_pre, m_norm_mix_post, m_norm_mlp_pre, m_norm_mlp_post, m_w_in, m_conv_w, m_conv_b, m_lru_wa_f, m_lru_ba_f, m_lru_wx_f, m_lru_bx_f, m_lru_lambda_f, m_lru_wa_b, m_lru_ba_b, m_lru_wx_b, m_lru_bx_b, m_lru_lambda_b, m_gla_wg_f, m_gla_bg_f, m_gla_wg_b, m_gla_bg_b, m_gla_head_norm, m_w_out, m_w_mlp_up, m_w_mlp_down, v_meta_tokens, v_norm_mix_pre, v_norm_mix_post, v_norm_mlp_pre, v_norm_mlp_post, v_w_in, v_conv_w, v_conv_b, v_lru_wa_f, v_lru_ba_f, v_lru_wx_f, v_lru_bx_f, v_lru_lambda_f, v_lru_wa_b, v_lru_ba_b, v_lru_wx_b, v_lru_bx_b, v_lru_lambda_b, v_gla_wg_f, v_gla_bg_f, v_gla_wg_b, v_gla_bg_b, v_gla_head_norm, v_w_out, v_w_mlp_up, v_w_mlp_down):
    args = locals()
    w = {nm: args[nm] for nm in WEIGHT_NAMES}
    m = {nm: args["m_" + nm] for nm in WEIGHT_NAMES}
    v = {nm: args["v_" + nm] for nm in WEIGHT_NAMES}
    depth = w_in.shape[0]

    shards = {nm: w[nm].astype(BF16) for nm in MATMUL_WEIGHTS}
    first_items = [("w_in", 0), ("w_out", 0)]
    p = {}
    gather = _WeightGather(shards, p)
    ex = gather.exchange(first_items)
    for nm in SMALL_SHARDED:
        ex.add(w[nm], _whole, _sds((N_DEV,) + w[nm].shape, F32), _slab)
    landed = ex.run("all_gather")
    gather.install(first_items, landed[:len(first_items)])
    p.update(_prepare_params(w, dict(zip(SMALL_SHARDED, landed[len(first_items):])), depth))

    h = jnp.concatenate([jnp.zeros((PAD_ROWS, D_MODEL), F32), p["meta_tokens"], x[0]], axis=0)
    saved = []
    for l in range(depth):
        h, s = _layer_fwd(h, l, p, gather, depth)
        saved.append(s)
    dh, loss_part = _loss_and_grad(h, loss_target[0])
    loss = lax.psum(loss_part[0, 0], ("x", "y", "c"))

    outbox = _GradOutbox()
    grads = [None] * depth
    for l in reversed(range(depth)):
        dh, grads[l], dz = _layer_bwd(dh, l, p, saved[l], outbox)
        if l > 0:
            outbox.put("w_in", l, _w_in_slabs(_matmul_tn(saved[l]["hn"], dz, "grad_w_in")))
    grad_x = dh[PAD_ROWS + N_META:][None]

    small = _pack_small_grads(grads, dh, depth)
    rep_bufs, small_slabs = small[:4], small[4:]
    ex, keys = outbox.exchange()
    for g in small_slabs:
        ex.add(g, _slab, _sds(g.shape, F32), _slab)
    for g in rep_bufs:
        ex.add(g, _whole, _sds((N_DEV,) + g.shape, g.dtype), _slab)
    grad_w_in, *landed = _matmul_tn(saved[0]["hn"], dz, "grad_w_in", exchange=ex)
    outbox.store(keys, landed[:len(keys)])
    small_received = landed[len(keys):len(keys) + len(small_slabs)]
    rep_received = landed[len(keys) + len(small_slabs):]
    outbox.put("w_in", 0, _w_in_slabs(grad_w_in))
    ex, keys = outbox.exchange()
    outbox.store(keys, ex.run("exchange_grads"))

    results = [{}, {}, {}, {}]
    for nm in MATMUL_WEIGHTS:
        parts = [outbox.received[(nm, l)] for l in range(depth)]
        for t, out in enumerate(_adamw_sharded(parts, w[nm], m[nm], v[nm], "adamw_" + nm)):
            results[t][nm] = out
    for nm, parts in zip(SMALL_SHARDED, small_received):
        for t, out in enumerate(_adamw_sharded(parts, w[nm], m[nm], v[nm], "adamw_" + nm)):
            results[t][nm] = out

    def kernel_side(tree):
        return {nm: tree[nm].reshape(depth, LRU_W, LRU_HD) if nm in LRU_MAT_NAMES else tree[nm]
                for nm in NORM_NAMES + VEC512_NAMES + VEC256_NAMES + LRU_MAT_NAMES}

    for t, tree in enumerate(_adamw_replicated(rep_received, kernel_side(w), kernel_side(m), kernel_side(v), depth)):
        for nm, out in tree.items():
            results[t][nm] = out.reshape(w[nm].shape)
    return (loss, grad_x, *[results[t][nm] for t in range(4) for nm in WEIGHT_NAMES])
```

```python
import functools

import jax
import jax.numpy as jnp
from jax import lax
from jax.experimental import pallas as pl
from jax.experimental.pallas import tpu as pltpu

F32 = jnp.float32
BF16 = jnp.bfloat16

N_DEV = 8
D_MODEL = 1024
N_META = 16
ROW_BLOCK = 256
PAD_ROWS = ROW_BLOCK - N_META
CHUNK = 64
LRU_W = 512
LRU_HEADS = 8
LRU_HD = 64
LRU_C = 8.0
GLA_HEADS = 4
GLA_DK = 64
GLA_DV = 128
GLA_QK = GLA_HEADS * GLA_DK
GLA_W = GLA_HEADS * GLA_DV
GLA_RANK = 16
GATE_NORM = 16.0
D_FF = 4096
D_IN = 2592
Z_W = 2688
ZG_COL_BLOCK = 2560 // 128
EPS = 1e-6
LANES = 128

ADAM_LR = 0.001
ADAM_B1 = 0.9
ADAM_B2 = 0.999
ADAM_EPS = 1e-08
ADAM_WD = 0.01
ADAM_STEP = 10
ADAM_ROWS = 512

VMEM_SPEC = pl.BlockSpec(memory_space=pltpu.VMEM)
ANY_SPEC = pl.BlockSpec(memory_space=pl.ANY)
MESH_ID = pl.DeviceIdType.MESH


def _sds(shape, dtype):
    return jax.ShapeDtypeStruct(shape, dtype)


def _params(sem=None, vmem_mb=None):
    kw = {}
    if sem is not None:
        kw["dimension_semantics"] = sem
    if vmem_mb is not None:
        kw["vmem_limit_bytes"] = vmem_mb * 2 ** 20
    return pltpu.CompilerParams(**kw)


def _row_tile(n, cap=768):
    for t in (768, 512, 384, 256):
        if t <= cap and n % t == 0:
            return t
    raise ValueError(n)


def _col_tile(k):
    for t in (1024, 896, 768, 640, 512, 384, 256, 128):
        if k % t == 0:
            return t
    raise ValueError(k)


def _sigmoid(x):
    return 1.0 / (1.0 + jnp.exp(-x))


def _gelu_and_grad(x):
    c = 0.7978845608028654
    inner = c * (x + 0.044715 * x * x * x)
    t = jnp.tanh(inner)
    gelu = 0.5 * x * (1.0 + t)
    dgelu = 0.5 * (1.0 + t) + 0.5 * x * (1.0 - t * t) * c * (1.0 + 3.0 * 0.044715 * x * x)
    return gelu, dgelu


def _neg_expm1(y):
    series = -y * (1.0 + y * (0.5 + y * (1.0 / 6.0 + y * (1.0 / 24.0 + y * (1.0 / 120.0 + y * (1.0 / 720.0))))))
    return jnp.where(y > -0.25, series, 1.0 - jnp.exp(y))


def _rms_fwd(x, g):
    rs = lax.rsqrt(jnp.mean(x * x, axis=-1, keepdims=True) + EPS)
    return x * rs * g


def _rms_bwd(x, g, dy):
    rs = lax.rsqrt(jnp.mean(x * x, axis=-1, keepdims=True) + EPS)
    xh = x * rs
    dyg = dy * g
    dx = rs * (dyg - xh * jnp.mean(dyg * xh, axis=-1, keepdims=True))
    return dx, jnp.sum(dy * xh, axis=0, keepdims=True)


def _dot(a, b):
    return jnp.dot(a.astype(BF16), b.astype(BF16), preferred_element_type=F32)


def _dot_nt(a, b):
    return lax.dot_general(a.astype(BF16), b.astype(BF16), (((1,), (1,)), ((), ())), preferred_element_type=F32)


def _dot_tn(a, b):
    return lax.dot_general(a.astype(BF16), b.astype(BF16), (((0,), (0,)), ((), ())), preferred_element_type=F32)


class _LayerParam:
    def __init__(self, array, *index):
        self.array = array
        self.index = index

    @property
    def spec(self):
        lead = len(self.index)
        tail = self.array.shape[lead:]
        index = self.index
        return pl.BlockSpec((None,) * lead + tail, lambda *_: index + (0,) * len(tail))


def _row_ids(rows, block_index):
    return block_index * rows + lax.broadcasted_iota(jnp.int32, (rows, 1), 0)


def _accumulate(ref, value, first):
    @pl.when(first)
    def _():
        ref[...] = value

    @pl.when(jnp.logical_not(first))
    def _():
        ref[...] += value


def _norm_in_proj(h, g, w):
    n, d = h.shape
    zw = w.shape[1]
    tr = _row_tile(n)

    def body(h_ref, g_ref, w_ref, hn_ref, z_ref):
        hn = _rms_fwd(h_ref[...], g_ref[...]).astype(BF16)
        hn_ref[...] = hn
        z_ref[...] = jnp.dot(hn, w_ref[...], preferred_element_type=F32)

    return pl.pallas_call(
        body, name="norm_in_proj", grid=(n // tr,),
        in_specs=[pl.BlockSpec((tr, d), lambda i: (i, 0)), g.spec, VMEM_SPEC],
        out_specs=[pl.BlockSpec((tr, d), lambda i: (i, 0)), pl.BlockSpec((tr, zw), lambda i: (i, 0))],
        out_shape=[_sds((n, d), BF16), _sds((n, zw), F32)],
        compiler_params=_params(("parallel",), 48),
    )(h, g.array, w)


def _halo_specs(width, nb, col=0):
    per = ROW_BLOCK // 8
    prev = pl.BlockSpec((8, width), lambda i: (jnp.maximum(i * per - 1, 0), col))
    nxt = pl.BlockSpec((8, width), lambda i: (jnp.minimum((i + 1) * per, nb * per - 1), col))
    return prev, nxt


def _shift_down(x, prev8, d):
    n = x.shape[0]
    r = pltpu.roll(x, d, 0)
    p = pltpu.roll(prev8, d, 0)
    row8 = lax.broadcasted_iota(jnp.int32, (8, 1), 0)
    head = jnp.where(row8 < d, p, r[0:8])
    return jnp.concatenate([head, r[8:]], axis=0)


def _shift_up(x, next8, d):
    n = x.shape[0]
    r = pltpu.roll(x, n - d, 0)
    q = pltpu.roll(next8, 8 - d, 0)
    row8 = lax.broadcasted_iota(jnp.int32, (8, 1), 0)
    tail = jnp.where(row8 >= 8 - d, q, r[n - 8:])
    return jnp.concatenate([r[:n - 8], tail], axis=0)


def _conv_fwd(z, conv_w, conv_b):
    n = z.shape[0]
    nb = n // ROW_BLOCK
    prev_spec, next_spec = _halo_specs(LRU_W, nb)

    def body(cur_ref, prev_ref, next_ref, w_ref, b_ref, xc_ref):
        i = pl.program_id(0)
        cur = cur_ref[...]
        prev8 = prev_ref[...] * jnp.where(i > 0, 1.0, 0.0)
        next8 = next_ref[...] * jnp.where(i < nb - 1, 1.0, 0.0)
        w = [w_ref[pl.ds(k, 1), :] for k in range(4)]
        xc = (w[0] * _shift_down(cur, prev8, 2) + w[1] * _shift_down(cur, prev8, 1)
              + w[2] * cur + w[3] * _shift_up(cur, next8, 1) + b_ref[...])
        xc_ref[...] = xc

    return pl.pallas_call(
        body, name="conv_fwd", grid=(nb,),
        in_specs=[pl.BlockSpec((ROW_BLOCK, LRU_W), lambda i: (i, 0)), prev_spec, next_spec, conv_w.spec, conv_b.spec],
        out_specs=pl.BlockSpec((ROW_BLOCK, LRU_W), lambda i: (i, 0)),
        out_shape=_sds((n, LRU_W), F32),
        compiler_params=_params(("parallel",)),
    )(z, z, z, conv_w.array, conv_b.array)


def _conv_bwd(dxc_f, dxc_b, z, conv_w):
    n = z.shape[0]
    nb = n // ROW_BLOCK
    prev_spec, next_spec = _halo_specs(LRU_W, nb)
    row_spec = pl.BlockSpec((ROW_BLOCK, LRU_W), lambda i: (i, 0))

    def body(df_ref, dfp_ref, dfn_ref, db_ref, dbp_ref, dbn_ref, x_ref, xp_ref, xn_ref, w_ref,
             dx_ref, dw_ref, dbias_ref):
        i = pl.program_id(0)
        has_prev = jnp.where(i > 0, 1.0, 0.0)
        has_next = jnp.where(i < nb - 1, 1.0, 0.0)
        dxc = df_ref[...] + db_ref[...]
        dprev = (dfp_ref[...] + dbp_ref[...]) * has_prev
        dnext = (dfn_ref[...] + dbn_ref[...]) * has_next
        x = x_ref[...]
        xprev = xp_ref[...] * has_prev
        xnext = xn_ref[...] * has_next
        w = [w_ref[pl.ds(k, 1), :] for k in range(4)]
        dx_ref[...] = (w[0] * _shift_up(dxc, dnext, 2) + w[1] * _shift_up(dxc, dnext, 1)
                       + w[2] * dxc + w[3] * _shift_down(dxc, dprev, 1)).astype(BF16)
        dw = jnp.concatenate([
            jnp.sum(dxc * _shift_down(x, xprev, 2), axis=0, keepdims=True),
            jnp.sum(dxc * _shift_down(x, xprev, 1), axis=0, keepdims=True),
            jnp.sum(dxc * x, axis=0, keepdims=True),
            jnp.sum(dxc * _shift_up(x, xnext, 1), axis=0, keepdims=True),
            jnp.zeros((4, LRU_W), F32)], axis=0)
        _accumulate(dw_ref, dw, i == 0)
        _accumulate(dbias_ref, jnp.sum(dxc, axis=0, keepdims=True), i == 0)

    dx, dw, dbias = pl.pallas_call(
        body, name="conv_bwd", grid=(nb,),
        in_specs=[row_spec, prev_spec, next_spec, row_spec, prev_spec, next_spec, row_spec, prev_spec, next_spec,
                  conv_w.spec],
        out_specs=[row_spec, pl.BlockSpec((8, LRU_W), lambda i: (0, 0)), pl.BlockSpec((1, LRU_W), lambda i: (0, 0))],
        out_shape=[_sds((n, LRU_W), BF16), _sds((8, LRU_W), F32), _sds((1, LRU_W), F32)],
        compiler_params=_params(("arbitrary",)),
    )(dxc_f, dxc_f, dxc_f, dxc_b, dxc_b, dxc_b, z, z, z, conv_w.array)
    return dx, dw, dbias


def _mix_epilogue(h_f, h_b, o_f, o_b, z, head_norm):
    n = z.shape[0]
    tr = ROW_BLOCK
    spec = pl.BlockSpec((tr, 512), lambda i: (i, 0))

    def body(hf_ref, hb_ref, of_ref, ob_ref, gate_ref, gout_ref, w_ref, y_ref):
        gelu, _ = _gelu_and_grad(gate_ref[...])
        y_ref[:, 0:LRU_W] = ((hf_ref[...] + hb_ref[...]) * gelu).astype(BF16)
        o = of_ref[...] + ob_ref[...]
        gout = gout_ref[...]
        silu = gout * _sigmoid(gout)
        w = w_ref[...]
        for hd in range(GLA_HEADS):
            cs = slice(hd * GLA_DV, (hd + 1) * GLA_DV)
            oh = o[:, cs]
            on = oh * lax.rsqrt(jnp.mean(oh * oh, axis=-1, keepdims=True) + EPS)
            y_ref[:, LRU_W + hd * GLA_DV:LRU_W + (hd + 1) * GLA_DV] = (on * w[:, cs] * silu[:, cs]).astype(BF16)

    return pl.pallas_call(
        body, name="mix_epilogue", grid=(n // tr,),
        in_specs=[spec, spec, spec, spec, pl.BlockSpec((tr, 512), lambda i: (i, 1)),
                  pl.BlockSpec((tr, 512), lambda i: (i, 4)), head_norm.spec],
        out_specs=pl.BlockSpec((tr, D_MODEL), lambda i: (i, 0)),
        out_shape=_sds((n, D_MODEL), BF16),
        compiler_params=_params(("parallel",)),
    )(h_f, h_b, o_f, o_b, z, z, head_norm.array)


def _mix_epilogue_bwd(dymix, h_f, h_b, o_f, o_b, z, head_norm):
    n = z.shape[0]
    tr = ROW_BLOCK
    spec = pl.BlockSpec((tr, 512), lambda i: (i, 0))

    def body(dyl_ref, dyg_ref, hf_ref, hb_ref, of_ref, ob_ref, gate_ref, gout_ref, w_ref,
             dhs_ref, dgate_ref, do_ref, dgout_ref, dw_ref):
        i = pl.program_id(0)
        dyl = dyl_ref[...]
        gelu, dgelu = _gelu_and_grad(gate_ref[...])
        dhs_ref[...] = dyl * gelu
        dgate_ref[...] = (dyl * (hf_ref[...] + hb_ref[...]) * dgelu).astype(BF16)
        dyg = dyg_ref[...]
        o = of_ref[...] + ob_ref[...]
        gout = gout_ref[...]
        sg = _sigmoid(gout)
        silu = gout * sg
        dsilu = sg * (1.0 + gout * (1.0 - sg))
        w = w_ref[...]
        dws = []
        for hd in range(GLA_HEADS):
            cs = slice(hd * GLA_DV, (hd + 1) * GLA_DV)
            oh = o[:, cs]
            rs = lax.rsqrt(jnp.mean(oh * oh, axis=-1, keepdims=True) + EPS)
            on = oh * rs
            dy = dyg[:, cs]
            dgout_ref[:, cs] = (dy * on * w[:, cs] * dsilu[:, cs]).astype(BF16)
            dys = dy * silu[:, cs]
            dws.append(jnp.sum(dys * on, axis=0, keepdims=True))
            don = dys * w[:, cs]
            do_ref[:, cs] = (rs * (don - on * jnp.mean(don * on, axis=-1, keepdims=True))).astype(BF16)
        _accumulate(dw_ref, jnp.concatenate(dws, axis=1), i == 0)

    return pl.pallas_call(
        body, name="mix_epilogue_bwd", grid=(n // tr,),
        in_specs=[pl.BlockSpec((tr, 512), lambda i: (i, 0)), pl.BlockSpec((tr, 512), lambda i: (i, 1)),
                  spec, spec, spec, spec, pl.BlockSpec((tr, 512), lambda i: (i, 1)),
                  pl.BlockSpec((tr, 512), lambda i: (i, 4)), head_norm.spec],
        out_specs=[spec, spec, spec, spec, pl.BlockSpec((1, GLA_W), lambda i: (0, 0))],
        out_shape=[_sds((n, 512), F32)] + [_sds((n, 512), BF16)] * 3 + [_sds((1, GLA_W), F32)],
        compiler_params=_params(("arbitrary",)),
    )(dymix, dymix, h_f, h_b, o_f, o_b, z, z, head_norm.array)


def _out_proj(ymix, w_out, h, g):
    n, d = h.shape
    tr = _row_tile(n)
    spec = pl.BlockSpec((tr, d), lambda i: (i, 0))

    def body(y_ref, w_ref, h_ref, g_ref, mix_ref, hmid_ref):
        mix = jnp.dot(y_ref[...], w_ref[...], preferred_element_type=F32)
        mix_ref[...] = mix
        hmid_ref[...] = h_ref[...] + _rms_fwd(mix, g_ref[...])

    return pl.pallas_call(
        body, name="out_proj", grid=(n // tr,),
        in_specs=[spec, VMEM_SPEC, spec, g.spec],
        out_specs=[spec, spec],
        out_shape=[_sds((n, d), F32), _sds((n, d), F32)],
        compiler_params=_params(("parallel",), 44),
    )(ymix, w_out, h, g.array)


def _out_proj_bwd(dh_mid, mix, g, w_out):
    n, d = mix.shape
    tr = _row_tile(n)
    spec = pl.BlockSpec((tr, d), lambda i: (i, 0))

    def body(dh_ref, mix_ref, g_ref, w_ref, dmix_ref, dy_ref, dg_ref):
        i = pl.program_id(0)
        dmix, dg = _rms_bwd(mix_ref[...], g_ref[...], dh_ref[...])
        dmix = dmix.astype(BF16)
        dmix_ref[...] = dmix
        dy_ref[...] = _dot_nt(dmix, w_ref[...])
        _accumulate(dg_ref, dg, i == 0)

    return pl.pallas_call(
        body, name="out_proj_bwd", grid=(n // tr,),
        in_specs=[spec, spec, g.spec, VMEM_SPEC],
        out_specs=[spec, spec, pl.BlockSpec((1, d), lambda i: (0, 0))],
        out_shape=[_sds((n, d), BF16), _sds((n, d), F32), _sds((1, d), F32)],
        compiler_params=_params(("arbitrary",), 44),
    )(dh_mid, mix, g.array, w_out)


FF_SLAB = D_FF // N_DEV


def _relu_squared(up):
    return jnp.square(jnp.maximum(up.astype(F32), 0.0)).astype(BF16)


def _mlp_fwd(h_mid, g_pre, w_up, w_down, g_post, exchange=None):
    n, d = h_mid.shape
    tr = _row_tile(n, 384)
    spec = pl.BlockSpec((tr, d), lambda i: (i, 0))

    def body(h_ref, gpre_ref, wup_ref, wdn_ref, gpost_ref, hn_ref, up_ref, ff_ref, hout_ref):
        h = h_ref[...]
        hn = _rms_fwd(h, gpre_ref[...]).astype(BF16)
        hn_ref[...] = hn
        ff = jnp.zeros((tr, d), F32)
        for j in range(N_DEV):
            cs = slice(j * FF_SLAB, (j + 1) * FF_SLAB)
            up = jnp.dot(hn, wup_ref[j], preferred_element_type=F32).astype(BF16)
            up_ref[:, cs] = up
            ff = ff + jnp.dot(_relu_squared(up), wdn_ref[cs, :], preferred_element_type=F32)
        ff_ref[...] = ff
        hout_ref[...] = h + _rms_fwd(ff, gpost_ref[...])

    return _hosting_call(
        exchange, body, name="mlp_fwd", grid=(n // tr,),
        in_specs=[spec, g_pre.spec, VMEM_SPEC, VMEM_SPEC, g_post.spec],
        out_specs=[spec, pl.BlockSpec((tr, D_FF), lambda i: (i, 0)), spec, spec],
        out_shape=[_sds((n, d), BF16), _sds((n, D_FF), BF16), _sds((n, d), F32), _sds((n, d), F32)],
        scratch_shapes=[], compiler_params=_params(("arbitrary",), 52),
    )(h_mid, g_pre.array, w_up, w_down, g_post.array)


def _mlp_bwd(dh, ff, up, h_mid, g_pre, w_up, w_down, g_post, exchange=None):
    n, d = h_mid.shape
    tr = _row_tile(n, 384)
    spec = pl.BlockSpec((tr, d), lambda i: (i, 0))
    wide = pl.BlockSpec((tr, D_FF), lambda i: (i, 0))
    gspec = pl.BlockSpec((1, d), lambda i: (0, 0))

    def body(dh_ref, ff_ref, up_ref, h_ref, gpre_ref, wup_ref, wdn_ref, gpost_ref,
             dff_ref, dup_ref, dhmid_ref, dgpost_ref, dgpre_ref):
        i = pl.program_id(0)
        dh = dh_ref[...]
        dff, dgpost = _rms_bwd(ff_ref[...], gpost_ref[...], dh)
        dff = dff.astype(BF16)
        dff_ref[...] = dff
        dhn = jnp.zeros((tr, d), F32)
        for j in range(N_DEV):
            cs = slice(j * FF_SLAB, (j + 1) * FF_SLAB)
            relu = jnp.maximum(up_ref[:, cs].astype(F32), 0.0)
            dact = _dot_nt(dff, wdn_ref[cs, :])
            dup = (dact * 2.0 * relu).astype(BF16)
            dup_ref[:, cs] = dup
            dhn = dhn + _dot_nt(dup, wup_ref[j])
        dx, dgpre = _rms_bwd(h_ref[...], gpre_ref[...], dhn)
        dhmid_ref[...] = dh + dx
        _accumulate(dgpost_ref, dgpost, i == 0)
        _accumulate(dgpre_ref, dgpre, i == 0)

    return _hosting_call(
        exchange, body, name="mlp_bwd", grid=(n // tr,),
        in_specs=[spec, spec, wide, spec, g_pre.spec, VMEM_SPEC, VMEM_SPEC, g_post.spec],
        out_specs=[spec, wide, spec, gspec, gspec],
        out_shape=[_sds((n, d), BF16), _sds((n, D_FF), BF16), _sds((n, d), F32), _sds((1, d), F32), _sds((1, d), F32)],
        scratch_shapes=[], compiler_params=_params(("arbitrary",), 56),
    )(dh, ff, up, h_mid, g_pre.array, w_up, w_down, g_post.array)


def _in_proj_bwd(pieces, w_in, h, g, dh_mid):
    dxbr, dgate, dqk_f, dqk_b, dv_f, dv_b, dgout, dzg_f, dzg_b = pieces
    n, d = h.shape
    tr = _row_tile(n, 384)
    spec = pl.BlockSpec((tr, d), lambda i: (i, 0))
    s512 = pl.BlockSpec((tr, 512), lambda i: (i, 0))
    s128 = pl.BlockSpec((tr, LANES), lambda i: (i, 0))

    def body(a_ref, b_ref, cf_ref, cb_ref, df_ref, db_ref, e_ref, ff_ref, fb_ref, w_ref, h_ref, g_ref, dhm_ref,
             dz_ref, dh_ref, dg_ref):
        i = pl.program_id(0)
        real = (_row_ids(tr, i) >= PAD_ROWS).astype(F32)
        f32 = lambda ref: ref[...].astype(F32)
        dz = jnp.concatenate([f32(a_ref), f32(b_ref), f32(cf_ref) + f32(cb_ref), f32(df_ref) + f32(db_ref),
                              f32(e_ref), f32(ff_ref) + f32(fb_ref)], axis=1) * real
        dz = dz.astype(BF16)
        dz_ref[...] = dz
        dhn = _dot_nt(dz, w_ref[...])
        dx, dg = _rms_bwd(h_ref[...], g_ref[...], dhn)
        dh_ref[...] = (dhm_ref[...] + dx) * real
        _accumulate(dg_ref, dg, i == 0)

    return pl.pallas_call(
        body, name="in_proj_bwd", grid=(n // tr,),
        in_specs=[s512, s512, s512, s512, s512, s512, s512, s128, s128, VMEM_SPEC, spec, g.spec, spec],
        out_specs=[pl.BlockSpec((tr, Z_W), lambda i: (i, 0)), spec, pl.BlockSpec((1, d), lambda i: (0, 0))],
        out_shape=[_sds((n, Z_W), BF16), _sds((n, d), F32), _sds((1, d), F32)],
        compiler_params=_params(("arbitrary",), 48),
    )(dxbr, dgate, dqk_f, dqk_b, dv_f, dv_b, dgout, dzg_f, dzg_b, w_in, h, g.array, dh_mid)


def _matmul_tn(a, b, name, column_slabs=False, exchange=None, a_map=None):
    n, m = a.shape
    k = b.shape[1]
    tr, tm = _row_tile(n), _col_tile(m)
    tk = k // N_DEV if column_slabs else _col_tile(k)
    steps = n // tr

    def body(a_ref, b_ref, o_ref, acc_ref):
        r = pl.program_id(2)
        a_blk = a_ref[...] if a_map is None else a_map(a_ref[...])
        _accumulate(acc_ref, _dot_tn(a_blk, b_ref[...]), r == 0)

        @pl.when(r == steps - 1)
        def _():
            o_ref[...] = acc_ref[...].astype(BF16)

    if column_slabs:
        out_spec = pl.BlockSpec((None, tm, tk), lambda mi, ki, r: (ki, mi, 0))
        out_shape = _sds((N_DEV, m, tk), BF16)
    else:
        out_spec = pl.BlockSpec((tm, tk), lambda mi, ki, r: (mi, ki))
        out_shape = _sds((m, k), BF16)
    outs = _hosting_call(
        exchange, body, name=name, grid=(m // tm, k // tk, steps),
        in_specs=[pl.BlockSpec((tr, tm), lambda mi, ki, r: (r, mi)), pl.BlockSpec((tr, tk), lambda mi, ki, r: (r, ki))],
        out_specs=[out_spec], out_shape=[out_shape], scratch_shapes=[pltpu.VMEM((tm, tk), F32)],
        compiler_params=_params(("arbitrary", "arbitrary", "arbitrary"), 40),
    )(a, b)
    return outs[0] if exchange is None else outs


def _loss_and_grad(h_out, target):
    n, d = h_out.shape
    tr = ROW_BLOCK
    first = (PAD_ROWS + N_META) // tr

    def body(h_ref, t_ref, dh_ref, loss_ref):
        i = pl.program_id(0)
        real = jnp.where(i >= first, 1.0, 0.0)
        diff = (h_ref[...] - t_ref[...]) * real
        dh_ref[...] = diff * (1.0 / d)
        part = 0.5 * jnp.sum(jnp.mean(diff * diff, axis=-1, keepdims=True), axis=0, keepdims=True)
        _accumulate(loss_ref, jnp.broadcast_to(part, (1, LANES)), i == 0)

    return pl.pallas_call(
        body, name="loss_and_grad", grid=(n // tr,),
        in_specs=[pl.BlockSpec((tr, d), lambda i: (i, 0)), pl.BlockSpec((tr, d), lambda i: (jnp.maximum(i - first, 0), 0))],
        out_specs=[pl.BlockSpec((tr, d), lambda i: (i, 0)), pl.BlockSpec((1, LANES), lambda i: (0, 0))],
        out_shape=[_sds((n, d), F32), _sds((1, LANES), F32)],
        compiler_params=_params(("arbitrary",)),
    )(h_out, target)


SUBLANES = 8


def _scan_rows(a, u, reverse, window=None):
    n = a.shape[0]
    window = window or n
    pos = lax.broadcasted_iota(jnp.int32, (n, 1), 0) & (window - 1) if window < n else lax.broadcasted_iota(jnp.int32, (n, 1), 0)
    d = 1
    while d < window:
        shift = n - d if reverse else d
        keep = (pos < window - d) if reverse else (pos >= d)
        a_s = pltpu.roll(a, shift, 0)
        u_s = pltpu.roll(u, shift, 0)
        u = jnp.where(keep, a * u_s + u, u)
        a = jnp.where(keep, a * a_s, a)
        d *= 2
    return a, u


def _scan_block(a, u, h_in, reverse, stage_ref):
    n, width = a.shape
    groups = n // SUBLANES
    lanes = [slice(cb * LANES, (cb + 1) * LANES) for cb in range(width // LANES)]
    a1, u1 = _scan_rows(a, u, reverse, window=SUBLANES)
    for cb, cs in enumerate(lanes):
        stage_ref[0, cb] = a1[:, cs]
        stage_ref[1, cb] = u1[:, cs]
    edge = 0 if reverse else SUBLANES - 1
    group_rows = pl.ds(edge, groups, stride=SUBLANES)
    a2, u2 = _scan_rows(jnp.concatenate([stage_ref[0, cb, group_rows, :] for cb in range(len(lanes))], axis=1),
                        jnp.concatenate([stage_ref[1, cb, group_rows, :] for cb in range(len(lanes))], axis=1), reverse)
    leaving = a2 * h_in + u2
    grow = lax.broadcasted_iota(jnp.int32, (groups, 1), 0)
    if reverse:
        entering = jnp.where(grow == groups - 1, h_in, pltpu.roll(leaving, groups - 1, 0))
    else:
        entering = jnp.where(grow == 0, h_in, pltpu.roll(leaving, 1, 0))
    for cb, cs in enumerate(lanes):
        for k in range(SUBLANES):
            stage_ref[0, cb, pl.ds(k, groups, stride=SUBLANES), :] = entering[:, cs]
    entering_rows = jnp.concatenate([stage_ref[0, cb] for cb in range(len(lanes))], axis=1)
    return a1 * entering_rows + u1


def _lru_gates(xc, wcat_ref, bias_ref, lam_ref):
    nl = -lam_ref[...]
    nsp = -LRU_C * (jnp.maximum(nl, 0.0) + jnp.log(1.0 + jnp.exp(-jnp.abs(nl))))
    pre = _dot(xc, wcat_ref[...]) + bias_ref[...]
    r = _sigmoid(pre[:, :LRU_W])
    ig = _sigmoid(pre[:, LRU_W:])
    log_a = r * nsp
    a = jnp.exp(log_a)
    m = jnp.sqrt(_neg_expm1(2.0 * log_a))
    return r, ig, a, m, nsp


def _lru_scan(xc, wcat, bias, lam, reverse):
    n = xc.shape[0]
    nb = n // ROW_BLOCK
    order = (lambda i: nb - 1 - i) if reverse else (lambda i: i)
    spec = pl.BlockSpec((ROW_BLOCK, LRU_W), lambda i: (order(i), 0))
    edge = 0 if reverse else ROW_BLOCK - 1

    def body(xc_ref, wcat_ref, bias_ref, lam_ref, h_ref, carry_ref, stage_ref):
        i = pl.program_id(0)

        @pl.when(i == 0)
        def _():
            carry_ref[...] = jnp.zeros_like(carry_ref)

        xc = xc_ref[...]
        r, ig, a, m, _ = _lru_gates(xc, wcat_ref, bias_ref, lam_ref)
        u = jnp.where(_row_ids(ROW_BLOCK, order(i)) >= PAD_ROWS, m * (ig * xc), 0.0)
        h_ref[...] = _scan_block(a, u, carry_ref[0:1, :], reverse, stage_ref)
        carry_ref[0:1, :] = h_ref[pl.ds(edge, 1), :]

    return pl.pallas_call(
        body, name="lru_scan_b" if reverse else "lru_scan_f", grid=(nb,),
        in_specs=[spec, wcat.spec, bias.spec, lam.spec],
        out_specs=spec,
        out_shape=_sds((n, LRU_W), F32),
        scratch_shapes=[pltpu.VMEM((8, LRU_W), F32), pltpu.VMEM((2, LRU_W // LANES, ROW_BLOCK, LANES), F32)],
        compiler_params=_params(("arbitrary",)),
    )(xc, wcat.array, bias.array, lam.array)


def _lru_scan_bwd(dhs, xc, h, wcat, bias, lam, reverse):
    n = xc.shape[0]
    nb = n // ROW_BLOCK
    per = ROW_BLOCK // 8
    order = (lambda i: i) if reverse else (lambda i: nb - 1 - i)
    spec = pl.BlockSpec((ROW_BLOCK, LRU_W), lambda i: (order(i), 0))
    if reverse:
        halo = pl.BlockSpec((8, LRU_W), lambda i: (jnp.minimum((order(i) + 1) * per, nb * per - 1), 0))
    else:
        halo = pl.BlockSpec((8, LRU_W), lambda i: (jnp.maximum(order(i) * per - 1, 0), 0))
    edge = ROW_BLOCK - 1 if reverse else 0

    def body(dhs_ref, xc_ref, h_ref, halo_ref, wcat_ref, bias_ref, lam_ref,
             dxc_ref, dw_ref, db_ref, dlam_ref, cdh_ref, ca_ref, tmp_ref, stage_ref):
        i = pl.program_id(0)
        ib = order(i)

        @pl.when(i == 0)
        def _():
            cdh_ref[...] = jnp.zeros_like(cdh_ref)
            ca_ref[...] = jnp.zeros_like(ca_ref)

        xc = xc_ref[...]
        r, ig, a, m, nsp = _lru_gates(xc, wcat_ref, bias_ref, lam_ref)
        row = lax.broadcasted_iota(jnp.int32, (ROW_BLOCK, 1), 0)
        if reverse:
            coef = jnp.where(row == 0, ca_ref[0:1, :], pltpu.roll(a, 1, 0))
            h_nb = jnp.where(row == ROW_BLOCK - 1, halo_ref[0:1, :] * jnp.where(ib < nb - 1, 1.0, 0.0),
                             pltpu.roll(h_ref[...], ROW_BLOCK - 1, 0))
        else:
            coef = jnp.where(row == ROW_BLOCK - 1, ca_ref[0:1, :], pltpu.roll(a, ROW_BLOCK - 1, 0))
            h_nb = jnp.where(row == 0, halo_ref[7:8, :] * jnp.where(ib > 0, 1.0, 0.0), pltpu.roll(h_ref[...], 1, 0))
        dh = _scan_block(coef, dhs_ref[...], cdh_ref[0:1, :], not reverse, stage_ref)
        tmp_ref[...] = dh
        cdh_ref[0:1, :] = tmp_ref[pl.ds(edge, 1), :]
        tmp_ref[...] = a
        ca_ref[0:1, :] = tmp_ref[pl.ds(edge, 1), :]

        du = jnp.where(_row_ids(ROW_BLOCK, ib) >= PAD_ROWS, dh, 0.0)
        da = dh * h_nb
        dm = du * (ig * xc)
        di = du * (m * xc)
        dlog_a = da * a - dm * (a * a) / m
        dr = dlog_a * nsp
        dpre = jnp.concatenate([dr * r * (1.0 - r), di * ig * (1.0 - ig)], axis=1)
        dxc_ref[...] = du * (m * ig) + _dot_nt(dpre, wcat_ref[...])
        _accumulate(dw_ref, _dot_tn(xc, dpre), i == 0)
        _accumulate(db_ref, jnp.sum(dpre, axis=0, keepdims=True), i == 0)
        _accumulate(dlam_ref, jnp.sum(dlog_a * r, axis=0, keepdims=True), i == 0)

        @pl.when(i == nb - 1)
        def _():
            dlam_ref[...] = dlam_ref[...] * (LRU_C * _sigmoid(-lam_ref[...]))

    return pl.pallas_call(
        body, name="lru_scan_bwd_b" if reverse else "lru_scan_bwd_f", grid=(nb,),
        in_specs=[spec, spec, spec, halo, wcat.spec, bias.spec, lam.spec],
        out_specs=[spec, pl.BlockSpec((LRU_W, 2 * LRU_W), lambda i: (0, 0)),
                   pl.BlockSpec((1, 2 * LRU_W), lambda i: (0, 0)), pl.BlockSpec((1, LRU_W), lambda i: (0, 0))],
        out_shape=[_sds((n, LRU_W), F32), _sds((LRU_W, 2 * LRU_W), F32), _sds((1, 2 * LRU_W), F32), _sds((1, LRU_W), F32)],
        scratch_shapes=[pltpu.VMEM((8, LRU_W), F32), pltpu.VMEM((8, LRU_W), F32), pltpu.VMEM((ROW_BLOCK, LRU_W), F32),
                        pltpu.VMEM((2, LRU_W // LANES, ROW_BLOCK, LANES), F32)],
        compiler_params=_params(("arbitrary",)),
    )(dhs, xc, h, h, wcat.array, bias.array, lam.array)


def _gla_masks(reverse):
    t = lax.broadcasted_iota(jnp.int32, (CHUNK, CHUNK), 0)
    s = lax.broadcasted_iota(jnp.int32, (CHUNK, CHUNK), 1)
    if reverse:
        return (s >= t).astype(F32), s > t
    return (s <= t).astype(F32), s <= t


def _gla_gate(zg, wg_ref, bg_ref):
    pre = _dot(zg, wg_ref[...]) + bg_ref[...]
    g = (jnp.minimum(pre, 0.0) - jnp.log(1.0 + jnp.exp(-jnp.abs(pre)))) * (1.0 / GATE_NORM)
    return pre, g


def _gla_decays(gc, tri):
    b = jnp.dot(tri, gc, precision=lax.Precision.HIGHEST, preferred_element_type=F32)
    b_last = jnp.sum(gc, axis=0, keepdims=True)
    return jnp.exp(b), jnp.exp(-b), jnp.exp(b_last - b), jnp.exp(b_last)


def _gla_scan(z, wg, bg, reverse, exchange=None):
    n = z.shape[0]
    nb = n // ROW_BLOCK
    cpb = ROW_BLOCK // CHUNK
    order = (lambda i: nb - 1 - i) if reverse else (lambda i: i)
    chunks = range(cpb - 1, -1, -1) if reverse else range(cpb)

    def body(qk_ref, v_ref, zg_ref, wg_ref, bg_ref, o_ref, sall_ref, s_ref):
        i = pl.program_id(0)

        @pl.when(i == 0)
        def _():
            s_ref[...] = jnp.zeros_like(s_ref)

        tri, mask = _gla_masks(reverse)
        _, g = _gla_gate(zg_ref[...], wg_ref, bg_ref)
        heads = range(GLA_HEADS)
        ks = [slice(hd * GLA_DK, (hd + 1) * GLA_DK) for hd in heads]
        vs = [slice(hd * GLA_DV, (hd + 1) * GLA_DV) for hd in heads]
        qh, kb, v, el, p, intra, kv = {}, {}, {}, {}, {}, {}, {}
        for c in chunks:
            rows = slice(c * CHUNK, (c + 1) * CHUNK)
            eb, enb, ebl, el[c] = _gla_decays(g[rows], tri)
            qk = qk_ref[rows, :]
            q_all = (qk[:, :GLA_QK] * (GLA_DK ** -0.5) * eb).astype(BF16)
            k_all = (qk[:, GLA_QK:] * enb).astype(BF16)
            kb_all = (qk[:, GLA_QK:] * ebl).astype(BF16)
            v_all = v_ref[rows, :].astype(BF16)
            for hd in heads:
                qh[c, hd], kb[c, hd], v[c, hd] = q_all[:, ks[hd]], kb_all[:, ks[hd]], v_all[:, vs[hd]]
                p[c, hd] = _dot_nt(qh[c, hd], k_all[:, ks[hd]])
        for c in chunks:
            for hd in heads:
                intra[c, hd] = _dot(jnp.where(mask, p[c, hd], 0.0), v[c, hd])
                kv[c, hd] = _dot_tn(v[c, hd], kb[c, hd])
        state = [s_ref[:, ks[hd]] for hd in heads]
        for c in chunks:
            rows = slice(c * CHUNK, (c + 1) * CHUNK)
            for hd in heads:
                sall_ref[c, :, ks[hd]] = state[hd]
                o_ref[rows, vs[hd]] = intra[c, hd] + _dot_nt(qh[c, hd], state[hd])
                state[hd] = state[hd] * el[c][:, ks[hd]] + kv[c, hd]
        for hd in heads:
            s_ref[:, ks[hd]] = state[hd]

    return _hosting_call(
        exchange, body, name="gla_scan_b" if reverse else "gla_scan_f", grid=(nb,),
        in_specs=[pl.BlockSpec((ROW_BLOCK, 512), lambda i: (order(i), 2)), pl.BlockSpec((ROW_BLOCK, 512), lambda i: (order(i), 3)),
                  pl.BlockSpec((ROW_BLOCK, LANES), lambda i: (order(i), ZG_COL_BLOCK)), wg.spec, bg.spec],
        out_specs=[pl.BlockSpec((ROW_BLOCK, GLA_W), lambda i: (order(i), 0)),
                   pl.BlockSpec((cpb, GLA_DV, GLA_QK), lambda i: (order(i), 0, 0))],
        out_shape=[_sds((n, GLA_W), F32), _sds((n // CHUNK, GLA_DV, GLA_QK), F32)],
        scratch_shapes=[pltpu.VMEM((GLA_DV, GLA_QK), F32)],
        compiler_params=_params(("arbitrary",)),
    )(z, z, z, wg.array, bg.array)


def _gla_scan_bwd(do, z, states, wg, bg, reverse, exchange=None):
    n = z.shape[0]
    nb = n // ROW_BLOCK
    cpb = ROW_BLOCK // CHUNK
    order = (lambda i: i) if reverse else (lambda i: nb - 1 - i)
    chunks = range(cpb) if reverse else range(cpb - 1, -1, -1)

    def body(do_ref, qk_ref, v_ref, zg_ref, sall_ref, wg_ref, bg_ref,
             dqk_ref, dv_ref, dzg_ref, dwg_ref, dbg_ref, ds_ref):
        i = pl.program_id(0)

        @pl.when(i == 0)
        def _():
            ds_ref[...] = jnp.zeros_like(ds_ref)

        tri, mask = _gla_masks(reverse)
        tri_t, _ = _gla_masks(not reverse)
        zg = zg_ref[...]
        pre, g = _gla_gate(zg, wg_ref, bg_ref)
        heads = range(GLA_HEADS)
        ks = [slice(hd * GLA_DK, (hd + 1) * GLA_DK) for hd in heads]
        vs = [slice(hd * GLA_DV, (hd + 1) * GLA_DV) for hd in heads]
        dec, full, qh, kh, kb, v, dout, p, dp = {}, {}, {}, {}, {}, {}, {}, {}, {}
        for c in chunks:
            rows = slice(c * CHUNK, (c + 1) * CHUNK)
            dec[c] = _gla_decays(g[rows], tri)
            eb, enb, ebl, _ = dec[c]
            qk = qk_ref[rows, :]
            q_f = qk[:, :GLA_QK] * (GLA_DK ** -0.5) * eb
            k_f = qk[:, GLA_QK:] * enb
            kb_f = qk[:, GLA_QK:] * ebl
            full[c] = (q_f, k_f, kb_f)
            q_all, k_all, kb_all = q_f.astype(BF16), k_f.astype(BF16), kb_f.astype(BF16)
            v_all, do_all = v_ref[rows, :].astype(BF16), do_ref[rows, :].astype(BF16)
            for hd in heads:
                qh[c, hd], kh[c, hd], kb[c, hd] = q_all[:, ks[hd]], k_all[:, ks[hd]], kb_all[:, ks[hd]]
                v[c, hd], dout[c, hd] = v_all[:, vs[hd]], do_all[:, vs[hd]]
                p[c, hd] = _dot_nt(qh[c, hd], kh[c, hd])
                dp[c, hd] = _dot_nt(dout[c, hd], v[c, hd])
        dv_i, dqh, dkh, dsq, state = {}, {}, {}, {}, {}
        for c in chunks:
            for hd in heads:
                pm = jnp.where(mask, p[c, hd], 0.0).astype(BF16)
                dpm = jnp.where(mask, dp[c, hd], 0.0).astype(BF16)
                state[c, hd] = sall_ref[c, :, ks[hd]]
                dv_i[c, hd] = _dot_tn(pm, dout[c, hd])
                dqh[c, hd] = _dot(dpm, kh[c, hd]) + _dot(dout[c, hd], state[c, hd])
                dkh[c, hd] = _dot_tn(dpm, qh[c, hd])
                dsq[c, hd] = _dot_tn(dout[c, hd], qh[c, hd])
        dstate = [ds_ref[:, ks[hd]] for hd in heads]
        dkb, sds = {}, {}
        for c in chunks:
            rows = slice(c * CHUNK, (c + 1) * CHUNK)
            el = dec[c][3]
            for hd in heads:
                dv_ref[rows, vs[hd]] = (dv_i[c, hd] + _dot_nt(kb[c, hd], dstate[hd])).astype(BF16)
                dkb[c, hd] = _dot(v[c, hd], dstate[hd])
                sds[c, hd] = jnp.sum(state[c, hd] * dstate[hd], axis=0, keepdims=True)
                dstate[hd] = dstate[hd] * el[:, ks[hd]] + dsq[c, hd]
        for hd in heads:
            ds_ref[:, ks[hd]] = dstate[hd]
        dgs = [None] * cpb
        for c in chunks:
            rows = slice(c * CHUNK, (c + 1) * CHUNK)
            eb, enb, ebl, el = dec[c]
            q_f, k_f, kb_f = full[c]
            dqh_c = jnp.concatenate([dqh[c, hd] for hd in heads], axis=1)
            dkh_c = jnp.concatenate([dkh[c, hd] for hd in heads], axis=1)
            dkb_c = jnp.concatenate([dkb[c, hd] for hd in heads], axis=1)
            sds_c = jnp.concatenate([sds[c, hd] for hd in heads], axis=1)
            dqk_ref[rows, :] = jnp.concatenate([dqh_c * eb * (GLA_DK ** -0.5), dkh_c * enb + dkb_c * ebl], axis=1).astype(BF16)
            dkb_kb = dkb_c * kb_f
            db = dqh_c * q_f - dkh_c * k_f - dkb_kb
            db_last = el * sds_c + jnp.sum(dkb_kb, axis=0, keepdims=True)
            dgs[c] = jnp.dot(tri_t, db, precision=lax.Precision.HIGHEST, preferred_element_type=F32) + db_last
        dg = jnp.concatenate(dgs, axis=0)
        dpre = dg * _sigmoid(-pre) * (1.0 / GATE_NORM)
        dzg_ref[...] = _dot_nt(dpre, wg_ref[...]).astype(BF16)
        _accumulate(dwg_ref, _dot_tn(zg, dpre), i == 0)
        _accumulate(dbg_ref, jnp.sum(dpre, axis=0, keepdims=True), i == 0)

    return _hosting_call(
        exchange, body, name="gla_scan_bwd_b" if reverse else "gla_scan_bwd_f", grid=(nb,),
        in_specs=[pl.BlockSpec((ROW_BLOCK, GLA_W), lambda i: (order(i), 0)),
                  pl.BlockSpec((ROW_BLOCK, 512), lambda i: (order(i), 2)), pl.BlockSpec((ROW_BLOCK, 512), lambda i: (order(i), 3)),
                  pl.BlockSpec((ROW_BLOCK, LANES), lambda i: (order(i), ZG_COL_BLOCK)),
                  pl.BlockSpec((cpb, GLA_DV, GLA_QK), lambda i: (order(i), 0, 0)), wg.spec, bg.spec],
        out_specs=[pl.BlockSpec((ROW_BLOCK, 512), lambda i: (order(i), 0)), pl.BlockSpec((ROW_BLOCK, 512), lambda i: (order(i), 0)),
                   pl.BlockSpec((ROW_BLOCK, LANES), lambda i: (order(i), 0)),
                   pl.BlockSpec((LANES, GLA_QK), lambda i: (0, 0)), pl.BlockSpec((1, GLA_QK), lambda i: (0, 0))],
        out_shape=[_sds((n, 512), BF16), _sds((n, 512), BF16), _sds((n, LANES), BF16), _sds((LANES, GLA_QK), F32),
                   _sds((1, GLA_QK), F32)],
        scratch_shapes=[pltpu.VMEM((GLA_DV, GLA_QK), F32)],
        compiler_params=_params(("arbitrary",)),
    )(do, z, z, z, states, wg.array, bg.array)


NORM_NAMES = ("norm_mix_pre", "norm_mix_post", "norm_mlp_pre", "norm_mlp_post")
VEC512_NAMES = ("conv_b", "lru_ba_f", "lru_bx_f", "lru_lambda_f", "lru_ba_b", "lru_bx_b", "lru_lambda_b", "gla_head_norm")
VEC256_NAMES = ("gla_bg_f", "gla_bg_b")
LRU_MAT_NAMES = ("lru_wa_f", "lru_wx_f", "lru_wa_b", "lru_wx_b")
DIRS = ("f", "b")


def _prepare_params(w, gathered, depth):
    row_names = NORM_NAMES + ("conv_b", "gla_head_norm")
    ins = ([w[nm] for nm in row_names] + [w["lru_ba_" + d] for d in DIRS] + [w["lru_bx_" + d] for d in DIRS]
           + [w["lru_lambda_" + d] for d in DIRS] + [w["gla_bg_" + d] for d in DIRS]
           + [w["lru_wa_" + d].reshape(depth, LRU_W, LRU_HD) for d in DIRS]
           + [w["lru_wx_" + d].reshape(depth, LRU_W, LRU_HD) for d in DIRS]
           + [gathered["conv_w"], gathered["gla_wg_f"], gathered["gla_wg_b"], gathered["meta_tokens"]])
    n_rows = len(row_names)

    def body(*refs):
        rows_in = refs[:n_rows]
        ba, bx, lam, bg, wa, wx = (refs[n_rows + 2 * t:n_rows + 2 * t + 2] for t in range(6))
        convw_g, wgf_g, wgb_g, meta_g = refs[n_rows + 12:n_rows + 16]
        outs = refs[n_rows + 16:]
        rows_out = outs[:n_rows]
        convw, wcat, bias, lam_o, wg, bg_o, meta = outs[n_rows:]
        for l in range(depth):
            for src, dst in zip(rows_in, rows_out):
                dst[l] = src[pl.ds(l, 1), :]
            convw[l] = jnp.zeros((8, LRU_W), F32)
            for j in range(N_DEV):
                convw[l, 0:4, j * 64:(j + 1) * 64] = convw_g[j, l]
            for d in range(2):
                wcat[l, d] = jnp.zeros((LRU_W, 2 * LRU_W), BF16)
                for hd in range(LRU_HEADS):
                    rs = slice(hd * LRU_HD, (hd + 1) * LRU_HD)
                    wcat[l, d, rs, hd * LRU_HD:(hd + 1) * LRU_HD] = wa[d][l, rs, :].astype(BF16)
                    wcat[l, d, rs, LRU_W + hd * LRU_HD:LRU_W + (hd + 1) * LRU_HD] = wx[d][l, rs, :].astype(BF16)
                bias[l, d, :, 0:LRU_W] = ba[d][pl.ds(l, 1), :]
                bias[l, d, :, LRU_W:2 * LRU_W] = bx[d][pl.ds(l, 1), :]
                lam_o[l, d] = lam[d][pl.ds(l, 1), :]
                bg_o[l, d] = bg[d][pl.ds(l, 1), :]
                wg[l, d] = jnp.zeros((LANES, GLA_QK), BF16)
                src = wgf_g if d == 0 else wgb_g
                for j in range(N_DEV):
                    wg[l, d, d * GLA_RANK:(d + 1) * GLA_RANK, j * 32:(j + 1) * 32] = src[j, l].astype(BF16)
        for j in range(N_DEV):
            meta[:, j * LANES:(j + 1) * LANES] = meta_g[j]

    out_shape = ([_sds((depth, 1, w[nm].shape[1]), F32) for nm in row_names]
                 + [_sds((depth, 8, LRU_W), F32), _sds((depth, 2, LRU_W, 2 * LRU_W), BF16), _sds((depth, 2, 1, 2 * LRU_W), F32),
                    _sds((depth, 2, 1, LRU_W), F32), _sds((depth, 2, LANES, GLA_QK), BF16), _sds((depth, 2, 1, GLA_QK), F32),
                    _sds((N_META, D_MODEL), F32)])
    outs = pl.pallas_call(
        body, name="prepare_params", in_specs=[VMEM_SPEC] * len(ins), out_specs=[VMEM_SPEC] * len(out_shape),
        out_shape=out_shape, compiler_params=_params(None, 32),
    )(*ins)
    prepared = dict(zip(row_names, outs[:n_rows]))
    prepared.update(zip(("conv_w", "wcat", "lru_bias", "lru_lam", "wg", "gla_bg", "meta_tokens"), outs[n_rows:]))
    return prepared


class _WeightGather:
    def __init__(self, shards, p):
        self.shards, self.p = shards, p

    def exchange(self, items):
        ex = _Exchange()
        for nm, l in items:
            ex.add(self.shards[nm], _layer_of(l), _sds((N_DEV,) + self.shards[nm].shape[1:], BF16), _slab)
        return ex

    def install(self, items, landed):
        for (nm, l), g in zip(items, landed):
            if nm == "w_in":
                g = jnp.pad(jnp.concatenate([g[j] for j in range(N_DEV)], axis=1), ((0, 0), (0, Z_W - D_IN)))
            elif nm == "w_out":
                g = g.reshape(D_MODEL, D_MODEL)
            elif nm == "w_mlp_down":
                g = g.reshape(D_FF, D_MODEL)
            self.p.setdefault(nm, {})[l] = g


class _GradOutbox:
    def __init__(self):
        self.pending, self.received = [], {}

    def put(self, nm, l, slabs):
        self.pending.append((nm, l, slabs))

    def exchange(self, only=None):
        ex, keys, rest = _Exchange(), [], []
        for nm, l, slabs in self.pending:
            if only is None or nm == only:
                ex.add(slabs, _slab, _sds(slabs.shape, slabs.dtype), _slab)
                keys.append((nm, l))
            else:
                rest.append((nm, l, slabs))
        self.pending = rest
        return ex, keys

    def store(self, keys, landed):
        self.received.update(zip(keys, landed))


def _layer_fwd(h, l, p, gather, depth):
    lp = lambda name, *index: _LayerParam(p[name], l, *index)
    hn, z = _norm_in_proj(h, lp("norm_mix_pre"), p["w_in"][l])
    xc = _conv_fwd(z, lp("conv_w"), lp("conv_b"))
    s = dict(h=h, hn=hn, z=z, xc=xc)
    for d, name in enumerate(DIRS):
        s["h_" + name] = _lru_scan(xc, lp("wcat", d), lp("lru_bias", d), lp("lru_lam", d), d == 1)
        items = [(("w_mlp_up", "w_mlp_down")[d], l)]
        s["o_" + name], s["s_" + name], *landed = _gla_scan(z, lp("wg", d), lp("gla_bg", d), d == 1, gather.exchange(items))
        gather.install(items, landed)
    s["ymix"] = _mix_epilogue(s["h_f"], s["h_b"], s["o_f"], s["o_b"], z, lp("gla_head_norm"))
    s["mix"], s["h_mid"] = _out_proj(s["ymix"], p["w_out"][l], h, lp("norm_mix_post"))
    items = [("w_in", l + 1), ("w_out", l + 1)] if l + 1 < depth else []
    s["hn2"], s["up"], s["ff"], h_out, *landed = _mlp_fwd(
        s["h_mid"], lp("norm_mlp_pre"), p["w_mlp_up"][l], p["w_mlp_down"][l], lp("norm_mlp_post"), gather.exchange(items))
    gather.install(items, landed)
    return h_out, s


def _layer_bwd(dh_out, l, p, s, outbox):
    lp = lambda name, *index: _LayerParam(p[name], l, *index)
    g = {}
    ex, keys = outbox.exchange()
    d_ff, dup, dh_mid, g["norm_mlp_post"], g["norm_mlp_pre"], *landed = _mlp_bwd(
        dh_out, s["ff"], s["up"], s["h_mid"], lp("norm_mlp_pre"), p["w_mlp_up"][l], p["w_mlp_down"][l], lp("norm_mlp_post"), ex)
    outbox.store(keys, landed)
    outbox.put("w_mlp_down", l, _matmul_tn(s["up"], d_ff, "grad_w_down", a_map=_relu_squared)
               .reshape(N_DEV, D_FF // N_DEV, D_MODEL))
    outbox.put("w_mlp_up", l, _matmul_tn(s["hn2"], dup, "grad_w_up", column_slabs=True))
    dmix, dymix, g["norm_mix_post"] = _out_proj_bwd(dh_mid, s["mix"], lp("norm_mix_post"), p["w_out"][l])
    outbox.put("w_out", l, _matmul_tn(s["ymix"], dmix, "grad_w_out").reshape(N_DEV, D_MODEL // N_DEV, D_MODEL))
    dhs, dgate, do, dgout, g["gla_head_norm"] = _mix_epilogue_bwd(
        dymix, s["h_f"], s["h_b"], s["o_f"], s["o_b"], s["z"], lp("gla_head_norm"))
    dqk, dv, dzg, dxc = {}, {}, {}, {}
    for d, name in enumerate(DIRS):
        ex, keys = outbox.exchange(only=("w_mlp_down", "w_mlp_up")[d])
        dqk[name], dv[name], dzg[name], g["wg_" + name], g["gla_bg_" + name], *landed = _gla_scan_bwd(
            do, s["z"], s["s_" + name], lp("wg", d), lp("gla_bg", d), d == 1, ex)
        outbox.store(keys, landed)
        dxc[name], g["wcat_" + name], g["lru_bias_" + name], g["lru_lambda_" + name] = _lru_scan_bwd(
            dhs, s["xc"], s["h_" + name], lp("wcat", d), lp("lru_bias", d), lp("lru_lam", d), d == 1)
    dxbr, g["conv_w"], g["conv_b"] = _conv_bwd(dxc["f"], dxc["b"], s["z"], lp("conv_w"))
    dz, dh_in, g["norm_mix_pre"] = _in_proj_bwd(
        (dxbr, dgate, dqk["f"], dqk["b"], dv["f"], dv["b"], dgout, dzg["f"], dzg["b"]),
        p["w_in"][l], s["h"], lp("norm_mix_pre"), dh_mid)
    return dh_in, g, dz


def _w_in_slabs(grad_w_in):
    shard = D_IN // N_DEV
    return jnp.stack([grad_w_in[:, j * shard:(j + 1) * shard] for j in range(N_DEV)])


def _folded_block(hd):
    return slice((hd // 2) * LRU_HD, (hd // 2 + 1) * LRU_HD), slice((hd % 2) * LRU_HD, (hd % 2 + 1) * LRU_HD)


def _pack_small_grads(grads, dh0, depth):
    per_layer = ("norm_mix_pre", "norm_mix_post", "norm_mlp_pre", "norm_mlp_post", "conv_b", "gla_head_norm",
                 "lru_bias_f", "lru_bias_b", "lru_lambda_f", "lru_lambda_b", "gla_bg_f", "gla_bg_b",
                 "wcat_f", "wcat_b", "conv_w", "wg_f", "wg_b")
    ins = [grads[l][nm] for l in range(depth) for nm in per_layer]
    k = len(per_layer)
    meta_rows = PAD_ROWS // N_META

    def body(*refs):
        g = [dict(zip(per_layer, refs[l * k:(l + 1) * k])) for l in range(depth)]
        dh0_ref = refs[depth * k]
        norms, v512, v256, mats, convw, wgf, wgb, meta = refs[depth * k + 1:]
        v256[...] = jnp.zeros_like(v256)
        for l in range(depth):
            for p_, nm in enumerate(NORM_NAMES):
                norms[pl.ds(2 * p_ + l, 1), :] = g[l][nm][...]
            rows512 = [g[l]["conv_b"][...], g[l]["lru_bias_f"][:, 0:LRU_W], g[l]["lru_bias_f"][:, LRU_W:2 * LRU_W],
                       g[l]["lru_lambda_f"][...], g[l]["lru_bias_b"][:, 0:LRU_W], g[l]["lru_bias_b"][:, LRU_W:2 * LRU_W],
                       g[l]["lru_lambda_b"][...], g[l]["gla_head_norm"][...]]
            for p_, row in enumerate(rows512):
                v512[pl.ds(2 * p_ + l, 1), :] = row
            for p_, nm in enumerate(("gla_bg_f", "gla_bg_b")):
                v256[pl.ds(2 * p_ + l, 1), :] = g[l][nm][...]
            for d, name in enumerate(DIRS):
                for hd in range(LRU_HEADS):
                    rs = slice(hd * LRU_HD, (hd + 1) * LRU_HD)
                    dst_rows, dst_cols = _folded_block(hd)
                    mats[2 * d, l, dst_rows, dst_cols] = g[l]["wcat_" + name][rs, hd * LRU_HD:(hd + 1) * LRU_HD].astype(BF16)
                    mats[2 * d + 1, l, dst_rows, dst_cols] = (
                        g[l]["wcat_" + name][rs, LRU_W + hd * LRU_HD:LRU_W + (hd + 1) * LRU_HD].astype(BF16))
            for j in range(N_DEV):
                convw[j, l] = g[l]["conv_w"][0:4, j * 64:(j + 1) * 64]
                wgf[j, l] = g[l]["wg_f"][0:GLA_RANK, j * 32:(j + 1) * 32]
                wgb[j, l] = g[l]["wg_b"][GLA_RANK:2 * GLA_RANK, j * 32:(j + 1) * 32]
        for j in range(N_DEV):
            meta[j] = dh0_ref[:, j * LANES:(j + 1) * LANES]

    out_shape = [_sds((8, D_MODEL), F32), _sds((16, LRU_W), F32), _sds((8, GLA_QK), F32),
                 _sds((4, depth, LRU_W // 2, 2 * LRU_HD), BF16),
                 _sds((N_DEV, depth, 4, 64), F32), _sds((N_DEV, depth, GLA_RANK, 32), F32), _sds((N_DEV, depth, GLA_RANK, 32), F32),
                 _sds((N_DEV, N_META, LANES), F32)]
    return pl.pallas_call(
        body, name="pack_small_grads", grid=(1,),
        in_specs=[VMEM_SPEC] * (depth * k) + [pl.BlockSpec((N_META, D_MODEL), lambda i: (meta_rows, 0))],
        out_specs=[VMEM_SPEC] * len(out_shape), out_shape=out_shape, compiler_params=_params(("arbitrary",), 32),
    )(*ins, dh0)


def _my_index():
    return 4 * lax.axis_index("x") + 2 * lax.axis_index("y") + lax.axis_index("c")


def _peer(k):
    x, y, c = lax.axis_index("x"), lax.axis_index("y"), lax.axis_index("c")
    px = x ^ ((k >> 2) & 1)
    py = y ^ ((k >> 1) & 1)
    pc = c ^ (k & 1)
    return (px, py, pc), 4 * px + 2 * py + pc


class _Exchange:
    def __init__(self):
        self.inputs, self.out_shapes, self.transfers = [], [], []

    def add(self, array, src, out_shape, dst):
        self.transfers.append((len(self.inputs), src, len(self.out_shapes), dst))
        self.inputs.append(array)
        self.out_shapes.append(out_shape)
        return len(self.out_shapes) - 1

    def sem_shapes(self):
        nsem = len(self.transfers) * (N_DEV - 1)
        return [pltpu.SemaphoreType.DMA((nsem,)), pltpu.SemaphoreType.DMA((nsem,)),
                pltpu.SemaphoreType.DMA((len(self.transfers),))]

    def _local(self, ins, outs, sems):
        me = _my_index()
        return [pltpu.make_async_copy(src(ins[a], me), dst(outs[b], me), sems[2].at[t])
                for t, (a, src, b, dst) in enumerate(self.transfers)]

    def _remote(self, ins, outs, sems, t, k, sending):
        a, src, b, dst = self.transfers[t]
        peer, peer_index = _peer(k)
        sem = t * (N_DEV - 1) + k - 1
        return pltpu.make_async_remote_copy(
            src_ref=src(ins[a], peer_index), dst_ref=dst(outs[b], _my_index() if sending else peer_index),
            send_sem=sems[0].at[sem], recv_sem=sems[1].at[sem], device_id=peer, device_id_type=MESH_ID)

    def start(self, ins, outs, sems):
        for cp in self._local(ins, outs, sems):
            cp.start()
        for k in range(1, N_DEV):
            for t in range(len(self.transfers)):
                self._remote(ins, outs, sems, t, k, True).start()

    def wait(self, ins, outs, sems):
        for k in range(1, N_DEV):
            for t in range(len(self.transfers)):
                self._remote(ins, outs, sems, t, k, False).wait_recv()
        for k in range(1, N_DEV):
            for t in range(len(self.transfers)):
                self._remote(ins, outs, sems, t, k, True).wait_send()
        for cp in self._local(ins, outs, sems):
            cp.wait()

    def run(self, name):
        n_in, n_out = len(self.inputs), len(self.out_shapes)

        def body(*refs):
            ins, outs, sems = refs[:n_in], refs[n_in:n_in + n_out], refs[n_in + n_out:]
            self.start(ins, outs, sems)
            self.wait(ins, outs, sems)

        return pl.pallas_call(
            body, name=name, in_specs=[ANY_SPEC] * n_in, out_specs=[ANY_SPEC] * n_out, out_shape=self.out_shapes,
            scratch_shapes=self.sem_shapes(), compiler_params=pltpu.CompilerParams(has_side_effects=True),
        )(*self.inputs)


def _hosting_call(exchange, body, *, name, grid, in_specs, out_specs, out_shape, scratch_shapes, compiler_params):
    if exchange is None or not exchange.transfers:
        return pl.pallas_call(body, name=name, grid=grid, in_specs=in_specs, out_specs=out_specs, out_shape=out_shape,
                              scratch_shapes=scratch_shapes, compiler_params=compiler_params)
    n_in, n_out, n_scr = len(in_specs), len(out_specs), len(scratch_shapes)
    x_in, x_out = len(exchange.inputs), len(exchange.out_shapes)

    def hosted(*refs):
        ins, x_ins = refs[:n_in], refs[n_in:n_in + x_in]
        o0 = n_in + x_in
        outs, x_outs = refs[o0:o0 + n_out], refs[o0 + n_out:o0 + n_out + x_out]
        s0 = o0 + n_out + x_out
        scratch, sems = refs[s0:s0 + n_scr], refs[s0 + n_scr:]
        ids = [pl.program_id(a) for a in range(len(grid))]
        first = functools.reduce(jnp.logical_and, [i == 0 for i in ids])
        last = functools.reduce(jnp.logical_and, [i == g - 1 for i, g in zip(ids, grid)])

        @pl.when(first)
        def _():
            exchange.start(x_ins, x_outs, sems)

        body(*ins, *outs, *scratch)

        @pl.when(last)
        def _():
            exchange.wait(x_ins, x_outs, sems)

    call = pl.pallas_call(
        hosted, name=name, grid=grid, in_specs=list(in_specs) + [ANY_SPEC] * x_in,
        out_specs=list(out_specs) + [ANY_SPEC] * x_out, out_shape=list(out_shape) + list(exchange.out_shapes),
        scratch_shapes=list(scratch_shapes) + exchange.sem_shapes(), compiler_params=compiler_params)
    return lambda *operands: call(*operands, *exchange.inputs)


def _whole(ref, j):
    return ref


def _slab(ref, j):
    return ref.at[j]


def _layer_of(l):
    return lambda ref, j: ref.at[l]


def _slab_layer(l):
    return lambda ref, j: ref.at[j, l]


def _adamw(g, w, m, v):
    nm = ADAM_B1 * m + (1.0 - ADAM_B1) * g
    nv = ADAM_B2 * v + (1.0 - ADAM_B2) * jnp.square(g)
    m_hat = nm / (1.0 - ADAM_B1 ** ADAM_STEP)
    v_hat = nv / (1.0 - ADAM_B2 ** ADAM_STEP)
    return -ADAM_LR * (m_hat / (jnp.sqrt(v_hat) + ADAM_EPS) + ADAM_WD * w), nm, nv


def _sum_parts(p_ref):
    g = p_ref[0].astype(F32)
    for j in range(1, N_DEV):
        g = g + p_ref[j].astype(F32)
    return g


def _adamw_sharded(parts, w, m, v, name):
    shape = w.shape
    lead, (rows, cols) = shape[:-2], shape[-2:]
    tr = min(rows, ROW_BLOCK)
    assert rows % tr == 0
    steps = rows // tr
    nl = len(lead)
    spec = pl.BlockSpec((None,) * nl + (tr, cols), lambda *idx: idx + (0,))
    per_layer = isinstance(parts, (list, tuple))
    if per_layer:
        def part_spec(l):
            return pl.BlockSpec((N_DEV, tr, cols), lambda li, r: (0, jnp.where(li == l, r, jnp.where(li < l, 0, steps - 1)), 0))
        part_specs = [part_spec(l) for l in range(len(parts))]
    else:
        parts = [parts]
        part_specs = [pl.BlockSpec((N_DEV,) + (None,) * nl + (tr, cols), lambda *idx: (0,) + idx + (0,))]
    count = len(parts)

    def body(*refs):
        p_refs = refs[:count]
        w_ref, m_ref, v_ref, g_ref, d_ref, nm_ref, nv_ref = refs[count:]

        def update(p_ref):
            g = _sum_parts(p_ref)
            g_ref[...] = g
            d_ref[...], nm_ref[...], nv_ref[...] = _adamw(g, w_ref[...], m_ref[...], v_ref[...])

        if per_layer:
            for l in range(count):
                pl.when(pl.program_id(0) == l)(functools.partial(update, p_refs[l]))
        else:
            update(p_refs[0])

    return pl.pallas_call(
        body, name=name, grid=lead + (steps,),
        in_specs=part_specs + [spec, spec, spec], out_specs=[spec] * 4, out_shape=[_sds(shape, F32)] * 4,
        compiler_params=_params(("arbitrary",) * (nl + 1)),
    )(*parts, w, m, v)


def _adamw_replicated(gathered, w, m, v, depth):
    names = NORM_NAMES + VEC512_NAMES + VEC256_NAMES + LRU_MAT_NAMES
    count = len(names)

    def body(*refs):
        norms, v512, v256, mats = refs[:4]
        w_refs, m_refs, v_refs = (refs[4 + t * count:4 + (t + 1) * count] for t in range(3))
        outs = refs[4 + 3 * count:4 + 7 * count]
        sum_norms, sum_512, sum_256, unfolded = refs[4 + 7 * count:]
        sum_norms[...] = _sum_parts(norms)
        sum_512[...] = _sum_parts(v512)
        sum_256[...] = _sum_parts(v256)
        for n_, nm in enumerate(names):
            if nm in NORM_NAMES:
                g = sum_norms[pl.ds(depth * NORM_NAMES.index(nm), depth), :]
            elif nm in VEC512_NAMES:
                g = sum_512[pl.ds(depth * VEC512_NAMES.index(nm), depth), :]
            elif nm in VEC256_NAMES:
                g = sum_256[pl.ds(depth * VEC256_NAMES.index(nm), depth), :]
            else:
                p_ = LRU_MAT_NAMES.index(nm)
                folded = mats[0, p_].astype(F32)
                for j in range(1, N_DEV):
                    folded = folded + mats[j, p_].astype(F32)
                for hd in range(LRU_HEADS):
                    src_rows, src_cols = _folded_block(hd)
                    unfolded[:, hd * LRU_HD:(hd + 1) * LRU_HD, :] = folded[:, src_rows, src_cols]
                g = unfolded[...]
            delta, nm_, nv_ = _adamw(g, w_refs[n_][...], m_refs[n_][...], v_refs[n_][...])
            outs[n_][...] = g
            outs[count + n_][...] = delta
            outs[2 * count + n_][...] = nm_
            outs[3 * count + n_][...] = nv_

    shapes = [_sds(w[nm].shape, F32) for nm in names]
    ins = list(gathered) + [t[nm] for t in (w, m, v) for nm in names]
    outs = pl.pallas_call(
        body, name="adamw_replicated", in_specs=[VMEM_SPEC] * len(ins), out_specs=[VMEM_SPEC] * (4 * count),
        out_shape=shapes * 4,
        scratch_shapes=[pltpu.VMEM(gathered[0].shape[1:], F32), pltpu.VMEM(gathered[1].shape[1:], F32),
                        pltpu.VMEM(gathered[2].shape[1:], F32), pltpu.VMEM((depth, LRU_W, LRU_HD), F32)],
        compiler_params=_params(None, 48),
    )(*ins)
    return [dict(zip(names, outs[t * count:(t + 1) * count])) for t in range(4)]


WEIGHT_NAMES = ("meta_tokens", "norm_mix_pre", "norm_mix_post", "norm_mlp_pre", "norm_mlp_post", "w_in", "conv_w", "conv_b",
                "lru_wa_f", "lru_ba_f", "lru_wx_f", "lru_bx_f", "lru_lambda_f", "lru_wa_b", "lru_ba_b", "lru_wx_b",
                "lru_bx_b", "lru_lambda_b", "gla_wg_f", "gla_bg_f", "gla_wg_b", "gla_bg_b", "gla_head_norm", "w_out",
                "w_mlp_up", "w_mlp_down")
MATMUL_WEIGHTS = ("w_in", "w_out", "w_mlp_up", "w_mlp_down")
SMALL_SHARDED = ("conv_w", "gla_wg_f", "gla_wg_b", "meta_tokens")


def kernel(x, meta_tokens, norm_mix_pre, norm_mix_post, norm_mlp_pre, norm_mlp_post, w_in, conv_w, conv_b, lru_wa_f, lru_ba_f, lru_wx_f, lru_bx_f, lru_lambda_f, lru_wa_b, lru_ba_b, lru_wx_b, lru_bx_b, lru_lambda_b, gla_wg_f, gla_bg_f, gla_wg_b, gla_bg_b, gla_head_norm, w_out, w_mlp_up, w_mlp_down, loss_target, m_meta_tokens, m_norm_mix_pre, m_norm_mix_post, m_norm_mlp_pre, m_norm_mlp_post, m_w_in, m_conv_w, m_conv_b, m_lru_wa_f, m_lru_ba_f, m_lru_wx_f, m_lru_bx_f, m_lru_lambda_f, m_lru_wa_b, m_lru_ba_b, m_lru_wx_b, m_lru_bx_b, m_lru_lambda_b, m_gla_wg_f, m_gla_bg_f, m_gla_wg_b, m_gla_bg_b, m_gla_head_norm, m_w_out, m_w_mlp_up, m_w_mlp_down, v_meta_tokens, v_norm_mix_pre, v_norm_mix_post, v_norm_mlp_pre, v_norm_mlp_post, v_w_in, v_conv_w, v_conv_b, v_lru_wa_f, v_lru_ba_f, v_lru_wx_f, v_lru_bx_f, v_lru_lambda_f, v_lru_wa_b, v_lru_ba_b, v_lru_wx_b, v_lru_bx_b, v_lru_lambda_b, v_gla_wg_f, v_gla_bg_f, v_gla_wg_b, v_gla_bg_b, v_gla_head_norm, v_w_out, v_w_mlp_up, v_w_mlp_down):
    args = locals()
    w = {nm: args[nm] for nm in WEIGHT_NAMES}
    m = {nm: args["m_" + nm] for nm in WEIGHT_NAMES}
    v = {nm: args["v_" + nm] for nm in WEIGHT_NAMES}
    depth = w_in.shape[0]

    shards = {nm: w[nm].astype(BF16) for nm in MATMUL_WEIGHTS}
    first_items = [("w_in", 0), ("w_out", 0)]
    p = {}
    gather = _WeightGather(shards, p)
    ex = gather.exchange(first_items)
    for nm in SMALL_SHARDED:
        ex.add(w[nm], _whole, _sds((N_DEV,) + w[nm].shape, F32), _slab)
    landed = ex.run("all_gather")
    gather.install(first_items, landed[:len(first_items)])
    p.update(_prepare_params(w, dict(zip(SMALL_SHARDED, landed[len(first_items):])), depth))

    h = jnp.concatenate([jnp.zeros((PAD_ROWS, D_MODEL), F32), p["meta_tokens"], x[0]], axis=0)
    saved = []
    for l in range(depth):
        h, s = _layer_fwd(h, l, p, gather, depth)
        saved.append(s)
    dh, loss_part = _loss_and_grad(h, loss_target[0])
    loss = lax.psum(loss_part[0, 0], ("x", "y", "c"))

    outbox = _GradOutbox()
    grads = [None] * depth
    for l in reversed(range(depth)):
        dh, grads[l], dz = _layer_bwd(dh, l, p, saved[l], outbox)
        if l > 0:
            outbox.put("w_in", l, _w_in_slabs(_matmul_tn(saved[l]["hn"], dz, "grad_w_in")))
    grad_x = dh[PAD_ROWS + N_META:][None]

    small = _pack_small_grads(grads, dh, depth)
    rep_bufs, small_slabs = small[:4], small[4:]
    ex, keys = outbox.exchange()
    for g in small_slabs:
        ex.add(g, _slab, _sds(g.shape, F32), _slab)
    for g in rep_bufs:
        ex.add(g, _whole, _sds((N_DEV,) + g.shape, g.dtype), _slab)
    grad_w_in, *landed = _matmul_tn(saved[0]["hn"], dz, "grad_w_in", exchange=ex)
    outbox.store(keys, landed[:len(keys)])
    small_received = landed[len(keys):len(keys) + len(small_slabs)]
    rep_received = landed[len(keys) + len(small_slabs):]
    outbox.put("w_in", 0, _w_in_slabs(grad_w_in))
    ex, keys = outbox.exchange()
    outbox.store(keys, ex.run("exchange_grads"))

    results = [{}, {}, {}, {}]
    for nm in MATMUL_WEIGHTS:
        parts = [outbox.received[(nm, l)] for l in range(depth)]
        for t, out in enumerate(_adamw_sharded(parts, w[nm], m[nm], v[nm], "adamw_" + nm)):
            results[t][nm] = out
    for nm, parts in zip(SMALL_SHARDED, small_received):
        for t, out in enumerate(_adamw_sharded(parts, w[nm], m[nm], v[nm], "adamw_" + nm)):
            results[t][nm] = out

    def kernel_side(tree):
        return {nm: tree[nm].reshape(depth, LRU_W, LRU_HD) if nm in LRU_MAT_NAMES else tree[nm]
                for nm in NORM_NAMES + VEC512_NAMES + VEC256_NAMES + LRU_MAT_NAMES}

    for t, tree in enumerate(_adamw_replicated(rep_received, kernel_side(w), kernel_side(m), kernel_side(v), depth)):
        for nm, out in tree.items():
            results[t][nm] = out.reshape(w[nm].shape)
    return (loss, grad_x, *[results[t][nm] for t in range(4) for nm in WEIGHT_NAMES])
```

```python
import functools

import jax
import jax.numpy as jnp
from jax import lax
from jax.experimental import pallas as pl
from jax.experimental.pallas import tpu as pltpu

F32 = jnp.float32
BF16 = jnp.bfloat16

N_DEV = 8
D_MODEL = 1024
N_META = 16
ROW_BLOCK = 256
PAD_ROWS = ROW_BLOCK - N_META
CHUNK = 64
LRU_W = 512
LRU_HEADS = 8
LRU_HD = 64
LRU_C = 8.0
GLA_HEADS = 4
GLA_DK = 64
GLA_DV = 128
GLA_QK = GLA_HEADS * GLA_DK
GLA_W = GLA_HEADS * GLA_DV
GLA_RANK = 16
GATE_NORM = 16.0
D_FF = 4096
D_IN = 2592
Z_W = 2688
ZG_COL_BLOCK = 2560 // 128
EPS = 1e-6
LANES = 128

ADAM_LR = 0.001
ADAM_B1 = 0.9
ADAM_B2 = 0.999
ADAM_EPS = 1e-08
ADAM_WD = 0.01
ADAM_STEP = 10
ADAM_ROWS = 512

VMEM_SPEC = pl.BlockSpec(memory_space=pltpu.VMEM)
ANY_SPEC = pl.BlockSpec(memory_space=pl.ANY)
MESH_ID = pl.DeviceIdType.MESH


def _sds(shape, dtype):
    return jax.ShapeDtypeStruct(shape, dtype)


def _params(sem=None, vmem_mb=None):
    kw = {}
    if sem is not None:
        kw["dimension_semantics"] = sem
    if vmem_mb is not None:
        kw["vmem_limit_bytes"] = vmem_mb * 2 ** 20
    return pltpu.CompilerParams(**kw)


def _row_tile(n, cap=768):
    for t in (768, 512, 384, 256):
        if t <= cap and n % t == 0:
            return t
    raise ValueError(n)


def _col_tile(k):
    for t in (1024, 896, 768, 640, 512, 384, 256, 128):
        if k % t == 0:
            return t
    raise ValueError(k)


def _sigmoid(x):
    return 0.5 * jnp.tanh(0.5 * x) + 0.5


def _gelu_and_grad(x):
    c = 0.7978845608028654
    inner = c * (x + 0.044715 * x * x * x)
    t = jnp.tanh(inner)
    gelu = 0.5 * x * (1.0 + t)
    dgelu = 0.5 * (1.0 + t) + 0.5 * x * (1.0 - t * t) * c * (1.0 + 3.0 * 0.044715 * x * x)
    return gelu, dgelu


def _one_minus_square(a, log_a):
    return jnp.tanh(-log_a) * (1.0 + a * a)


def _rms_fwd(x, g):
    rs = lax.rsqrt(jnp.mean(x * x, axis=-1, keepdims=True) + EPS)
    return x * rs * g


def _rms_bwd(x, g, dy):
    rs = lax.rsqrt(jnp.mean(x * x, axis=-1, keepdims=True) + EPS)
    xh = x * rs
    dyg = dy * g
    dx = rs * (dyg - xh * jnp.mean(dyg * xh, axis=-1, keepdims=True))
    return dx, jnp.sum(dy * xh, axis=0, keepdims=True)


def _dot(a, b):
    return jnp.dot(a.astype(BF16), b.astype(BF16), preferred_element_type=F32)


def _dot_nt(a, b):
    return lax.dot_general(a.astype(BF16), b.astype(BF16), (((1,), (1,)), ((), ())), preferred_element_type=F32)


def _dot_tn(a, b):
    return lax.dot_general(a.astype(BF16), b.astype(BF16), (((0,), (0,)), ((), ())), preferred_element_type=F32)


class _LayerParam:
    def __init__(self, array, *index):
        self.array = array
        self.index = index

    @property
    def spec(self):
        lead = len(self.index)
        tail = self.array.shape[lead:]
        index = self.index
        return pl.BlockSpec((None,) * lead + tail, lambda *_: index + (0,) * len(tail))


def _row_ids(rows, block_index):
    return block_index * rows + lax.broadcasted_iota(jnp.int32, (rows, 1), 0)


def _accumulate(ref, value, first):
    @pl.when(first)
    def _():
        ref[...] = value

    @pl.when(jnp.logical_not(first))
    def _():
        ref[...] += value


def _norm_in_proj(h, g, w):
    n, d = h.shape
    zw = w.shape[1]
    tr = _row_tile(n)

    def body(h_ref, g_ref, w_ref, hn_ref, z_ref):
        hn = _rms_fwd(h_ref[...], g_ref[...]).astype(BF16)
        hn_ref[...] = hn
        z_ref[...] = jnp.dot(hn, w_ref[...], preferred_element_type=F32)

    return pl.pallas_call(
        body, name="norm_in_proj", grid=(n // tr,),
        in_specs=[pl.BlockSpec((tr, d), lambda i: (i, 0)), g.spec, VMEM_SPEC],
        out_specs=[pl.BlockSpec((tr, d), lambda i: (i, 0)), pl.BlockSpec((tr, zw), lambda i: (i, 0))],
        out_shape=[_sds((n, d), BF16), _sds((n, zw), F32)],
        compiler_params=_params(("parallel",), 48),
    )(h, g.array, w)


def _halo_specs(width, nb, col=0):
    per = ROW_BLOCK // 8
    prev = pl.BlockSpec((8, width), lambda i: (jnp.maximum(i * per - 1, 0), col))
    nxt = pl.BlockSpec((8, width), lambda i: (jnp.minimum((i + 1) * per, nb * per - 1), col))
    return prev, nxt


def _shift_down(x, prev8, d):
    n = x.shape[0]
    r = pltpu.roll(x, d, 0)
    p = pltpu.roll(prev8, d, 0)
    row8 = lax.broadcasted_iota(jnp.int32, (8, 1), 0)
    head = jnp.where(row8 < d, p, r[0:8])
    return jnp.concatenate([head, r[8:]], axis=0)


def _shift_up(x, next8, d):
    n = x.shape[0]
    r = pltpu.roll(x, n - d, 0)
    q = pltpu.roll(next8, 8 - d, 0)
    row8 = lax.broadcasted_iota(jnp.int32, (8, 1), 0)
    tail = jnp.where(row8 >= 8 - d, q, r[n - 8:])
    return jnp.concatenate([r[:n - 8], tail], axis=0)


def _conv_fwd(z, conv_w, conv_b):
    n = z.shape[0]
    nb = n // ROW_BLOCK
    prev_spec, next_spec = _halo_specs(LRU_W, nb)

    def body(cur_ref, prev_ref, next_ref, w_ref, b_ref, xc_ref):
        i = pl.program_id(0)
        cur = cur_ref[...]
        prev8 = prev_ref[...] * jnp.where(i > 0, 1.0, 0.0)
        next8 = next_ref[...] * jnp.where(i < nb - 1, 1.0, 0.0)
        w = [w_ref[pl.ds(k, 1), :] for k in range(4)]
        xc = (w[0] * _shift_down(cur, prev8, 2) + w[1] * _shift_down(cur, prev8, 1)
              + w[2] * cur + w[3] * _shift_up(cur, next8, 1) + b_ref[...])
        xc_ref[...] = xc

    return pl.pallas_call(
        body, name="conv_fwd", grid=(nb,),
        in_specs=[pl.BlockSpec((ROW_BLOCK, LRU_W), lambda i: (i, 0)), prev_spec, next_spec, conv_w.spec, conv_b.spec],
        out_specs=pl.BlockSpec((ROW_BLOCK, LRU_W), lambda i: (i, 0)),
        out_shape=_sds((n, LRU_W), F32),
        compiler_params=_params(("parallel",)),
    )(z, z, z, conv_w.array, conv_b.array)


def _conv_bwd(dxc_f, dxc_b, z, conv_w):
    n = z.shape[0]
    nb = n // ROW_BLOCK
    prev_spec, next_spec = _halo_specs(LRU_W, nb)
    row_spec = pl.BlockSpec((ROW_BLOCK, LRU_W), lambda i: (i, 0))

    def body(df_ref, dfp_ref, dfn_ref, db_ref, dbp_ref, dbn_ref, x_ref, xp_ref, xn_ref, w_ref,
             dx_ref, dw_ref, dbias_ref):
        i = pl.program_id(0)
        has_prev = jnp.where(i > 0, 1.0, 0.0)
        has_next = jnp.where(i < nb - 1, 1.0, 0.0)
        dxc = df_ref[...] + db_ref[...]
        dprev = (dfp_ref[...] + dbp_ref[...]) * has_prev
        dnext = (dfn_ref[...] + dbn_ref[...]) * has_next
        x = x_ref[...]
        xprev = xp_ref[...] * has_prev
        xnext = xn_ref[...] * has_next
        w = [w_ref[pl.ds(k, 1), :] for k in range(4)]
        dx_ref[...] = (w[0] * _shift_up(dxc, dnext, 2) + w[1] * _shift_up(dxc, dnext, 1)
                       + w[2] * dxc + w[3] * _shift_down(dxc, dprev, 1)).astype(BF16)
        dw = jnp.concatenate([
            jnp.sum(dxc * _shift_down(x, xprev, 2), axis=0, keepdims=True),
            jnp.sum(dxc * _shift_down(x, xprev, 1), axis=0, keepdims=True),
            jnp.sum(dxc * x, axis=0, keepdims=True),
            jnp.sum(dxc * _shift_up(x, xnext, 1), axis=0, keepdims=True),
            jnp.zeros((4, LRU_W), F32)], axis=0)
        _accumulate(dw_ref, dw, i == 0)
        _accumulate(dbias_ref, jnp.sum(dxc, axis=0, keepdims=True), i == 0)

    dx, dw, dbias = pl.pallas_call(
        body, name="conv_bwd", grid=(nb,),
        in_specs=[row_spec, prev_spec, next_spec, row_spec, prev_spec, next_spec, row_spec, prev_spec, next_spec,
                  conv_w.spec],
        out_specs=[row_spec, pl.BlockSpec((8, LRU_W), lambda i: (0, 0)), pl.BlockSpec((1, LRU_W), lambda i: (0, 0))],
        out_shape=[_sds((n, LRU_W), BF16), _sds((8, LRU_W), F32), _sds((1, LRU_W), F32)],
        compiler_params=_params(("arbitrary",)),
    )(dxc_f, dxc_f, dxc_f, dxc_b, dxc_b, dxc_b, z, z, z, conv_w.array)
    return dx, dw, dbias


def _mix_epilogue(h_f, h_b, o_f, o_b, z, head_norm):
    n = z.shape[0]
    tr = ROW_BLOCK
    spec = pl.BlockSpec((tr, 512), lambda i: (i, 0))

    def body(hf_ref, hb_ref, of_ref, ob_ref, gate_ref, gout_ref, w_ref, y_ref):
        gelu, _ = _gelu_and_grad(gate_ref[...])
        y_ref[:, 0:LRU_W] = ((hf_ref[...] + hb_ref[...]) * gelu).astype(BF16)
        o = of_ref[...] + ob_ref[...]
        gout = gout_ref[...]
        silu = gout * _sigmoid(gout)
        w = w_ref[...]
        for hd in range(GLA_HEADS):
            cs = slice(hd * GLA_DV, (hd + 1) * GLA_DV)
            oh = o[:, cs]
            on = oh * lax.rsqrt(jnp.mean(oh * oh, axis=-1, keepdims=True) + EPS)
            y_ref[:, LRU_W + hd * GLA_DV:LRU_W + (hd + 1) * GLA_DV] = (on * w[:, cs] * silu[:, cs]).astype(BF16)

    return pl.pallas_call(
        body, name="mix_epilogue", grid=(n // tr,),
        in_specs=[spec, spec, spec, spec, pl.BlockSpec((tr, 512), lambda i: (i, 1)),
                  pl.BlockSpec((tr, 512), lambda i: (i, 4)), head_norm.spec],
        out_specs=pl.BlockSpec((tr, D_MODEL), lambda i: (i, 0)),
        out_shape=_sds((n, D_MODEL), BF16),
        compiler_params=_params(("parallel",)),
    )(h_f, h_b, o_f, o_b, z, z, head_norm.array)


def _mix_epilogue_bwd(dymix, h_f, h_b, o_f, o_b, z, head_norm):
    n = z.shape[0]
    tr = ROW_BLOCK
    spec = pl.BlockSpec((tr, 512), lambda i: (i, 0))

    def body(dyl_ref, dyg_ref, hf_ref, hb_ref, of_ref, ob_ref, gate_ref, gout_ref, w_ref,
             dhs_ref, dgate_ref, do_ref, dgout_ref, dw_ref):
        i = pl.program_id(0)
        dyl = dyl_ref[...]
        gelu, dgelu = _gelu_and_grad(gate_ref[...])
        dhs_ref[...] = dyl * gelu
        dgate_ref[...] = (dyl * (hf_ref[...] + hb_ref[...]) * dgelu).astype(BF16)
        dyg = dyg_ref[...]
        o = of_ref[...] + ob_ref[...]
        gout = gout_ref[...]
        sg = _sigmoid(gout)
        silu = gout * sg
        dsilu = sg * (1.0 + gout * (1.0 - sg))
        w = w_ref[...]
        dws = []
        for hd in range(GLA_HEADS):
            cs = slice(hd * GLA_DV, (hd + 1) * GLA_DV)
            oh = o[:, cs]
            rs = lax.rsqrt(jnp.mean(oh * oh, axis=-1, keepdims=True) + EPS)
            on = oh * rs
            dy = dyg[:, cs]
            dgout_ref[:, cs] = (dy * on * w[:, cs] * dsilu[:, cs]).astype(BF16)
            dys = dy * silu[:, cs]
            dws.append(jnp.sum(dys * on, axis=0, keepdims=True))
            don = dys * w[:, cs]
            do_ref[:, cs] = (rs * (don - on * jnp.mean(don * on, axis=-1, keepdims=True))).astype(BF16)
        _accumulate(dw_ref, jnp.concatenate(dws, axis=1), i == 0)

    return pl.pallas_call(
        body, name="mix_epilogue_bwd", grid=(n // tr,),
        in_specs=[pl.BlockSpec((tr, 512), lambda i: (i, 0)), pl.BlockSpec((tr, 512), lambda i: (i, 1)),
                  spec, spec, spec, spec, pl.BlockSpec((tr, 512), lambda i: (i, 1)),
                  pl.BlockSpec((tr, 512), lambda i: (i, 4)), head_norm.spec],
        out_specs=[spec, spec, spec, spec, pl.BlockSpec((1, GLA_W), lambda i: (0, 0))],
        out_shape=[_sds((n, 512), F32)] + [_sds((n, 512), BF16)] * 3 + [_sds((1, GLA_W), F32)],
        compiler_params=_params(("arbitrary",)),
    )(dymix, dymix, h_f, h_b, o_f, o_b, z, z, head_norm.array)


def _out_proj(ymix, w_out, h, g):
    n, d = h.shape
    tr = _row_tile(n)
    spec = pl.BlockSpec((tr, d), lambda i: (i, 0))

    def body(y_ref, w_ref, h_ref, g_ref, mix_ref, hmid_ref):
        mix = jnp.dot(y_ref[...], w_ref[...], preferred_element_type=F32)
        mix_ref[...] = mix
        hmid_ref[...] = h_ref[...] + _rms_fwd(mix, g_ref[...])

    return pl.pallas_call(
        body, name="out_proj", grid=(n // tr,),
        in_specs=[spec, VMEM_SPEC, spec, g.spec],
        out_specs=[spec, spec],
        out_shape=[_sds((n, d), F32), _sds((n, d), F32)],
        compiler_params=_params(("parallel",), 44),
    )(ymix, w_out, h, g.array)


def _out_proj_bwd(dh_mid, mix, g, w_out):
    n, d = mix.shape
    tr = _row_tile(n)
    spec = pl.BlockSpec((tr, d), lambda i: (i, 0))

    def body(dh_ref, mix_ref, g_ref, w_ref, dmix_ref, dy_ref, dg_ref):
        i = pl.program_id(0)
        dmix, dg = _rms_bwd(mix_ref[...], g_ref[...], dh_ref[...])
        dmix = dmix.astype(BF16)
        dmix_ref[...] = dmix
        dy_ref[...] = _dot_nt(dmix, w_ref[...])
        _accumulate(dg_ref, dg, i == 0)

    return pl.pallas_call(
        body, name="out_proj_bwd", grid=(n // tr,),
        in_specs=[spec, spec, g.spec, VMEM_SPEC],
        out_specs=[spec, spec, pl.BlockSpec((1, d), lambda i: (0, 0))],
        out_shape=[_sds((n, d), BF16), _sds((n, d), F32), _sds((1, d), F32)],
        compiler_params=_params(("arbitrary",), 44),
    )(dh_mid, mix, g.array, w_out)


FF_SLAB = D_FF // N_DEV


def _relu_squared(up):
    return jnp.square(jnp.maximum(up.astype(F32), 0.0)).astype(BF16)


def _mlp_fwd(h_mid, g_pre, w_up, w_down, g_post, exchange=None):
    n, d = h_mid.shape
    tr = _row_tile(n, 384)
    spec = pl.BlockSpec((tr, d), lambda i: (i, 0))

    def body(h_ref, gpre_ref, wup_ref, wdn_ref, gpost_ref, hn_ref, up_ref, ff_ref, hout_ref):
        h = h_ref[...]
        hn = _rms_fwd(h, gpre_ref[...]).astype(BF16)
        hn_ref[...] = hn
        ff = jnp.zeros((tr, d), F32)
        for j in range(N_DEV):
            cs = slice(j * FF_SLAB, (j + 1) * FF_SLAB)
            up = jnp.dot(hn, wup_ref[j], preferred_element_type=F32).astype(BF16)
            up_ref[:, cs] = up
            ff = ff + jnp.dot(_relu_squared(up), wdn_ref[cs, :], preferred_element_type=F32)
        ff_ref[...] = ff
        hout_ref[...] = h + _rms_fwd(ff, gpost_ref[...])

    return _hosting_call(
        exchange, body, name="mlp_fwd", grid=(n // tr,),
        in_specs=[spec, g_pre.spec, VMEM_SPEC, VMEM_SPEC, g_post.spec],
        out_specs=[spec, pl.BlockSpec((tr, D_FF), lambda i: (i, 0)), spec, spec],
        out_shape=[_sds((n, d), BF16), _sds((n, D_FF), BF16), _sds((n, d), F32), _sds((n, d), F32)],
        scratch_shapes=[], compiler_params=_params(("arbitrary",), 52),
    )(h_mid, g_pre.array, w_up, w_down, g_post.array)


def _mlp_bwd(dh, ff, up, h_mid, g_pre, w_up, w_down, g_post, exchange=None):
    n, d = h_mid.shape
    tr = _row_tile(n, 384)
    spec = pl.BlockSpec((tr, d), lambda i: (i, 0))
    wide = pl.BlockSpec((tr, D_FF), lambda i: (i, 0))
    gspec = pl.BlockSpec((1, d), lambda i: (0, 0))

    def body(dh_ref, ff_ref, up_ref, h_ref, gpre_ref, wup_ref, wdn_ref, gpost_ref,
             dff_ref, dup_ref, dhmid_ref, dgpost_ref, dgpre_ref):
        i = pl.program_id(0)
        dh = dh_ref[...]
        dff, dgpost = _rms_bwd(ff_ref[...], gpost_ref[...], dh)
        dff = dff.astype(BF16)
        dff_ref[...] = dff
        dhn = jnp.zeros((tr, d), F32)
        for j in range(N_DEV):
            cs = slice(j * FF_SLAB, (j + 1) * FF_SLAB)
            relu = jnp.maximum(up_ref[:, cs].astype(F32), 0.0)
            dact = _dot_nt(dff, wdn_ref[cs, :])
            dup = (dact * 2.0 * relu).astype(BF16)
            dup_ref[:, cs] = dup
            dhn = dhn + _dot_nt(dup, wup_ref[j])
        dx, dgpre = _rms_bwd(h_ref[...], gpre_ref[...], dhn)
        dhmid_ref[...] = dh + dx
        _accumulate(dgpost_ref, dgpost, i == 0)
        _accumulate(dgpre_ref, dgpre, i == 0)

    return _hosting_call(
        exchange, body, name="mlp_bwd", grid=(n // tr,),
        in_specs=[spec, spec, wide, spec, g_pre.spec, VMEM_SPEC, VMEM_SPEC, g_post.spec],
        out_specs=[spec, wide, spec, gspec, gspec],
        out_shape=[_sds((n, d), BF16), _sds((n, D_FF), BF16), _sds((n, d), F32), _sds((1, d), F32), _sds((1, d), F32)],
        scratch_shapes=[], compiler_params=_params(("arbitrary",), 56),
    )(dh, ff, up, h_mid, g_pre.array, w_up, w_down, g_post.array)


def _in_proj_bwd(pieces, w_in, h, g, dh_mid):
    dxbr, dgate, dqk_f, dqk_b, dv_f, dv_b, dgout, dzg_f, dzg_b = pieces
    n, d = h.shape
    tr = _row_tile(n, 384)
    spec = pl.BlockSpec((tr, d), lambda i: (i, 0))
    s512 = pl.BlockSpec((tr, 512), lambda i: (i, 0))
    s128 = pl.BlockSpec((tr, LANES), lambda i: (i, 0))

    def body(a_ref, b_ref, cf_ref, cb_ref, df_ref, db_ref, e_ref, ff_ref, fb_ref, w_ref, h_ref, g_ref, dhm_ref,
             dz_ref, dh_ref, dg_ref):
        i = pl.program_id(0)
        real = (_row_ids(tr, i) >= PAD_ROWS).astype(F32)
        f32 = lambda ref: ref[...].astype(F32)
        dz = jnp.concatenate([f32(a_ref), f32(b_ref), f32(cf_ref) + f32(cb_ref), f32(df_ref) + f32(db_ref),
                              f32(e_ref), f32(ff_ref) + f32(fb_ref)], axis=1) * real
        dz = dz.astype(BF16)
        dz_ref[...] = dz
        dhn = _dot_nt(dz, w_ref[...])
        dx, dg = _rms_bwd(h_ref[...], g_ref[...], dhn)
        dh_ref[...] = (dhm_ref[...] + dx) * real
        _accumulate(dg_ref, dg, i == 0)

    return pl.pallas_call(
        body, name="in_proj_bwd", grid=(n // tr,),
        in_specs=[s512, s512, s512, s512, s512, s512, s512, s128, s128, VMEM_SPEC, spec, g.spec, spec],
        out_specs=[pl.BlockSpec((tr, Z_W), lambda i: (i, 0)), spec, pl.BlockSpec((1, d), lambda i: (0, 0))],
        out_shape=[_sds((n, Z_W), BF16), _sds((n, d), F32), _sds((1, d), F32)],
        compiler_params=_params(("arbitrary",), 48),
    )(dxbr, dgate, dqk_f, dqk_b, dv_f, dv_b, dgout, dzg_f, dzg_b, w_in, h, g.array, dh_mid)


def _matmul_tn(a, b, name, column_slabs=False, exchange=None, a_map=None):
    n, m = a.shape
    k = b.shape[1]
    tr, tm, tk = _row_tile(n), _col_tile(m), _col_tile(k)
    steps = n // tr
    slab = k // N_DEV
    per_step = tk // slab if column_slabs else 1

    def body(a_ref, b_ref, o_ref, acc_ref):
        r = pl.program_id(2)
        a_blk = a_ref[...] if a_map is None else a_map(a_ref[...])
        _accumulate(acc_ref, _dot_tn(a_blk, b_ref[...]), r == 0)

        @pl.when(r == steps - 1)
        def _():
            if column_slabs:
                for j in range(per_step):
                    o_ref[j] = acc_ref[:, j * slab:(j + 1) * slab].astype(BF16)
            else:
                o_ref[...] = acc_ref[...].astype(BF16)

    if column_slabs:
        out_spec = pl.BlockSpec((per_step, tm, slab), lambda mi, ki, r: (ki, mi, 0))
        out_shape = _sds((N_DEV, m, slab), BF16)
    else:
        out_spec = pl.BlockSpec((tm, tk), lambda mi, ki, r: (mi, ki))
        out_shape = _sds((m, k), BF16)
    outs = _hosting_call(
        exchange, body, name=name, grid=(m // tm, k // tk, steps),
        in_specs=[pl.BlockSpec((tr, tm), lambda mi, ki, r: (r, mi)), pl.BlockSpec((tr, tk), lambda mi, ki, r: (r, ki))],
        out_specs=[out_spec], out_shape=[out_shape], scratch_shapes=[pltpu.VMEM((tm, tk), F32)],
        compiler_params=_params(("arbitrary", "arbitrary", "arbitrary"), 40),
    )(a, b)
    return outs[0] if exchange is None else outs


def _loss_and_grad(h_out, target):
    n, d = h_out.shape
    tr = ROW_BLOCK
    first = (PAD_ROWS + N_META) // tr

    def body(h_ref, t_ref, dh_ref, loss_ref):
        i = pl.program_id(0)
        real = jnp.where(i >= first, 1.0, 0.0)
        diff = (h_ref[...] - t_ref[...]) * real
        dh_ref[...] = diff * (1.0 / d)
        part = 0.5 * jnp.sum(jnp.mean(diff * diff, axis=-1, keepdims=True), axis=0, keepdims=True)
        _accumulate(loss_ref, jnp.broadcast_to(part, (1, LANES)), i == 0)

    return pl.pallas_call(
        body, name="loss_and_grad", grid=(n // tr,),
        in_specs=[pl.BlockSpec((tr, d), lambda i: (i, 0)), pl.BlockSpec((tr, d), lambda i: (jnp.maximum(i - first, 0), 0))],
        out_specs=[pl.BlockSpec((tr, d), lambda i: (i, 0)), pl.BlockSpec((1, LANES), lambda i: (0, 0))],
        out_shape=[_sds((n, d), F32), _sds((1, LANES), F32)],
        compiler_params=_params(("arbitrary",)),
    )(h_out, target)


SUBLANES = 8


def _scan_rows(a, u, reverse, window=None):
    n = a.shape[0]
    window = window or n
    pos = lax.broadcasted_iota(jnp.int32, (n, 1), 0) & (window - 1) if window < n else lax.broadcasted_iota(jnp.int32, (n, 1), 0)
    d = 1
    while d < window:
        shift = n - d if reverse else d
        keep = (pos < window - d) if reverse else (pos >= d)
        a_s = pltpu.roll(a, shift, 0)
        u_s = pltpu.roll(u, shift, 0)
        u = jnp.where(keep, a * u_s + u, u)
        a = jnp.where(keep, a * a_s, a)
        d *= 2
    return a, u


def _scan_block(a, u, h_in, reverse, stage_ref):
    n, width = a.shape
    groups = n // SUBLANES
    lanes = [slice(cb * LANES, (cb + 1) * LANES) for cb in range(width // LANES)]
    a1, u1 = _scan_rows(a, u, reverse, window=SUBLANES)
    for cb, cs in enumerate(lanes):
        stage_ref[0, cb] = a1[:, cs]
        stage_ref[1, cb] = u1[:, cs]
    edge = 0 if reverse else SUBLANES - 1
    group_rows = pl.ds(edge, groups, stride=SUBLANES)
    a2, u2 = _scan_rows(jnp.concatenate([stage_ref[0, cb, group_rows, :] for cb in range(len(lanes))], axis=1),
                        jnp.concatenate([stage_ref[1, cb, group_rows, :] for cb in range(len(lanes))], axis=1), reverse)
    leaving = a2 * h_in + u2
    grow = lax.broadcasted_iota(jnp.int32, (groups, 1), 0)
    if reverse:
        entering = jnp.where(grow == groups - 1, h_in, pltpu.roll(leaving, groups - 1, 0))
    else:
        entering = jnp.where(grow == 0, h_in, pltpu.roll(leaving, 1, 0))
    for cb, cs in enumerate(lanes):
        for k in range(SUBLANES):
            stage_ref[0, cb, pl.ds(k, groups, stride=SUBLANES), :] = entering[:, cs]
    entering_rows = jnp.concatenate([stage_ref[0, cb] for cb in range(len(lanes))], axis=1)
    return a1 * entering_rows + u1


def _lru_gates(xc, wcat_ref, bias_ref, lam_ref):
    nl = -lam_ref[...]
    nsp = -LRU_C * (jnp.maximum(nl, 0.0) + jnp.log(1.0 + jnp.exp(-jnp.abs(nl))))
    pre = _dot(xc, wcat_ref[...]) + bias_ref[...]
    r = _sigmoid(pre[:, :LRU_W])
    ig = _sigmoid(pre[:, LRU_W:])
    log_a = r * nsp
    a = jnp.exp(log_a)
    m2 = _one_minus_square(a, log_a)
    inv_m = lax.rsqrt(jnp.maximum(m2, 1e-30))
    return r, ig, a, m2 * inv_m, inv_m, nsp


def _lru_scan(xc, wcat, bias, lam, reverse):
    n = xc.shape[0]
    nb = n // ROW_BLOCK
    order = (lambda i: nb - 1 - i) if reverse else (lambda i: i)
    spec = pl.BlockSpec((ROW_BLOCK, LRU_W), lambda i: (order(i), 0))
    edge = 0 if reverse else ROW_BLOCK - 1

    def body(xc_ref, wcat_ref, bias_ref, lam_ref, h_ref, carry_ref, stage_ref):
        i = pl.program_id(0)

        @pl.when(i == 0)
        def _():
            carry_ref[...] = jnp.zeros_like(carry_ref)

        xc = xc_ref[...]
        r, ig, a, m, _, _ = _lru_gates(xc, wcat_ref, bias_ref, lam_ref)
        u = jnp.where(_row_ids(ROW_BLOCK, order(i)) >= PAD_ROWS, m * (ig * xc), 0.0)
        h_ref[...] = _scan_block(a, u, carry_ref[0:1, :], reverse, stage_ref)
        carry_ref[0:1, :] = h_ref[pl.ds(edge, 1), :]

    return pl.pallas_call(
        body, name="lru_scan_b" if reverse else "lru_scan_f", grid=(nb,),
        in_specs=[spec, wcat.spec, bias.spec, lam.spec],
        out_specs=spec,
        out_shape=_sds((n, LRU_W), F32),
        scratch_shapes=[pltpu.VMEM((8, LRU_W), F32), pltpu.VMEM((2, LRU_W // LANES, ROW_BLOCK, LANES), F32)],
        compiler_params=_params(("arbitrary",)),
    )(xc, wcat.array, bias.array, lam.array)


def _lru_scan_bwd(dhs, xc, h, wcat, bias, lam, reverse):
    n = xc.shape[0]
    nb = n // ROW_BLOCK
    per = ROW_BLOCK // 8
    order = (lambda i: i) if reverse else (lambda i: nb - 1 - i)
    spec = pl.BlockSpec((ROW_BLOCK, LRU_W), lambda i: (order(i), 0))
    if reverse:
        halo = pl.BlockSpec((8, LRU_W), lambda i: (jnp.minimum((order(i) + 1) * per, nb * per - 1), 0))
    else:
        halo = pl.BlockSpec((8, LRU_W), lambda i: (jnp.maximum(order(i) * per - 1, 0), 0))
    edge = ROW_BLOCK - 1 if reverse else 0

    def body(dhs_ref, xc_ref, h_ref, halo_ref, wcat_ref, bias_ref, lam_ref,
             dxc_ref, dw_ref, db_ref, dlam_ref, cdh_ref, ca_ref, tmp_ref, stage_ref):
        i = pl.program_id(0)
        ib = order(i)

        @pl.when(i == 0)
        def _():
            cdh_ref[...] = jnp.zeros_like(cdh_ref)
            ca_ref[...] = jnp.zeros_like(ca_ref)

        xc = xc_ref[...]
        r, ig, a, m, inv_m, nsp = _lru_gates(xc, wcat_ref, bias_ref, lam_ref)
        row = lax.broadcasted_iota(jnp.int32, (ROW_BLOCK, 1), 0)
        if reverse:
            coef = jnp.where(row == 0, ca_ref[0:1, :], pltpu.roll(a, 1, 0))
            h_nb = jnp.where(row == ROW_BLOCK - 1, halo_ref[0:1, :] * jnp.where(ib < nb - 1, 1.0, 0.0),
                             pltpu.roll(h_ref[...], ROW_BLOCK - 1, 0))
        else:
            coef = jnp.where(row == ROW_BLOCK - 1, ca_ref[0:1, :], pltpu.roll(a, ROW_BLOCK - 1, 0))
            h_nb = jnp.where(row == 0, halo_ref[7:8, :] * jnp.where(ib > 0, 1.0, 0.0), pltpu.roll(h_ref[...], 1, 0))
        dh = _scan_block(coef, dhs_ref[...], cdh_ref[0:1, :], not reverse, stage_ref)
        tmp_ref[...] = dh
        cdh_ref[0:1, :] = tmp_ref[pl.ds(edge, 1), :]
        tmp_ref[...] = a
        ca_ref[0:1, :] = tmp_ref[pl.ds(edge, 1), :]

        du = jnp.where(_row_ids(ROW_BLOCK, ib) >= PAD_ROWS, dh, 0.0)
        da = dh * h_nb
        dm = du * (ig * xc)
        di = du * (m * xc)
        dlog_a = da * a - dm * (a * a) * inv_m
        dr = dlog_a * nsp
        dpre = jnp.concatenate([dr * r * (1.0 - r), di * ig * (1.0 - ig)], axis=1)
        dxc_ref[...] = du * (m * ig) + _dot_nt(dpre, wcat_ref[...])
        _accumulate(dw_ref, _dot_tn(xc, dpre), i == 0)
        _accumulate(db_ref, jnp.sum(dpre, axis=0, keepdims=True), i == 0)
        _accumulate(dlam_ref, jnp.sum(dlog_a * r, axis=0, keepdims=True), i == 0)

        @pl.when(i == nb - 1)
        def _():
            dlam_ref[...] = dlam_ref[...] * (LRU_C * _sigmoid(-lam_ref[...]))

    return pl.pallas_call(
        body, name="lru_scan_bwd_b" if reverse else "lru_scan_bwd_f", grid=(nb,),
        in_specs=[spec, spec, spec, halo, wcat.spec, bias.spec, lam.spec],
        out_specs=[spec, pl.BlockSpec((LRU_W, 2 * LRU_W), lambda i: (0, 0)),
                   pl.BlockSpec((1, 2 * LRU_W), lambda i: (0, 0)), pl.BlockSpec((1, LRU_W), lambda i: (0, 0))],
        out_shape=[_sds((n, LRU_W), F32), _sds((LRU_W, 2 * LRU_W), F32), _sds((1, 2 * LRU_W), F32), _sds((1, LRU_W), F32)],
        scratch_shapes=[pltpu.VMEM((8, LRU_W), F32), pltpu.VMEM((8, LRU_W), F32), pltpu.VMEM((ROW_BLOCK, LRU_W), F32),
                        pltpu.VMEM((2, LRU_W // LANES, ROW_BLOCK, LANES), F32)],
        compiler_params=_params(("arbitrary",)),
    )(dhs, xc, h, h, wcat.array, bias.array, lam.array)


def _gla_rows(n):
    return 384 if n % 384 == 0 else ROW_BLOCK


def _gla_masks(reverse):
    t = lax.broadcasted_iota(jnp.int32, (CHUNK, CHUNK), 0)
    s = lax.broadcasted_iota(jnp.int32, (CHUNK, CHUNK), 1)
    if reverse:
        return (s >= t).astype(F32), s > t
    return (s <= t).astype(F32), s <= t


def _gla_gate(zg, wg_ref, bg_ref):
    pre = _dot(zg, wg_ref[...]) + bg_ref[...]
    g = (jnp.minimum(pre, 0.0) - jnp.log(1.0 + jnp.exp(-jnp.abs(pre)))) * (1.0 / GATE_NORM)
    return pre, g


def _gla_decays(gc, tri):
    b = jnp.dot(tri, gc, precision=lax.Precision.HIGHEST, preferred_element_type=F32)
    b_last = jnp.sum(gc, axis=0, keepdims=True)
    return jnp.exp(b), jnp.exp(-b), jnp.exp(b_last - b), jnp.exp(b_last)


def _gla_scan(z, wg, bg, reverse, exchange=None):
    n = z.shape[0]
    rb = _gla_rows(n)
    nb = n // rb
    cpb = rb // CHUNK
    order = (lambda i: nb - 1 - i) if reverse else (lambda i: i)
    chunks = range(cpb - 1, -1, -1) if reverse else range(cpb)

    def body(qk_ref, v_ref, zg_ref, wg_ref, bg_ref, o_ref, sall_ref, s_ref):
        i = pl.program_id(0)

        @pl.when(i == 0)
        def _():
            s_ref[...] = jnp.zeros_like(s_ref)

        tri, mask = _gla_masks(reverse)
        _, g = _gla_gate(zg_ref[...], wg_ref, bg_ref)
        heads = range(GLA_HEADS)
        ks = [slice(hd * GLA_DK, (hd + 1) * GLA_DK) for hd in heads]
        vs = [slice(hd * GLA_DV, (hd + 1) * GLA_DV) for hd in heads]
        qh, kb, v, el, p, intra, kv = {}, {}, {}, {}, {}, {}, {}
        for c in chunks:
            rows = slice(c * CHUNK, (c + 1) * CHUNK)
            eb, enb, ebl, el[c] = _gla_decays(g[rows], tri)
            qk = qk_ref[rows, :]
            q_all = (qk[:, :GLA_QK] * (GLA_DK ** -0.5) * eb).astype(BF16)
            k_all = (qk[:, GLA_QK:] * enb).astype(BF16)
            kb_all = (qk[:, GLA_QK:] * ebl).astype(BF16)
            v_all = v_ref[rows, :].astype(BF16)
            for hd in heads:
                qh[c, hd], kb[c, hd], v[c, hd] = q_all[:, ks[hd]], kb_all[:, ks[hd]], v_all[:, vs[hd]]
                p[c, hd] = _dot_nt(qh[c, hd], k_all[:, ks[hd]])
        for c in chunks:
            for hd in heads:
                intra[c, hd] = _dot(jnp.where(mask, p[c, hd], 0.0), v[c, hd])
                kv[c, hd] = _dot_tn(v[c, hd], kb[c, hd])
        state = [s_ref[:, ks[hd]] for hd in heads]
        for c in chunks:
            rows = slice(c * CHUNK, (c + 1) * CHUNK)
            for hd in heads:
                sall_ref[c, :, ks[hd]] = state[hd]
                o_ref[rows, vs[hd]] = intra[c, hd] + _dot_nt(qh[c, hd], state[hd])
                state[hd] = state[hd] * el[c][:, ks[hd]] + kv[c, hd]
        for hd in heads:
            s_ref[:, ks[hd]] = state[hd]

    return _hosting_call(
        exchange, body, name="gla_scan_b" if reverse else "gla_scan_f", grid=(nb,),
        in_specs=[pl.BlockSpec((rb, 512), lambda i: (order(i), 2)), pl.BlockSpec((rb, 512), lambda i: (order(i), 3)),
                  pl.BlockSpec((rb, LANES), lambda i: (order(i), ZG_COL_BLOCK)), wg.spec, bg.spec],
        out_specs=[pl.BlockSpec((rb, GLA_W), lambda i: (order(i), 0)),
                   pl.BlockSpec((cpb, GLA_DV, GLA_QK), lambda i: (order(i), 0, 0))],
        out_shape=[_sds((n, GLA_W), F32), _sds((n // CHUNK, GLA_DV, GLA_QK), F32)],
        scratch_shapes=[pltpu.VMEM((GLA_DV, GLA_QK), F32)],
        compiler_params=_params(("arbitrary",)),
    )(z, z, z, wg.array, bg.array)


def _gla_scan_bwd(do, z, states, wg, bg, reverse, exchange=None):
    n = z.shape[0]
    rb = _gla_rows(n)
    nb = n // rb
    cpb = rb // CHUNK
    order = (lambda i: i) if reverse else (lambda i: nb - 1 - i)
    chunks = range(cpb) if reverse else range(cpb - 1, -1, -1)

    def body(do_ref, qk_ref, v_ref, zg_ref, sall_ref, wg_ref, bg_ref,
             dqk_ref, dv_ref, dzg_ref, dwg_ref, dbg_ref, ds_ref):
        i = pl.program_id(0)

        @pl.when(i == 0)
        def _():
            ds_ref[...] = jnp.zeros_like(ds_ref)

        tri, mask = _gla_masks(reverse)
        tri_t, _ = _gla_masks(not reverse)
        zg = zg_ref[...]
        pre, g = _gla_gate(zg, wg_ref, bg_ref)
        heads = range(GLA_HEADS)
        ks = [slice(hd * GLA_DK, (hd + 1) * GLA_DK) for hd in heads]
        vs = [slice(hd * GLA_DV, (hd + 1) * GLA_DV) for hd in heads]
        dec, full, qh, kh, kb, v, dout, p, dp = {}, {}, {}, {}, {}, {}, {}, {}, {}
        for c in chunks:
            rows = slice(c * CHUNK, (c + 1) * CHUNK)
            dec[c] = _gla_decays(g[rows], tri)
            eb, enb, ebl, _ = dec[c]
            qk = qk_ref[rows, :]
            q_f = qk[:, :GLA_QK] * (GLA_DK ** -0.5) * eb
            k_f = qk[:, GLA_QK:] * enb
            kb_f = qk[:, GLA_QK:] * ebl
            full[c] = (q_f, k_f, kb_f)
            q_all, k_all, kb_all = q_f.astype(BF16), k_f.astype(BF16), kb_f.astype(BF16)
            v_all, do_all = v_ref[rows, :].astype(BF16), do_ref[rows, :].astype(BF16)
            for hd in heads:
                qh[c, hd], kh[c, hd], kb[c, hd] = q_all[:, ks[hd]], k_all[:, ks[hd]], kb_all[:, ks[hd]]
                v[c, hd], dout[c, hd] = v_all[:, vs[hd]], do_all[:, vs[hd]]
                p[c, hd] = _dot_nt(qh[c, hd], kh[c, hd])
                dp[c, hd] = _dot_nt(dout[c, hd], v[c, hd])
        dv_i, dqh, dkh, dsq, state = {}, {}, {}, {}, {}
        for c in chunks:
            for hd in heads:
                pm = jnp.where(mask, p[c, hd], 0.0).astype(BF16)
                dpm = jnp.where(mask, dp[c, hd], 0.0).astype(BF16)
                state[c, hd] = sall_ref[c, :, ks[hd]]
                dv_i[c, hd] = _dot_tn(pm, dout[c, hd])
                dqh[c, hd] = _dot(dpm, kh[c, hd]) + _dot(dout[c, hd], state[c, hd])
                dkh[c, hd] = _dot_tn(dpm, qh[c, hd])
                dsq[c, hd] = _dot_tn(dout[c, hd], qh[c, hd])
        dstate = [ds_ref[:, ks[hd]] for hd in heads]
        dkb, sds = {}, {}
        for c in chunks:
            rows = slice(c * CHUNK, (c + 1) * CHUNK)
            el = dec[c][3]
            for hd in heads:
                dv_ref[rows, vs[hd]] = (dv_i[c, hd] + _dot_nt(kb[c, hd], dstate[hd])).astype(BF16)
                dkb[c, hd] = _dot(v[c, hd], dstate[hd])
                sds[c, hd] = jnp.sum(state[c, hd] * dstate[hd], axis=0, keepdims=True)
                dstate[hd] = dstate[hd] * el[:, ks[hd]] + dsq[c, hd]
        for hd in heads:
            ds_ref[:, ks[hd]] = dstate[hd]
        dgs = [None] * cpb
        for c in chunks:
            rows = slice(c * CHUNK, (c + 1) * CHUNK)
            eb, enb, ebl, el = dec[c]
            q_f, k_f, kb_f = full[c]
            dqh_c = jnp.concatenate([dqh[c, hd] for hd in heads], axis=1)
            dkh_c = jnp.concatenate([dkh[c, hd] for hd in heads], axis=1)
            dkb_c = jnp.concatenate([dkb[c, hd] for hd in heads], axis=1)
            sds_c = jnp.concatenate([sds[c, hd] for hd in heads], axis=1)
            dqk_ref[rows, :] = jnp.concatenate([dqh_c * eb * (GLA_DK ** -0.5), dkh_c * enb + dkb_c * ebl], axis=1).astype(BF16)
            dkb_kb = dkb_c * kb_f
            db = dqh_c * q_f - dkh_c * k_f - dkb_kb
            db_last = el * sds_c + jnp.sum(dkb_kb, axis=0, keepdims=True)
            dgs[c] = jnp.dot(tri_t, db, precision=lax.Precision.HIGHEST, preferred_element_type=F32) + db_last
        dg = jnp.concatenate(dgs, axis=0)
        dpre = dg * _sigmoid(-pre) * (1.0 / GATE_NORM)
        dzg_ref[...] = _dot_nt(dpre, wg_ref[...]).astype(BF16)
        _accumulate(dwg_ref, _dot_tn(zg, dpre), i == 0)
        _accumulate(dbg_ref, jnp.sum(dpre, axis=0, keepdims=True), i == 0)

    return _hosting_call(
        exchange, body, name="gla_scan_bwd_b" if reverse else "gla_scan_bwd_f", grid=(nb,),
        in_specs=[pl.BlockSpec((rb, GLA_W), lambda i: (order(i), 0)),
                  pl.BlockSpec((rb, 512), lambda i: (order(i), 2)), pl.BlockSpec((rb, 512), lambda i: (order(i), 3)),
                  pl.BlockSpec((rb, LANES), lambda i: (order(i), ZG_COL_BLOCK)),
                  pl.BlockSpec((cpb, GLA_DV, GLA_QK), lambda i: (order(i), 0, 0)), wg.spec, bg.spec],
        out_specs=[pl.BlockSpec((rb, 512), lambda i: (order(i), 0)), pl.BlockSpec((rb, 512), lambda i: (order(i), 0)),
                   pl.BlockSpec((rb, LANES), lambda i: (order(i), 0)),
                   pl.BlockSpec((LANES, GLA_QK), lambda i: (0, 0)), pl.BlockSpec((1, GLA_QK), lambda i: (0, 0))],
        out_shape=[_sds((n, 512), BF16), _sds((n, 512), BF16), _sds((n, LANES), BF16), _sds((LANES, GLA_QK), F32),
                   _sds((1, GLA_QK), F32)],
        scratch_shapes=[pltpu.VMEM((GLA_DV, GLA_QK), F32)],
        compiler_params=_params(("arbitrary",)),
    )(do, z, z, z, states, wg.array, bg.array)


NORM_NAMES = ("norm_mix_pre", "norm_mix_post", "norm_mlp_pre", "norm_mlp_post")
VEC512_NAMES = ("conv_b", "lru_ba_f", "lru_bx_f", "lru_lambda_f", "lru_ba_b", "lru_bx_b", "lru_lambda_b", "gla_head_norm")
VEC256_NAMES = ("gla_bg_f", "gla_bg_b")
LRU_MAT_NAMES = ("lru_wa_f", "lru_wx_f", "lru_wa_b", "lru_wx_b")
DIRS = ("f", "b")


def _prepare_params(w, gathered, depth):
    row_names = NORM_NAMES + ("conv_b", "gla_head_norm")
    ins = ([w[nm] for nm in row_names] + [w["lru_ba_" + d] for d in DIRS] + [w["lru_bx_" + d] for d in DIRS]
           + [w["lru_lambda_" + d] for d in DIRS] + [w["gla_bg_" + d] for d in DIRS]
           + [w["lru_wa_" + d].reshape(depth, LRU_W, LRU_HD) for d in DIRS]
           + [w["lru_wx_" + d].reshape(depth, LRU_W, LRU_HD) for d in DIRS]
           + [gathered["conv_w"], gathered["gla_wg_f"], gathered["gla_wg_b"], gathered["meta_tokens"]])
    n_rows = len(row_names)

    def body(*refs):
        rows_in = refs[:n_rows]
        ba, bx, lam, bg, wa, wx = (refs[n_rows + 2 * t:n_rows + 2 * t + 2] for t in range(6))
        convw_g, wgf_g, wgb_g, meta_g = refs[n_rows + 12:n_rows + 16]
        outs = refs[n_rows + 16:]
        rows_out = outs[:n_rows]
        convw, wcat, bias, lam_o, wg, bg_o, meta = outs[n_rows:]
        for l in range(depth):
            for src, dst in zip(rows_in, rows_out):
                dst[l] = src[pl.ds(l, 1), :]
            convw[l] = jnp.zeros((8, LRU_W), F32)
            for j in range(N_DEV):
                convw[l, 0:4, j * 64:(j + 1) * 64] = convw_g[j, l]
            for d in range(2):
                wcat[l, d] = jnp.zeros((LRU_W, 2 * LRU_W), BF16)
                for hd in range(LRU_HEADS):
                    rs = slice(hd * LRU_HD, (hd + 1) * LRU_HD)
                    wcat[l, d, rs, hd * LRU_HD:(hd + 1) * LRU_HD] = wa[d][l, rs, :].astype(BF16)
                    wcat[l, d, rs, LRU_W + hd * LRU_HD:LRU_W + (hd + 1) * LRU_HD] = wx[d][l, rs, :].astype(BF16)
                bias[l, d, :, 0:LRU_W] = ba[d][pl.ds(l, 1), :]
                bias[l, d, :, LRU_W:2 * LRU_W] = bx[d][pl.ds(l, 1), :]
                lam_o[l, d] = lam[d][pl.ds(l, 1), :]
                bg_o[l, d] = bg[d][pl.ds(l, 1), :]
                wg[l, d] = jnp.zeros((LANES, GLA_QK), BF16)
                src = wgf_g if d == 0 else wgb_g
                for j in range(N_DEV):
                    wg[l, d, d * GLA_RANK:(d + 1) * GLA_RANK, j * 32:(j + 1) * 32] = src[j, l].astype(BF16)
        for j in range(N_DEV):
            meta[:, j * LANES:(j + 1) * LANES] = meta_g[j]

    out_shape = ([_sds((depth, 1, w[nm].shape[1]), F32) for nm in row_names]
                 + [_sds((depth, 8, LRU_W), F32), _sds((depth, 2, LRU_W, 2 * LRU_W), BF16), _sds((depth, 2, 1, 2 * LRU_W), F32),
                    _sds((depth, 2, 1, LRU_W), F32), _sds((depth, 2, LANES, GLA_QK), BF16), _sds((depth, 2, 1, GLA_QK), F32),
                    _sds((N_META, D_MODEL), F32)])
    outs = pl.pallas_call(
        body, name="prepare_params", in_specs=[VMEM_SPEC] * len(ins), out_specs=[VMEM_SPEC] * len(out_shape),
        out_shape=out_shape, compiler_params=_params(None, 32),
    )(*ins)
    prepared = dict(zip(row_names, outs[:n_rows]))
    prepared.update(zip(("conv_w", "wcat", "lru_bias", "lru_lam", "wg", "gla_bg", "meta_tokens"), outs[n_rows:]))
    return prepared


class _WeightGather:
    def __init__(self, shards, p):
        self.shards, self.p = shards, p

    def exchange(self, items):
        ex = _Exchange()
        for nm, l in items:
            ex.add(self.shards[nm], _layer_of(l), _sds((N_DEV,) + self.shards[nm].shape[1:], BF16), _slab)
        return ex

    def install(self, items, landed):
        for (nm, l), g in zip(items, landed):
            if nm == "w_in":
                g = jnp.pad(jnp.concatenate([g[j] for j in range(N_DEV)], axis=1), ((0, 0), (0, Z_W - D_IN)))
            elif nm == "w_out":
                g = g.reshape(D_MODEL, D_MODEL)
            elif nm == "w_mlp_down":
                g = g.reshape(D_FF, D_MODEL)
            self.p.setdefault(nm, {})[l] = g


class _GradOutbox:
    def __init__(self):
        self.pending, self.received = [], {}

    def put(self, nm, l, slabs):
        self.pending.append((nm, l, slabs))

    def exchange(self, only=None):
        ex, keys, rest = _Exchange(), [], []
        for nm, l, slabs in self.pending:
            if only is None or nm == only:
                ex.add(slabs, _slab, _sds(slabs.shape, slabs.dtype), _slab)
                keys.append((nm, l))
            else:
                rest.append((nm, l, slabs))
        self.pending = rest
        return ex, keys

    def store(self, keys, landed):
        self.received.update(zip(keys, landed))


def _layer_fwd(h, l, p, gather, depth):
    lp = lambda name, *index: _LayerParam(p[name], l, *index)
    hn, z = _norm_in_proj(h, lp("norm_mix_pre"), p["w_in"][l])
    xc = _conv_fwd(z, lp("conv_w"), lp("conv_b"))
    s = dict(h=h, hn=hn, z=z, xc=xc)
    for d, name in enumerate(DIRS):
        s["h_" + name] = _lru_scan(xc, lp("wcat", d), lp("lru_bias", d), lp("lru_lam", d), d == 1)
        items = [(("w_mlp_up", "w_mlp_down")[d], l)]
        s["o_" + name], s["s_" + name], *landed = _gla_scan(z, lp("wg", d), lp("gla_bg", d), d == 1, gather.exchange(items))
        gather.install(items, landed)
    s["ymix"] = _mix_epilogue(s["h_f"], s["h_b"], s["o_f"], s["o_b"], z, lp("gla_head_norm"))
    s["mix"], s["h_mid"] = _out_proj(s["ymix"], p["w_out"][l], h, lp("norm_mix_post"))
    items = [("w_in", l + 1), ("w_out", l + 1)] if l + 1 < depth else []
    s["hn2"], s["up"], s["ff"], h_out, *landed = _mlp_fwd(
        s["h_mid"], lp("norm_mlp_pre"), p["w_mlp_up"][l], p["w_mlp_down"][l], lp("norm_mlp_post"), gather.exchange(items))
    gather.install(items, landed)
    return h_out, s


def _layer_bwd(dh_out, l, p, s, outbox):
    lp = lambda name, *index: _LayerParam(p[name], l, *index)
    g = {}
    ex, keys = outbox.exchange()
    d_ff, dup, dh_mid, g["norm_mlp_post"], g["norm_mlp_pre"], *landed = _mlp_bwd(
        dh_out, s["ff"], s["up"], s["h_mid"], lp("norm_mlp_pre"), p["w_mlp_up"][l], p["w_mlp_down"][l], lp("norm_mlp_post"), ex)
    outbox.store(keys, landed)
    outbox.put("w_mlp_down", l, _matmul_tn(s["up"], d_ff, "grad_w_down", a_map=_relu_squared)
               .reshape(N_DEV, D_FF // N_DEV, D_MODEL))
    outbox.put("w_mlp_up", l, _matmul_tn(s["hn2"], dup, "grad_w_up", column_slabs=True))
    dmix, dymix, g["norm_mix_post"] = _out_proj_bwd(dh_mid, s["mix"], lp("norm_mix_post"), p["w_out"][l])
    outbox.put("w_out", l, _matmul_tn(s["ymix"], dmix, "grad_w_out").reshape(N_DEV, D_MODEL // N_DEV, D_MODEL))
    dhs, dgate, do, dgout, g["gla_head_norm"] = _mix_epilogue_bwd(
        dymix, s["h_f"], s["h_b"], s["o_f"], s["o_b"], s["z"], lp("gla_head_norm"))
    dqk, dv, dzg, dxc = {}, {}, {}, {}
    for d, name in enumerate(DIRS):
        ex, keys = outbox.exchange(only=("w_mlp_down", "w_mlp_up")[d])
        dqk[name], dv[name], dzg[name], g["wg_" + name], g["gla_bg_" + name], *landed = _gla_scan_bwd(
            do, s["z"], s["s_" + name], lp("wg", d), lp("gla_bg", d), d == 1, ex)
        outbox.store(keys, landed)
        dxc[name], g["wcat_" + name], g["lru_bias_" + name], g["lru_lambda_" + name] = _lru_scan_bwd(
            dhs, s["xc"], s["h_" + name], lp("wcat", d), lp("lru_bias", d), lp("lru_lam", d), d == 1)
    dxbr, g["conv_w"], g["conv_b"] = _conv_bwd(dxc["f"], dxc["b"], s["z"], lp("conv_w"))
    dz, dh_in, g["norm_mix_pre"] = _in_proj_bwd(
        (dxbr, dgate, dqk["f"], dqk["b"], dv["f"], dv["b"], dgout, dzg["f"], dzg["b"]),
        p["w_in"][l], s["h"], lp("norm_mix_pre"), dh_mid)
    return dh_in, g, dz


def _w_in_slabs(grad_w_in):
    shard = D_IN // N_DEV
    return jnp.stack([grad_w_in[:, j * shard:(j + 1) * shard] for j in range(N_DEV)])


def _folded_block(hd):
    return slice((hd // 2) * LRU_HD, (hd // 2 + 1) * LRU_HD), slice((hd % 2) * LRU_HD, (hd % 2 + 1) * LRU_HD)


def _pack_small_grads(grads, dh0, depth):
    per_layer = ("norm_mix_pre", "norm_mix_post", "norm_mlp_pre", "norm_mlp_post", "conv_b", "gla_head_norm",
                 "lru_bias_f", "lru_bias_b", "lru_lambda_f", "lru_lambda_b", "gla_bg_f", "gla_bg_b",
                 "wcat_f", "wcat_b", "conv_w", "wg_f", "wg_b")
    ins = [grads[l][nm] for l in range(depth) for nm in per_layer]
    k = len(per_layer)
    meta_rows = PAD_ROWS // N_META

    def body(*refs):
        g = [dict(zip(per_layer, refs[l * k:(l + 1) * k])) for l in range(depth)]
        dh0_ref = refs[depth * k]
        norms, v512, v256, mats, convw, wgf, wgb, meta = refs[depth * k + 1:]
        v256[...] = jnp.zeros_like(v256)
        for l in range(depth):
            for p_, nm in enumerate(NORM_NAMES):
                norms[pl.ds(2 * p_ + l, 1), :] = g[l][nm][...]
            rows512 = [g[l]["conv_b"][...], g[l]["lru_bias_f"][:, 0:LRU_W], g[l]["lru_bias_f"][:, LRU_W:2 * LRU_W],
                       g[l]["lru_lambda_f"][...], g[l]["lru_bias_b"][:, 0:LRU_W], g[l]["lru_bias_b"][:, LRU_W:2 * LRU_W],
                       g[l]["lru_lambda_b"][...], g[l]["gla_head_norm"][...]]
            for p_, row in enumerate(rows512):
                v512[pl.ds(2 * p_ + l, 1), :] = row
            for p_, nm in enumerate(("gla_bg_f", "gla_bg_b")):
                v256[pl.ds(2 * p_ + l, 1), :] = g[l][nm][...]
            for d, name in enumerate(DIRS):
                for hd in range(LRU_HEADS):
                    rs = slice(hd * LRU_HD, (hd + 1) * LRU_HD)
                    dst_rows, dst_cols = _folded_block(hd)
                    mats[2 * d, l, dst_rows, dst_cols] = g[l]["wcat_" + name][rs, hd * LRU_HD:(hd + 1) * LRU_HD].astype(BF16)
                    mats[2 * d + 1, l, dst_rows, dst_cols] = (
                        g[l]["wcat_" + name][rs, LRU_W + hd * LRU_HD:LRU_W + (hd + 1) * LRU_HD].astype(BF16))
            for j in range(N_DEV):
                convw[j, l] = g[l]["conv_w"][0:4, j * 64:(j + 1) * 64]
                wgf[j, l] = g[l]["wg_f"][0:GLA_RANK, j * 32:(j + 1) * 32]
                wgb[j, l] = g[l]["wg_b"][GLA_RANK:2 * GLA_RANK, j * 32:(j + 1) * 32]
        for j in range(N_DEV):
            meta[j] = dh0_ref[:, j * LANES:(j + 1) * LANES]

    out_shape = [_sds((8, D_MODEL), F32), _sds((16, LRU_W), F32), _sds((8, GLA_QK), F32),
                 _sds((4, depth, LRU_W // 2, 2 * LRU_HD), BF16),
                 _sds((N_DEV, depth, 4, 64), F32), _sds((N_DEV, depth, GLA_RANK, 32), F32), _sds((N_DEV, depth, GLA_RANK, 32), F32),
                 _sds((N_DEV, N_META, LANES), F32)]
    return pl.pallas_call(
        body, name="pack_small_grads", grid=(1,),
        in_specs=[VMEM_SPEC] * (depth * k) + [pl.BlockSpec((N_META, D_MODEL), lambda i: (meta_rows, 0))],
        out_specs=[VMEM_SPEC] * len(out_shape), out_shape=out_shape, compiler_params=_params(("arbitrary",), 32),
    )(*ins, dh0)


def _my_index():
    return 4 * lax.axis_index("x") + 2 * lax.axis_index("y") + lax.axis_index("c")


def _peer(k):
    x, y, c = lax.axis_index("x"), lax.axis_index("y"), lax.axis_index("c")
    px = x ^ ((k >> 2) & 1)
    py = y ^ ((k >> 1) & 1)
    pc = c ^ (k & 1)
    return (px, py, pc), 4 * px + 2 * py + pc


class _Exchange:
    def __init__(self):
        self.inputs, self.out_shapes, self.transfers = [], [], []

    def add(self, array, src, out_shape, dst):
        self.transfers.append((len(self.inputs), src, len(self.out_shapes), dst))
        self.inputs.append(array)
        self.out_shapes.append(out_shape)
        return len(self.out_shapes) - 1

    def sem_shapes(self):
        nsem = len(self.transfers) * (N_DEV - 1)
        return [pltpu.SemaphoreType.DMA((nsem,)), pltpu.SemaphoreType.DMA((nsem,)),
                pltpu.SemaphoreType.DMA((len(self.transfers),))]

    def _local(self, ins, outs, sems):
        me = _my_index()
        return [pltpu.make_async_copy(src(ins[a], me), dst(outs[b], me), sems[2].at[t])
                for t, (a, src, b, dst) in enumerate(self.transfers)]

    def _remote(self, ins, outs, sems, t, k, sending):
        a, src, b, dst = self.transfers[t]
        peer, peer_index = _peer(k)
        sem = t * (N_DEV - 1) + k - 1
        return pltpu.make_async_remote_copy(
            src_ref=src(ins[a], peer_index), dst_ref=dst(outs[b], _my_index() if sending else peer_index),
            send_sem=sems[0].at[sem], recv_sem=sems[1].at[sem], device_id=peer, device_id_type=MESH_ID)

    def start(self, ins, outs, sems):
        for cp in self._local(ins, outs, sems):
            cp.start()
        for k in range(1, N_DEV):
            for t in range(len(self.transfers)):
                self._remote(ins, outs, sems, t, k, True).start()

    def wait(self, ins, outs, sems):
        for k in range(1, N_DEV):
            for t in range(len(self.transfers)):
                self._remote(ins, outs, sems, t, k, False).wait_recv()
        for k in range(1, N_DEV):
            for t in range(len(self.transfers)):
                self._remote(ins, outs, sems, t, k, True).wait_send()
        for cp in self._local(ins, outs, sems):
            cp.wait()

    def run(self, name):
        n_in, n_out = len(self.inputs), len(self.out_shapes)

        def body(*refs):
            ins, outs, sems = refs[:n_in], refs[n_in:n_in + n_out], refs[n_in + n_out:]
            self.start(ins, outs, sems)
            self.wait(ins, outs, sems)

        return pl.pallas_call(
            body, name=name, in_specs=[ANY_SPEC] * n_in, out_specs=[ANY_SPEC] * n_out, out_shape=self.out_shapes,
            scratch_shapes=self.sem_shapes(), compiler_params=pltpu.CompilerParams(has_side_effects=True),
        )(*self.inputs)


def _hosting_call(exchange, body, *, name, grid, in_specs, out_specs, out_shape, scratch_shapes, compiler_params):
    if exchange is None or not exchange.transfers:
        return pl.pallas_call(body, name=name, grid=grid, in_specs=in_specs, out_specs=out_specs, out_shape=out_shape,
                              scratch_shapes=scratch_shapes, compiler_params=compiler_params)
    n_in, n_out, n_scr = len(in_specs), len(out_specs), len(scratch_shapes)
    x_in, x_out = len(exchange.inputs), len(exchange.out_shapes)

    def hosted(*refs):
        ins, x_ins = refs[:n_in], refs[n_in:n_in + x_in]
        o0 = n_in + x_in
        outs, x_outs = refs[o0:o0 + n_out], refs[o0 + n_out:o0 + n_out + x_out]
        s0 = o0 + n_out + x_out
        scratch, sems = refs[s0:s0 + n_scr], refs[s0 + n_scr:]
        ids = [pl.program_id(a) for a in range(len(grid))]
        first = functools.reduce(jnp.logical_and, [i == 0 for i in ids])
        last = functools.reduce(jnp.logical_and, [i == g - 1 for i, g in zip(ids, grid)])

        @pl.when(first)
        def _():
            exchange.start(x_ins, x_outs, sems)

        body(*ins, *outs, *scratch)

        @pl.when(last)
        def _():
            exchange.wait(x_ins, x_outs, sems)

    call = pl.pallas_call(
        hosted, name=name, grid=grid, in_specs=list(in_specs) + [ANY_SPEC] * x_in,
        out_specs=list(out_specs) + [ANY_SPEC] * x_out, out_shape=list(out_shape) + list(exchange.out_shapes),
        scratch_shapes=list(scratch_shapes) + exchange.sem_shapes(), compiler_params=compiler_params)
    return lambda *operands: call(*operands, *exchange.inputs)


def _whole(ref, j):
    return ref


def _slab(ref, j):
    return ref.at[j]


def _layer_of(l):
    return lambda ref, j: ref.at[l]


def _slab_layer(l):
    return lambda ref, j: ref.at[j, l]


def _adamw(g, w, m, v):
    nm = ADAM_B1 * m + (1.0 - ADAM_B1) * g
    nv = ADAM_B2 * v + (1.0 - ADAM_B2) * jnp.square(g)
    m_hat = nm / (1.0 - ADAM_B1 ** ADAM_STEP)
    v_hat = nv / (1.0 - ADAM_B2 ** ADAM_STEP)
    return -ADAM_LR * (m_hat / (jnp.sqrt(v_hat) + ADAM_EPS) + ADAM_WD * w), nm, nv


def _sum_parts(p_ref):
    g = p_ref[0].astype(F32)
    for j in range(1, N_DEV):
        g = g + p_ref[j].astype(F32)
    return g


def _adamw_sharded(parts, w, m, v, name):
    shape = w.shape
    lead, (rows, cols) = shape[:-2], shape[-2:]
    tr = min(rows, ROW_BLOCK)
    assert rows % tr == 0
    steps = rows // tr
    nl = len(lead)
    spec = pl.BlockSpec((None,) * nl + (tr, cols), lambda *idx: idx + (0,))
    per_layer = isinstance(parts, (list, tuple))
    if per_layer:
        def part_spec(l):
            return pl.BlockSpec((N_DEV, tr, cols), lambda li, r: (0, jnp.where(li == l, r, jnp.where(li < l, 0, steps - 1)), 0))
        part_specs = [part_spec(l) for l in range(len(parts))]
    else:
        parts = [parts]
        part_specs = [pl.BlockSpec((N_DEV,) + (None,) * nl + (tr, cols), lambda *idx: (0,) + idx + (0,))]
    count = len(parts)

    def body(*refs):
        p_refs = refs[:count]
        w_ref, m_ref, v_ref, g_ref, d_ref, nm_ref, nv_ref = refs[count:]

        def update(p_ref):
            g = _sum_parts(p_ref)
            g_ref[...] = g
            d_ref[...], nm_ref[...], nv_ref[...] = _adamw(g, w_ref[...], m_ref[...], v_ref[...])

        if per_layer:
            for l in range(count):
                pl.when(pl.program_id(0) == l)(functools.partial(update, p_refs[l]))
        else:
            update(p_refs[0])

    return pl.pallas_call(
        body, name=name, grid=lead + (steps,),
        in_specs=part_specs + [spec, spec, spec], out_specs=[spec] * 4, out_shape=[_sds(shape, F32)] * 4,
        compiler_params=_params(("arbitrary",) * (nl + 1)),
    )(*parts, w, m, v)


def _adamw_replicated(gathered, w, m, v, depth):
    names = NORM_NAMES + VEC512_NAMES + VEC256_NAMES + LRU_MAT_NAMES
    count = len(names)

    def body(*refs):
        norms, v512, v256, mats = refs[:4]
        w_refs, m_refs, v_refs = (refs[4 + t * count:4 + (t + 1) * count] for t in range(3))
        outs = refs[4 + 3 * count:4 + 7 * count]
        sum_norms, sum_512, sum_256, unfolded = refs[4 + 7 * count:]
        sum_norms[...] = _sum_parts(norms)
        sum_512[...] = _sum_parts(v512)
        sum_256[...] = _sum_parts(v256)
        for n_, nm in enumerate(names):
            if nm in NORM_NAMES:
                g = sum_norms[pl.ds(depth * NORM_NAMES.index(nm), depth), :]
            elif nm in VEC512_NAMES:
                g = sum_512[pl.ds(depth * VEC512_NAMES.index(nm), depth), :]
            elif nm in VEC256_NAMES:
                g = sum_256[pl.ds(depth * VEC256_NAMES.index(nm), depth), :]
            else:
                p_ = LRU_MAT_NAMES.index(nm)
                folded = mats[0, p_].astype(F32)
                for j in range(1, N_DEV):
                    folded = folded + mats[j, p_].astype(F32)
                for hd in range(LRU_HEADS):
                    src_rows, src_cols = _folded_block(hd)
                    unfolded[:, hd * LRU_HD:(hd + 1) * LRU_HD, :] = folded[:, src_rows, src_cols]
                g = unfolded[...]
            delta, nm_, nv_ = _adamw(g, w_refs[n_][...], m_refs[n_][...], v_refs[n_][...])
            outs[n_][...] = g
            outs[count + n_][...] = delta
            outs[2 * count + n_][...] = nm_
            outs[3 * count + n_][...] = nv_

    shapes = [_sds(w[nm].shape, F32) for nm in names]
    ins = list(gathered) + [t[nm] for t in (w, m, v) for nm in names]
    outs = pl.pallas_call(
        body, name="adamw_replicated", in_specs=[VMEM_SPEC] * len(ins), out_specs=[VMEM_SPEC] * (4 * count),
        out_shape=shapes * 4,
        scratch_shapes=[pltpu.VMEM(gathered[0].shape[1:], F32), pltpu.VMEM(gathered[1].shape[1:], F32),
                        pltpu.VMEM(gathered[2].shape[1:], F32), pltpu.VMEM((depth, LRU_W, LRU_HD), F32)],
        compiler_params=_params(None, 48),
    )(*ins)
    return [dict(zip(names, outs[t * count:(t + 1) * count])) for t in range(4)]


WEIGHT_NAMES = ("meta_tokens", "norm_mix_pre", "norm_mix_post", "norm_mlp_pre", "norm_mlp_post", "w_in", "conv_w", "conv_b",
                "lru_wa_f", "lru_ba_f", "lru_wx_f", "lru_bx_f", "lru_lambda_f", "lru_wa_b", "lru_ba_b", "lru_wx_b",
                "lru_bx_b", "lru_lambda_b", "gla_wg_f", "gla_bg_f", "gla_wg_b", "gla_bg_b", "gla_head_norm", "w_out",
                "w_mlp_up", "w_mlp_down")
MATMUL_WEIGHTS = ("w_in", "w_out", "w_mlp_up", "w_mlp_down")
SMALL_SHARDED = ("conv_w", "gla_wg_f", "gla_wg_b", "meta_tokens")


def kernel(x, meta_tokens, norm_mix_pre, norm_mix_post, norm_mlp_pre, norm_mlp_post, w_in, conv_w, conv_b, lru_wa_f, lru_ba_f, lru_wx_f, lru_bx_f, lru_lambda_f, lru_wa_b, lru_ba_b, lru_wx_b, lru_bx_b, lru_lambda_b, gla_wg_f, gla_bg_f, gla_wg_b, gla_bg_b, gla_head_norm, w_out, w_mlp_up, w_mlp_down, loss_target, m_meta_tokens, m_norm_mix_pre, m_norm_mix_post, m_norm_mlp_pre, m_norm_mlp_post, m_w_in, m_conv_w, m_conv_b, m_lru_wa_f, m_lru_ba_f, m_lru_wx_f, m_lru_bx_f, m_lru_lambda_f, m_lru_wa_b, m_lru_ba_b, m_lru_wx_b, m_lru_bx_b, m_lru_lambda_b, m_gla_wg_f, m_gla_bg_f, m_gla_wg_b, m_gla_bg_b, m_gla_head_norm, m_w_out, m_w_mlp_up, m_w_mlp_down, v_meta_tokens, v_norm_mix_pre, v_norm_mix_post, v_norm_mlp_pre, v_norm_mlp_post, v_w_in, v_conv_w, v_conv_b, v_lru_wa_f, v_lru_ba_f, v_lru_wx_f, v_lru_bx_f, v_lru_lambda_f, v_lru_wa_b, v_lru_ba_b, v_lru_wx_b, v_lru_bx_b, v_lru_lambda_b, v_gla_wg_f, v_gla_bg_f, v_gla_wg_b, v_gla_bg_b, v_gla_head_norm, v_w_out, v_w_mlp_up, v_w_mlp_down):
    args = locals()
    w = {nm: args[nm] for nm in WEIGHT_NAMES}
    m = {nm: args["m_" + nm] for nm in WEIGHT_NAMES}
    v = {nm: args["v_" + nm] for nm in WEIGHT_NAMES}
    depth = w_in.shape[0]

    shards = {nm: w[nm].astype(BF16) for nm in MATMUL_WEIGHTS}
    first_items = [("w_in", 0), ("w_out", 0)]
    p = {}
    gather = _WeightGather(shards, p)
    ex = gather.exchange(first_items)
    for nm in SMALL_SHARDED:
        ex.add(w[nm], _whole, _sds((N_DEV,) + w[nm].shape, F32), _slab)
    landed = ex.run("all_gather")
    gather.install(first_items, landed[:len(first_items)])
    p.update(_prepare_params(w, dict(zip(SMALL_SHARDED, landed[len(first_items):])), depth))

    h = jnp.concatenate([jnp.zeros((PAD_ROWS, D_MODEL), F32), p["meta_tokens"], x[0]], axis=0)
    saved = []
    for l in range(depth):
        h, s = _layer_fwd(h, l, p, gather, depth)
        saved.append(s)
    dh, loss_part = _loss_and_grad(h, loss_target[0])
    loss = lax.psum(loss_part[0, 0], ("x", "y", "c"))

    outbox = _GradOutbox()
    grads = [None] * depth
    for l in reversed(range(depth)):
        dh, grads[l], dz = _layer_bwd(dh, l, p, saved[l], outbox)
        if l > 0:
            outbox.put("w_in", l, _w_in_slabs(_matmul_tn(saved[l]["hn"], dz, "grad_w_in")))
    grad_x = dh[PAD_ROWS + N_META:][None]

    small = _pack_small_grads(grads, dh, depth)
    rep_bufs, small_slabs = small[:4], small[4:]
    ex, keys = outbox.exchange()
    for g in small_slabs:
        ex.add(g, _slab, _sds(g.shape, F32), _slab)
    for g in rep_bufs:
        ex.add(g, _whole, _sds((N_DEV,) + g.shape, g.dtype), _slab)
    grad_w_in, *landed = _matmul_tn(saved[0]["hn"], dz, "grad_w_in", exchange=ex)
    outbox.store(keys, landed[:len(keys)])
    small_received = landed[len(keys):len(keys) + len(small_slabs)]
    rep_received = landed[len(keys) + len(small_slabs):]
    outbox.put("w_in", 0, _w_in_slabs(grad_w_in))
    ex, keys = outbox.exchange()
    outbox.store(keys, ex.run("exchange_grads"))

    results = [{}, {}, {}, {}]
    for nm in MATMUL_WEIGHTS:
        parts = [outbox.received[(nm, l)] for l in range(depth)]
        for t, out in enumerate(_adamw_sharded(parts, w[nm], m[nm], v[nm], "adamw_" + nm)):
            results[t][nm] = out
    for nm, parts in zip(SMALL_SHARDED, small_received):
        for t, out in enumerate(_adamw_sharded(parts, w[nm], m[nm], v[nm], "adamw_" + nm)):
            results[t][nm] = out

    def kernel_side(tree):
        return {nm: tree[nm].reshape(depth, LRU_W, LRU_HD) if nm in LRU_MAT_NAMES else tree[nm]
                for nm in NORM_NAMES + VEC512_NAMES + VEC256_NAMES + LRU_MAT_NAMES}

    for t, tree in enumerate(_adamw_replicated(rep_received, kernel_side(w), kernel_side(m), kernel_side(v), depth)):
        for nm, out in tree.items():
            results[t][nm] = out.reshape(w[nm].shape)
    return (loss, grad_x, *[results[t][nm] for t in range(4) for nm in WEIGHT_NAMES])
```

```python
import functools

import jax
import jax.numpy as jnp
from jax import lax
from jax.experimental import pallas as pl
from jax.experimental.pallas import tpu as pltpu

F32 = jnp.float32
BF16 = jnp.bfloat16

N_DEV = 8
D_MODEL = 1024
N_META = 16
ROW_BLOCK = 256
PAD_ROWS = ROW_BLOCK - N_META
CHUNK = 128
LRU_W = 512
LRU_HEADS = 8
LRU_HD = 64
LRU_C = 8.0
GLA_HEADS = 4
GLA_DK = 64
GLA_DV = 128
GLA_QK = GLA_HEADS * GLA_DK
GLA_W = GLA_HEADS * GLA_DV
GLA_RANK = 16
GATE_NORM = 16.0
D_FF = 4096
D_IN = 2592
Z_W = 2688
ZG_COL_BLOCK = 2560 // 128
EPS = 1e-6
LANES = 128

ADAM_LR = 0.001
ADAM_B1 = 0.9
ADAM_B2 = 0.999
ADAM_EPS = 1e-08
ADAM_WD = 0.01
ADAM_STEP = 10
ADAM_ROWS = 512

VMEM_SPEC = pl.BlockSpec(memory_space=pltpu.VMEM)
ANY_SPEC = pl.BlockSpec(memory_space=pl.ANY)
MESH_ID = pl.DeviceIdType.MESH


def _sds(shape, dtype):
    return jax.ShapeDtypeStruct(shape, dtype)


def _params(sem=None, vmem_mb=None):
    kw = {}
    if sem is not None:
        kw["dimension_semantics"] = sem
    if vmem_mb is not None:
        kw["vmem_limit_bytes"] = vmem_mb * 2 ** 20
    return pltpu.CompilerParams(**kw)


def _row_tile(n, cap=768):
    for t in (768, 512, 384, 256):
        if t <= cap and n % t == 0:
            return t
    raise ValueError(n)


def _col_tile(k):
    for t in (1024, 896, 768, 640, 512, 384, 256, 128):
        if k % t == 0:
            return t
    raise ValueError(k)


def _sigmoid(x):
    return 0.5 * jnp.tanh(0.5 * x) + 0.5


def _gelu_and_grad(x):
    c = 0.7978845608028654
    inner = c * (x + 0.044715 * x * x * x)
    t = jnp.tanh(inner)
    gelu = 0.5 * x * (1.0 + t)
    dgelu = 0.5 * (1.0 + t) + 0.5 * x * (1.0 - t * t) * c * (1.0 + 3.0 * 0.044715 * x * x)
    return gelu, dgelu


def _one_minus_square(a, log_a):
    return jnp.tanh(-log_a) * (1.0 + a * a)


def _rms_fwd(x, g):
    rs = lax.rsqrt(jnp.mean(x * x, axis=-1, keepdims=True) + EPS)
    return x * rs * g


def _rms_bwd(x, g, dy):
    rs = lax.rsqrt(jnp.mean(x * x, axis=-1, keepdims=True) + EPS)
    xh = x * rs
    dyg = dy * g
    dx = rs * (dyg - xh * jnp.mean(dyg * xh, axis=-1, keepdims=True))
    return dx, jnp.sum(dy * xh, axis=0, keepdims=True)


def _dot(a, b):
    return jnp.dot(a.astype(BF16), b.astype(BF16), preferred_element_type=F32)


def _dot_nt(a, b):
    return lax.dot_general(a.astype(BF16), b.astype(BF16), (((1,), (1,)), ((), ())), preferred_element_type=F32)


def _dot_tn(a, b):
    return lax.dot_general(a.astype(BF16), b.astype(BF16), (((0,), (0,)), ((), ())), preferred_element_type=F32)


class _LayerParam:
    def __init__(self, array, *index):
        self.array = array
        self.index = index

    @property
    def spec(self):
        lead = len(self.index)
        tail = self.array.shape[lead:]
        index = self.index
        return pl.BlockSpec((None,) * lead + tail, lambda *_: index + (0,) * len(tail))


def _row_ids(rows, block_index):
    return block_index * rows + lax.broadcasted_iota(jnp.int32, (rows, 1), 0)


def _accumulate(ref, value, first):
    @pl.when(first)
    def _():
        ref[...] = value

    @pl.when(jnp.logical_not(first))
    def _():
        ref[...] += value


def _norm_in_proj(h, g, w):
    n, d = h.shape
    zw = w.shape[1]
    tr = _row_tile(n)

    def body(h_ref, g_ref, w_ref, hn_ref, z_ref):
        hn = _rms_fwd(h_ref[...], g_ref[...]).astype(BF16)
        hn_ref[...] = hn
        z_ref[...] = jnp.dot(hn, w_ref[...], preferred_element_type=F32)

    return pl.pallas_call(
        body, name="norm_in_proj", grid=(n // tr,),
        in_specs=[pl.BlockSpec((tr, d), lambda i: (i, 0)), g.spec, VMEM_SPEC],
        out_specs=[pl.BlockSpec((tr, d), lambda i: (i, 0)), pl.BlockSpec((tr, zw), lambda i: (i, 0))],
        out_shape=[_sds((n, d), BF16), _sds((n, zw), F32)],
        compiler_params=_params(("parallel",), 48),
    )(h, g.array, w)


def _halo_specs(width, nb, col=0):
    per = ROW_BLOCK // 8
    prev = pl.BlockSpec((8, width), lambda i: (jnp.maximum(i * per - 1, 0), col))
    nxt = pl.BlockSpec((8, width), lambda i: (jnp.minimum((i + 1) * per, nb * per - 1), col))
    return prev, nxt


def _shift_down(x, prev8, d):
    n = x.shape[0]
    r = pltpu.roll(x, d, 0)
    p = pltpu.roll(prev8, d, 0)
    row8 = lax.broadcasted_iota(jnp.int32, (8, 1), 0)
    head = jnp.where(row8 < d, p, r[0:8])
    return jnp.concatenate([head, r[8:]], axis=0)


def _shift_up(x, next8, d):
    n = x.shape[0]
    r = pltpu.roll(x, n - d, 0)
    q = pltpu.roll(next8, 8 - d, 0)
    row8 = lax.broadcasted_iota(jnp.int32, (8, 1), 0)
    tail = jnp.where(row8 >= 8 - d, q, r[n - 8:])
    return jnp.concatenate([r[:n - 8], tail], axis=0)


def _conv_fwd(z, conv_w, conv_b):
    n = z.shape[0]
    nb = n // ROW_BLOCK
    prev_spec, next_spec = _halo_specs(LRU_W, nb)

    def body(cur_ref, prev_ref, next_ref, w_ref, b_ref, xc_ref):
        i = pl.program_id(0)
        cur = cur_ref[...]
        prev8 = prev_ref[...] * jnp.where(i > 0, 1.0, 0.0)
        next8 = next_ref[...] * jnp.where(i < nb - 1, 1.0, 0.0)
        w = [w_ref[pl.ds(k, 1), :] for k in range(4)]
        xc = (w[0] * _shift_down(cur, prev8, 2) + w[1] * _shift_down(cur, prev8, 1)
              + w[2] * cur + w[3] * _shift_up(cur, next8, 1) + b_ref[...])
        xc_ref[...] = xc

    return pl.pallas_call(
        body, name="conv_fwd", grid=(nb,),
        in_specs=[pl.BlockSpec((ROW_BLOCK, LRU_W), lambda i: (i, 0)), prev_spec, next_spec, conv_w.spec, conv_b.spec],
        out_specs=pl.BlockSpec((ROW_BLOCK, LRU_W), lambda i: (i, 0)),
        out_shape=_sds((n, LRU_W), F32),
        compiler_params=_params(("parallel",)),
    )(z, z, z, conv_w.array, conv_b.array)


def _conv_bwd(dxc_f, dxc_b, z, conv_w):
    n = z.shape[0]
    nb = n // ROW_BLOCK
    prev_spec, next_spec = _halo_specs(LRU_W, nb)
    row_spec = pl.BlockSpec((ROW_BLOCK, LRU_W), lambda i: (i, 0))

    def body(df_ref, dfp_ref, dfn_ref, db_ref, dbp_ref, dbn_ref, x_ref, xp_ref, xn_ref, w_ref,
             dx_ref, dw_ref, dbias_ref):
        i = pl.program_id(0)
        has_prev = jnp.where(i > 0, 1.0, 0.0)
        has_next = jnp.where(i < nb - 1, 1.0, 0.0)
        dxc = df_ref[...] + db_ref[...]
        dprev = (dfp_ref[...] + dbp_ref[...]) * has_prev
        dnext = (dfn_ref[...] + dbn_ref[...]) * has_next
        x = x_ref[...]
        xprev = xp_ref[...] * has_prev
        xnext = xn_ref[...] * has_next
        w = [w_ref[pl.ds(k, 1), :] for k in range(4)]
        dx_ref[...] = (w[0] * _shift_up(dxc, dnext, 2) + w[1] * _shift_up(dxc, dnext, 1)
                       + w[2] * dxc + w[3] * _shift_down(dxc, dprev, 1)).astype(BF16)
        dw = jnp.concatenate([
            jnp.sum(dxc * _shift_down(x, xprev, 2), axis=0, keepdims=True),
            jnp.sum(dxc * _shift_down(x, xprev, 1), axis=0, keepdims=True),
            jnp.sum(dxc * x, axis=0, keepdims=True),
            jnp.sum(dxc * _shift_up(x, xnext, 1), axis=0, keepdims=True),
            jnp.zeros((4, LRU_W), F32)], axis=0)
        _accumulate(dw_ref, dw, i == 0)
        _accumulate(dbias_ref, jnp.sum(dxc, axis=0, keepdims=True), i == 0)

    dx, dw, dbias = pl.pallas_call(
        body, name="conv_bwd", grid=(nb,),
        in_specs=[row_spec, prev_spec, next_spec, row_spec, prev_spec, next_spec, row_spec, prev_spec, next_spec,
                  conv_w.spec],
        out_specs=[row_spec, pl.BlockSpec((8, LRU_W), lambda i: (0, 0)), pl.BlockSpec((1, LRU_W), lambda i: (0, 0))],
        out_shape=[_sds((n, LRU_W), BF16), _sds((8, LRU_W), F32), _sds((1, LRU_W), F32)],
        compiler_params=_params(("arbitrary",)),
    )(dxc_f, dxc_f, dxc_f, dxc_b, dxc_b, dxc_b, z, z, z, conv_w.array)
    return dx, dw, dbias


def _mix_epilogue(h_f, h_b, o_f, o_b, z, head_norm):
    n = z.shape[0]
    tr = ROW_BLOCK
    spec = pl.BlockSpec((tr, 512), lambda i: (i, 0))

    def body(hf_ref, hb_ref, of_ref, ob_ref, gate_ref, gout_ref, w_ref, y_ref):
        gelu, _ = _gelu_and_grad(gate_ref[...])
        y_ref[:, 0:LRU_W] = ((hf_ref[...] + hb_ref[...]) * gelu).astype(BF16)
        o = of_ref[...] + ob_ref[...]
        gout = gout_ref[...]
        silu = gout * _sigmoid(gout)
        w = w_ref[...]
        for hd in range(GLA_HEADS):
            cs = slice(hd * GLA_DV, (hd + 1) * GLA_DV)
            oh = o[:, cs]
            on = oh * lax.rsqrt(jnp.mean(oh * oh, axis=-1, keepdims=True) + EPS)
            y_ref[:, LRU_W + hd * GLA_DV:LRU_W + (hd + 1) * GLA_DV] = (on * w[:, cs] * silu[:, cs]).astype(BF16)

    return pl.pallas_call(
        body, name="mix_epilogue", grid=(n // tr,),
        in_specs=[spec, spec, spec, spec, pl.BlockSpec((tr, 512), lambda i: (i, 1)),
                  pl.BlockSpec((tr, 512), lambda i: (i, 4)), head_norm.spec],
        out_specs=pl.BlockSpec((tr, D_MODEL), lambda i: (i, 0)),
        out_shape=_sds((n, D_MODEL), BF16),
        compiler_params=_params(("parallel",)),
    )(h_f, h_b, o_f, o_b, z, z, head_norm.array)


def _mix_epilogue_bwd(dymix, h_f, h_b, o_f, o_b, z, head_norm):
    n = z.shape[0]
    tr = ROW_BLOCK
    spec = pl.BlockSpec((tr, 512), lambda i: (i, 0))

    def body(dyl_ref, dyg_ref, hf_ref, hb_ref, of_ref, ob_ref, gate_ref, gout_ref, w_ref,
             dhs_ref, dgate_ref, do_ref, dgout_ref, dw_ref):
        i = pl.program_id(0)
        dyl = dyl_ref[...]
        gelu, dgelu = _gelu_and_grad(gate_ref[...])
        dhs_ref[...] = dyl * gelu
        dgate_ref[...] = (dyl * (hf_ref[...] + hb_ref[...]) * dgelu).astype(BF16)
        dyg = dyg_ref[...]
        o = of_ref[...] + ob_ref[...]
        gout = gout_ref[...]
        sg = _sigmoid(gout)
        silu = gout * sg
        dsilu = sg * (1.0 + gout * (1.0 - sg))
        w = w_ref[...]
        dws = []
        for hd in range(GLA_HEADS):
            cs = slice(hd * GLA_DV, (hd + 1) * GLA_DV)
            oh = o[:, cs]
            rs = lax.rsqrt(jnp.mean(oh * oh, axis=-1, keepdims=True) + EPS)
            on = oh * rs
            dy = dyg[:, cs]
            dgout_ref[:, cs] = (dy * on * w[:, cs] * dsilu[:, cs]).astype(BF16)
            dys = dy * silu[:, cs]
            dws.append(jnp.sum(dys * on, axis=0, keepdims=True))
            don = dys * w[:, cs]
            do_ref[:, cs] = (rs * (don - on * jnp.mean(don * on, axis=-1, keepdims=True))).astype(BF16)
        _accumulate(dw_ref, jnp.concatenate(dws, axis=1), i == 0)

    return pl.pallas_call(
        body, name="mix_epilogue_bwd", grid=(n // tr,),
        in_specs=[pl.BlockSpec((tr, 512), lambda i: (i, 0)), pl.BlockSpec((tr, 512), lambda i: (i, 1)),
                  spec, spec, spec, spec, pl.BlockSpec((tr, 512), lambda i: (i, 1)),
                  pl.BlockSpec((tr, 512), lambda i: (i, 4)), head_norm.spec],
        out_specs=[spec, spec, spec, spec, pl.BlockSpec((1, GLA_W), lambda i: (0, 0))],
        out_shape=[_sds((n, 512), F32)] + [_sds((n, 512), BF16)] * 3 + [_sds((1, GLA_W), F32)],
        compiler_params=_params(("arbitrary",)),
    )(dymix, dymix, h_f, h_b, o_f, o_b, z, z, head_norm.array)


def _out_proj(ymix, w_out, h, g):
    n, d = h.shape
    tr = _row_tile(n)
    spec = pl.BlockSpec((tr, d), lambda i: (i, 0))

    def body(y_ref, w_ref, h_ref, g_ref, mix_ref, hmid_ref):
        mix = jnp.dot(y_ref[...], w_ref[...], preferred_element_type=F32)
        mix_ref[...] = mix
        hmid_ref[...] = h_ref[...] + _rms_fwd(mix, g_ref[...])

    return pl.pallas_call(
        body, name="out_proj", grid=(n // tr,),
        in_specs=[spec, VMEM_SPEC, spec, g.spec],
        out_specs=[spec, spec],
        out_shape=[_sds((n, d), F32), _sds((n, d), F32)],
        compiler_params=_params(("parallel",), 44),
    )(ymix, w_out, h, g.array)


def _out_proj_bwd(dh_mid, mix, g, w_out):
    n, d = mix.shape
    tr = _row_tile(n)
    spec = pl.BlockSpec((tr, d), lambda i: (i, 0))

    def body(dh_ref, mix_ref, g_ref, w_ref, dmix_ref, dy_ref, dg_ref):
        i = pl.program_id(0)
        dmix, dg = _rms_bwd(mix_ref[...], g_ref[...], dh_ref[...])
        dmix = dmix.astype(BF16)
        dmix_ref[...] = dmix
        dy_ref[...] = _dot_nt(dmix, w_ref[...])
        _accumulate(dg_ref, dg, i == 0)

    return pl.pallas_call(
        body, name="out_proj_bwd", grid=(n // tr,),
        in_specs=[spec, spec, g.spec, VMEM_SPEC],
        out_specs=[spec, spec, pl.BlockSpec((1, d), lambda i: (0, 0))],
        out_shape=[_sds((n, d), BF16), _sds((n, d), F32), _sds((1, d), F32)],
        compiler_params=_params(("arbitrary",), 44),
    )(dh_mid, mix, g.array, w_out)


FF_SLAB = D_FF // N_DEV


def _relu_squared(up):
    return jnp.square(jnp.maximum(up.astype(F32), 0.0)).astype(BF16)


def _mlp_fwd(h_mid, g_pre, w_up, w_down, g_post, exchange=None):
    n, d = h_mid.shape
    tr = _row_tile(n, 384)
    spec = pl.BlockSpec((tr, d), lambda i: (i, 0))

    def body(h_ref, gpre_ref, wup_ref, wdn_ref, gpost_ref, hn_ref, up_ref, ff_ref, hout_ref):
        h = h_ref[...]
        hn = _rms_fwd(h, gpre_ref[...]).astype(BF16)
        hn_ref[...] = hn
        ff = jnp.zeros((tr, d), F32)
        for j in range(N_DEV):
            cs = slice(j * FF_SLAB, (j + 1) * FF_SLAB)
            up = jnp.dot(hn, wup_ref[j], preferred_element_type=F32).astype(BF16)
            up_ref[:, cs] = up
            ff = ff + jnp.dot(_relu_squared(up), wdn_ref[cs, :], preferred_element_type=F32)
        ff_ref[...] = ff
        hout_ref[...] = h + _rms_fwd(ff, gpost_ref[...])

    return _hosting_call(
        exchange, body, name="mlp_fwd", grid=(n // tr,),
        in_specs=[spec, g_pre.spec, VMEM_SPEC, VMEM_SPEC, g_post.spec],
        out_specs=[spec, pl.BlockSpec((tr, D_FF), lambda i: (i, 0)), spec, spec],
        out_shape=[_sds((n, d), BF16), _sds((n, D_FF), BF16), _sds((n, d), F32), _sds((n, d), F32)],
        scratch_shapes=[], compiler_params=_params(("arbitrary",), 52),
    )(h_mid, g_pre.array, w_up, w_down, g_post.array)


def _mlp_bwd(dh, ff, up, h_mid, g_pre, w_up, w_down, g_post, exchange=None):
    n, d = h_mid.shape
    tr = _row_tile(n, 384)
    spec = pl.BlockSpec((tr, d), lambda i: (i, 0))
    wide = pl.BlockSpec((tr, D_FF), lambda i: (i, 0))
    gspec = pl.BlockSpec((1, d), lambda i: (0, 0))

    def body(dh_ref, ff_ref, up_ref, h_ref, gpre_ref, wup_ref, wdn_ref, gpost_ref,
             dff_ref, dup_ref, dhmid_ref, dgpost_ref, dgpre_ref):
        i = pl.program_id(0)
        dh = dh_ref[...]
        dff, dgpost = _rms_bwd(ff_ref[...], gpost_ref[...], dh)
        dff = dff.astype(BF16)
        dff_ref[...] = dff
        dhn = jnp.zeros((tr, d), F32)
        for j in range(N_DEV):
            cs = slice(j * FF_SLAB, (j + 1) * FF_SLAB)
            relu = jnp.maximum(up_ref[:, cs].astype(F32), 0.0)
            dact = _dot_nt(dff, wdn_ref[cs, :])
            dup = (dact * 2.0 * relu).astype(BF16)
            dup_ref[:, cs] = dup
            dhn = dhn + _dot_nt(dup, wup_ref[j])
        dx, dgpre = _rms_bwd(h_ref[...], gpre_ref[...], dhn)
        dhmid_ref[...] = dh + dx
        _accumulate(dgpost_ref, dgpost, i == 0)
        _accumulate(dgpre_ref, dgpre, i == 0)

    return _hosting_call(
        exchange, body, name="mlp_bwd", grid=(n // tr,),
        in_specs=[spec, spec, wide, spec, g_pre.spec, VMEM_SPEC, VMEM_SPEC, g_post.spec],
        out_specs=[spec, wide, spec, gspec, gspec],
        out_shape=[_sds((n, d), BF16), _sds((n, D_FF), BF16), _sds((n, d), F32), _sds((1, d), F32), _sds((1, d), F32)],
        scratch_shapes=[], compiler_params=_params(("arbitrary",), 56),
    )(dh, ff, up, h_mid, g_pre.array, w_up, w_down, g_post.array)


def _in_proj_bwd(pieces, w_in, h, g, dh_mid):
    dxbr, dgate, dqk_f, dqk_b, dv_f, dv_b, dgout, dzg_f, dzg_b = pieces
    n, d = h.shape
    tr = _row_tile(n, 384)
    spec = pl.BlockSpec((tr, d), lambda i: (i, 0))
    s512 = pl.BlockSpec((tr, 512), lambda i: (i, 0))
    s128 = pl.BlockSpec((tr, LANES), lambda i: (i, 0))

    def body(a_ref, b_ref, cf_ref, cb_ref, df_ref, db_ref, e_ref, ff_ref, fb_ref, w_ref, h_ref, g_ref, dhm_ref,
             dz_ref, dh_ref, dg_ref):
        i = pl.program_id(0)
        real = (_row_ids(tr, i) >= PAD_ROWS).astype(F32)
        f32 = lambda ref: ref[...].astype(F32)
        dz = jnp.concatenate([f32(a_ref), f32(b_ref), f32(cf_ref) + f32(cb_ref), f32(df_ref) + f32(db_ref),
                              f32(e_ref), f32(ff_ref) + f32(fb_ref)], axis=1) * real
        dz = dz.astype(BF16)
        dz_ref[...] = dz
        dhn = _dot_nt(dz, w_ref[...])
        dx, dg = _rms_bwd(h_ref[...], g_ref[...], dhn)
        dh_ref[...] = (dhm_ref[...] + dx) * real
        _accumulate(dg_ref, dg, i == 0)

    return pl.pallas_call(
        body, name="in_proj_bwd", grid=(n // tr,),
        in_specs=[s512, s512, s512, s512, s512, s512, s512, s128, s128, VMEM_SPEC, spec, g.spec, spec],
        out_specs=[pl.BlockSpec((tr, Z_W), lambda i: (i, 0)), spec, pl.BlockSpec((1, d), lambda i: (0, 0))],
        out_shape=[_sds((n, Z_W), BF16), _sds((n, d), F32), _sds((1, d), F32)],
        compiler_params=_params(("arbitrary",), 48),
    )(dxbr, dgate, dqk_f, dqk_b, dv_f, dv_b, dgout, dzg_f, dzg_b, w_in, h, g.array, dh_mid)


def _matmul_tn(a, b, name, column_slabs=False, exchange=None, a_map=None):
    n, m = a.shape
    k = b.shape[1]
    tr, tm, tk = _row_tile(n), _col_tile(m), _col_tile(k)
    steps = n // tr
    slab = k // N_DEV
    per_step = tk // slab if column_slabs else 1

    def body(a_ref, b_ref, o_ref, acc_ref):
        r = pl.program_id(2)
        a_blk = a_ref[...] if a_map is None else a_map(a_ref[...])
        _accumulate(acc_ref, _dot_tn(a_blk, b_ref[...]), r == 0)

        @pl.when(r == steps - 1)
        def _():
            if column_slabs:
                for j in range(per_step):
                    o_ref[j] = acc_ref[:, j * slab:(j + 1) * slab].astype(BF16)
            else:
                o_ref[...] = acc_ref[...].astype(BF16)

    if column_slabs:
        out_spec = pl.BlockSpec((per_step, tm, slab), lambda mi, ki, r: (ki, mi, 0))
        out_shape = _sds((N_DEV, m, slab), BF16)
    else:
        out_spec = pl.BlockSpec((tm, tk), lambda mi, ki, r: (mi, ki))
        out_shape = _sds((m, k), BF16)
    outs = _hosting_call(
        exchange, body, name=name, grid=(m // tm, k // tk, steps),
        in_specs=[pl.BlockSpec((tr, tm), lambda mi, ki, r: (r, mi)), pl.BlockSpec((tr, tk), lambda mi, ki, r: (r, ki))],
        out_specs=[out_spec], out_shape=[out_shape], scratch_shapes=[pltpu.VMEM((tm, tk), F32)],
        compiler_params=_params(("arbitrary", "arbitrary", "arbitrary"), 40),
    )(a, b)
    return outs[0] if exchange is None else outs


def _loss_and_grad(h_out, target):
    n, d = h_out.shape
    tr = ROW_BLOCK
    first = (PAD_ROWS + N_META) // tr

    def body(h_ref, t_ref, dh_ref, loss_ref):
        i = pl.program_id(0)
        real = jnp.where(i >= first, 1.0, 0.0)
        diff = (h_ref[...] - t_ref[...]) * real
        dh_ref[...] = diff * (1.0 / d)
        part = 0.5 * jnp.sum(jnp.mean(diff * diff, axis=-1, keepdims=True), axis=0, keepdims=True)
        _accumulate(loss_ref, jnp.broadcast_to(part, (1, LANES)), i == 0)

    return pl.pallas_call(
        body, name="loss_and_grad", grid=(n // tr,),
        in_specs=[pl.BlockSpec((tr, d), lambda i: (i, 0)), pl.BlockSpec((tr, d), lambda i: (jnp.maximum(i - first, 0), 0))],
        out_specs=[pl.BlockSpec((tr, d), lambda i: (i, 0)), pl.BlockSpec((1, LANES), lambda i: (0, 0))],
        out_shape=[_sds((n, d), F32), _sds((1, LANES), F32)],
        compiler_params=_params(("arbitrary",)),
    )(h_out, target)


SUBLANES = 8


def _scan_rows(a, u, reverse, window=None):
    n = a.shape[0]
    window = window or n
    pos = lax.broadcasted_iota(jnp.int32, (n, 1), 0) & (window - 1) if window < n else lax.broadcasted_iota(jnp.int32, (n, 1), 0)
    d = 1
    while d < window:
        shift = n - d if reverse else d
        keep = (pos < window - d) if reverse else (pos >= d)
        a_s = pltpu.roll(a, shift, 0)
        u_s = pltpu.roll(u, shift, 0)
        u = jnp.where(keep, a * u_s + u, u)
        a = jnp.where(keep, a * a_s, a)
        d *= 2
    return a, u


def _scan_block(a, u, h_in, reverse, stage_ref):
    n, width = a.shape
    groups = n // SUBLANES
    lanes = [slice(cb * LANES, (cb + 1) * LANES) for cb in range(width // LANES)]
    a1, u1 = _scan_rows(a, u, reverse, window=SUBLANES)
    for cb, cs in enumerate(lanes):
        stage_ref[0, cb] = a1[:, cs]
        stage_ref[1, cb] = u1[:, cs]
    edge = 0 if reverse else SUBLANES - 1
    group_rows = pl.ds(edge, groups, stride=SUBLANES)
    a2, u2 = _scan_rows(jnp.concatenate([stage_ref[0, cb, group_rows, :] for cb in range(len(lanes))], axis=1),
                        jnp.concatenate([stage_ref[1, cb, group_rows, :] for cb in range(len(lanes))], axis=1), reverse)
    leaving = a2 * h_in + u2
    grow = lax.broadcasted_iota(jnp.int32, (groups, 1), 0)
    if reverse:
        entering = jnp.where(grow == groups - 1, h_in, pltpu.roll(leaving, groups - 1, 0))
    else:
        entering = jnp.where(grow == 0, h_in, pltpu.roll(leaving, 1, 0))
    for cb, cs in enumerate(lanes):
        for k in range(SUBLANES):
            stage_ref[0, cb, pl.ds(k, groups, stride=SUBLANES), :] = entering[:, cs]
    entering_rows = jnp.concatenate([stage_ref[0, cb] for cb in range(len(lanes))], axis=1)
    return a1 * entering_rows + u1


def _lru_gates(xc, wcat_ref, bias_ref, lam_ref):
    nl = -lam_ref[...]
    nsp = -LRU_C * (jnp.maximum(nl, 0.0) + jnp.log(1.0 + jnp.exp(-jnp.abs(nl))))
    pre = _dot(xc, wcat_ref[...]) + bias_ref[...]
    r = _sigmoid(pre[:, :LRU_W])
    ig = _sigmoid(pre[:, LRU_W:])
    log_a = r * nsp
    a = jnp.exp(log_a)
    m2 = _one_minus_square(a, log_a)
    inv_m = lax.rsqrt(jnp.maximum(m2, 1e-30))
    return r, ig, a, m2 * inv_m, inv_m, nsp


def _lru_scan(xc, wcat, bias, lam, reverse):
    n = xc.shape[0]
    nb = n // ROW_BLOCK
    order = (lambda i: nb - 1 - i) if reverse else (lambda i: i)
    spec = pl.BlockSpec((ROW_BLOCK, LRU_W), lambda i: (order(i), 0))
    edge = 0 if reverse else ROW_BLOCK - 1

    def body(xc_ref, wcat_ref, bias_ref, lam_ref, h_ref, carry_ref, stage_ref):
        i = pl.program_id(0)

        @pl.when(i == 0)
        def _():
            carry_ref[...] = jnp.zeros_like(carry_ref)

        xc = xc_ref[...]
        r, ig, a, m, _, _ = _lru_gates(xc, wcat_ref, bias_ref, lam_ref)
        u = jnp.where(_row_ids(ROW_BLOCK, order(i)) >= PAD_ROWS, m * (ig * xc), 0.0)
        h_ref[...] = _scan_block(a, u, carry_ref[0:1, :], reverse, stage_ref)
        carry_ref[0:1, :] = h_ref[pl.ds(edge, 1), :]

    return pl.pallas_call(
        body, name="lru_scan_b" if reverse else "lru_scan_f", grid=(nb,),
        in_specs=[spec, wcat.spec, bias.spec, lam.spec],
        out_specs=spec,
        out_shape=_sds((n, LRU_W), F32),
        scratch_shapes=[pltpu.VMEM((8, LRU_W), F32), pltpu.VMEM((2, LRU_W // LANES, ROW_BLOCK, LANES), F32)],
        compiler_params=_params(("arbitrary",)),
    )(xc, wcat.array, bias.array, lam.array)


def _lru_scan_bwd(dhs, xc, h, wcat, bias, lam, reverse):
    n = xc.shape[0]
    nb = n // ROW_BLOCK
    per = ROW_BLOCK // 8
    order = (lambda i: i) if reverse else (lambda i: nb - 1 - i)
    spec = pl.BlockSpec((ROW_BLOCK, LRU_W), lambda i: (order(i), 0))
    if reverse:
        halo = pl.BlockSpec((8, LRU_W), lambda i: (jnp.minimum((order(i) + 1) * per, nb * per - 1), 0))
    else:
        halo = pl.BlockSpec((8, LRU_W), lambda i: (jnp.maximum(order(i) * per - 1, 0), 0))
    edge = ROW_BLOCK - 1 if reverse else 0

    def body(dhs_ref, xc_ref, h_ref, halo_ref, wcat_ref, bias_ref, lam_ref,
             dxc_ref, dw_ref, db_ref, dlam_ref, cdh_ref, ca_ref, tmp_ref, stage_ref):
        i = pl.program_id(0)
        ib = order(i)

        @pl.when(i == 0)
        def _():
            cdh_ref[...] = jnp.zeros_like(cdh_ref)
            ca_ref[...] = jnp.zeros_like(ca_ref)

        xc = xc_ref[...]
        r, ig, a, m, inv_m, nsp = _lru_gates(xc, wcat_ref, bias_ref, lam_ref)
        row = lax.broadcasted_iota(jnp.int32, (ROW_BLOCK, 1), 0)
        if reverse:
            coef = jnp.where(row == 0, ca_ref[0:1, :], pltpu.roll(a, 1, 0))
            h_nb = jnp.where(row == ROW_BLOCK - 1, halo_ref[0:1, :] * jnp.where(ib < nb - 1, 1.0, 0.0),
                             pltpu.roll(h_ref[...], ROW_BLOCK - 1, 0))
        else:
            coef = jnp.where(row == ROW_BLOCK - 1, ca_ref[0:1, :], pltpu.roll(a, ROW_BLOCK - 1, 0))
            h_nb = jnp.where(row == 0, halo_ref[7:8, :] * jnp.where(ib > 0, 1.0, 0.0), pltpu.roll(h_ref[...], 1, 0))
        dh = _scan_block(coef, dhs_ref[...], cdh_ref[0:1, :], not reverse, stage_ref)
        tmp_ref[...] = dh
        cdh_ref[0:1, :] = tmp_ref[pl.ds(edge, 1), :]
        tmp_ref[...] = a
        ca_ref[0:1, :] = tmp_ref[pl.ds(edge, 1), :]

        du = jnp.where(_row_ids(ROW_BLOCK, ib) >= PAD_ROWS, dh, 0.0)
        da = dh * h_nb
        dm = du * (ig * xc)
        di = du * (m * xc)
        dlog_a = da * a - dm * (a * a) * inv_m
        dr = dlog_a * nsp
        dpre = jnp.concatenate([dr * r * (1.0 - r), di * ig * (1.0 - ig)], axis=1)
        dxc_ref[...] = du * (m * ig) + _dot_nt(dpre, wcat_ref[...])
        _accumulate(dw_ref, _dot_tn(xc, dpre), i == 0)
        _accumulate(db_ref, jnp.sum(dpre, axis=0, keepdims=True), i == 0)
        _accumulate(dlam_ref, jnp.sum(dlog_a * r, axis=0, keepdims=True), i == 0)

        @pl.when(i == nb - 1)
        def _():
            dlam_ref[...] = dlam_ref[...] * (LRU_C * _sigmoid(-lam_ref[...]))

    return pl.pallas_call(
        body, name="lru_scan_bwd_b" if reverse else "lru_scan_bwd_f", grid=(nb,),
        in_specs=[spec, spec, spec, halo, wcat.spec, bias.spec, lam.spec],
        out_specs=[spec, pl.BlockSpec((LRU_W, 2 * LRU_W), lambda i: (0, 0)),
                   pl.BlockSpec((1, 2 * LRU_W), lambda i: (0, 0)), pl.BlockSpec((1, LRU_W), lambda i: (0, 0))],
        out_shape=[_sds((n, LRU_W), F32), _sds((LRU_W, 2 * LRU_W), F32), _sds((1, 2 * LRU_W), F32), _sds((1, LRU_W), F32)],
        scratch_shapes=[pltpu.VMEM((8, LRU_W), F32), pltpu.VMEM((8, LRU_W), F32), pltpu.VMEM((ROW_BLOCK, LRU_W), F32),
                        pltpu.VMEM((2, LRU_W // LANES, ROW_BLOCK, LANES), F32)],
        compiler_params=_params(("arbitrary",)),
    )(dhs, xc, h, h, wcat.array, bias.array, lam.array)


def _gla_rows(n):
    return 768 if n % 768 == 0 else ROW_BLOCK


def _gla_masks(reverse):
    t = lax.broadcasted_iota(jnp.int32, (CHUNK, CHUNK), 0)
    s = lax.broadcasted_iota(jnp.int32, (CHUNK, CHUNK), 1)
    if reverse:
        return (s >= t).astype(F32), s > t
    return (s <= t).astype(F32), s <= t


def _gla_gate(zg, wg_ref, bg_ref):
    pre = _dot(zg, wg_ref[...]) + bg_ref[...]
    g = (jnp.minimum(pre, 0.0) - jnp.log(1.0 + jnp.exp(-jnp.abs(pre)))) * (1.0 / GATE_NORM)
    return pre, g


def _gla_decays(gc, tri):
    b = jnp.dot(tri, gc, precision=lax.Precision.HIGHEST, preferred_element_type=F32)
    b_last = jnp.sum(gc, axis=0, keepdims=True)
    return jnp.exp(b), jnp.exp(-b), jnp.exp(b_last - b), jnp.exp(b_last)


def _gla_scan(z, wg, bg, reverse, exchange=None):
    n = z.shape[0]
    rb = _gla_rows(n)
    nb = n // rb
    cpb = rb // CHUNK
    order = (lambda i: nb - 1 - i) if reverse else (lambda i: i)
    chunks = range(cpb - 1, -1, -1) if reverse else range(cpb)

    def body(qk_ref, v_ref, zg_ref, wg_ref, bg_ref, o_ref, sall_ref, s_ref):
        i = pl.program_id(0)

        @pl.when(i == 0)
        def _():
            s_ref[...] = jnp.zeros_like(s_ref)

        tri, mask = _gla_masks(reverse)
        _, g = _gla_gate(zg_ref[...], wg_ref, bg_ref)
        heads = range(GLA_HEADS)
        ks = [slice(hd * GLA_DK, (hd + 1) * GLA_DK) for hd in heads]
        vs = [slice(hd * GLA_DV, (hd + 1) * GLA_DV) for hd in heads]
        qh, kb, v, el, p, intra, kv = {}, {}, {}, {}, {}, {}, {}
        for c in chunks:
            rows = slice(c * CHUNK, (c + 1) * CHUNK)
            eb, enb, ebl, el[c] = _gla_decays(g[rows], tri)
            qk = qk_ref[rows, :]
            q_all = (qk[:, :GLA_QK] * (GLA_DK ** -0.5) * eb).astype(BF16)
            k_all = (qk[:, GLA_QK:] * enb).astype(BF16)
            kb_all = (qk[:, GLA_QK:] * ebl).astype(BF16)
            v_all = v_ref[rows, :].astype(BF16)
            for hd in heads:
                qh[c, hd], kb[c, hd], v[c, hd] = q_all[:, ks[hd]], kb_all[:, ks[hd]], v_all[:, vs[hd]]
                p[c, hd] = _dot_nt(qh[c, hd], k_all[:, ks[hd]])
        for c in chunks:
            for hd in heads:
                intra[c, hd] = _dot(jnp.where(mask, p[c, hd], 0.0), v[c, hd])
                kv[c, hd] = _dot_tn(v[c, hd], kb[c, hd])
        state = [s_ref[:, ks[hd]] for hd in heads]
        for c in chunks:
            rows = slice(c * CHUNK, (c + 1) * CHUNK)
            for hd in heads:
                sall_ref[c, :, ks[hd]] = state[hd]
                o_ref[rows, vs[hd]] = intra[c, hd] + _dot_nt(qh[c, hd], state[hd])
                state[hd] = state[hd] * el[c][:, ks[hd]] + kv[c, hd]
        for hd in heads:
            s_ref[:, ks[hd]] = state[hd]

    return _hosting_call(
        exchange, body, name="gla_scan_b" if reverse else "gla_scan_f", grid=(nb,),
        in_specs=[pl.BlockSpec((rb, 512), lambda i: (order(i), 2)), pl.BlockSpec((rb, 512), lambda i: (order(i), 3)),
                  pl.BlockSpec((rb, LANES), lambda i: (order(i), ZG_COL_BLOCK)), wg.spec, bg.spec],
        out_specs=[pl.BlockSpec((rb, GLA_W), lambda i: (order(i), 0)),
                   pl.BlockSpec((cpb, GLA_DV, GLA_QK), lambda i: (order(i), 0, 0))],
        out_shape=[_sds((n, GLA_W), F32), _sds((n // CHUNK, GLA_DV, GLA_QK), F32)],
        scratch_shapes=[pltpu.VMEM((GLA_DV, GLA_QK), F32)],
        compiler_params=_params(("arbitrary",)),
    )(z, z, z, wg.array, bg.array)


def _gla_scan_bwd(do, z, states, wg, bg, reverse, exchange=None):
    n = z.shape[0]
    rb = _gla_rows(n)
    nb = n // rb
    cpb = rb // CHUNK
    order = (lambda i: i) if reverse else (lambda i: nb - 1 - i)
    chunks = range(cpb) if reverse else range(cpb - 1, -1, -1)

    def body(do_ref, qk_ref, v_ref, zg_ref, sall_ref, wg_ref, bg_ref,
             dqk_ref, dv_ref, dzg_ref, dwg_ref, dbg_ref, ds_ref):
        i = pl.program_id(0)

        @pl.when(i == 0)
        def _():
            ds_ref[...] = jnp.zeros_like(ds_ref)

        tri, mask = _gla_masks(reverse)
        tri_t, _ = _gla_masks(not reverse)
        zg = zg_ref[...]
        pre, g = _gla_gate(zg, wg_ref, bg_ref)
        heads = range(GLA_HEADS)
        ks = [slice(hd * GLA_DK, (hd + 1) * GLA_DK) for hd in heads]
        vs = [slice(hd * GLA_DV, (hd + 1) * GLA_DV) for hd in heads]
        dec, full, qh, kh, kb, v, dout, p, dp = {}, {}, {}, {}, {}, {}, {}, {}, {}
        for c in chunks:
            rows = slice(c * CHUNK, (c + 1) * CHUNK)
            dec[c] = _gla_decays(g[rows], tri)
            eb, enb, ebl, _ = dec[c]
            qk = qk_ref[rows, :]
            q_f = qk[:, :GLA_QK] * (GLA_DK ** -0.5) * eb
            k_f = qk[:, GLA_QK:] * enb
            kb_f = qk[:, GLA_QK:] * ebl
            full[c] = (q_f, k_f, kb_f)
            q_all, k_all, kb_all = q_f.astype(BF16), k_f.astype(BF16), kb_f.astype(BF16)
            v_all, do_all = v_ref[rows, :].astype(BF16), do_ref[rows, :].astype(BF16)
            for hd in heads:
                qh[c, hd], kh[c, hd], kb[c, hd] = q_all[:, ks[hd]], k_all[:, ks[hd]], kb_all[:, ks[hd]]
                v[c, hd], dout[c, hd] = v_all[:, vs[hd]], do_all[:, vs[hd]]
                p[c, hd] = _dot_nt(qh[c, hd], kh[c, hd])
                dp[c, hd] = _dot_nt(dout[c, hd], v[c, hd])
        dv_i, dqh, dkh, dsq, state = {}, {}, {}, {}, {}
        for c in chunks:
            for hd in heads:
                pm = jnp.where(mask, p[c, hd], 0.0).astype(BF16)
                dpm = jnp.where(mask, dp[c, hd], 0.0).astype(BF16)
                state[c, hd] = sall_ref[c, :, ks[hd]]
                dv_i[c, hd] = _dot_tn(pm, dout[c, hd])
                dqh[c, hd] = _dot(dpm, kh[c, hd]) + _dot(dout[c, hd], state[c, hd])
                dkh[c, hd] = _dot_tn(dpm, qh[c, hd])
                dsq[c, hd] = _dot_tn(dout[c, hd], qh[c, hd])
        dstate = [ds_ref[:, ks[hd]] for hd in heads]
        dkb, sds = {}, {}
        for c in chunks:
            rows = slice(c * CHUNK, (c + 1) * CHUNK)
            el = dec[c][3]
            for hd in heads:
                dv_ref[rows, vs[hd]] = (dv_i[c, hd] + _dot_nt(kb[c, hd], dstate[hd])).astype(BF16)
                dkb[c, hd] = _dot(v[c, hd], dstate[hd])
                sds[c, hd] = jnp.sum(state[c, hd] * dstate[hd], axis=0, keepdims=True)
                dstate[hd] = dstate[hd] * el[:, ks[hd]] + dsq[c, hd]
        for hd in heads:
            ds_ref[:, ks[hd]] = dstate[hd]
        dgs = [None] * cpb
        for c in chunks:
            rows = slice(c * CHUNK, (c + 1) * CHUNK)
            eb, enb, ebl, el = dec[c]
            q_f, k_f, kb_f = full[c]
            dqh_c = jnp.concatenate([dqh[c, hd] for hd in heads], axis=1)
            dkh_c = jnp.concatenate([dkh[c, hd] for hd in heads], axis=1)
            dkb_c = jnp.concatenate([dkb[c, hd] for hd in heads], axis=1)
            sds_c = jnp.concatenate([sds[c, hd] for hd in heads], axis=1)
            dqk_ref[rows, :] = jnp.concatenate([dqh_c * eb * (GLA_DK ** -0.5), dkh_c * enb + dkb_c * ebl], axis=1).astype(BF16)
            dkb_kb = dkb_c * kb_f
            db = dqh_c * q_f - dkh_c * k_f - dkb_kb
            db_last = el * sds_c + jnp.sum(dkb_kb, axis=0, keepdims=True)
            dgs[c] = jnp.dot(tri_t, db, precision=lax.Precision.HIGHEST, preferred_element_type=F32) + db_last
        dg = jnp.concatenate(dgs, axis=0)
        dpre = dg * _sigmoid(-pre) * (1.0 / GATE_NORM)
        dzg_ref[...] = _dot_nt(dpre, wg_ref[...]).astype(BF16)
        _accumulate(dwg_ref, _dot_tn(zg, dpre), i == 0)
        _accumulate(dbg_ref, jnp.sum(dpre, axis=0, keepdims=True), i == 0)

    return _hosting_call(
        exchange, body, name="gla_scan_bwd_b" if reverse else "gla_scan_bwd_f", grid=(nb,),
        in_specs=[pl.BlockSpec((rb, GLA_W), lambda i: (order(i), 0)),
                  pl.BlockSpec((rb, 512), lambda i: (order(i), 2)), pl.BlockSpec((rb, 512), lambda i: (order(i), 3)),
                  pl.BlockSpec((rb, LANES), lambda i: (order(i), ZG_COL_BLOCK)),
                  pl.BlockSpec((cpb, GLA_DV, GLA_QK), lambda i: (order(i), 0, 0)), wg.spec, bg.spec],
        out_specs=[pl.BlockSpec((rb, 512), lambda i: (order(i), 0)), pl.BlockSpec((rb, 512), lambda i: (order(i), 0)),
                   pl.BlockSpec((rb, LANES), lambda i: (order(i), 0)),
                   pl.BlockSpec((LANES, GLA_QK), lambda i: (0, 0)), pl.BlockSpec((1, GLA_QK), lambda i: (0, 0))],
        out_shape=[_sds((n, 512), BF16), _sds((n, 512), BF16), _sds((n, LANES), BF16), _sds((LANES, GLA_QK), F32),
                   _sds((1, GLA_QK), F32)],
        scratch_shapes=[pltpu.VMEM((GLA_DV, GLA_QK), F32)],
        compiler_params=_params(("arbitrary",)),
    )(do, z, z, z, states, wg.array, bg.array)


NORM_NAMES = ("norm_mix_pre", "norm_mix_post", "norm_mlp_pre", "norm_mlp_post")
VEC512_NAMES = ("conv_b", "lru_ba_f", "lru_bx_f", "lru_lambda_f", "lru_ba_b", "lru_bx_b", "lru_lambda_b", "gla_head_norm")
VEC256_NAMES = ("gla_bg_f", "gla_bg_b")
LRU_MAT_NAMES = ("lru_wa_f", "lru_wx_f", "lru_wa_b", "lru_wx_b")
DIRS = ("f", "b")


def _prepare_params(w, gathered, depth):
    row_names = NORM_NAMES + ("conv_b", "gla_head_norm")
    ins = ([w[nm] for nm in row_names] + [w["lru_ba_" + d] for d in DIRS] + [w["lru_bx_" + d] for d in DIRS]
           + [w["lru_lambda_" + d] for d in DIRS] + [w["gla_bg_" + d] for d in DIRS]
           + [w["lru_wa_" + d].reshape(depth, LRU_W, LRU_HD) for d in DIRS]
           + [w["lru_wx_" + d].reshape(depth, LRU_W, LRU_HD) for d in DIRS]
           + [gathered["conv_w"], gathered["gla_wg_f"], gathered["gla_wg_b"], gathered["meta_tokens"]])
    n_rows = len(row_names)

    def body(*refs):
        rows_in = refs[:n_rows]
        ba, bx, lam, bg, wa, wx = (refs[n_rows + 2 * t:n_rows + 2 * t + 2] for t in range(6))
        convw_g, wgf_g, wgb_g, meta_g = refs[n_rows + 12:n_rows + 16]
        outs = refs[n_rows + 16:]
        rows_out = outs[:n_rows]
        convw, wcat, bias, lam_o, wg, bg_o, meta = outs[n_rows:]
        for l in range(depth):
            for src, dst in zip(rows_in, rows_out):
                dst[l] = src[pl.ds(l, 1), :]
            convw[l] = jnp.zeros((8, LRU_W), F32)
            for j in range(N_DEV):
                convw[l, 0:4, j * 64:(j + 1) * 64] = convw_g[j, l]
            for d in range(2):
                wcat[l, d] = jnp.zeros((LRU_W, 2 * LRU_W), BF16)
                for hd in range(LRU_HEADS):
                    rs = slice(hd * LRU_HD, (hd + 1) * LRU_HD)
                    wcat[l, d, rs, hd * LRU_HD:(hd + 1) * LRU_HD] = wa[d][l, rs, :].astype(BF16)
                    wcat[l, d, rs, LRU_W + hd * LRU_HD:LRU_W + (hd + 1) * LRU_HD] = wx[d][l, rs, :].astype(BF16)
                bias[l, d, :, 0:LRU_W] = ba[d][pl.ds(l, 1), :]
                bias[l, d, :, LRU_W:2 * LRU_W] = bx[d][pl.ds(l, 1), :]
                lam_o[l, d] = lam[d][pl.ds(l, 1), :]
                bg_o[l, d] = bg[d][pl.ds(l, 1), :]
                wg[l, d] = jnp.zeros((LANES, GLA_QK), BF16)
                src = wgf_g if d == 0 else wgb_g
                for j in range(N_DEV):
                    wg[l, d, d * GLA_RANK:(d + 1) * GLA_RANK, j * 32:(j + 1) * 32] = src[j, l].astype(BF16)
        for j in range(N_DEV):
            meta[:, j * LANES:(j + 1) * LANES] = meta_g[j]

    out_shape = ([_sds((depth, 1, w[nm].shape[1]), F32) for nm in row_names]
                 + [_sds((depth, 8, LRU_W), F32), _sds((depth, 2, LRU_W, 2 * LRU_W), BF16), _sds((depth, 2, 1, 2 * LRU_W), F32),
                    _sds((depth, 2, 1, LRU_W), F32), _sds((depth, 2, LANES, GLA_QK), BF16), _sds((depth, 2, 1, GLA_QK), F32),
                    _sds((N_META, D_MODEL), F32)])
    outs = pl.pallas_call(
        body, name="prepare_params", in_specs=[VMEM_SPEC] * len(ins), out_specs=[VMEM_SPEC] * len(out_shape),
        out_shape=out_shape, compiler_params=_params(None, 32),
    )(*ins)
    prepared = dict(zip(row_names, outs[:n_rows]))
    prepared.update(zip(("conv_w", "wcat", "lru_bias", "lru_lam", "wg", "gla_bg", "meta_tokens"), outs[n_rows:]))
    return prepared


class _WeightGather:
    def __init__(self, shards, p):
        self.shards, self.p = shards, p

    def exchange(self, items):
        ex = _Exchange()
        for nm, l in items:
            ex.add(self.shards[nm], _layer_of(l), _sds((N_DEV,) + self.shards[nm].shape[1:], BF16), _slab)
        return ex

    def install(self, items, landed):
        for (nm, l), g in zip(items, landed):
            if nm == "w_in":
                g = jnp.pad(jnp.concatenate([g[j] for j in range(N_DEV)], axis=1), ((0, 0), (0, Z_W - D_IN)))
            elif nm == "w_out":
                g = g.reshape(D_MODEL, D_MODEL)
            elif nm == "w_mlp_down":
                g = g.reshape(D_FF, D_MODEL)
            self.p.setdefault(nm, {})[l] = g


class _GradOutbox:
    def __init__(self):
        self.pending, self.received = [], {}

    def put(self, nm, l, slabs):
        self.pending.append((nm, l, slabs))

    def exchange(self, only=None):
        ex, keys, rest = _Exchange(), [], []
        for nm, l, slabs in self.pending:
            if only is None or nm == only:
                ex.add(slabs, _slab, _sds(slabs.shape, slabs.dtype), _slab)
                keys.append((nm, l))
            else:
                rest.append((nm, l, slabs))
        self.pending = rest
        return ex, keys

    def store(self, keys, landed):
        self.received.update(zip(keys, landed))


def _layer_fwd(h, l, p, gather, depth):
    lp = lambda name, *index: _LayerParam(p[name], l, *index)
    hn, z = _norm_in_proj(h, lp("norm_mix_pre"), p["w_in"][l])
    xc = _conv_fwd(z, lp("conv_w"), lp("conv_b"))
    s = dict(h=h, hn=hn, z=z, xc=xc)
    for d, name in enumerate(DIRS):
        s["h_" + name] = _lru_scan(xc, lp("wcat", d), lp("lru_bias", d), lp("lru_lam", d), d == 1)
        items = [(("w_mlp_up", "w_mlp_down")[d], l)]
        s["o_" + name], s["s_" + name], *landed = _gla_scan(z, lp("wg", d), lp("gla_bg", d), d == 1, gather.exchange(items))
        gather.install(items, landed)
    s["ymix"] = _mix_epilogue(s["h_f"], s["h_b"], s["o_f"], s["o_b"], z, lp("gla_head_norm"))
    s["mix"], s["h_mid"] = _out_proj(s["ymix"], p["w_out"][l], h, lp("norm_mix_post"))
    items = [("w_in", l + 1), ("w_out", l + 1)] if l + 1 < depth else []
    s["hn2"], s["up"], s["ff"], h_out, *landed = _mlp_fwd(
        s["h_mid"], lp("norm_mlp_pre"), p["w_mlp_up"][l], p["w_mlp_down"][l], lp("norm_mlp_post"), gather.exchange(items))
    gather.install(items, landed)
    return h_out, s


def _layer_bwd(dh_out, l, p, s, outbox):
    lp = lambda name, *index: _LayerParam(p[name], l, *index)
    g = {}
    ex, keys = outbox.exchange()
    d_ff, dup, dh_mid, g["norm_mlp_post"], g["norm_mlp_pre"], *landed = _mlp_bwd(
        dh_out, s["ff"], s["up"], s["h_mid"], lp("norm_mlp_pre"), p["w_mlp_up"][l], p["w_mlp_down"][l], lp("norm_mlp_post"), ex)
    outbox.store(keys, landed)
    outbox.put("w_mlp_down", l, _matmul_tn(s["up"], d_ff, "grad_w_down", a_map=_relu_squared)
               .reshape(N_DEV, D_FF // N_DEV, D_MODEL))
    outbox.put("w_mlp_up", l, _matmul_tn(s["hn2"], dup, "grad_w_up", column_slabs=True))
    dmix, dymix, g["norm_mix_post"] = _out_proj_bwd(dh_mid, s["mix"], lp("norm_mix_post"), p["w_out"][l])
    outbox.put("w_out", l, _matmul_tn(s["ymix"], dmix, "grad_w_out").reshape(N_DEV, D_MODEL // N_DEV, D_MODEL))
    dhs, dgate, do, dgout, g["gla_head_norm"] = _mix_epilogue_bwd(
        dymix, s["h_f"], s["h_b"], s["o_f"], s["o_b"], s["z"], lp("gla_head_norm"))
    dqk, dv, dzg, dxc = {}, {}, {}, {}
    for d, name in enumerate(DIRS):
        ex, keys = outbox.exchange(only=("w_mlp_down", "w_mlp_up")[d])
        dqk[name], dv[name], dzg[name], g["wg_" + name], g["gla_bg_" + name], *landed = _gla_scan_bwd(
            do, s["z"], s["s_" + name], lp("wg", d), lp("gla_bg", d), d == 1, ex)
        outbox.store(keys, landed)
        dxc[name], g["wcat_" + name], g["lru_bias_" + name], g["lru_lambda_" + name] = _lru_scan_bwd(
            dhs, s["xc"], s["h_" + name], lp("wcat", d), lp("lru_bias", d), lp("lru_lam", d), d == 1)
    dxbr, g["conv_w"], g["conv_b"] = _conv_bwd(dxc["f"], dxc["b"], s["z"], lp("conv_w"))
    dz, dh_in, g["norm_mix_pre"] = _in_proj_bwd(
        (dxbr, dgate, dqk["f"], dqk["b"], dv["f"], dv["b"], dgout, dzg["f"], dzg["b"]),
        p["w_in"][l], s["h"], lp("norm_mix_pre"), dh_mid)
    return dh_in, g, dz


def _w_in_slabs(grad_w_in):
    shard = D_IN // N_DEV
    return jnp.stack([grad_w_in[:, j * shard:(j + 1) * shard] for j in range(N_DEV)])


def _folded_block(hd):
    return slice((hd // 2) * LRU_HD, (hd // 2 + 1) * LRU_HD), slice((hd % 2) * LRU_HD, (hd % 2 + 1) * LRU_HD)


def _pack_small_grads(grads, dh0, depth):
    per_layer = ("norm_mix_pre", "norm_mix_post", "norm_mlp_pre", "norm_mlp_post", "conv_b", "gla_head_norm",
                 "lru_bias_f", "lru_bias_b", "lru_lambda_f", "lru_lambda_b", "gla_bg_f", "gla_bg_b",
                 "wcat_f", "wcat_b", "conv_w", "wg_f", "wg_b")
    ins = [grads[l][nm] for l in range(depth) for nm in per_layer]
    k = len(per_layer)
    meta_rows = PAD_ROWS // N_META

    def body(*refs):
        g = [dict(zip(per_layer, refs[l * k:(l + 1) * k])) for l in range(depth)]
        dh0_ref = refs[depth * k]
        norms, v512, v256, mats, convw, wgf, wgb, meta = refs[depth * k + 1:]
        v256[...] = jnp.zeros_like(v256)
        for l in range(depth):
            for p_, nm in enumerate(NORM_NAMES):
                norms[pl.ds(2 * p_ + l, 1), :] = g[l][nm][...]
            rows512 = [g[l]["conv_b"][...], g[l]["lru_bias_f"][:, 0:LRU_W], g[l]["lru_bias_f"][:, LRU_W:2 * LRU_W],
                       g[l]["lru_lambda_f"][...], g[l]["lru_bias_b"][:, 0:LRU_W], g[l]["lru_bias_b"][:, LRU_W:2 * LRU_W],
                       g[l]["lru_lambda_b"][...], g[l]["gla_head_norm"][...]]
            for p_, row in enumerate(rows512):
                v512[pl.ds(2 * p_ + l, 1), :] = row
            for p_, nm in enumerate(("gla_bg_f", "gla_bg_b")):
                v256[pl.ds(2 * p_ + l, 1), :] = g[l][nm][...]
            for d, name in enumerate(DIRS):
                for hd in range(LRU_HEADS):
                    rs = slice(hd * LRU_HD, (hd + 1) * LRU_HD)
                    dst_rows, dst_cols = _folded_block(hd)
                    mats[2 * d, l, dst_rows, dst_cols] = g[l]["wcat_" + name][rs, hd * LRU_HD:(hd + 1) * LRU_HD].astype(BF16)
                    mats[2 * d + 1, l, dst_rows, dst_cols] = (
                        g[l]["wcat_" + name][rs, LRU_W + hd * LRU_HD:LRU_W + (hd + 1) * LRU_HD].astype(BF16))
            for j in range(N_DEV):
                convw[j, l] = g[l]["conv_w"][0:4, j * 64:(j + 1) * 64]
                wgf[j, l] = g[l]["wg_f"][0:GLA_RANK, j * 32:(j + 1) * 32]
                wgb[j, l] = g[l]["wg_b"][GLA_RANK:2 * GLA_RANK, j * 32:(j + 1) * 32]
        for j in range(N_DEV):
            meta[j] = dh0_ref[:, j * LANES:(j + 1) * LANES]

    out_shape = [_sds((8, D_MODEL), F32), _sds((16, LRU_W), F32), _sds((8, GLA_QK), F32),
                 _sds((4, depth, LRU_W // 2, 2 * LRU_HD), BF16),
                 _sds((N_DEV, depth, 4, 64), F32), _sds((N_DEV, depth, GLA_RANK, 32), F32), _sds((N_DEV, depth, GLA_RANK, 32), F32),
                 _sds((N_DEV, N_META, LANES), F32)]
    return pl.pallas_call(
        body, name="pack_small_grads", grid=(1,),
        in_specs=[VMEM_SPEC] * (depth * k) + [pl.BlockSpec((N_META, D_MODEL), lambda i: (meta_rows, 0))],
        out_specs=[VMEM_SPEC] * len(out_shape), out_shape=out_shape, compiler_params=_params(("arbitrary",), 32),
    )(*ins, dh0)


def _my_index():
    return 4 * lax.axis_index("x") + 2 * lax.axis_index("y") + lax.axis_index("c")


def _peer(k):
    x, y, c = lax.axis_index("x"), lax.axis_index("y"), lax.axis_index("c")
    px = x ^ ((k >> 2) & 1)
    py = y ^ ((k >> 1) & 1)
    pc = c ^ (k & 1)
    return (px, py, pc), 4 * px + 2 * py + pc


class _Exchange:
    def __init__(self):
        self.inputs, self.out_shapes, self.transfers = [], [], []

    def add(self, array, src, out_shape, dst):
        self.transfers.append((len(self.inputs), src, len(self.out_shapes), dst))
        self.inputs.append(array)
        self.out_shapes.append(out_shape)
        return len(self.out_shapes) - 1

    def sem_shapes(self):
        nsem = len(self.transfers) * (N_DEV - 1)
        return [pltpu.SemaphoreType.DMA((nsem,)), pltpu.SemaphoreType.DMA((nsem,)),
                pltpu.SemaphoreType.DMA((len(self.transfers),))]

    def _local(self, ins, outs, sems):
        me = _my_index()
        return [pltpu.make_async_copy(src(ins[a], me), dst(outs[b], me), sems[2].at[t])
                for t, (a, src, b, dst) in enumerate(self.transfers)]

    def _remote(self, ins, outs, sems, t, k, sending):
        a, src, b, dst = self.transfers[t]
        peer, peer_index = _peer(k)
        sem = t * (N_DEV - 1) + k - 1
        return pltpu.make_async_remote_copy(
            src_ref=src(ins[a], peer_index), dst_ref=dst(outs[b], _my_index() if sending else peer_index),
            send_sem=sems[0].at[sem], recv_sem=sems[1].at[sem], device_id=peer, device_id_type=MESH_ID)

    def start(self, ins, outs, sems):
        for cp in self._local(ins, outs, sems):
            cp.start()
        for k in range(1, N_DEV):
            for t in range(len(self.transfers)):
                self._remote(ins, outs, sems, t, k, True).start()

    def wait(self, ins, outs, sems):
        for k in range(1, N_DEV):
            for t in range(len(self.transfers)):
                self._remote(ins, outs, sems, t, k, False).wait_recv()
        for k in range(1, N_DEV):
            for t in range(len(self.transfers)):
                self._remote(ins, outs, sems, t, k, True).wait_send()
        for cp in self._local(ins, outs, sems):
            cp.wait()

    def run(self, name):
        n_in, n_out = len(self.inputs), len(self.out_shapes)

        def body(*refs):
            ins, outs, sems = refs[:n_in], refs[n_in:n_in + n_out], refs[n_in + n_out:]
            self.start(ins, outs, sems)
            self.wait(ins, outs, sems)

        return pl.pallas_call(
            body, name=name, in_specs=[ANY_SPEC] * n_in, out_specs=[ANY_SPEC] * n_out, out_shape=self.out_shapes,
            scratch_shapes=self.sem_shapes(), compiler_params=pltpu.CompilerParams(has_side_effects=True),
        )(*self.inputs)


def _hosting_call(exchange, body, *, name, grid, in_specs, out_specs, out_shape, scratch_shapes, compiler_params):
    if exchange is None or not exchange.transfers:
        return pl.pallas_call(body, name=name, grid=grid, in_specs=in_specs, out_specs=out_specs, out_shape=out_shape,
                              scratch_shapes=scratch_shapes, compiler_params=compiler_params)
    n_in, n_out, n_scr = len(in_specs), len(out_specs), len(scratch_shapes)
    x_in, x_out = len(exchange.inputs), len(exchange.out_shapes)

    def hosted(*refs):
        ins, x_ins = refs[:n_in], refs[n_in:n_in + x_in]
        o0 = n_in + x_in
        outs, x_outs = refs[o0:o0 + n_out], refs[o0 + n_out:o0 + n_out + x_out]
        s0 = o0 + n_out + x_out
        scratch, sems = refs[s0:s0 + n_scr], refs[s0 + n_scr:]
        ids = [pl.program_id(a) for a in range(len(grid))]
        first = functools.reduce(jnp.logical_and, [i == 0 for i in ids])
        last = functools.reduce(jnp.logical_and, [i == g - 1 for i, g in zip(ids, grid)])

        @pl.when(first)
        def _():
            exchange.start(x_ins, x_outs, sems)

        body(*ins, *outs, *scratch)

        @pl.when(last)
        def _():
            exchange.wait(x_ins, x_outs, sems)

    call = pl.pallas_call(
        hosted, name=name, grid=grid, in_specs=list(in_specs) + [ANY_SPEC] * x_in,
        out_specs=list(out_specs) + [ANY_SPEC] * x_out, out_shape=list(out_shape) + list(exchange.out_shapes),
        scratch_shapes=list(scratch_shapes) + exchange.sem_shapes(), compiler_params=compiler_params)
    return lambda *operands: call(*operands, *exchange.inputs)


def _whole(ref, j):
    return ref


def _slab(ref, j):
    return ref.at[j]


def _layer_of(l):
    return lambda ref, j: ref.at[l]


def _slab_layer(l):
    return lambda ref, j: ref.at[j, l]


def _adamw(g, w, m, v):
    nm = ADAM_B1 * m + (1.0 - ADAM_B1) * g
    nv = ADAM_B2 * v + (1.0 - ADAM_B2) * jnp.square(g)
    m_hat = nm / (1.0 - ADAM_B1 ** ADAM_STEP)
    v_hat = nv / (1.0 - ADAM_B2 ** ADAM_STEP)
    return -ADAM_LR * (m_hat / (jnp.sqrt(v_hat) + ADAM_EPS) + ADAM_WD * w), nm, nv


def _sum_parts(p_ref):
    g = p_ref[0].astype(F32)
    for j in range(1, N_DEV):
        g = g + p_ref[j].astype(F32)
    return g


def _adamw_sharded(parts, w, m, v, name):
    shape = w.shape
    lead, (rows, cols) = shape[:-2], shape[-2:]
    tr = min(rows, ROW_BLOCK)
    assert rows % tr == 0
    steps = rows // tr
    nl = len(lead)
    spec = pl.BlockSpec((None,) * nl + (tr, cols), lambda *idx: idx + (0,))
    per_layer = isinstance(parts, (list, tuple))
    if per_layer:
        def part_spec(l):
            return pl.BlockSpec((N_DEV, tr, cols), lambda li, r: (0, jnp.where(li == l, r, jnp.where(li < l, 0, steps - 1)), 0))
        part_specs = [part_spec(l) for l in range(len(parts))]
    else:
        parts = [parts]
        part_specs = [pl.BlockSpec((N_DEV,) + (None,) * nl + (tr, cols), lambda *idx: (0,) + idx + (0,))]
    count = len(parts)

    def body(*refs):
        p_refs = refs[:count]
        w_ref, m_ref, v_ref, g_ref, d_ref, nm_ref, nv_ref = refs[count:]

        def update(p_ref):
            g = _sum_parts(p_ref)
            g_ref[...] = g
            d_ref[...], nm_ref[...], nv_ref[...] = _adamw(g, w_ref[...], m_ref[...], v_ref[...])

        if per_layer:
            for l in range(count):
                pl.when(pl.program_id(0) == l)(functools.partial(update, p_refs[l]))
        else:
            update(p_refs[0])

    return pl.pallas_call(
        body, name=name, grid=lead + (steps,),
        in_specs=part_specs + [spec, spec, spec], out_specs=[spec] * 4, out_shape=[_sds(shape, F32)] * 4,
        compiler_params=_params(("arbitrary",) * (nl + 1)),
    )(*parts, w, m, v)


def _adamw_replicated(gathered, w, m, v, depth):
    names = NORM_NAMES + VEC512_NAMES + VEC256_NAMES + LRU_MAT_NAMES
    count = len(names)

    def body(*refs):
        norms, v512, v256, mats = refs[:4]
        w_refs, m_refs, v_refs = (refs[4 + t * count:4 + (t + 1) * count] for t in range(3))
        outs = refs[4 + 3 * count:4 + 7 * count]
        sum_norms, sum_512, sum_256, unfolded = refs[4 + 7 * count:]
        sum_norms[...] = _sum_parts(norms)
        sum_512[...] = _sum_parts(v512)
        sum_256[...] = _sum_parts(v256)
        for n_, nm in enumerate(names):
            if nm in NORM_NAMES:
                g = sum_norms[pl.ds(depth * NORM_NAMES.index(nm), depth), :]
            elif nm in VEC512_NAMES:
                g = sum_512[pl.ds(depth * VEC512_NAMES.index(nm), depth), :]
            elif nm in VEC256_NAMES:
                g = sum_256[pl.ds(depth * VEC256_NAMES.index(nm), depth), :]
            else:
                p_ = LRU_MAT_NAMES.index(nm)
                folded = mats[0, p_].astype(F32)
                for j in range(1, N_DEV):
                    folded = folded + mats[j, p_].astype(F32)
                for hd in range(LRU_HEADS):
                    src_rows, src_cols = _folded_block(hd)
                    unfolded[:, hd * LRU_HD:(hd + 1) * LRU_HD, :] = folded[:, src_rows, src_cols]
                g = unfolded[...]
            delta, nm_, nv_ = _adamw(g, w_refs[n_][...], m_refs[n_][...], v_refs[n_][...])
            outs[n_][...] = g
            outs[count + n_][...] = delta
            outs[2 * count + n_][...] = nm_
            outs[3 * count + n_][...] = nv_

    shapes = [_sds(w[nm].shape, F32) for nm in names]
    ins = list(gathered) + [t[nm] for t in (w, m, v) for nm in names]
    outs = pl.pallas_call(
        body, name="adamw_replicated", in_specs=[VMEM_SPEC] * len(ins), out_specs=[VMEM_SPEC] * (4 * count),
        out_shape=shapes * 4,
        scratch_shapes=[pltpu.VMEM(gathered[0].shape[1:], F32), pltpu.VMEM(gathered[1].shape[1:], F32),
                        pltpu.VMEM(gathered[2].shape[1:], F32), pltpu.VMEM((depth, LRU_W, LRU_HD), F32)],
        compiler_params=_params(None, 48),
    )(*ins)
    return [dict(zip(names, outs[t * count:(t + 1) * count])) for t in range(4)]


WEIGHT_NAMES = ("meta_tokens", "norm_mix_pre", "norm_mix_post", "norm_mlp_pre", "norm_mlp_post", "w_in", "conv_w", "conv_b",
                "lru_wa_f", "lru_ba_f", "lru_wx_f", "lru_bx_f", "lru_lambda_f", "lru_wa_b", "lru_ba_b", "lru_wx_b",
                "lru_bx_b", "lru_lambda_b", "gla_wg_f", "gla_bg_f", "gla_wg_b", "gla_bg_b", "gla_head_norm", "w_out",
                "w_mlp_up", "w_mlp_down")
MATMUL_WEIGHTS = ("w_in", "w_out", "w_mlp_up", "w_mlp_down")
SMALL_SHARDED = ("conv_w", "gla_wg_f", "gla_wg_b", "meta_tokens")


def kernel(x, meta_tokens, norm_mix_pre, norm_mix_post, norm_mlp_pre, norm_mlp_post, w_in, conv_w, conv_b, lru_wa_f, lru_ba_f, lru_wx_f, lru_bx_f, lru_lambda_f, lru_wa_b, lru_ba_b, lru_wx_b, lru_bx_b, lru_lambda_b, gla_wg_f, gla_bg_f, gla_wg_b, gla_bg_b, gla_head_norm, w_out, w_mlp_up, w_mlp_down, loss_target, m_meta_tokens, m_norm_mix_pre, m_norm_mix_post, m_norm_mlp_pre, m_norm_mlp_post, m_w_in, m_conv_w, m_conv_b, m_lru_wa_f, m_lru_ba_f, m_lru_wx_f, m_lru_bx_f, m_lru_lambda_f, m_lru_wa_b, m_lru_ba_b, m_lru_wx_b, m_lru_bx_b, m_lru_lambda_b, m_gla_wg_f, m_gla_bg_f, m_gla_wg_b, m_gla_bg_b, m_gla_head_norm, m_w_out, m_w_mlp_up, m_w_mlp_down, v_meta_tokens, v_norm_mix_pre, v_norm_mix_post, v_norm_mlp_pre, v_norm_mlp_post, v_w_in, v_conv_w, v_conv_b, v_lru_wa_f, v_lru_ba_f, v_lru_wx_f, v_lru_bx_f, v_lru_lambda_f, v_lru_wa_b, v_lru_ba_b, v_lru_wx_b, v_lru_bx_b, v_lru_lambda_b, v_gla_wg_f, v_gla_bg_f, v_gla_wg_b, v_gla_bg_b, v_gla_head_norm, v_w_out, v_w_mlp_up, v_w_mlp_down):
    args = locals()
    w = {nm: args[nm] for nm in WEIGHT_NAMES}
    m = {nm: args["m_" + nm] for nm in WEIGHT_NAMES}
    v = {nm: args["v_" + nm] for nm in WEIGHT_NAMES}
    depth = w_in.shape[0]

    shards = {nm: w[nm].astype(BF16) for nm in MATMUL_WEIGHTS}
    first_items = [("w_in", 0), ("w_out", 0)]
    p = {}
    gather = _WeightGather(shards, p)
    ex = gather.exchange(first_items)
    for nm in SMALL_SHARDED:
        ex.add(w[nm], _whole, _sds((N_DEV,) + w[nm].shape, F32), _slab)
    landed = ex.run("all_gather")
    gather.install(first_items, landed[:len(first_items)])
    p.update(_prepare_params(w, dict(zip(SMALL_SHARDED, landed[len(first_items):])), depth))

    h = jnp.concatenate([jnp.zeros((PAD_ROWS, D_MODEL), F32), p["meta_tokens"], x[0]], axis=0)
    saved = []
    for l in range(depth):
        h, s = _layer_fwd(h, l, p, gather, depth)
        saved.append(s)
    dh, loss_part = _loss_and_grad(h, loss_target[0])
    loss = lax.psum(loss_part[0, 0], ("x", "y", "c"))

    outbox = _GradOutbox()
    grads = [None] * depth
    for l in reversed(range(depth)):
        dh, grads[l], dz = _layer_bwd(dh, l, p, saved[l], outbox)
        if l > 0:
            outbox.put("w_in", l, _w_in_slabs(_matmul_tn(saved[l]["hn"], dz, "grad_w_in")))
    grad_x = dh[PAD_ROWS + N_META:][None]

    small = _pack_small_grads(grads, dh, depth)
    rep_bufs, small_slabs = small[:4], small[4:]
    ex, keys = outbox.exchange()
    for g in small_slabs:
        ex.add(g, _slab, _sds(g.shape, F32), _slab)
    for g in rep_bufs:
        ex.add(g, _whole, _sds((N_DEV,) + g.shape, g.dtype), _slab)
    grad_w_in, *landed = _matmul_tn(saved[0]["hn"], dz, "grad_w_in", exchange=ex)
    outbox.store(keys, landed[:len(keys)])
    small_received = landed[len(keys):len(keys) + len(small_slabs)]
    rep_received = landed[len(keys) + len(small_slabs):]
    outbox.put("w_in", 0, _w_in_slabs(grad_w_in))
    ex, keys = outbox.exchange()
    outbox.store(keys, ex.run("exchange_grads"))

    results = [{}, {}, {}, {}]
    for nm in MATMUL_WEIGHTS:
        parts = [outbox.received[(nm, l)] for l in range(depth)]
        for t, out in enumerate(_adamw_sharded(parts, w[nm], m[nm], v[nm], "adamw_" + nm)):
            results[t][nm] = out
    for nm, parts in zip(SMALL_SHARDED, small_received):
        for t, out in enumerate(_adamw_sharded(parts, w[nm], m[nm], v[nm], "adamw_" + nm)):
            results[t][nm] = out

    def kernel_side(tree):
        return {nm: tree[nm].reshape(depth, LRU_W, LRU_HD) if nm in LRU_MAT_NAMES else tree[nm]
                for nm in NORM_NAMES + VEC512_NAMES + VEC256_NAMES + LRU_MAT_NAMES}

    for t, tree in enumerate(_adamw_replicated(rep_received, kernel_side(w), kernel_side(m), kernel_side(v), depth)):
        for nm, out in tree.items():
            results[t][nm] = out.reshape(w[nm].shape)
    return (loss, grad_x, *[results[t][nm] for t in range(4) for nm in WEIGHT_NAMES])
```

```python
import functools

import jax
import jax.numpy as jnp
from jax import lax
from jax.experimental import pallas as pl
from jax.experimental.pallas import tpu as pltpu

F32 = jnp.float32
BF16 = jnp.bfloat16

N_DEV = 8
D_MODEL = 1024
N_META = 16
ROW_BLOCK = 256
PAD_ROWS = ROW_BLOCK - N_META
CHUNK = 128
LRU_W = 512
LRU_HEADS = 8
LRU_HD = 64
LRU_C = 8.0
GLA_HEADS = 4
GLA_DK = 64
GLA_DV = 128
GLA_QK = GLA_HEADS * GLA_DK
GLA_W = GLA_HEADS * GLA_DV
GLA_RANK = 16
GATE_NORM = 16.0
D_FF = 4096
D_IN = 2592
Z_W = 2688
ZG_COL_BLOCK = 2560 // 128
EPS = 1e-6
LANES = 128

ADAM_LR = 0.001
ADAM_B1 = 0.9
ADAM_B2 = 0.999
ADAM_EPS = 1e-08
ADAM_WD = 0.01
ADAM_STEP = 10
ADAM_ROWS = 512

VMEM_SPEC = pl.BlockSpec(memory_space=pltpu.VMEM)
ANY_SPEC = pl.BlockSpec(memory_space=pl.ANY)
MESH_ID = pl.DeviceIdType.MESH


def _sds(shape, dtype):
    return jax.ShapeDtypeStruct(shape, dtype)


def _params(sem=None, vmem_mb=None):
    kw = {}
    if sem is not None:
        kw["dimension_semantics"] = sem
    if vmem_mb is not None:
        kw["vmem_limit_bytes"] = vmem_mb * 2 ** 20
    return pltpu.CompilerParams(**kw)


def _row_tile(n, cap=768):
    for t in (768, 512, 384, 256):
        if t <= cap and n % t == 0:
            return t
    raise ValueError(n)


def _col_tile(k):
    for t in (1024, 896, 768, 640, 512, 384, 256, 128):
        if k % t == 0:
            return t
    raise ValueError(k)


def _sigmoid(x):
    return 0.5 * jnp.tanh(0.5 * x) + 0.5


def _gelu_and_grad(x):
    c = 0.7978845608028654
    inner = c * (x + 0.044715 * x * x * x)
    t = jnp.tanh(inner)
    gelu = 0.5 * x * (1.0 + t)
    dgelu = 0.5 * (1.0 + t) + 0.5 * x * (1.0 - t * t) * c * (1.0 + 3.0 * 0.044715 * x * x)
    return gelu, dgelu


def _one_minus_square(a, log_a):
    return jnp.tanh(-log_a) * (1.0 + a * a)


def _rms_fwd(x, g):
    rs = lax.rsqrt(jnp.mean(x * x, axis=-1, keepdims=True) + EPS)
    return x * rs * g


def _rms_bwd(x, g, dy):
    rs = lax.rsqrt(jnp.mean(x * x, axis=-1, keepdims=True) + EPS)
    xh = x * rs
    dyg = dy * g
    dx = rs * (dyg - xh * jnp.mean(dyg * xh, axis=-1, keepdims=True))
    return dx, jnp.sum(dy * xh, axis=0, keepdims=True)


def _dot(a, b):
    return jnp.dot(a.astype(BF16), b.astype(BF16), preferred_element_type=F32)


def _dot_nt(a, b):
    return lax.dot_general(a.astype(BF16), b.astype(BF16), (((1,), (1,)), ((), ())), preferred_element_type=F32)


def _dot_tn(a, b):
    return lax.dot_general(a.astype(BF16), b.astype(BF16), (((0,), (0,)), ((), ())), preferred_element_type=F32)


class _LayerParam:
    def __init__(self, array, *index):
        self.array = array
        self.index = index

    @property
    def spec(self):
        lead = len(self.index)
        tail = self.array.shape[lead:]
        index = self.index
        return pl.BlockSpec((None,) * lead + tail, lambda *_: index + (0,) * len(tail))


def _row_ids(rows, block_index):
    return block_index * rows + lax.broadcasted_iota(jnp.int32, (rows, 1), 0)


def _accumulate(ref, value, first):
    @pl.when(first)
    def _():
        ref[...] = value

    @pl.when(jnp.logical_not(first))
    def _():
        ref[...] += value


def _norm_in_proj(h, g, w, exchange=None):
    n, d = h.shape
    zw = w.shape[1]
    tr = _row_tile(n)

    def body(h_ref, g_ref, w_ref, hn_ref, z_ref):
        hn = _rms_fwd(h_ref[...], g_ref[...]).astype(BF16)
        hn_ref[...] = hn
        z_ref[...] = jnp.dot(hn, w_ref[...], preferred_element_type=F32)

    return _hosting_call(
        exchange, body, name="norm_in_proj", grid=(n // tr,),
        in_specs=[pl.BlockSpec((tr, d), lambda i: (i, 0)), g.spec, VMEM_SPEC],
        out_specs=[pl.BlockSpec((tr, d), lambda i: (i, 0)), pl.BlockSpec((tr, zw), lambda i: (i, 0))],
        out_shape=[_sds((n, d), BF16), _sds((n, zw), F32)],
        scratch_shapes=[], compiler_params=_params(("arbitrary",), 48),
    )(h, g.array, w)


def _halo_specs(width, nb, col=0):
    per = ROW_BLOCK // 8
    prev = pl.BlockSpec((8, width), lambda i: (jnp.maximum(i * per - 1, 0), col))
    nxt = pl.BlockSpec((8, width), lambda i: (jnp.minimum((i + 1) * per, nb * per - 1), col))
    return prev, nxt


def _shift_down(x, prev8, d):
    n = x.shape[0]
    r = pltpu.roll(x, d, 0)
    p = pltpu.roll(prev8, d, 0)
    row8 = lax.broadcasted_iota(jnp.int32, (8, 1), 0)
    head = jnp.where(row8 < d, p, r[0:8])
    return jnp.concatenate([head, r[8:]], axis=0)


def _shift_up(x, next8, d):
    n = x.shape[0]
    r = pltpu.roll(x, n - d, 0)
    q = pltpu.roll(next8, 8 - d, 0)
    row8 = lax.broadcasted_iota(jnp.int32, (8, 1), 0)
    tail = jnp.where(row8 >= 8 - d, q, r[n - 8:])
    return jnp.concatenate([r[:n - 8], tail], axis=0)


def _conv_fwd(z, conv_w, conv_b):
    n = z.shape[0]
    nb = n // ROW_BLOCK
    prev_spec, next_spec = _halo_specs(LRU_W, nb)

    def body(cur_ref, prev_ref, next_ref, w_ref, b_ref, xc_ref):
        i = pl.program_id(0)
        cur = cur_ref[...]
        prev8 = prev_ref[...] * jnp.where(i > 0, 1.0, 0.0)
        next8 = next_ref[...] * jnp.where(i < nb - 1, 1.0, 0.0)
        w = [w_ref[pl.ds(k, 1), :] for k in range(4)]
        xc = (w[0] * _shift_down(cur, prev8, 2) + w[1] * _shift_down(cur, prev8, 1)
              + w[2] * cur + w[3] * _shift_up(cur, next8, 1) + b_ref[...])
        xc_ref[...] = xc

    return pl.pallas_call(
        body, name="conv_fwd", grid=(nb,),
        in_specs=[pl.BlockSpec((ROW_BLOCK, LRU_W), lambda i: (i, 0)), prev_spec, next_spec, conv_w.spec, conv_b.spec],
        out_specs=pl.BlockSpec((ROW_BLOCK, LRU_W), lambda i: (i, 0)),
        out_shape=_sds((n, LRU_W), F32),
        compiler_params=_params(("parallel",)),
    )(z, z, z, conv_w.array, conv_b.array)


def _conv_bwd(dxc_f, dxc_b, z, conv_w):
    n = z.shape[0]
    nb = n // ROW_BLOCK
    prev_spec, next_spec = _halo_specs(LRU_W, nb)
    row_spec = pl.BlockSpec((ROW_BLOCK, LRU_W), lambda i: (i, 0))

    def body(df_ref, dfp_ref, dfn_ref, db_ref, dbp_ref, dbn_ref, x_ref, xp_ref, xn_ref, w_ref,
             dx_ref, dw_ref, dbias_ref):
        i = pl.program_id(0)
        has_prev = jnp.where(i > 0, 1.0, 0.0)
        has_next = jnp.where(i < nb - 1, 1.0, 0.0)
        dxc = df_ref[...] + db_ref[...]
        dprev = (dfp_ref[...] + dbp_ref[...]) * has_prev
        dnext = (dfn_ref[...] + dbn_ref[...]) * has_next
        x = x_ref[...]
        xprev = xp_ref[...] * has_prev
        xnext = xn_ref[...] * has_next
        w = [w_ref[pl.ds(k, 1), :] for k in range(4)]
        dx_ref[...] = (w[0] * _shift_up(dxc, dnext, 2) + w[1] * _shift_up(dxc, dnext, 1)
                       + w[2] * dxc + w[3] * _shift_down(dxc, dprev, 1)).astype(BF16)
        dw = jnp.concatenate([
            jnp.sum(dxc * _shift_down(x, xprev, 2), axis=0, keepdims=True),
            jnp.sum(dxc * _shift_down(x, xprev, 1), axis=0, keepdims=True),
            jnp.sum(dxc * x, axis=0, keepdims=True),
            jnp.sum(dxc * _shift_up(x, xnext, 1), axis=0, keepdims=True),
            jnp.zeros((4, LRU_W), F32)], axis=0)
        _accumulate(dw_ref, dw, i == 0)
        _accumulate(dbias_ref, jnp.sum(dxc, axis=0, keepdims=True), i == 0)

    dx, dw, dbias = pl.pallas_call(
        body, name="conv_bwd", grid=(nb,),
        in_specs=[row_spec, prev_spec, next_spec, row_spec, prev_spec, next_spec, row_spec, prev_spec, next_spec,
                  conv_w.spec],
        out_specs=[row_spec, pl.BlockSpec((8, LRU_W), lambda i: (0, 0)), pl.BlockSpec((1, LRU_W), lambda i: (0, 0))],
        out_shape=[_sds((n, LRU_W), BF16), _sds((8, LRU_W), F32), _sds((1, LRU_W), F32)],
        compiler_params=_params(("arbitrary",)),
    )(dxc_f, dxc_f, dxc_f, dxc_b, dxc_b, dxc_b, z, z, z, conv_w.array)
    return dx, dw, dbias


def _mix_epilogue(h_f, h_b, o_f, o_b, z, head_norm):
    n = z.shape[0]
    tr = ROW_BLOCK
    spec = pl.BlockSpec((tr, 512), lambda i: (i, 0))

    def body(hf_ref, hb_ref, of_ref, ob_ref, gate_ref, gout_ref, w_ref, y_ref):
        gelu, _ = _gelu_and_grad(gate_ref[...])
        y_ref[:, 0:LRU_W] = ((hf_ref[...] + hb_ref[...]) * gelu).astype(BF16)
        o = of_ref[...] + ob_ref[...]
        gout = gout_ref[...]
        silu = gout * _sigmoid(gout)
        w = w_ref[...]
        for hd in range(GLA_HEADS):
            cs = slice(hd * GLA_DV, (hd + 1) * GLA_DV)
            oh = o[:, cs]
            on = oh * lax.rsqrt(jnp.mean(oh * oh, axis=-1, keepdims=True) + EPS)
            y_ref[:, LRU_W + hd * GLA_DV:LRU_W + (hd + 1) * GLA_DV] = (on * w[:, cs] * silu[:, cs]).astype(BF16)

    return pl.pallas_call(
        body, name="mix_epilogue", grid=(n // tr,),
        in_specs=[spec, spec, spec, spec, pl.BlockSpec((tr, 512), lambda i: (i, 1)),
                  pl.BlockSpec((tr, 512), lambda i: (i, 4)), head_norm.spec],
        out_specs=pl.BlockSpec((tr, D_MODEL), lambda i: (i, 0)),
        out_shape=_sds((n, D_MODEL), BF16),
        compiler_params=_params(("parallel",)),
    )(h_f, h_b, o_f, o_b, z, z, head_norm.array)


def _mix_epilogue_bwd(dymix, h_f, h_b, o_f, o_b, z, head_norm):
    n = z.shape[0]
    tr = ROW_BLOCK
    spec = pl.BlockSpec((tr, 512), lambda i: (i, 0))

    def body(dyl_ref, dyg_ref, hf_ref, hb_ref, of_ref, ob_ref, gate_ref, gout_ref, w_ref,
             dhs_ref, dgate_ref, do_ref, dgout_ref, dw_ref):
        i = pl.program_id(0)
        dyl = dyl_ref[...]
        gelu, dgelu = _gelu_and_grad(gate_ref[...])
        dhs_ref[...] = dyl * gelu
        dgate_ref[...] = (dyl * (hf_ref[...] + hb_ref[...]) * dgelu).astype(BF16)
        dyg = dyg_ref[...]
        o = of_ref[...] + ob_ref[...]
        gout = gout_ref[...]
        sg = _sigmoid(gout)
        silu = gout * sg
        dsilu = sg * (1.0 + gout * (1.0 - sg))
        w = w_ref[...]
        dws = []
        for hd in range(GLA_HEADS):
            cs = slice(hd * GLA_DV, (hd + 1) * GLA_DV)
            oh = o[:, cs]
            rs = lax.rsqrt(jnp.mean(oh * oh, axis=-1, keepdims=True) + EPS)
            on = oh * rs
            dy = dyg[:, cs]
            dgout_ref[:, cs] = (dy * on * w[:, cs] * dsilu[:, cs]).astype(BF16)
            dys = dy * silu[:, cs]
            dws.append(jnp.sum(dys * on, axis=0, keepdims=True))
            don = dys * w[:, cs]
            do_ref[:, cs] = (rs * (don - on * jnp.mean(don * on, axis=-1, keepdims=True))).astype(BF16)
        _accumulate(dw_ref, jnp.concatenate(dws, axis=1), i == 0)

    return pl.pallas_call(
        body, name="mix_epilogue_bwd", grid=(n // tr,),
        in_specs=[pl.BlockSpec((tr, 512), lambda i: (i, 0)), pl.BlockSpec((tr, 512), lambda i: (i, 1)),
                  spec, spec, spec, spec, pl.BlockSpec((tr, 512), lambda i: (i, 1)),
                  pl.BlockSpec((tr, 512), lambda i: (i, 4)), head_norm.spec],
        out_specs=[spec, spec, spec, spec, pl.BlockSpec((1, GLA_W), lambda i: (0, 0))],
        out_shape=[_sds((n, 512), F32)] + [_sds((n, 512), BF16)] * 3 + [_sds((1, GLA_W), F32)],
        compiler_params=_params(("arbitrary",)),
    )(dymix, dymix, h_f, h_b, o_f, o_b, z, z, head_norm.array)


def _out_proj(ymix, w_out, h, g, exchange=None):
    n, d = h.shape
    tr = _row_tile(n)
    spec = pl.BlockSpec((tr, d), lambda i: (i, 0))

    def body(y_ref, w_ref, h_ref, g_ref, mix_ref, hmid_ref):
        mix = jnp.dot(y_ref[...], w_ref[...], preferred_element_type=F32)
        mix_ref[...] = mix
        hmid_ref[...] = h_ref[...] + _rms_fwd(mix, g_ref[...])

    return _hosting_call(
        exchange, body, name="out_proj", grid=(n // tr,),
        in_specs=[spec, VMEM_SPEC, spec, g.spec],
        out_specs=[spec, spec],
        out_shape=[_sds((n, d), F32), _sds((n, d), F32)],
        scratch_shapes=[], compiler_params=_params(("arbitrary",), 44),
    )(ymix, w_out, h, g.array)


def _out_proj_bwd(dh_mid, mix, g, w_out, exchange=None):
    n, d = mix.shape
    tr = _row_tile(n)
    spec = pl.BlockSpec((tr, d), lambda i: (i, 0))

    def body(dh_ref, mix_ref, g_ref, w_ref, dmix_ref, dy_ref, dg_ref):
        i = pl.program_id(0)
        dmix, dg = _rms_bwd(mix_ref[...], g_ref[...], dh_ref[...])
        dmix = dmix.astype(BF16)
        dmix_ref[...] = dmix
        dy_ref[...] = _dot_nt(dmix, w_ref[...])
        _accumulate(dg_ref, dg, i == 0)

    return _hosting_call(
        exchange, body, name="out_proj_bwd", grid=(n // tr,),
        in_specs=[spec, spec, g.spec, VMEM_SPEC],
        out_specs=[spec, spec, pl.BlockSpec((1, d), lambda i: (0, 0))],
        out_shape=[_sds((n, d), BF16), _sds((n, d), F32), _sds((1, d), F32)],
        scratch_shapes=[], compiler_params=_params(("arbitrary",), 44),
    )(dh_mid, mix, g.array, w_out)


FF_SLAB = D_FF // N_DEV


def _relu_squared(up):
    return jnp.square(jnp.maximum(up.astype(F32), 0.0)).astype(BF16)


def _mlp_fwd(h_mid, g_pre, w_up, w_down, g_post, exchange=None):
    n, d = h_mid.shape
    tr = _row_tile(n, 384)
    spec = pl.BlockSpec((tr, d), lambda i: (i, 0))

    def body(h_ref, gpre_ref, wup_ref, wdn_ref, gpost_ref, hn_ref, up_ref, ff_ref, hout_ref):
        h = h_ref[...]
        hn = _rms_fwd(h, gpre_ref[...]).astype(BF16)
        hn_ref[...] = hn
        ff = jnp.zeros((tr, d), F32)
        for j in range(N_DEV):
            cs = slice(j * FF_SLAB, (j + 1) * FF_SLAB)
            up = jnp.dot(hn, wup_ref[j], preferred_element_type=F32).astype(BF16)
            up_ref[:, cs] = up
            ff = ff + jnp.dot(_relu_squared(up), wdn_ref[cs, :], preferred_element_type=F32)
        ff_ref[...] = ff
        hout_ref[...] = h + _rms_fwd(ff, gpost_ref[...])

    return _hosting_call(
        exchange, body, name="mlp_fwd", grid=(n // tr,),
        in_specs=[spec, g_pre.spec, VMEM_SPEC, VMEM_SPEC, g_post.spec],
        out_specs=[spec, pl.BlockSpec((tr, D_FF), lambda i: (i, 0)), spec, spec],
        out_shape=[_sds((n, d), BF16), _sds((n, D_FF), BF16), _sds((n, d), F32), _sds((n, d), F32)],
        scratch_shapes=[], compiler_params=_params(("arbitrary",), 52),
    )(h_mid, g_pre.array, w_up, w_down, g_post.array)


def _mlp_bwd(dh, ff, up, h_mid, g_pre, w_up, w_down, g_post, exchange=None):
    n, d = h_mid.shape
    tr = _row_tile(n, 384)
    spec = pl.BlockSpec((tr, d), lambda i: (i, 0))
    wide = pl.BlockSpec((tr, D_FF), lambda i: (i, 0))
    gspec = pl.BlockSpec((1, d), lambda i: (0, 0))

    def body(dh_ref, ff_ref, up_ref, h_ref, gpre_ref, wup_ref, wdn_ref, gpost_ref,
             dff_ref, dup_ref, dhmid_ref, dgpost_ref, dgpre_ref):
        i = pl.program_id(0)
        dh = dh_ref[...]
        dff, dgpost = _rms_bwd(ff_ref[...], gpost_ref[...], dh)
        dff = dff.astype(BF16)
        dff_ref[...] = dff
        dhn = jnp.zeros((tr, d), F32)
        for j in range(N_DEV):
            cs = slice(j * FF_SLAB, (j + 1) * FF_SLAB)
            relu = jnp.maximum(up_ref[:, cs].astype(F32), 0.0)
            dact = _dot_nt(dff, wdn_ref[cs, :])
            dup = (dact * 2.0 * relu).astype(BF16)
            dup_ref[:, cs] = dup
            dhn = dhn + _dot_nt(dup, wup_ref[j])
        dx, dgpre = _rms_bwd(h_ref[...], gpre_ref[...], dhn)
        dhmid_ref[...] = dh + dx
        _accumulate(dgpost_ref, dgpost, i == 0)
        _accumulate(dgpre_ref, dgpre, i == 0)

    return _hosting_call(
        exchange, body, name="mlp_bwd", grid=(n // tr,),
        in_specs=[spec, spec, wide, spec, g_pre.spec, VMEM_SPEC, VMEM_SPEC, g_post.spec],
        out_specs=[spec, wide, spec, gspec, gspec],
        out_shape=[_sds((n, d), BF16), _sds((n, D_FF), BF16), _sds((n, d), F32), _sds((1, d), F32), _sds((1, d), F32)],
        scratch_shapes=[], compiler_params=_params(("arbitrary",), 56),
    )(dh, ff, up, h_mid, g_pre.array, w_up, w_down, g_post.array)


def _in_proj_bwd(pieces, w_in, h, g, dh_mid):
    dxbr, dgate, dqk_f, dqk_b, dv_f, dv_b, dgout, dzg_f, dzg_b = pieces
    n, d = h.shape
    tr = _row_tile(n, 384)
    spec = pl.BlockSpec((tr, d), lambda i: (i, 0))
    s512 = pl.BlockSpec((tr, 512), lambda i: (i, 0))
    s128 = pl.BlockSpec((tr, LANES), lambda i: (i, 0))

    def body(a_ref, b_ref, cf_ref, cb_ref, df_ref, db_ref, e_ref, ff_ref, fb_ref, w_ref, h_ref, g_ref, dhm_ref,
             dz_ref, dh_ref, dg_ref):
        i = pl.program_id(0)
        real = (_row_ids(tr, i) >= PAD_ROWS).astype(F32)
        f32 = lambda ref: ref[...].astype(F32)
        dz = jnp.concatenate([f32(a_ref), f32(b_ref), f32(cf_ref) + f32(cb_ref), f32(df_ref) + f32(db_ref),
                              f32(e_ref), f32(ff_ref) + f32(fb_ref)], axis=1) * real
        dz = dz.astype(BF16)
        dz_ref[...] = dz
        dhn = _dot_nt(dz, w_ref[...])
        dx, dg = _rms_bwd(h_ref[...], g_ref[...], dhn)
        dh_ref[...] = (dhm_ref[...] + dx) * real
        _accumulate(dg_ref, dg, i == 0)

    return pl.pallas_call(
        body, name="in_proj_bwd", grid=(n // tr,),
        in_specs=[s512, s512, s512, s512, s512, s512, s512, s128, s128, VMEM_SPEC, spec, g.spec, spec],
        out_specs=[pl.BlockSpec((tr, Z_W), lambda i: (i, 0)), spec, pl.BlockSpec((1, d), lambda i: (0, 0))],
        out_shape=[_sds((n, Z_W), BF16), _sds((n, d), F32), _sds((1, d), F32)],
        compiler_params=_params(("arbitrary",), 48),
    )(dxbr, dgate, dqk_f, dqk_b, dv_f, dv_b, dgout, dzg_f, dzg_b, w_in, h, g.array, dh_mid)


def _matmul_tn(a, b, name, column_slabs=False, exchange=None, a_map=None):
    n, m = a.shape
    k = b.shape[1]
    tr, tm, tk = _row_tile(n), _col_tile(m), _col_tile(k)
    steps = n // tr
    slab = k // N_DEV
    per_step = tk // slab if column_slabs else 1

    def body(a_ref, b_ref, o_ref, acc_ref):
        r = pl.program_id(2)
        a_blk = a_ref[...] if a_map is None else a_map(a_ref[...])
        _accumulate(acc_ref, _dot_tn(a_blk, b_ref[...]), r == 0)

        @pl.when(r == steps - 1)
        def _():
            if column_slabs:
                for j in range(per_step):
                    o_ref[j] = acc_ref[:, j * slab:(j + 1) * slab].astype(BF16)
            else:
                o_ref[...] = acc_ref[...].astype(BF16)

    if column_slabs:
        out_spec = pl.BlockSpec((per_step, tm, slab), lambda mi, ki, r: (ki, mi, 0))
        out_shape = _sds((N_DEV, m, slab), BF16)
    else:
        out_spec = pl.BlockSpec((tm, tk), lambda mi, ki, r: (mi, ki))
        out_shape = _sds((m, k), BF16)
    outs = _hosting_call(
        exchange, body, name=name, grid=(m // tm, k // tk, steps),
        in_specs=[pl.BlockSpec((tr, tm), lambda mi, ki, r: (r, mi)), pl.BlockSpec((tr, tk), lambda mi, ki, r: (r, ki))],
        out_specs=[out_spec], out_shape=[out_shape], scratch_shapes=[pltpu.VMEM((tm, tk), F32)],
        compiler_params=_params(("arbitrary", "arbitrary", "arbitrary"), 40),
    )(a, b)
    return outs[0] if exchange is None else outs


def _loss_and_grad(h_out, target):
    n, d = h_out.shape
    tr = ROW_BLOCK
    first = (PAD_ROWS + N_META) // tr

    def body(h_ref, t_ref, dh_ref, loss_ref):
        i = pl.program_id(0)
        real = jnp.where(i >= first, 1.0, 0.0)
        diff = (h_ref[...] - t_ref[...]) * real
        dh_ref[...] = diff * (1.0 / d)
        part = 0.5 * jnp.sum(jnp.mean(diff * diff, axis=-1, keepdims=True), axis=0, keepdims=True)
        _accumulate(loss_ref, jnp.broadcast_to(part, (1, LANES)), i == 0)

    return pl.pallas_call(
        body, name="loss_and_grad", grid=(n // tr,),
        in_specs=[pl.BlockSpec((tr, d), lambda i: (i, 0)), pl.BlockSpec((tr, d), lambda i: (jnp.maximum(i - first, 0), 0))],
        out_specs=[pl.BlockSpec((tr, d), lambda i: (i, 0)), pl.BlockSpec((1, LANES), lambda i: (0, 0))],
        out_shape=[_sds((n, d), F32), _sds((1, LANES), F32)],
        compiler_params=_params(("arbitrary",)),
    )(h_out, target)


SUBLANES = 8


def _scan_rows(a, u, reverse, window=None):
    n = a.shape[0]
    window = window or n
    pos = lax.broadcasted_iota(jnp.int32, (n, 1), 0) & (window - 1) if window < n else lax.broadcasted_iota(jnp.int32, (n, 1), 0)
    d = 1
    while d < window:
        shift = n - d if reverse else d
        keep = (pos < window - d) if reverse else (pos >= d)
        a_s = pltpu.roll(a, shift, 0)
        u_s = pltpu.roll(u, shift, 0)
        u = jnp.where(keep, a * u_s + u, u)
        a = jnp.where(keep, a * a_s, a)
        d *= 2
    return a, u


def _scan_block(a, u, h_in, reverse, stage_ref):
    n, width = a.shape
    groups = n // SUBLANES
    lanes = [slice(cb * LANES, (cb + 1) * LANES) for cb in range(width // LANES)]
    a1, u1 = _scan_rows(a, u, reverse, window=SUBLANES)
    for cb, cs in enumerate(lanes):
        stage_ref[0, cb] = a1[:, cs]
        stage_ref[1, cb] = u1[:, cs]
    edge = 0 if reverse else SUBLANES - 1
    group_rows = pl.ds(edge, groups, stride=SUBLANES)
    a2, u2 = _scan_rows(jnp.concatenate([stage_ref[0, cb, group_rows, :] for cb in range(len(lanes))], axis=1),
                        jnp.concatenate([stage_ref[1, cb, group_rows, :] for cb in range(len(lanes))], axis=1), reverse)
    leaving = a2 * h_in + u2
    grow = lax.broadcasted_iota(jnp.int32, (groups, 1), 0)
    if reverse:
        entering = jnp.where(grow == groups - 1, h_in, pltpu.roll(leaving, groups - 1, 0))
    else:
        entering = jnp.where(grow == 0, h_in, pltpu.roll(leaving, 1, 0))
    for cb, cs in enumerate(lanes):
        for k in range(SUBLANES):
            stage_ref[0, cb, pl.ds(k, groups, stride=SUBLANES), :] = entering[:, cs]
    entering_rows = jnp.concatenate([stage_ref[0, cb] for cb in range(len(lanes))], axis=1)
    return a1 * entering_rows + u1


def _lru_gates(xc, wcat_ref, bias_ref, lam_ref):
    nl = -lam_ref[...]
    nsp = -LRU_C * (jnp.maximum(nl, 0.0) + jnp.log(1.0 + jnp.exp(-jnp.abs(nl))))
    pre = _dot(xc, wcat_ref[...]) + bias_ref[...]
    r = _sigmoid(pre[:, :LRU_W])
    ig = _sigmoid(pre[:, LRU_W:])
    log_a = r * nsp
    a = jnp.exp(log_a)
    m2 = _one_minus_square(a, log_a)
    inv_m = lax.rsqrt(jnp.maximum(m2, 1e-30))
    return r, ig, a, m2 * inv_m, inv_m, nsp


def _lru_scan(xc, wcat, bias, lam, reverse, exchange=None):
    n = xc.shape[0]
    nb = n // ROW_BLOCK
    order = (lambda i: nb - 1 - i) if reverse else (lambda i: i)
    spec = pl.BlockSpec((ROW_BLOCK, LRU_W), lambda i: (order(i), 0))
    edge = 0 if reverse else ROW_BLOCK - 1

    def body(xc_ref, wcat_ref, bias_ref, lam_ref, h_ref, carry_ref, stage_ref):
        i = pl.program_id(0)

        @pl.when(i == 0)
        def _():
            carry_ref[...] = jnp.zeros_like(carry_ref)

        xc = xc_ref[...]
        r, ig, a, m, _, _ = _lru_gates(xc, wcat_ref, bias_ref, lam_ref)
        u = jnp.where(_row_ids(ROW_BLOCK, order(i)) >= PAD_ROWS, m * (ig * xc), 0.0)
        h_ref[...] = _scan_block(a, u, carry_ref[0:1, :], reverse, stage_ref)
        carry_ref[0:1, :] = h_ref[pl.ds(edge, 1), :]

    return _hosting_call(
        exchange, body, name="lru_scan_b" if reverse else "lru_scan_f", grid=(nb,),
        in_specs=[spec, wcat.spec, bias.spec, lam.spec],
        out_specs=[spec],
        out_shape=[_sds((n, LRU_W), F32)],
        scratch_shapes=[pltpu.VMEM((8, LRU_W), F32), pltpu.VMEM((2, LRU_W // LANES, ROW_BLOCK, LANES), F32)],
        compiler_params=_params(("arbitrary",)),
    )(xc, wcat.array, bias.array, lam.array)


def _lru_scan_bwd(dhs, xc, h, wcat, bias, lam, reverse, exchange=None):
    n = xc.shape[0]
    nb = n // ROW_BLOCK
    per = ROW_BLOCK // 8
    order = (lambda i: i) if reverse else (lambda i: nb - 1 - i)
    spec = pl.BlockSpec((ROW_BLOCK, LRU_W), lambda i: (order(i), 0))
    if reverse:
        halo = pl.BlockSpec((8, LRU_W), lambda i: (jnp.minimum((order(i) + 1) * per, nb * per - 1), 0))
    else:
        halo = pl.BlockSpec((8, LRU_W), lambda i: (jnp.maximum(order(i) * per - 1, 0), 0))
    edge = ROW_BLOCK - 1 if reverse else 0

    def body(dhs_ref, xc_ref, h_ref, halo_ref, wcat_ref, bias_ref, lam_ref,
             dxc_ref, dw_ref, db_ref, dlam_ref, cdh_ref, ca_ref, tmp_ref, stage_ref):
        i = pl.program_id(0)
        ib = order(i)

        @pl.when(i == 0)
        def _():
            cdh_ref[...] = jnp.zeros_like(cdh_ref)
            ca_ref[...] = jnp.zeros_like(ca_ref)

        xc = xc_ref[...]
        r, ig, a, m, inv_m, nsp = _lru_gates(xc, wcat_ref, bias_ref, lam_ref)
        row = lax.broadcasted_iota(jnp.int32, (ROW_BLOCK, 1), 0)
        if reverse:
            coef = jnp.where(row == 0, ca_ref[0:1, :], pltpu.roll(a, 1, 0))
            h_nb = jnp.where(row == ROW_BLOCK - 1, halo_ref[0:1, :] * jnp.where(ib < nb - 1, 1.0, 0.0),
                             pltpu.roll(h_ref[...], ROW_BLOCK - 1, 0))
        else:
            coef = jnp.where(row == ROW_BLOCK - 1, ca_ref[0:1, :], pltpu.roll(a, ROW_BLOCK - 1, 0))
            h_nb = jnp.where(row == 0, halo_ref[7:8, :] * jnp.where(ib > 0, 1.0, 0.0), pltpu.roll(h_ref[...], 1, 0))
        dh = _scan_block(coef, dhs_ref[...], cdh_ref[0:1, :], not reverse, stage_ref)
        tmp_ref[...] = dh
        cdh_ref[0:1, :] = tmp_ref[pl.ds(edge, 1), :]
        tmp_ref[...] = a
        ca_ref[0:1, :] = tmp_ref[pl.ds(edge, 1), :]

        du = jnp.where(_row_ids(ROW_BLOCK, ib) >= PAD_ROWS, dh, 0.0)
        da = dh * h_nb
        dm = du * (ig * xc)
        di = du * (m * xc)
        dlog_a = da * a - dm * (a * a) * inv_m
        dr = dlog_a * nsp
        dpre = jnp.concatenate([dr * r * (1.0 - r), di * ig * (1.0 - ig)], axis=1)
        dxc_ref[...] = du * (m * ig) + _dot_nt(dpre, wcat_ref[...])
        _accumulate(dw_ref, _dot_tn(xc, dpre), i == 0)
        _accumulate(db_ref, jnp.sum(dpre, axis=0, keepdims=True), i == 0)
        _accumulate(dlam_ref, jnp.sum(dlog_a * r, axis=0, keepdims=True), i == 0)

        @pl.when(i == nb - 1)
        def _():
            dlam_ref[...] = dlam_ref[...] * (LRU_C * _sigmoid(-lam_ref[...]))

    return _hosting_call(
        exchange, body, name="lru_scan_bwd_b" if reverse else "lru_scan_bwd_f", grid=(nb,),
        in_specs=[spec, spec, spec, halo, wcat.spec, bias.spec, lam.spec],
        out_specs=[spec, pl.BlockSpec((LRU_W, 2 * LRU_W), lambda i: (0, 0)),
                   pl.BlockSpec((1, 2 * LRU_W), lambda i: (0, 0)), pl.BlockSpec((1, LRU_W), lambda i: (0, 0))],
        out_shape=[_sds((n, LRU_W), F32), _sds((LRU_W, 2 * LRU_W), F32), _sds((1, 2 * LRU_W), F32), _sds((1, LRU_W), F32)],
        scratch_shapes=[pltpu.VMEM((8, LRU_W), F32), pltpu.VMEM((8, LRU_W), F32), pltpu.VMEM((ROW_BLOCK, LRU_W), F32),
                        pltpu.VMEM((2, LRU_W // LANES, ROW_BLOCK, LANES), F32)],
        compiler_params=_params(("arbitrary",)),
    )(dhs, xc, h, h, wcat.array, bias.array, lam.array)


def _gla_rows(n):
    return 768 if n % 768 == 0 else ROW_BLOCK


def _gla_masks(reverse):
    t = lax.broadcasted_iota(jnp.int32, (CHUNK, CHUNK), 0)
    s = lax.broadcasted_iota(jnp.int32, (CHUNK, CHUNK), 1)
    if reverse:
        return (s >= t).astype(F32), s > t
    return (s <= t).astype(F32), s <= t


def _gla_gate(zg, wg_ref, bg_ref):
    pre = _dot(zg, wg_ref[...]) + bg_ref[...]
    g = (jnp.minimum(pre, 0.0) - jnp.log(1.0 + jnp.exp(-jnp.abs(pre)))) * (1.0 / GATE_NORM)
    return pre, g


def _gla_decays(gc, tri):
    b = jnp.dot(tri, gc, precision=lax.Precision.HIGHEST, preferred_element_type=F32)
    b_last = jnp.sum(gc, axis=0, keepdims=True)
    return jnp.exp(b), jnp.exp(-b), jnp.exp(b_last - b), jnp.exp(b_last)


def _gla_scan(z, wg, bg, reverse, exchange=None):
    n = z.shape[0]
    rb = _gla_rows(n)
    nb = n // rb
    cpb = rb // CHUNK
    order = (lambda i: nb - 1 - i) if reverse else (lambda i: i)
    chunks = range(cpb - 1, -1, -1) if reverse else range(cpb)

    def body(qk_ref, v_ref, zg_ref, wg_ref, bg_ref, o_ref, sall_ref, s_ref):
        i = pl.program_id(0)

        @pl.when(i == 0)
        def _():
            s_ref[...] = jnp.zeros_like(s_ref)

        tri, mask = _gla_masks(reverse)
        _, g = _gla_gate(zg_ref[...], wg_ref, bg_ref)
        heads = range(GLA_HEADS)
        ks = [slice(hd * GLA_DK, (hd + 1) * GLA_DK) for hd in heads]
        vs = [slice(hd * GLA_DV, (hd + 1) * GLA_DV) for hd in heads]
        qh, kb, v, el, p, intra, kv = {}, {}, {}, {}, {}, {}, {}
        for c in chunks:
            rows = slice(c * CHUNK, (c + 1) * CHUNK)
            eb, enb, ebl, el[c] = _gla_decays(g[rows], tri)
            qk = qk_ref[rows, :]
            q_all = (qk[:, :GLA_QK] * (GLA_DK ** -0.5) * eb).astype(BF16)
            k_all = (qk[:, GLA_QK:] * enb).astype(BF16)
            kb_all = (qk[:, GLA_QK:] * ebl).astype(BF16)
            v_all = v_ref[rows, :].astype(BF16)
            for hd in heads:
                qh[c, hd], kb[c, hd], v[c, hd] = q_all[:, ks[hd]], kb_all[:, ks[hd]], v_all[:, vs[hd]]
                p[c, hd] = _dot_nt(qh[c, hd], k_all[:, ks[hd]])
        for c in chunks:
            for hd in heads:
                intra[c, hd] = _dot(jnp.where(mask, p[c, hd], 0.0), v[c, hd])
                kv[c, hd] = _dot_tn(v[c, hd], kb[c, hd])
        state = [s_ref[:, ks[hd]] for hd in heads]
        for c in chunks:
            rows = slice(c * CHUNK, (c + 1) * CHUNK)
            for hd in heads:
                sall_ref[c, :, ks[hd]] = state[hd]
                o_ref[rows, vs[hd]] = intra[c, hd] + _dot_nt(qh[c, hd], state[hd])
                state[hd] = state[hd] * el[c][:, ks[hd]] + kv[c, hd]
        for hd in heads:
            s_ref[:, ks[hd]] = state[hd]

    return _hosting_call(
        exchange, body, name="gla_scan_b" if reverse else "gla_scan_f", grid=(nb,),
        in_specs=[pl.BlockSpec((rb, 512), lambda i: (order(i), 2)), pl.BlockSpec((rb, 512), lambda i: (order(i), 3)),
                  pl.BlockSpec((rb, LANES), lambda i: (order(i), ZG_COL_BLOCK)), wg.spec, bg.spec],
        out_specs=[pl.BlockSpec((rb, GLA_W), lambda i: (order(i), 0)),
                   pl.BlockSpec((cpb, GLA_DV, GLA_QK), lambda i: (order(i), 0, 0))],
        out_shape=[_sds((n, GLA_W), F32), _sds((n // CHUNK, GLA_DV, GLA_QK), F32)],
        scratch_shapes=[pltpu.VMEM((GLA_DV, GLA_QK), F32)],
        compiler_params=_params(("arbitrary",)),
    )(z, z, z, wg.array, bg.array)


def _gla_scan_bwd(do, z, states, wg, bg, reverse, exchange=None):
    n = z.shape[0]
    rb = _gla_rows(n)
    nb = n // rb
    cpb = rb // CHUNK
    order = (lambda i: i) if reverse else (lambda i: nb - 1 - i)
    chunks = range(cpb) if reverse else range(cpb - 1, -1, -1)

    def body(do_ref, qk_ref, v_ref, zg_ref, sall_ref, wg_ref, bg_ref,
             dqk_ref, dv_ref, dzg_ref, dwg_ref, dbg_ref, ds_ref):
        i = pl.program_id(0)

        @pl.when(i == 0)
        def _():
            ds_ref[...] = jnp.zeros_like(ds_ref)

        tri, mask = _gla_masks(reverse)
        tri_t, _ = _gla_masks(not reverse)
        zg = zg_ref[...]
        pre, g = _gla_gate(zg, wg_ref, bg_ref)
        heads = range(GLA_HEADS)
        ks = [slice(hd * GLA_DK, (hd + 1) * GLA_DK) for hd in heads]
        vs = [slice(hd * GLA_DV, (hd + 1) * GLA_DV) for hd in heads]
        dec, full, qh, kh, kb, v, dout, p, dp = {}, {}, {}, {}, {}, {}, {}, {}, {}
        for c in chunks:
            rows = slice(c * CHUNK, (c + 1) * CHUNK)
            dec[c] = _gla_decays(g[rows], tri)
            eb, enb, ebl, _ = dec[c]
            qk = qk_ref[rows, :]
            q_f = qk[:, :GLA_QK] * (GLA_DK ** -0.5) * eb
            k_f = qk[:, GLA_QK:] * enb
            kb_f = qk[:, GLA_QK:] * ebl
            full[c] = (q_f, k_f, kb_f)
            q_all, k_all, kb_all = q_f.astype(BF16), k_f.astype(BF16), kb_f.astype(BF16)
            v_all, do_all = v_ref[rows, :].astype(BF16), do_ref[rows, :].astype(BF16)
            for hd in heads:
                qh[c, hd], kh[c, hd], kb[c, hd] = q_all[:, ks[hd]], k_all[:, ks[hd]], kb_all[:, ks[hd]]
                v[c, hd], dout[c, hd] = v_all[:, vs[hd]], do_all[:, vs[hd]]
                p[c, hd] = _dot_nt(qh[c, hd], kh[c, hd])
                dp[c, hd] = _dot_nt(dout[c, hd], v[c, hd])
        dv_i, dqh, dkh, dsq, state = {}, {}, {}, {}, {}
        for c in chunks:
            for hd in heads:
                pm = jnp.where(mask, p[c, hd], 0.0).astype(BF16)
                dpm = jnp.where(mask, dp[c, hd], 0.0).astype(BF16)
                state[c, hd] = sall_ref[c, :, ks[hd]]
                dv_i[c, hd] = _dot_tn(pm, dout[c, hd])
                dqh[c, hd] = _dot(dpm, kh[c, hd]) + _dot(dout[c, hd], state[c, hd])
                dkh[c, hd] = _dot_tn(dpm, qh[c, hd])
                dsq[c, hd] = _dot_tn(dout[c, hd], qh[c, hd])
        dstate = [ds_ref[:, ks[hd]] for hd in heads]
        dkb, sds = {}, {}
        for c in chunks:
            rows = slice(c * CHUNK, (c + 1) * CHUNK)
            el = dec[c][3]
            for hd in heads:
                dv_ref[rows, vs[hd]] = (dv_i[c, hd] + _dot_nt(kb[c, hd], dstate[hd])).astype(BF16)
                dkb[c, hd] = _dot(v[c, hd], dstate[hd])
                sds[c, hd] = jnp.sum(state[c, hd] * dstate[hd], axis=0, keepdims=True)
                dstate[hd] = dstate[hd] * el[:, ks[hd]] + dsq[c, hd]
        for hd in heads:
            ds_ref[:, ks[hd]] = dstate[hd]
        dgs = [None] * cpb
        for c in chunks:
            rows = slice(c * CHUNK, (c + 1) * CHUNK)
            eb, enb, ebl, el = dec[c]
            q_f, k_f, kb_f = full[c]
            dqh_c = jnp.concatenate([dqh[c, hd] for hd in heads], axis=1)
            dkh_c = jnp.concatenate([dkh[c, hd] for hd in heads], axis=1)
            dkb_c = jnp.concatenate([dkb[c, hd] for hd in heads], axis=1)
            sds_c = jnp.concatenate([sds[c, hd] for hd in heads], axis=1)
            dqk_ref[rows, :] = jnp.concatenate([dqh_c * eb * (GLA_DK ** -0.5), dkh_c * enb + dkb_c * ebl], axis=1).astype(BF16)
            dkb_kb = dkb_c * kb_f
            db = dqh_c * q_f - dkh_c * k_f - dkb_kb
            db_last = el * sds_c + jnp.sum(dkb_kb, axis=0, keepdims=True)
            dgs[c] = jnp.dot(tri_t, db, precision=lax.Precision.HIGHEST, preferred_element_type=F32) + db_last
        dg = jnp.concatenate(dgs, axis=0)
        dpre = dg * _sigmoid(-pre) * (1.0 / GATE_NORM)
        dzg_ref[...] = _dot_nt(dpre, wg_ref[...]).astype(BF16)
        _accumulate(dwg_ref, _dot_tn(zg, dpre), i == 0)
        _accumulate(dbg_ref, jnp.sum(dpre, axis=0, keepdims=True), i == 0)

    return _hosting_call(
        exchange, body, name="gla_scan_bwd_b" if reverse else "gla_scan_bwd_f", grid=(nb,),
        in_specs=[pl.BlockSpec((rb, GLA_W), lambda i: (order(i), 0)),
                  pl.BlockSpec((rb, 512), lambda i: (order(i), 2)), pl.BlockSpec((rb, 512), lambda i: (order(i), 3)),
                  pl.BlockSpec((rb, LANES), lambda i: (order(i), ZG_COL_BLOCK)),
                  pl.BlockSpec((cpb, GLA_DV, GLA_QK), lambda i: (order(i), 0, 0)), wg.spec, bg.spec],
        out_specs=[pl.BlockSpec((rb, 512), lambda i: (order(i), 0)), pl.BlockSpec((rb, 512), lambda i: (order(i), 0)),
                   pl.BlockSpec((rb, LANES), lambda i: (order(i), 0)),
                   pl.BlockSpec((LANES, GLA_QK), lambda i: (0, 0)), pl.BlockSpec((1, GLA_QK), lambda i: (0, 0))],
        out_shape=[_sds((n, 512), BF16), _sds((n, 512), BF16), _sds((n, LANES), BF16), _sds((LANES, GLA_QK), F32),
                   _sds((1, GLA_QK), F32)],
        scratch_shapes=[pltpu.VMEM((GLA_DV, GLA_QK), F32)],
        compiler_params=_params(("arbitrary",)),
    )(do, z, z, z, states, wg.array, bg.array)


NORM_NAMES = ("norm_mix_pre", "norm_mix_post", "norm_mlp_pre", "norm_mlp_post")
VEC512_NAMES = ("conv_b", "lru_ba_f", "lru_bx_f", "lru_lambda_f", "lru_ba_b", "lru_bx_b", "lru_lambda_b", "gla_head_norm")
VEC256_NAMES = ("gla_bg_f", "gla_bg_b")
LRU_MAT_NAMES = ("lru_wa_f", "lru_wx_f", "lru_wa_b", "lru_wx_b")
DIRS = ("f", "b")


def _prepare_params(w, gathered, depth):
    row_names = NORM_NAMES + ("conv_b", "gla_head_norm")
    ins = ([w[nm] for nm in row_names] + [w["lru_ba_" + d] for d in DIRS] + [w["lru_bx_" + d] for d in DIRS]
           + [w["lru_lambda_" + d] for d in DIRS] + [w["gla_bg_" + d] for d in DIRS]
           + [w["lru_wa_" + d].reshape(depth, LRU_W, LRU_HD) for d in DIRS]
           + [w["lru_wx_" + d].reshape(depth, LRU_W, LRU_HD) for d in DIRS]
           + [gathered["conv_w"], gathered["gla_wg_f"], gathered["gla_wg_b"], gathered["meta_tokens"]])
    n_rows = len(row_names)

    def body(*refs):
        rows_in = refs[:n_rows]
        ba, bx, lam, bg, wa, wx = (refs[n_rows + 2 * t:n_rows + 2 * t + 2] for t in range(6))
        convw_g, wgf_g, wgb_g, meta_g = refs[n_rows + 12:n_rows + 16]
        outs = refs[n_rows + 16:]
        rows_out = outs[:n_rows]
        convw, wcat, bias, lam_o, wg, bg_o, meta = outs[n_rows:]
        for l in range(depth):
            for src, dst in zip(rows_in, rows_out):
                dst[l] = src[pl.ds(l, 1), :]
            convw[l] = jnp.zeros((8, LRU_W), F32)
            for j in range(N_DEV):
                convw[l, 0:4, j * 64:(j + 1) * 64] = convw_g[j, l]
            for d in range(2):
                wcat[l, d] = jnp.zeros((LRU_W, 2 * LRU_W), BF16)
                for hd in range(LRU_HEADS):
                    rs = slice(hd * LRU_HD, (hd + 1) * LRU_HD)
                    wcat[l, d, rs, hd * LRU_HD:(hd + 1) * LRU_HD] = wa[d][l, rs, :].astype(BF16)
                    wcat[l, d, rs, LRU_W + hd * LRU_HD:LRU_W + (hd + 1) * LRU_HD] = wx[d][l, rs, :].astype(BF16)
                bias[l, d, :, 0:LRU_W] = ba[d][pl.ds(l, 1), :]
                bias[l, d, :, LRU_W:2 * LRU_W] = bx[d][pl.ds(l, 1), :]
                lam_o[l, d] = lam[d][pl.ds(l, 1), :]
                bg_o[l, d] = bg[d][pl.ds(l, 1), :]
                wg[l, d] = jnp.zeros((LANES, GLA_QK), BF16)
                src = wgf_g if d == 0 else wgb_g
                for j in range(N_DEV):
                    wg[l, d, d * GLA_RANK:(d + 1) * GLA_RANK, j * 32:(j + 1) * 32] = src[j, l].astype(BF16)
        for j in range(N_DEV):
            meta[:, j * LANES:(j + 1) * LANES] = meta_g[j]

    out_shape = ([_sds((depth, 1, w[nm].shape[1]), F32) for nm in row_names]
                 + [_sds((depth, 8, LRU_W), F32), _sds((depth, 2, LRU_W, 2 * LRU_W), BF16), _sds((depth, 2, 1, 2 * LRU_W), F32),
                    _sds((depth, 2, 1, LRU_W), F32), _sds((depth, 2, LANES, GLA_QK), BF16), _sds((depth, 2, 1, GLA_QK), F32),
                    _sds((N_META, D_MODEL), F32)])
    outs = pl.pallas_call(
        body, name="prepare_params", in_specs=[VMEM_SPEC] * len(ins), out_specs=[VMEM_SPEC] * len(out_shape),
        out_shape=out_shape, compiler_params=_params(None, 32),
    )(*ins)
    prepared = dict(zip(row_names, outs[:n_rows]))
    prepared.update(zip(("conv_w", "wcat", "lru_bias", "lru_lam", "wg", "gla_bg", "meta_tokens"), outs[n_rows:]))
    return prepared


class _Outbox:
    def __init__(self, on_complete):
        self.pending, self.on_complete = {}, on_complete

    def put(self, key, array, src, landing_shape):
        self.pending[key] = dict(array=array, src=src, landing=landing_shape, groups=list(range(len(PEER_GROUPS))))

    def exchange(self, wanted=None):
        ex, tickets = _Exchange(), []
        for key, item in self.pending.items():
            groups = [g for g in item["groups"] if wanted is None or (key[0], g) in wanted]
            out = None
            for g in groups:
                landing = item["landing"] if out is None else out
                out = ex.add(item["array"], item["src"], landing, _slab, peers=PEER_GROUPS[g], local=(g == 0))
                item["groups"].remove(g)
            if groups:
                tickets.append((key, out))
        return ex, tickets

    def store(self, tickets, landed):
        for key, out in tickets:
            item = self.pending[key]
            item["landing"] = landed[out]
            if not item["groups"]:
                del self.pending[key]
                self.on_complete(key, landed[out])


def _install_weight(p):
    def install(key, g):
        nm, l = key
        if nm == "w_in":
            g = jnp.pad(jnp.concatenate([g[j] for j in range(N_DEV)], axis=1), ((0, 0), (0, Z_W - D_IN)))
        elif nm == "w_out":
            g = g.reshape(D_MODEL, D_MODEL)
        elif nm == "w_mlp_down":
            g = g.reshape(D_FF, D_MODEL)
        p.setdefault(nm, {})[l] = g
    return install


def _request_weight(gather, shards, nm, l):
    gather.put((nm, l), shards[nm], _layer_of(l), _sds((N_DEV,) + shards[nm].shape[1:], BF16))


def _layer_fwd(h, l, p, gather, shards, depth):
    lp = lambda name, *index: _LayerParam(p[name], l, *index)
    s = dict(h=h)

    def hosted(fn, wanted, *args):
        ex, tickets = gather.exchange(wanted)
        outs = fn(*args, ex)
        own = len(outs) - len(ex.landings)
        gather.store(tickets, outs[own:])
        return outs[:own]

    _request_weight(gather, shards, "w_mlp_up", l)
    _request_weight(gather, shards, "w_mlp_down", l)
    s["hn"], s["z"] = hosted(_norm_in_proj, [("w_mlp_up", 0)], h, lp("norm_mix_pre"), p["w_in"][l])
    s["xc"] = _conv_fwd(s["z"], lp("conv_w"), lp("conv_b"))
    plan = {"f": ([("w_mlp_up", 1)], [("w_mlp_up", 2)]), "b": ([("w_mlp_down", 0)], [("w_mlp_down", 1)])}
    for d, name in enumerate(DIRS):
        s["h_" + name], = hosted(_lru_scan, plan[name][0], s["xc"], lp("wcat", d), lp("lru_bias", d), lp("lru_lam", d), d == 1)
        s["o_" + name], s["s_" + name] = hosted(_gla_scan, plan[name][1], s["z"], lp("wg", d), lp("gla_bg", d), d == 1)
    s["ymix"] = _mix_epilogue(s["h_f"], s["h_b"], s["o_f"], s["o_b"], s["z"], lp("gla_head_norm"))
    s["mix"], s["h_mid"] = hosted(_out_proj, [("w_mlp_down", 2)], s["ymix"], p["w_out"][l], h, lp("norm_mix_post"))
    if l + 1 < depth:
        _request_weight(gather, shards, "w_in", l + 1)
        _request_weight(gather, shards, "w_out", l + 1)
    s["hn2"], s["up"], s["ff"], h_out = hosted(
        _mlp_fwd, None, s["h_mid"], lp("norm_mlp_pre"), p["w_mlp_up"][l], p["w_mlp_down"][l], lp("norm_mlp_post"))
    return h_out, s


def _layer_bwd(dh_out, l, p, s, outbox):
    lp = lambda name, *index: _LayerParam(p[name], l, *index)
    g = {}

    def hosted(fn, wanted, *args):
        ex, tickets = outbox.exchange(wanted)
        outs = fn(*args, ex)
        own = len(outs) - len(ex.landings)
        outbox.store(tickets, outs[own:])
        return outs[:own]

    d_ff, dup, dh_mid, g["norm_mlp_post"], g["norm_mlp_pre"] = hosted(
        _mlp_bwd, None, dh_out, s["ff"], s["up"], s["h_mid"], lp("norm_mlp_pre"), p["w_mlp_up"][l], p["w_mlp_down"][l],
        lp("norm_mlp_post"))
    _send_grad(outbox, "w_mlp_down", l, _matmul_tn(s["up"], d_ff, "grad_w_down", a_map=_relu_squared)
               .reshape(N_DEV, D_FF // N_DEV, D_MODEL))
    _send_grad(outbox, "w_mlp_up", l, _matmul_tn(s["hn2"], dup, "grad_w_up", column_slabs=True))
    dmix, dymix, g["norm_mix_post"] = hosted(_out_proj_bwd, [("w_mlp_down", 0)], dh_mid, s["mix"], lp("norm_mix_post"),
                                             p["w_out"][l])
    grad_w_out = _matmul_tn(s["ymix"], dmix, "grad_w_out").reshape(N_DEV, D_MODEL // N_DEV, D_MODEL)
    dhs, dgate, do, dgout, g["gla_head_norm"] = _mix_epilogue_bwd(
        dymix, s["h_f"], s["h_b"], s["o_f"], s["o_b"], s["z"], lp("gla_head_norm"))
    plan = {"f": ([("w_mlp_down", 1)], [("w_mlp_down", 2), ("w_mlp_up", 0)]), "b": ([("w_mlp_up", 1)], [("w_mlp_up", 2)])}
    dqk, dv, dzg, dxc = {}, {}, {}, {}
    for d, name in enumerate(DIRS):
        dqk[name], dv[name], dzg[name], g["wg_" + name], g["gla_bg_" + name] = hosted(
            _gla_scan_bwd, plan[name][0], do, s["z"], s["s_" + name], lp("wg", d), lp("gla_bg", d), d == 1)
        dxc[name], g["wcat_" + name], g["lru_bias_" + name], g["lru_lambda_" + name] = hosted(
            _lru_scan_bwd, plan[name][1], dhs, s["xc"], s["h_" + name], lp("wcat", d), lp("lru_bias", d), lp("lru_lam", d),
            d == 1)
    _send_grad(outbox, "w_out", l, grad_w_out)
    dxbr, g["conv_w"], g["conv_b"] = _conv_bwd(dxc["f"], dxc["b"], s["z"], lp("conv_w"))
    dz, dh_in, g["norm_mix_pre"] = _in_proj_bwd(
        (dxbr, dgate, dqk["f"], dqk["b"], dv["f"], dv["b"], dgout, dzg["f"], dzg["b"]),
        p["w_in"][l], s["h"], lp("norm_mix_pre"), dh_mid)
    return dh_in, g, dz


def _send_grad(outbox, nm, l, slabs):
    outbox.put((nm, l), slabs, _slab, _sds(slabs.shape, slabs.dtype))


def _w_in_slabs(grad_w_in):
    shard = D_IN // N_DEV
    return jnp.stack([grad_w_in[:, j * shard:(j + 1) * shard] for j in range(N_DEV)])


def _folded_block(hd):
    return slice((hd // 2) * LRU_HD, (hd // 2 + 1) * LRU_HD), slice((hd % 2) * LRU_HD, (hd % 2 + 1) * LRU_HD)


def _pack_small_grads(grads, dh0, depth):
    per_layer = ("norm_mix_pre", "norm_mix_post", "norm_mlp_pre", "norm_mlp_post", "conv_b", "gla_head_norm",
                 "lru_bias_f", "lru_bias_b", "lru_lambda_f", "lru_lambda_b", "gla_bg_f", "gla_bg_b",
                 "wcat_f", "wcat_b", "conv_w", "wg_f", "wg_b")
    ins = [grads[l][nm] for l in range(depth) for nm in per_layer]
    k = len(per_layer)
    meta_rows = PAD_ROWS // N_META

    def body(*refs):
        g = [dict(zip(per_layer, refs[l * k:(l + 1) * k])) for l in range(depth)]
        dh0_ref = refs[depth * k]
        norms, v512, v256, mats, convw, wgf, wgb, meta = refs[depth * k + 1:]
        v256[...] = jnp.zeros_like(v256)
        for l in range(depth):
            for p_, nm in enumerate(NORM_NAMES):
                norms[pl.ds(2 * p_ + l, 1), :] = g[l][nm][...]
            rows512 = [g[l]["conv_b"][...], g[l]["lru_bias_f"][:, 0:LRU_W], g[l]["lru_bias_f"][:, LRU_W:2 * LRU_W],
                       g[l]["lru_lambda_f"][...], g[l]["lru_bias_b"][:, 0:LRU_W], g[l]["lru_bias_b"][:, LRU_W:2 * LRU_W],
                       g[l]["lru_lambda_b"][...], g[l]["gla_head_norm"][...]]
            for p_, row in enumerate(rows512):
                v512[pl.ds(2 * p_ + l, 1), :] = row
            for p_, nm in enumerate(("gla_bg_f", "gla_bg_b")):
                v256[pl.ds(2 * p_ + l, 1), :] = g[l][nm][...]
            for d, name in enumerate(DIRS):
                for hd in range(LRU_HEADS):
                    rs = slice(hd * LRU_HD, (hd + 1) * LRU_HD)
                    dst_rows, dst_cols = _folded_block(hd)
                    mats[2 * d, l, dst_rows, dst_cols] = g[l]["wcat_" + name][rs, hd * LRU_HD:(hd + 1) * LRU_HD].astype(BF16)
                    mats[2 * d + 1, l, dst_rows, dst_cols] = (
                        g[l]["wcat_" + name][rs, LRU_W + hd * LRU_HD:LRU_W + (hd + 1) * LRU_HD].astype(BF16))
            for j in range(N_DEV):
                convw[j, l] = g[l]["conv_w"][0:4, j * 64:(j + 1) * 64]
                wgf[j, l] = g[l]["wg_f"][0:GLA_RANK, j * 32:(j + 1) * 32]
                wgb[j, l] = g[l]["wg_b"][GLA_RANK:2 * GLA_RANK, j * 32:(j + 1) * 32]
        for j in range(N_DEV):
            meta[j] = dh0_ref[:, j * LANES:(j + 1) * LANES]

    out_shape = [_sds((8, D_MODEL), F32), _sds((16, LRU_W), F32), _sds((8, GLA_QK), F32),
                 _sds((4, depth, LRU_W // 2, 2 * LRU_HD), BF16),
                 _sds((N_DEV, depth, 4, 64), F32), _sds((N_DEV, depth, GLA_RANK, 32), F32), _sds((N_DEV, depth, GLA_RANK, 32), F32),
                 _sds((N_DEV, N_META, LANES), F32)]
    return pl.pallas_call(
        body, name="pack_small_grads", grid=(1,),
        in_specs=[VMEM_SPEC] * (depth * k) + [pl.BlockSpec((N_META, D_MODEL), lambda i: (meta_rows, 0))],
        out_specs=[VMEM_SPEC] * len(out_shape), out_shape=out_shape, compiler_params=_params(("arbitrary",), 32),
    )(*ins, dh0)


def _my_index():
    return 4 * lax.axis_index("x") + 2 * lax.axis_index("y") + lax.axis_index("c")


def _peer(k):
    x, y, c = lax.axis_index("x"), lax.axis_index("y"), lax.axis_index("c")
    px = x ^ ((k >> 2) & 1)
    py = y ^ ((k >> 1) & 1)
    pc = c ^ (k & 1)
    return (px, py, pc), 4 * px + 2 * py + pc


ALL_PEERS = tuple(range(1, N_DEV))
PEER_GROUPS = ((1, 2, 3), (4, 5), (6, 7))


class _Exchange:
    def __init__(self):
        self.inputs, self.landings, self.transfers = [], [], []

    def add(self, array, src, landing, dst, peers=ALL_PEERS, local=True):
        if isinstance(landing, int):
            out = landing
        else:
            out = len(self.landings)
            self.landings.append(landing)
        self.transfers.append((len(self.inputs), src, out, dst, tuple(peers), local))
        self.inputs.append(array)
        return out

    def _pairs(self):
        return [(t, k) for t, tr in enumerate(self.transfers) for k in tr[4]]

    def _locals(self):
        return [t for t, tr in enumerate(self.transfers) if tr[5]]

    def out_shapes(self):
        return [g if isinstance(g, jax.ShapeDtypeStruct) else _sds(g.shape, g.dtype) for g in self.landings]

    def continued(self):
        return [(b, g) for b, g in enumerate(self.landings) if not isinstance(g, jax.ShapeDtypeStruct)]

    def sem_shapes(self):
        return [pltpu.SemaphoreType.DMA((max(len(self._pairs()), 1),)), pltpu.SemaphoreType.DMA((max(len(self._pairs()), 1),)),
                pltpu.SemaphoreType.DMA((max(len(self._locals()), 1),))]

    def _local(self, ins, outs, sems):
        me = _my_index()
        copies = []
        for s, t in enumerate(self._locals()):
            a, src, b, dst, _, _ = self.transfers[t]
            copies.append(pltpu.make_async_copy(src(ins[a], me), dst(outs[b], me), sems[2].at[s]))
        return copies

    def _remote(self, ins, outs, sems, sending):
        copies = []
        for s, (t, k) in enumerate(self._pairs()):
            a, src, b, dst, _, _ = self.transfers[t]
            peer, peer_index = _peer(k)
            copies.append(pltpu.make_async_remote_copy(
                src_ref=src(ins[a], peer_index), dst_ref=dst(outs[b], _my_index() if sending else peer_index),
                send_sem=sems[0].at[s], recv_sem=sems[1].at[s], device_id=peer, device_id_type=MESH_ID))
        return copies

    def start(self, ins, outs, sems):
        for cp in self._local(ins, outs, sems) + self._remote(ins, outs, sems, True):
            cp.start()

    def wait(self, ins, outs, sems):
        for cp in self._remote(ins, outs, sems, False):
            cp.wait_recv()
        for cp in self._remote(ins, outs, sems, True):
            cp.wait_send()
        for cp in self._local(ins, outs, sems):
            cp.wait()

    def run(self, name):
        return _hosting_call(self, None, name=name, grid=(), in_specs=[], out_specs=[], out_shape=[], scratch_shapes=[],
                             compiler_params=pltpu.CompilerParams(has_side_effects=True))()


def _hosting_call(exchange, body, *, name, grid, in_specs, out_specs, out_shape, scratch_shapes, compiler_params):
    if exchange is None or not exchange.transfers:
        return pl.pallas_call(body, name=name, grid=grid, in_specs=in_specs, out_specs=out_specs, out_shape=out_shape,
                              scratch_shapes=scratch_shapes, compiler_params=compiler_params)
    n_in, n_out, n_scr = len(in_specs), len(out_specs), len(scratch_shapes)
    x_in, x_out = len(exchange.inputs), len(exchange.landings)
    continued = exchange.continued()

    def hosted(*refs):
        ins, x_ins = refs[:n_in], refs[n_in:n_in + x_in]
        o0 = n_in + x_in + len(continued)
        outs, x_outs = refs[o0:o0 + n_out], refs[o0 + n_out:o0 + n_out + x_out]
        s0 = o0 + n_out + x_out
        scratch, sems = refs[s0:s0 + n_scr], refs[s0 + n_scr:]
        if body is None:
            exchange.start(x_ins, x_outs, sems)
            exchange.wait(x_ins, x_outs, sems)
            return
        ids = [pl.program_id(a) for a in range(len(grid))]
        first = functools.reduce(jnp.logical_and, [i == 0 for i in ids])
        last = functools.reduce(jnp.logical_and, [i == g - 1 for i, g in zip(ids, grid)])

        @pl.when(first)
        def _():
            exchange.start(x_ins, x_outs, sems)

        body(*ins, *outs, *scratch)

        @pl.when(last)
        def _():
            exchange.wait(x_ins, x_outs, sems)

    aliases = {n_in + x_in + i: n_out + b for i, (b, _) in enumerate(continued)}
    kwargs = dict(grid=grid) if grid else {}
    call = pl.pallas_call(
        hosted, name=name, in_specs=list(in_specs) + [ANY_SPEC] * (x_in + len(continued)),
        out_specs=list(out_specs) + [ANY_SPEC] * x_out, out_shape=list(out_shape) + exchange.out_shapes(),
        scratch_shapes=list(scratch_shapes) + exchange.sem_shapes(), compiler_params=compiler_params,
        input_output_aliases=aliases, **kwargs)
    return lambda *operands: call(*operands, *exchange.inputs, *[g for _, g in continued])


def _whole(ref, j):
    return ref


def _slab(ref, j):
    return ref.at[j]


def _layer_of(l):
    return lambda ref, j: ref.at[l]


def _slab_layer(l):
    return lambda ref, j: ref.at[j, l]


def _adamw(g, w, m, v):
    nm = ADAM_B1 * m + (1.0 - ADAM_B1) * g
    nv = ADAM_B2 * v + (1.0 - ADAM_B2) * jnp.square(g)
    m_hat = nm / (1.0 - ADAM_B1 ** ADAM_STEP)
    v_hat = nv / (1.0 - ADAM_B2 ** ADAM_STEP)
    return -ADAM_LR * (m_hat / (jnp.sqrt(v_hat) + ADAM_EPS) + ADAM_WD * w), nm, nv


def _sum_parts(p_ref):
    g = p_ref[0].astype(F32)
    for j in range(1, N_DEV):
        g = g + p_ref[j].astype(F32)
    return g


def _adamw_sharded(parts, w, m, v, name):
    shape = w.shape
    lead, (rows, cols) = shape[:-2], shape[-2:]
    tr = min(rows, ROW_BLOCK)
    assert rows % tr == 0
    steps = rows // tr
    nl = len(lead)
    spec = pl.BlockSpec((None,) * nl + (tr, cols), lambda *idx: idx + (0,))
    per_layer = isinstance(parts, (list, tuple))
    if per_layer:
        def part_spec(l):
            return pl.BlockSpec((N_DEV, tr, cols), lambda li, r: (0, jnp.where(li == l, r, jnp.where(li < l, 0, steps - 1)), 0))
        part_specs = [part_spec(l) for l in range(len(parts))]
    else:
        parts = [parts]
        part_specs = [pl.BlockSpec((N_DEV,) + (None,) * nl + (tr, cols), lambda *idx: (0,) + idx + (0,))]
    count = len(parts)

    def body(*refs):
        p_refs = refs[:count]
        w_ref, m_ref, v_ref, g_ref, d_ref, nm_ref, nv_ref = refs[count:]

        def update(p_ref):
            g = _sum_parts(p_ref)
            g_ref[...] = g
            d_ref[...], nm_ref[...], nv_ref[...] = _adamw(g, w_ref[...], m_ref[...], v_ref[...])

        if per_layer:
            for l in range(count):
                pl.when(pl.program_id(0) == l)(functools.partial(update, p_refs[l]))
        else:
            update(p_refs[0])

    return pl.pallas_call(
        body, name=name, grid=lead + (steps,),
        in_specs=part_specs + [spec, spec, spec], out_specs=[spec] * 4, out_shape=[_sds(shape, F32)] * 4,
        compiler_params=_params(("arbitrary",) * (nl + 1)),
    )(*parts, w, m, v)


def _adamw_replicated(gathered, w, m, v, depth):
    names = NORM_NAMES + VEC512_NAMES + VEC256_NAMES + LRU_MAT_NAMES
    count = len(names)

    def body(*refs):
        norms, v512, v256, mats = refs[:4]
        w_refs, m_refs, v_refs = (refs[4 + t * count:4 + (t + 1) * count] for t in range(3))
        outs = refs[4 + 3 * count:4 + 7 * count]
        sum_norms, sum_512, sum_256, unfolded = refs[4 + 7 * count:]
        sum_norms[...] = _sum_parts(norms)
        sum_512[...] = _sum_parts(v512)
        sum_256[...] = _sum_parts(v256)
        for n_, nm in enumerate(names):
            if nm in NORM_NAMES:
                g = sum_norms[pl.ds(depth * NORM_NAMES.index(nm), depth), :]
            elif nm in VEC512_NAMES:
                g = sum_512[pl.ds(depth * VEC512_NAMES.index(nm), depth), :]
            elif nm in VEC256_NAMES:
                g = sum_256[pl.ds(depth * VEC256_NAMES.index(nm), depth), :]
            else:
                p_ = LRU_MAT_NAMES.index(nm)
                folded = mats[0, p_].astype(F32)
                for j in range(1, N_DEV):
                    folded = folded + mats[j, p_].astype(F32)
                for hd in range(LRU_HEADS):
                    src_rows, src_cols = _folded_block(hd)
                    unfolded[:, hd * LRU_HD:(hd + 1) * LRU_HD, :] = folded[:, src_rows, src_cols]
                g = unfolded[...]
            delta, nm_, nv_ = _adamw(g, w_refs[n_][...], m_refs[n_][...], v_refs[n_][...])
            outs[n_][...] = g
            outs[count + n_][...] = delta
            outs[2 * count + n_][...] = nm_
            outs[3 * count + n_][...] = nv_

    shapes = [_sds(w[nm].shape, F32) for nm in names]
    ins = list(gathered) + [t[nm] for t in (w, m, v) for nm in names]
    outs = pl.pallas_call(
        body, name="adamw_replicated", in_specs=[VMEM_SPEC] * len(ins), out_specs=[VMEM_SPEC] * (4 * count),
        out_shape=shapes * 4,
        scratch_shapes=[pltpu.VMEM(gathered[0].shape[1:], F32), pltpu.VMEM(gathered[1].shape[1:], F32),
                        pltpu.VMEM(gathered[2].shape[1:], F32), pltpu.VMEM((depth, LRU_W, LRU_HD), F32)],
        compiler_params=_params(None, 48),
    )(*ins)
    return [dict(zip(names, outs[t * count:(t + 1) * count])) for t in range(4)]


WEIGHT_NAMES = ("meta_tokens", "norm_mix_pre", "norm_mix_post", "norm_mlp_pre", "norm_mlp_post", "w_in", "conv_w", "conv_b",
                "lru_wa_f", "lru_ba_f", "lru_wx_f", "lru_bx_f", "lru_lambda_f", "lru_wa_b", "lru_ba_b", "lru_wx_b",
                "lru_bx_b", "lru_lambda_b", "gla_wg_f", "gla_bg_f", "gla_wg_b", "gla_bg_b", "gla_head_norm", "w_out",
                "w_mlp_up", "w_mlp_down")
MATMUL_WEIGHTS = ("w_in", "w_out", "w_mlp_up", "w_mlp_down")
SMALL_SHARDED = ("conv_w", "gla_wg_f", "gla_wg_b", "meta_tokens")


def kernel(x, meta_tokens, norm_mix_pre, norm_mix_post, norm_mlp_pre, norm_mlp_post, w_in, conv_w, conv_b, lru_wa_f, lru_ba_f, lru_wx_f, lru_bx_f, lru_lambda_f, lru_wa_b, lru_ba_b, lru_wx_b, lru_bx_b, lru_lambda_b, gla_wg_f, gla_bg_f, gla_wg_b, gla_bg_b, gla_head_norm, w_out, w_mlp_up, w_mlp_down, loss_target, m_meta_tokens, m_norm_mix_pre, m_norm_mix_post, m_norm_mlp_pre, m_norm_mlp_post, m_w_in, m_conv_w, m_conv_b, m_lru_wa_f, m_lru_ba_f, m_lru_wx_f, m_lru_bx_f, m_lru_lambda_f, m_lru_wa_b, m_lru_ba_b, m_lru_wx_b, m_lru_bx_b, m_lru_lambda_b, m_gla_wg_f, m_gla_bg_f, m_gla_wg_b, m_gla_bg_b, m_gla_head_norm, m_w_out, m_w_mlp_up, m_w_mlp_down, v_meta_tokens, v_norm_mix_pre, v_norm_mix_post, v_norm_mlp_pre, v_norm_mlp_post, v_w_in, v_conv_w, v_conv_b, v_lru_wa_f, v_lru_ba_f, v_lru_wx_f, v_lru_bx_f, v_lru_lambda_f, v_lru_wa_b, v_lru_ba_b, v_lru_wx_b, v_lru_bx_b, v_lru_lambda_b, v_gla_wg_f, v_gla_bg_f, v_gla_wg_b, v_gla_bg_b, v_gla_head_norm, v_w_out, v_w_mlp_up, v_w_mlp_down):
    args = locals()
    w = {nm: args[nm] for nm in WEIGHT_NAMES}
    m = {nm: args["m_" + nm] for nm in WEIGHT_NAMES}
    v = {nm: args["v_" + nm] for nm in WEIGHT_NAMES}
    depth = w_in.shape[0]

    shards = {nm: w[nm].astype(BF16) for nm in MATMUL_WEIGHTS}
    p = {}
    gather = _Outbox(_install_weight(p))
    _request_weight(gather, shards, "w_in", 0)
    _request_weight(gather, shards, "w_out", 0)
    ex, tickets = gather.exchange()
    first_small = len(ex.landings)
    for nm in SMALL_SHARDED:
        ex.add(w[nm], _whole, _sds((N_DEV,) + w[nm].shape, F32), _slab)
    landed = ex.run("all_gather")
    gather.store(tickets, landed)
    p.update(_prepare_params(w, dict(zip(SMALL_SHARDED, landed[first_small:])), depth))

    h = jnp.concatenate([jnp.zeros((PAD_ROWS, D_MODEL), F32), p["meta_tokens"], x[0]], axis=0)
    saved = []
    for l in range(depth):
        h, s = _layer_fwd(h, l, p, gather, shards, depth)
        saved.append(s)
    dh, loss_part = _loss_and_grad(h, loss_target[0])
    loss = lax.psum(loss_part[0, 0], ("x", "y", "c"))

    received = {}
    outbox = _Outbox(received.__setitem__)
    grads = [None] * depth
    for l in reversed(range(depth)):
        dh, grads[l], dz = _layer_bwd(dh, l, p, saved[l], outbox)
        if l > 0:
            _send_grad(outbox, "w_in", l, _w_in_slabs(_matmul_tn(saved[l]["hn"], dz, "grad_w_in")))
    grad_x = dh[PAD_ROWS + N_META:][None]

    small = _pack_small_grads(grads, dh, depth)
    rep_bufs, small_slabs = small[:4], small[4:]
    ex, tickets = outbox.exchange()
    first_small = len(ex.landings)
    for g in small_slabs:
        ex.add(g, _slab, _sds(g.shape, F32), _slab)
    for g in rep_bufs:
        ex.add(g, _whole, _sds((N_DEV,) + g.shape, g.dtype), _slab)
    grad_w_in, *landed = _matmul_tn(saved[0]["hn"], dz, "grad_w_in", exchange=ex)
    outbox.store(tickets, landed)
    small_received = landed[first_small:first_small + len(small_slabs)]
    rep_received = landed[first_small + len(small_slabs):]
    _send_grad(outbox, "w_in", 0, _w_in_slabs(grad_w_in))
    ex, tickets = outbox.exchange()
    outbox.store(tickets, ex.run("exchange_grads"))

    results = [{}, {}, {}, {}]
    for nm in MATMUL_WEIGHTS:
        parts = [received[(nm, l)] for l in range(depth)]
        for t, out in enumerate(_adamw_sharded(parts, w[nm], m[nm], v[nm], "adamw_" + nm)):
            results[t][nm] = out
    for nm, parts in zip(SMALL_SHARDED, small_received):
        for t, out in enumerate(_adamw_sharded(parts, w[nm], m[nm], v[nm], "adamw_" + nm)):
            results[t][nm] = out

    def kernel_side(tree):
        return {nm: tree[nm].reshape(depth, LRU_W, LRU_HD) if nm in LRU_MAT_NAMES else tree[nm]
                for nm in NORM_NAMES + VEC512_NAMES + VEC256_NAMES + LRU_MAT_NAMES}

    for t, tree in enumerate(_adamw_replicated(rep_received, kernel_side(w), kernel_side(m), kernel_side(v), depth)):
        for nm, out in tree.items():
            results[t][nm] = out.reshape(w[nm].shape)
    return (loss, grad_x, *[results[t][nm] for t in range(4) for nm in WEIGHT_NAMES])
```

```python
import functools

import jax
import jax.numpy as jnp
from jax import lax
from jax.experimental import pallas as pl
from jax.experimental.pallas import tpu as pltpu

F32 = jnp.float32
BF16 = jnp.bfloat16

N_DEV = 8
D_MODEL = 1024
N_META = 16
ROW_BLOCK = 256
PAD_ROWS = ROW_BLOCK - N_META
CHUNK = 128
LRU_W = 512
LRU_HEADS = 8
LRU_HD = 64
LRU_C = 8.0
GLA_HEADS = 4
GLA_DK = 64
GLA_DV = 128
GLA_QK = GLA_HEADS * GLA_DK
GLA_W = GLA_HEADS * GLA_DV
GLA_RANK = 16
GATE_NORM = 16.0
D_FF = 4096
D_IN = 2592
Z_W = 2688
ZG_COL_BLOCK = 2560 // 128
EPS = 1e-6
LANES = 128

ADAM_LR = 0.001
ADAM_B1 = 0.9
ADAM_B2 = 0.999
ADAM_EPS = 1e-08
ADAM_WD = 0.01
ADAM_STEP = 10
ADAM_ROWS = 512

VMEM_SPEC = pl.BlockSpec(memory_space=pltpu.VMEM)
ANY_SPEC = pl.BlockSpec(memory_space=pl.ANY)
MESH_ID = pl.DeviceIdType.MESH


def _sds(shape, dtype):
    return jax.ShapeDtypeStruct(shape, dtype)


def _params(sem=None, vmem_mb=None):
    kw = {}
    if sem is not None:
        kw["dimension_semantics"] = sem
    if vmem_mb is not None:
        kw["vmem_limit_bytes"] = vmem_mb * 2 ** 20
    return pltpu.CompilerParams(**kw)


def _row_tile(n, cap=768):
    for t in (768, 512, 384, 256):
        if t <= cap and n % t == 0:
            return t
    raise ValueError(n)


def _col_tile(k):
    for t in (1024, 896, 768, 640, 512, 384, 256, 128):
        if k % t == 0:
            return t
    raise ValueError(k)


def _sigmoid(x):
    return 0.5 * jnp.tanh(0.5 * x) + 0.5


def _gelu_and_grad(x):
    c = 0.7978845608028654
    inner = c * (x + 0.044715 * x * x * x)
    t = jnp.tanh(inner)
    gelu = 0.5 * x * (1.0 + t)
    dgelu = 0.5 * (1.0 + t) + 0.5 * x * (1.0 - t * t) * c * (1.0 + 3.0 * 0.044715 * x * x)
    return gelu, dgelu


def _one_minus_square(a, log_a):
    return jnp.tanh(-log_a) * (1.0 + a * a)


def _rms_fwd(x, g):
    rs = lax.rsqrt(jnp.mean(x * x, axis=-1, keepdims=True) + EPS)
    return x * rs * g


def _rms_bwd(x, g, dy):
    rs = lax.rsqrt(jnp.mean(x * x, axis=-1, keepdims=True) + EPS)
    xh = x * rs
    dyg = dy * g
    dx = rs * (dyg - xh * jnp.mean(dyg * xh, axis=-1, keepdims=True))
    return dx, jnp.sum(dy * xh, axis=0, keepdims=True)


def _dot(a, b):
    return jnp.dot(a.astype(BF16), b.astype(BF16), preferred_element_type=F32)


def _dot_nt(a, b):
    return lax.dot_general(a.astype(BF16), b.astype(BF16), (((1,), (1,)), ((), ())), preferred_element_type=F32)


def _dot_tn(a, b):
    return lax.dot_general(a.astype(BF16), b.astype(BF16), (((0,), (0,)), ((), ())), preferred_element_type=F32)


class _LayerParam:
    def __init__(self, array, *index):
        self.array = array
        self.index = index

    @property
    def spec(self):
        lead = len(self.index)
        tail = self.array.shape[lead:]
        index = self.index
        return pl.BlockSpec((None,) * lead + tail, lambda *_: index + (0,) * len(tail))


def _row_ids(rows, block_index):
    return block_index * rows + lax.broadcasted_iota(jnp.int32, (rows, 1), 0)


def _accumulate(ref, value, first):
    @pl.when(first)
    def _():
        ref[...] = value

    @pl.when(jnp.logical_not(first))
    def _():
        ref[...] += value


def _norm_in_proj(h, g, w, exchange=None):
    n, d = h.shape
    zw = w.shape[1]
    tr = _row_tile(n)

    def body(h_ref, g_ref, w_ref, hn_ref, z_ref):
        hn = _rms_fwd(h_ref[...], g_ref[...]).astype(BF16)
        hn_ref[...] = hn
        z_ref[...] = jnp.dot(hn, w_ref[...], preferred_element_type=F32)

    return _hosting_call(
        exchange, body, name="norm_in_proj", grid=(n // tr,),
        in_specs=[pl.BlockSpec((tr, d), lambda i: (i, 0)), g.spec, VMEM_SPEC],
        out_specs=[pl.BlockSpec((tr, d), lambda i: (i, 0)), pl.BlockSpec((tr, zw), lambda i: (i, 0))],
        out_shape=[_sds((n, d), BF16), _sds((n, zw), F32)],
        scratch_shapes=[], compiler_params=_params(("arbitrary",), 48),
    )(h, g.array, w)


def _halo_specs(width, nb, col=0):
    per = ROW_BLOCK // 8
    prev = pl.BlockSpec((8, width), lambda i: (jnp.maximum(i * per - 1, 0), col))
    nxt = pl.BlockSpec((8, width), lambda i: (jnp.minimum((i + 1) * per, nb * per - 1), col))
    return prev, nxt


def _shift_down(x, prev8, d):
    n = x.shape[0]
    r = pltpu.roll(x, d, 0)
    p = pltpu.roll(prev8, d, 0)
    row8 = lax.broadcasted_iota(jnp.int32, (8, 1), 0)
    head = jnp.where(row8 < d, p, r[0:8])
    return jnp.concatenate([head, r[8:]], axis=0)


def _shift_up(x, next8, d):
    n = x.shape[0]
    r = pltpu.roll(x, n - d, 0)
    q = pltpu.roll(next8, 8 - d, 0)
    row8 = lax.broadcasted_iota(jnp.int32, (8, 1), 0)
    tail = jnp.where(row8 >= 8 - d, q, r[n - 8:])
    return jnp.concatenate([r[:n - 8], tail], axis=0)


def _conv_fwd(z, conv_w, conv_b):
    n = z.shape[0]
    nb = n // ROW_BLOCK
    prev_spec, next_spec = _halo_specs(LRU_W, nb)

    def body(cur_ref, prev_ref, next_ref, w_ref, b_ref, xc_ref):
        i = pl.program_id(0)
        cur = cur_ref[...]
        prev8 = prev_ref[...] * jnp.where(i > 0, 1.0, 0.0)
        next8 = next_ref[...] * jnp.where(i < nb - 1, 1.0, 0.0)
        w = [w_ref[pl.ds(k, 1), :] for k in range(4)]
        xc = (w[0] * _shift_down(cur, prev8, 2) + w[1] * _shift_down(cur, prev8, 1)
              + w[2] * cur + w[3] * _shift_up(cur, next8, 1) + b_ref[...])
        xc_ref[...] = xc

    return pl.pallas_call(
        body, name="conv_fwd", grid=(nb,),
        in_specs=[pl.BlockSpec((ROW_BLOCK, LRU_W), lambda i: (i, 0)), prev_spec, next_spec, conv_w.spec, conv_b.spec],
        out_specs=pl.BlockSpec((ROW_BLOCK, LRU_W), lambda i: (i, 0)),
        out_shape=_sds((n, LRU_W), F32),
        compiler_params=_params(("parallel",)),
    )(z, z, z, conv_w.array, conv_b.array)


def _conv_bwd(dxc_f, dxc_b, z, conv_w):
    n = z.shape[0]
    nb = n // ROW_BLOCK
    prev_spec, next_spec = _halo_specs(LRU_W, nb)
    row_spec = pl.BlockSpec((ROW_BLOCK, LRU_W), lambda i: (i, 0))

    def body(df_ref, dfp_ref, dfn_ref, db_ref, dbp_ref, dbn_ref, x_ref, xp_ref, xn_ref, w_ref,
             dx_ref, dw_ref, dbias_ref):
        i = pl.program_id(0)
        has_prev = jnp.where(i > 0, 1.0, 0.0)
        has_next = jnp.where(i < nb - 1, 1.0, 0.0)
        dxc = df_ref[...] + db_ref[...]
        dprev = (dfp_ref[...] + dbp_ref[...]) * has_prev
        dnext = (dfn_ref[...] + dbn_ref[...]) * has_next
        x = x_ref[...]
        xprev = xp_ref[...] * has_prev
        xnext = xn_ref[...] * has_next
        w = [w_ref[pl.ds(k, 1), :] for k in range(4)]
        dx_ref[...] = (w[0] * _shift_up(dxc, dnext, 2) + w[1] * _shift_up(dxc, dnext, 1)
                       + w[2] * dxc + w[3] * _shift_down(dxc, dprev, 1)).astype(BF16)
        dw = jnp.concatenate([
            jnp.sum(dxc * _shift_down(x, xprev, 2), axis=0, keepdims=True),
            jnp.sum(dxc * _shift_down(x, xprev, 1), axis=0, keepdims=True),
            jnp.sum(dxc * x, axis=0, keepdims=True),
            jnp.sum(dxc * _shift_up(x, xnext, 1), axis=0, keepdims=True),
            jnp.zeros((4, LRU_W), F32)], axis=0)
        _accumulate(dw_ref, dw, i == 0)
        _accumulate(dbias_ref, jnp.sum(dxc, axis=0, keepdims=True), i == 0)

    dx, dw, dbias = pl.pallas_call(
        body, name="conv_bwd", grid=(nb,),
        in_specs=[row_spec, prev_spec, next_spec, row_spec, prev_spec, next_spec, row_spec, prev_spec, next_spec,
                  conv_w.spec],
        out_specs=[row_spec, pl.BlockSpec((8, LRU_W), lambda i: (0, 0)), pl.BlockSpec((1, LRU_W), lambda i: (0, 0))],
        out_shape=[_sds((n, LRU_W), BF16), _sds((8, LRU_W), F32), _sds((1, LRU_W), F32)],
        compiler_params=_params(("arbitrary",)),
    )(dxc_f, dxc_f, dxc_f, dxc_b, dxc_b, dxc_b, z, z, z, conv_w.array)
    return dx, dw, dbias


def _mix_epilogue(h_f, h_b, o_f, o_b, z, head_norm):
    n = z.shape[0]
    tr = ROW_BLOCK
    spec = pl.BlockSpec((tr, 512), lambda i: (i, 0))

    def body(hf_ref, hb_ref, of_ref, ob_ref, gate_ref, gout_ref, w_ref, y_ref):
        gelu, _ = _gelu_and_grad(gate_ref[...])
        y_ref[:, 0:LRU_W] = ((hf_ref[...] + hb_ref[...]) * gelu).astype(BF16)
        o = of_ref[...] + ob_ref[...]
        gout = gout_ref[...]
        silu = gout * _sigmoid(gout)
        w = w_ref[...]
        for hd in range(GLA_HEADS):
            cs = slice(hd * GLA_DV, (hd + 1) * GLA_DV)
            oh = o[:, cs]
            on = oh * lax.rsqrt(jnp.mean(oh * oh, axis=-1, keepdims=True) + EPS)
            y_ref[:, LRU_W + hd * GLA_DV:LRU_W + (hd + 1) * GLA_DV] = (on * w[:, cs] * silu[:, cs]).astype(BF16)

    return pl.pallas_call(
        body, name="mix_epilogue", grid=(n // tr,),
        in_specs=[spec, spec, spec, spec, pl.BlockSpec((tr, 512), lambda i: (i, 1)),
                  pl.BlockSpec((tr, 512), lambda i: (i, 4)), head_norm.spec],
        out_specs=pl.BlockSpec((tr, D_MODEL), lambda i: (i, 0)),
        out_shape=_sds((n, D_MODEL), BF16),
        compiler_params=_params(("parallel",)),
    )(h_f, h_b, o_f, o_b, z, z, head_norm.array)


def _mix_epilogue_bwd(dymix, h_f, h_b, o_f, o_b, z, head_norm):
    n = z.shape[0]
    tr = ROW_BLOCK
    spec = pl.BlockSpec((tr, 512), lambda i: (i, 0))

    def body(dyl_ref, dyg_ref, hf_ref, hb_ref, of_ref, ob_ref, gate_ref, gout_ref, w_ref,
             dhs_ref, dgate_ref, do_ref, dgout_ref, dw_ref):
        i = pl.program_id(0)
        dyl = dyl_ref[...]
        gelu, dgelu = _gelu_and_grad(gate_ref[...])
        dhs_ref[...] = dyl * gelu
        dgate_ref[...] = (dyl * (hf_ref[...] + hb_ref[...]) * dgelu).astype(BF16)
        dyg = dyg_ref[...]
        o = of_ref[...] + ob_ref[...]
        gout = gout_ref[...]
        sg = _sigmoid(gout)
        silu = gout * sg
        dsilu = sg * (1.0 + gout * (1.0 - sg))
        w = w_ref[...]
        dws = []
        for hd in range(GLA_HEADS):
            cs = slice(hd * GLA_DV, (hd + 1) * GLA_DV)
            oh = o[:, cs]
            rs = lax.rsqrt(jnp.mean(oh * oh, axis=-1, keepdims=True) + EPS)
            on = oh * rs
            dy = dyg[:, cs]
            dgout_ref[:, cs] = (dy * on * w[:, cs] * dsilu[:, cs]).astype(BF16)
            dys = dy * silu[:, cs]
            dws.append(jnp.sum(dys * on, axis=0, keepdims=True))
            don = dys * w[:, cs]
            do_ref[:, cs] = (rs * (don - on * jnp.mean(don * on, axis=-1, keepdims=True))).astype(BF16)
        _accumulate(dw_ref, jnp.concatenate(dws, axis=1), i == 0)

    return pl.pallas_call(
        body, name="mix_epilogue_bwd", grid=(n // tr,),
        in_specs=[pl.BlockSpec((tr, 512), lambda i: (i, 0)), pl.BlockSpec((tr, 512), lambda i: (i, 1)),
                  spec, spec, spec, spec, pl.BlockSpec((tr, 512), lambda i: (i, 1)),
                  pl.BlockSpec((tr, 512), lambda i: (i, 4)), head_norm.spec],
        out_specs=[spec, spec, spec, spec, pl.BlockSpec((1, GLA_W), lambda i: (0, 0))],
        out_shape=[_sds((n, 512), F32)] + [_sds((n, 512), BF16)] * 3 + [_sds((1, GLA_W), F32)],
        compiler_params=_params(("arbitrary",)),
    )(dymix, dymix, h_f, h_b, o_f, o_b, z, z, head_norm.array)


def _out_proj(ymix, w_out, h, g, exchange=None):
    n, d = h.shape
    tr = _row_tile(n)
    spec = pl.BlockSpec((tr, d), lambda i: (i, 0))

    def body(y_ref, w_ref, h_ref, g_ref, mix_ref, hmid_ref):
        mix = jnp.dot(y_ref[...], w_ref[...], preferred_element_type=F32)
        mix_ref[...] = mix
        hmid_ref[...] = h_ref[...] + _rms_fwd(mix, g_ref[...])

    return _hosting_call(
        exchange, body, name="out_proj", grid=(n // tr,),
        in_specs=[spec, VMEM_SPEC, spec, g.spec],
        out_specs=[spec, spec],
        out_shape=[_sds((n, d), F32), _sds((n, d), F32)],
        scratch_shapes=[], compiler_params=_params(("arbitrary",), 44),
    )(ymix, w_out, h, g.array)


def _out_proj_bwd(dh_mid, mix, g, w_out, exchange=None):
    n, d = mix.shape
    tr = _row_tile(n)
    spec = pl.BlockSpec((tr, d), lambda i: (i, 0))

    def body(dh_ref, mix_ref, g_ref, w_ref, dmix_ref, dy_ref, dg_ref):
        i = pl.program_id(0)
        dmix, dg = _rms_bwd(mix_ref[...], g_ref[...], dh_ref[...])
        dmix = dmix.astype(BF16)
        dmix_ref[...] = dmix
        dy_ref[...] = _dot_nt(dmix, w_ref[...])
        _accumulate(dg_ref, dg, i == 0)

    return _hosting_call(
        exchange, body, name="out_proj_bwd", grid=(n // tr,),
        in_specs=[spec, spec, g.spec, VMEM_SPEC],
        out_specs=[spec, spec, pl.BlockSpec((1, d), lambda i: (0, 0))],
        out_shape=[_sds((n, d), BF16), _sds((n, d), F32), _sds((1, d), F32)],
        scratch_shapes=[], compiler_params=_params(("arbitrary",), 44),
    )(dh_mid, mix, g.array, w_out)


FF_SLAB = D_FF // N_DEV
FF_SLABS_PER_STEP = 2


def _relu_squared(up):
    return jnp.square(jnp.maximum(up.astype(F32), 0.0)).astype(BF16)


def _mlp_fwd(h_mid, g_pre, w_up, w_down, g_post, exchange=None):
    n, d = h_mid.shape
    tr = _row_tile(n)
    spec = pl.BlockSpec((tr, d), lambda i, j: (i, 0))
    ff_steps = N_DEV // FF_SLABS_PER_STEP

    def body(h_ref, gpre_ref, wup_ref, wdn_ref, gpost_ref, hn_ref, up_ref, ff_ref, hout_ref, acc_ref):
        j = pl.program_id(1)

        @pl.when(j == 0)
        def _():
            hn_ref[...] = _rms_fwd(h_ref[...], gpre_ref[...]).astype(BF16)

        hn = hn_ref[...]
        part = None
        for t in range(FF_SLABS_PER_STEP):
            cs = slice(t * FF_SLAB, (t + 1) * FF_SLAB)
            up = jnp.dot(hn, wup_ref[t], preferred_element_type=F32).astype(BF16)
            up_ref[:, cs] = up
            term = jnp.dot(_relu_squared(up), wdn_ref[cs, :], preferred_element_type=F32)
            part = term if part is None else part + term
        _accumulate(acc_ref, part, j == 0)

        @pl.when(j == ff_steps - 1)
        def _():
            ff = acc_ref[...]
            ff_ref[...] = ff
            hout_ref[...] = h_ref[...] + _rms_fwd(ff, gpost_ref[...])

    step_w = FF_SLABS_PER_STEP * FF_SLAB
    return _hosting_call(
        exchange, body, name="mlp_fwd", grid=(n // tr, ff_steps),
        in_specs=[spec, g_pre.spec, pl.BlockSpec((FF_SLABS_PER_STEP, d, FF_SLAB), lambda i, j: (j, 0, 0)),
                  pl.BlockSpec((step_w, d), lambda i, j: (j, 0)), g_post.spec],
        out_specs=[spec, pl.BlockSpec((tr, step_w), lambda i, j: (i, j)), spec, spec],
        out_shape=[_sds((n, d), BF16), _sds((n, D_FF), BF16), _sds((n, d), F32), _sds((n, d), F32)],
        scratch_shapes=[pltpu.VMEM((tr, d), F32)], compiler_params=_params(("arbitrary", "arbitrary"), 48),
    )(h_mid, g_pre.array, w_up, w_down, g_post.array)


def _mlp_bwd(dh, ff, up, h_mid, g_pre, w_up, w_down, g_post, exchange=None):
    n, d = h_mid.shape
    tr = _row_tile(n)
    ff_steps = N_DEV // FF_SLABS_PER_STEP
    step_w = FF_SLABS_PER_STEP * FF_SLAB
    spec = pl.BlockSpec((tr, d), lambda i, j: (i, 0))
    wide = pl.BlockSpec((tr, step_w), lambda i, j: (i, j))
    gspec = pl.BlockSpec((1, d), lambda i, j: (0, 0))

    def body(dh_ref, ff_ref, up_ref, h_ref, gpre_ref, wup_ref, wdn_ref, gpost_ref,
             dff_ref, dup_ref, dhmid_ref, dgpost_ref, dgpre_ref, acc_ref):
        i, j = pl.program_id(0), pl.program_id(1)

        @pl.when(j == 0)
        def _():
            dff, dgpost = _rms_bwd(ff_ref[...], gpost_ref[...], dh_ref[...])
            dff_ref[...] = dff.astype(BF16)
            _accumulate(dgpost_ref, dgpost, i == 0)

        dff = dff_ref[...]
        part = None
        for t in range(FF_SLABS_PER_STEP):
            cs = slice(t * FF_SLAB, (t + 1) * FF_SLAB)
            relu = jnp.maximum(up_ref[:, cs].astype(F32), 0.0)
            dact = _dot_nt(dff, wdn_ref[cs, :])
            dup = (dact * 2.0 * relu).astype(BF16)
            dup_ref[:, cs] = dup
            term = _dot_nt(dup, wup_ref[t])
            part = term if part is None else part + term
        _accumulate(acc_ref, part, j == 0)

        @pl.when(j == ff_steps - 1)
        def _():
            dx, dgpre = _rms_bwd(h_ref[...], gpre_ref[...], acc_ref[...])
            dhmid_ref[...] = dh_ref[...] + dx
            _accumulate(dgpre_ref, dgpre, i == 0)

    return _hosting_call(
        exchange, body, name="mlp_bwd", grid=(n // tr, ff_steps),
        in_specs=[spec, spec, wide, spec, g_pre.spec, pl.BlockSpec((FF_SLABS_PER_STEP, d, FF_SLAB), lambda i, j: (j, 0, 0)),
                  pl.BlockSpec((step_w, d), lambda i, j: (j, 0)), g_post.spec],
        out_specs=[spec, wide, spec, gspec, gspec],
        out_shape=[_sds((n, d), BF16), _sds((n, D_FF), BF16), _sds((n, d), F32), _sds((1, d), F32), _sds((1, d), F32)],
        scratch_shapes=[pltpu.VMEM((tr, d), F32)], compiler_params=_params(("arbitrary", "arbitrary"), 52),
    )(dh, ff, up, h_mid, g_pre.array, w_up, w_down, g_post.array)


def _in_proj_bwd(pieces, w_in, h, g, dh_mid):
    dxbr, dgate, dqk_f, dqk_b, dv_f, dv_b, dgout, dzg_f, dzg_b = pieces
    n, d = h.shape
    tr = _row_tile(n, 384)
    spec = pl.BlockSpec((tr, d), lambda i: (i, 0))
    s512 = pl.BlockSpec((tr, 512), lambda i: (i, 0))
    s128 = pl.BlockSpec((tr, LANES), lambda i: (i, 0))

    def body(a_ref, b_ref, cf_ref, cb_ref, df_ref, db_ref, e_ref, ff_ref, fb_ref, w_ref, h_ref, g_ref, dhm_ref,
             dz_ref, dh_ref, dg_ref):
        i = pl.program_id(0)
        real = (_row_ids(tr, i) >= PAD_ROWS).astype(F32)
        f32 = lambda ref: ref[...].astype(F32)
        dz = jnp.concatenate([f32(a_ref), f32(b_ref), f32(cf_ref) + f32(cb_ref), f32(df_ref) + f32(db_ref),
                              f32(e_ref), f32(ff_ref) + f32(fb_ref)], axis=1) * real
        dz = dz.astype(BF16)
        dz_ref[...] = dz
        dhn = _dot_nt(dz, w_ref[...])
        dx, dg = _rms_bwd(h_ref[...], g_ref[...], dhn)
        dh_ref[...] = (dhm_ref[...] + dx) * real
        _accumulate(dg_ref, dg, i == 0)

    return pl.pallas_call(
        body, name="in_proj_bwd", grid=(n // tr,),
        in_specs=[s512, s512, s512, s512, s512, s512, s512, s128, s128, VMEM_SPEC, spec, g.spec, spec],
        out_specs=[pl.BlockSpec((tr, Z_W), lambda i: (i, 0)), spec, pl.BlockSpec((1, d), lambda i: (0, 0))],
        out_shape=[_sds((n, Z_W), BF16), _sds((n, d), F32), _sds((1, d), F32)],
        compiler_params=_params(("arbitrary",), 48),
    )(dxbr, dgate, dqk_f, dqk_b, dv_f, dv_b, dgout, dzg_f, dzg_b, w_in, h, g.array, dh_mid)


def _matmul_tn(a, b, name, column_slabs=False, exchange=None, a_map=None):
    n, m = a.shape
    k = b.shape[1]
    tr, tm, tk = _row_tile(n), _col_tile(m), _col_tile(k)
    steps = n // tr
    slab = k // N_DEV
    per_step = tk // slab if column_slabs else 1

    def body(a_ref, b_ref, o_ref, acc_ref):
        r = pl.program_id(2)
        a_blk = a_ref[...] if a_map is None else a_map(a_ref[...])
        _accumulate(acc_ref, _dot_tn(a_blk, b_ref[...]), r == 0)

        @pl.when(r == steps - 1)
        def _():
            if column_slabs:
                for j in range(per_step):
                    o_ref[j] = acc_ref[:, j * slab:(j + 1) * slab].astype(BF16)
            else:
                o_ref[...] = acc_ref[...].astype(BF16)

    if column_slabs:
        out_spec = pl.BlockSpec((per_step, tm, slab), lambda mi, ki, r: (ki, mi, 0))
        out_shape = _sds((N_DEV, m, slab), BF16)
    else:
        out_spec = pl.BlockSpec((tm, tk), lambda mi, ki, r: (mi, ki))
        out_shape = _sds((m, k), BF16)
    outs = _hosting_call(
        exchange, body, name=name, grid=(m // tm, k // tk, steps),
        in_specs=[pl.BlockSpec((tr, tm), lambda mi, ki, r: (r, mi)), pl.BlockSpec((tr, tk), lambda mi, ki, r: (r, ki))],
        out_specs=[out_spec], out_shape=[out_shape], scratch_shapes=[pltpu.VMEM((tm, tk), F32)],
        compiler_params=_params(("arbitrary", "arbitrary", "arbitrary"), 40),
    )(a, b)
    return outs[0] if exchange is None else outs


def _loss_and_grad(h_out, target):
    n, d = h_out.shape
    tr = ROW_BLOCK
    first = (PAD_ROWS + N_META) // tr

    def body(h_ref, t_ref, dh_ref, loss_ref):
        i = pl.program_id(0)
        real = jnp.where(i >= first, 1.0, 0.0)
        diff = (h_ref[...] - t_ref[...]) * real
        dh_ref[...] = diff * (1.0 / d)
        part = 0.5 * jnp.sum(jnp.mean(diff * diff, axis=-1, keepdims=True), axis=0, keepdims=True)
        _accumulate(loss_ref, jnp.broadcast_to(part, (1, LANES)), i == 0)

    return pl.pallas_call(
        body, name="loss_and_grad", grid=(n // tr,),
        in_specs=[pl.BlockSpec((tr, d), lambda i: (i, 0)), pl.BlockSpec((tr, d), lambda i: (jnp.maximum(i - first, 0), 0))],
        out_specs=[pl.BlockSpec((tr, d), lambda i: (i, 0)), pl.BlockSpec((1, LANES), lambda i: (0, 0))],
        out_shape=[_sds((n, d), F32), _sds((1, LANES), F32)],
        compiler_params=_params(("arbitrary",)),
    )(h_out, target)


SUBLANES = 8


def _scan_rows(a, u, reverse, window=None):
    n = a.shape[0]
    window = window or n
    pos = lax.broadcasted_iota(jnp.int32, (n, 1), 0) & (window - 1) if window < n else lax.broadcasted_iota(jnp.int32, (n, 1), 0)
    d = 1
    while d < window:
        shift = n - d if reverse else d
        keep = (pos < window - d) if reverse else (pos >= d)
        a_s = pltpu.roll(a, shift, 0)
        u_s = pltpu.roll(u, shift, 0)
        u = jnp.where(keep, a * u_s + u, u)
        a = jnp.where(keep, a * a_s, a)
        d *= 2
    return a, u


def _scan_block(a, u, h_in, reverse, stage_ref):
    n, width = a.shape
    groups = n // SUBLANES
    lanes = [slice(cb * LANES, (cb + 1) * LANES) for cb in range(width // LANES)]
    a1, u1 = _scan_rows(a, u, reverse, window=SUBLANES)
    for cb, cs in enumerate(lanes):
        stage_ref[0, cb] = a1[:, cs]
        stage_ref[1, cb] = u1[:, cs]
    edge = 0 if reverse else SUBLANES - 1
    group_rows = pl.ds(edge, groups, stride=SUBLANES)
    a2, u2 = _scan_rows(jnp.concatenate([stage_ref[0, cb, group_rows, :] for cb in range(len(lanes))], axis=1),
                        jnp.concatenate([stage_ref[1, cb, group_rows, :] for cb in range(len(lanes))], axis=1), reverse)
    leaving = a2 * h_in + u2
    grow = lax.broadcasted_iota(jnp.int32, (groups, 1), 0)
    if reverse:
        entering = jnp.where(grow == groups - 1, h_in, pltpu.roll(leaving, groups - 1, 0))
    else:
        entering = jnp.where(grow == 0, h_in, pltpu.roll(leaving, 1, 0))
    for cb, cs in enumerate(lanes):
        for k in range(SUBLANES):
            stage_ref[0, cb, pl.ds(k, groups, stride=SUBLANES), :] = entering[:, cs]
    entering_rows = jnp.concatenate([stage_ref[0, cb] for cb in range(len(lanes))], axis=1)
    return a1 * entering_rows + u1


def _lru_gates(xc, wcat_ref, bias_ref, lam_ref):
    nl = -lam_ref[...]
    nsp = -LRU_C * (jnp.maximum(nl, 0.0) + jnp.log(1.0 + jnp.exp(-jnp.abs(nl))))
    pre = _dot(xc, wcat_ref[...]) + bias_ref[...]
    r = _sigmoid(pre[:, :LRU_W])
    ig = _sigmoid(pre[:, LRU_W:])
    log_a = r * nsp
    a = jnp.exp(log_a)
    m2 = _one_minus_square(a, log_a)
    inv_m = lax.rsqrt(jnp.maximum(m2, 1e-30))
    return r, ig, a, m2 * inv_m, inv_m, nsp


def _lru_scan(xc, wcat, bias, lam, reverse, exchange=None):
    n = xc.shape[0]
    nb = n // ROW_BLOCK
    order = (lambda i: nb - 1 - i) if reverse else (lambda i: i)
    spec = pl.BlockSpec((ROW_BLOCK, LRU_W), lambda i: (order(i), 0))
    edge = 0 if reverse else ROW_BLOCK - 1

    def body(xc_ref, wcat_ref, bias_ref, lam_ref, h_ref, carry_ref, stage_ref):
        i = pl.program_id(0)

        @pl.when(i == 0)
        def _():
            carry_ref[...] = jnp.zeros_like(carry_ref)

        xc = xc_ref[...]
        r, ig, a, m, _, _ = _lru_gates(xc, wcat_ref, bias_ref, lam_ref)
        u = jnp.where(_row_ids(ROW_BLOCK, order(i)) >= PAD_ROWS, m * (ig * xc), 0.0)
        h_ref[...] = _scan_block(a, u, carry_ref[0:1, :], reverse, stage_ref)
        carry_ref[0:1, :] = h_ref[pl.ds(edge, 1), :]

    return _hosting_call(
        exchange, body, name="lru_scan_b" if reverse else "lru_scan_f", grid=(nb,),
        in_specs=[spec, wcat.spec, bias.spec, lam.spec],
        out_specs=[spec],
        out_shape=[_sds((n, LRU_W), F32)],
        scratch_shapes=[pltpu.VMEM((8, LRU_W), F32), pltpu.VMEM((2, LRU_W // LANES, ROW_BLOCK, LANES), F32)],
        compiler_params=_params(("arbitrary",)),
    )(xc, wcat.array, bias.array, lam.array)


def _lru_scan_bwd(dhs, xc, h, wcat, bias, lam, reverse, exchange=None):
    n = xc.shape[0]
    nb = n // ROW_BLOCK
    per = ROW_BLOCK // 8
    order = (lambda i: i) if reverse else (lambda i: nb - 1 - i)
    spec = pl.BlockSpec((ROW_BLOCK, LRU_W), lambda i: (order(i), 0))
    if reverse:
        halo = pl.BlockSpec((8, LRU_W), lambda i: (jnp.minimum((order(i) + 1) * per, nb * per - 1), 0))
    else:
        halo = pl.BlockSpec((8, LRU_W), lambda i: (jnp.maximum(order(i) * per - 1, 0), 0))
    edge = ROW_BLOCK - 1 if reverse else 0

    def body(dhs_ref, xc_ref, h_ref, halo_ref, wcat_ref, bias_ref, lam_ref,
             dxc_ref, dw_ref, db_ref, dlam_ref, cdh_ref, ca_ref, tmp_ref, stage_ref):
        i = pl.program_id(0)
        ib = order(i)

        @pl.when(i == 0)
        def _():
            cdh_ref[...] = jnp.zeros_like(cdh_ref)
            ca_ref[...] = jnp.zeros_like(ca_ref)

        xc = xc_ref[...]
        r, ig, a, m, inv_m, nsp = _lru_gates(xc, wcat_ref, bias_ref, lam_ref)
        row = lax.broadcasted_iota(jnp.int32, (ROW_BLOCK, 1), 0)
        if reverse:
            coef = jnp.where(row == 0, ca_ref[0:1, :], pltpu.roll(a, 1, 0))
            h_nb = jnp.where(row == ROW_BLOCK - 1, halo_ref[0:1, :] * jnp.where(ib < nb - 1, 1.0, 0.0),
                             pltpu.roll(h_ref[...], ROW_BLOCK - 1, 0))
        else:
            coef = jnp.where(row == ROW_BLOCK - 1, ca_ref[0:1, :], pltpu.roll(a, ROW_BLOCK - 1, 0))
            h_nb = jnp.where(row == 0, halo_ref[7:8, :] * jnp.where(ib > 0, 1.0, 0.0), pltpu.roll(h_ref[...], 1, 0))
        dh = _scan_block(coef, dhs_ref[...], cdh_ref[0:1, :], not reverse, stage_ref)
        tmp_ref[...] = dh
        cdh_ref[0:1, :] = tmp_ref[pl.ds(edge, 1), :]
        tmp_ref[...] = a
        ca_ref[0:1, :] = tmp_ref[pl.ds(edge, 1), :]

        du = jnp.where(_row_ids(ROW_BLOCK, ib) >= PAD_ROWS, dh, 0.0)
        da = dh * h_nb
        dm = du * (ig * xc)
        di = du * (m * xc)
        dlog_a = da * a - dm * (a * a) * inv_m
        dr = dlog_a * nsp
        dpre = jnp.concatenate([dr * r * (1.0 - r), di * ig * (1.0 - ig)], axis=1)
        dxc_ref[...] = du * (m * ig) + _dot_nt(dpre, wcat_ref[...])
        _accumulate(dw_ref, _dot_tn(xc, dpre), i == 0)
        _accumulate(db_ref, jnp.sum(dpre, axis=0, keepdims=True), i == 0)
        _accumulate(dlam_ref, jnp.sum(dlog_a * r, axis=0, keepdims=True), i == 0)

        @pl.when(i == nb - 1)
        def _():
            dlam_ref[...] = dlam_ref[...] * (LRU_C * _sigmoid(-lam_ref[...]))

    return _hosting_call(
        exchange, body, name="lru_scan_bwd_b" if reverse else "lru_scan_bwd_f", grid=(nb,),
        in_specs=[spec, spec, spec, halo, wcat.spec, bias.spec, lam.spec],
        out_specs=[spec, pl.BlockSpec((LRU_W, 2 * LRU_W), lambda i: (0, 0)),
                   pl.BlockSpec((1, 2 * LRU_W), lambda i: (0, 0)), pl.BlockSpec((1, LRU_W), lambda i: (0, 0))],
        out_shape=[_sds((n, LRU_W), F32), _sds((LRU_W, 2 * LRU_W), F32), _sds((1, 2 * LRU_W), F32), _sds((1, LRU_W), F32)],
        scratch_shapes=[pltpu.VMEM((8, LRU_W), F32), pltpu.VMEM((8, LRU_W), F32), pltpu.VMEM((ROW_BLOCK, LRU_W), F32),
                        pltpu.VMEM((2, LRU_W // LANES, ROW_BLOCK, LANES), F32)],
        compiler_params=_params(("arbitrary",)),
    )(dhs, xc, h, h, wcat.array, bias.array, lam.array)


def _gla_rows(n):
    return 768 if n % 768 == 0 else ROW_BLOCK


def _gla_masks(reverse):
    t = lax.broadcasted_iota(jnp.int32, (CHUNK, CHUNK), 0)
    s = lax.broadcasted_iota(jnp.int32, (CHUNK, CHUNK), 1)
    if reverse:
        return (s >= t).astype(F32), s > t
    return (s <= t).astype(F32), s <= t


def _gla_gate(zg, wg_ref, bg_ref):
    pre = _dot(zg, wg_ref[...]) + bg_ref[...]
    g = (jnp.minimum(pre, 0.0) - jnp.log(1.0 + jnp.exp(-jnp.abs(pre)))) * (1.0 / GATE_NORM)
    return pre, g


def _gla_decays(gc, tri):
    b = jnp.dot(tri, gc, precision=lax.Precision.HIGHEST, preferred_element_type=F32)
    b_last = jnp.sum(gc, axis=0, keepdims=True)
    return jnp.exp(b), jnp.exp(-b), jnp.exp(b_last - b), jnp.exp(b_last)


def _gla_scan(z, wg, bg, reverse, exchange=None):
    n = z.shape[0]
    rb = _gla_rows(n)
    nb = n // rb
    cpb = rb // CHUNK
    order = (lambda i: nb - 1 - i) if reverse else (lambda i: i)
    chunks = range(cpb - 1, -1, -1) if reverse else range(cpb)

    def body(qk_ref, v_ref, zg_ref, wg_ref, bg_ref, o_ref, sall_ref, s_ref):
        i = pl.program_id(0)

        @pl.when(i == 0)
        def _():
            s_ref[...] = jnp.zeros_like(s_ref)

        tri, mask = _gla_masks(reverse)
        _, g = _gla_gate(zg_ref[...], wg_ref, bg_ref)
        heads = range(GLA_HEADS)
        ks = [slice(hd * GLA_DK, (hd + 1) * GLA_DK) for hd in heads]
        vs = [slice(hd * GLA_DV, (hd + 1) * GLA_DV) for hd in heads]
        qh, kb, v, el, p, intra, kv = {}, {}, {}, {}, {}, {}, {}
        for c in chunks:
            rows = slice(c * CHUNK, (c + 1) * CHUNK)
            eb, enb, ebl, el[c] = _gla_decays(g[rows], tri)
            qk = qk_ref[rows, :]
            q_all = (qk[:, :GLA_QK] * (GLA_DK ** -0.5) * eb).astype(BF16)
            k_all = (qk[:, GLA_QK:] * enb).astype(BF16)
            kb_all = (qk[:, GLA_QK:] * ebl).astype(BF16)
            v_all = v_ref[rows, :].astype(BF16)
            for hd in heads:
                qh[c, hd], kb[c, hd], v[c, hd] = q_all[:, ks[hd]], kb_all[:, ks[hd]], v_all[:, vs[hd]]
                p[c, hd] = _dot_nt(qh[c, hd], k_all[:, ks[hd]])
        for c in chunks:
            for hd in heads:
                intra[c, hd] = _dot(jnp.where(mask, p[c, hd], 0.0), v[c, hd])
                kv[c, hd] = _dot_tn(v[c, hd], kb[c, hd])
        state = [s_ref[:, ks[hd]] for hd in heads]
        for c in chunks:
            rows = slice(c * CHUNK, (c + 1) * CHUNK)
            for hd in heads:
                sall_ref[c, :, ks[hd]] = state[hd]
                o_ref[rows, vs[hd]] = intra[c, hd] + _dot_nt(qh[c, hd], state[hd])
                state[hd] = state[hd] * el[c][:, ks[hd]] + kv[c, hd]
        for hd in heads:
            s_ref[:, ks[hd]] = state[hd]

    return _hosting_call(
        exchange, body, name="gla_scan_b" if reverse else "gla_scan_f", grid=(nb,),
        in_specs=[pl.BlockSpec((rb, 512), lambda i: (order(i), 2)), pl.BlockSpec((rb, 512), lambda i: (order(i), 3)),
                  pl.BlockSpec((rb, LANES), lambda i: (order(i), ZG_COL_BLOCK)), wg.spec, bg.spec],
        out_specs=[pl.BlockSpec((rb, GLA_W), lambda i: (order(i), 0)),
                   pl.BlockSpec((cpb, GLA_DV, GLA_QK), lambda i: (order(i), 0, 0))],
        out_shape=[_sds((n, GLA_W), F32), _sds((n // CHUNK, GLA_DV, GLA_QK), F32)],
        scratch_shapes=[pltpu.VMEM((GLA_DV, GLA_QK), F32)],
        compiler_params=_params(("arbitrary",)),
    )(z, z, z, wg.array, bg.array)


def _gla_scan_bwd(do, z, states, wg, bg, reverse, exchange=None):
    n = z.shape[0]
    rb = _gla_rows(n)
    nb = n // rb
    cpb = rb // CHUNK
    order = (lambda i: i) if reverse else (lambda i: nb - 1 - i)
    chunks = range(cpb) if reverse else range(cpb - 1, -1, -1)

    def body(do_ref, qk_ref, v_ref, zg_ref, sall_ref, wg_ref, bg_ref,
             dqk_ref, dv_ref, dzg_ref, dwg_ref, dbg_ref, ds_ref):
        i = pl.program_id(0)

        @pl.when(i == 0)
        def _():
            ds_ref[...] = jnp.zeros_like(ds_ref)

        tri, mask = _gla_masks(reverse)
        tri_t, _ = _gla_masks(not reverse)
        zg = zg_ref[...]
        pre, g = _gla_gate(zg, wg_ref, bg_ref)
        heads = range(GLA_HEADS)
        ks = [slice(hd * GLA_DK, (hd + 1) * GLA_DK) for hd in heads]
        vs = [slice(hd * GLA_DV, (hd + 1) * GLA_DV) for hd in heads]
        dec, full, qh, kh, kb, v, dout, p, dp = {}, {}, {}, {}, {}, {}, {}, {}, {}
        for c in chunks:
            rows = slice(c * CHUNK, (c + 1) * CHUNK)
            dec[c] = _gla_decays(g[rows], tri)
            eb, enb, ebl, _ = dec[c]
            qk = qk_ref[rows, :]
            q_f = qk[:, :GLA_QK] * (GLA_DK ** -0.5) * eb
            k_f = qk[:, GLA_QK:] * enb
            kb_f = qk[:, GLA_QK:] * ebl
            full[c] = (q_f, k_f, kb_f)
            q_all, k_all, kb_all = q_f.astype(BF16), k_f.astype(BF16), kb_f.astype(BF16)
            v_all, do_all = v_ref[rows, :].astype(BF16), do_ref[rows, :].astype(BF16)
            for hd in heads:
                qh[c, hd], kh[c, hd], kb[c, hd] = q_all[:, ks[hd]], k_all[:, ks[hd]], kb_all[:, ks[hd]]
                v[c, hd], dout[c, hd] = v_all[:, vs[hd]], do_all[:, vs[hd]]
                p[c, hd] = _dot_nt(qh[c, hd], kh[c, hd])
                dp[c, hd] = _dot_nt(dout[c, hd], v[c, hd])
        dv_i, dqh, dkh, dsq, state = {}, {}, {}, {}, {}
        for c in chunks:
            for hd in heads:
                pm = jnp.where(mask, p[c, hd], 0.0).astype(BF16)
                dpm = jnp.where(mask, dp[c, hd], 0.0).astype(BF16)
                state[c, hd] = sall_ref[c, :, ks[hd]]
                dv_i[c, hd] = _dot_tn(pm, dout[c, hd])
                dqh[c, hd] = _dot(dpm, kh[c, hd]) + _dot(dout[c, hd], state[c, hd])
                dkh[c, hd] = _dot_tn(dpm, qh[c, hd])
                dsq[c, hd] = _dot_tn(dout[c, hd], qh[c, hd])
        dstate = [ds_ref[:, ks[hd]] for hd in heads]
        dkb, sds = {}, {}
        for c in chunks:
            rows = slice(c * CHUNK, (c + 1) * CHUNK)
            el = dec[c][3]
            for hd in heads:
                dv_ref[rows, vs[hd]] = (dv_i[c, hd] + _dot_nt(kb[c, hd], dstate[hd])).astype(BF16)
                dkb[c, hd] = _dot(v[c, hd], dstate[hd])
                sds[c, hd] = jnp.sum(state[c, hd] * dstate[hd], axis=0, keepdims=True)
                dstate[hd] = dstate[hd] * el[:, ks[hd]] + dsq[c, hd]
        for hd in heads:
            ds_ref[:, ks[hd]] = dstate[hd]
        dgs = [None] * cpb
        for c in chunks:
            rows = slice(c * CHUNK, (c + 1) * CHUNK)
            eb, enb, ebl, el = dec[c]
            q_f, k_f, kb_f = full[c]
            dqh_c = jnp.concatenate([dqh[c, hd] for hd in heads], axis=1)
            dkh_c = jnp.concatenate([dkh[c, hd] for hd in heads], axis=1)
            dkb_c = jnp.concatenate([dkb[c, hd] for hd in heads], axis=1)
            sds_c = jnp.concatenate([sds[c, hd] for hd in heads], axis=1)
            dqk_ref[rows, :] = jnp.concatenate([dqh_c * eb * (GLA_DK ** -0.5), dkh_c * enb + dkb_c * ebl], axis=1).astype(BF16)
            dkb_kb = dkb_c * kb_f
            db = dqh_c * q_f - dkh_c * k_f - dkb_kb
            db_last = el * sds_c + jnp.sum(dkb_kb, axis=0, keepdims=True)
            dgs[c] = jnp.dot(tri_t, db, precision=lax.Precision.HIGHEST, preferred_element_type=F32) + db_last
        dg = jnp.concatenate(dgs, axis=0)
        dpre = dg * _sigmoid(-pre) * (1.0 / GATE_NORM)
        dzg_ref[...] = _dot_nt(dpre, wg_ref[...]).astype(BF16)
        _accumulate(dwg_ref, _dot_tn(zg, dpre), i == 0)
        _accumulate(dbg_ref, jnp.sum(dpre, axis=0, keepdims=True), i == 0)

    return _hosting_call(
        exchange, body, name="gla_scan_bwd_b" if reverse else "gla_scan_bwd_f", grid=(nb,),
        in_specs=[pl.BlockSpec((rb, GLA_W), lambda i: (order(i), 0)),
                  pl.BlockSpec((rb, 512), lambda i: (order(i), 2)), pl.BlockSpec((rb, 512), lambda i: (order(i), 3)),
                  pl.BlockSpec((rb, LANES), lambda i: (order(i), ZG_COL_BLOCK)),
                  pl.BlockSpec((cpb, GLA_DV, GLA_QK), lambda i: (order(i), 0, 0)), wg.spec, bg.spec],
        out_specs=[pl.BlockSpec((rb, 512), lambda i: (order(i), 0)), pl.BlockSpec((rb, 512), lambda i: (order(i), 0)),
                   pl.BlockSpec((rb, LANES), lambda i: (order(i), 0)),
                   pl.BlockSpec((LANES, GLA_QK), lambda i: (0, 0)), pl.BlockSpec((1, GLA_QK), lambda i: (0, 0))],
        out_shape=[_sds((n, 512), BF16), _sds((n, 512), BF16), _sds((n, LANES), BF16), _sds((LANES, GLA_QK), F32),
                   _sds((1, GLA_QK), F32)],
        scratch_shapes=[pltpu.VMEM((GLA_DV, GLA_QK), F32)],
        compiler_params=_params(("arbitrary",)),
    )(do, z, z, z, states, wg.array, bg.array)


NORM_NAMES = ("norm_mix_pre", "norm_mix_post", "norm_mlp_pre", "norm_mlp_post")
VEC512_NAMES = ("conv_b", "lru_ba_f", "lru_bx_f", "lru_lambda_f", "lru_ba_b", "lru_bx_b", "lru_lambda_b", "gla_head_norm")
VEC256_NAMES = ("gla_bg_f", "gla_bg_b")
LRU_MAT_NAMES = ("lru_wa_f", "lru_wx_f", "lru_wa_b", "lru_wx_b")
DIRS = ("f", "b")


def _prepare_params(w, gathered, depth):
    row_names = NORM_NAMES + ("conv_b", "gla_head_norm")
    ins = ([w[nm] for nm in row_names] + [w["lru_ba_" + d] for d in DIRS] + [w["lru_bx_" + d] for d in DIRS]
           + [w["lru_lambda_" + d] for d in DIRS] + [w["gla_bg_" + d] for d in DIRS]
           + [w["lru_wa_" + d].reshape(depth, LRU_W, LRU_HD) for d in DIRS]
           + [w["lru_wx_" + d].reshape(depth, LRU_W, LRU_HD) for d in DIRS]
           + [gathered["conv_w"], gathered["gla_wg_f"], gathered["gla_wg_b"], gathered["meta_tokens"]])
    n_rows = len(row_names)

    def body(*refs):
        rows_in = refs[:n_rows]
        ba, bx, lam, bg, wa, wx = (refs[n_rows + 2 * t:n_rows + 2 * t + 2] for t in range(6))
        convw_g, wgf_g, wgb_g, meta_g = refs[n_rows + 12:n_rows + 16]
        outs = refs[n_rows + 16:]
        rows_out = outs[:n_rows]
        convw, wcat, bias, lam_o, wg, bg_o, meta = outs[n_rows:]
        for l in range(depth):
            for src, dst in zip(rows_in, rows_out):
                dst[l] = src[pl.ds(l, 1), :]
            convw[l] = jnp.zeros((8, LRU_W), F32)
            for j in range(N_DEV):
                convw[l, 0:4, j * 64:(j + 1) * 64] = convw_g[j, l]
            for d in range(2):
                wcat[l, d] = jnp.zeros((LRU_W, 2 * LRU_W), BF16)
                for hd in range(LRU_HEADS):
                    rs = slice(hd * LRU_HD, (hd + 1) * LRU_HD)
                    wcat[l, d, rs, hd * LRU_HD:(hd + 1) * LRU_HD] = wa[d][l, rs, :].astype(BF16)
                    wcat[l, d, rs, LRU_W + hd * LRU_HD:LRU_W + (hd + 1) * LRU_HD] = wx[d][l, rs, :].astype(BF16)
                bias[l, d, :, 0:LRU_W] = ba[d][pl.ds(l, 1), :]
                bias[l, d, :, LRU_W:2 * LRU_W] = bx[d][pl.ds(l, 1), :]
                lam_o[l, d] = lam[d][pl.ds(l, 1), :]
                bg_o[l, d] = bg[d][pl.ds(l, 1), :]
                wg[l, d] = jnp.zeros((LANES, GLA_QK), BF16)
                src = wgf_g if d == 0 else wgb_g
                for j in range(N_DEV):
                    wg[l, d, d * GLA_RANK:(d + 1) * GLA_RANK, j * 32:(j + 1) * 32] = src[j, l].astype(BF16)
        for j in range(N_DEV):
            meta[:, j * LANES:(j + 1) * LANES] = meta_g[j]

    out_shape = ([_sds((depth, 1, w[nm].shape[1]), F32) for nm in row_names]
                 + [_sds((depth, 8, LRU_W), F32), _sds((depth, 2, LRU_W, 2 * LRU_W), BF16), _sds((depth, 2, 1, 2 * LRU_W), F32),
                    _sds((depth, 2, 1, LRU_W), F32), _sds((depth, 2, LANES, GLA_QK), BF16), _sds((depth, 2, 1, GLA_QK), F32),
                    _sds((N_META, D_MODEL), F32)])
    outs = pl.pallas_call(
        body, name="prepare_params", in_specs=[VMEM_SPEC] * len(ins), out_specs=[VMEM_SPEC] * len(out_shape),
        out_shape=out_shape, compiler_params=_params(None, 32),
    )(*ins)
    prepared = dict(zip(row_names, outs[:n_rows]))
    prepared.update(zip(("conv_w", "wcat", "lru_bias", "lru_lam", "wg", "gla_bg", "meta_tokens"), outs[n_rows:]))
    return prepared


class _Outbox:
    def __init__(self, on_complete):
        self.pending, self.on_complete = {}, on_complete

    def put(self, key, array, src, landing_shape):
        self.pending[key] = dict(array=array, src=src, landing=landing_shape, groups=list(range(len(PEER_GROUPS))))

    def exchange(self, wanted=None):
        ex, tickets = _Exchange(), []
        for key, item in self.pending.items():
            groups = [g for g in item["groups"] if wanted is None or (key[0], g) in wanted]
            out = None
            for g in groups:
                landing = item["landing"] if out is None else out
                out = ex.add(item["array"], item["src"], landing, _slab, peers=PEER_GROUPS[g], local=(g == 0))
                item["groups"].remove(g)
            if groups:
                tickets.append((key, out))
        return ex, tickets

    def store(self, tickets, landed):
        for key, out in tickets:
            item = self.pending[key]
            item["landing"] = landed[out]
            if not item["groups"]:
                del self.pending[key]
                self.on_complete(key, landed[out])


def _install_weight(p):
    def install(key, g):
        nm, l = key
        if nm == "w_in":
            g = jnp.pad(jnp.concatenate([g[j] for j in range(N_DEV)], axis=1), ((0, 0), (0, Z_W - D_IN)))
        elif nm == "w_out":
            g = g.reshape(D_MODEL, D_MODEL)
        elif nm == "w_mlp_down":
            g = g.reshape(D_FF, D_MODEL)
        p.setdefault(nm, {})[l] = g
    return install


def _request_weight(gather, shards, nm, l):
    gather.put((nm, l), shards[nm], _layer_of(l), _sds((N_DEV,) + shards[nm].shape[1:], BF16))


def _layer_fwd(h, l, p, gather, shards, depth):
    lp = lambda name, *index: _LayerParam(p[name], l, *index)
    s = dict(h=h)

    def hosted(fn, wanted, *args):
        ex, tickets = gather.exchange(wanted)
        outs = fn(*args, ex)
        own = len(outs) - len(ex.landings)
        gather.store(tickets, outs[own:])
        return outs[:own]

    _request_weight(gather, shards, "w_mlp_up", l)
    _request_weight(gather, shards, "w_mlp_down", l)
    s["hn"], s["z"] = hosted(_norm_in_proj, [("w_mlp_up", 0)], h, lp("norm_mix_pre"), p["w_in"][l])
    s["xc"] = _conv_fwd(s["z"], lp("conv_w"), lp("conv_b"))
    plan = {"f": ([("w_mlp_up", 1)], [("w_mlp_up", 2)]), "b": ([("w_mlp_down", 0)], [("w_mlp_down", 1)])}
    for d, name in enumerate(DIRS):
        s["h_" + name], = hosted(_lru_scan, plan[name][0], s["xc"], lp("wcat", d), lp("lru_bias", d), lp("lru_lam", d), d == 1)
        s["o_" + name], s["s_" + name] = hosted(_gla_scan, plan[name][1], s["z"], lp("wg", d), lp("gla_bg", d), d == 1)
    s["ymix"] = _mix_epilogue(s["h_f"], s["h_b"], s["o_f"], s["o_b"], s["z"], lp("gla_head_norm"))
    s["mix"], s["h_mid"] = hosted(_out_proj, [("w_mlp_down", 2)], s["ymix"], p["w_out"][l], h, lp("norm_mix_post"))
    if l + 1 < depth:
        _request_weight(gather, shards, "w_in", l + 1)
        _request_weight(gather, shards, "w_out", l + 1)
    s["hn2"], s["up"], s["ff"], h_out = hosted(
        _mlp_fwd, None, s["h_mid"], lp("norm_mlp_pre"), p["w_mlp_up"][l], p["w_mlp_down"][l], lp("norm_mlp_post"))
    return h_out, s


def _layer_bwd(dh_out, l, p, s, outbox):
    lp = lambda name, *index: _LayerParam(p[name], l, *index)
    g = {}

    def hosted(fn, wanted, *args):
        ex, tickets = outbox.exchange(wanted)
        outs = fn(*args, ex)
        own = len(outs) - len(ex.landings)
        outbox.store(tickets, outs[own:])
        return outs[:own]

    d_ff, dup, dh_mid, g["norm_mlp_post"], g["norm_mlp_pre"] = hosted(
        _mlp_bwd, None, dh_out, s["ff"], s["up"], s["h_mid"], lp("norm_mlp_pre"), p["w_mlp_up"][l], p["w_mlp_down"][l],
        lp("norm_mlp_post"))
    _send_grad(outbox, "w_mlp_down", l, _matmul_tn(s["up"], d_ff, "grad_w_down", a_map=_relu_squared)
               .reshape(N_DEV, D_FF // N_DEV, D_MODEL))
    _send_grad(outbox, "w_mlp_up", l, _matmul_tn(s["hn2"], dup, "grad_w_up", column_slabs=True))
    dmix, dymix, g["norm_mix_post"] = hosted(_out_proj_bwd, [("w_mlp_down", 0)], dh_mid, s["mix"], lp("norm_mix_post"),
                                             p["w_out"][l])
    grad_w_out = _matmul_tn(s["ymix"], dmix, "grad_w_out").reshape(N_DEV, D_MODEL // N_DEV, D_MODEL)
    dhs, dgate, do, dgout, g["gla_head_norm"] = _mix_epilogue_bwd(
        dymix, s["h_f"], s["h_b"], s["o_f"], s["o_b"], s["z"], lp("gla_head_norm"))
    plan = {"f": ([("w_mlp_down", 1)], [("w_mlp_down", 2), ("w_mlp_up", 0)]), "b": ([("w_mlp_up", 1)], [("w_mlp_up", 2)])}
    dqk, dv, dzg, dxc = {}, {}, {}, {}
    for d, name in enumerate(DIRS):
        dqk[name], dv[name], dzg[name], g["wg_" + name], g["gla_bg_" + name] = hosted(
            _gla_scan_bwd, plan[name][0], do, s["z"], s["s_" + name], lp("wg", d), lp("gla_bg", d), d == 1)
        dxc[name], g["wcat_" + name], g["lru_bias_" + name], g["lru_lambda_" + name] = hosted(
            _lru_scan_bwd, plan[name][1], dhs, s["xc"], s["h_" + name], lp("wcat", d), lp("lru_bias", d), lp("lru_lam", d),
            d == 1)
    _send_grad(outbox, "w_out", l, grad_w_out)
    dxbr, g["conv_w"], g["conv_b"] = _conv_bwd(dxc["f"], dxc["b"], s["z"], lp("conv_w"))
    dz, dh_in, g["norm_mix_pre"] = _in_proj_bwd(
        (dxbr, dgate, dqk["f"], dqk["b"], dv["f"], dv["b"], dgout, dzg["f"], dzg["b"]),
        p["w_in"][l], s["h"], lp("norm_mix_pre"), dh_mid)
    return dh_in, g, dz


def _send_grad(outbox, nm, l, slabs):
    outbox.put((nm, l), slabs, _slab, _sds(slabs.shape, slabs.dtype))


def _w_in_slabs(grad_w_in):
    shard = D_IN // N_DEV
    return jnp.stack([grad_w_in[:, j * shard:(j + 1) * shard] for j in range(N_DEV)])


def _folded_block(hd):
    return slice((hd // 2) * LRU_HD, (hd // 2 + 1) * LRU_HD), slice((hd % 2) * LRU_HD, (hd % 2 + 1) * LRU_HD)


def _pack_small_grads(grads, dh0, depth):
    per_layer = ("norm_mix_pre", "norm_mix_post", "norm_mlp_pre", "norm_mlp_post", "conv_b", "gla_head_norm",
                 "lru_bias_f", "lru_bias_b", "lru_lambda_f", "lru_lambda_b", "gla_bg_f", "gla_bg_b",
                 "wcat_f", "wcat_b", "conv_w", "wg_f", "wg_b")
    ins = [grads[l][nm] for l in range(depth) for nm in per_layer]
    k = len(per_layer)
    meta_rows = PAD_ROWS // N_META

    def body(*refs):
        g = [dict(zip(per_layer, refs[l * k:(l + 1) * k])) for l in range(depth)]
        dh0_ref = refs[depth * k]
        norms, v512, v256, mats, convw, wgf, wgb, meta = refs[depth * k + 1:]
        v256[...] = jnp.zeros_like(v256)
        for l in range(depth):
            for p_, nm in enumerate(NORM_NAMES):
                norms[pl.ds(2 * p_ + l, 1), :] = g[l][nm][...]
            rows512 = [g[l]["conv_b"][...], g[l]["lru_bias_f"][:, 0:LRU_W], g[l]["lru_bias_f"][:, LRU_W:2 * LRU_W],
                       g[l]["lru_lambda_f"][...], g[l]["lru_bias_b"][:, 0:LRU_W], g[l]["lru_bias_b"][:, LRU_W:2 * LRU_W],
                       g[l]["lru_lambda_b"][...], g[l]["gla_head_norm"][...]]
            for p_, row in enumerate(rows512):
                v512[pl.ds(2 * p_ + l, 1), :] = row
            for p_, nm in enumerate(("gla_bg_f", "gla_bg_b")):
                v256[pl.ds(2 * p_ + l, 1), :] = g[l][nm][...]
            for d, name in enumerate(DIRS):
                for hd in range(LRU_HEADS):
                    rs = slice(hd * LRU_HD, (hd + 1) * LRU_HD)
                    dst_rows, dst_cols = _folded_block(hd)
                    mats[2 * d, l, dst_rows, dst_cols] = g[l]["wcat_" + name][rs, hd * LRU_HD:(hd + 1) * LRU_HD].astype(BF16)
                    mats[2 * d + 1, l, dst_rows, dst_cols] = (
                        g[l]["wcat_" + name][rs, LRU_W + hd * LRU_HD:LRU_W + (hd + 1) * LRU_HD].astype(BF16))
            for j in range(N_DEV):
                convw[j, l] = g[l]["conv_w"][0:4, j * 64:(j + 1) * 64]
                wgf[j, l] = g[l]["wg_f"][0:GLA_RANK, j * 32:(j + 1) * 32]
                wgb[j, l] = g[l]["wg_b"][GLA_RANK:2 * GLA_RANK, j * 32:(j + 1) * 32]
        for j in range(N_DEV):
            meta[j] = dh0_ref[:, j * LANES:(j + 1) * LANES]

    out_shape = [_sds((8, D_MODEL), F32), _sds((16, LRU_W), F32), _sds((8, GLA_QK), F32),
                 _sds((4, depth, LRU_W // 2, 2 * LRU_HD), BF16),
                 _sds((N_DEV, depth, 4, 64), F32), _sds((N_DEV, depth, GLA_RANK, 32), F32), _sds((N_DEV, depth, GLA_RANK, 32), F32),
                 _sds((N_DEV, N_META, LANES), F32)]
    return pl.pallas_call(
        body, name="pack_small_grads", grid=(1,),
        in_specs=[VMEM_SPEC] * (depth * k) + [pl.BlockSpec((N_META, D_MODEL), lambda i: (meta_rows, 0))],
        out_specs=[VMEM_SPEC] * len(out_shape), out_shape=out_shape, compiler_params=_params(("arbitrary",), 32),
    )(*ins, dh0)


def _my_index():
    return 4 * lax.axis_index("x") + 2 * lax.axis_index("y") + lax.axis_index("c")


def _peer(k):
    x, y, c = lax.axis_index("x"), lax.axis_index("y"), lax.axis_index("c")
    px = x ^ ((k >> 2) & 1)
    py = y ^ ((k >> 1) & 1)
    pc = c ^ (k & 1)
    return (px, py, pc), 4 * px + 2 * py + pc


ALL_PEERS = tuple(range(1, N_DEV))
PEER_GROUPS = ((1, 2, 3), (4, 5), (6, 7))


class _Exchange:
    def __init__(self):
        self.inputs, self.landings, self.transfers = [], [], []

    def add(self, array, src, landing, dst, peers=ALL_PEERS, local=True):
        if isinstance(landing, int):
            out = landing
        else:
            out = len(self.landings)
            self.landings.append(landing)
        self.transfers.append((len(self.inputs), src, out, dst, tuple(peers), local))
        self.inputs.append(array)
        return out

    def _pairs(self):
        return [(t, k) for t, tr in enumerate(self.transfers) for k in tr[4]]

    def _locals(self):
        return [t for t, tr in enumerate(self.transfers) if tr[5]]

    def out_shapes(self):
        return [g if isinstance(g, jax.ShapeDtypeStruct) else _sds(g.shape, g.dtype) for g in self.landings]

    def continued(self):
        return [(b, g) for b, g in enumerate(self.landings) if not isinstance(g, jax.ShapeDtypeStruct)]

    def sem_shapes(self):
        return [pltpu.SemaphoreType.DMA((max(len(self._pairs()), 1),)), pltpu.SemaphoreType.DMA((max(len(self._pairs()), 1),)),
                pltpu.SemaphoreType.DMA((max(len(self._locals()), 1),))]

    def _local(self, ins, outs, sems):
        me = _my_index()
        copies = []
        for s, t in enumerate(self._locals()):
            a, src, b, dst, _, _ = self.transfers[t]
            copies.append(pltpu.make_async_copy(src(ins[a], me), dst(outs[b], me), sems[2].at[s]))
        return copies

    def _remote(self, ins, outs, sems, sending):
        copies = []
        for s, (t, k) in enumerate(self._pairs()):
            a, src, b, dst, _, _ = self.transfers[t]
            peer, peer_index = _peer(k)
            copies.append(pltpu.make_async_remote_copy(
                src_ref=src(ins[a], peer_index), dst_ref=dst(outs[b], _my_index() if sending else peer_index),
                send_sem=sems[0].at[s], recv_sem=sems[1].at[s], device_id=peer, device_id_type=MESH_ID))
        return copies

    def start(self, ins, outs, sems):
        for cp in self._local(ins, outs, sems) + self._remote(ins, outs, sems, True):
            cp.start()

    def wait(self, ins, outs, sems):
        for cp in self._remote(ins, outs, sems, False):
            cp.wait_recv()
        for cp in self._remote(ins, outs, sems, True):
            cp.wait_send()
        for cp in self._local(ins, outs, sems):
            cp.wait()

    def run(self, name):
        return _hosting_call(self, None, name=name, grid=(), in_specs=[], out_specs=[], out_shape=[], scratch_shapes=[],
                             compiler_params=pltpu.CompilerParams(has_side_effects=True))()


def _hosting_call(exchange, body, *, name, grid, in_specs, out_specs, out_shape, scratch_shapes, compiler_params):
    if exchange is None or not exchange.transfers:
        return pl.pallas_call(body, name=name, grid=grid, in_specs=in_specs, out_specs=out_specs, out_shape=out_shape,
                              scratch_shapes=scratch_shapes, compiler_params=compiler_params)
    n_in, n_out, n_scr = len(in_specs), len(out_specs), len(scratch_shapes)
    x_in, x_out = len(exchange.inputs), len(exchange.landings)
    continued = exchange.continued()

    def hosted(*refs):
        ins, x_ins = refs[:n_in], refs[n_in:n_in + x_in]
        o0 = n_in + x_in + len(continued)
        outs, x_outs = refs[o0:o0 + n_out], refs[o0 + n_out:o0 + n_out + x_out]
        s0 = o0 + n_out + x_out
        scratch, sems = refs[s0:s0 + n_scr], refs[s0 + n_scr:]
        if body is None:
            exchange.start(x_ins, x_outs, sems)
            exchange.wait(x_ins, x_outs, sems)
            return
        ids = [pl.program_id(a) for a in range(len(grid))]
        first = functools.reduce(jnp.logical_and, [i == 0 for i in ids])
        last = functools.reduce(jnp.logical_and, [i == g - 1 for i, g in zip(ids, grid)])

        @pl.when(first)
        def _():
            exchange.start(x_ins, x_outs, sems)

        body(*ins, *outs, *scratch)

        @pl.when(last)
        def _():
            exchange.wait(x_ins, x_outs, sems)

    aliases = {n_in + x_in + i: n_out + b for i, (b, _) in enumerate(continued)}
    kwargs = dict(grid=grid) if grid else {}
    call = pl.pallas_call(
        hosted, name=name, in_specs=list(in_specs) + [ANY_SPEC] * (x_in + len(continued)),
        out_specs=list(out_specs) + [ANY_SPEC] * x_out, out_shape=list(out_shape) + exchange.out_shapes(),
        scratch_shapes=list(scratch_shapes) + exchange.sem_shapes(), compiler_params=compiler_params,
        input_output_aliases=aliases, **kwargs)
    return lambda *operands: call(*operands, *exchange.inputs, *[g for _, g in continued])


def _whole(ref, j):
    return ref


def _slab(ref, j):
    return ref.at[j]


def _layer_of(l):
    return lambda ref, j: ref.at[l]


def _slab_layer(l):
    return lambda ref, j: ref.at[j, l]


def _adamw(g, w, m, v):
    nm = ADAM_B1 * m + (1.0 - ADAM_B1) * g
    nv = ADAM_B2 * v + (1.0 - ADAM_B2) * jnp.square(g)
    m_hat = nm / (1.0 - ADAM_B1 ** ADAM_STEP)
    v_hat = nv / (1.0 - ADAM_B2 ** ADAM_STEP)
    return -ADAM_LR * (m_hat / (jnp.sqrt(v_hat) + ADAM_EPS) + ADAM_WD * w), nm, nv


def _sum_parts(p_ref):
    g = p_ref[0].astype(F32)
    for j in range(1, N_DEV):
        g = g + p_ref[j].astype(F32)
    return g


def _adamw_sharded(parts, w, m, v, name):
    shape = w.shape
    lead, (rows, cols) = shape[:-2], shape[-2:]
    tr = min(rows, ROW_BLOCK)
    assert rows % tr == 0
    steps = rows // tr
    nl = len(lead)
    spec = pl.BlockSpec((None,) * nl + (tr, cols), lambda *idx: idx + (0,))
    per_layer = isinstance(parts, (list, tuple))
    if per_layer:
        def part_spec(l):
            return pl.BlockSpec((N_DEV, tr, cols), lambda li, r: (0, jnp.where(li == l, r, jnp.where(li < l, 0, steps - 1)), 0))
        part_specs = [part_spec(l) for l in range(len(parts))]
    else:
        parts = [parts]
        part_specs = [pl.BlockSpec((N_DEV,) + (None,) * nl + (tr, cols), lambda *idx: (0,) + idx + (0,))]
    count = len(parts)

    def body(*refs):
        p_refs = refs[:count]
        w_ref, m_ref, v_ref, g_ref, d_ref, nm_ref, nv_ref = refs[count:]

        def update(p_ref):
            g = _sum_parts(p_ref)
            g_ref[...] = g
            d_ref[...], nm_ref[...], nv_ref[...] = _adamw(g, w_ref[...], m_ref[...], v_ref[...])

        if per_layer:
            for l in range(count):
                pl.when(pl.program_id(0) == l)(functools.partial(update, p_refs[l]))
        else:
            update(p_refs[0])

    return pl.pallas_call(
        body, name=name, grid=lead + (steps,),
        in_specs=part_specs + [spec, spec, spec], out_specs=[spec] * 4, out_shape=[_sds(shape, F32)] * 4,
        compiler_params=_params(("arbitrary",) * (nl + 1)),
    )(*parts, w, m, v)


def _adamw_replicated(gathered, w, m, v, depth):
    names = NORM_NAMES + VEC512_NAMES + VEC256_NAMES + LRU_MAT_NAMES
    count = len(names)

    def body(*refs):
        norms, v512, v256, mats = refs[:4]
        w_refs, m_refs, v_refs = (refs[4 + t * count:4 + (t + 1) * count] for t in range(3))
        outs = refs[4 + 3 * count:4 + 7 * count]
        sum_norms, sum_512, sum_256, unfolded = refs[4 + 7 * count:]
        sum_norms[...] = _sum_parts(norms)
        sum_512[...] = _sum_parts(v512)
        sum_256[...] = _sum_parts(v256)
        for n_, nm in enumerate(names):
            if nm in NORM_NAMES:
                g = sum_norms[pl.ds(depth * NORM_NAMES.index(nm), depth), :]
            elif nm in VEC512_NAMES:
                g = sum_512[pl.ds(depth * VEC512_NAMES.index(nm), depth), :]
            elif nm in VEC256_NAMES:
                g = sum_256[pl.ds(depth * VEC256_NAMES.index(nm), depth), :]
            else:
                p_ = LRU_MAT_NAMES.index(nm)
                folded = mats[0, p_].astype(F32)
                for j in range(1, N_DEV):
                    folded = folded + mats[j, p_].astype(F32)
                for hd in range(LRU_HEADS):
                    src_rows, src_cols = _folded_block(hd)
                    unfolded[:, hd * LRU_HD:(hd + 1) * LRU_HD, :] = folded[:, src_rows, src_cols]
                g = unfolded[...]
            delta, nm_, nv_ = _adamw(g, w_refs[n_][...], m_refs[n_][...], v_refs[n_][...])
            outs[n_][...] = g
            outs[count + n_][...] = delta
            outs[2 * count + n_][...] = nm_
            outs[3 * count + n_][...] = nv_

    shapes = [_sds(w[nm].shape, F32) for nm in names]
    ins = list(gathered) + [t[nm] for t in (w, m, v) for nm in names]
    outs = pl.pallas_call(
        body, name="adamw_replicated", in_specs=[VMEM_SPEC] * len(ins), out_specs=[VMEM_SPEC] * (4 * count),
        out_shape=shapes * 4,
        scratch_shapes=[pltpu.VMEM(gathered[0].shape[1:], F32), pltpu.VMEM(gathered[1].shape[1:], F32),
                        pltpu.VMEM(gathered[2].shape[1:], F32), pltpu.VMEM((depth, LRU_W, LRU_HD), F32)],
        compiler_params=_params(None, 48),
    )(*ins)
    return [dict(zip(names, outs[t * count:(t + 1) * count])) for t in range(4)]


WEIGHT_NAMES = ("meta_tokens", "norm_mix_pre", "norm_mix_post", "norm_mlp_pre", "norm_mlp_post", "w_in", "conv_w", "conv_b",
                "lru_wa_f", "lru_ba_f", "lru_wx_f", "lru_bx_f", "lru_lambda_f", "lru_wa_b", "lru_ba_b", "lru_wx_b",
                "lru_bx_b", "lru_lambda_b", "gla_wg_f", "gla_bg_f", "gla_wg_b", "gla_bg_b", "gla_head_norm", "w_out",
                "w_mlp_up", "w_mlp_down")
MATMUL_WEIGHTS = ("w_in", "w_out", "w_mlp_up", "w_mlp_down")
SMALL_SHARDED = ("conv_w", "gla_wg_f", "gla_wg_b", "meta_tokens")


def kernel(x, meta_tokens, norm_mix_pre, norm_mix_post, norm_mlp_pre, norm_mlp_post, w_in, conv_w, conv_b, lru_wa_f, lru_ba_f, lru_wx_f, lru_bx_f, lru_lambda_f, lru_wa_b, lru_ba_b, lru_wx_b, lru_bx_b, lru_lambda_b, gla_wg_f, gla_bg_f, gla_wg_b, gla_bg_b, gla_head_norm, w_out, w_mlp_up, w_mlp_down, loss_target, m_meta_tokens, m_norm_mix_pre, m_norm_mix_post, m_norm_mlp_pre, m_norm_mlp_post, m_w_in, m_conv_w, m_conv_b, m_lru_wa_f, m_lru_ba_f, m_lru_wx_f, m_lru_bx_f, m_lru_lambda_f, m_lru_wa_b, m_lru_ba_b, m_lru_wx_b, m_lru_bx_b, m_lru_lambda_b, m_gla_wg_f, m_gla_bg_f, m_gla_wg_b, m_gla_bg_b, m_gla_head_norm, m_w_out, m_w_mlp_up, m_w_mlp_down, v_meta_tokens, v_norm_mix_pre, v_norm_mix_post, v_norm_mlp_pre, v_norm_mlp_post, v_w_in, v_conv_w, v_conv_b, v_lru_wa_f, v_lru_ba_f, v_lru_wx_f, v_lru_bx_f, v_lru_lambda_f, v_lru_wa_b, v_lru_ba_b, v_lru_wx_b, v_lru_bx_b, v_lru_lambda_b, v_gla_wg_f, v_gla_bg_f, v_gla_wg_b, v_gla_bg_b, v_gla_head_norm, v_w_out, v_w_mlp_up, v_w_mlp_down):
    args = locals()
    w = {nm: args[nm] for nm in WEIGHT_NAMES}
    m = {nm: args["m_" + nm] for nm in WEIGHT_NAMES}
    v = {nm: args["v_" + nm] for nm in WEIGHT_NAMES}
    depth = w_in.shape[0]

    shards = {nm: w[nm].astype(BF16) for nm in MATMUL_WEIGHTS}
    p = {}
    gather = _Outbox(_install_weight(p))
    _request_weight(gather, shards, "w_in", 0)
    _request_weight(gather, shards, "w_out", 0)
    ex, tickets = gather.exchange()
    first_small = len(ex.landings)
    for nm in SMALL_SHARDED:
        ex.add(w[nm], _whole, _sds((N_DEV,) + w[nm].shape, F32), _slab)
    landed = ex.run("all_gather")
    gather.store(tickets, landed)
    p.update(_prepare_params(w, dict(zip(SMALL_SHARDED, landed[first_small:])), depth))

    h = jnp.concatenate([jnp.zeros((PAD_ROWS, D_MODEL), F32), p["meta_tokens"], x[0]], axis=0)
    saved = []
    for l in range(depth):
        h, s = _layer_fwd(h, l, p, gather, shards, depth)
        saved.append(s)
    dh, loss_part = _loss_and_grad(h, loss_target[0])
    loss = lax.psum(loss_part[0, 0], ("x", "y", "c"))

    received = {}
    outbox = _Outbox(received.__setitem__)
    grads = [None] * depth
    for l in reversed(range(depth)):
        dh, grads[l], dz = _layer_bwd(dh, l, p, saved[l], outbox)
        if l > 0:
            _send_grad(outbox, "w_in", l, _w_in_slabs(_matmul_tn(saved[l]["hn"], dz, "grad_w_in")))
    grad_x = dh[PAD_ROWS + N_META:][None]

    small = _pack_small_grads(grads, dh, depth)
    rep_bufs, small_slabs = small[:4], small[4:]
    ex, tickets = outbox.exchange()
    first_small = len(ex.landings)
    for g in small_slabs:
        ex.add(g, _slab, _sds(g.shape, F32), _slab)
    for g in rep_bufs:
        ex.add(g, _whole, _sds((N_DEV,) + g.shape, g.dtype), _slab)
    grad_w_in, *landed = _matmul_tn(saved[0]["hn"], dz, "grad_w_in", exchange=ex)
    outbox.store(tickets, landed)
    small_received = landed[first_small:first_small + len(small_slabs)]
    rep_received = landed[first_small + len(small_slabs):]
    _send_grad(outbox, "w_in", 0, _w_in_slabs(grad_w_in))
    ex, tickets = outbox.exchange()
    outbox.store(tickets, ex.run("exchange_grads"))

    results = [{}, {}, {}, {}]
    for nm in MATMUL_WEIGHTS:
        parts = [received[(nm, l)] for l in range(depth)]
        for t, out in enumerate(_adamw_sharded(parts, w[nm], m[nm], v[nm], "adamw_" + nm)):
            results[t][nm] = out
    for nm, parts in zip(SMALL_SHARDED, small_received):
        for t, out in enumerate(_adamw_sharded(parts, w[nm], m[nm], v[nm], "adamw_" + nm)):
            results[t][nm] = out

    def kernel_side(tree):
        return {nm: tree[nm].reshape(depth, LRU_W, LRU_HD) if nm in LRU_MAT_NAMES else tree[nm]
                for nm in NORM_NAMES + VEC512_NAMES + VEC256_NAMES + LRU_MAT_NAMES}

    for t, tree in enumerate(_adamw_replicated(rep_received, kernel_side(w), kernel_side(m), kernel_side(v), depth)):
        for nm, out in tree.items():
            results[t][nm] = out.reshape(w[nm].shape)
    return (loss, grad_x, *[results[t][nm] for t in range(4) for nm in WEIGHT_NAMES])
```

```python
import functools

import jax
import jax.numpy as jnp
from jax import lax
from jax.experimental import pallas as pl
from jax.experimental.pallas import tpu as pltpu

F32 = jnp.float32
BF16 = jnp.bfloat16

N_DEV = 8
D_MODEL = 1024
N_META = 16
ROW_BLOCK = 256
PAD_ROWS = ROW_BLOCK - N_META
CHUNK = 128
LRU_W = 512
LRU_HEADS = 8
LRU_HD = 64
LRU_C = 8.0
GLA_HEADS = 4
GLA_DK = 64
GLA_DV = 128
GLA_QK = GLA_HEADS * GLA_DK
GLA_W = GLA_HEADS * GLA_DV
GLA_RANK = 16
GATE_NORM = 16.0
D_FF = 4096
D_IN = 2592
Z_W = 2688
ZG_COL_BLOCK = 2560 // 128
EPS = 1e-6
LANES = 128

ADAM_LR = 0.001
ADAM_B1 = 0.9
ADAM_B2 = 0.999
ADAM_EPS = 1e-08
ADAM_WD = 0.01
ADAM_STEP = 10
ADAM_ROWS = 512

VMEM_SPEC = pl.BlockSpec(memory_space=pltpu.VMEM)
ANY_SPEC = pl.BlockSpec(memory_space=pl.ANY)
MESH_ID = pl.DeviceIdType.MESH


def _sds(shape, dtype):
    return jax.ShapeDtypeStruct(shape, dtype)


def _params(sem=None, vmem_mb=None):
    kw = {}
    if sem is not None:
        kw["dimension_semantics"] = sem
    if vmem_mb is not None:
        kw["vmem_limit_bytes"] = vmem_mb * 2 ** 20
    return pltpu.CompilerParams(**kw)


def _row_tile(n, cap=768):
    for t in (768, 512, 384, 256):
        if t <= cap and n % t == 0:
            return t
    raise ValueError(n)


def _col_tile(k):
    for t in (1024, 896, 768, 640, 512, 384, 256, 128):
        if k % t == 0:
            return t
    raise ValueError(k)


def _sigmoid(x):
    return 0.5 * jnp.tanh(0.5 * x) + 0.5


def _gelu_and_grad(x):
    c = 0.7978845608028654
    inner = c * (x + 0.044715 * x * x * x)
    t = jnp.tanh(inner)
    gelu = 0.5 * x * (1.0 + t)
    dgelu = 0.5 * (1.0 + t) + 0.5 * x * (1.0 - t * t) * c * (1.0 + 3.0 * 0.044715 * x * x)
    return gelu, dgelu


def _one_minus_square(a, log_a):
    return jnp.tanh(-log_a) * (1.0 + a * a)


def _rms_fwd(x, g):
    rs = lax.rsqrt(jnp.mean(x * x, axis=-1, keepdims=True) + EPS)
    return x * rs * g


def _rms_bwd(x, g, dy):
    rs = lax.rsqrt(jnp.mean(x * x, axis=-1, keepdims=True) + EPS)
    xh = x * rs
    dyg = dy * g
    dx = rs * (dyg - xh * jnp.mean(dyg * xh, axis=-1, keepdims=True))
    return dx, jnp.sum(dy * xh, axis=0, keepdims=True)


def _dot(a, b):
    return jnp.dot(a.astype(BF16), b.astype(BF16), preferred_element_type=F32)


def _dot_nt(a, b):
    return lax.dot_general(a.astype(BF16), b.astype(BF16), (((1,), (1,)), ((), ())), preferred_element_type=F32)


def _dot_tn(a, b):
    return lax.dot_general(a.astype(BF16), b.astype(BF16), (((0,), (0,)), ((), ())), preferred_element_type=F32)


class _LayerParam:
    def __init__(self, array, *index):
        self.array = array
        self.index = index

    @property
    def spec(self):
        lead = len(self.index)
        tail = self.array.shape[lead:]
        index = self.index
        return pl.BlockSpec((None,) * lead + tail, lambda *_: index + (0,) * len(tail))


def _row_ids(rows, block_index):
    return block_index * rows + lax.broadcasted_iota(jnp.int32, (rows, 1), 0)


def _accumulate(ref, value, first):
    @pl.when(first)
    def _():
        ref[...] = value

    @pl.when(jnp.logical_not(first))
    def _():
        ref[...] += value


def _norm_in_proj(h, g, w, exchange=None):
    n, d = h.shape
    zw = w.shape[1]
    tr = _row_tile(n)

    def body(h_ref, g_ref, w_ref, hn_ref, z_ref):
        hn = _rms_fwd(h_ref[...], g_ref[...]).astype(BF16)
        hn_ref[...] = hn
        z_ref[...] = jnp.dot(hn, w_ref[...], preferred_element_type=F32)

    return _hosting_call(
        exchange, body, name="norm_in_proj", grid=(n // tr,),
        in_specs=[pl.BlockSpec((tr, d), lambda i: (i, 0)), g.spec, VMEM_SPEC],
        out_specs=[pl.BlockSpec((tr, d), lambda i: (i, 0)), pl.BlockSpec((tr, zw), lambda i: (i, 0))],
        out_shape=[_sds((n, d), BF16), _sds((n, zw), F32)],
        scratch_shapes=[], compiler_params=_params(("arbitrary",), 48),
    )(h, g.array, w)


def _halo_specs(width, nb, col=0):
    per = ROW_BLOCK // 8
    prev = pl.BlockSpec((8, width), lambda i: (jnp.maximum(i * per - 1, 0), col))
    nxt = pl.BlockSpec((8, width), lambda i: (jnp.minimum((i + 1) * per, nb * per - 1), col))
    return prev, nxt


def _shift_down(x, prev8, d):
    n = x.shape[0]
    r = pltpu.roll(x, d, 0)
    p = pltpu.roll(prev8, d, 0)
    row8 = lax.broadcasted_iota(jnp.int32, (8, 1), 0)
    head = jnp.where(row8 < d, p, r[0:8])
    return jnp.concatenate([head, r[8:]], axis=0)


def _shift_up(x, next8, d):
    n = x.shape[0]
    r = pltpu.roll(x, n - d, 0)
    q = pltpu.roll(next8, 8 - d, 0)
    row8 = lax.broadcasted_iota(jnp.int32, (8, 1), 0)
    tail = jnp.where(row8 >= 8 - d, q, r[n - 8:])
    return jnp.concatenate([r[:n - 8], tail], axis=0)


def _conv_fwd(z, conv_w, conv_b):
    n = z.shape[0]
    nb = n // ROW_BLOCK
    prev_spec, next_spec = _halo_specs(LRU_W, nb)

    def body(cur_ref, prev_ref, next_ref, w_ref, b_ref, xc_ref):
        i = pl.program_id(0)
        cur = cur_ref[...]
        prev8 = prev_ref[...] * jnp.where(i > 0, 1.0, 0.0)
        next8 = next_ref[...] * jnp.where(i < nb - 1, 1.0, 0.0)
        w = [w_ref[pl.ds(k, 1), :] for k in range(4)]
        xc = (w[0] * _shift_down(cur, prev8, 2) + w[1] * _shift_down(cur, prev8, 1)
              + w[2] * cur + w[3] * _shift_up(cur, next8, 1) + b_ref[...])
        xc_ref[...] = xc

    return pl.pallas_call(
        body, name="conv_fwd", grid=(nb,),
        in_specs=[pl.BlockSpec((ROW_BLOCK, LRU_W), lambda i: (i, 0)), prev_spec, next_spec, conv_w.spec, conv_b.spec],
        out_specs=pl.BlockSpec((ROW_BLOCK, LRU_W), lambda i: (i, 0)),
        out_shape=_sds((n, LRU_W), F32),
        compiler_params=_params(("parallel",)),
    )(z, z, z, conv_w.array, conv_b.array)


def _conv_bwd(dxc_f, dxc_b, z, conv_w):
    n = z.shape[0]
    nb = n // ROW_BLOCK
    prev_spec, next_spec = _halo_specs(LRU_W, nb)
    row_spec = pl.BlockSpec((ROW_BLOCK, LRU_W), lambda i: (i, 0))

    def body(df_ref, dfp_ref, dfn_ref, db_ref, dbp_ref, dbn_ref, x_ref, xp_ref, xn_ref, w_ref,
             dx_ref, dw_ref, dbias_ref):
        i = pl.program_id(0)
        has_prev = jnp.where(i > 0, 1.0, 0.0)
        has_next = jnp.where(i < nb - 1, 1.0, 0.0)
        dxc = df_ref[...] + db_ref[...]
        dprev = (dfp_ref[...] + dbp_ref[...]) * has_prev
        dnext = (dfn_ref[...] + dbn_ref[...]) * has_next
        x = x_ref[...]
        xprev = xp_ref[...] * has_prev
        xnext = xn_ref[...] * has_next
        w = [w_ref[pl.ds(k, 1), :] for k in range(4)]
        dx_ref[...] = (w[0] * _shift_up(dxc, dnext, 2) + w[1] * _shift_up(dxc, dnext, 1)
                       + w[2] * dxc + w[3] * _shift_down(dxc, dprev, 1)).astype(BF16)
        dw = jnp.concatenate([
            jnp.sum(dxc * _shift_down(x, xprev, 2), axis=0, keepdims=True),
            jnp.sum(dxc * _shift_down(x, xprev, 1), axis=0, keepdims=True),
            jnp.sum(dxc * x, axis=0, keepdims=True),
            jnp.sum(dxc * _shift_up(x, xnext, 1), axis=0, keepdims=True),
            jnp.zeros((4, LRU_W), F32)], axis=0)
        _accumulate(dw_ref, dw, i == 0)
        _accumulate(dbias_ref, jnp.sum(dxc, axis=0, keepdims=True), i == 0)

    dx, dw, dbias = pl.pallas_call(
        body, name="conv_bwd", grid=(nb,),
        in_specs=[row_spec, prev_spec, next_spec, row_spec, prev_spec, next_spec, row_spec, prev_spec, next_spec,
                  conv_w.spec],
        out_specs=[row_spec, pl.BlockSpec((8, LRU_W), lambda i: (0, 0)), pl.BlockSpec((1, LRU_W), lambda i: (0, 0))],
        out_shape=[_sds((n, LRU_W), BF16), _sds((8, LRU_W), F32), _sds((1, LRU_W), F32)],
        compiler_params=_params(("arbitrary",)),
    )(dxc_f, dxc_f, dxc_f, dxc_b, dxc_b, dxc_b, z, z, z, conv_w.array)
    return dx, dw, dbias


def _mix_epilogue(h_f, h_b, o_f, o_b, z, head_norm):
    n = z.shape[0]
    tr = ROW_BLOCK
    spec = pl.BlockSpec((tr, 512), lambda i: (i, 0))

    def body(hf_ref, hb_ref, of_ref, ob_ref, gate_ref, gout_ref, w_ref, y_ref):
        gelu, _ = _gelu_and_grad(gate_ref[...])
        y_ref[:, 0:LRU_W] = ((hf_ref[...] + hb_ref[...]) * gelu).astype(BF16)
        o = of_ref[...] + ob_ref[...]
        gout = gout_ref[...]
        silu = gout * _sigmoid(gout)
        w = w_ref[...]
        for hd in range(GLA_HEADS):
            cs = slice(hd * GLA_DV, (hd + 1) * GLA_DV)
            oh = o[:, cs]
            on = oh * lax.rsqrt(jnp.mean(oh * oh, axis=-1, keepdims=True) + EPS)
            y_ref[:, LRU_W + hd * GLA_DV:LRU_W + (hd + 1) * GLA_DV] = (on * w[:, cs] * silu[:, cs]).astype(BF16)

    return pl.pallas_call(
        body, name="mix_epilogue", grid=(n // tr,),
        in_specs=[spec, spec, spec, spec, pl.BlockSpec((tr, 512), lambda i: (i, 1)),
                  pl.BlockSpec((tr, 512), lambda i: (i, 4)), head_norm.spec],
        out_specs=pl.BlockSpec((tr, D_MODEL), lambda i: (i, 0)),
        out_shape=_sds((n, D_MODEL), BF16),
        compiler_params=_params(("parallel",)),
    )(h_f, h_b, o_f, o_b, z, z, head_norm.array)


def _mix_epilogue_bwd(dymix, h_f, h_b, o_f, o_b, z, head_norm):
    n = z.shape[0]
    tr = ROW_BLOCK
    spec = pl.BlockSpec((tr, 512), lambda i: (i, 0))

    def body(dyl_ref, dyg_ref, hf_ref, hb_ref, of_ref, ob_ref, gate_ref, gout_ref, w_ref,
             dhs_ref, dgate_ref, do_ref, dgout_ref, dw_ref):
        i = pl.program_id(0)
        dyl = dyl_ref[...]
        gelu, dgelu = _gelu_and_grad(gate_ref[...])
        dhs_ref[...] = dyl * gelu
        dgate_ref[...] = (dyl * (hf_ref[...] + hb_ref[...]) * dgelu).astype(BF16)
        dyg = dyg_ref[...]
        o = of_ref[...] + ob_ref[...]
        gout = gout_ref[...]
        sg = _sigmoid(gout)
        silu = gout * sg
        dsilu = sg * (1.0 + gout * (1.0 - sg))
        w = w_ref[...]
        dws = []
        for hd in range(GLA_HEADS):
            cs = slice(hd * GLA_DV, (hd + 1) * GLA_DV)
            oh = o[:, cs]
            rs = lax.rsqrt(jnp.mean(oh * oh, axis=-1, keepdims=True) + EPS)
            on = oh * rs
            dy = dyg[:, cs]
            dgout_ref[:, cs] = (dy * on * w[:, cs] * dsilu[:, cs]).astype(BF16)
            dys = dy * silu[:, cs]
            dws.append(jnp.sum(dys * on, axis=0, keepdims=True))
            don = dys * w[:, cs]
            do_ref[:, cs] = (rs * (don - on * jnp.mean(don * on, axis=-1, keepdims=True))).astype(BF16)
        _accumulate(dw_ref, jnp.concatenate(dws, axis=1), i == 0)

    return pl.pallas_call(
        body, name="mix_epilogue_bwd", grid=(n // tr,),
        in_specs=[pl.BlockSpec((tr, 512), lambda i: (i, 0)), pl.BlockSpec((tr, 512), lambda i: (i, 1)),
                  spec, spec, spec, spec, pl.BlockSpec((tr, 512), lambda i: (i, 1)),
                  pl.BlockSpec((tr, 512), lambda i: (i, 4)), head_norm.spec],
        out_specs=[spec, spec, spec, spec, pl.BlockSpec((1, GLA_W), lambda i: (0, 0))],
        out_shape=[_sds((n, 512), F32)] + [_sds((n, 512), BF16)] * 3 + [_sds((1, GLA_W), F32)],
        compiler_params=_params(("arbitrary",)),
    )(dymix, dymix, h_f, h_b, o_f, o_b, z, z, head_norm.array)


def _out_proj(ymix, w_out, h, g, exchange=None):
    n, d = h.shape
    tr = _row_tile(n)
    spec = pl.BlockSpec((tr, d), lambda i: (i, 0))

    def body(y_ref, w_ref, h_ref, g_ref, mix_ref, hmid_ref):
        mix = jnp.dot(y_ref[...], w_ref[...], preferred_element_type=F32)
        mix_ref[...] = mix
        hmid_ref[...] = h_ref[...] + _rms_fwd(mix, g_ref[...])

    return _hosting_call(
        exchange, body, name="out_proj", grid=(n // tr,),
        in_specs=[spec, VMEM_SPEC, spec, g.spec],
        out_specs=[spec, spec],
        out_shape=[_sds((n, d), F32), _sds((n, d), F32)],
        scratch_shapes=[], compiler_params=_params(("arbitrary",), 44),
    )(ymix, w_out, h, g.array)


def _out_proj_bwd(dh_mid, mix, g, w_out, exchange=None):
    n, d = mix.shape
    tr = _row_tile(n)
    spec = pl.BlockSpec((tr, d), lambda i: (i, 0))

    def body(dh_ref, mix_ref, g_ref, w_ref, dmix_ref, dy_ref, dg_ref):
        i = pl.program_id(0)
        dmix, dg = _rms_bwd(mix_ref[...], g_ref[...], dh_ref[...])
        dmix = dmix.astype(BF16)
        dmix_ref[...] = dmix
        dy_ref[...] = _dot_nt(dmix, w_ref[...])
        _accumulate(dg_ref, dg, i == 0)

    return _hosting_call(
        exchange, body, name="out_proj_bwd", grid=(n // tr,),
        in_specs=[spec, spec, g.spec, VMEM_SPEC],
        out_specs=[spec, spec, pl.BlockSpec((1, d), lambda i: (0, 0))],
        out_shape=[_sds((n, d), BF16), _sds((n, d), F32), _sds((1, d), F32)],
        scratch_shapes=[], compiler_params=_params(("arbitrary",), 44),
    )(dh_mid, mix, g.array, w_out)


FF_SLAB = D_FF // N_DEV


def _relu_squared(up):
    return jnp.square(jnp.maximum(up.astype(F32), 0.0)).astype(BF16)


def _mlp_fwd(h_mid, g_pre, w_up, w_down, g_post, exchange=None):
    n, d = h_mid.shape
    tr = _row_tile(n, 384)
    spec = pl.BlockSpec((tr, d), lambda i: (i, 0))

    def body(h_ref, gpre_ref, wup_ref, wdn_ref, gpost_ref, hn_ref, up_ref, ff_ref, hout_ref):
        h = h_ref[...]
        hn = _rms_fwd(h, gpre_ref[...]).astype(BF16)
        hn_ref[...] = hn
        ff = jnp.zeros((tr, d), F32)
        for j in range(N_DEV):
            cs = slice(j * FF_SLAB, (j + 1) * FF_SLAB)
            up = jnp.dot(hn, wup_ref[j], preferred_element_type=F32).astype(BF16)
            up_ref[:, cs] = up
            ff = ff + jnp.dot(_relu_squared(up), wdn_ref[cs, :], preferred_element_type=F32)
        ff_ref[...] = ff
        hout_ref[...] = h + _rms_fwd(ff, gpost_ref[...])

    return _hosting_call(
        exchange, body, name="mlp_fwd", grid=(n // tr,),
        in_specs=[spec, g_pre.spec, VMEM_SPEC, VMEM_SPEC, g_post.spec],
        out_specs=[spec, pl.BlockSpec((tr, D_FF), lambda i: (i, 0)), spec, spec],
        out_shape=[_sds((n, d), BF16), _sds((n, D_FF), BF16), _sds((n, d), F32), _sds((n, d), F32)],
        scratch_shapes=[], compiler_params=_params(("arbitrary",), 52),
    )(h_mid, g_pre.array, w_up, w_down, g_post.array)


def _mlp_bwd(dh, ff, up, h_mid, g_pre, w_up, w_down, g_post, exchange=None):
    n, d = h_mid.shape
    tr = _row_tile(n, 384)
    spec = pl.BlockSpec((tr, d), lambda i: (i, 0))
    wide = pl.BlockSpec((tr, D_FF), lambda i: (i, 0))
    gspec = pl.BlockSpec((1, d), lambda i: (0, 0))

    def body(dh_ref, ff_ref, up_ref, h_ref, gpre_ref, wup_ref, wdn_ref, gpost_ref,
             dff_ref, dup_ref, dhmid_ref, dgpost_ref, dgpre_ref):
        i = pl.program_id(0)
        dh = dh_ref[...]
        dff, dgpost = _rms_bwd(ff_ref[...], gpost_ref[...], dh)
        dff = dff.astype(BF16)
        dff_ref[...] = dff
        dhn = jnp.zeros((tr, d), F32)
        for j in range(N_DEV):
            cs = slice(j * FF_SLAB, (j + 1) * FF_SLAB)
            relu = jnp.maximum(up_ref[:, cs].astype(F32), 0.0)
            dact = _dot_nt(dff, wdn_ref[cs, :])
            dup = (dact * 2.0 * relu).astype(BF16)
            dup_ref[:, cs] = dup
            dhn = dhn + _dot_nt(dup, wup_ref[j])
        dx, dgpre = _rms_bwd(h_ref[...], gpre_ref[...], dhn)
        dhmid_ref[...] = dh + dx
        _accumulate(dgpost_ref, dgpost, i == 0)
        _accumulate(dgpre_ref, dgpre, i == 0)

    return _hosting_call(
        exchange, body, name="mlp_bwd", grid=(n // tr,),
        in_specs=[spec, spec, wide, spec, g_pre.spec, VMEM_SPEC, VMEM_SPEC, g_post.spec],
        out_specs=[spec, wide, spec, gspec, gspec],
        out_shape=[_sds((n, d), BF16), _sds((n, D_FF), BF16), _sds((n, d), F32), _sds((1, d), F32), _sds((1, d), F32)],
        scratch_shapes=[], compiler_params=_params(("arbitrary",), 56),
    )(dh, ff, up, h_mid, g_pre.array, w_up, w_down, g_post.array)


def _in_proj_bwd(pieces, w_in, h, g, dh_mid):
    dxbr, dgate, dqk_f, dqk_b, dv_f, dv_b, dgout, dzg_f, dzg_b = pieces
    n, d = h.shape
    tr = _row_tile(n, 384)
    spec = pl.BlockSpec((tr, d), lambda i: (i, 0))
    s512 = pl.BlockSpec((tr, 512), lambda i: (i, 0))
    s128 = pl.BlockSpec((tr, LANES), lambda i: (i, 0))

    def body(a_ref, b_ref, cf_ref, cb_ref, df_ref, db_ref, e_ref, ff_ref, fb_ref, w_ref, h_ref, g_ref, dhm_ref,
             dz_ref, dh_ref, dg_ref):
        i = pl.program_id(0)
        real = (_row_ids(tr, i) >= PAD_ROWS).astype(F32)
        f32 = lambda ref: ref[...].astype(F32)
        dz = jnp.concatenate([f32(a_ref), f32(b_ref), f32(cf_ref) + f32(cb_ref), f32(df_ref) + f32(db_ref),
                              f32(e_ref), f32(ff_ref) + f32(fb_ref)], axis=1) * real
        dz = dz.astype(BF16)
        dz_ref[...] = dz
        dhn = _dot_nt(dz, w_ref[...])
        dx, dg = _rms_bwd(h_ref[...], g_ref[...], dhn)
        dh_ref[...] = (dhm_ref[...] + dx) * real
        _accumulate(dg_ref, dg, i == 0)

    return pl.pallas_call(
        body, name="in_proj_bwd", grid=(n // tr,),
        in_specs=[s512, s512, s512, s512, s512, s512, s512, s128, s128, VMEM_SPEC, spec, g.spec, spec],
        out_specs=[pl.BlockSpec((tr, Z_W), lambda i: (i, 0)), spec, pl.BlockSpec((1, d), lambda i: (0, 0))],
        out_shape=[_sds((n, Z_W), BF16), _sds((n, d), F32), _sds((1, d), F32)],
        compiler_params=_params(("arbitrary",), 48),
    )(dxbr, dgate, dqk_f, dqk_b, dv_f, dv_b, dgout, dzg_f, dzg_b, w_in, h, g.array, dh_mid)


def _matmul_tn(a, b, name, column_slabs=False, exchange=None, a_map=None):
    n, m = a.shape
    k = b.shape[1]
    tr = next(t for t in (2816, 1408, 768, 512, 256) if n % t == 0)
    tm, tk = _col_tile(m), _col_tile(k)
    steps = n // tr
    slab = k // N_DEV
    per_step = tk // slab if column_slabs else 1
    sub = next(t for t in (704, 768, 512, 256) if tr % t == 0)

    def body(a_ref, b_ref, o_ref, acc_ref, *mapped_ref):
        r = pl.program_id(2)
        if a_map is None:
            a_blk = a_ref[...]
        else:
            for c in range(tr // sub):
                rows = pl.ds(c * sub, sub)
                mapped_ref[0][rows, :] = a_map(a_ref[rows, :])
            a_blk = mapped_ref[0][...]
        _accumulate(acc_ref, _dot_tn(a_blk, b_ref[...]), r == 0)

        @pl.when(r == steps - 1)
        def _():
            if column_slabs:
                for j in range(per_step):
                    o_ref[j] = acc_ref[:, j * slab:(j + 1) * slab].astype(BF16)
            else:
                o_ref[...] = acc_ref[...].astype(BF16)

    if column_slabs:
        out_spec = pl.BlockSpec((per_step, tm, slab), lambda mi, ki, r: (ki, mi, 0))
        out_shape = _sds((N_DEV, m, slab), BF16)
    else:
        out_spec = pl.BlockSpec((tm, tk), lambda mi, ki, r: (mi, ki))
        out_shape = _sds((m, k), BF16)
    outs = _hosting_call(
        exchange, body, name=name, grid=(m // tm, k // tk, steps),
        in_specs=[pl.BlockSpec((tr, tm), lambda mi, ki, r: (r, mi)), pl.BlockSpec((tr, tk), lambda mi, ki, r: (r, ki))],
        out_specs=[out_spec], out_shape=[out_shape],
        scratch_shapes=[pltpu.VMEM((tm, tk), F32)] + ([] if a_map is None else [pltpu.VMEM((tr, tm), BF16)]),
        compiler_params=_params(("arbitrary", "arbitrary", "arbitrary"), 52),
    )(a, b)
    return outs[0] if exchange is None else outs


def _loss_and_grad(h_out, target):
    n, d = h_out.shape
    tr = ROW_BLOCK
    first = (PAD_ROWS + N_META) // tr

    def body(h_ref, t_ref, dh_ref, loss_ref):
        i = pl.program_id(0)
        real = jnp.where(i >= first, 1.0, 0.0)
        diff = (h_ref[...] - t_ref[...]) * real
        dh_ref[...] = diff * (1.0 / d)
        part = 0.5 * jnp.sum(jnp.mean(diff * diff, axis=-1, keepdims=True), axis=0, keepdims=True)
        _accumulate(loss_ref, jnp.broadcast_to(part, (1, LANES)), i == 0)

    return pl.pallas_call(
        body, name="loss_and_grad", grid=(n // tr,),
        in_specs=[pl.BlockSpec((tr, d), lambda i: (i, 0)), pl.BlockSpec((tr, d), lambda i: (jnp.maximum(i - first, 0), 0))],
        out_specs=[pl.BlockSpec((tr, d), lambda i: (i, 0)), pl.BlockSpec((1, LANES), lambda i: (0, 0))],
        out_shape=[_sds((n, d), F32), _sds((1, LANES), F32)],
        compiler_params=_params(("arbitrary",)),
    )(h_out, target)


SUBLANES = 8


def _scan_rows(a, u, reverse, window=None):
    n = a.shape[0]
    window = window or n
    pos = lax.broadcasted_iota(jnp.int32, (n, 1), 0) & (window - 1) if window < n else lax.broadcasted_iota(jnp.int32, (n, 1), 0)
    d = 1
    while d < window:
        shift = n - d if reverse else d
        keep = (pos < window - d) if reverse else (pos >= d)
        a_s = pltpu.roll(a, shift, 0)
        u_s = pltpu.roll(u, shift, 0)
        u = jnp.where(keep, a * u_s + u, u)
        a = jnp.where(keep, a * a_s, a)
        d *= 2
    return a, u


def _scan_block(a, u, h_in, reverse, stage_ref):
    n, width = a.shape
    groups = n // SUBLANES
    lanes = [slice(cb * LANES, (cb + 1) * LANES) for cb in range(width // LANES)]
    a1, u1 = _scan_rows(a, u, reverse, window=SUBLANES)
    for cb, cs in enumerate(lanes):
        stage_ref[0, cb] = a1[:, cs]
        stage_ref[1, cb] = u1[:, cs]
    edge = 0 if reverse else SUBLANES - 1
    group_rows = pl.ds(edge, groups, stride=SUBLANES)
    a2, u2 = _scan_rows(jnp.concatenate([stage_ref[0, cb, group_rows, :] for cb in range(len(lanes))], axis=1),
                        jnp.concatenate([stage_ref[1, cb, group_rows, :] for cb in range(len(lanes))], axis=1), reverse)
    leaving = a2 * h_in + u2
    grow = lax.broadcasted_iota(jnp.int32, (groups, 1), 0)
    if reverse:
        entering = jnp.where(grow == groups - 1, h_in, pltpu.roll(leaving, groups - 1, 0))
    else:
        entering = jnp.where(grow == 0, h_in, pltpu.roll(leaving, 1, 0))
    for cb, cs in enumerate(lanes):
        for k in range(SUBLANES):
            stage_ref[0, cb, pl.ds(k, groups, stride=SUBLANES), :] = entering[:, cs]
    entering_rows = jnp.concatenate([stage_ref[0, cb] for cb in range(len(lanes))], axis=1)
    return a1 * entering_rows + u1


def _lru_gates(xc, wcat_ref, bias_ref, lam_ref):
    nl = -lam_ref[...]
    nsp = -LRU_C * (jnp.maximum(nl, 0.0) + jnp.log(1.0 + jnp.exp(-jnp.abs(nl))))
    pre = _dot(xc, wcat_ref[...]) + bias_ref[...]
    r = _sigmoid(pre[:, :LRU_W])
    ig = _sigmoid(pre[:, LRU_W:])
    log_a = r * nsp
    a = jnp.exp(log_a)
    m2 = _one_minus_square(a, log_a)
    inv_m = lax.rsqrt(jnp.maximum(m2, 1e-30))
    return r, ig, a, m2 * inv_m, inv_m, nsp


def _lru_scan(xc, wcat, bias, lam, reverse, exchange=None):
    n = xc.shape[0]
    nb = n // ROW_BLOCK
    order = (lambda i: nb - 1 - i) if reverse else (lambda i: i)
    spec = pl.BlockSpec((ROW_BLOCK, LRU_W), lambda i: (order(i), 0))
    edge = 0 if reverse else ROW_BLOCK - 1

    def body(xc_ref, wcat_ref, bias_ref, lam_ref, h_ref, carry_ref, stage_ref):
        i = pl.program_id(0)

        @pl.when(i == 0)
        def _():
            carry_ref[...] = jnp.zeros_like(carry_ref)

        xc = xc_ref[...]
        r, ig, a, m, _, _ = _lru_gates(xc, wcat_ref, bias_ref, lam_ref)
        u = jnp.where(_row_ids(ROW_BLOCK, order(i)) >= PAD_ROWS, m * (ig * xc), 0.0)
        h_ref[...] = _scan_block(a, u, carry_ref[0:1, :], reverse, stage_ref)
        carry_ref[0:1, :] = h_ref[pl.ds(edge, 1), :]

    return _hosting_call(
        exchange, body, name="lru_scan_b" if reverse else "lru_scan_f", grid=(nb,),
        in_specs=[spec, wcat.spec, bias.spec, lam.spec],
        out_specs=[spec],
        out_shape=[_sds((n, LRU_W), F32)],
        scratch_shapes=[pltpu.VMEM((8, LRU_W), F32), pltpu.VMEM((2, LRU_W // LANES, ROW_BLOCK, LANES), F32)],
        compiler_params=_params(("arbitrary",)),
    )(xc, wcat.array, bias.array, lam.array)


def _lru_scan_bwd(dhs, xc, h, wcat, bias, lam, reverse, exchange=None):
    n = xc.shape[0]
    nb = n // ROW_BLOCK
    per = ROW_BLOCK // 8
    order = (lambda i: i) if reverse else (lambda i: nb - 1 - i)
    spec = pl.BlockSpec((ROW_BLOCK, LRU_W), lambda i: (order(i), 0))
    if reverse:
        halo = pl.BlockSpec((8, LRU_W), lambda i: (jnp.minimum((order(i) + 1) * per, nb * per - 1), 0))
    else:
        halo = pl.BlockSpec((8, LRU_W), lambda i: (jnp.maximum(order(i) * per - 1, 0), 0))
    edge = ROW_BLOCK - 1 if reverse else 0

    def body(dhs_ref, xc_ref, h_ref, halo_ref, wcat_ref, bias_ref, lam_ref,
             dxc_ref, dw_ref, db_ref, dlam_ref, cdh_ref, ca_ref, tmp_ref, stage_ref):
        i = pl.program_id(0)
        ib = order(i)

        @pl.when(i == 0)
        def _():
            cdh_ref[...] = jnp.zeros_like(cdh_ref)
            ca_ref[...] = jnp.zeros_like(ca_ref)

        xc = xc_ref[...]
        r, ig, a, m, inv_m, nsp = _lru_gates(xc, wcat_ref, bias_ref, lam_ref)
        row = lax.broadcasted_iota(jnp.int32, (ROW_BLOCK, 1), 0)
        if reverse:
            coef = jnp.where(row == 0, ca_ref[0:1, :], pltpu.roll(a, 1, 0))
            h_nb = jnp.where(row == ROW_BLOCK - 1, halo_ref[0:1, :] * jnp.where(ib < nb - 1, 1.0, 0.0),
                             pltpu.roll(h_ref[...], ROW_BLOCK - 1, 0))
        else:
            coef = jnp.where(row == ROW_BLOCK - 1, ca_ref[0:1, :], pltpu.roll(a, ROW_BLOCK - 1, 0))
            h_nb = jnp.where(row == 0, halo_ref[7:8, :] * jnp.where(ib > 0, 1.0, 0.0), pltpu.roll(h_ref[...], 1, 0))
        dh = _scan_block(coef, dhs_ref[...], cdh_ref[0:1, :], not reverse, stage_ref)
        tmp_ref[...] = dh
        cdh_ref[0:1, :] = tmp_ref[pl.ds(edge, 1), :]
        tmp_ref[...] = a
        ca_ref[0:1, :] = tmp_ref[pl.ds(edge, 1), :]

        du = jnp.where(_row_ids(ROW_BLOCK, ib) >= PAD_ROWS, dh, 0.0)
        da = dh * h_nb
        dm = du * (ig * xc)
        di = du * (m * xc)
        dlog_a = da * a - dm * (a * a) * inv_m
        dr = dlog_a * nsp
        dpre = jnp.concatenate([dr * r * (1.0 - r), di * ig * (1.0 - ig)], axis=1)
        dxc_ref[...] = du * (m * ig) + _dot_nt(dpre, wcat_ref[...])
        _accumulate(dw_ref, _dot_tn(xc, dpre), i == 0)
        _accumulate(db_ref, jnp.sum(dpre, axis=0, keepdims=True), i == 0)
        _accumulate(dlam_ref, jnp.sum(dlog_a * r, axis=0, keepdims=True), i == 0)

        @pl.when(i == nb - 1)
        def _():
            dlam_ref[...] = dlam_ref[...] * (LRU_C * _sigmoid(-lam_ref[...]))

    return _hosting_call(
        exchange, body, name="lru_scan_bwd_b" if reverse else "lru_scan_bwd_f", grid=(nb,),
        in_specs=[spec, spec, spec, halo, wcat.spec, bias.spec, lam.spec],
        out_specs=[spec, pl.BlockSpec((LRU_W, 2 * LRU_W), lambda i: (0, 0)),
                   pl.BlockSpec((1, 2 * LRU_W), lambda i: (0, 0)), pl.BlockSpec((1, LRU_W), lambda i: (0, 0))],
        out_shape=[_sds((n, LRU_W), F32), _sds((LRU_W, 2 * LRU_W), F32), _sds((1, 2 * LRU_W), F32), _sds((1, LRU_W), F32)],
        scratch_shapes=[pltpu.VMEM((8, LRU_W), F32), pltpu.VMEM((8, LRU_W), F32), pltpu.VMEM((ROW_BLOCK, LRU_W), F32),
                        pltpu.VMEM((2, LRU_W // LANES, ROW_BLOCK, LANES), F32)],
        compiler_params=_params(("arbitrary",)),
    )(dhs, xc, h, h, wcat.array, bias.array, lam.array)


def _gla_rows(n):
    return 768 if n % 768 == 0 else ROW_BLOCK


def _gla_masks(reverse):
    t = lax.broadcasted_iota(jnp.int32, (CHUNK, CHUNK), 0)
    s = lax.broadcasted_iota(jnp.int32, (CHUNK, CHUNK), 1)
    if reverse:
        return (s >= t).astype(F32), s > t
    return (s <= t).astype(F32), s <= t


def _gla_gate(zg, wg_ref, bg_ref):
    pre = _dot(zg, wg_ref[...]) + bg_ref[...]
    g = (jnp.minimum(pre, 0.0) - jnp.log(1.0 + jnp.exp(-jnp.abs(pre)))) * (1.0 / GATE_NORM)
    return pre, g


def _gla_decays(gc, tri):
    b = jnp.dot(tri, gc, precision=lax.Precision.HIGHEST, preferred_element_type=F32)
    b_last = jnp.sum(gc, axis=0, keepdims=True)
    return jnp.exp(b), jnp.exp(-b), jnp.exp(b_last - b), jnp.exp(b_last)


def _gla_scan(z, wg, bg, reverse, exchange=None):
    n = z.shape[0]
    rb = _gla_rows(n)
    nb = n // rb
    cpb = rb // CHUNK
    order = (lambda i: nb - 1 - i) if reverse else (lambda i: i)
    chunks = range(cpb - 1, -1, -1) if reverse else range(cpb)

    def body(qk_ref, v_ref, zg_ref, wg_ref, bg_ref, o_ref, sall_ref, s_ref):
        i = pl.program_id(0)

        @pl.when(i == 0)
        def _():
            s_ref[...] = jnp.zeros_like(s_ref)

        tri, mask = _gla_masks(reverse)
        _, g = _gla_gate(zg_ref[...], wg_ref, bg_ref)
        heads = range(GLA_HEADS)
        ks = [slice(hd * GLA_DK, (hd + 1) * GLA_DK) for hd in heads]
        vs = [slice(hd * GLA_DV, (hd + 1) * GLA_DV) for hd in heads]
        qh, kb, v, el, p, intra, kv = {}, {}, {}, {}, {}, {}, {}
        for c in chunks:
            rows = slice(c * CHUNK, (c + 1) * CHUNK)
            eb, enb, ebl, el[c] = _gla_decays(g[rows], tri)
            qk = qk_ref[rows, :]
            q_all = (qk[:, :GLA_QK] * (GLA_DK ** -0.5) * eb).astype(BF16)
            k_all = (qk[:, GLA_QK:] * enb).astype(BF16)
            kb_all = (qk[:, GLA_QK:] * ebl).astype(BF16)
            v_all = v_ref[rows, :].astype(BF16)
            for hd in heads:
                qh[c, hd], kb[c, hd], v[c, hd] = q_all[:, ks[hd]], kb_all[:, ks[hd]], v_all[:, vs[hd]]
                p[c, hd] = _dot_nt(qh[c, hd], k_all[:, ks[hd]])
        for c in chunks:
            for hd in heads:
                intra[c, hd] = _dot(jnp.where(mask, p[c, hd], 0.0), v[c, hd])
                kv[c, hd] = _dot_tn(v[c, hd], kb[c, hd])
        state = [s_ref[:, ks[hd]] for hd in heads]
        for c in chunks:
            rows = slice(c * CHUNK, (c + 1) * CHUNK)
            for hd in heads:
                sall_ref[c, :, ks[hd]] = state[hd]
                o_ref[rows, vs[hd]] = intra[c, hd] + _dot_nt(qh[c, hd], state[hd])
                state[hd] = state[hd] * el[c][:, ks[hd]] + kv[c, hd]
        for hd in heads:
            s_ref[:, ks[hd]] = state[hd]

    return _hosting_call(
        exchange, body, name="gla_scan_b" if reverse else "gla_scan_f", grid=(nb,),
        in_specs=[pl.BlockSpec((rb, 512), lambda i: (order(i), 2)), pl.BlockSpec((rb, 512), lambda i: (order(i), 3)),
                  pl.BlockSpec((rb, LANES), lambda i: (order(i), ZG_COL_BLOCK)), wg.spec, bg.spec],
        out_specs=[pl.BlockSpec((rb, GLA_W), lambda i: (order(i), 0)),
                   pl.BlockSpec((cpb, GLA_DV, GLA_QK), lambda i: (order(i), 0, 0))],
        out_shape=[_sds((n, GLA_W), F32), _sds((n // CHUNK, GLA_DV, GLA_QK), F32)],
        scratch_shapes=[pltpu.VMEM((GLA_DV, GLA_QK), F32)],
        compiler_params=_params(("arbitrary",)),
    )(z, z, z, wg.array, bg.array)


def _gla_scan_bwd(do, z, states, wg, bg, reverse, exchange=None):
    n = z.shape[0]
    rb = _gla_rows(n)
    nb = n // rb
    cpb = rb // CHUNK
    order = (lambda i: i) if reverse else (lambda i: nb - 1 - i)
    chunks = range(cpb) if reverse else range(cpb - 1, -1, -1)

    def body(do_ref, qk_ref, v_ref, zg_ref, sall_ref, wg_ref, bg_ref,
             dqk_ref, dv_ref, dzg_ref, dwg_ref, dbg_ref, ds_ref):
        i = pl.program_id(0)

        @pl.when(i == 0)
        def _():
            ds_ref[...] = jnp.zeros_like(ds_ref)

        tri, mask = _gla_masks(reverse)
        tri_t, _ = _gla_masks(not reverse)
        zg = zg_ref[...]
        pre, g = _gla_gate(zg, wg_ref, bg_ref)
        heads = range(GLA_HEADS)
        ks = [slice(hd * GLA_DK, (hd + 1) * GLA_DK) for hd in heads]
        vs = [slice(hd * GLA_DV, (hd + 1) * GLA_DV) for hd in heads]
        dec, full, qh, kh, kb, v, dout, p, dp = {}, {}, {}, {}, {}, {}, {}, {}, {}
        for c in chunks:
            rows = slice(c * CHUNK, (c + 1) * CHUNK)
            dec[c] = _gla_decays(g[rows], tri)
            eb, enb, ebl, _ = dec[c]
            qk = qk_ref[rows, :]
            q_f = qk[:, :GLA_QK] * (GLA_DK ** -0.5) * eb
            k_f = qk[:, GLA_QK:] * enb
            kb_f = qk[:, GLA_QK:] * ebl
            full[c] = (q_f, k_f, kb_f)
            q_all, k_all, kb_all = q_f.astype(BF16), k_f.astype(BF16), kb_f.astype(BF16)
            v_all, do_all = v_ref[rows, :].astype(BF16), do_ref[rows, :].astype(BF16)
            for hd in heads:
                qh[c, hd], kh[c, hd], kb[c, hd] = q_all[:, ks[hd]], k_all[:, ks[hd]], kb_all[:, ks[hd]]
                v[c, hd], dout[c, hd] = v_all[:, vs[hd]], do_all[:, vs[hd]]
                p[c, hd] = _dot_nt(qh[c, hd], kh[c, hd])
                dp[c, hd] = _dot_nt(dout[c, hd], v[c, hd])
        dv_i, dqh, dkh, dsq, state = {}, {}, {}, {}, {}
        for c in chunks:
            for hd in heads:
                pm = jnp.where(mask, p[c, hd], 0.0).astype(BF16)
                dpm = jnp.where(mask, dp[c, hd], 0.0).astype(BF16)
                state[c, hd] = sall_ref[c, :, ks[hd]]
                dv_i[c, hd] = _dot_tn(pm, dout[c, hd])
                dqh[c, hd] = _dot(dpm, kh[c, hd]) + _dot(dout[c, hd], state[c, hd])
                dkh[c, hd] = _dot_tn(dpm, qh[c, hd])
                dsq[c, hd] = _dot_tn(dout[c, hd], qh[c, hd])
        dstate = [ds_ref[:, ks[hd]] for hd in heads]
        dkb, sds = {}, {}
        for c in chunks:
            rows = slice(c * CHUNK, (c + 1) * CHUNK)
            el = dec[c][3]
            for hd in heads:
                dv_ref[rows, vs[hd]] = (dv_i[c, hd] + _dot_nt(kb[c, hd], dstate[hd])).astype(BF16)
                dkb[c, hd] = _dot(v[c, hd], dstate[hd])
                sds[c, hd] = jnp.sum(state[c, hd] * dstate[hd], axis=0, keepdims=True)
                dstate[hd] = dstate[hd] * el[:, ks[hd]] + dsq[c, hd]
        for hd in heads:
            ds_ref[:, ks[hd]] = dstate[hd]
        dgs = [None] * cpb
        for c in chunks:
            rows = slice(c * CHUNK, (c + 1) * CHUNK)
            eb, enb, ebl, el = dec[c]
            q_f, k_f, kb_f = full[c]
            dqh_c = jnp.concatenate([dqh[c, hd] for hd in heads], axis=1)
            dkh_c = jnp.concatenate([dkh[c, hd] for hd in heads], axis=1)
            dkb_c = jnp.concatenate([dkb[c, hd] for hd in heads], axis=1)
            sds_c = jnp.concatenate([sds[c, hd] for hd in heads], axis=1)
            dqk_ref[rows, :] = jnp.concatenate([dqh_c * eb * (GLA_DK ** -0.5), dkh_c * enb + dkb_c * ebl], axis=1).astype(BF16)
            dkb_kb = dkb_c * kb_f
            db = dqh_c * q_f - dkh_c * k_f - dkb_kb
            db_last = el * sds_c + jnp.sum(dkb_kb, axis=0, keepdims=True)
            dgs[c] = jnp.dot(tri_t, db, precision=lax.Precision.HIGHEST, preferred_element_type=F32) + db_last
        dg = jnp.concatenate(dgs, axis=0)
        dpre = dg * _sigmoid(-pre) * (1.0 / GATE_NORM)
        dzg_ref[...] = _dot_nt(dpre, wg_ref[...]).astype(BF16)
        _accumulate(dwg_ref, _dot_tn(zg, dpre), i == 0)
        _accumulate(dbg_ref, jnp.sum(dpre, axis=0, keepdims=True), i == 0)

    return _hosting_call(
        exchange, body, name="gla_scan_bwd_b" if reverse else "gla_scan_bwd_f", grid=(nb,),
        in_specs=[pl.BlockSpec((rb, GLA_W), lambda i: (order(i), 0)),
                  pl.BlockSpec((rb, 512), lambda i: (order(i), 2)), pl.BlockSpec((rb, 512), lambda i: (order(i), 3)),
                  pl.BlockSpec((rb, LANES), lambda i: (order(i), ZG_COL_BLOCK)),
                  pl.BlockSpec((cpb, GLA_DV, GLA_QK), lambda i: (order(i), 0, 0)), wg.spec, bg.spec],
        out_specs=[pl.BlockSpec((rb, 512), lambda i: (order(i), 0)), pl.BlockSpec((rb, 512), lambda i: (order(i), 0)),
                   pl.BlockSpec((rb, LANES), lambda i: (order(i), 0)),
                   pl.BlockSpec((LANES, GLA_QK), lambda i: (0, 0)), pl.BlockSpec((1, GLA_QK), lambda i: (0, 0))],
        out_shape=[_sds((n, 512), BF16), _sds((n, 512), BF16), _sds((n, LANES), BF16), _sds((LANES, GLA_QK), F32),
                   _sds((1, GLA_QK), F32)],
        scratch_shapes=[pltpu.VMEM((GLA_DV, GLA_QK), F32)],
        compiler_params=_params(("arbitrary",)),
    )(do, z, z, z, states, wg.array, bg.array)


NORM_NAMES = ("norm_mix_pre", "norm_mix_post", "norm_mlp_pre", "norm_mlp_post")
VEC512_NAMES = ("conv_b", "lru_ba_f", "lru_bx_f", "lru_lambda_f", "lru_ba_b", "lru_bx_b", "lru_lambda_b", "gla_head_norm")
VEC256_NAMES = ("gla_bg_f", "gla_bg_b")
LRU_MAT_NAMES = ("lru_wa_f", "lru_wx_f", "lru_wa_b", "lru_wx_b")
DIRS = ("f", "b")


def _prepare_params(w, gathered, depth):
    row_names = NORM_NAMES + ("conv_b", "gla_head_norm")
    ins = ([w[nm] for nm in row_names] + [w["lru_ba_" + d] for d in DIRS] + [w["lru_bx_" + d] for d in DIRS]
           + [w["lru_lambda_" + d] for d in DIRS] + [w["gla_bg_" + d] for d in DIRS]
           + [w["lru_wa_" + d].reshape(depth, LRU_W, LRU_HD) for d in DIRS]
           + [w["lru_wx_" + d].reshape(depth, LRU_W, LRU_HD) for d in DIRS]
           + [gathered["conv_w"], gathered["gla_wg_f"], gathered["gla_wg_b"], gathered["meta_tokens"]])
    n_rows = len(row_names)

    def body(*refs):
        rows_in = refs[:n_rows]
        ba, bx, lam, bg, wa, wx = (refs[n_rows + 2 * t:n_rows + 2 * t + 2] for t in range(6))
        convw_g, wgf_g, wgb_g, meta_g = refs[n_rows + 12:n_rows + 16]
        outs = refs[n_rows + 16:]
        rows_out = outs[:n_rows]
        convw, wcat, bias, lam_o, wg, bg_o, meta = outs[n_rows:]
        for l in range(depth):
            for src, dst in zip(rows_in, rows_out):
                dst[l] = src[pl.ds(l, 1), :]
            convw[l] = jnp.zeros((8, LRU_W), F32)
            for j in range(N_DEV):
                convw[l, 0:4, j * 64:(j + 1) * 64] = convw_g[j, l]
            for d in range(2):
                wcat[l, d] = jnp.zeros((LRU_W, 2 * LRU_W), BF16)
                for hd in range(LRU_HEADS):
                    rs = slice(hd * LRU_HD, (hd + 1) * LRU_HD)
                    wcat[l, d, rs, hd * LRU_HD:(hd + 1) * LRU_HD] = wa[d][l, rs, :].astype(BF16)
                    wcat[l, d, rs, LRU_W + hd * LRU_HD:LRU_W + (hd + 1) * LRU_HD] = wx[d][l, rs, :].astype(BF16)
                bias[l, d, :, 0:LRU_W] = ba[d][pl.ds(l, 1), :]
                bias[l, d, :, LRU_W:2 * LRU_W] = bx[d][pl.ds(l, 1), :]
                lam_o[l, d] = lam[d][pl.ds(l, 1), :]
                bg_o[l, d] = bg[d][pl.ds(l, 1), :]
                wg[l, d] = jnp.zeros((LANES, GLA_QK), BF16)
                src = wgf_g if d == 0 else wgb_g
                for j in range(N_DEV):
                    wg[l, d, d * GLA_RANK:(d + 1) * GLA_RANK, j * 32:(j + 1) * 32] = src[j, l].astype(BF16)
        for j in range(N_DEV):
            meta[:, j * LANES:(j + 1) * LANES] = meta_g[j]

    out_shape = ([_sds((depth, 1, w[nm].shape[1]), F32) for nm in row_names]
                 + [_sds((depth, 8, LRU_W), F32), _sds((depth, 2, LRU_W, 2 * LRU_W), BF16), _sds((depth, 2, 1, 2 * LRU_W), F32),
                    _sds((depth, 2, 1, LRU_W), F32), _sds((depth, 2, LANES, GLA_QK), BF16), _sds((depth, 2, 1, GLA_QK), F32),
                    _sds((N_META, D_MODEL), F32)])
    outs = pl.pallas_call(
        body, name="prepare_params", in_specs=[VMEM_SPEC] * len(ins), out_specs=[VMEM_SPEC] * len(out_shape),
        out_shape=out_shape, compiler_params=_params(None, 32),
    )(*ins)
    prepared = dict(zip(row_names, outs[:n_rows]))
    prepared.update(zip(("conv_w", "wcat", "lru_bias", "lru_lam", "wg", "gla_bg", "meta_tokens"), outs[n_rows:]))
    return prepared


class _Outbox:
    def __init__(self, on_complete):
        self.pending, self.on_complete = {}, on_complete

    def put(self, key, array, src, landing_shape):
        self.pending[key] = dict(array=array, src=src, landing=landing_shape, groups=list(range(len(PEER_GROUPS))))

    def exchange(self, wanted=None):
        ex, tickets = _Exchange(), []
        for key, item in self.pending.items():
            groups = [g for g in item["groups"] if wanted is None or (key[0], g) in wanted]
            out = None
            for g in groups:
                landing = item["landing"] if out is None else out
                out = ex.add(item["array"], item["src"], landing, _slab, peers=PEER_GROUPS[g], local=(g == 0))
                item["groups"].remove(g)
            if groups:
                tickets.append((key, out))
        return ex, tickets

    def store(self, tickets, landed):
        for key, out in tickets:
            item = self.pending[key]
            item["landing"] = landed[out]
            if not item["groups"]:
                del self.pending[key]
                self.on_complete(key, landed[out])


def _install_weight(p):
    def install(key, g):
        nm, l = key
        if nm == "w_in":
            g = jnp.pad(jnp.concatenate([g[j] for j in range(N_DEV)], axis=1), ((0, 0), (0, Z_W - D_IN)))
        elif nm == "w_out":
            g = g.reshape(D_MODEL, D_MODEL)
        elif nm == "w_mlp_down":
            g = g.reshape(D_FF, D_MODEL)
        p.setdefault(nm, {})[l] = g
    return install


def _request_weight(gather, shards, nm, l):
    gather.put((nm, l), shards[nm], _layer_of(l), _sds((N_DEV,) + shards[nm].shape[1:], BF16))


def _layer_fwd(h, l, p, gather, shards, depth):
    lp = lambda name, *index: _LayerParam(p[name], l, *index)
    s = dict(h=h)

    def hosted(fn, wanted, *args):
        ex, tickets = gather.exchange(wanted)
        outs = fn(*args, ex)
        own = len(outs) - len(ex.landings)
        gather.store(tickets, outs[own:])
        return outs[:own]

    _request_weight(gather, shards, "w_mlp_up", l)
    _request_weight(gather, shards, "w_mlp_down", l)
    s["hn"], s["z"] = hosted(_norm_in_proj, [("w_mlp_up", 0)], h, lp("norm_mix_pre"), p["w_in"][l])
    s["xc"] = _conv_fwd(s["z"], lp("conv_w"), lp("conv_b"))
    plan = {"f": ([("w_mlp_up", 1)], [("w_mlp_up", 2)]), "b": ([("w_mlp_down", 0)], [("w_mlp_down", 1)])}
    for d, name in enumerate(DIRS):
        s["h_" + name], = hosted(_lru_scan, plan[name][0], s["xc"], lp("wcat", d), lp("lru_bias", d), lp("lru_lam", d), d == 1)
        s["o_" + name], s["s_" + name] = hosted(_gla_scan, plan[name][1], s["z"], lp("wg", d), lp("gla_bg", d), d == 1)
    s["ymix"] = _mix_epilogue(s["h_f"], s["h_b"], s["o_f"], s["o_b"], s["z"], lp("gla_head_norm"))
    s["mix"], s["h_mid"] = hosted(_out_proj, [("w_mlp_down", 2)], s["ymix"], p["w_out"][l], h, lp("norm_mix_post"))
    if l + 1 < depth:
        _request_weight(gather, shards, "w_in", l + 1)
        _request_weight(gather, shards, "w_out", l + 1)
    s["hn2"], s["up"], s["ff"], h_out = hosted(
        _mlp_fwd, None, s["h_mid"], lp("norm_mlp_pre"), p["w_mlp_up"][l], p["w_mlp_down"][l], lp("norm_mlp_post"))
    return h_out, s


def _layer_bwd(dh_out, l, p, s, outbox):
    lp = lambda name, *index: _LayerParam(p[name], l, *index)
    g = {}

    def hosted(fn, wanted, *args):
        ex, tickets = outbox.exchange(wanted)
        outs = fn(*args, ex)
        own = len(outs) - len(ex.landings)
        outbox.store(tickets, outs[own:])
        return outs[:own]

    d_ff, dup, dh_mid, g["norm_mlp_post"], g["norm_mlp_pre"] = hosted(
        _mlp_bwd, None, dh_out, s["ff"], s["up"], s["h_mid"], lp("norm_mlp_pre"), p["w_mlp_up"][l], p["w_mlp_down"][l],
        lp("norm_mlp_post"))
    _send_grad(outbox, "w_mlp_down", l, _matmul_tn(s["up"], d_ff, "grad_w_down", a_map=_relu_squared)
               .reshape(N_DEV, D_FF // N_DEV, D_MODEL))
    _send_grad(outbox, "w_mlp_up", l, _matmul_tn(s["hn2"], dup, "grad_w_up", column_slabs=True))
    dmix, dymix, g["norm_mix_post"] = hosted(_out_proj_bwd, [("w_mlp_down", 0)], dh_mid, s["mix"], lp("norm_mix_post"),
                                             p["w_out"][l])
    grad_w_out = _matmul_tn(s["ymix"], dmix, "grad_w_out").reshape(N_DEV, D_MODEL // N_DEV, D_MODEL)
    dhs, dgate, do, dgout, g["gla_head_norm"] = _mix_epilogue_bwd(
        dymix, s["h_f"], s["h_b"], s["o_f"], s["o_b"], s["z"], lp("gla_head_norm"))
    plan = {"f": ([("w_mlp_down", 1)], [("w_mlp_down", 2), ("w_mlp_up", 0)]), "b": ([("w_mlp_up", 1)], [("w_mlp_up", 2)])}
    dqk, dv, dzg, dxc = {}, {}, {}, {}
    for d, name in enumerate(DIRS):
        dqk[name], dv[name], dzg[name], g["wg_" + name], g["gla_bg_" + name] = hosted(
            _gla_scan_bwd, plan[name][0], do, s["z"], s["s_" + name], lp("wg", d), lp("gla_bg", d), d == 1)
        dxc[name], g["wcat_" + name], g["lru_bias_" + name], g["lru_lambda_" + name] = hosted(
            _lru_scan_bwd, plan[name][1], dhs, s["xc"], s["h_" + name], lp("wcat", d), lp("lru_bias", d), lp("lru_lam", d),
            d == 1)
    _send_grad(outbox, "w_out", l, grad_w_out)
    dxbr, g["conv_w"], g["conv_b"] = _conv_bwd(dxc["f"], dxc["b"], s["z"], lp("conv_w"))
    dz, dh_in, g["norm_mix_pre"] = _in_proj_bwd(
        (dxbr, dgate, dqk["f"], dqk["b"], dv["f"], dv["b"], dgout, dzg["f"], dzg["b"]),
        p["w_in"][l], s["h"], lp("norm_mix_pre"), dh_mid)
    return dh_in, g, dz


def _send_grad(outbox, nm, l, slabs):
    outbox.put((nm, l), slabs, _slab, _sds(slabs.shape, slabs.dtype))


def _w_in_slabs(grad_w_in):
    shard = D_IN // N_DEV
    return jnp.stack([grad_w_in[:, j * shard:(j + 1) * shard] for j in range(N_DEV)])


def _folded_block(hd):
    return slice((hd // 2) * LRU_HD, (hd // 2 + 1) * LRU_HD), slice((hd % 2) * LRU_HD, (hd % 2 + 1) * LRU_HD)


def _pack_small_grads(grads, dh0, depth):
    per_layer = ("norm_mix_pre", "norm_mix_post", "norm_mlp_pre", "norm_mlp_post", "conv_b", "gla_head_norm",
                 "lru_bias_f", "lru_bias_b", "lru_lambda_f", "lru_lambda_b", "gla_bg_f", "gla_bg_b",
                 "wcat_f", "wcat_b", "conv_w", "wg_f", "wg_b")
    ins = [grads[l][nm] for l in range(depth) for nm in per_layer]
    k = len(per_layer)
    meta_rows = PAD_ROWS // N_META

    def body(*refs):
        g = [dict(zip(per_layer, refs[l * k:(l + 1) * k])) for l in range(depth)]
        dh0_ref = refs[depth * k]
        norms, v512, v256, mats, convw, wgf, wgb, meta = refs[depth * k + 1:]
        v256[...] = jnp.zeros_like(v256)
        for l in range(depth):
            for p_, nm in enumerate(NORM_NAMES):
                norms[pl.ds(2 * p_ + l, 1), :] = g[l][nm][...]
            rows512 = [g[l]["conv_b"][...], g[l]["lru_bias_f"][:, 0:LRU_W], g[l]["lru_bias_f"][:, LRU_W:2 * LRU_W],
                       g[l]["lru_lambda_f"][...], g[l]["lru_bias_b"][:, 0:LRU_W], g[l]["lru_bias_b"][:, LRU_W:2 * LRU_W],
                       g[l]["lru_lambda_b"][...], g[l]["gla_head_norm"][...]]
            for p_, row in enumerate(rows512):
                v512[pl.ds(2 * p_ + l, 1), :] = row
            for p_, nm in enumerate(("gla_bg_f", "gla_bg_b")):
                v256[pl.ds(2 * p_ + l, 1), :] = g[l][nm][...]
            for d, name in enumerate(DIRS):
                for hd in range(LRU_HEADS):
                    rs = slice(hd * LRU_HD, (hd + 1) * LRU_HD)
                    dst_rows, dst_cols = _folded_block(hd)
                    mats[2 * d, l, dst_rows, dst_cols] = g[l]["wcat_" + name][rs, hd * LRU_HD:(hd + 1) * LRU_HD].astype(BF16)
                    mats[2 * d + 1, l, dst_rows, dst_cols] = (
                        g[l]["wcat_" + name][rs, LRU_W + hd * LRU_HD:LRU_W + (hd + 1) * LRU_HD].astype(BF16))
            for j in range(N_DEV):
                convw[j, l] = g[l]["conv_w"][0:4, j * 64:(j + 1) * 64]
                wgf[j, l] = g[l]["wg_f"][0:GLA_RANK, j * 32:(j + 1) * 32]
                wgb[j, l] = g[l]["wg_b"][GLA_RANK:2 * GLA_RANK, j * 32:(j + 1) * 32]
        for j in range(N_DEV):
            meta[j] = dh0_ref[:, j * LANES:(j + 1) * LANES]

    out_shape = [_sds((8, D_MODEL), F32), _sds((16, LRU_W), F32), _sds((8, GLA_QK), F32),
                 _sds((4, depth, LRU_W // 2, 2 * LRU_HD), BF16),
                 _sds((N_DEV, depth, 4, 64), F32), _sds((N_DEV, depth, GLA_RANK, 32), F32), _sds((N_DEV, depth, GLA_RANK, 32), F32),
                 _sds((N_DEV, N_META, LANES), F32)]
    return pl.pallas_call(
        body, name="pack_small_grads", grid=(1,),
        in_specs=[VMEM_SPEC] * (depth * k) + [pl.BlockSpec((N_META, D_MODEL), lambda i: (meta_rows, 0))],
        out_specs=[VMEM_SPEC] * len(out_shape), out_shape=out_shape, compiler_params=_params(("arbitrary",), 32),
    )(*ins, dh0)


def _my_index():
    return 4 * lax.axis_index("x") + 2 * lax.axis_index("y") + lax.axis_index("c")


def _peer(k):
    x, y, c = lax.axis_index("x"), lax.axis_index("y"), lax.axis_index("c")
    px = x ^ ((k >> 2) & 1)
    py = y ^ ((k >> 1) & 1)
    pc = c ^ (k & 1)
    return (px, py, pc), 4 * px + 2 * py + pc


ALL_PEERS = tuple(range(1, N_DEV))
PEER_GROUPS = ((1, 2, 3), (4, 5), (6, 7))


class _Exchange:
    def __init__(self):
        self.inputs, self.landings, self.transfers = [], [], []

    def add(self, array, src, landing, dst, peers=ALL_PEERS, local=True):
        if isinstance(landing, int):
            out = landing
        else:
            out = len(self.landings)
            self.landings.append(landing)
        self.transfers.append((len(self.inputs), src, out, dst, tuple(peers), local))
        self.inputs.append(array)
        return out

    def _pairs(self):
        return [(t, k) for t, tr in enumerate(self.transfers) for k in tr[4]]

    def _locals(self):
        return [t for t, tr in enumerate(self.transfers) if tr[5]]

    def out_shapes(self):
        return [g if isinstance(g, jax.ShapeDtypeStruct) else _sds(g.shape, g.dtype) for g in self.landings]

    def continued(self):
        return [(b, g) for b, g in enumerate(self.landings) if not isinstance(g, jax.ShapeDtypeStruct)]

    def sem_shapes(self):
        return [pltpu.SemaphoreType.DMA((max(len(self._pairs()), 1),)), pltpu.SemaphoreType.DMA((max(len(self._pairs()), 1),)),
                pltpu.SemaphoreType.DMA((max(len(self._locals()), 1),))]

    def _local(self, ins, outs, sems):
        me = _my_index()
        copies = []
        for s, t in enumerate(self._locals()):
            a, src, b, dst, _, _ = self.transfers[t]
            copies.append(pltpu.make_async_copy(src(ins[a], me), dst(outs[b], me), sems[2].at[s]))
        return copies

    def _remote(self, ins, outs, sems, sending):
        copies = []
        for s, (t, k) in enumerate(self._pairs()):
            a, src, b, dst, _, _ = self.transfers[t]
            peer, peer_index = _peer(k)
            copies.append(pltpu.make_async_remote_copy(
                src_ref=src(ins[a], peer_index), dst_ref=dst(outs[b], _my_index() if sending else peer_index),
                send_sem=sems[0].at[s], recv_sem=sems[1].at[s], device_id=peer, device_id_type=MESH_ID))
        return copies

    def start(self, ins, outs, sems):
        for cp in self._local(ins, outs, sems) + self._remote(ins, outs, sems, True):
            cp.start()

    def wait(self, ins, outs, sems):
        for cp in self._remote(ins, outs, sems, False):
            cp.wait_recv()
        for cp in self._remote(ins, outs, sems, True):
            cp.wait_send()
        for cp in self._local(ins, outs, sems):
            cp.wait()

    def run(self, name):
        return _hosting_call(self, None, name=name, grid=(), in_specs=[], out_specs=[], out_shape=[], scratch_shapes=[],
                             compiler_params=pltpu.CompilerParams(has_side_effects=True))()


def _hosting_call(exchange, body, *, name, grid, in_specs, out_specs, out_shape, scratch_shapes, compiler_params):
    if exchange is None or not exchange.transfers:
        return pl.pallas_call(body, name=name, grid=grid, in_specs=in_specs, out_specs=out_specs, out_shape=out_shape,
                              scratch_shapes=scratch_shapes, compiler_params=compiler_params)
    n_in, n_out, n_scr = len(in_specs), len(out_specs), len(scratch_shapes)
    x_in, x_out = len(exchange.inputs), len(exchange.landings)
    continued = exchange.continued()

    def hosted(*refs):
        ins, x_ins = refs[:n_in], refs[n_in:n_in + x_in]
        o0 = n_in + x_in + len(continued)
        outs, x_outs = refs[o0:o0 + n_out], refs[o0 + n_out:o0 + n_out + x_out]
        s0 = o0 + n_out + x_out
        scratch, sems = refs[s0:s0 + n_scr], refs[s0 + n_scr:]
        if body is None:
            exchange.start(x_ins, x_outs, sems)
            exchange.wait(x_ins, x_outs, sems)
            return
        ids = [pl.program_id(a) for a in range(len(grid))]
        first = functools.reduce(jnp.logical_and, [i == 0 for i in ids])
        last = functools.reduce(jnp.logical_and, [i == g - 1 for i, g in zip(ids, grid)])

        @pl.when(first)
        def _():
            exchange.start(x_ins, x_outs, sems)

        body(*ins, *outs, *scratch)

        @pl.when(last)
        def _():
            exchange.wait(x_ins, x_outs, sems)

    aliases = {n_in + x_in + i: n_out + b for i, (b, _) in enumerate(continued)}
    kwargs = dict(grid=grid) if grid else {}
    call = pl.pallas_call(
        hosted, name=name, in_specs=list(in_specs) + [ANY_SPEC] * (x_in + len(continued)),
        out_specs=list(out_specs) + [ANY_SPEC] * x_out, out_shape=list(out_shape) + exchange.out_shapes(),
        scratch_shapes=list(scratch_shapes) + exchange.sem_shapes(), compiler_params=compiler_params,
        input_output_aliases=aliases, **kwargs)
    return lambda *operands: call(*operands, *exchange.inputs, *[g for _, g in continued])


def _whole(ref, j):
    return ref


def _slab(ref, j):
    return ref.at[j]


def _layer_of(l):
    return lambda ref, j: ref.at[l]


def _slab_layer(l):
    return lambda ref, j: ref.at[j, l]


def _adamw(g, w, m, v):
    nm = ADAM_B1 * m + (1.0 - ADAM_B1) * g
    nv = ADAM_B2 * v + (1.0 - ADAM_B2) * jnp.square(g)
    m_hat = nm / (1.0 - ADAM_B1 ** ADAM_STEP)
    v_hat = nv / (1.0 - ADAM_B2 ** ADAM_STEP)
    return -ADAM_LR * (m_hat / (jnp.sqrt(v_hat) + ADAM_EPS) + ADAM_WD * w), nm, nv


def _sum_parts(p_ref):
    g = p_ref[0].astype(F32)
    for j in range(1, N_DEV):
        g = g + p_ref[j].astype(F32)
    return g


def _adamw_sharded(parts, w, m, v, name):
    shape = w.shape
    lead, (rows, cols) = shape[:-2], shape[-2:]
    tr = min(rows, ROW_BLOCK)
    assert rows % tr == 0
    steps = rows // tr
    nl = len(lead)
    spec = pl.BlockSpec((None,) * nl + (tr, cols), lambda *idx: idx + (0,))
    per_layer = isinstance(parts, (list, tuple))
    if per_layer:
        def part_spec(l):
            return pl.BlockSpec((N_DEV, tr, cols), lambda li, r: (0, jnp.where(li == l, r, jnp.where(li < l, 0, steps - 1)), 0))
        part_specs = [part_spec(l) for l in range(len(parts))]
    else:
        parts = [parts]
        part_specs = [pl.BlockSpec((N_DEV,) + (None,) * nl + (tr, cols), lambda *idx: (0,) + idx + (0,))]
    count = len(parts)

    def body(*refs):
        p_refs = refs[:count]
        w_ref, m_ref, v_ref, g_ref, d_ref, nm_ref, nv_ref = refs[count:]

        def update(p_ref):
            g = _sum_parts(p_ref)
            g_ref[...] = g
            d_ref[...], nm_ref[...], nv_ref[...] = _adamw(g, w_ref[...], m_ref[...], v_ref[...])

        if per_layer:
            for l in range(count):
                pl.when(pl.program_id(0) == l)(functools.partial(update, p_refs[l]))
        else:
            update(p_refs[0])

    return pl.pallas_call(
        body, name=name, grid=lead + (steps,),
        in_specs=part_specs + [spec, spec, spec], out_specs=[spec] * 4, out_shape=[_sds(shape, F32)] * 4,
        compiler_params=_params(("arbitrary",) * (nl + 1)),
    )(*parts, w, m, v)


def _adamw_replicated(gathered, w, m, v, depth):
    names = NORM_NAMES + VEC512_NAMES + VEC256_NAMES + LRU_MAT_NAMES
    count = len(names)

    def body(*refs):
        norms, v512, v256, mats = refs[:4]
        w_refs, m_refs, v_refs = (refs[4 + t * count:4 + (t + 1) * count] for t in range(3))
        outs = refs[4 + 3 * count:4 + 7 * count]
        sum_norms, sum_512, sum_256, unfolded = refs[4 + 7 * count:]
        sum_norms[...] = _sum_parts(norms)
        sum_512[...] = _sum_parts(v512)
        sum_256[...] = _sum_parts(v256)
        for n_, nm in enumerate(names):
            if nm in NORM_NAMES:
                g = sum_norms[pl.ds(depth * NORM_NAMES.index(nm), depth), :]
            elif nm in VEC512_NAMES:
                g = sum_512[pl.ds(depth * VEC512_NAMES.index(nm), depth), :]
            elif nm in VEC256_NAMES:
                g = sum_256[pl.ds(depth * VEC256_NAMES.index(nm), depth), :]
            else:
                p_ = LRU_MAT_NAMES.index(nm)
                folded = mats[0, p_].astype(F32)
                for j in range(1, N_DEV):
                    folded = folded + mats[j, p_].astype(F32)
                for hd in range(LRU_HEADS):
                    src_rows, src_cols = _folded_block(hd)
                    unfolded[:, hd * LRU_HD:(hd + 1) * LRU_HD, :] = folded[:, src_rows, src_cols]
                g = unfolded[...]
            delta, nm_, nv_ = _adamw(g, w_refs[n_][...], m_refs[n_][...], v_refs[n_][...])
            outs[n_][...] = g
            outs[count + n_][...] = delta
            outs[2 * count + n_][...] = nm_
            outs[3 * count + n_][...] = nv_

    shapes = [_sds(w[nm].shape, F32) for nm in names]
    ins = list(gathered) + [t[nm] for t in (w, m, v) for nm in names]
    outs = pl.pallas_call(
        body, name="adamw_replicated", in_specs=[VMEM_SPEC] * len(ins), out_specs=[VMEM_SPEC] * (4 * count),
        out_shape=shapes * 4,
        scratch_shapes=[pltpu.VMEM(gathered[0].shape[1:], F32), pltpu.VMEM(gathered[1].shape[1:], F32),
                        pltpu.VMEM(gathered[2].shape[1:], F32), pltpu.VMEM((depth, LRU_W, LRU_HD), F32)],
        compiler_params=_params(None, 48),
    )(*ins)
    return [dict(zip(names, outs[t * count:(t + 1) * count])) for t in range(4)]


WEIGHT_NAMES = ("meta_tokens", "norm_mix_pre", "norm_mix_post", "norm_mlp_pre", "norm_mlp_post", "w_in", "conv_w", "conv_b",
                "lru_wa_f", "lru_ba_f", "lru_wx_f", "lru_bx_f", "lru_lambda_f", "lru_wa_b", "lru_ba_b", "lru_wx_b",
                "lru_bx_b", "lru_lambda_b", "gla_wg_f", "gla_bg_f", "gla_wg_b", "gla_bg_b", "gla_head_norm", "w_out",
                "w_mlp_up", "w_mlp_down")
MATMUL_WEIGHTS = ("w_in", "w_out", "w_mlp_up", "w_mlp_down")
SMALL_SHARDED = ("conv_w", "gla_wg_f", "gla_wg_b", "meta_tokens")


def kernel(x, meta_tokens, norm_mix_pre, norm_mix_post, norm_mlp_pre, norm_mlp_post, w_in, conv_w, conv_b, lru_wa_f, lru_ba_f, lru_wx_f, lru_bx_f, lru_lambda_f, lru_wa_b, lru_ba_b, lru_wx_b, lru_bx_b, lru_lambda_b, gla_wg_f, gla_bg_f, gla_wg_b, gla_bg_b, gla_head_norm, w_out, w_mlp_up, w_mlp_down, loss_target, m_meta_tokens, m_norm_mix_pre, m_norm_mix_post, m_norm_mlp_pre, m_norm_mlp_post, m_w_in, m_conv_w, m_conv_b, m_lru_wa_f, m_lru_ba_f, m_lru_wx_f, m_lru_bx_f, m_lru_lambda_f, m_lru_wa_b, m_lru_ba_b, m_lru_wx_b, m_lru_bx_b, m_lru_lambda_b, m_gla_wg_f, m_gla_bg_f, m_gla_wg_b, m_gla_bg_b, m_gla_head_norm, m_w_out, m_w_mlp_up, m_w_mlp_down, v_meta_tokens, v_norm_mix_pre, v_norm_mix_post, v_norm_mlp_pre, v_norm_mlp_post, v_w_in, v_conv_w, v_conv_b, v_lru_wa_f, v_lru_ba_f, v_lru_wx_f, v_lru_bx_f, v_lru_lambda_f, v_lru_wa_b, v_lru_ba_b, v_lru_wx_b, v_lru_bx_b, v_lru_lambda_b, v_gla_wg_f, v_gla_bg_f, v_gla_wg_b, v_gla_bg_b, v_gla_head_norm, v_w_out, v_w_mlp_up, v_w_mlp_down):
    args = locals()
    w = {nm: args[nm] for nm in WEIGHT_NAMES}
    m = {nm: args["m_" + nm] for nm in WEIGHT_NAMES}
    v = {nm: args["v_" + nm] for nm in WEIGHT_NAMES}
    depth = w_in.shape[0]

    shards = {nm: w[nm].astype(BF16) for nm in MATMUL_WEIGHTS}
    p = {}
    gather = _Outbox(_install_weight(p))
    _request_weight(gather, shards, "w_in", 0)
    _request_weight(gather, shards, "w_out", 0)
    ex, tickets = gather.exchange()
    first_small = len(ex.landings)
    for nm in SMALL_SHARDED:
        ex.add(w[nm], _whole, _sds((N_DEV,) + w[nm].shape, F32), _slab)
    landed = ex.run("all_gather")
    gather.store(tickets, landed)
    p.update(_prepare_params(w, dict(zip(SMALL_SHARDED, landed[first_small:])), depth))

    h = jnp.concatenate([jnp.zeros((PAD_ROWS, D_MODEL), F32), p["meta_tokens"], x[0]], axis=0)
    saved = []
    for l in range(depth):
        h, s = _layer_fwd(h, l, p, gather, shards, depth)
        saved.append(s)
    dh, loss_part = _loss_and_grad(h, loss_target[0])
    loss = lax.psum(loss_part[0, 0], ("x", "y", "c"))

    received = {}
    outbox = _Outbox(received.__setitem__)
    grads = [None] * depth
    for l in reversed(range(depth)):
        dh, grads[l], dz = _layer_bwd(dh, l, p, saved[l], outbox)
        if l > 0:
            _send_grad(outbox, "w_in", l, _w_in_slabs(_matmul_tn(saved[l]["hn"], dz, "grad_w_in")))
    grad_x = dh[PAD_ROWS + N_META:][None]

    small = _pack_small_grads(grads, dh, depth)
    rep_bufs, small_slabs = small[:4], small[4:]
    ex, tickets = outbox.exchange()
    first_small = len(ex.landings)
    for g in small_slabs:
        ex.add(g, _slab, _sds(g.shape, F32), _slab)
    for g in rep_bufs:
        ex.add(g, _whole, _sds((N_DEV,) + g.shape, g.dtype), _slab)
    grad_w_in, *landed = _matmul_tn(saved[0]["hn"], dz, "grad_w_in", exchange=ex)
    outbox.store(tickets, landed)
    small_received = landed[first_small:first_small + len(small_slabs)]
    rep_received = landed[first_small + len(small_slabs):]
    _send_grad(outbox, "w_in", 0, _w_in_slabs(grad_w_in))
    ex, tickets = outbox.exchange()
    outbox.store(tickets, ex.run("exchange_grads"))

    results = [{}, {}, {}, {}]
    for nm in MATMUL_WEIGHTS:
        parts = [received[(nm, l)] for l in range(depth)]
        for t, out in enumerate(_adamw_sharded(parts, w[nm], m[nm], v[nm], "adamw_" + nm)):
            results[t][nm] = out
    for nm, parts in zip(SMALL_SHARDED, small_received):
        for t, out in enumerate(_adamw_sharded(parts, w[nm], m[nm], v[nm], "adamw_" + nm)):
            results[t][nm] = out

    def kernel_side(tree):
        return {nm: tree[nm].reshape(depth, LRU_W, LRU_HD) if nm in LRU_MAT_NAMES else tree[nm]
                for nm in NORM_NAMES + VEC512_NAMES + VEC256_NAMES + LRU_MAT_NAMES}

    for t, tree in enumerate(_adamw_replicated(rep_received, kernel_side(w), kernel_side(m), kernel_side(v), depth)):
        for nm, out in tree.items():
            results[t][nm] = out.reshape(w[nm].shape)
    return (loss, grad_x, *[results[t][nm] for t in range(4) for nm in WEIGHT_NAMES])
```

```python
import functools

import jax
import jax.numpy as jnp
from jax import lax
from jax.experimental import pallas as pl
from jax.experimental.pallas import tpu as pltpu

F32 = jnp.float32
BF16 = jnp.bfloat16

N_DEV = 8
D_MODEL = 1024
N_META = 16
ROW_BLOCK = 256
PAD_ROWS = ROW_BLOCK - N_META
CHUNK = 128
LRU_W = 512
LRU_HEADS = 8
LRU_HD = 64
LRU_C = 8.0
GLA_HEADS = 4
GLA_DK = 64
GLA_DV = 128
GLA_QK = GLA_HEADS * GLA_DK
GLA_W = GLA_HEADS * GLA_DV
GLA_RANK = 16
GATE_NORM = 16.0
D_FF = 4096
D_IN = 2592
Z_W = 2688
ZG_COL_BLOCK = 2560 // 128
EPS = 1e-6
LANES = 128

ADAM_LR = 0.001
ADAM_B1 = 0.9
ADAM_B2 = 0.999
ADAM_EPS = 1e-08
ADAM_WD = 0.01
ADAM_STEP = 10

VMEM_SPEC = pl.BlockSpec(memory_space=pltpu.VMEM)
ANY_SPEC = pl.BlockSpec(memory_space=pl.ANY)
MESH_ID = pl.DeviceIdType.MESH


def _sds(shape, dtype):
    return jax.ShapeDtypeStruct(shape, dtype)


def _params(sem=None, vmem_mb=None):
    kw = {}
    if sem is not None:
        kw["dimension_semantics"] = sem
    if vmem_mb is not None:
        kw["vmem_limit_bytes"] = vmem_mb * 2 ** 20
    return pltpu.CompilerParams(**kw)


def _row_tile(n, cap=768):
    for t in (768, 512, 384, 256):
        if t <= cap and n % t == 0:
            return t
    raise ValueError(n)


def _col_tile(k):
    for t in (1024, 896, 768, 640, 512, 384, 256, 128):
        if k % t == 0:
            return t
    raise ValueError(k)


def _sigmoid(x):
    return 0.5 * jnp.tanh(0.5 * x) + 0.5


def _gelu_and_grad(x):
    c = 0.7978845608028654
    inner = c * (x + 0.044715 * x * x * x)
    t = jnp.tanh(inner)
    gelu = 0.5 * x * (1.0 + t)
    dgelu = 0.5 * (1.0 + t) + 0.5 * x * (1.0 - t * t) * c * (1.0 + 3.0 * 0.044715 * x * x)
    return gelu, dgelu


def _one_minus_square(a, log_a):
    return jnp.tanh(-log_a) * (1.0 + a * a)


def _rms_fwd(x, g):
    rs = lax.rsqrt(jnp.mean(x * x, axis=-1, keepdims=True) + EPS)
    return x * rs * g


def _rms_bwd(x, g, dy):
    rs = lax.rsqrt(jnp.mean(x * x, axis=-1, keepdims=True) + EPS)
    xh = x * rs
    dyg = dy * g
    dx = rs * (dyg - xh * jnp.mean(dyg * xh, axis=-1, keepdims=True))
    return dx, jnp.sum(dy * xh, axis=0, keepdims=True)


def _dot(a, b):
    return jnp.dot(a.astype(BF16), b.astype(BF16), preferred_element_type=F32)


def _dot_nt(a, b):
    return lax.dot_general(a.astype(BF16), b.astype(BF16), (((1,), (1,)), ((), ())), preferred_element_type=F32)


def _dot_tn(a, b):
    return lax.dot_general(a.astype(BF16), b.astype(BF16), (((0,), (0,)), ((), ())), preferred_element_type=F32)


class _LayerParam:
    def __init__(self, array, *index):
        self.array = array
        self.index = index

    @property
    def spec(self):
        lead = len(self.index)
        tail = self.array.shape[lead:]
        index = self.index
        return pl.BlockSpec((None,) * lead + tail, lambda *_: index + (0,) * len(tail))


def _row_ids(rows, block_index):
    return block_index * rows + lax.broadcasted_iota(jnp.int32, (rows, 1), 0)


def _accumulate(ref, value, first):
    @pl.when(first)
    def _():
        ref[...] = value

    @pl.when(jnp.logical_not(first))
    def _():
        ref[...] += value


def _norm_in_proj(h, g, w, exchange=None):
    n, d = h.shape
    zw = w.shape[1]
    tr = _row_tile(n)

    def body(h_ref, g_ref, w_ref, hn_ref, z_ref):
        hn = _rms_fwd(h_ref[...], g_ref[...]).astype(BF16)
        hn_ref[...] = hn
        z_ref[...] = jnp.dot(hn, w_ref[...], preferred_element_type=F32)

    return _hosting_call(
        exchange, body, name="norm_in_proj", grid=(n // tr,),
        in_specs=[pl.BlockSpec((tr, d), lambda i: (i, 0)), g.spec, VMEM_SPEC],
        out_specs=[pl.BlockSpec((tr, d), lambda i: (i, 0)), pl.BlockSpec((tr, zw), lambda i: (i, 0))],
        out_shape=[_sds((n, d), BF16), _sds((n, zw), F32)],
        scratch_shapes=[], compiler_params=_params(("arbitrary",), 48),
    )(h, g.array, w)


def _halo_specs(width, nb, col=0):
    per = ROW_BLOCK // 8
    prev = pl.BlockSpec((8, width), lambda i: (jnp.maximum(i * per - 1, 0), col))
    nxt = pl.BlockSpec((8, width), lambda i: (jnp.minimum((i + 1) * per, nb * per - 1), col))
    return prev, nxt


def _shift_down(x, prev8, d):
    n = x.shape[0]
    r = pltpu.roll(x, d, 0)
    p = pltpu.roll(prev8, d, 0)
    row8 = lax.broadcasted_iota(jnp.int32, (8, 1), 0)
    head = jnp.where(row8 < d, p, r[0:8])
    return jnp.concatenate([head, r[8:]], axis=0)


def _shift_up(x, next8, d):
    n = x.shape[0]
    r = pltpu.roll(x, n - d, 0)
    q = pltpu.roll(next8, 8 - d, 0)
    row8 = lax.broadcasted_iota(jnp.int32, (8, 1), 0)
    tail = jnp.where(row8 >= 8 - d, q, r[n - 8:])
    return jnp.concatenate([r[:n - 8], tail], axis=0)


def _conv_fwd(z, conv_w, conv_b):
    n = z.shape[0]
    nb = n // ROW_BLOCK
    prev_spec, next_spec = _halo_specs(LRU_W, nb)

    def body(cur_ref, prev_ref, next_ref, w_ref, b_ref, xc_ref):
        i = pl.program_id(0)
        cur = cur_ref[...]
        prev8 = prev_ref[...] * jnp.where(i > 0, 1.0, 0.0)
        next8 = next_ref[...] * jnp.where(i < nb - 1, 1.0, 0.0)
        w = [w_ref[pl.ds(k, 1), :] for k in range(4)]
        xc = (w[0] * _shift_down(cur, prev8, 2) + w[1] * _shift_down(cur, prev8, 1)
              + w[2] * cur + w[3] * _shift_up(cur, next8, 1) + b_ref[...])
        xc_ref[...] = xc

    return pl.pallas_call(
        body, name="conv_fwd", grid=(nb,),
        in_specs=[pl.BlockSpec((ROW_BLOCK, LRU_W), lambda i: (i, 0)), prev_spec, next_spec, conv_w.spec, conv_b.spec],
        out_specs=pl.BlockSpec((ROW_BLOCK, LRU_W), lambda i: (i, 0)),
        out_shape=_sds((n, LRU_W), F32),
        compiler_params=_params(("parallel",)),
    )(z, z, z, conv_w.array, conv_b.array)


def _conv_bwd(dxc_f, dxc_b, z, conv_w):
    n = z.shape[0]
    nb = n // ROW_BLOCK
    prev_spec, next_spec = _halo_specs(LRU_W, nb)
    row_spec = pl.BlockSpec((ROW_BLOCK, LRU_W), lambda i: (i, 0))

    def body(df_ref, dfp_ref, dfn_ref, db_ref, dbp_ref, dbn_ref, x_ref, xp_ref, xn_ref, w_ref,
             dx_ref, dw_ref, dbias_ref):
        i = pl.program_id(0)
        has_prev = jnp.where(i > 0, 1.0, 0.0)
        has_next = jnp.where(i < nb - 1, 1.0, 0.0)
        dxc = df_ref[...] + db_ref[...]
        dprev = (dfp_ref[...] + dbp_ref[...]) * has_prev
        dnext = (dfn_ref[...] + dbn_ref[...]) * has_next
        x = x_ref[...]
        xprev = xp_ref[...] * has_prev
        xnext = xn_ref[...] * has_next
        w = [w_ref[pl.ds(k, 1), :] for k in range(4)]
        dx_ref[...] = (w[0] * _shift_up(dxc, dnext, 2) + w[1] * _shift_up(dxc, dnext, 1)
                       + w[2] * dxc + w[3] * _shift_down(dxc, dprev, 1)).astype(BF16)
        dw = jnp.concatenate([
            jnp.sum(dxc * _shift_down(x, xprev, 2), axis=0, keepdims=True),
            jnp.sum(dxc * _shift_down(x, xprev, 1), axis=0, keepdims=True),
            jnp.sum(dxc * x, axis=0, keepdims=True),
            jnp.sum(dxc * _shift_up(x, xnext, 1), axis=0, keepdims=True),
            jnp.zeros((4, LRU_W), F32)], axis=0)
        _accumulate(dw_ref, dw, i == 0)
        _accumulate(dbias_ref, jnp.sum(dxc, axis=0, keepdims=True), i == 0)

    dx, dw, dbias = pl.pallas_call(
        body, name="conv_bwd", grid=(nb,),
        in_specs=[row_spec, prev_spec, next_spec, row_spec, prev_spec, next_spec, row_spec, prev_spec, next_spec,
                  conv_w.spec],
        out_specs=[row_spec, pl.BlockSpec((8, LRU_W), lambda i: (0, 0)), pl.BlockSpec((1, LRU_W), lambda i: (0, 0))],
        out_shape=[_sds((n, LRU_W), BF16), _sds((8, LRU_W), F32), _sds((1, LRU_W), F32)],
        compiler_params=_params(("arbitrary",)),
    )(dxc_f, dxc_f, dxc_f, dxc_b, dxc_b, dxc_b, z, z, z, conv_w.array)
    return dx, dw, dbias


def _mix_epilogue(h_f, h_b, o_f, o_b, z, head_norm):
    n = z.shape[0]
    tr = ROW_BLOCK
    spec = pl.BlockSpec((tr, 512), lambda i: (i, 0))

    def body(hf_ref, hb_ref, of_ref, ob_ref, gate_ref, gout_ref, w_ref, y_ref):
        gelu, _ = _gelu_and_grad(gate_ref[...])
        y_ref[:, 0:LRU_W] = ((hf_ref[...] + hb_ref[...]) * gelu).astype(BF16)
        o = of_ref[...] + ob_ref[...]
        gout = gout_ref[...]
        silu = gout * _sigmoid(gout)
        w = w_ref[...]
        for hd in range(GLA_HEADS):
            cs = slice(hd * GLA_DV, (hd + 1) * GLA_DV)
            oh = o[:, cs]
            on = oh * lax.rsqrt(jnp.mean(oh * oh, axis=-1, keepdims=True) + EPS)
            y_ref[:, LRU_W + hd * GLA_DV:LRU_W + (hd + 1) * GLA_DV] = (on * w[:, cs] * silu[:, cs]).astype(BF16)

    return pl.pallas_call(
        body, name="mix_epilogue", grid=(n // tr,),
        in_specs=[spec, spec, spec, spec, pl.BlockSpec((tr, 512), lambda i: (i, 1)),
                  pl.BlockSpec((tr, 512), lambda i: (i, 4)), head_norm.spec],
        out_specs=pl.BlockSpec((tr, D_MODEL), lambda i: (i, 0)),
        out_shape=_sds((n, D_MODEL), BF16),
        compiler_params=_params(("parallel",)),
    )(h_f, h_b, o_f, o_b, z, z, head_norm.array)


def _mix_epilogue_bwd(dymix, h_f, h_b, o_f, o_b, z, head_norm):
    n = z.shape[0]
    tr = ROW_BLOCK
    spec = pl.BlockSpec((tr, 512), lambda i: (i, 0))

    def body(dyl_ref, dyg_ref, hf_ref, hb_ref, of_ref, ob_ref, gate_ref, gout_ref, w_ref,
             dhs_ref, dgate_ref, do_ref, dgout_ref, dw_ref):
        i = pl.program_id(0)
        dyl = dyl_ref[...]
        gelu, dgelu = _gelu_and_grad(gate_ref[...])
        dhs_ref[...] = dyl * gelu
        dgate_ref[...] = (dyl * (hf_ref[...] + hb_ref[...]) * dgelu).astype(BF16)
        dyg = dyg_ref[...]
        o = of_ref[...] + ob_ref[...]
        gout = gout_ref[...]
        sg = _sigmoid(gout)
        silu = gout * sg
        dsilu = sg * (1.0 + gout * (1.0 - sg))
        w = w_ref[...]
        dws = []
        for hd in range(GLA_HEADS):
            cs = slice(hd * GLA_DV, (hd + 1) * GLA_DV)
            oh = o[:, cs]
            rs = lax.rsqrt(jnp.mean(oh * oh, axis=-1, keepdims=True) + EPS)
            on = oh * rs
            dy = dyg[:, cs]
            dgout_ref[:, cs] = (dy * on * w[:, cs] * dsilu[:, cs]).astype(BF16)
            dys = dy * silu[:, cs]
            dws.append(jnp.sum(dys * on, axis=0, keepdims=True))
            don = dys * w[:, cs]
            do_ref[:, cs] = (rs * (don - on * jnp.mean(don * on, axis=-1, keepdims=True))).astype(BF16)
        _accumulate(dw_ref, jnp.concatenate(dws, axis=1), i == 0)

    return pl.pallas_call(
        body, name="mix_epilogue_bwd", grid=(n // tr,),
        in_specs=[pl.BlockSpec((tr, 512), lambda i: (i, 0)), pl.BlockSpec((tr, 512), lambda i: (i, 1)),
                  spec, spec, spec, spec, pl.BlockSpec((tr, 512), lambda i: (i, 1)),
                  pl.BlockSpec((tr, 512), lambda i: (i, 4)), head_norm.spec],
        out_specs=[spec, spec, spec, spec, pl.BlockSpec((1, GLA_W), lambda i: (0, 0))],
        out_shape=[_sds((n, 512), F32)] + [_sds((n, 512), BF16)] * 3 + [_sds((1, GLA_W), F32)],
        compiler_params=_params(("arbitrary",)),
    )(dymix, dymix, h_f, h_b, o_f, o_b, z, z, head_norm.array)


def _out_proj(ymix, w_out, h, g, exchange=None):
    n, d = h.shape
    tr = _row_tile(n)
    spec = pl.BlockSpec((tr, d), lambda i: (i, 0))

    def body(y_ref, w_ref, h_ref, g_ref, mix_ref, hmid_ref):
        mix = jnp.dot(y_ref[...], w_ref[...], preferred_element_type=F32)
        mix_ref[...] = mix
        hmid_ref[...] = h_ref[...] + _rms_fwd(mix, g_ref[...])

    return _hosting_call(
        exchange, body, name="out_proj", grid=(n // tr,),
        in_specs=[spec, VMEM_SPEC, spec, g.spec],
        out_specs=[spec, spec],
        out_shape=[_sds((n, d), F32), _sds((n, d), F32)],
        scratch_shapes=[], compiler_params=_params(("arbitrary",), 44),
    )(ymix, w_out, h, g.array)


def _out_proj_bwd(dh_mid, mix, g, w_out, exchange=None):
    n, d = mix.shape
    tr = _row_tile(n)
    spec = pl.BlockSpec((tr, d), lambda i: (i, 0))

    def body(dh_ref, mix_ref, g_ref, w_ref, dmix_ref, dy_ref, dg_ref):
        i = pl.program_id(0)
        dmix, dg = _rms_bwd(mix_ref[...], g_ref[...], dh_ref[...])
        dmix = dmix.astype(BF16)
        dmix_ref[...] = dmix
        dy_ref[...] = _dot_nt(dmix, w_ref[...])
        _accumulate(dg_ref, dg, i == 0)

    return _hosting_call(
        exchange, body, name="out_proj_bwd", grid=(n // tr,),
        in_specs=[spec, spec, g.spec, VMEM_SPEC],
        out_specs=[spec, spec, pl.BlockSpec((1, d), lambda i: (0, 0))],
        out_shape=[_sds((n, d), BF16), _sds((n, d), F32), _sds((1, d), F32)],
        scratch_shapes=[], compiler_params=_params(("arbitrary",), 44),
    )(dh_mid, mix, g.array, w_out)


FF_SLAB = D_FF // N_DEV


def _relu_squared(up):
    return jnp.square(jnp.maximum(up.astype(F32), 0.0)).astype(BF16)


def _mlp_fwd(h_mid, g_pre, w_up, w_down, g_post, exchange=None):
    n, d = h_mid.shape
    tr = _row_tile(n, 384)
    spec = pl.BlockSpec((tr, d), lambda i: (i, 0))

    def body(h_ref, gpre_ref, wup_ref, wdn_ref, gpost_ref, hn_ref, up_ref, ff_ref, hout_ref):
        h = h_ref[...]
        hn = _rms_fwd(h, gpre_ref[...]).astype(BF16)
        hn_ref[...] = hn
        ff = jnp.zeros((tr, d), F32)
        for j in range(N_DEV):
            cs = slice(j * FF_SLAB, (j + 1) * FF_SLAB)
            up = jnp.dot(hn, wup_ref[j], preferred_element_type=F32).astype(BF16)
            up_ref[:, cs] = up
            ff = ff + jnp.dot(_relu_squared(up), wdn_ref[cs, :], preferred_element_type=F32)
        ff_ref[...] = ff
        hout_ref[...] = h + _rms_fwd(ff, gpost_ref[...])

    return _hosting_call(
        exchange, body, name="mlp_fwd", grid=(n // tr,),
        in_specs=[spec, g_pre.spec, VMEM_SPEC, VMEM_SPEC, g_post.spec],
        out_specs=[spec, pl.BlockSpec((tr, D_FF), lambda i: (i, 0)), spec, spec],
        out_shape=[_sds((n, d), BF16), _sds((n, D_FF), BF16), _sds((n, d), F32), _sds((n, d), F32)],
        scratch_shapes=[], compiler_params=_params(("arbitrary",), 52),
    )(h_mid, g_pre.array, w_up, w_down, g_post.array)


def _mlp_bwd(dh, ff, up, h_mid, g_pre, w_up, w_down, g_post, exchange=None):
    n, d = h_mid.shape
    tr = _row_tile(n, 384)
    spec = pl.BlockSpec((tr, d), lambda i: (i, 0))
    wide = pl.BlockSpec((tr, D_FF), lambda i: (i, 0))
    gspec = pl.BlockSpec((1, d), lambda i: (0, 0))

    def body(dh_ref, ff_ref, up_ref, h_ref, gpre_ref, wup_ref, wdn_ref, gpost_ref,
             dff_ref, dup_ref, dhmid_ref, dgpost_ref, dgpre_ref):
        i = pl.program_id(0)
        dh = dh_ref[...]
        dff, dgpost = _rms_bwd(ff_ref[...], gpost_ref[...], dh)
        dff = dff.astype(BF16)
        dff_ref[...] = dff
        dhn = jnp.zeros((tr, d), F32)
        for j in range(N_DEV):
            cs = slice(j * FF_SLAB, (j + 1) * FF_SLAB)
            relu = jnp.maximum(up_ref[:, cs].astype(F32), 0.0)
            dact = _dot_nt(dff, wdn_ref[cs, :])
            dup = (dact * 2.0 * relu).astype(BF16)
            dup_ref[:, cs] = dup
            dhn = dhn + _dot_nt(dup, wup_ref[j])
        dx, dgpre = _rms_bwd(h_ref[...], gpre_ref[...], dhn)
        dhmid_ref[...] = dh + dx
        _accumulate(dgpost_ref, dgpost, i == 0)
        _accumulate(dgpre_ref, dgpre, i == 0)

    return _hosting_call(
        exchange, body, name="mlp_bwd", grid=(n // tr,),
        in_specs=[spec, spec, wide, spec, g_pre.spec, VMEM_SPEC, VMEM_SPEC, g_post.spec],
        out_specs=[spec, wide, spec, gspec, gspec],
        out_shape=[_sds((n, d), BF16), _sds((n, D_FF), BF16), _sds((n, d), F32), _sds((1, d), F32), _sds((1, d), F32)],
        scratch_shapes=[], compiler_params=_params(("arbitrary",), 56),
    )(dh, ff, up, h_mid, g_pre.array, w_up, w_down, g_post.array)


def _in_proj_bwd(pieces, w_in, h, g, dh_mid):
    dxbr, dgate, dqk_f, dqk_b, dv_f, dv_b, dgout, dzg_f, dzg_b = pieces
    n, d = h.shape
    tr = _row_tile(n, 384)
    spec = pl.BlockSpec((tr, d), lambda i: (i, 0))
    s512 = pl.BlockSpec((tr, 512), lambda i: (i, 0))
    s128 = pl.BlockSpec((tr, LANES), lambda i: (i, 0))

    def body(a_ref, b_ref, cf_ref, cb_ref, df_ref, db_ref, e_ref, ff_ref, fb_ref, w_ref, h_ref, g_ref, dhm_ref,
             dz_ref, dh_ref, dg_ref):
        i = pl.program_id(0)
        real = (_row_ids(tr, i) >= PAD_ROWS).astype(F32)
        f32 = lambda ref: ref[...].astype(F32)
        dz = jnp.concatenate([f32(a_ref), f32(b_ref), f32(cf_ref) + f32(cb_ref), f32(df_ref) + f32(db_ref),
                              f32(e_ref), f32(ff_ref) + f32(fb_ref)], axis=1) * real
        dz = dz.astype(BF16)
        dz_ref[...] = dz
        dhn = _dot_nt(dz, w_ref[...])
        dx, dg = _rms_bwd(h_ref[...], g_ref[...], dhn)
        dh_ref[...] = (dhm_ref[...] + dx) * real
        _accumulate(dg_ref, dg, i == 0)

    return pl.pallas_call(
        body, name="in_proj_bwd", grid=(n // tr,),
        in_specs=[s512, s512, s512, s512, s512, s512, s512, s128, s128, VMEM_SPEC, spec, g.spec, spec],
        out_specs=[pl.BlockSpec((tr, Z_W), lambda i: (i, 0)), spec, pl.BlockSpec((1, d), lambda i: (0, 0))],
        out_shape=[_sds((n, Z_W), BF16), _sds((n, d), F32), _sds((1, d), F32)],
        compiler_params=_params(("arbitrary",), 48),
    )(dxbr, dgate, dqk_f, dqk_b, dv_f, dv_b, dgout, dzg_f, dzg_b, w_in, h, g.array, dh_mid)


def _matmul_tn(a, b, name, column_slabs=False, exchange=None, a_map=None):
    n, m = a.shape
    k = b.shape[1]
    tr = next(t for t in (2816, 1408, 768, 512, 256) if n % t == 0)
    tm, tk = _col_tile(m), _col_tile(k)
    steps = n // tr
    slab = k // N_DEV
    per_step = tk // slab if column_slabs else 1
    sub = next(t for t in (704, 768, 512, 256) if tr % t == 0)

    def body(a_ref, b_ref, o_ref, acc_ref, *mapped_ref):
        r = pl.program_id(2)
        if a_map is None:
            a_blk = a_ref[...]
        else:
            for c in range(tr // sub):
                rows = pl.ds(c * sub, sub)
                mapped_ref[0][rows, :] = a_map(a_ref[rows, :])
            a_blk = mapped_ref[0][...]
        _accumulate(acc_ref, _dot_tn(a_blk, b_ref[...]), r == 0)

        @pl.when(r == steps - 1)
        def _():
            if column_slabs:
                for j in range(per_step):
                    o_ref[j] = acc_ref[:, j * slab:(j + 1) * slab].astype(BF16)
            else:
                o_ref[...] = acc_ref[...].astype(BF16)

    if column_slabs:
        out_spec = pl.BlockSpec((per_step, tm, slab), lambda mi, ki, r: (ki, mi, 0))
        out_shape = _sds((N_DEV, m, slab), BF16)
    else:
        out_spec = pl.BlockSpec((tm, tk), lambda mi, ki, r: (mi, ki))
        out_shape = _sds((m, k), BF16)
    outs = _hosting_call(
        exchange, body, name=name, grid=(m // tm, k // tk, steps),
        in_specs=[pl.BlockSpec((tr, tm), lambda mi, ki, r: (r, mi)), pl.BlockSpec((tr, tk), lambda mi, ki, r: (r, ki))],
        out_specs=[out_spec], out_shape=[out_shape],
        scratch_shapes=[pltpu.VMEM((tm, tk), F32)] + ([] if a_map is None else [pltpu.VMEM((tr, tm), BF16)]),
        compiler_params=_params(("arbitrary", "arbitrary", "arbitrary"), 52),
    )(a, b)
    return outs[0] if exchange is None else outs


def _loss_and_grad(h_out, target):
    n, d = h_out.shape
    tr = ROW_BLOCK
    first = (PAD_ROWS + N_META) // tr

    def body(h_ref, t_ref, dh_ref, loss_ref):
        i = pl.program_id(0)
        real = jnp.where(i >= first, 1.0, 0.0)
        diff = (h_ref[...] - t_ref[...]) * real
        dh_ref[...] = diff * (1.0 / d)
        part = 0.5 * jnp.sum(jnp.mean(diff * diff, axis=-1, keepdims=True), axis=0, keepdims=True)
        _accumulate(loss_ref, jnp.broadcast_to(part, (1, LANES)), i == 0)

    return pl.pallas_call(
        body, name="loss_and_grad", grid=(n // tr,),
        in_specs=[pl.BlockSpec((tr, d), lambda i: (i, 0)), pl.BlockSpec((tr, d), lambda i: (jnp.maximum(i - first, 0), 0))],
        out_specs=[pl.BlockSpec((tr, d), lambda i: (i, 0)), pl.BlockSpec((1, LANES), lambda i: (0, 0))],
        out_shape=[_sds((n, d), F32), _sds((1, LANES), F32)],
        compiler_params=_params(("arbitrary",)),
    )(h_out, target)


def _scan_block(a, u, h_in, reverse):
    n = a.shape[0]
    row = lax.broadcasted_iota(jnp.int32, (n, 1), 0)
    d = 1
    while d < n:
        shift = n - d if reverse else d
        keep = (row < n - d) if reverse else (row >= d)
        a_s = pltpu.roll(a, shift, 0)
        u_s = pltpu.roll(u, shift, 0)
        u = jnp.where(keep, a * u_s + u, u)
        a = jnp.where(keep, a * a_s, a)
        d *= 2
    return a * h_in + u


def _lru_gates(xc, wcat_ref, bias_ref, lam_ref):
    nl = -lam_ref[...]
    nsp = -LRU_C * (jnp.maximum(nl, 0.0) + jnp.log(1.0 + jnp.exp(-jnp.abs(nl))))
    pre = _dot(xc, wcat_ref[...]) + bias_ref[...]
    r = _sigmoid(pre[:, :LRU_W])
    ig = _sigmoid(pre[:, LRU_W:])
    log_a = r * nsp
    a = jnp.exp(log_a)
    m2 = _one_minus_square(a, log_a)
    inv_m = lax.rsqrt(jnp.maximum(m2, 1e-30))
    return r, ig, a, m2 * inv_m, inv_m, nsp


def _lru_scan(xc, wcat, bias, lam, reverse, exchange=None):
    n = xc.shape[0]
    nb = n // ROW_BLOCK
    order = (lambda i: nb - 1 - i) if reverse else (lambda i: i)
    spec = pl.BlockSpec((ROW_BLOCK, LRU_W), lambda i: (order(i), 0))
    edge = 0 if reverse else ROW_BLOCK - 1

    def body(xc_ref, wcat_ref, bias_ref, lam_ref, h_ref, carry_ref):
        i = pl.program_id(0)

        @pl.when(i == 0)
        def _():
            carry_ref[...] = jnp.zeros_like(carry_ref)

        xc = xc_ref[...]
        r, ig, a, m, _, _ = _lru_gates(xc, wcat_ref, bias_ref, lam_ref)
        u = jnp.where(_row_ids(ROW_BLOCK, order(i)) >= PAD_ROWS, m * (ig * xc), 0.0)
        h_ref[...] = _scan_block(a, u, carry_ref[0:1, :], reverse)
        carry_ref[0:1, :] = h_ref[pl.ds(edge, 1), :]

    return _hosting_call(
        exchange, body, name="lru_scan_b" if reverse else "lru_scan_f", grid=(nb,),
        in_specs=[spec, wcat.spec, bias.spec, lam.spec],
        out_specs=[spec],
        out_shape=[_sds((n, LRU_W), F32)],
        scratch_shapes=[pltpu.VMEM((8, LRU_W), F32)],
        compiler_params=_params(("arbitrary",)),
    )(xc, wcat.array, bias.array, lam.array)


def _lru_scan_bwd(dhs, xc, h, wcat, bias, lam, reverse, exchange=None):
    n = xc.shape[0]
    nb = n // ROW_BLOCK
    per = ROW_BLOCK // 8
    order = (lambda i: i) if reverse else (lambda i: nb - 1 - i)
    spec = pl.BlockSpec((ROW_BLOCK, LRU_W), lambda i: (order(i), 0))
    if reverse:
        halo = pl.BlockSpec((8, LRU_W), lambda i: (jnp.minimum((order(i) + 1) * per, nb * per - 1), 0))
    else:
        halo = pl.BlockSpec((8, LRU_W), lambda i: (jnp.maximum(order(i) * per - 1, 0), 0))
    edge = ROW_BLOCK - 1 if reverse else 0

    def body(dhs_ref, xc_ref, h_ref, halo_ref, wcat_ref, bias_ref, lam_ref,
             dxc_ref, dw_ref, db_ref, dlam_ref, cdh_ref, ca_ref, tmp_ref):
        i = pl.program_id(0)
        ib = order(i)

        @pl.when(i == 0)
        def _():
            cdh_ref[...] = jnp.zeros_like(cdh_ref)
            ca_ref[...] = jnp.zeros_like(ca_ref)

        xc = xc_ref[...]
        r, ig, a, m, inv_m, nsp = _lru_gates(xc, wcat_ref, bias_ref, lam_ref)
        row = lax.broadcasted_iota(jnp.int32, (ROW_BLOCK, 1), 0)
        if reverse:
            coef = jnp.where(row == 0, ca_ref[0:1, :], pltpu.roll(a, 1, 0))
            h_nb = jnp.where(row == ROW_BLOCK - 1, halo_ref[0:1, :] * jnp.where(ib < nb - 1, 1.0, 0.0),
                             pltpu.roll(h_ref[...], ROW_BLOCK - 1, 0))
        else:
            coef = jnp.where(row == ROW_BLOCK - 1, ca_ref[0:1, :], pltpu.roll(a, ROW_BLOCK - 1, 0))
            h_nb = jnp.where(row == 0, halo_ref[7:8, :] * jnp.where(ib > 0, 1.0, 0.0), pltpu.roll(h_ref[...], 1, 0))
        dh = _scan_block(coef, dhs_ref[...], cdh_ref[0:1, :], not reverse)
        tmp_ref[...] = dh
        cdh_ref[0:1, :] = tmp_ref[pl.ds(edge, 1), :]
        tmp_ref[...] = a
        ca_ref[0:1, :] = tmp_ref[pl.ds(edge, 1), :]

        du = jnp.where(_row_ids(ROW_BLOCK, ib) >= PAD_ROWS, dh, 0.0)
        da = dh * h_nb
        dm = du * (ig * xc)
        di = du * (m * xc)
        dlog_a = da * a - dm * (a * a) * inv_m
        dr = dlog_a * nsp
        dpre = jnp.concatenate([dr * r * (1.0 - r), di * ig * (1.0 - ig)], axis=1)
        dxc_ref[...] = du * (m * ig) + _dot_nt(dpre, wcat_ref[...])
        _accumulate(dw_ref, _dot_tn(xc, dpre), i == 0)
        _accumulate(db_ref, jnp.sum(dpre, axis=0, keepdims=True), i == 0)
        _accumulate(dlam_ref, jnp.sum(dlog_a * r, axis=0, keepdims=True), i == 0)

        @pl.when(i == nb - 1)
        def _():
            dlam_ref[...] = dlam_ref[...] * (LRU_C * _sigmoid(-lam_ref[...]))

    return _hosting_call(
        exchange, body, name="lru_scan_bwd_b" if reverse else "lru_scan_bwd_f", grid=(nb,),
        in_specs=[spec, spec, spec, halo, wcat.spec, bias.spec, lam.spec],
        out_specs=[spec, pl.BlockSpec((LRU_W, 2 * LRU_W), lambda i: (0, 0)),
                   pl.BlockSpec((1, 2 * LRU_W), lambda i: (0, 0)), pl.BlockSpec((1, LRU_W), lambda i: (0, 0))],
        out_shape=[_sds((n, LRU_W), F32), _sds((LRU_W, 2 * LRU_W), F32), _sds((1, 2 * LRU_W), F32), _sds((1, LRU_W), F32)],
        scratch_shapes=[pltpu.VMEM((8, LRU_W), F32), pltpu.VMEM((8, LRU_W), F32), pltpu.VMEM((ROW_BLOCK, LRU_W), F32)],
        compiler_params=_params(("arbitrary",)),
    )(dhs, xc, h, h, wcat.array, bias.array, lam.array)


def _gla_rows(n):
    return 768 if n % 768 == 0 else ROW_BLOCK


def _gla_masks(reverse):
    t = lax.broadcasted_iota(jnp.int32, (CHUNK, CHUNK), 0)
    s = lax.broadcasted_iota(jnp.int32, (CHUNK, CHUNK), 1)
    if reverse:
        return (s >= t).astype(F32), s > t
    return (s <= t).astype(F32), s <= t


def _gla_gate(zg, wg_ref, bg_ref):
    pre = _dot(zg, wg_ref[...]) + bg_ref[...]
    g = (jnp.minimum(pre, 0.0) - jnp.log(1.0 + jnp.exp(-jnp.abs(pre)))) * (1.0 / GATE_NORM)
    return pre, g


def _gla_decays(gc, tri):
    b = jnp.dot(tri, gc, precision=lax.Precision.HIGHEST, preferred_element_type=F32)
    b_last = jnp.sum(gc, axis=0, keepdims=True)
    return jnp.exp(b), jnp.exp(-b), jnp.exp(b_last - b), jnp.exp(b_last)


def _gla_scan(z, wg, bg, reverse, exchange=None):
    n = z.shape[0]
    rb = _gla_rows(n)
    nb = n // rb
    cpb = rb // CHUNK
    order = (lambda i: nb - 1 - i) if reverse else (lambda i: i)
    chunks = range(cpb - 1, -1, -1) if reverse else range(cpb)

    def body(qk_ref, v_ref, zg_ref, wg_ref, bg_ref, o_ref, sall_ref, s_ref):
        i = pl.program_id(0)

        @pl.when(i == 0)
        def _():
            s_ref[...] = jnp.zeros_like(s_ref)

        tri, mask = _gla_masks(reverse)
        _, g = _gla_gate(zg_ref[...], wg_ref, bg_ref)
        heads = range(GLA_HEADS)
        ks = [slice(hd * GLA_DK, (hd + 1) * GLA_DK) for hd in heads]
        vs = [slice(hd * GLA_DV, (hd + 1) * GLA_DV) for hd in heads]
        qh, kb, v, el, p, intra, kv = {}, {}, {}, {}, {}, {}, {}
        for c in chunks:
            rows = slice(c * CHUNK, (c + 1) * CHUNK)
            eb, enb, ebl, el[c] = _gla_decays(g[rows], tri)
            qk = qk_ref[rows, :]
            q_all = (qk[:, :GLA_QK] * (GLA_DK ** -0.5) * eb).astype(BF16)
            k_all = (qk[:, GLA_QK:] * enb).astype(BF16)
            kb_all = (qk[:, GLA_QK:] * ebl).astype(BF16)
            v_all = v_ref[rows, :].astype(BF16)
            for hd in heads:
                qh[c, hd], kb[c, hd], v[c, hd] = q_all[:, ks[hd]], kb_all[:, ks[hd]], v_all[:, vs[hd]]
                p[c, hd] = _dot_nt(qh[c, hd], k_all[:, ks[hd]])
        for c in chunks:
            for hd in heads:
                intra[c, hd] = _dot(jnp.where(mask, p[c, hd], 0.0), v[c, hd])
                kv[c, hd] = _dot_tn(v[c, hd], kb[c, hd])
        state = [s_ref[:, ks[hd]] for hd in heads]
        for c in chunks:
            rows = slice(c * CHUNK, (c + 1) * CHUNK)
            for hd in heads:
                sall_ref[c, :, ks[hd]] = state[hd]
                o_ref[rows, vs[hd]] = intra[c, hd] + _dot_nt(qh[c, hd], state[hd])
                state[hd] = state[hd] * el[c][:, ks[hd]] + kv[c, hd]
        for hd in heads:
            s_ref[:, ks[hd]] = state[hd]

    return _hosting_call(
        exchange, body, name="gla_scan_b" if reverse else "gla_scan_f", grid=(nb,),
        in_specs=[pl.BlockSpec((rb, 512), lambda i: (order(i), 2)), pl.BlockSpec((rb, 512), lambda i: (order(i), 3)),
                  pl.BlockSpec((rb, LANES), lambda i: (order(i), ZG_COL_BLOCK)), wg.spec, bg.spec],
        out_specs=[pl.BlockSpec((rb, GLA_W), lambda i: (order(i), 0)),
                   pl.BlockSpec((cpb, GLA_DV, GLA_QK), lambda i: (order(i), 0, 0))],
        out_shape=[_sds((n, GLA_W), F32), _sds((n // CHUNK, GLA_DV, GLA_QK), F32)],
        scratch_shapes=[pltpu.VMEM((GLA_DV, GLA_QK), F32)],
        compiler_params=_params(("arbitrary",)),
    )(z, z, z, wg.array, bg.array)


def _gla_scan_bwd(do, z, states, wg, bg, reverse, exchange=None):
    n = z.shape[0]
    rb = _gla_rows(n)
    nb = n // rb
    cpb = rb // CHUNK
    order = (lambda i: i) if reverse else (lambda i: nb - 1 - i)
    chunks = range(cpb) if reverse else range(cpb - 1, -1, -1)

    def body(do_ref, qk_ref, v_ref, zg_ref, sall_ref, wg_ref, bg_ref,
             dqk_ref, dv_ref, dzg_ref, dwg_ref, dbg_ref, ds_ref):
        i = pl.program_id(0)

        @pl.when(i == 0)
        def _():
            ds_ref[...] = jnp.zeros_like(ds_ref)

        tri, mask = _gla_masks(reverse)
        tri_t, _ = _gla_masks(not reverse)
        zg = zg_ref[...]
        pre, g = _gla_gate(zg, wg_ref, bg_ref)
        heads = range(GLA_HEADS)
        ks = [slice(hd * GLA_DK, (hd + 1) * GLA_DK) for hd in heads]
        vs = [slice(hd * GLA_DV, (hd + 1) * GLA_DV) for hd in heads]
        dec, full, qh, kh, kb, v, dout, p, dp = {}, {}, {}, {}, {}, {}, {}, {}, {}
        for c in chunks:
            rows = slice(c * CHUNK, (c + 1) * CHUNK)
            dec[c] = _gla_decays(g[rows], tri)
            eb, enb, ebl, _ = dec[c]
            qk = qk_ref[rows, :]
            q_f = qk[:, :GLA_QK] * (GLA_DK ** -0.5) * eb
            k_f = qk[:, GLA_QK:] * enb
            kb_f = qk[:, GLA_QK:] * ebl
            full[c] = (q_f, k_f, kb_f)
            q_all, k_all, kb_all = q_f.astype(BF16), k_f.astype(BF16), kb_f.astype(BF16)
            v_all, do_all = v_ref[rows, :].astype(BF16), do_ref[rows, :].astype(BF16)
            for hd in heads:
                qh[c, hd], kh[c, hd], kb[c, hd] = q_all[:, ks[hd]], k_all[:, ks[hd]], kb_all[:, ks[hd]]
                v[c, hd], dout[c, hd] = v_all[:, vs[hd]], do_all[:, vs[hd]]
                p[c, hd] = _dot_nt(qh[c, hd], kh[c, hd])
                dp[c, hd] = _dot_nt(dout[c, hd], v[c, hd])
        dv_i, dqh, dkh, dsq, state = {}, {}, {}, {}, {}
        for c in chunks:
            for hd in heads:
                pm = jnp.where(mask, p[c, hd], 0.0).astype(BF16)
                dpm = jnp.where(mask, dp[c, hd], 0.0).astype(BF16)
                state[c, hd] = sall_ref[c, :, ks[hd]]
                dv_i[c, hd] = _dot_tn(pm, dout[c, hd])
                dqh[c, hd] = _dot(dpm, kh[c, hd]) + _dot(dout[c, hd], state[c, hd])
                dkh[c, hd] = _dot_tn(dpm, qh[c, hd])
                dsq[c, hd] = _dot_tn(dout[c, hd], qh[c, hd])
        dstate = [ds_ref[:, ks[hd]] for hd in heads]
        dkb, sds = {}, {}
        for c in chunks:
            rows = slice(c * CHUNK, (c + 1) * CHUNK)
            el = dec[c][3]
            for hd in heads:
                dv_ref[rows, vs[hd]] = (dv_i[c, hd] + _dot_nt(kb[c, hd], dstate[hd])).astype(BF16)
                dkb[c, hd] = _dot(v[c, hd], dstate[hd])
                sds[c, hd] = jnp.sum(state[c, hd] * dstate[hd], axis=0, keepdims=True)
                dstate[hd] = dstate[hd] * el[:, ks[hd]] + dsq[c, hd]
        for hd in heads:
            ds_ref[:, ks[hd]] = dstate[hd]
        dgs = [None] * cpb
        for c in chunks:
            rows = slice(c * CHUNK, (c + 1) * CHUNK)
            eb, enb, ebl, el = dec[c]
            q_f, k_f, kb_f = full[c]
            dqh_c = jnp.concatenate([dqh[c, hd] for hd in heads], axis=1)
            dkh_c = jnp.concatenate([dkh[c, hd] for hd in heads], axis=1)
            dkb_c = jnp.concatenate([dkb[c, hd] for hd in heads], axis=1)
            sds_c = jnp.concatenate([sds[c, hd] for hd in heads], axis=1)
            dqk_ref[rows, :] = jnp.concatenate([dqh_c * eb * (GLA_DK ** -0.5), dkh_c * enb + dkb_c * ebl], axis=1).astype(BF16)
            dkb_kb = dkb_c * kb_f
            db = dqh_c * q_f - dkh_c * k_f - dkb_kb
            db_last = el * sds_c + jnp.sum(dkb_kb, axis=0, keepdims=True)
            dgs[c] = jnp.dot(tri_t, db, precision=lax.Precision.HIGHEST, preferred_element_type=F32) + db_last
        dg = jnp.concatenate(dgs, axis=0)
        dpre = dg * _sigmoid(-pre) * (1.0 / GATE_NORM)
        dzg_ref[...] = _dot_nt(dpre, wg_ref[...]).astype(BF16)
        _accumulate(dwg_ref, _dot_tn(zg, dpre), i == 0)
        _accumulate(dbg_ref, jnp.sum(dpre, axis=0, keepdims=True), i == 0)

    return _hosting_call(
        exchange, body, name="gla_scan_bwd_b" if reverse else "gla_scan_bwd_f", grid=(nb,),
        in_specs=[pl.BlockSpec((rb, GLA_W), lambda i: (order(i), 0)),
                  pl.BlockSpec((rb, 512), lambda i: (order(i), 2)), pl.BlockSpec((rb, 512), lambda i: (order(i), 3)),
                  pl.BlockSpec((rb, LANES), lambda i: (order(i), ZG_COL_BLOCK)),
                  pl.BlockSpec((cpb, GLA_DV, GLA_QK), lambda i: (order(i), 0, 0)), wg.spec, bg.spec],
        out_specs=[pl.BlockSpec((rb, 512), lambda i: (order(i), 0)), pl.BlockSpec((rb, 512), lambda i: (order(i), 0)),
                   pl.BlockSpec((rb, LANES), lambda i: (order(i), 0)),
                   pl.BlockSpec((LANES, GLA_QK), lambda i: (0, 0)), pl.BlockSpec((1, GLA_QK), lambda i: (0, 0))],
        out_shape=[_sds((n, 512), BF16), _sds((n, 512), BF16), _sds((n, LANES), BF16), _sds((LANES, GLA_QK), F32),
                   _sds((1, GLA_QK), F32)],
        scratch_shapes=[pltpu.VMEM((GLA_DV, GLA_QK), F32)],
        compiler_params=_params(("arbitrary",)),
    )(do, z, z, z, states, wg.array, bg.array)


NORM_NAMES = ("norm_mix_pre", "norm_mix_post", "norm_mlp_pre", "norm_mlp_post")
VEC512_NAMES = ("conv_b", "lru_ba_f", "lru_bx_f", "lru_lambda_f", "lru_ba_b", "lru_bx_b", "lru_lambda_b", "gla_head_norm")
VEC256_NAMES = ("gla_bg_f", "gla_bg_b")
LRU_MAT_NAMES = ("lru_wa_f", "lru_wx_f", "lru_wa_b", "lru_wx_b")
DIRS = ("f", "b")


def _prepare_params(w, gathered, depth):
    row_names = NORM_NAMES + ("conv_b", "gla_head_norm")
    ins = ([w[nm] for nm in row_names] + [w["lru_ba_" + d] for d in DIRS] + [w["lru_bx_" + d] for d in DIRS]
           + [w["lru_lambda_" + d] for d in DIRS] + [w["gla_bg_" + d] for d in DIRS]
           + [w["lru_wa_" + d].reshape(depth, LRU_W, LRU_HD) for d in DIRS]
           + [w["lru_wx_" + d].reshape(depth, LRU_W, LRU_HD) for d in DIRS]
           + [gathered["conv_w"], gathered["gla_wg_f"], gathered["gla_wg_b"], gathered["meta_tokens"]])
    n_rows = len(row_names)

    def body(*refs):
        rows_in = refs[:n_rows]
        ba, bx, lam, bg, wa, wx = (refs[n_rows + 2 * t:n_rows + 2 * t + 2] for t in range(6))
        convw_g, wgf_g, wgb_g, meta_g = refs[n_rows + 12:n_rows + 16]
        outs = refs[n_rows + 16:]
        rows_out = outs[:n_rows]
        convw, wcat, bias, lam_o, wg, bg_o, meta = outs[n_rows:]
        for l in range(depth):
            for src, dst in zip(rows_in, rows_out):
                dst[l] = src[pl.ds(l, 1), :]
            convw[l] = jnp.zeros((8, LRU_W), F32)
            for j in range(N_DEV):
                convw[l, 0:4, j * 64:(j + 1) * 64] = convw_g[j, l]
            for d in range(2):
                wcat[l, d] = jnp.zeros((LRU_W, 2 * LRU_W), BF16)
                for hd in range(LRU_HEADS):
                    rs = slice(hd * LRU_HD, (hd + 1) * LRU_HD)
                    wcat[l, d, rs, hd * LRU_HD:(hd + 1) * LRU_HD] = wa[d][l, rs, :].astype(BF16)
                    wcat[l, d, rs, LRU_W + hd * LRU_HD:LRU_W + (hd + 1) * LRU_HD] = wx[d][l, rs, :].astype(BF16)
                bias[l, d, :, 0:LRU_W] = ba[d][pl.ds(l, 1), :]
                bias[l, d, :, LRU_W:2 * LRU_W] = bx[d][pl.ds(l, 1), :]
                lam_o[l, d] = lam[d][pl.ds(l, 1), :]
                bg_o[l, d] = bg[d][pl.ds(l, 1), :]
                wg[l, d] = jnp.zeros((LANES, GLA_QK), BF16)
                src = wgf_g if d == 0 else wgb_g
                for j in range(N_DEV):
                    wg[l, d, d * GLA_RANK:(d + 1) * GLA_RANK, j * 32:(j + 1) * 32] = src[j, l].astype(BF16)
        for j in range(N_DEV):
            meta[:, j * LANES:(j + 1) * LANES] = meta_g[j]

    out_shape = ([_sds((depth, 1, w[nm].shape[1]), F32) for nm in row_names]
                 + [_sds((depth, 8, LRU_W), F32), _sds((depth, 2, LRU_W, 2 * LRU_W), BF16), _sds((depth, 2, 1, 2 * LRU_W), F32),
                    _sds((depth, 2, 1, LRU_W), F32), _sds((depth, 2, LANES, GLA_QK), BF16), _sds((depth, 2, 1, GLA_QK), F32),
                    _sds((N_META, D_MODEL), F32)])
    outs = pl.pallas_call(
        body, name="prepare_params", in_specs=[VMEM_SPEC] * len(ins), out_specs=[VMEM_SPEC] * len(out_shape),
        out_shape=out_shape, compiler_params=_params(None, 32),
    )(*ins)
    prepared = dict(zip(row_names, outs[:n_rows]))
    prepared.update(zip(("conv_w", "wcat", "lru_bias", "lru_lam", "wg", "gla_bg", "meta_tokens"), outs[n_rows:]))
    return prepared


class _Outbox:
    def __init__(self, on_complete):
        self.pending, self.on_complete = {}, on_complete

    def put(self, key, array, src, landing_shape):
        self.pending[key] = dict(array=array, src=src, landing=landing_shape, groups=list(range(len(PEER_GROUPS))))

    def exchange(self, wanted=None):
        ex, tickets = _Exchange(), []
        for key, item in self.pending.items():
            groups = [g for g in item["groups"] if wanted is None or (key[0], g) in wanted]
            out = None
            for g in groups:
                landing = item["landing"] if out is None else out
                out = ex.add(item["array"], item["src"], landing, _slab, peers=PEER_GROUPS[g], local=(g == 0))
                item["groups"].remove(g)
            if groups:
                tickets.append((key, out))
        return ex, tickets

    def store(self, tickets, landed):
        for key, out in tickets:
            item = self.pending[key]
            item["landing"] = landed[out]
            if not item["groups"]:
                del self.pending[key]
                self.on_complete(key, landed[out])


def _install_weight(p):
    def install(key, g):
        nm, l = key
        if nm == "w_in":
            g = jnp.pad(jnp.concatenate([g[j] for j in range(N_DEV)], axis=1), ((0, 0), (0, Z_W - D_IN)))
        elif nm == "w_out":
            g = g.reshape(D_MODEL, D_MODEL)
        elif nm == "w_mlp_down":
            g = g.reshape(D_FF, D_MODEL)
        p.setdefault(nm, {})[l] = g
    return install


def _request_weight(gather, shards, nm, l):
    gather.put((nm, l), shards[nm], _layer_of(l), _sds((N_DEV,) + shards[nm].shape[1:], BF16))


def _layer_fwd(h, l, p, gather, shards, depth):
    lp = lambda name, *index: _LayerParam(p[name], l, *index)
    s = dict(h=h)

    def hosted(fn, wanted, *args):
        ex, tickets = gather.exchange(wanted)
        outs = fn(*args, ex)
        own = len(outs) - len(ex.landings)
        gather.store(tickets, outs[own:])
        return outs[:own]

    _request_weight(gather, shards, "w_mlp_up", l)
    _request_weight(gather, shards, "w_mlp_down", l)
    first = [("w_mlp_up", 0)] + [("w_out", group) for group in range(len(PEER_GROUPS))]
    s["hn"], s["z"] = hosted(_norm_in_proj, first, h, lp("norm_mix_pre"), p["w_in"][l])
    s["xc"] = _conv_fwd(s["z"], lp("conv_w"), lp("conv_b"))
    plan = {"f": ([("w_mlp_up", 1)], [("w_mlp_up", 2)]), "b": ([("w_mlp_down", 0)], [("w_mlp_down", 1)])}
    for d, name in enumerate(DIRS):
        s["h_" + name], = hosted(_lru_scan, plan[name][0], s["xc"], lp("wcat", d), lp("lru_bias", d), lp("lru_lam", d), d == 1)
        s["o_" + name], s["s_" + name] = hosted(_gla_scan, plan[name][1], s["z"], lp("wg", d), lp("gla_bg", d), d == 1)
    s["ymix"] = _mix_epilogue(s["h_f"], s["h_b"], s["o_f"], s["o_b"], s["z"], lp("gla_head_norm"))
    s["mix"], s["h_mid"] = hosted(_out_proj, [("w_mlp_down", 2)], s["ymix"], p["w_out"][l], h, lp("norm_mix_post"))
    if l + 1 < depth:
        _request_weight(gather, shards, "w_in", l + 1)
        _request_weight(gather, shards, "w_out", l + 1)
    s["hn2"], s["up"], s["ff"], h_out = hosted(
        _mlp_fwd, None, s["h_mid"], lp("norm_mlp_pre"), p["w_mlp_up"][l], p["w_mlp_down"][l], lp("norm_mlp_post"))
    return h_out, s


def _layer_bwd(dh_out, l, p, s, outbox):
    lp = lambda name, *index: _LayerParam(p[name], l, *index)
    g = {}

    def hosted(fn, wanted, *args):
        ex, tickets = outbox.exchange(wanted)
        outs = fn(*args, ex)
        own = len(outs) - len(ex.landings)
        outbox.store(tickets, outs[own:])
        return outs[:own]

    d_ff, dup, dh_mid, g["norm_mlp_post"], g["norm_mlp_pre"] = hosted(
        _mlp_bwd, None, dh_out, s["ff"], s["up"], s["h_mid"], lp("norm_mlp_pre"), p["w_mlp_up"][l], p["w_mlp_down"][l],
        lp("norm_mlp_post"))
    _send_grad(outbox, "w_mlp_down", l, _matmul_tn(s["up"], d_ff, "grad_w_down", a_map=_relu_squared)
               .reshape(N_DEV, D_FF // N_DEV, D_MODEL))
    _send_grad(outbox, "w_mlp_up", l, _matmul_tn(s["hn2"], dup, "grad_w_up", column_slabs=True))
    dmix, dymix, g["norm_mix_post"] = hosted(_out_proj_bwd, [("w_mlp_down", 0)], dh_mid, s["mix"], lp("norm_mix_post"),
                                             p["w_out"][l])
    grad_w_out = _matmul_tn(s["ymix"], dmix, "grad_w_out").reshape(N_DEV, D_MODEL // N_DEV, D_MODEL)
    dhs, dgate, do, dgout, g["gla_head_norm"] = _mix_epilogue_bwd(
        dymix, s["h_f"], s["h_b"], s["o_f"], s["o_b"], s["z"], lp("gla_head_norm"))
    plan = {"f": ([("w_mlp_down", 1)], [("w_mlp_down", 2), ("w_mlp_up", 0)]), "b": ([("w_mlp_up", 1)], [("w_mlp_up", 2)])}
    dqk, dv, dzg, dxc = {}, {}, {}, {}
    for d, name in enumerate(DIRS):
        dqk[name], dv[name], dzg[name], g["wg_" + name], g["gla_bg_" + name] = hosted(
            _gla_scan_bwd, plan[name][0], do, s["z"], s["s_" + name], lp("wg", d), lp("gla_bg", d), d == 1)
        dxc[name], g["wcat_" + name], g["lru_bias_" + name], g["lru_lambda_" + name] = hosted(
            _lru_scan_bwd, plan[name][1], dhs, s["xc"], s["h_" + name], lp("wcat", d), lp("lru_bias", d), lp("lru_lam", d),
            d == 1)
    _send_grad(outbox, "w_out", l, grad_w_out)
    dxbr, g["conv_w"], g["conv_b"] = _conv_bwd(dxc["f"], dxc["b"], s["z"], lp("conv_w"))
    dz, dh_in, g["norm_mix_pre"] = _in_proj_bwd(
        (dxbr, dgate, dqk["f"], dqk["b"], dv["f"], dv["b"], dgout, dzg["f"], dzg["b"]),
        p["w_in"][l], s["h"], lp("norm_mix_pre"), dh_mid)
    return dh_in, g, dz


def _send_grad(outbox, nm, l, slabs):
    outbox.put((nm, l), slabs, _slab, _sds(slabs.shape, slabs.dtype))


def _w_in_slabs(grad_w_in):
    shard = D_IN // N_DEV
    return jnp.stack([grad_w_in[:, j * shard:(j + 1) * shard] for j in range(N_DEV)])


def _folded_block(hd):
    return slice((hd // 2) * LRU_HD, (hd // 2 + 1) * LRU_HD), slice((hd % 2) * LRU_HD, (hd % 2 + 1) * LRU_HD)


def _pack_small_grads(grads, dh0, depth):
    per_layer = ("norm_mix_pre", "norm_mix_post", "norm_mlp_pre", "norm_mlp_post", "conv_b", "gla_head_norm",
                 "lru_bias_f", "lru_bias_b", "lru_lambda_f", "lru_lambda_b", "gla_bg_f", "gla_bg_b",
                 "wcat_f", "wcat_b", "conv_w", "wg_f", "wg_b")
    ins = [grads[l][nm] for l in range(depth) for nm in per_layer]
    k = len(per_layer)
    meta_rows = PAD_ROWS // N_META

    def body(*refs):
        g = [dict(zip(per_layer, refs[l * k:(l + 1) * k])) for l in range(depth)]
        dh0_ref = refs[depth * k]
        norms, v512, v256, mats, convw, wgf, wgb, meta = refs[depth * k + 1:]
        v256[...] = jnp.zeros_like(v256)
        for l in range(depth):
            for p_, nm in enumerate(NORM_NAMES):
                norms[pl.ds(2 * p_ + l, 1), :] = g[l][nm][...]
            rows512 = [g[l]["conv_b"][...], g[l]["lru_bias_f"][:, 0:LRU_W], g[l]["lru_bias_f"][:, LRU_W:2 * LRU_W],
                       g[l]["lru_lambda_f"][...], g[l]["lru_bias_b"][:, 0:LRU_W], g[l]["lru_bias_b"][:, LRU_W:2 * LRU_W],
                       g[l]["lru_lambda_b"][...], g[l]["gla_head_norm"][...]]
            for p_, row in enumerate(rows512):
                v512[pl.ds(2 * p_ + l, 1), :] = row
            for p_, nm in enumerate(("gla_bg_f", "gla_bg_b")):
                v256[pl.ds(2 * p_ + l, 1), :] = g[l][nm][...]
            for d, name in enumerate(DIRS):
                for hd in range(LRU_HEADS):
                    rs = slice(hd * LRU_HD, (hd + 1) * LRU_HD)
                    dst_rows, dst_cols = _folded_block(hd)
                    mats[2 * d, l, dst_rows, dst_cols] = g[l]["wcat_" + name][rs, hd * LRU_HD:(hd + 1) * LRU_HD].astype(BF16)
                    mats[2 * d + 1, l, dst_rows, dst_cols] = (
                        g[l]["wcat_" + name][rs, LRU_W + hd * LRU_HD:LRU_W + (hd + 1) * LRU_HD].astype(BF16))
            for j in range(N_DEV):
                convw[j, l] = g[l]["conv_w"][0:4, j * 64:(j + 1) * 64]
                wgf[j, l] = g[l]["wg_f"][0:GLA_RANK, j * 32:(j + 1) * 32]
                wgb[j, l] = g[l]["wg_b"][GLA_RANK:2 * GLA_RANK, j * 32:(j + 1) * 32]
        for j in range(N_DEV):
            meta[j] = dh0_ref[:, j * LANES:(j + 1) * LANES]

    out_shape = [_sds((8, D_MODEL), F32), _sds((16, LRU_W), F32), _sds((8, GLA_QK), F32),
                 _sds((4, depth, LRU_W // 2, 2 * LRU_HD), BF16),
                 _sds((N_DEV, depth, 4, 64), F32), _sds((N_DEV, depth, GLA_RANK, 32), F32), _sds((N_DEV, depth, GLA_RANK, 32), F32),
                 _sds((N_DEV, N_META, LANES), F32)]
    return pl.pallas_call(
        body, name="pack_small_grads", grid=(1,),
        in_specs=[VMEM_SPEC] * (depth * k) + [pl.BlockSpec((N_META, D_MODEL), lambda i: (meta_rows, 0))],
        out_specs=[VMEM_SPEC] * len(out_shape), out_shape=out_shape, compiler_params=_params(("arbitrary",), 32),
    )(*ins, dh0)


def _my_index():
    return 4 * lax.axis_index("x") + 2 * lax.axis_index("y") + lax.axis_index("c")


def _peer(k):
    x, y, c = lax.axis_index("x"), lax.axis_index("y"), lax.axis_index("c")
    px = x ^ ((k >> 2) & 1)
    py = y ^ ((k >> 1) & 1)
    pc = c ^ (k & 1)
    return (px, py, pc), 4 * px + 2 * py + pc


ALL_PEERS = tuple(range(1, N_DEV))
PEER_GROUPS = ((1, 2, 3), (4, 5), (6, 7))


class _Exchange:
    def __init__(self):
        self.inputs, self.landings, self.transfers = [], [], []

    def add(self, array, src, landing, dst, peers=ALL_PEERS, local=True):
        if isinstance(landing, int):
            out = landing
        else:
            out = len(self.landings)
            self.landings.append(landing)
        self.transfers.append((len(self.inputs), src, out, dst, tuple(peers), local))
        self.inputs.append(array)
        return out

    def _pairs(self):
        return [(t, k) for t, tr in enumerate(self.transfers) for k in tr[4]]

    def _locals(self):
        return [t for t, tr in enumerate(self.transfers) if tr[5]]

    def out_shapes(self):
        return [g if isinstance(g, jax.ShapeDtypeStruct) else _sds(g.shape, g.dtype) for g in self.landings]

    def continued(self):
        return [(b, g) for b, g in enumerate(self.landings) if not isinstance(g, jax.ShapeDtypeStruct)]

    def sem_shapes(self):
        return [pltpu.SemaphoreType.DMA((max(len(self._pairs()), 1),)), pltpu.SemaphoreType.DMA((max(len(self._pairs()), 1),)),
                pltpu.SemaphoreType.DMA((max(len(self._locals()), 1),))]

    def _local(self, ins, outs, sems):
        me = _my_index()
        copies = []
        for s, t in enumerate(self._locals()):
            a, src, b, dst, _, _ = self.transfers[t]
            copies.append(pltpu.make_async_copy(src(ins[a], me), dst(outs[b], me), sems[2].at[s]))
        return copies

    def _remote(self, ins, outs, sems, sending):
        copies = []
        for s, (t, k) in enumerate(self._pairs()):
            a, src, b, dst, _, _ = self.transfers[t]
            peer, peer_index = _peer(k)
            copies.append(pltpu.make_async_remote_copy(
                src_ref=src(ins[a], peer_index), dst_ref=dst(outs[b], _my_index() if sending else peer_index),
                send_sem=sems[0].at[s], recv_sem=sems[1].at[s], device_id=peer, device_id_type=MESH_ID))
        return copies

    def start(self, ins, outs, sems):
        for cp in self._local(ins, outs, sems) + self._remote(ins, outs, sems, True):
            cp.start()

    def wait(self, ins, outs, sems):
        for cp in self._remote(ins, outs, sems, False):
            cp.wait_recv()
        for cp in self._remote(ins, outs, sems, True):
            cp.wait_send()
        for cp in self._local(ins, outs, sems):
            cp.wait()

    def run(self, name):
        return _hosting_call(self, None, name=name, grid=(), in_specs=[], out_specs=[], out_shape=[], scratch_shapes=[],
                             compiler_params=pltpu.CompilerParams(has_side_effects=True))()


def _hosting_call(exchange, body, *, name, grid, in_specs, out_specs, out_shape, scratch_shapes, compiler_params):
    if exchange is None or not exchange.transfers:
        return pl.pallas_call(body, name=name, grid=grid, in_specs=in_specs, out_specs=out_specs, out_shape=out_shape,
                              scratch_shapes=scratch_shapes, compiler_params=compiler_params)
    n_in, n_out, n_scr = len(in_specs), len(out_specs), len(scratch_shapes)
    x_in, x_out = len(exchange.inputs), len(exchange.landings)
    continued = exchange.continued()

    def hosted(*refs):
        ins, x_ins = refs[:n_in], refs[n_in:n_in + x_in]
        o0 = n_in + x_in + len(continued)
        outs, x_outs = refs[o0:o0 + n_out], refs[o0 + n_out:o0 + n_out + x_out]
        s0 = o0 + n_out + x_out
        scratch, sems = refs[s0:s0 + n_scr], refs[s0 + n_scr:]
        if body is None:
            exchange.start(x_ins, x_outs, sems)
            exchange.wait(x_ins, x_outs, sems)
            return
        ids = [pl.program_id(a) for a in range(len(grid))]
        first = functools.reduce(jnp.logical_and, [i == 0 for i in ids])
        last = functools.reduce(jnp.logical_and, [i == g - 1 for i, g in zip(ids, grid)])

        @pl.when(first)
        def _():
            exchange.start(x_ins, x_outs, sems)

        body(*ins, *outs, *scratch)

        @pl.when(last)
        def _():
            exchange.wait(x_ins, x_outs, sems)

    aliases = {n_in + x_in + i: n_out + b for i, (b, _) in enumerate(continued)}
    kwargs = dict(grid=grid) if grid else {}
    call = pl.pallas_call(
        hosted, name=name, in_specs=list(in_specs) + [ANY_SPEC] * (x_in + len(continued)),
        out_specs=list(out_specs) + [ANY_SPEC] * x_out, out_shape=list(out_shape) + exchange.out_shapes(),
        scratch_shapes=list(scratch_shapes) + exchange.sem_shapes(), compiler_params=compiler_params,
        input_output_aliases=aliases, **kwargs)
    return lambda *operands: call(*operands, *exchange.inputs, *[g for _, g in continued])


def _whole(ref, j):
    return ref


def _slab(ref, j):
    return ref.at[j]


def _layer_of(l):
    return lambda ref, j: ref.at[l]


def _adamw(g, w, m, v):
    nm = ADAM_B1 * m + (1.0 - ADAM_B1) * g
    nv = ADAM_B2 * v + (1.0 - ADAM_B2) * jnp.square(g)
    m_hat = nm / (1.0 - ADAM_B1 ** ADAM_STEP)
    v_hat = nv / (1.0 - ADAM_B2 ** ADAM_STEP)
    return -ADAM_LR * (m_hat / (jnp.sqrt(v_hat) + ADAM_EPS) + ADAM_WD * w), nm, nv


def _sum_parts(p_ref):
    g = p_ref[0].astype(F32)
    for j in range(1, N_DEV):
        g = g + p_ref[j].astype(F32)
    return g


def _adamw_sharded(parts, w, m, v, name):
    shape = w.shape
    lead, (rows, cols) = shape[:-2], shape[-2:]
    tr = min(rows, ROW_BLOCK)
    assert rows % tr == 0
    steps = rows // tr
    nl = len(lead)
    spec = pl.BlockSpec((None,) * nl + (tr, cols), lambda *idx: idx + (0,))
    per_layer = isinstance(parts, (list, tuple))
    if per_layer:
        def part_spec(l):
            return pl.BlockSpec((N_DEV, tr, cols), lambda li, r: (0, jnp.where(li == l, r, jnp.where(li < l, 0, steps - 1)), 0))
        part_specs = [part_spec(l) for l in range(len(parts))]
    else:
        parts = [parts]
        part_specs = [pl.BlockSpec((N_DEV,) + (None,) * nl + (tr, cols), lambda *idx: (0,) + idx + (0,))]
    count = len(parts)

    def body(*refs):
        p_refs = refs[:count]
        w_ref, m_ref, v_ref, g_ref, d_ref, nm_ref, nv_ref = refs[count:]

        def update(p_ref):
            g = _sum_parts(p_ref)
            g_ref[...] = g
            d_ref[...], nm_ref[...], nv_ref[...] = _adamw(g, w_ref[...], m_ref[...], v_ref[...])

        if per_layer:
            for l in range(count):
                pl.when(pl.program_id(0) == l)(functools.partial(update, p_refs[l]))
        else:
            update(p_refs[0])

    return pl.pallas_call(
        body, name=name, grid=lead + (steps,),
        in_specs=part_specs + [spec, spec, spec], out_specs=[spec] * 4, out_shape=[_sds(shape, F32)] * 4,
        compiler_params=_params(("arbitrary",) * (nl + 1)),
    )(*parts, w, m, v)


def _adamw_replicated(gathered, w, m, v, depth):
    names = NORM_NAMES + VEC512_NAMES + VEC256_NAMES + LRU_MAT_NAMES
    count = len(names)

    def body(*refs):
        norms, v512, v256, mats = refs[:4]
        w_refs, m_refs, v_refs = (refs[4 + t * count:4 + (t + 1) * count] for t in range(3))
        outs = refs[4 + 3 * count:4 + 7 * count]
        sum_norms, sum_512, sum_256, unfolded = refs[4 + 7 * count:]
        sum_norms[...] = _sum_parts(norms)
        sum_512[...] = _sum_parts(v512)
        sum_256[...] = _sum_parts(v256)
        for n_, nm in enumerate(names):
            if nm in NORM_NAMES:
                g = sum_norms[pl.ds(depth * NORM_NAMES.index(nm), depth), :]
            elif nm in VEC512_NAMES:
                g = sum_512[pl.ds(depth * VEC512_NAMES.index(nm), depth), :]
            elif nm in VEC256_NAMES:
                g = sum_256[pl.ds(depth * VEC256_NAMES.index(nm), depth), :]
            else:
                p_ = LRU_MAT_NAMES.index(nm)
                folded = mats[0, p_].astype(F32)
                for j in range(1, N_DEV):
                    folded = folded + mats[j, p_].astype(F32)
                for hd in range(LRU_HEADS):
                    src_rows, src_cols = _folded_block(hd)
                    unfolded[:, hd * LRU_HD:(hd + 1) * LRU_HD, :] = folded[:, src_rows, src_cols]
                g = unfolded[...]
            delta, nm_, nv_ = _adamw(g, w_refs[n_][...], m_refs[n_][...], v_refs[n_][...])
            outs[n_][...] = g
            outs[count + n_][...] = delta
            outs[2 * count + n_][...] = nm_
            outs[3 * count + n_][...] = nv_

    shapes = [_sds(w[nm].shape, F32) for nm in names]
    ins = list(gathered) + [t[nm] for t in (w, m, v) for nm in names]
    outs = pl.pallas_call(
        body, name="adamw_replicated", in_specs=[VMEM_SPEC] * len(ins), out_specs=[VMEM_SPEC] * (4 * count),
        out_shape=shapes * 4,
        scratch_shapes=[pltpu.VMEM(gathered[0].shape[1:], F32), pltpu.VMEM(gathered[1].shape[1:], F32),
                        pltpu.VMEM(gathered[2].shape[1:], F32), pltpu.VMEM((depth, LRU_W, LRU_HD), F32)],
        compiler_params=_params(None, 48),
    )(*ins)
    return [dict(zip(names, outs[t * count:(t + 1) * count])) for t in range(4)]


WEIGHT_NAMES = ("meta_tokens", "norm_mix_pre", "norm_mix_post", "norm_mlp_pre", "norm_mlp_post", "w_in", "conv_w", "conv_b",
                "lru_wa_f", "lru_ba_f", "lru_wx_f", "lru_bx_f", "lru_lambda_f", "lru_wa_b", "lru_ba_b", "lru_wx_b",
                "lru_bx_b", "lru_lambda_b", "gla_wg_f", "gla_bg_f", "gla_wg_b", "gla_bg_b", "gla_head_norm", "w_out",
                "w_mlp_up", "w_mlp_down")
MATMUL_WEIGHTS = ("w_in", "w_out", "w_mlp_up", "w_mlp_down")
SMALL_SHARDED = ("conv_w", "gla_wg_f", "gla_wg_b", "meta_tokens")


def kernel(x, meta_tokens, norm_mix_pre, norm_mix_post, norm_mlp_pre, norm_mlp_post, w_in, conv_w, conv_b, lru_wa_f, lru_ba_f, lru_wx_f, lru_bx_f, lru_lambda_f, lru_wa_b, lru_ba_b, lru_wx_b, lru_bx_b, lru_lambda_b, gla_wg_f, gla_bg_f, gla_wg_b, gla_bg_b, gla_head_norm, w_out, w_mlp_up, w_mlp_down, loss_target, m_meta_tokens, m_norm_mix_pre, m_norm_mix_post, m_norm_mlp_pre, m_norm_mlp_post, m_w_in, m_conv_w, m_conv_b, m_lru_wa_f, m_lru_ba_f, m_lru_wx_f, m_lru_bx_f, m_lru_lambda_f, m_lru_wa_b, m_lru_ba_b, m_lru_wx_b, m_lru_bx_b, m_lru_lambda_b, m_gla_wg_f, m_gla_bg_f, m_gla_wg_b, m_gla_bg_b, m_gla_head_norm, m_w_out, m_w_mlp_up, m_w_mlp_down, v_meta_tokens, v_norm_mix_pre, v_norm_mix_post, v_norm_mlp_pre, v_norm_mlp_post, v_w_in, v_conv_w, v_conv_b, v_lru_wa_f, v_lru_ba_f, v_lru_wx_f, v_lru_bx_f, v_lru_lambda_f, v_lru_wa_b, v_lru_ba_b, v_lru_wx_b, v_lru_bx_b, v_lru_lambda_b, v_gla_wg_f, v_gla_bg_f, v_gla_wg_b, v_gla_bg_b, v_gla_head_norm, v_w_out, v_w_mlp_up, v_w_mlp_down):
    args = locals()
    w = {nm: args[nm] for nm in WEIGHT_NAMES}
    m = {nm: args["m_" + nm] for nm in WEIGHT_NAMES}
    v = {nm: args["v_" + nm] for nm in WEIGHT_NAMES}
    depth = w_in.shape[0]

    shards = {nm: w[nm].astype(BF16) for nm in MATMUL_WEIGHTS}
    p = {}
    gather = _Outbox(_install_weight(p))
    _request_weight(gather, shards, "w_in", 0)
    _request_weight(gather, shards, "w_out", 0)
    ex, tickets = gather.exchange([("w_in", group) for group in range(len(PEER_GROUPS))])
    first_small = len(ex.landings)
    for nm in SMALL_SHARDED:
        ex.add(w[nm], _whole, _sds((N_DEV,) + w[nm].shape, F32), _slab)
    landed = ex.run("all_gather")
    gather.store(tickets, landed)
    p.update(_prepare_params(w, dict(zip(SMALL_SHARDED, landed[first_small:])), depth))

    h = jnp.concatenate([jnp.zeros((PAD_ROWS, D_MODEL), F32), p["meta_tokens"], x[0]], axis=0)
    saved = []
    for l in range(depth):
        h, s = _layer_fwd(h, l, p, gather, shards, depth)
        saved.append(s)
    dh, loss_part = _loss_and_grad(h, loss_target[0])
    loss = lax.psum(loss_part[0, 0], ("x", "y", "c"))

    received = {}
    outbox = _Outbox(received.__setitem__)
    grads = [None] * depth
    for l in reversed(range(depth)):
        dh, grads[l], dz = _layer_bwd(dh, l, p, saved[l], outbox)
        if l > 0:
            _send_grad(outbox, "w_in", l, _w_in_slabs(_matmul_tn(saved[l]["hn"], dz, "grad_w_in")))
    grad_x = dh[PAD_ROWS + N_META:][None]

    small = _pack_small_grads(grads, dh, depth)
    rep_bufs, small_slabs = small[:4], small[4:]
    ex, tickets = outbox.exchange()
    first_small = len(ex.landings)
    for g in small_slabs:
        ex.add(g, _slab, _sds(g.shape, F32), _slab)
    for g in rep_bufs:
        ex.add(g, _whole, _sds((N_DEV,) + g.shape, g.dtype), _slab)
    grad_w_in, *landed = _matmul_tn(saved[0]["hn"], dz, "grad_w_in", exchange=ex)
    outbox.store(tickets, landed)
    small_received = landed[first_small:first_small + len(small_slabs)]
    rep_received = landed[first_small + len(small_slabs):]
    _send_grad(outbox, "w_in", 0, _w_in_slabs(grad_w_in))
    ex, tickets = outbox.exchange()
    outbox.store(tickets, ex.run("exchange_grads"))

    results = [{}, {}, {}, {}]
    for nm in MATMUL_WEIGHTS:
        parts = [received[(nm, l)] for l in range(depth)]
        for t, out in enumerate(_adamw_sharded(parts, w[nm], m[nm], v[nm], "adamw_" + nm)):
            results[t][nm] = out
    for nm, parts in zip(SMALL_SHARDED, small_received):
        for t, out in enumerate(_adamw_sharded(parts, w[nm], m[nm], v[nm], "adamw_" + nm)):
            results[t][nm] = out

    def kernel_side(tree):
        return {nm: tree[nm].reshape(depth, LRU_W, LRU_HD) if nm in LRU_MAT_NAMES else tree[nm]
                for nm in NORM_NAMES + VEC512_NAMES + VEC256_NAMES + LRU_MAT_NAMES}

    for t, tree in enumerate(_adamw_replicated(rep_received, kernel_side(w), kernel_side(m), kernel_side(v), depth)):
        for nm, out in tree.items():
            results[t][nm] = out.reshape(w[nm].shape)
    return (loss, grad_x, *[results[t][nm] for t in range(4) for nm in WEIGHT_NAMES])
```

```python
import functools

import jax
import jax.numpy as jnp
from jax import lax
from jax.experimental import pallas as pl
from jax.experimental.pallas import tpu as pltpu

F32 = jnp.float32
BF16 = jnp.bfloat16

N_DEV = 8
D_MODEL = 1024
N_META = 16
ROW_BLOCK = 256
PAD_ROWS = ROW_BLOCK - N_META
CHUNK = 128
LRU_W = 512
LRU_HEADS = 8
LRU_HD = 64
LRU_C = 8.0
GLA_HEADS = 4
GLA_DK = 64
GLA_DV = 128
GLA_QK = GLA_HEADS * GLA_DK
GLA_W = GLA_HEADS * GLA_DV
GLA_RANK = 16
GATE_NORM = 16.0
D_FF = 4096
D_IN = 2592
Z_W = 2688
ZG_COL_BLOCK = 2560 // 128
EPS = 1e-6
LANES = 128

ADAM_LR = 0.001
ADAM_B1 = 0.9
ADAM_B2 = 0.999
ADAM_EPS = 1e-08
ADAM_WD = 0.01
ADAM_STEP = 10

VMEM_SPEC = pl.BlockSpec(memory_space=pltpu.VMEM)
ANY_SPEC = pl.BlockSpec(memory_space=pl.ANY)
MESH_ID = pl.DeviceIdType.MESH


def _sds(shape, dtype):
    return jax.ShapeDtypeStruct(shape, dtype)


def _params(sem=None, vmem_mb=None):
    kw = {}
    if sem is not None:
        kw["dimension_semantics"] = sem
    if vmem_mb is not None:
        kw["vmem_limit_bytes"] = vmem_mb * 2 ** 20
    return pltpu.CompilerParams(**kw)


def _row_tile(n, cap=768):
    for t in (768, 512, 384, 256):
        if t <= cap and n % t == 0:
            return t
    raise ValueError(n)


def _col_tile(k):
    for t in (1024, 896, 768, 640, 512, 384, 256, 128):
        if k % t == 0:
            return t
    raise ValueError(k)


def _sigmoid(x):
    return 0.5 * jnp.tanh(0.5 * x) + 0.5


def _gelu_and_grad(x):
    c = 0.7978845608028654
    inner = c * (x + 0.044715 * x * x * x)
    t = jnp.tanh(inner)
    gelu = 0.5 * x * (1.0 + t)
    dgelu = 0.5 * (1.0 + t) + 0.5 * x * (1.0 - t * t) * c * (1.0 + 3.0 * 0.044715 * x * x)
    return gelu, dgelu


def _one_minus_square(a, log_a):
    return jnp.tanh(-log_a) * (1.0 + a * a)


def _rms_fwd(x, g):
    rs = lax.rsqrt(jnp.mean(x * x, axis=-1, keepdims=True) + EPS)
    return x * rs * g


def _rms_bwd(x, g, dy):
    rs = lax.rsqrt(jnp.mean(x * x, axis=-1, keepdims=True) + EPS)
    xh = x * rs
    dyg = dy * g
    dx = rs * (dyg - xh * jnp.mean(dyg * xh, axis=-1, keepdims=True))
    return dx, jnp.sum(dy * xh, axis=0, keepdims=True)


def _dot(a, b):
    return jnp.dot(a.astype(BF16), b.astype(BF16), preferred_element_type=F32)


def _dot_nt(a, b):
    return lax.dot_general(a.astype(BF16), b.astype(BF16), (((1,), (1,)), ((), ())), preferred_element_type=F32)


def _dot_tn(a, b):
    return lax.dot_general(a.astype(BF16), b.astype(BF16), (((0,), (0,)), ((), ())), preferred_element_type=F32)


class _LayerParam:
    def __init__(self, array, *index):
        self.array = array
        self.index = index

    @property
    def spec(self):
        lead = len(self.index)
        tail = self.array.shape[lead:]
        index = self.index
        return pl.BlockSpec((None,) * lead + tail, lambda *_: index + (0,) * len(tail))


def _row_ids(rows, block_index):
    return block_index * rows + lax.broadcasted_iota(jnp.int32, (rows, 1), 0)


def _accumulate(ref, value, first):
    @pl.when(first)
    def _():
        ref[...] = value

    @pl.when(jnp.logical_not(first))
    def _():
        ref[...] += value


def _norm_in_proj(h, g, w, exchange=None):
    n, d = h.shape
    zw = w.shape[1]
    tr = _row_tile(n)

    def body(h_ref, g_ref, w_ref, hn_ref, z_ref):
        hn = _rms_fwd(h_ref[...], g_ref[...]).astype(BF16)
        hn_ref[...] = hn
        z_ref[...] = jnp.dot(hn, w_ref[...], preferred_element_type=F32)

    return _hosting_call(
        exchange, body, name="norm_in_proj", grid=(n // tr,),
        in_specs=[pl.BlockSpec((tr, d), lambda i: (i, 0)), g.spec, VMEM_SPEC],
        out_specs=[pl.BlockSpec((tr, d), lambda i: (i, 0)), pl.BlockSpec((tr, zw), lambda i: (i, 0))],
        out_shape=[_sds((n, d), BF16), _sds((n, zw), F32)],
        scratch_shapes=[], compiler_params=_params(("arbitrary",), 48),
    )(h, g.array, w)


def _halo_specs(width, nb, col=0):
    per = ROW_BLOCK // 8
    prev = pl.BlockSpec((8, width), lambda i: (jnp.maximum(i * per - 1, 0), col))
    nxt = pl.BlockSpec((8, width), lambda i: (jnp.minimum((i + 1) * per, nb * per - 1), col))
    return prev, nxt


def _shift_down(x, prev8, d):
    n = x.shape[0]
    r = pltpu.roll(x, d, 0)
    p = pltpu.roll(prev8, d, 0)
    row8 = lax.broadcasted_iota(jnp.int32, (8, 1), 0)
    head = jnp.where(row8 < d, p, r[0:8])
    return jnp.concatenate([head, r[8:]], axis=0)


def _shift_up(x, next8, d):
    n = x.shape[0]
    r = pltpu.roll(x, n - d, 0)
    q = pltpu.roll(next8, 8 - d, 0)
    row8 = lax.broadcasted_iota(jnp.int32, (8, 1), 0)
    tail = jnp.where(row8 >= 8 - d, q, r[n - 8:])
    return jnp.concatenate([r[:n - 8], tail], axis=0)


def _conv_fwd(z, conv_w, conv_b):
    n = z.shape[0]
    nb = n // ROW_BLOCK
    prev_spec, next_spec = _halo_specs(LRU_W, nb)

    def body(cur_ref, prev_ref, next_ref, w_ref, b_ref, xc_ref):
        i = pl.program_id(0)
        cur = cur_ref[...]
        prev8 = prev_ref[...] * jnp.where(i > 0, 1.0, 0.0)
        next8 = next_ref[...] * jnp.where(i < nb - 1, 1.0, 0.0)
        w = [w_ref[pl.ds(k, 1), :] for k in range(4)]
        xc = (w[0] * _shift_down(cur, prev8, 2) + w[1] * _shift_down(cur, prev8, 1)
              + w[2] * cur + w[3] * _shift_up(cur, next8, 1) + b_ref[...])
        xc_ref[...] = xc

    return pl.pallas_call(
        body, name="conv_fwd", grid=(nb,),
        in_specs=[pl.BlockSpec((ROW_BLOCK, LRU_W), lambda i: (i, 0)), prev_spec, next_spec, conv_w.spec, conv_b.spec],
        out_specs=pl.BlockSpec((ROW_BLOCK, LRU_W), lambda i: (i, 0)),
        out_shape=_sds((n, LRU_W), F32),
        compiler_params=_params(("parallel",)),
    )(z, z, z, conv_w.array, conv_b.array)


def _conv_bwd(dxc_f, dxc_b, z, conv_w):
    n = z.shape[0]
    nb = n // ROW_BLOCK
    prev_spec, next_spec = _halo_specs(LRU_W, nb)
    row_spec = pl.BlockSpec((ROW_BLOCK, LRU_W), lambda i: (i, 0))

    def body(df_ref, dfp_ref, dfn_ref, db_ref, dbp_ref, dbn_ref, x_ref, xp_ref, xn_ref, w_ref,
             dx_ref, dw_ref, dbias_ref):
        i = pl.program_id(0)
        has_prev = jnp.where(i > 0, 1.0, 0.0)
        has_next = jnp.where(i < nb - 1, 1.0, 0.0)
        dxc = df_ref[...] + db_ref[...]
        dprev = (dfp_ref[...] + dbp_ref[...]) * has_prev
        dnext = (dfn_ref[...] + dbn_ref[...]) * has_next
        x = x_ref[...]
        xprev = xp_ref[...] * has_prev
        xnext = xn_ref[...] * has_next
        w = [w_ref[pl.ds(k, 1), :] for k in range(4)]
        dx_ref[...] = (w[0] * _shift_up(dxc, dnext, 2) + w[1] * _shift_up(dxc, dnext, 1)
                       + w[2] * dxc + w[3] * _shift_down(dxc, dprev, 1)).astype(BF16)
        dw = jnp.concatenate([
            jnp.sum(dxc * _shift_down(x, xprev, 2), axis=0, keepdims=True),
            jnp.sum(dxc * _shift_down(x, xprev, 1), axis=0, keepdims=True),
            jnp.sum(dxc * x, axis=0, keepdims=True),
            jnp.sum(dxc * _shift_up(x, xnext, 1), axis=0, keepdims=True),
            jnp.zeros((4, LRU_W), F32)], axis=0)
        _accumulate(dw_ref, dw, i == 0)
        _accumulate(dbias_ref, jnp.sum(dxc, axis=0, keepdims=True), i == 0)

    dx, dw, dbias = pl.pallas_call(
        body, name="conv_bwd", grid=(nb,),
        in_specs=[row_spec, prev_spec, next_spec, row_spec, prev_spec, next_spec, row_spec, prev_spec, next_spec,
                  conv_w.spec],
        out_specs=[row_spec, pl.BlockSpec((8, LRU_W), lambda i: (0, 0)), pl.BlockSpec((1, LRU_W), lambda i: (0, 0))],
        out_shape=[_sds((n, LRU_W), BF16), _sds((8, LRU_W), F32), _sds((1, LRU_W), F32)],
        compiler_params=_params(("arbitrary",)),
    )(dxc_f, dxc_f, dxc_f, dxc_b, dxc_b, dxc_b, z, z, z, conv_w.array)
    return dx, dw, dbias


def _mix_epilogue(h_f, h_b, o_f, o_b, z, head_norm):
    n = z.shape[0]
    tr = ROW_BLOCK
    spec = pl.BlockSpec((tr, 512), lambda i: (i, 0))

    def body(hf_ref, hb_ref, of_ref, ob_ref, gate_ref, gout_ref, w_ref, y_ref):
        gelu, _ = _gelu_and_grad(gate_ref[...])
        y_ref[:, 0:LRU_W] = ((hf_ref[...] + hb_ref[...]) * gelu).astype(BF16)
        o = of_ref[...] + ob_ref[...]
        gout = gout_ref[...]
        silu = gout * _sigmoid(gout)
        w = w_ref[...]
        for hd in range(GLA_HEADS):
            cs = slice(hd * GLA_DV, (hd + 1) * GLA_DV)
            oh = o[:, cs]
            on = oh * lax.rsqrt(jnp.mean(oh * oh, axis=-1, keepdims=True) + EPS)
            y_ref[:, LRU_W + hd * GLA_DV:LRU_W + (hd + 1) * GLA_DV] = (on * w[:, cs] * silu[:, cs]).astype(BF16)

    return pl.pallas_call(
        body, name="mix_epilogue", grid=(n // tr,),
        in_specs=[spec, spec, spec, spec, pl.BlockSpec((tr, 512), lambda i: (i, 1)),
                  pl.BlockSpec((tr, 512), lambda i: (i, 4)), head_norm.spec],
        out_specs=pl.BlockSpec((tr, D_MODEL), lambda i: (i, 0)),
        out_shape=_sds((n, D_MODEL), BF16),
        compiler_params=_params(("parallel",)),
    )(h_f, h_b, o_f, o_b, z, z, head_norm.array)


def _mix_epilogue_bwd(dymix, h_f, h_b, o_f, o_b, z, head_norm):
    n = z.shape[0]
    tr = ROW_BLOCK
    spec = pl.BlockSpec((tr, 512), lambda i: (i, 0))

    def body(dyl_ref, dyg_ref, hf_ref, hb_ref, of_ref, ob_ref, gate_ref, gout_ref, w_ref,
             dhs_ref, dgate_ref, do_ref, dgout_ref, dw_ref):
        i = pl.program_id(0)
        dyl = dyl_ref[...]
        gelu, dgelu = _gelu_and_grad(gate_ref[...])
        dhs_ref[...] = dyl * gelu
        dgate_ref[...] = (dyl * (hf_ref[...] + hb_ref[...]) * dgelu).astype(BF16)
        dyg = dyg_ref[...]
        o = of_ref[...] + ob_ref[...]
        gout = gout_ref[...]
        sg = _sigmoid(gout)
        silu = gout * sg
        dsilu = sg * (1.0 + gout * (1.0 - sg))
        w = w_ref[...]
        dws = []
        for hd in range(GLA_HEADS):
            cs = slice(hd * GLA_DV, (hd + 1) * GLA_DV)
            oh = o[:, cs]
            rs = lax.rsqrt(jnp.mean(oh * oh, axis=-1, keepdims=True) + EPS)
            on = oh * rs
            dy = dyg[:, cs]
            dgout_ref[:, cs] = (dy * on * w[:, cs] * dsilu[:, cs]).astype(BF16)
            dys = dy * silu[:, cs]
            dws.append(jnp.sum(dys * on, axis=0, keepdims=True))
            don = dys * w[:, cs]
            do_ref[:, cs] = (rs * (don - on * jnp.mean(don * on, axis=-1, keepdims=True))).astype(BF16)
        _accumulate(dw_ref, jnp.concatenate(dws, axis=1), i == 0)

    return pl.pallas_call(
        body, name="mix_epilogue_bwd", grid=(n // tr,),
        in_specs=[pl.BlockSpec((tr, 512), lambda i: (i, 0)), pl.BlockSpec((tr, 512), lambda i: (i, 1)),
                  spec, spec, spec, spec, pl.BlockSpec((tr, 512), lambda i: (i, 1)),
                  pl.BlockSpec((tr, 512), lambda i: (i, 4)), head_norm.spec],
        out_specs=[spec, spec, spec, spec, pl.BlockSpec((1, GLA_W), lambda i: (0, 0))],
        out_shape=[_sds((n, 512), F32)] + [_sds((n, 512), BF16)] * 3 + [_sds((1, GLA_W), F32)],
        compiler_params=_params(("arbitrary",)),
    )(dymix, dymix, h_f, h_b, o_f, o_b, z, z, head_norm.array)


def _out_proj(ymix, w_out, h, g, exchange=None):
    n, d = h.shape
    tr = _row_tile(n)
    spec = pl.BlockSpec((tr, d), lambda i: (i, 0))

    def body(y_ref, w_ref, h_ref, g_ref, mix_ref, hmid_ref):
        mix = jnp.dot(y_ref[...], w_ref[...], preferred_element_type=F32)
        mix_ref[...] = mix
        hmid_ref[...] = h_ref[...] + _rms_fwd(mix, g_ref[...])

    return _hosting_call(
        exchange, body, name="out_proj", grid=(n // tr,),
        in_specs=[spec, VMEM_SPEC, spec, g.spec],
        out_specs=[spec, spec],
        out_shape=[_sds((n, d), F32), _sds((n, d), F32)],
        scratch_shapes=[], compiler_params=_params(("arbitrary",), 44),
    )(ymix, w_out, h, g.array)


def _out_proj_bwd(dh_mid, mix, g, w_out, exchange=None):
    n, d = mix.shape
    tr = _row_tile(n)
    spec = pl.BlockSpec((tr, d), lambda i: (i, 0))

    def body(dh_ref, mix_ref, g_ref, w_ref, dmix_ref, dy_ref, dg_ref):
        i = pl.program_id(0)
        dmix, dg = _rms_bwd(mix_ref[...], g_ref[...], dh_ref[...])
        dmix = dmix.astype(BF16)
        dmix_ref[...] = dmix
        dy_ref[...] = _dot_nt(dmix, w_ref[...])
        _accumulate(dg_ref, dg, i == 0)

    return _hosting_call(
        exchange, body, name="out_proj_bwd", grid=(n // tr,),
        in_specs=[spec, spec, g.spec, VMEM_SPEC],
        out_specs=[spec, spec, pl.BlockSpec((1, d), lambda i: (0, 0))],
        out_shape=[_sds((n, d), BF16), _sds((n, d), F32), _sds((1, d), F32)],
        scratch_shapes=[], compiler_params=_params(("arbitrary",), 44),
    )(dh_mid, mix, g.array, w_out)


FF_SLAB = D_FF // N_DEV


def _relu_squared(up):
    return jnp.square(jnp.maximum(up.astype(F32), 0.0)).astype(BF16)


def _mlp_fwd(h_mid, g_pre, w_up, w_down, g_post, exchange=None):
    n, d = h_mid.shape
    tr = _row_tile(n, 384)
    spec = pl.BlockSpec((tr, d), lambda i: (i, 0))

    def body(h_ref, gpre_ref, wup_ref, wdn_ref, gpost_ref, hn_ref, up_ref, ff_ref, hout_ref):
        h = h_ref[...]
        hn = _rms_fwd(h, gpre_ref[...]).astype(BF16)
        hn_ref[...] = hn
        ff = jnp.zeros((tr, d), F32)
        for j in range(N_DEV):
            cs = slice(j * FF_SLAB, (j + 1) * FF_SLAB)
            up = jnp.dot(hn, wup_ref[j], preferred_element_type=F32).astype(BF16)
            up_ref[:, cs] = up
            ff = ff + jnp.dot(_relu_squared(up), wdn_ref[cs, :], preferred_element_type=F32)
        ff_ref[...] = ff
        hout_ref[...] = h + _rms_fwd(ff, gpost_ref[...])

    return _hosting_call(
        exchange, body, name="mlp_fwd", grid=(n // tr,),
        in_specs=[spec, g_pre.spec, VMEM_SPEC, VMEM_SPEC, g_post.spec],
        out_specs=[spec, pl.BlockSpec((tr, D_FF), lambda i: (i, 0)), spec, spec],
        out_shape=[_sds((n, d), BF16), _sds((n, D_FF), BF16), _sds((n, d), F32), _sds((n, d), F32)],
        scratch_shapes=[], compiler_params=_params(("arbitrary",), 52),
    )(h_mid, g_pre.array, w_up, w_down, g_post.array)


def _mlp_bwd(dh, ff, up, h_mid, g_pre, w_up, w_down, g_post, exchange=None):
    n, d = h_mid.shape
    tr = _row_tile(n, 384)
    spec = pl.BlockSpec((tr, d), lambda i: (i, 0))
    wide = pl.BlockSpec((tr, D_FF), lambda i: (i, 0))
    gspec = pl.BlockSpec((1, d), lambda i: (0, 0))

    def body(dh_ref, ff_ref, up_ref, h_ref, gpre_ref, wup_ref, wdn_ref, gpost_ref,
             dff_ref, dup_ref, dhmid_ref, dgpost_ref, dgpre_ref):
        i = pl.program_id(0)
        dh = dh_ref[...]
        dff, dgpost = _rms_bwd(ff_ref[...], gpost_ref[...], dh)
        dff = dff.astype(BF16)
        dff_ref[...] = dff
        dhn = jnp.zeros((tr, d), F32)
        for j in range(N_DEV):
            cs = slice(j * FF_SLAB, (j + 1) * FF_SLAB)
            relu = jnp.maximum(up_ref[:, cs].astype(F32), 0.0)
            dact = _dot_nt(dff, wdn_ref[cs, :])
            dup = (dact * 2.0 * relu).astype(BF16)
            dup_ref[:, cs] = dup
            dhn = dhn + _dot_nt(dup, wup_ref[j])
        dx, dgpre = _rms_bwd(h_ref[...], gpre_ref[...], dhn)
        dhmid_ref[...] = dh + dx
        _accumulate(dgpost_ref, dgpost, i == 0)
        _accumulate(dgpre_ref, dgpre, i == 0)

    return _hosting_call(
        exchange, body, name="mlp_bwd", grid=(n // tr,),
        in_specs=[spec, spec, wide, spec, g_pre.spec, VMEM_SPEC, VMEM_SPEC, g_post.spec],
        out_specs=[spec, wide, spec, gspec, gspec],
        out_shape=[_sds((n, d), BF16), _sds((n, D_FF), BF16), _sds((n, d), F32), _sds((1, d), F32), _sds((1, d), F32)],
        scratch_shapes=[], compiler_params=_params(("arbitrary",), 56),
    )(dh, ff, up, h_mid, g_pre.array, w_up, w_down, g_post.array)


def _in_proj_bwd(pieces, w_in, h, g, dh_mid, exchange=None):
    dxbr, dgate, dqk_f, dqk_b, dv_f, dv_b, dgout, dzg_f, dzg_b = pieces
    n, d = h.shape
    tr = _row_tile(n, 384)
    spec = pl.BlockSpec((tr, d), lambda i: (i, 0))
    s512 = pl.BlockSpec((tr, 512), lambda i: (i, 0))
    s128 = pl.BlockSpec((tr, LANES), lambda i: (i, 0))

    def body(a_ref, b_ref, cf_ref, cb_ref, df_ref, db_ref, e_ref, ff_ref, fb_ref, w_ref, h_ref, g_ref, dhm_ref,
             dz_ref, dh_ref, dg_ref):
        i = pl.program_id(0)
        real = (_row_ids(tr, i) >= PAD_ROWS).astype(F32)
        f32 = lambda ref: ref[...].astype(F32)
        dz = jnp.concatenate([f32(a_ref), f32(b_ref), f32(cf_ref) + f32(cb_ref), f32(df_ref) + f32(db_ref),
                              f32(e_ref), f32(ff_ref) + f32(fb_ref)], axis=1) * real
        dz = dz.astype(BF16)
        dz_ref[...] = dz
        dhn = _dot_nt(dz, w_ref[...])
        dx, dg = _rms_bwd(h_ref[...], g_ref[...], dhn)
        dh_ref[...] = (dhm_ref[...] + dx) * real
        _accumulate(dg_ref, dg, i == 0)

    return _hosting_call(
        exchange, body, name="in_proj_bwd", grid=(n // tr,),
        in_specs=[s512, s512, s512, s512, s512, s512, s512, s128, s128, VMEM_SPEC, spec, g.spec, spec],
        out_specs=[pl.BlockSpec((tr, Z_W), lambda i: (i, 0)), spec, pl.BlockSpec((1, d), lambda i: (0, 0))],
        out_shape=[_sds((n, Z_W), BF16), _sds((n, d), F32), _sds((1, d), F32)],
        scratch_shapes=[], compiler_params=_params(("arbitrary",), 48),
    )(dxbr, dgate, dqk_f, dqk_b, dv_f, dv_b, dgout, dzg_f, dzg_b, w_in, h, g.array, dh_mid)


def _matmul_tn(a, b, name, column_slabs=False, exchange=None, a_map=None):
    n, m = a.shape
    k = b.shape[1]
    tr = next(t for t in (2816, 1408, 768, 512, 256) if n % t == 0)
    tm, tk = _col_tile(m), _col_tile(k)
    steps = n // tr
    slab = k // N_DEV
    per_step = tk // slab if column_slabs else 1
    sub = next(t for t in (704, 768, 512, 256) if tr % t == 0)

    def body(a_ref, b_ref, o_ref, acc_ref, *mapped_ref):
        r = pl.program_id(2)
        if a_map is None:
            a_blk = a_ref[...]
        else:
            for c in range(tr // sub):
                rows = pl.ds(c * sub, sub)
                mapped_ref[0][rows, :] = a_map(a_ref[rows, :])
            a_blk = mapped_ref[0][...]
        _accumulate(acc_ref, _dot_tn(a_blk, b_ref[...]), r == 0)

        @pl.when(r == steps - 1)
        def _():
            if column_slabs:
                for j in range(per_step):
                    o_ref[j] = acc_ref[:, j * slab:(j + 1) * slab].astype(BF16)
            else:
                o_ref[...] = acc_ref[...].astype(BF16)

    if column_slabs:
        out_spec = pl.BlockSpec((per_step, tm, slab), lambda mi, ki, r: (ki, mi, 0))
        out_shape = _sds((N_DEV, m, slab), BF16)
    else:
        out_spec = pl.BlockSpec((tm, tk), lambda mi, ki, r: (mi, ki))
        out_shape = _sds((m, k), BF16)
    outs = _hosting_call(
        exchange, body, name=name, grid=(m // tm, k // tk, steps),
        in_specs=[pl.BlockSpec((tr, tm), lambda mi, ki, r: (r, mi)), pl.BlockSpec((tr, tk), lambda mi, ki, r: (r, ki))],
        out_specs=[out_spec], out_shape=[out_shape],
        scratch_shapes=[pltpu.VMEM((tm, tk), F32)] + ([] if a_map is None else [pltpu.VMEM((tr, tm), BF16)]),
        compiler_params=_params(("arbitrary", "arbitrary", "arbitrary"), 52),
    )(a, b)
    return outs[0] if exchange is None else outs


def _loss_and_grad(h_out, target):
    n, d = h_out.shape
    tr = ROW_BLOCK
    first = (PAD_ROWS + N_META) // tr

    def body(h_ref, t_ref, dh_ref, loss_ref):
        i = pl.program_id(0)
        real = jnp.where(i >= first, 1.0, 0.0)
        diff = (h_ref[...] - t_ref[...]) * real
        dh_ref[...] = diff * (1.0 / d)
        part = 0.5 * jnp.sum(jnp.mean(diff * diff, axis=-1, keepdims=True), axis=0, keepdims=True)
        _accumulate(loss_ref, jnp.broadcast_to(part, (1, LANES)), i == 0)

    return pl.pallas_call(
        body, name="loss_and_grad", grid=(n // tr,),
        in_specs=[pl.BlockSpec((tr, d), lambda i: (i, 0)), pl.BlockSpec((tr, d), lambda i: (jnp.maximum(i - first, 0), 0))],
        out_specs=[pl.BlockSpec((tr, d), lambda i: (i, 0)), pl.BlockSpec((1, LANES), lambda i: (0, 0))],
        out_shape=[_sds((n, d), F32), _sds((1, LANES), F32)],
        compiler_params=_params(("arbitrary",)),
    )(h_out, target)


def _scan_block(a, u, h_in, reverse):
    n = a.shape[0]
    row = lax.broadcasted_iota(jnp.int32, (n, 1), 0)
    d = 1
    while d < n:
        shift = n - d if reverse else d
        keep = (row < n - d) if reverse else (row >= d)
        a_s = pltpu.roll(a, shift, 0)
        u_s = pltpu.roll(u, shift, 0)
        u = jnp.where(keep, a * u_s + u, u)
        a = jnp.where(keep, a * a_s, a)
        d *= 2
    return a * h_in + u


def _lru_gates(xc, wcat_ref, bias_ref, lam_ref):
    nl = -lam_ref[...]
    nsp = -LRU_C * (jnp.maximum(nl, 0.0) + jnp.log(1.0 + jnp.exp(-jnp.abs(nl))))
    pre = _dot(xc, wcat_ref[...]) + bias_ref[...]
    r = _sigmoid(pre[:, :LRU_W])
    ig = _sigmoid(pre[:, LRU_W:])
    log_a = r * nsp
    a = jnp.exp(log_a)
    m2 = _one_minus_square(a, log_a)
    inv_m = lax.rsqrt(jnp.maximum(m2, 1e-30))
    return r, ig, a, m2 * inv_m, inv_m, nsp


def _lru_scan(xc, wcat, bias, lam, reverse, exchange=None):
    n = xc.shape[0]
    nb = n // ROW_BLOCK
    order = (lambda i: nb - 1 - i) if reverse else (lambda i: i)
    spec = pl.BlockSpec((ROW_BLOCK, LRU_W), lambda i: (order(i), 0))
    edge = 0 if reverse else ROW_BLOCK - 1

    def body(xc_ref, wcat_ref, bias_ref, lam_ref, h_ref, carry_ref):
        i = pl.program_id(0)

        @pl.when(i == 0)
        def _():
            carry_ref[...] = jnp.zeros_like(carry_ref)

        xc = xc_ref[...]
        r, ig, a, m, _, _ = _lru_gates(xc, wcat_ref, bias_ref, lam_ref)
        u = jnp.where(_row_ids(ROW_BLOCK, order(i)) >= PAD_ROWS, m * (ig * xc), 0.0)
        h_ref[...] = _scan_block(a, u, carry_ref[0:1, :], reverse)
        carry_ref[0:1, :] = h_ref[pl.ds(edge, 1), :]

    return _hosting_call(
        exchange, body, name="lru_scan_b" if reverse else "lru_scan_f", grid=(nb,),
        in_specs=[spec, wcat.spec, bias.spec, lam.spec],
        out_specs=[spec],
        out_shape=[_sds((n, LRU_W), F32)],
        scratch_shapes=[pltpu.VMEM((8, LRU_W), F32)],
        compiler_params=_params(("arbitrary",)),
    )(xc, wcat.array, bias.array, lam.array)


def _lru_scan_bwd(dhs, xc, h, wcat, bias, lam, reverse, exchange=None):
    n = xc.shape[0]
    nb = n // ROW_BLOCK
    per = ROW_BLOCK // 8
    order = (lambda i: i) if reverse else (lambda i: nb - 1 - i)
    spec = pl.BlockSpec((ROW_BLOCK, LRU_W), lambda i: (order(i), 0))
    if reverse:
        halo = pl.BlockSpec((8, LRU_W), lambda i: (jnp.minimum((order(i) + 1) * per, nb * per - 1), 0))
    else:
        halo = pl.BlockSpec((8, LRU_W), lambda i: (jnp.maximum(order(i) * per - 1, 0), 0))
    edge = ROW_BLOCK - 1 if reverse else 0

    def body(dhs_ref, xc_ref, h_ref, halo_ref, wcat_ref, bias_ref, lam_ref,
             dxc_ref, dw_ref, db_ref, dlam_ref, cdh_ref, ca_ref, tmp_ref):
        i = pl.program_id(0)
        ib = order(i)

        @pl.when(i == 0)
        def _():
            cdh_ref[...] = jnp.zeros_like(cdh_ref)
            ca_ref[...] = jnp.zeros_like(ca_ref)

        xc = xc_ref[...]
        r, ig, a, m, inv_m, nsp = _lru_gates(xc, wcat_ref, bias_ref, lam_ref)
        row = lax.broadcasted_iota(jnp.int32, (ROW_BLOCK, 1), 0)
        if reverse:
            coef = jnp.where(row == 0, ca_ref[0:1, :], pltpu.roll(a, 1, 0))
            h_nb = jnp.where(row == ROW_BLOCK - 1, halo_ref[0:1, :] * jnp.where(ib < nb - 1, 1.0, 0.0),
                             pltpu.roll(h_ref[...], ROW_BLOCK - 1, 0))
        else:
            coef = jnp.where(row == ROW_BLOCK - 1, ca_ref[0:1, :], pltpu.roll(a, ROW_BLOCK - 1, 0))
            h_nb = jnp.where(row == 0, halo_ref[7:8, :] * jnp.where(ib > 0, 1.0, 0.0), pltpu.roll(h_ref[...], 1, 0))
        dh = _scan_block(coef, dhs_ref[...], cdh_ref[0:1, :], not reverse)
        tmp_ref[...] = dh
        cdh_ref[0:1, :] = tmp_ref[pl.ds(edge, 1), :]
        tmp_ref[...] = a
        ca_ref[0:1, :] = tmp_ref[pl.ds(edge, 1), :]

        du = jnp.where(_row_ids(ROW_BLOCK, ib) >= PAD_ROWS, dh, 0.0)
        da = dh * h_nb
        dm = du * (ig * xc)
        di = du * (m * xc)
        dlog_a = da * a - dm * (a * a) * inv_m
        dr = dlog_a * nsp
        dpre = jnp.concatenate([dr * r * (1.0 - r), di * ig * (1.0 - ig)], axis=1)
        dxc_ref[...] = du * (m * ig) + _dot_nt(dpre, wcat_ref[...])
        _accumulate(dw_ref, _dot_tn(xc, dpre), i == 0)
        _accumulate(db_ref, jnp.sum(dpre, axis=0, keepdims=True), i == 0)
        _accumulate(dlam_ref, jnp.sum(dlog_a * r, axis=0, keepdims=True), i == 0)

        @pl.when(i == nb - 1)
        def _():
            dlam_ref[...] = dlam_ref[...] * (LRU_C * _sigmoid(-lam_ref[...]))

    return _hosting_call(
        exchange, body, name="lru_scan_bwd_b" if reverse else "lru_scan_bwd_f", grid=(nb,),
        in_specs=[spec, spec, spec, halo, wcat.spec, bias.spec, lam.spec],
        out_specs=[spec, pl.BlockSpec((LRU_W, 2 * LRU_W), lambda i: (0, 0)),
                   pl.BlockSpec((1, 2 * LRU_W), lambda i: (0, 0)), pl.BlockSpec((1, LRU_W), lambda i: (0, 0))],
        out_shape=[_sds((n, LRU_W), F32), _sds((LRU_W, 2 * LRU_W), F32), _sds((1, 2 * LRU_W), F32), _sds((1, LRU_W), F32)],
        scratch_shapes=[pltpu.VMEM((8, LRU_W), F32), pltpu.VMEM((8, LRU_W), F32), pltpu.VMEM((ROW_BLOCK, LRU_W), F32)],
        compiler_params=_params(("arbitrary",)),
    )(dhs, xc, h, h, wcat.array, bias.array, lam.array)


def _gla_rows(n):
    return 768 if n % 768 == 0 else ROW_BLOCK


def _gla_masks(reverse):
    t = lax.broadcasted_iota(jnp.int32, (CHUNK, CHUNK), 0)
    s = lax.broadcasted_iota(jnp.int32, (CHUNK, CHUNK), 1)
    if reverse:
        return (s >= t).astype(F32), s > t
    return (s <= t).astype(F32), s <= t


def _gla_gate(zg, wg_ref, bg_ref):
    pre = _dot(zg, wg_ref[...]) + bg_ref[...]
    g = (jnp.minimum(pre, 0.0) - jnp.log(1.0 + jnp.exp(-jnp.abs(pre)))) * (1.0 / GATE_NORM)
    return pre, g


def _gla_decays(gc, tri):
    b = jnp.dot(tri, gc, precision=lax.Precision.HIGHEST, preferred_element_type=F32)
    b_last = jnp.sum(gc, axis=0, keepdims=True)
    return jnp.exp(b), jnp.exp(-b), jnp.exp(b_last - b), jnp.exp(b_last)


def _gla_scan(z, wg, bg, reverse, exchange=None):
    n = z.shape[0]
    rb = _gla_rows(n)
    nb = n // rb
    cpb = rb // CHUNK
    order = (lambda i: nb - 1 - i) if reverse else (lambda i: i)
    chunks = range(cpb - 1, -1, -1) if reverse else range(cpb)

    def body(qk_ref, v_ref, zg_ref, wg_ref, bg_ref, o_ref, sall_ref, s_ref):
        i = pl.program_id(0)

        @pl.when(i == 0)
        def _():
            s_ref[...] = jnp.zeros_like(s_ref)

        tri, mask = _gla_masks(reverse)
        _, g = _gla_gate(zg_ref[...], wg_ref, bg_ref)
        heads = range(GLA_HEADS)
        ks = [slice(hd * GLA_DK, (hd + 1) * GLA_DK) for hd in heads]
        vs = [slice(hd * GLA_DV, (hd + 1) * GLA_DV) for hd in heads]
        qh, kb, v, el, p, intra, kv = {}, {}, {}, {}, {}, {}, {}
        for c in chunks:
            rows = slice(c * CHUNK, (c + 1) * CHUNK)
            eb, enb, ebl, el[c] = _gla_decays(g[rows], tri)
            qk = qk_ref[rows, :]
            q_all = (qk[:, :GLA_QK] * (GLA_DK ** -0.5) * eb).astype(BF16)
            k_all = (qk[:, GLA_QK:] * enb).astype(BF16)
            kb_all = (qk[:, GLA_QK:] * ebl).astype(BF16)
            v_all = v_ref[rows, :].astype(BF16)
            for hd in heads:
                qh[c, hd], kb[c, hd], v[c, hd] = q_all[:, ks[hd]], kb_all[:, ks[hd]], v_all[:, vs[hd]]
                p[c, hd] = _dot_nt(qh[c, hd], k_all[:, ks[hd]])
        for c in chunks:
            for hd in heads:
                intra[c, hd] = _dot(jnp.where(mask, p[c, hd], 0.0), v[c, hd])
                kv[c, hd] = _dot_tn(v[c, hd], kb[c, hd])
        state = [s_ref[:, ks[hd]] for hd in heads]
        for c in chunks:
            rows = slice(c * CHUNK, (c + 1) * CHUNK)
            for hd in heads:
                sall_ref[c, :, ks[hd]] = state[hd]
                o_ref[rows, vs[hd]] = intra[c, hd] + _dot_nt(qh[c, hd], state[hd])
                state[hd] = state[hd] * el[c][:, ks[hd]] + kv[c, hd]
        for hd in heads:
            s_ref[:, ks[hd]] = state[hd]

    return _hosting_call(
        exchange, body, name="gla_scan_b" if reverse else "gla_scan_f", grid=(nb,),
        in_specs=[pl.BlockSpec((rb, 512), lambda i: (order(i), 2)), pl.BlockSpec((rb, 512), lambda i: (order(i), 3)),
                  pl.BlockSpec((rb, LANES), lambda i: (order(i), ZG_COL_BLOCK)), wg.spec, bg.spec],
        out_specs=[pl.BlockSpec((rb, GLA_W), lambda i: (order(i), 0)),
                   pl.BlockSpec((cpb, GLA_DV, GLA_QK), lambda i: (order(i), 0, 0))],
        out_shape=[_sds((n, GLA_W), F32), _sds((n // CHUNK, GLA_DV, GLA_QK), F32)],
        scratch_shapes=[pltpu.VMEM((GLA_DV, GLA_QK), F32)],
        compiler_params=_params(("arbitrary",)),
    )(z, z, z, wg.array, bg.array)


def _gla_scan_bwd(do, z, states, wg, bg, reverse, exchange=None):
    n = z.shape[0]
    rb = _gla_rows(n)
    nb = n // rb
    cpb = rb // CHUNK
    order = (lambda i: i) if reverse else (lambda i: nb - 1 - i)
    chunks = range(cpb) if reverse else range(cpb - 1, -1, -1)

    def body(do_ref, qk_ref, v_ref, zg_ref, sall_ref, wg_ref, bg_ref,
             dqk_ref, dv_ref, dzg_ref, dwg_ref, dbg_ref, ds_ref):
        i = pl.program_id(0)

        @pl.when(i == 0)
        def _():
            ds_ref[...] = jnp.zeros_like(ds_ref)

        tri, mask = _gla_masks(reverse)
        tri_t, _ = _gla_masks(not reverse)
        zg = zg_ref[...]
        pre, g = _gla_gate(zg, wg_ref, bg_ref)
        heads = range(GLA_HEADS)
        ks = [slice(hd * GLA_DK, (hd + 1) * GLA_DK) for hd in heads]
        vs = [slice(hd * GLA_DV, (hd + 1) * GLA_DV) for hd in heads]
        dec, full, qh, kh, kb, v, dout, p, dp = {}, {}, {}, {}, {}, {}, {}, {}, {}
        for c in chunks:
            rows = slice(c * CHUNK, (c + 1) * CHUNK)
            dec[c] = _gla_decays(g[rows], tri)
            eb, enb, ebl, _ = dec[c]
            qk = qk_ref[rows, :]
            q_f = qk[:, :GLA_QK] * (GLA_DK ** -0.5) * eb
            k_f = qk[:, GLA_QK:] * enb
            kb_f = qk[:, GLA_QK:] * ebl
            full[c] = (q_f, k_f, kb_f)
            q_all, k_all, kb_all = q_f.astype(BF16), k_f.astype(BF16), kb_f.astype(BF16)
            v_all, do_all = v_ref[rows, :].astype(BF16), do_ref[rows, :].astype(BF16)
            for hd in heads:
                qh[c, hd], kh[c, hd], kb[c, hd] = q_all[:, ks[hd]], k_all[:, ks[hd]], kb_all[:, ks[hd]]
                v[c, hd], dout[c, hd] = v_all[:, vs[hd]], do_all[:, vs[hd]]
                p[c, hd] = _dot_nt(qh[c, hd], kh[c, hd])
                dp[c, hd] = _dot_nt(dout[c, hd], v[c, hd])
        dv_i, dqh, dkh, dsq, state = {}, {}, {}, {}, {}
        for c in chunks:
            for hd in heads:
                pm = jnp.where(mask, p[c, hd], 0.0).astype(BF16)
                dpm = jnp.where(mask, dp[c, hd], 0.0).astype(BF16)
                state[c, hd] = sall_ref[c, :, ks[hd]]
                dv_i[c, hd] = _dot_tn(pm, dout[c, hd])
                dqh[c, hd] = _dot(dpm, kh[c, hd]) + _dot(dout[c, hd], state[c, hd])
                dkh[c, hd] = _dot_tn(dpm, qh[c, hd])
                dsq[c, hd] = _dot_tn(dout[c, hd], qh[c, hd])
        dstate = [ds_ref[:, ks[hd]] for hd in heads]
        dkb, sds = {}, {}
        for c in chunks:
            rows = slice(c * CHUNK, (c + 1) * CHUNK)
            el = dec[c][3]
            for hd in heads:
                dv_ref[rows, vs[hd]] = (dv_i[c, hd] + _dot_nt(kb[c, hd], dstate[hd])).astype(BF16)
                dkb[c, hd] = _dot(v[c, hd], dstate[hd])
                sds[c, hd] = jnp.sum(state[c, hd] * dstate[hd], axis=0, keepdims=True)
                dstate[hd] = dstate[hd] * el[:, ks[hd]] + dsq[c, hd]
        for hd in heads:
            ds_ref[:, ks[hd]] = dstate[hd]
        dgs = [None] * cpb
        for c in chunks:
            rows = slice(c * CHUNK, (c + 1) * CHUNK)
            eb, enb, ebl, el = dec[c]
            q_f, k_f, kb_f = full[c]
            dqh_c = jnp.concatenate([dqh[c, hd] for hd in heads], axis=1)
            dkh_c = jnp.concatenate([dkh[c, hd] for hd in heads], axis=1)
            dkb_c = jnp.concatenate([dkb[c, hd] for hd in heads], axis=1)
            sds_c = jnp.concatenate([sds[c, hd] for hd in heads], axis=1)
            dqk_ref[rows, :] = jnp.concatenate([dqh_c * eb * (GLA_DK ** -0.5), dkh_c * enb + dkb_c * ebl], axis=1).astype(BF16)
            dkb_kb = dkb_c * kb_f
            db = dqh_c * q_f - dkh_c * k_f - dkb_kb
            db_last = el * sds_c + jnp.sum(dkb_kb, axis=0, keepdims=True)
            dgs[c] = jnp.dot(tri_t, db, precision=lax.Precision.HIGHEST, preferred_element_type=F32) + db_last
        dg = jnp.concatenate(dgs, axis=0)
        dpre = dg * _sigmoid(-pre) * (1.0 / GATE_NORM)
        dzg_ref[...] = _dot_nt(dpre, wg_ref[...]).astype(BF16)
        _accumulate(dwg_ref, _dot_tn(zg, dpre), i == 0)
        _accumulate(dbg_ref, jnp.sum(dpre, axis=0, keepdims=True), i == 0)

    return _hosting_call(
        exchange, body, name="gla_scan_bwd_b" if reverse else "gla_scan_bwd_f", grid=(nb,),
        in_specs=[pl.BlockSpec((rb, GLA_W), lambda i: (order(i), 0)),
                  pl.BlockSpec((rb, 512), lambda i: (order(i), 2)), pl.BlockSpec((rb, 512), lambda i: (order(i), 3)),
                  pl.BlockSpec((rb, LANES), lambda i: (order(i), ZG_COL_BLOCK)),
                  pl.BlockSpec((cpb, GLA_DV, GLA_QK), lambda i: (order(i), 0, 0)), wg.spec, bg.spec],
        out_specs=[pl.BlockSpec((rb, 512), lambda i: (order(i), 0)), pl.BlockSpec((rb, 512), lambda i: (order(i), 0)),
                   pl.BlockSpec((rb, LANES), lambda i: (order(i), 0)),
                   pl.BlockSpec((LANES, GLA_QK), lambda i: (0, 0)), pl.BlockSpec((1, GLA_QK), lambda i: (0, 0))],
        out_shape=[_sds((n, 512), BF16), _sds((n, 512), BF16), _sds((n, LANES), BF16), _sds((LANES, GLA_QK), F32),
                   _sds((1, GLA_QK), F32)],
        scratch_shapes=[pltpu.VMEM((GLA_DV, GLA_QK), F32)],
        compiler_params=_params(("arbitrary",)),
    )(do, z, z, z, states, wg.array, bg.array)


NORM_NAMES = ("norm_mix_pre", "norm_mix_post", "norm_mlp_pre", "norm_mlp_post")
VEC512_NAMES = ("conv_b", "lru_ba_f", "lru_bx_f", "lru_lambda_f", "lru_ba_b", "lru_bx_b", "lru_lambda_b", "gla_head_norm")
VEC256_NAMES = ("gla_bg_f", "gla_bg_b")
LRU_MAT_NAMES = ("lru_wa_f", "lru_wx_f", "lru_wa_b", "lru_wx_b")
DIRS = ("f", "b")


def _prepare_params(w, gathered, depth):
    row_names = NORM_NAMES + ("conv_b", "gla_head_norm")
    ins = ([w[nm] for nm in row_names] + [w["lru_ba_" + d] for d in DIRS] + [w["lru_bx_" + d] for d in DIRS]
           + [w["lru_lambda_" + d] for d in DIRS] + [w["gla_bg_" + d] for d in DIRS]
           + [w["lru_wa_" + d].reshape(depth, LRU_W, LRU_HD) for d in DIRS]
           + [w["lru_wx_" + d].reshape(depth, LRU_W, LRU_HD) for d in DIRS]
           + [gathered["conv_w"], gathered["gla_wg_f"], gathered["gla_wg_b"], gathered["meta_tokens"]])
    n_rows = len(row_names)

    def body(*refs):
        rows_in = refs[:n_rows]
        ba, bx, lam, bg, wa, wx = (refs[n_rows + 2 * t:n_rows + 2 * t + 2] for t in range(6))
        convw_g, wgf_g, wgb_g, meta_g = refs[n_rows + 12:n_rows + 16]
        outs = refs[n_rows + 16:]
        rows_out = outs[:n_rows]
        convw, wcat, bias, lam_o, wg, bg_o, meta = outs[n_rows:]
        for l in range(depth):
            for src, dst in zip(rows_in, rows_out):
                dst[l] = src[pl.ds(l, 1), :]
            convw[l] = jnp.zeros((8, LRU_W), F32)
            for j in range(N_DEV):
                convw[l, 0:4, j * 64:(j + 1) * 64] = convw_g[j, l]
            for d in range(2):
                wcat[l, d] = jnp.zeros((LRU_W, 2 * LRU_W), BF16)
                for hd in range(LRU_HEADS):
                    rs = slice(hd * LRU_HD, (hd + 1) * LRU_HD)
                    wcat[l, d, rs, hd * LRU_HD:(hd + 1) * LRU_HD] = wa[d][l, rs, :].astype(BF16)
                    wcat[l, d, rs, LRU_W + hd * LRU_HD:LRU_W + (hd + 1) * LRU_HD] = wx[d][l, rs, :].astype(BF16)
                bias[l, d, :, 0:LRU_W] = ba[d][pl.ds(l, 1), :]
                bias[l, d, :, LRU_W:2 * LRU_W] = bx[d][pl.ds(l, 1), :]
                lam_o[l, d] = lam[d][pl.ds(l, 1), :]
                bg_o[l, d] = bg[d][pl.ds(l, 1), :]
                wg[l, d] = jnp.zeros((LANES, GLA_QK), BF16)
                src = wgf_g if d == 0 else wgb_g
                for j in range(N_DEV):
                    wg[l, d, d * GLA_RANK:(d + 1) * GLA_RANK, j * 32:(j + 1) * 32] = src[j, l].astype(BF16)
        for j in range(N_DEV):
            meta[:, j * LANES:(j + 1) * LANES] = meta_g[j]

    out_shape = ([_sds((depth, 1, w[nm].shape[1]), F32) for nm in row_names]
                 + [_sds((depth, 8, LRU_W), F32), _sds((depth, 2, LRU_W, 2 * LRU_W), BF16), _sds((depth, 2, 1, 2 * LRU_W), F32),
                    _sds((depth, 2, 1, LRU_W), F32), _sds((depth, 2, LANES, GLA_QK), BF16), _sds((depth, 2, 1, GLA_QK), F32),
                    _sds((N_META, D_MODEL), F32)])
    outs = pl.pallas_call(
        body, name="prepare_params", in_specs=[VMEM_SPEC] * len(ins), out_specs=[VMEM_SPEC] * len(out_shape),
        out_shape=out_shape, compiler_params=_params(None, 32),
    )(*ins)
    prepared = dict(zip(row_names, outs[:n_rows]))
    prepared.update(zip(("conv_w", "wcat", "lru_bias", "lru_lam", "wg", "gla_bg", "meta_tokens"), outs[n_rows:]))
    return prepared


class _Outbox:
    def __init__(self, on_complete):
        self.pending, self.on_complete = {}, on_complete

    def put(self, key, array, src, landing_shape):
        self.pending[key] = dict(array=array, src=src, landing=landing_shape, groups=list(range(len(PEER_GROUPS))))

    def exchange(self, wanted=None):
        ex, tickets = _Exchange(), []
        for key, item in self.pending.items():
            groups = [g for g in item["groups"] if wanted is None or (key[0], g) in wanted]
            out = None
            for g in groups:
                landing = item["landing"] if out is None else out
                out = ex.add(item["array"], item["src"], landing, _slab, peers=PEER_GROUPS[g], local=(g == 0))
                item["groups"].remove(g)
            if groups:
                tickets.append((key, out))
        return ex, tickets

    def store(self, tickets, landed):
        for key, out in tickets:
            item = self.pending[key]
            item["landing"] = landed[out]
            if not item["groups"]:
                del self.pending[key]
                self.on_complete(key, landed[out])


def _install_weight(p):
    def install(key, g):
        nm, l = key
        if nm == "w_in":
            g = jnp.pad(jnp.concatenate([g[j] for j in range(N_DEV)], axis=1), ((0, 0), (0, Z_W - D_IN)))
        elif nm == "w_out":
            g = g.reshape(D_MODEL, D_MODEL)
        elif nm == "w_mlp_down":
            g = g.reshape(D_FF, D_MODEL)
        p.setdefault(nm, {})[l] = g
    return install


def _request_weight(gather, shards, nm, l):
    gather.put((nm, l), shards[nm], _layer_of(l), _sds((N_DEV,) + shards[nm].shape[1:], BF16))


def _layer_fwd(h, l, p, gather, shards, depth):
    lp = lambda name, *index: _LayerParam(p[name], l, *index)
    s = dict(h=h)

    def hosted(fn, wanted, *args):
        ex, tickets = gather.exchange(wanted)
        outs = fn(*args, ex)
        own = len(outs) - len(ex.landings)
        gather.store(tickets, outs[own:])
        return outs[:own]

    _request_weight(gather, shards, "w_mlp_up", l)
    _request_weight(gather, shards, "w_mlp_down", l)
    s["hn"], s["z"] = hosted(_norm_in_proj, [("w_mlp_up", 0), ("w_out", 0)], h, lp("norm_mix_pre"), p["w_in"][l])
    s["xc"] = _conv_fwd(s["z"], lp("conv_w"), lp("conv_b"))
    plan = {"f": ([("w_mlp_up", 1), ("w_out", 1)], [("w_mlp_up", 2), ("w_out", 2)]),
            "b": ([("w_mlp_down", 0)], [("w_mlp_down", 1)])}
    for d, name in enumerate(DIRS):
        s["h_" + name], = hosted(_lru_scan, plan[name][0], s["xc"], lp("wcat", d), lp("lru_bias", d), lp("lru_lam", d), d == 1)
        s["o_" + name], s["s_" + name] = hosted(_gla_scan, plan[name][1], s["z"], lp("wg", d), lp("gla_bg", d), d == 1)
    s["ymix"] = _mix_epilogue(s["h_f"], s["h_b"], s["o_f"], s["o_b"], s["z"], lp("gla_head_norm"))
    s["mix"], s["h_mid"] = hosted(_out_proj, [("w_mlp_down", 2)], s["ymix"], p["w_out"][l], h, lp("norm_mix_post"))
    if l + 1 < depth:
        _request_weight(gather, shards, "w_in", l + 1)
        _request_weight(gather, shards, "w_out", l + 1)
    s["hn2"], s["up"], s["ff"], h_out = hosted(
        _mlp_fwd, None, s["h_mid"], lp("norm_mlp_pre"), p["w_mlp_up"][l], p["w_mlp_down"][l], lp("norm_mlp_post"))
    return h_out, s


def _layer_bwd(dh_out, l, p, s, outbox):
    lp = lambda name, *index: _LayerParam(p[name], l, *index)
    g = {}

    def hosted(fn, wanted, *args):
        ex, tickets = outbox.exchange(wanted)
        outs = fn(*args, ex)
        own = len(outs) - len(ex.landings)
        outbox.store(tickets, outs[own:])
        return outs[:own]

    d_ff, dup, dh_mid, g["norm_mlp_post"], g["norm_mlp_pre"] = hosted(
        _mlp_bwd, None, dh_out, s["ff"], s["up"], s["h_mid"], lp("norm_mlp_pre"), p["w_mlp_up"][l], p["w_mlp_down"][l],
        lp("norm_mlp_post"))
    _send_grad(outbox, "w_mlp_down", l, _matmul_tn(s["up"], d_ff, "grad_w_down", a_map=_relu_squared)
               .reshape(N_DEV, D_FF // N_DEV, D_MODEL))
    _send_grad(outbox, "w_mlp_up", l, _matmul_tn(s["hn2"], dup, "grad_w_up", column_slabs=True))
    dmix, dymix, g["norm_mix_post"] = hosted(_out_proj_bwd, [("w_mlp_down", 0)], dh_mid, s["mix"], lp("norm_mix_post"),
                                             p["w_out"][l])
    grad_w_out = _matmul_tn(s["ymix"], dmix, "grad_w_out").reshape(N_DEV, D_MODEL // N_DEV, D_MODEL)
    dhs, dgate, do, dgout, g["gla_head_norm"] = _mix_epilogue_bwd(
        dymix, s["h_f"], s["h_b"], s["o_f"], s["o_b"], s["z"], lp("gla_head_norm"))
    plan = {"f": ([("w_mlp_down", 1)], [("w_mlp_down", 2), ("w_mlp_up", 0)]), "b": ([("w_mlp_up", 1)], [("w_mlp_up", 2)])}
    dqk, dv, dzg, dxc = {}, {}, {}, {}
    for d, name in enumerate(DIRS):
        dqk[name], dv[name], dzg[name], g["wg_" + name], g["gla_bg_" + name] = hosted(
            _gla_scan_bwd, plan[name][0], do, s["z"], s["s_" + name], lp("wg", d), lp("gla_bg", d), d == 1)
        dxc[name], g["wcat_" + name], g["lru_bias_" + name], g["lru_lambda_" + name] = hosted(
            _lru_scan_bwd, plan[name][1], dhs, s["xc"], s["h_" + name], lp("wcat", d), lp("lru_bias", d), lp("lru_lam", d),
            d == 1)
    _send_grad(outbox, "w_out", l, grad_w_out)
    dxbr, g["conv_w"], g["conv_b"] = _conv_bwd(dxc["f"], dxc["b"], s["z"], lp("conv_w"))
    dz, dh_in, g["norm_mix_pre"] = hosted(
        _in_proj_bwd, [("w_out", group) for group in range(len(PEER_GROUPS))],
        (dxbr, dgate, dqk["f"], dqk["b"], dv["f"], dv["b"], dgout, dzg["f"], dzg["b"]),
        p["w_in"][l], s["h"], lp("norm_mix_pre"), dh_mid)
    return dh_in, g, dz


def _send_grad(outbox, nm, l, slabs):
    outbox.put((nm, l), slabs, _slab, _sds(slabs.shape, slabs.dtype))


def _w_in_slabs(grad_w_in):
    shard = D_IN // N_DEV
    return jnp.stack([grad_w_in[:, j * shard:(j + 1) * shard] for j in range(N_DEV)])


def _folded_block(hd):
    return slice((hd // 2) * LRU_HD, (hd // 2 + 1) * LRU_HD), slice((hd % 2) * LRU_HD, (hd % 2 + 1) * LRU_HD)


def _pack_small_grads(grads, dh0, depth):
    per_layer = ("norm_mix_pre", "norm_mix_post", "norm_mlp_pre", "norm_mlp_post", "conv_b", "gla_head_norm",
                 "lru_bias_f", "lru_bias_b", "lru_lambda_f", "lru_lambda_b", "gla_bg_f", "gla_bg_b",
                 "wcat_f", "wcat_b", "conv_w", "wg_f", "wg_b")
    ins = [grads[l][nm] for l in range(depth) for nm in per_layer]
    k = len(per_layer)
    meta_rows = PAD_ROWS // N_META

    def body(*refs):
        g = [dict(zip(per_layer, refs[l * k:(l + 1) * k])) for l in range(depth)]
        dh0_ref = refs[depth * k]
        norms, v512, v256, mats, convw, wgf, wgb, meta = refs[depth * k + 1:]
        v256[...] = jnp.zeros_like(v256)
        for l in range(depth):
            for p_, nm in enumerate(NORM_NAMES):
                norms[pl.ds(2 * p_ + l, 1), :] = g[l][nm][...]
            rows512 = [g[l]["conv_b"][...], g[l]["lru_bias_f"][:, 0:LRU_W], g[l]["lru_bias_f"][:, LRU_W:2 * LRU_W],
                       g[l]["lru_lambda_f"][...], g[l]["lru_bias_b"][:, 0:LRU_W], g[l]["lru_bias_b"][:, LRU_W:2 * LRU_W],
                       g[l]["lru_lambda_b"][...], g[l]["gla_head_norm"][...]]
            for p_, row in enumerate(rows512):
                v512[pl.ds(2 * p_ + l, 1), :] = row
            for p_, nm in enumerate(("gla_bg_f", "gla_bg_b")):
                v256[pl.ds(2 * p_ + l, 1), :] = g[l][nm][...]
            for d, name in enumerate(DIRS):
                for hd in range(LRU_HEADS):
                    rs = slice(hd * LRU_HD, (hd + 1) * LRU_HD)
                    dst_rows, dst_cols = _folded_block(hd)
                    mats[2 * d, l, dst_rows, dst_cols] = g[l]["wcat_" + name][rs, hd * LRU_HD:(hd + 1) * LRU_HD].astype(BF16)
                    mats[2 * d + 1, l, dst_rows, dst_cols] = (
                        g[l]["wcat_" + name][rs, LRU_W + hd * LRU_HD:LRU_W + (hd + 1) * LRU_HD].astype(BF16))
            for j in range(N_DEV):
                convw[j, l] = g[l]["conv_w"][0:4, j * 64:(j + 1) * 64]
                wgf[j, l] = g[l]["wg_f"][0:GLA_RANK, j * 32:(j + 1) * 32]
                wgb[j, l] = g[l]["wg_b"][GLA_RANK:2 * GLA_RANK, j * 32:(j + 1) * 32]
        for j in range(N_DEV):
            meta[j] = dh0_ref[:, j * LANES:(j + 1) * LANES]

    out_shape = [_sds((8, D_MODEL), F32), _sds((16, LRU_W), F32), _sds((8, GLA_QK), F32),
                 _sds((4, depth, LRU_W // 2, 2 * LRU_HD), BF16),
                 _sds((N_DEV, depth, 4, 64), F32), _sds((N_DEV, depth, GLA_RANK, 32), F32), _sds((N_DEV, depth, GLA_RANK, 32), F32),
                 _sds((N_DEV, N_META, LANES), F32)]
    return pl.pallas_call(
        body, name="pack_small_grads", grid=(1,),
        in_specs=[VMEM_SPEC] * (depth * k) + [pl.BlockSpec((N_META, D_MODEL), lambda i: (meta_rows, 0))],
        out_specs=[VMEM_SPEC] * len(out_shape), out_shape=out_shape, compiler_params=_params(("arbitrary",), 32),
    )(*ins, dh0)


def _my_index():
    return 4 * lax.axis_index("x") + 2 * lax.axis_index("y") + lax.axis_index("c")


def _peer(k):
    x, y, c = lax.axis_index("x"), lax.axis_index("y"), lax.axis_index("c")
    px = x ^ ((k >> 2) & 1)
    py = y ^ ((k >> 1) & 1)
    pc = c ^ (k & 1)
    return (px, py, pc), 4 * px + 2 * py + pc


ALL_PEERS = tuple(range(1, N_DEV))
PEER_GROUPS = ((1, 2, 3), (4, 5), (6, 7))


class _Exchange:
    def __init__(self):
        self.inputs, self.landings, self.transfers = [], [], []

    def add(self, array, src, landing, dst, peers=ALL_PEERS, local=True):
        if isinstance(landing, int):
            out = landing
        else:
            out = len(self.landings)
            self.landings.append(landing)
        self.transfers.append((len(self.inputs), src, out, dst, tuple(peers), local))
        self.inputs.append(array)
        return out

    def _pairs(self):
        return [(t, k) for t, tr in enumerate(self.transfers) for k in tr[4]]

    def _locals(self):
        return [t for t, tr in enumerate(self.transfers) if tr[5]]

    def out_shapes(self):
        return [g if isinstance(g, jax.ShapeDtypeStruct) else _sds(g.shape, g.dtype) for g in self.landings]

    def continued(self):
        return [(b, g) for b, g in enumerate(self.landings) if not isinstance(g, jax.ShapeDtypeStruct)]

    def sem_shapes(self):
        return [pltpu.SemaphoreType.DMA((max(len(self._pairs()), 1),)), pltpu.SemaphoreType.DMA((max(len(self._pairs()), 1),)),
                pltpu.SemaphoreType.DMA((max(len(self._locals()), 1),))]

    def _local(self, ins, outs, sems):
        me = _my_index()
        copies = []
        for s, t in enumerate(self._locals()):
            a, src, b, dst, _, _ = self.transfers[t]
            copies.append(pltpu.make_async_copy(src(ins[a], me), dst(outs[b], me), sems[2].at[s]))
        return copies

    def _remote(self, ins, outs, sems, sending):
        copies = []
        for s, (t, k) in enumerate(self._pairs()):
            a, src, b, dst, _, _ = self.transfers[t]
            peer, peer_index = _peer(k)
            copies.append(pltpu.make_async_remote_copy(
                src_ref=src(ins[a], peer_index), dst_ref=dst(outs[b], _my_index() if sending else peer_index),
                send_sem=sems[0].at[s], recv_sem=sems[1].at[s], device_id=peer, device_id_type=MESH_ID))
        return copies

    def start(self, ins, outs, sems):
        for cp in self._local(ins, outs, sems) + self._remote(ins, outs, sems, True):
            cp.start()

    def wait(self, ins, outs, sems):
        for cp in self._remote(ins, outs, sems, False):
            cp.wait_recv()
        for cp in self._remote(ins, outs, sems, True):
            cp.wait_send()
        for cp in self._local(ins, outs, sems):
            cp.wait()

    def run(self, name):
        return _hosting_call(self, None, name=name, grid=(), in_specs=[], out_specs=[], out_shape=[], scratch_shapes=[],
                             compiler_params=pltpu.CompilerParams(has_side_effects=True))()


def _hosting_call(exchange, body, *, name, grid, in_specs, out_specs, out_shape, scratch_shapes, compiler_params):
    if exchange is None or not exchange.transfers:
        return pl.pallas_call(body, name=name, grid=grid, in_specs=in_specs, out_specs=out_specs, out_shape=out_shape,
                              scratch_shapes=scratch_shapes, compiler_params=compiler_params)
    n_in, n_out, n_scr = len(in_specs), len(out_specs), len(scratch_shapes)
    x_in, x_out = len(exchange.inputs), len(exchange.landings)
    continued = exchange.continued()

    def hosted(*refs):
        ins, x_ins = refs[:n_in], refs[n_in:n_in + x_in]
        o0 = n_in + x_in + len(continued)
        outs, x_outs = refs[o0:o0 + n_out], refs[o0 + n_out:o0 + n_out + x_out]
        s0 = o0 + n_out + x_out
        scratch, sems = refs[s0:s0 + n_scr], refs[s0 + n_scr:]
        if body is None:
            exchange.start(x_ins, x_outs, sems)
            exchange.wait(x_ins, x_outs, sems)
            return
        ids = [pl.program_id(a) for a in range(len(grid))]
        first = functools.reduce(jnp.logical_and, [i == 0 for i in ids])
        last = functools.reduce(jnp.logical_and, [i == g - 1 for i, g in zip(ids, grid)])

        @pl.when(first)
        def _():
            exchange.start(x_ins, x_outs, sems)

        body(*ins, *outs, *scratch)

        @pl.when(last)
        def _():
            exchange.wait(x_ins, x_outs, sems)

    aliases = {n_in + x_in + i: n_out + b for i, (b, _) in enumerate(continued)}
    kwargs = dict(grid=grid) if grid else {}
    call = pl.pallas_call(
        hosted, name=name, in_specs=list(in_specs) + [ANY_SPEC] * (x_in + len(continued)),
        out_specs=list(out_specs) + [ANY_SPEC] * x_out, out_shape=list(out_shape) + exchange.out_shapes(),
        scratch_shapes=list(scratch_shapes) + exchange.sem_shapes(), compiler_params=compiler_params,
        input_output_aliases=aliases, **kwargs)
    return lambda *operands: call(*operands, *exchange.inputs, *[g for _, g in continued])


def _whole(ref, j):
    return ref


def _slab(ref, j):
    return ref.at[j]


def _layer_of(l):
    return lambda ref, j: ref.at[l]


def _adamw(g, w, m, v):
    nm = ADAM_B1 * m + (1.0 - ADAM_B1) * g
    nv = ADAM_B2 * v + (1.0 - ADAM_B2) * jnp.square(g)
    m_hat = nm / (1.0 - ADAM_B1 ** ADAM_STEP)
    v_hat = nv / (1.0 - ADAM_B2 ** ADAM_STEP)
    return -ADAM_LR * (m_hat / (jnp.sqrt(v_hat) + ADAM_EPS) + ADAM_WD * w), nm, nv


def _sum_parts(p_ref):
    g = p_ref[0].astype(F32)
    for j in range(1, N_DEV):
        g = g + p_ref[j].astype(F32)
    return g


def _adamw_sharded(parts, w, m, v, name, exchange=None):
    shape = w.shape
    lead, (rows, cols) = shape[:-2], shape[-2:]
    tr = min(rows, ROW_BLOCK)
    assert rows % tr == 0
    steps = rows // tr
    nl = len(lead)
    spec = pl.BlockSpec((None,) * nl + (tr, cols), lambda *idx: idx + (0,))
    per_layer = isinstance(parts, (list, tuple))
    if per_layer:
        def part_spec(l):
            return pl.BlockSpec((N_DEV, tr, cols), lambda li, r: (0, jnp.where(li == l, r, jnp.where(li < l, 0, steps - 1)), 0))
        part_specs = [part_spec(l) for l in range(len(parts))]
    else:
        parts = [parts]
        part_specs = [pl.BlockSpec((N_DEV,) + (None,) * nl + (tr, cols), lambda *idx: (0,) + idx + (0,))]
    count = len(parts)

    def body(*refs):
        p_refs = refs[:count]
        w_ref, m_ref, v_ref, g_ref, d_ref, nm_ref, nv_ref = refs[count:]

        def update(p_ref):
            g = _sum_parts(p_ref)
            g_ref[...] = g
            d_ref[...], nm_ref[...], nv_ref[...] = _adamw(g, w_ref[...], m_ref[...], v_ref[...])

        if per_layer:
            for l in range(count):
                pl.when(pl.program_id(0) == l)(functools.partial(update, p_refs[l]))
        else:
            update(p_refs[0])

    return _hosting_call(
        exchange, body, name=name, grid=lead + (steps,),
        in_specs=part_specs + [spec, spec, spec], out_specs=[spec] * 4, out_shape=[_sds(shape, F32)] * 4,
        scratch_shapes=[], compiler_params=_params(("arbitrary",) * (nl + 1)),
    )(*parts, w, m, v)


def _adamw_replicated(gathered, w, m, v, depth):
    names = NORM_NAMES + VEC512_NAMES + VEC256_NAMES + LRU_MAT_NAMES
    count = len(names)

    def body(*refs):
        norms, v512, v256, mats = refs[:4]
        w_refs, m_refs, v_refs = (refs[4 + t * count:4 + (t + 1) * count] for t in range(3))
        outs = refs[4 + 3 * count:4 + 7 * count]
        sum_norms, sum_512, sum_256, unfolded = refs[4 + 7 * count:]
        sum_norms[...] = _sum_parts(norms)
        sum_512[...] = _sum_parts(v512)
        sum_256[...] = _sum_parts(v256)
        for n_, nm in enumerate(names):
            if nm in NORM_NAMES:
                g = sum_norms[pl.ds(depth * NORM_NAMES.index(nm), depth), :]
            elif nm in VEC512_NAMES:
                g = sum_512[pl.ds(depth * VEC512_NAMES.index(nm), depth), :]
            elif nm in VEC256_NAMES:
                g = sum_256[pl.ds(depth * VEC256_NAMES.index(nm), depth), :]
            else:
                p_ = LRU_MAT_NAMES.index(nm)
                folded = mats[0, p_].astype(F32)
                for j in range(1, N_DEV):
                    folded = folded + mats[j, p_].astype(F32)
                for hd in range(LRU_HEADS):
                    src_rows, src_cols = _folded_block(hd)
                    unfolded[:, hd * LRU_HD:(hd + 1) * LRU_HD, :] = folded[:, src_rows, src_cols]
                g = unfolded[...]
            delta, nm_, nv_ = _adamw(g, w_refs[n_][...], m_refs[n_][...], v_refs[n_][...])
            outs[n_][...] = g
            outs[count + n_][...] = delta
            outs[2 * count + n_][...] = nm_
            outs[3 * count + n_][...] = nv_

    shapes = [_sds(w[nm].shape, F32) for nm in names]
    ins = list(gathered) + [t[nm] for t in (w, m, v) for nm in names]
    outs = pl.pallas_call(
        body, name="adamw_replicated", in_specs=[VMEM_SPEC] * len(ins), out_specs=[VMEM_SPEC] * (4 * count),
        out_shape=shapes * 4,
        scratch_shapes=[pltpu.VMEM(gathered[0].shape[1:], F32), pltpu.VMEM(gathered[1].shape[1:], F32),
                        pltpu.VMEM(gathered[2].shape[1:], F32), pltpu.VMEM((depth, LRU_W, LRU_HD), F32)],
        compiler_params=_params(None, 48),
    )(*ins)
    return [dict(zip(names, outs[t * count:(t + 1) * count])) for t in range(4)]


WEIGHT_NAMES = ("meta_tokens", "norm_mix_pre", "norm_mix_post", "norm_mlp_pre", "norm_mlp_post", "w_in", "conv_w", "conv_b",
                "lru_wa_f", "lru_ba_f", "lru_wx_f", "lru_bx_f", "lru_lambda_f", "lru_wa_b", "lru_ba_b", "lru_wx_b",
                "lru_bx_b", "lru_lambda_b", "gla_wg_f", "gla_bg_f", "gla_wg_b", "gla_bg_b", "gla_head_norm", "w_out",
                "w_mlp_up", "w_mlp_down")
MATMUL_WEIGHTS = ("w_in", "w_out", "w_mlp_up", "w_mlp_down")
SMALL_SHARDED = ("conv_w", "gla_wg_f", "gla_wg_b", "meta_tokens")


def kernel(x, meta_tokens, norm_mix_pre, norm_mix_post, norm_mlp_pre, norm_mlp_post, w_in, conv_w, conv_b, lru_wa_f, lru_ba_f, lru_wx_f, lru_bx_f, lru_lambda_f, lru_wa_b, lru_ba_b, lru_wx_b, lru_bx_b, lru_lambda_b, gla_wg_f, gla_bg_f, gla_wg_b, gla_bg_b, gla_head_norm, w_out, w_mlp_up, w_mlp_down, loss_target, m_meta_tokens, m_norm_mix_pre, m_norm_mix_post, m_norm_mlp_pre, m_norm_mlp_post, m_w_in, m_conv_w, m_conv_b, m_lru_wa_f, m_lru_ba_f, m_lru_wx_f, m_lru_bx_f, m_lru_lambda_f, m_lru_wa_b, m_lru_ba_b, m_lru_wx_b, m_lru_bx_b, m_lru_lambda_b, m_gla_wg_f, m_gla_bg_f, m_gla_wg_b, m_gla_bg_b, m_gla_head_norm, m_w_out, m_w_mlp_up, m_w_mlp_down, v_meta_tokens, v_norm_mix_pre, v_norm_mix_post, v_norm_mlp_pre, v_norm_mlp_post, v_w_in, v_conv_w, v_conv_b, v_lru_wa_f, v_lru_ba_f, v_lru_wx_f, v_lru_bx_f, v_lru_lambda_f, v_lru_wa_b, v_lru_ba_b, v_lru_wx_b, v_lru_bx_b, v_lru_lambda_b, v_gla_wg_f, v_gla_bg_f, v_gla_wg_b, v_gla_bg_b, v_gla_head_norm, v_w_out, v_w_mlp_up, v_w_mlp_down):
    args = locals()
    w = {nm: args[nm] for nm in WEIGHT_NAMES}
    m = {nm: args["m_" + nm] for nm in WEIGHT_NAMES}
    v = {nm: args["v_" + nm] for nm in WEIGHT_NAMES}
    depth = w_in.shape[0]

    shards = {nm: w[nm].astype(BF16) for nm in MATMUL_WEIGHTS}
    p = {}
    gather = _Outbox(_install_weight(p))
    _request_weight(gather, shards, "w_in", 0)
    _request_weight(gather, shards, "w_out", 0)
    ex, tickets = gather.exchange([("w_in", group) for group in range(len(PEER_GROUPS))])
    first_small = len(ex.landings)
    for nm in SMALL_SHARDED:
        ex.add(w[nm], _whole, _sds((N_DEV,) + w[nm].shape, F32), _slab)
    landed = ex.run("all_gather")
    gather.store(tickets, landed)
    p.update(_prepare_params(w, dict(zip(SMALL_SHARDED, landed[first_small:])), depth))

    h = jnp.concatenate([jnp.zeros((PAD_ROWS, D_MODEL), F32), p["meta_tokens"], x[0]], axis=0)
    saved = []
    for l in range(depth):
        h, s = _layer_fwd(h, l, p, gather, shards, depth)
        saved.append(s)
    dh, loss_part = _loss_and_grad(h, loss_target[0])
    loss = lax.psum(loss_part[0, 0], ("x", "y", "c"))

    received = {}
    outbox = _Outbox(received.__setitem__)
    grads = [None] * depth
    for l in reversed(range(depth)):
        dh, grads[l], dz = _layer_bwd(dh, l, p, saved[l], outbox)
        if l > 0:
            _send_grad(outbox, "w_in", l, _w_in_slabs(_matmul_tn(saved[l]["hn"], dz, "grad_w_in")))
    grad_x = dh[PAD_ROWS + N_META:][None]

    small = _pack_small_grads(grads, dh, depth)
    rep_bufs, small_slabs = small[:4], small[4:]
    ex, tickets = outbox.exchange()
    first_small = len(ex.landings)
    for g in small_slabs:
        ex.add(g, _slab, _sds(g.shape, F32), _slab)
    for g in rep_bufs:
        ex.add(g, _whole, _sds((N_DEV,) + g.shape, g.dtype), _slab)
    grad_w_in, *landed = _matmul_tn(saved[0]["hn"], dz, "grad_w_in", exchange=ex)
    outbox.store(tickets, landed)
    small_received = landed[first_small:first_small + len(small_slabs)]
    rep_received = landed[first_small + len(small_slabs):]
    _send_grad(outbox, "w_in", 0, _w_in_slabs(grad_w_in))

    results = [{}, {}, {}, {}]
    for group, nm in enumerate(("w_mlp_up", "w_mlp_down", "w_out", "w_in")):
        ex, tickets = outbox.exchange([("w_in", group)] if group < len(PEER_GROUPS) else None)
        parts = [received[(nm, l)] for l in range(depth)]
        outs = _adamw_sharded(parts, w[nm], m[nm], v[nm], "adamw_" + nm, ex)
        outbox.store(tickets, outs[4:])
        for t, out in enumerate(outs[:4]):
            results[t][nm] = out
    for nm, parts in zip(SMALL_SHARDED, small_received):
        for t, out in enumerate(_adamw_sharded(parts, w[nm], m[nm], v[nm], "adamw_" + nm)):
            results[t][nm] = out

    def kernel_side(tree):
        return {nm: tree[nm].reshape(depth, LRU_W, LRU_HD) if nm in LRU_MAT_NAMES else tree[nm]
                for nm in NORM_NAMES + VEC512_NAMES + VEC256_NAMES + LRU_MAT_NAMES}

    for t, tree in enumerate(_adamw_replicated(rep_received, kernel_side(w), kernel_side(m), kernel_side(v), depth)):
        for nm, out in tree.items():
            results[t][nm] = out.reshape(w[nm].shape)
    return (loss, grad_x, *[results[t][nm] for t in range(4) for nm in WEIGHT_NAMES])
```

```python
import functools

import jax
import jax.numpy as jnp
from jax import lax
from jax.experimental import pallas as pl
from jax.experimental.pallas import tpu as pltpu

F32 = jnp.float32
BF16 = jnp.bfloat16

N_DEV = 8
D_MODEL = 1024
N_META = 16
ROW_BLOCK = 256
PAD_ROWS = ROW_BLOCK - N_META
CHUNK = 128
LRU_W = 512
LRU_HEADS = 8
LRU_HD = 64
LRU_C = 8.0
GLA_HEADS = 4
GLA_DK = 64
GLA_DV = 128
GLA_QK = GLA_HEADS * GLA_DK
GLA_W = GLA_HEADS * GLA_DV
GLA_RANK = 16
GATE_NORM = 16.0
D_FF = 4096
D_IN = 2592
Z_W = 2688
ZG_COL_BLOCK = 2560 // 128
EPS = 1e-6
LANES = 128

ADAM_LR = 0.001
ADAM_B1 = 0.9
ADAM_B2 = 0.999
ADAM_EPS = 1e-08
ADAM_WD = 0.01
ADAM_STEP = 10

VMEM_SPEC = pl.BlockSpec(memory_space=pltpu.VMEM)
ANY_SPEC = pl.BlockSpec(memory_space=pl.ANY)
MESH_ID = pl.DeviceIdType.MESH


def _sds(shape, dtype):
    return jax.ShapeDtypeStruct(shape, dtype)


def _params(sem=None, vmem_mb=None):
    kw = {}
    if sem is not None:
        kw["dimension_semantics"] = sem
    if vmem_mb is not None:
        kw["vmem_limit_bytes"] = vmem_mb * 2 ** 20
    return pltpu.CompilerParams(**kw)


def _row_tile(n, cap=768):
    for t in (768, 512, 384, 256):
        if t <= cap and n % t == 0:
            return t
    raise ValueError(n)


def _col_tile(k):
    for t in (1024, 896, 768, 640, 512, 384, 256, 128):
        if k % t == 0:
            return t
    raise ValueError(k)


def _sigmoid(x):
    return 0.5 * jnp.tanh(0.5 * x) + 0.5


def _gelu_and_grad(x):
    c = 0.7978845608028654
    inner = c * (x + 0.044715 * x * x * x)
    t = jnp.tanh(inner)
    gelu = 0.5 * x * (1.0 + t)
    dgelu = 0.5 * (1.0 + t) + 0.5 * x * (1.0 - t * t) * c * (1.0 + 3.0 * 0.044715 * x * x)
    return gelu, dgelu


def _one_minus_square(a, log_a):
    return jnp.tanh(-log_a) * (1.0 + a * a)


def _rms_fwd(x, g):
    rs = lax.rsqrt(jnp.mean(x * x, axis=-1, keepdims=True) + EPS)
    return x * rs * g


def _rms_bwd(x, g, dy):
    rs = lax.rsqrt(jnp.mean(x * x, axis=-1, keepdims=True) + EPS)
    xh = x * rs
    dyg = dy * g
    dx = rs * (dyg - xh * jnp.mean(dyg * xh, axis=-1, keepdims=True))
    return dx, jnp.sum(dy * xh, axis=0, keepdims=True)


def _dot(a, b):
    return jnp.dot(a.astype(BF16), b.astype(BF16), preferred_element_type=F32)


def _dot_nt(a, b):
    return lax.dot_general(a.astype(BF16), b.astype(BF16), (((1,), (1,)), ((), ())), preferred_element_type=F32)


def _dot_tn(a, b):
    return lax.dot_general(a.astype(BF16), b.astype(BF16), (((0,), (0,)), ((), ())), preferred_element_type=F32)


class _LayerParam:
    def __init__(self, array, *index):
        self.array = array
        self.index = index

    @property
    def spec(self):
        lead = len(self.index)
        tail = self.array.shape[lead:]
        index = self.index
        return pl.BlockSpec((None,) * lead + tail, lambda *_: index + (0,) * len(tail))


def _row_ids(rows, block_index):
    return block_index * rows + lax.broadcasted_iota(jnp.int32, (rows, 1), 0)


def _accumulate(ref, value, first):
    @pl.when(first)
    def _():
        ref[...] = value

    @pl.when(jnp.logical_not(first))
    def _():
        ref[...] += value


def _norm_in_proj(h, g, w, exchange=None):
    n, d = h.shape
    zw = w.shape[1]
    tr = _row_tile(n)

    def body(h_ref, g_ref, w_ref, hn_ref, z_ref):
        hn = _rms_fwd(h_ref[...], g_ref[...]).astype(BF16)
        hn_ref[...] = hn
        z_ref[...] = jnp.dot(hn, w_ref[...], preferred_element_type=F32)

    return _hosting_call(
        exchange, body, name="norm_in_proj", grid=(n // tr,),
        in_specs=[pl.BlockSpec((tr, d), lambda i: (i, 0)), g.spec, VMEM_SPEC],
        out_specs=[pl.BlockSpec((tr, d), lambda i: (i, 0)), pl.BlockSpec((tr, zw), lambda i: (i, 0))],
        out_shape=[_sds((n, d), BF16), _sds((n, zw), F32)],
        scratch_shapes=[], compiler_params=_params(("arbitrary",), 48),
    )(h, g.array, w)


def _halo_specs(width, nb, col=0):
    per = ROW_BLOCK // 8
    prev = pl.BlockSpec((8, width), lambda i: (jnp.maximum(i * per - 1, 0), col))
    nxt = pl.BlockSpec((8, width), lambda i: (jnp.minimum((i + 1) * per, nb * per - 1), col))
    return prev, nxt


def _shift_down(x, prev8, d):
    n = x.shape[0]
    r = pltpu.roll(x, d, 0)
    p = pltpu.roll(prev8, d, 0)
    row8 = lax.broadcasted_iota(jnp.int32, (8, 1), 0)
    head = jnp.where(row8 < d, p, r[0:8])
    return jnp.concatenate([head, r[8:]], axis=0)


def _shift_up(x, next8, d):
    n = x.shape[0]
    r = pltpu.roll(x, n - d, 0)
    q = pltpu.roll(next8, 8 - d, 0)
    row8 = lax.broadcasted_iota(jnp.int32, (8, 1), 0)
    tail = jnp.where(row8 >= 8 - d, q, r[n - 8:])
    return jnp.concatenate([r[:n - 8], tail], axis=0)


def _conv_fwd(z, conv_w, conv_b):
    n = z.shape[0]
    nb = n // ROW_BLOCK
    prev_spec, next_spec = _halo_specs(LRU_W, nb)

    def body(cur_ref, prev_ref, next_ref, w_ref, b_ref, xc_ref):
        i = pl.program_id(0)
        cur = cur_ref[...]
        prev8 = prev_ref[...] * jnp.where(i > 0, 1.0, 0.0)
        next8 = next_ref[...] * jnp.where(i < nb - 1, 1.0, 0.0)
        w = [w_ref[pl.ds(k, 1), :] for k in range(4)]
        xc = (w[0] * _shift_down(cur, prev8, 2) + w[1] * _shift_down(cur, prev8, 1)
              + w[2] * cur + w[3] * _shift_up(cur, next8, 1) + b_ref[...])
        xc_ref[...] = xc

    return pl.pallas_call(
        body, name="conv_fwd", grid=(nb,),
        in_specs=[pl.BlockSpec((ROW_BLOCK, LRU_W), lambda i: (i, 0)), prev_spec, next_spec, conv_w.spec, conv_b.spec],
        out_specs=pl.BlockSpec((ROW_BLOCK, LRU_W), lambda i: (i, 0)),
        out_shape=_sds((n, LRU_W), F32),
        compiler_params=_params(("parallel",)),
    )(z, z, z, conv_w.array, conv_b.array)


def _conv_bwd(dxc_f, dxc_b, z, conv_w):
    n = z.shape[0]
    nb = n // ROW_BLOCK
    prev_spec, next_spec = _halo_specs(LRU_W, nb)
    row_spec = pl.BlockSpec((ROW_BLOCK, LRU_W), lambda i: (i, 0))

    def body(df_ref, dfp_ref, dfn_ref, db_ref, dbp_ref, dbn_ref, x_ref, xp_ref, xn_ref, w_ref,
             dx_ref, dw_ref, dbias_ref):
        i = pl.program_id(0)
        has_prev = jnp.where(i > 0, 1.0, 0.0)
        has_next = jnp.where(i < nb - 1, 1.0, 0.0)
        dxc = df_ref[...] + db_ref[...]
        dprev = (dfp_ref[...] + dbp_ref[...]) * has_prev
        dnext = (dfn_ref[...] + dbn_ref[...]) * has_next
        x = x_ref[...]
        xprev = xp_ref[...] * has_prev
        xnext = xn_ref[...] * has_next
        w = [w_ref[pl.ds(k, 1), :] for k in range(4)]
        dx_ref[...] = (w[0] * _shift_up(dxc, dnext, 2) + w[1] * _shift_up(dxc, dnext, 1)
                       + w[2] * dxc + w[3] * _shift_down(dxc, dprev, 1)).astype(BF16)
        dw = jnp.concatenate([
            jnp.sum(dxc * _shift_down(x, xprev, 2), axis=0, keepdims=True),
            jnp.sum(dxc * _shift_down(x, xprev, 1), axis=0, keepdims=True),
            jnp.sum(dxc * x, axis=0, keepdims=True),
            jnp.sum(dxc * _shift_up(x, xnext, 1), axis=0, keepdims=True),
            jnp.zeros((4, LRU_W), F32)], axis=0)
        _accumulate(dw_ref, dw, i == 0)
        _accumulate(dbias_ref, jnp.sum(dxc, axis=0, keepdims=True), i == 0)

    dx, dw, dbias = pl.pallas_call(
        body, name="conv_bwd", grid=(nb,),
        in_specs=[row_spec, prev_spec, next_spec, row_spec, prev_spec, next_spec, row_spec, prev_spec, next_spec,
                  conv_w.spec],
        out_specs=[row_spec, pl.BlockSpec((8, LRU_W), lambda i: (0, 0)), pl.BlockSpec((1, LRU_W), lambda i: (0, 0))],
        out_shape=[_sds((n, LRU_W), BF16), _sds((8, LRU_W), F32), _sds((1, LRU_W), F32)],
        compiler_params=_params(("arbitrary",)),
    )(dxc_f, dxc_f, dxc_f, dxc_b, dxc_b, dxc_b, z, z, z, conv_w.array)
    return dx, dw, dbias


def _mix_epilogue(h_f, h_b, o_f, o_b, z, head_norm):
    n = z.shape[0]
    tr = ROW_BLOCK
    spec = pl.BlockSpec((tr, 512), lambda i: (i, 0))

    def body(hf_ref, hb_ref, of_ref, ob_ref, gate_ref, gout_ref, w_ref, y_ref):
        gelu, _ = _gelu_and_grad(gate_ref[...])
        y_ref[:, 0:LRU_W] = ((hf_ref[...] + hb_ref[...]) * gelu).astype(BF16)
        o = of_ref[...] + ob_ref[...]
        gout = gout_ref[...]
        silu = gout * _sigmoid(gout)
        w = w_ref[...]
        for hd in range(GLA_HEADS):
            cs = slice(hd * GLA_DV, (hd + 1) * GLA_DV)
            oh = o[:, cs]
            on = oh * lax.rsqrt(jnp.mean(oh * oh, axis=-1, keepdims=True) + EPS)
            y_ref[:, LRU_W + hd * GLA_DV:LRU_W + (hd + 1) * GLA_DV] = (on * w[:, cs] * silu[:, cs]).astype(BF16)

    return pl.pallas_call(
        body, name="mix_epilogue", grid=(n // tr,),
        in_specs=[spec, spec, spec, spec, pl.BlockSpec((tr, 512), lambda i: (i, 1)),
                  pl.BlockSpec((tr, 512), lambda i: (i, 4)), head_norm.spec],
        out_specs=pl.BlockSpec((tr, D_MODEL), lambda i: (i, 0)),
        out_shape=_sds((n, D_MODEL), BF16),
        compiler_params=_params(("parallel",)),
    )(h_f, h_b, o_f, o_b, z, z, head_norm.array)


def _mix_epilogue_bwd(dymix, h_f, h_b, o_f, o_b, z, head_norm):
    n = z.shape[0]
    tr = ROW_BLOCK
    spec = pl.BlockSpec((tr, 512), lambda i: (i, 0))

    def body(dyl_ref, dyg_ref, hf_ref, hb_ref, of_ref, ob_ref, gate_ref, gout_ref, w_ref,
             dhs_ref, dgate_ref, do_ref, dgout_ref, dw_ref):
        i = pl.program_id(0)
        dyl = dyl_ref[...]
        gelu, dgelu = _gelu_and_grad(gate_ref[...])
        dhs_ref[...] = dyl * gelu
        dgate_ref[...] = (dyl * (hf_ref[...] + hb_ref[...]) * dgelu).astype(BF16)
        dyg = dyg_ref[...]
        o = of_ref[...] + ob_ref[...]
        gout = gout_ref[...]
        sg = _sigmoid(gout)
        silu = gout * sg
        dsilu = sg * (1.0 + gout * (1.0 - sg))
        w = w_ref[...]
        dws = []
        for hd in range(GLA_HEADS):
            cs = slice(hd * GLA_DV, (hd + 1) * GLA_DV)
            oh = o[:, cs]
            rs = lax.rsqrt(jnp.mean(oh * oh, axis=-1, keepdims=True) + EPS)
            on = oh * rs
            dy = dyg[:, cs]
            dgout_ref[:, cs] = (dy * on * w[:, cs] * dsilu[:, cs]).astype(BF16)
            dys = dy * silu[:, cs]
            dws.append(jnp.sum(dys * on, axis=0, keepdims=True))
            don = dys * w[:, cs]
            do_ref[:, cs] = (rs * (don - on * jnp.mean(don * on, axis=-1, keepdims=True))).astype(BF16)
        _accumulate(dw_ref, jnp.concatenate(dws, axis=1), i == 0)

    return pl.pallas_call(
        body, name="mix_epilogue_bwd", grid=(n // tr,),
        in_specs=[pl.BlockSpec((tr, 512), lambda i: (i, 0)), pl.BlockSpec((tr, 512), lambda i: (i, 1)),
                  spec, spec, spec, spec, pl.BlockSpec((tr, 512), lambda i: (i, 1)),
                  pl.BlockSpec((tr, 512), lambda i: (i, 4)), head_norm.spec],
        out_specs=[spec, spec, spec, spec, pl.BlockSpec((1, GLA_W), lambda i: (0, 0))],
        out_shape=[_sds((n, 512), F32)] + [_sds((n, 512), BF16)] * 3 + [_sds((1, GLA_W), F32)],
        compiler_params=_params(("arbitrary",)),
    )(dymix, dymix, h_f, h_b, o_f, o_b, z, z, head_norm.array)


def _out_proj(ymix, w_out, h, g, exchange=None):
    n, d = h.shape
    tr = _row_tile(n)
    spec = pl.BlockSpec((tr, d), lambda i: (i, 0))

    def body(y_ref, w_ref, h_ref, g_ref, mix_ref, hmid_ref):
        mix = jnp.dot(y_ref[...], w_ref[...], preferred_element_type=F32)
        mix_ref[...] = mix
        hmid_ref[...] = h_ref[...] + _rms_fwd(mix, g_ref[...])

    return _hosting_call(
        exchange, body, name="out_proj", grid=(n // tr,),
        in_specs=[spec, VMEM_SPEC, spec, g.spec],
        out_specs=[spec, spec],
        out_shape=[_sds((n, d), F32), _sds((n, d), F32)],
        scratch_shapes=[], compiler_params=_params(("arbitrary",), 44),
    )(ymix, w_out, h, g.array)


def _out_proj_bwd(dh_mid, mix, g, w_out, exchange=None):
    n, d = mix.shape
    tr = _row_tile(n)
    spec = pl.BlockSpec((tr, d), lambda i: (i, 0))

    def body(dh_ref, mix_ref, g_ref, w_ref, dmix_ref, dy_ref, dg_ref):
        i = pl.program_id(0)
        dmix, dg = _rms_bwd(mix_ref[...], g_ref[...], dh_ref[...])
        dmix = dmix.astype(BF16)
        dmix_ref[...] = dmix
        dy_ref[...] = _dot_nt(dmix, w_ref[...])
        _accumulate(dg_ref, dg, i == 0)

    return _hosting_call(
        exchange, body, name="out_proj_bwd", grid=(n // tr,),
        in_specs=[spec, spec, g.spec, VMEM_SPEC],
        out_specs=[spec, spec, pl.BlockSpec((1, d), lambda i: (0, 0))],
        out_shape=[_sds((n, d), BF16), _sds((n, d), F32), _sds((1, d), F32)],
        scratch_shapes=[], compiler_params=_params(("arbitrary",), 44),
    )(dh_mid, mix, g.array, w_out)


FF_SLAB = D_FF // N_DEV


def _relu_squared(up):
    return jnp.square(jnp.maximum(up.astype(F32), 0.0)).astype(BF16)


def _mlp_fwd(h_mid, g_pre, w_up, w_down, g_post, exchange=None):
    n, d = h_mid.shape
    tr = _row_tile(n, 384)
    spec = pl.BlockSpec((tr, d), lambda i: (i, 0))

    def body(h_ref, gpre_ref, wup_ref, wdn_ref, gpost_ref, hn_ref, up_ref, ff_ref, hout_ref):
        h = h_ref[...]
        hn = _rms_fwd(h, gpre_ref[...]).astype(BF16)
        hn_ref[...] = hn
        ff = jnp.zeros((tr, d), F32)
        for j in range(N_DEV):
            cs = slice(j * FF_SLAB, (j + 1) * FF_SLAB)
            up = jnp.dot(hn, wup_ref[j], preferred_element_type=F32).astype(BF16)
            up_ref[:, cs] = up
            ff = ff + jnp.dot(_relu_squared(up), wdn_ref[cs, :], preferred_element_type=F32)
        ff_ref[...] = ff
        hout_ref[...] = h + _rms_fwd(ff, gpost_ref[...])

    return _hosting_call(
        exchange, body, name="mlp_fwd", grid=(n // tr,),
        in_specs=[spec, g_pre.spec, VMEM_SPEC, VMEM_SPEC, g_post.spec],
        out_specs=[spec, pl.BlockSpec((tr, D_FF), lambda i: (i, 0)), spec, spec],
        out_shape=[_sds((n, d), BF16), _sds((n, D_FF), BF16), _sds((n, d), F32), _sds((n, d), F32)],
        scratch_shapes=[], compiler_params=_params(("arbitrary",), 52),
    )(h_mid, g_pre.array, w_up, w_down, g_post.array)


def _mlp_bwd(dh, ff, up, h_mid, g_pre, w_up, w_down, g_post, exchange=None):
    n, d = h_mid.shape
    tr = _row_tile(n, 384)
    spec = pl.BlockSpec((tr, d), lambda i: (i, 0))
    wide = pl.BlockSpec((tr, D_FF), lambda i: (i, 0))
    gspec = pl.BlockSpec((1, d), lambda i: (0, 0))

    def body(dh_ref, ff_ref, up_ref, h_ref, gpre_ref, wup_ref, wdn_ref, gpost_ref,
             dff_ref, dup_ref, dhmid_ref, dgpost_ref, dgpre_ref):
        i = pl.program_id(0)
        dh = dh_ref[...]
        dff, dgpost = _rms_bwd(ff_ref[...], gpost_ref[...], dh)
        dff = dff.astype(BF16)
        dff_ref[...] = dff
        dhn = jnp.zeros((tr, d), F32)
        for j in range(N_DEV):
            cs = slice(j * FF_SLAB, (j + 1) * FF_SLAB)
            relu = jnp.maximum(up_ref[:, cs].astype(F32), 0.0)
            dact = _dot_nt(dff, wdn_ref[cs, :])
            dup = (dact * 2.0 * relu).astype(BF16)
            dup_ref[:, cs] = dup
            dhn = dhn + _dot_nt(dup, wup_ref[j])
        dx, dgpre = _rms_bwd(h_ref[...], gpre_ref[...], dhn)
        dhmid_ref[...] = dh + dx
        _accumulate(dgpost_ref, dgpost, i == 0)
        _accumulate(dgpre_ref, dgpre, i == 0)

    return _hosting_call(
        exchange, body, name="mlp_bwd", grid=(n // tr,),
        in_specs=[spec, spec, wide, spec, g_pre.spec, VMEM_SPEC, VMEM_SPEC, g_post.spec],
        out_specs=[spec, wide, spec, gspec, gspec],
        out_shape=[_sds((n, d), BF16), _sds((n, D_FF), BF16), _sds((n, d), F32), _sds((1, d), F32), _sds((1, d), F32)],
        scratch_shapes=[], compiler_params=_params(("arbitrary",), 56),
    )(dh, ff, up, h_mid, g_pre.array, w_up, w_down, g_post.array)


def _in_proj_bwd(pieces, w_in, h, g, dh_mid, exchange=None):
    dxbr, dgate, dqk_f, dqk_b, dv_f, dv_b, dgout, dzg_f, dzg_b = pieces
    n, d = h.shape
    tr = _row_tile(n, 384)
    spec = pl.BlockSpec((tr, d), lambda i: (i, 0))
    s512 = pl.BlockSpec((tr, 512), lambda i: (i, 0))
    s128 = pl.BlockSpec((tr, LANES), lambda i: (i, 0))

    def body(a_ref, b_ref, cf_ref, cb_ref, df_ref, db_ref, e_ref, ff_ref, fb_ref, w_ref, h_ref, g_ref, dhm_ref,
             dz_ref, dh_ref, dg_ref):
        i = pl.program_id(0)
        real = (_row_ids(tr, i) >= PAD_ROWS).astype(F32)
        f32 = lambda ref: ref[...].astype(F32)
        dz = jnp.concatenate([f32(a_ref), f32(b_ref), f32(cf_ref) + f32(cb_ref), f32(df_ref) + f32(db_ref),
                              f32(e_ref), f32(ff_ref) + f32(fb_ref)], axis=1) * real
        dz = dz.astype(BF16)
        dz_ref[...] = dz
        dhn = _dot_nt(dz, w_ref[...])
        dx, dg = _rms_bwd(h_ref[...], g_ref[...], dhn)
        dh_ref[...] = (dhm_ref[...] + dx) * real
        _accumulate(dg_ref, dg, i == 0)

    return _hosting_call(
        exchange, body, name="in_proj_bwd", grid=(n // tr,),
        in_specs=[s512, s512, s512, s512, s512, s512, s512, s128, s128, VMEM_SPEC, spec, g.spec, spec],
        out_specs=[pl.BlockSpec((tr, Z_W), lambda i: (i, 0)), spec, pl.BlockSpec((1, d), lambda i: (0, 0))],
        out_shape=[_sds((n, Z_W), BF16), _sds((n, d), F32), _sds((1, d), F32)],
        scratch_shapes=[], compiler_params=_params(("arbitrary",), 48),
    )(dxbr, dgate, dqk_f, dqk_b, dv_f, dv_b, dgout, dzg_f, dzg_b, w_in, h, g.array, dh_mid)


def _matmul_tn(a, b, name, column_slabs=False, exchange=None, a_map=None):
    n, m = a.shape
    k = b.shape[1]
    tr = next(t for t in (2816, 1408, 768, 512, 256) if n % t == 0)
    tm, tk = _col_tile(m), _col_tile(k)
    steps = n // tr
    slab = k // N_DEV
    per_step = tk // slab if column_slabs else 1
    sub = next(t for t in (704, 768, 512, 256) if tr % t == 0)

    def body(a_ref, b_ref, o_ref, acc_ref, *mapped_ref):
        r = pl.program_id(2)
        if a_map is None:
            a_blk = a_ref[...]
        else:
            for c in range(tr // sub):
                rows = pl.ds(c * sub, sub)
                mapped_ref[0][rows, :] = a_map(a_ref[rows, :])
            a_blk = mapped_ref[0][...]
        _accumulate(acc_ref, _dot_tn(a_blk, b_ref[...]), r == 0)

        @pl.when(r == steps - 1)
        def _():
            if column_slabs:
                for j in range(per_step):
                    o_ref[j] = acc_ref[:, j * slab:(j + 1) * slab].astype(BF16)
            else:
                o_ref[...] = acc_ref[...].astype(BF16)

    if column_slabs:
        out_spec = pl.BlockSpec((per_step, tm, slab), lambda mi, ki, r: (ki, mi, 0))
        out_shape = _sds((N_DEV, m, slab), BF16)
    else:
        out_spec = pl.BlockSpec((tm, tk), lambda mi, ki, r: (mi, ki))
        out_shape = _sds((m, k), BF16)
    outs = _hosting_call(
        exchange, body, name=name, grid=(m // tm, k // tk, steps),
        in_specs=[pl.BlockSpec((tr, tm), lambda mi, ki, r: (r, mi)), pl.BlockSpec((tr, tk), lambda mi, ki, r: (r, ki))],
        out_specs=[out_spec], out_shape=[out_shape],
        scratch_shapes=[pltpu.VMEM((tm, tk), F32)] + ([] if a_map is None else [pltpu.VMEM((tr, tm), BF16)]),
        compiler_params=_params(("arbitrary", "arbitrary", "arbitrary"), 52),
    )(a, b)
    return outs[0] if exchange is None else outs


def _loss_and_grad(h_out, target):
    n, d = h_out.shape
    tr = ROW_BLOCK
    first = (PAD_ROWS + N_META) // tr

    def body(h_ref, t_ref, dh_ref, loss_ref):
        i = pl.program_id(0)
        real = jnp.where(i >= first, 1.0, 0.0)
        diff = (h_ref[...] - t_ref[...]) * real
        dh_ref[...] = diff * (1.0 / d)
        part = 0.5 * jnp.sum(jnp.mean(diff * diff, axis=-1, keepdims=True), axis=0, keepdims=True)
        _accumulate(loss_ref, jnp.broadcast_to(part, (1, LANES)), i == 0)

    return pl.pallas_call(
        body, name="loss_and_grad", grid=(n // tr,),
        in_specs=[pl.BlockSpec((tr, d), lambda i: (i, 0)), pl.BlockSpec((tr, d), lambda i: (jnp.maximum(i - first, 0), 0))],
        out_specs=[pl.BlockSpec((tr, d), lambda i: (i, 0)), pl.BlockSpec((1, LANES), lambda i: (0, 0))],
        out_shape=[_sds((n, d), F32), _sds((1, LANES), F32)],
        compiler_params=_params(("arbitrary",)),
    )(h_out, target)


def _scan_block(a, u, h_in, reverse):
    n = a.shape[0]
    row = lax.broadcasted_iota(jnp.int32, (n, 1), 0)
    d = 1
    while d < n:
        shift = n - d if reverse else d
        keep = (row < n - d) if reverse else (row >= d)
        a_s = pltpu.roll(a, shift, 0)
        u_s = pltpu.roll(u, shift, 0)
        u = jnp.where(keep, a * u_s + u, u)
        a = jnp.where(keep, a * a_s, a)
        d *= 2
    return a * h_in + u


def _lru_gates(xc, wcat_ref, bias_ref, lam_ref):
    nl = -lam_ref[...]
    nsp = -LRU_C * (jnp.maximum(nl, 0.0) + jnp.log(1.0 + jnp.exp(-jnp.abs(nl))))
    pre = _dot(xc, wcat_ref[...]) + bias_ref[...]
    r = _sigmoid(pre[:, :LRU_W])
    ig = _sigmoid(pre[:, LRU_W:])
    log_a = r * nsp
    a = jnp.exp(log_a)
    m2 = _one_minus_square(a, log_a)
    inv_m = lax.rsqrt(jnp.maximum(m2, 1e-30))
    return r, ig, a, m2 * inv_m, inv_m, nsp


def _lru_scan(xc, wcat, bias, lam, reverse, exchange=None):
    n = xc.shape[0]
    nb = n // ROW_BLOCK
    order = (lambda i: nb - 1 - i) if reverse else (lambda i: i)
    spec = pl.BlockSpec((ROW_BLOCK, LRU_W), lambda i: (order(i), 0))
    edge = 0 if reverse else ROW_BLOCK - 1

    def body(xc_ref, wcat_ref, bias_ref, lam_ref, h_ref, carry_ref):
        i = pl.program_id(0)

        @pl.when(i == 0)
        def _():
            carry_ref[...] = jnp.zeros_like(carry_ref)

        xc = xc_ref[...]
        r, ig, a, m, _, _ = _lru_gates(xc, wcat_ref, bias_ref, lam_ref)
        u = jnp.where(_row_ids(ROW_BLOCK, order(i)) >= PAD_ROWS, m * (ig * xc), 0.0)
        h_ref[...] = _scan_block(a, u, carry_ref[0:1, :], reverse)
        carry_ref[0:1, :] = h_ref[pl.ds(edge, 1), :]

    return _hosting_call(
        exchange, body, name="lru_scan_b" if reverse else "lru_scan_f", grid=(nb,),
        in_specs=[spec, wcat.spec, bias.spec, lam.spec],
        out_specs=[spec],
        out_shape=[_sds((n, LRU_W), F32)],
        scratch_shapes=[pltpu.VMEM((8, LRU_W), F32)],
        compiler_params=_params(("arbitrary",)),
    )(xc, wcat.array, bias.array, lam.array)


def _lru_scan_bwd(dhs, xc, h, wcat, bias, lam, reverse, exchange=None):
    n = xc.shape[0]
    nb = n // ROW_BLOCK
    per = ROW_BLOCK // 8
    order = (lambda i: i) if reverse else (lambda i: nb - 1 - i)
    spec = pl.BlockSpec((ROW_BLOCK, LRU_W), lambda i: (order(i), 0))
    if reverse:
        halo = pl.BlockSpec((8, LRU_W), lambda i: (jnp.minimum((order(i) + 1) * per, nb * per - 1), 0))
    else:
        halo = pl.BlockSpec((8, LRU_W), lambda i: (jnp.maximum(order(i) * per - 1, 0), 0))
    edge = ROW_BLOCK - 1 if reverse else 0

    def body(dhs_ref, xc_ref, h_ref, halo_ref, wcat_ref, bias_ref, lam_ref,
             dxc_ref, dw_ref, db_ref, dlam_ref, cdh_ref, ca_ref, tmp_ref):
        i = pl.program_id(0)
        ib = order(i)

        @pl.when(i == 0)
        def _():
            cdh_ref[...] = jnp.zeros_like(cdh_ref)
            ca_ref[...] = jnp.zeros_like(ca_ref)

        xc = xc_ref[...]
        r, ig, a, m, inv_m, nsp = _lru_gates(xc, wcat_ref, bias_ref, lam_ref)
        row = lax.broadcasted_iota(jnp.int32, (ROW_BLOCK, 1), 0)
        if reverse:
            coef = jnp.where(row == 0, ca_ref[0:1, :], pltpu.roll(a, 1, 0))
            h_nb = jnp.where(row == ROW_BLOCK - 1, halo_ref[0:1, :] * jnp.where(ib < nb - 1, 1.0, 0.0),
                             pltpu.roll(h_ref[...], ROW_BLOCK - 1, 0))
        else:
            coef = jnp.where(row == ROW_BLOCK - 1, ca_ref[0:1, :], pltpu.roll(a, ROW_BLOCK - 1, 0))
            h_nb = jnp.where(row == 0, halo_ref[7:8, :] * jnp.where(ib > 0, 1.0, 0.0), pltpu.roll(h_ref[...], 1, 0))
        dh = _scan_block(coef, dhs_ref[...], cdh_ref[0:1, :], not reverse)
        tmp_ref[...] = dh
        cdh_ref[0:1, :] = tmp_ref[pl.ds(edge, 1), :]
        tmp_ref[...] = a
        ca_ref[0:1, :] = tmp_ref[pl.ds(edge, 1), :]

        du = jnp.where(_row_ids(ROW_BLOCK, ib) >= PAD_ROWS, dh, 0.0)
        da = dh * h_nb
        dm = du * (ig * xc)
        di = du * (m * xc)
        dlog_a = da * a - dm * (a * a) * inv_m
        dr = dlog_a * nsp
        dpre = jnp.concatenate([dr * r * (1.0 - r), di * ig * (1.0 - ig)], axis=1)
        dxc_ref[...] = du * (m * ig) + _dot_nt(dpre, wcat_ref[...])
        _accumulate(dw_ref, _dot_tn(xc, dpre), i == 0)
        _accumulate(db_ref, jnp.sum(dpre, axis=0, keepdims=True), i == 0)
        _accumulate(dlam_ref, jnp.sum(dlog_a * r, axis=0, keepdims=True), i == 0)

        @pl.when(i == nb - 1)
        def _():
            dlam_ref[...] = dlam_ref[...] * (LRU_C * _sigmoid(-lam_ref[...]))

    return _hosting_call(
        exchange, body, name="lru_scan_bwd_b" if reverse else "lru_scan_bwd_f", grid=(nb,),
        in_specs=[spec, spec, spec, halo, wcat.spec, bias.spec, lam.spec],
        out_specs=[spec, pl.BlockSpec((LRU_W, 2 * LRU_W), lambda i: (0, 0)),
                   pl.BlockSpec((1, 2 * LRU_W), lambda i: (0, 0)), pl.BlockSpec((1, LRU_W), lambda i: (0, 0))],
        out_shape=[_sds((n, LRU_W), F32), _sds((LRU_W, 2 * LRU_W), F32), _sds((1, 2 * LRU_W), F32), _sds((1, LRU_W), F32)],
        scratch_shapes=[pltpu.VMEM((8, LRU_W), F32), pltpu.VMEM((8, LRU_W), F32), pltpu.VMEM((ROW_BLOCK, LRU_W), F32)],
        compiler_params=_params(("arbitrary",)),
    )(dhs, xc, h, h, wcat.array, bias.array, lam.array)


def _gla_rows(n):
    return 768 if n % 768 == 0 else ROW_BLOCK


def _gla_masks(reverse):
    t = lax.broadcasted_iota(jnp.int32, (CHUNK, CHUNK), 0)
    s = lax.broadcasted_iota(jnp.int32, (CHUNK, CHUNK), 1)
    if reverse:
        return (s >= t).astype(F32), s > t
    return (s <= t).astype(F32), s <= t


def _gla_gate(zg, wg_ref, bg_ref):
    pre = _dot(zg, wg_ref[...]) + bg_ref[...]
    g = (jnp.minimum(pre, 0.0) - jnp.log(1.0 + jnp.exp(-jnp.abs(pre)))) * (1.0 / GATE_NORM)
    return pre, g


def _gla_decays(gc, tri):
    b = jnp.dot(tri, gc, precision=lax.Precision.HIGHEST, preferred_element_type=F32)
    b_last = jnp.sum(gc, axis=0, keepdims=True)
    return jnp.exp(b), jnp.exp(-b), jnp.exp(b_last - b), jnp.exp(b_last)


def _gla_scan(z, wg, bg, reverse, exchange=None):
    n = z.shape[0]
    rb = _gla_rows(n)
    nb = n // rb
    cpb = rb // CHUNK
    order = (lambda i: nb - 1 - i) if reverse else (lambda i: i)
    chunks = range(cpb - 1, -1, -1) if reverse else range(cpb)

    def body(qk_ref, v_ref, zg_ref, wg_ref, bg_ref, o_ref, sall_ref, s_ref):
        i = pl.program_id(0)

        @pl.when(i == 0)
        def _():
            s_ref[...] = jnp.zeros_like(s_ref)

        tri, mask = _gla_masks(reverse)
        _, g = _gla_gate(zg_ref[...], wg_ref, bg_ref)
        heads = range(GLA_HEADS)
        ks = [slice(hd * GLA_DK, (hd + 1) * GLA_DK) for hd in heads]
        vs = [slice(hd * GLA_DV, (hd + 1) * GLA_DV) for hd in heads]
        qh, kb, v, el, p, intra, kv = {}, {}, {}, {}, {}, {}, {}
        for c in chunks:
            rows = slice(c * CHUNK, (c + 1) * CHUNK)
            eb, enb, ebl, el[c] = _gla_decays(g[rows], tri)
            qk = qk_ref[rows, :]
            q_all = (qk[:, :GLA_QK] * (GLA_DK ** -0.5) * eb).astype(BF16)
            k_all = (qk[:, GLA_QK:] * enb).astype(BF16)
            kb_all = (qk[:, GLA_QK:] * ebl).astype(BF16)
            v_all = v_ref[rows, :].astype(BF16)
            for hd in heads:
                qh[c, hd], kb[c, hd], v[c, hd] = q_all[:, ks[hd]], kb_all[:, ks[hd]], v_all[:, vs[hd]]
                p[c, hd] = _dot_nt(qh[c, hd], k_all[:, ks[hd]])
        for c in chunks:
            for hd in heads:
                intra[c, hd] = _dot(jnp.where(mask, p[c, hd], 0.0), v[c, hd])
                kv[c, hd] = _dot_tn(v[c, hd], kb[c, hd])
        state = [s_ref[:, ks[hd]] for hd in heads]
        for c in chunks:
            rows = slice(c * CHUNK, (c + 1) * CHUNK)
            for hd in heads:
                sall_ref[c, :, ks[hd]] = state[hd]
                o_ref[rows, vs[hd]] = intra[c, hd] + _dot_nt(qh[c, hd], state[hd])
                state[hd] = state[hd] * el[c][:, ks[hd]] + kv[c, hd]
        for hd in heads:
            s_ref[:, ks[hd]] = state[hd]

    return _hosting_call(
        exchange, body, name="gla_scan_b" if reverse else "gla_scan_f", grid=(nb,),
        in_specs=[pl.BlockSpec((rb, 512), lambda i: (order(i), 2)), pl.BlockSpec((rb, 512), lambda i: (order(i), 3)),
                  pl.BlockSpec((rb, LANES), lambda i: (order(i), ZG_COL_BLOCK)), wg.spec, bg.spec],
        out_specs=[pl.BlockSpec((rb, GLA_W), lambda i: (order(i), 0)),
                   pl.BlockSpec((cpb, GLA_DV, GLA_QK), lambda i: (order(i), 0, 0))],
        out_shape=[_sds((n, GLA_W), F32), _sds((n // CHUNK, GLA_DV, GLA_QK), F32)],
        scratch_shapes=[pltpu.VMEM((GLA_DV, GLA_QK), F32)],
        compiler_params=_params(("arbitrary",)),
    )(z, z, z, wg.array, bg.array)


def _gla_scan_bwd(do, z, states, wg, bg, reverse, exchange=None):
    n = z.shape[0]
    rb = _gla_rows(n)
    nb = n // rb
    cpb = rb // CHUNK
    order = (lambda i: i) if reverse else (lambda i: nb - 1 - i)
    chunks = range(cpb) if reverse else range(cpb - 1, -1, -1)

    def body(do_ref, qk_ref, v_ref, zg_ref, sall_ref, wg_ref, bg_ref,
             dqk_ref, dv_ref, dzg_ref, dwg_ref, dbg_ref, ds_ref):
        i = pl.program_id(0)

        @pl.when(i == 0)
        def _():
            ds_ref[...] = jnp.zeros_like(ds_ref)

        tri, mask = _gla_masks(reverse)
        tri_t, _ = _gla_masks(not reverse)
        zg = zg_ref[...]
        pre, g = _gla_gate(zg, wg_ref, bg_ref)
        heads = range(GLA_HEADS)
        ks = [slice(hd * GLA_DK, (hd + 1) * GLA_DK) for hd in heads]
        vs = [slice(hd * GLA_DV, (hd + 1) * GLA_DV) for hd in heads]
        dec, full, qh, kh, kb, v, dout, p, dp = {}, {}, {}, {}, {}, {}, {}, {}, {}
        for c in chunks:
            rows = slice(c * CHUNK, (c + 1) * CHUNK)
            dec[c] = _gla_decays(g[rows], tri)
            eb, enb, ebl, _ = dec[c]
            qk = qk_ref[rows, :]
            q_f = qk[:, :GLA_QK] * (GLA_DK ** -0.5) * eb
            k_f = qk[:, GLA_QK:] * enb
            kb_f = qk[:, GLA_QK:] * ebl
            full[c] = (q_f, k_f, kb_f)
            q_all, k_all, kb_all = q_f.astype(BF16), k_f.astype(BF16), kb_f.astype(BF16)
            v_all, do_all = v_ref[rows, :].astype(BF16), do_ref[rows, :].astype(BF16)
            for hd in heads:
                qh[c, hd], kh[c, hd], kb[c, hd] = q_all[:, ks[hd]], k_all[:, ks[hd]], kb_all[:, ks[hd]]
                v[c, hd], dout[c, hd] = v_all[:, vs[hd]], do_all[:, vs[hd]]
                p[c, hd] = _dot_nt(qh[c, hd], kh[c, hd])
                dp[c, hd] = _dot_nt(dout[c, hd], v[c, hd])
        dv_i, dqh, dkh, dsq, state = {}, {}, {}, {}, {}
        for c in chunks:
            for hd in heads:
                pm = jnp.where(mask, p[c, hd], 0.0).astype(BF16)
                dpm = jnp.where(mask, dp[c, hd], 0.0).astype(BF16)
                state[c, hd] = sall_ref[c, :, ks[hd]]
                dv_i[c, hd] = _dot_tn(pm, dout[c, hd])
                dqh[c, hd] = _dot(dpm, kh[c, hd]) + _dot(dout[c, hd], state[c, hd])
                dkh[c, hd] = _dot_tn(dpm, qh[c, hd])
                dsq[c, hd] = _dot_tn(dout[c, hd], qh[c, hd])
        dstate = [ds_ref[:, ks[hd]] for hd in heads]
        dkb, sds = {}, {}
        for c in chunks:
            rows = slice(c * CHUNK, (c + 1) * CHUNK)
            el = dec[c][3]
            for hd in heads:
                dv_ref[rows, vs[hd]] = (dv_i[c, hd] + _dot_nt(kb[c, hd], dstate[hd])).astype(BF16)
                dkb[c, hd] = _dot(v[c, hd], dstate[hd])
                sds[c, hd] = jnp.sum(state[c, hd] * dstate[hd], axis=0, keepdims=True)
                dstate[hd] = dstate[hd] * el[:, ks[hd]] + dsq[c, hd]
        for hd in heads:
            ds_ref[:, ks[hd]] = dstate[hd]
        dgs = [None] * cpb
        for c in chunks:
            rows = slice(c * CHUNK, (c + 1) * CHUNK)
            eb, enb, ebl, el = dec[c]
            q_f, k_f, kb_f = full[c]
            dqh_c = jnp.concatenate([dqh[c, hd] for hd in heads], axis=1)
            dkh_c = jnp.concatenate([dkh[c, hd] for hd in heads], axis=1)
            dkb_c = jnp.concatenate([dkb[c, hd] for hd in heads], axis=1)
            sds_c = jnp.concatenate([sds[c, hd] for hd in heads], axis=1)
            dqk_ref[rows, :] = jnp.concatenate([dqh_c * eb * (GLA_DK ** -0.5), dkh_c * enb + dkb_c * ebl], axis=1).astype(BF16)
            dkb_kb = dkb_c * kb_f
            db = dqh_c * q_f - dkh_c * k_f - dkb_kb
            db_last = el * sds_c + jnp.sum(dkb_kb, axis=0, keepdims=True)
            dgs[c] = jnp.dot(tri_t, db, precision=lax.Precision.HIGHEST, preferred_element_type=F32) + db_last
        dg = jnp.concatenate(dgs, axis=0)
        dpre = dg * _sigmoid(-pre) * (1.0 / GATE_NORM)
        dzg_ref[...] = _dot_nt(dpre, wg_ref[...]).astype(BF16)
        _accumulate(dwg_ref, _dot_tn(zg, dpre), i == 0)
        _accumulate(dbg_ref, jnp.sum(dpre, axis=0, keepdims=True), i == 0)

    return _hosting_call(
        exchange, body, name="gla_scan_bwd_b" if reverse else "gla_scan_bwd_f", grid=(nb,),
        in_specs=[pl.BlockSpec((rb, GLA_W), lambda i: (order(i), 0)),
                  pl.BlockSpec((rb, 512), lambda i: (order(i), 2)), pl.BlockSpec((rb, 512), lambda i: (order(i), 3)),
                  pl.BlockSpec((rb, LANES), lambda i: (order(i), ZG_COL_BLOCK)),
                  pl.BlockSpec((cpb, GLA_DV, GLA_QK), lambda i: (order(i), 0, 0)), wg.spec, bg.spec],
        out_specs=[pl.BlockSpec((rb, 512), lambda i: (order(i), 0)), pl.BlockSpec((rb, 512), lambda i: (order(i), 0)),
                   pl.BlockSpec((rb, LANES), lambda i: (order(i), 0)),
                   pl.BlockSpec((LANES, GLA_QK), lambda i: (0, 0)), pl.BlockSpec((1, GLA_QK), lambda i: (0, 0))],
        out_shape=[_sds((n, 512), BF16), _sds((n, 512), BF16), _sds((n, LANES), BF16), _sds((LANES, GLA_QK), F32),
                   _sds((1, GLA_QK), F32)],
        scratch_shapes=[pltpu.VMEM((GLA_DV, GLA_QK), F32)],
        compiler_params=_params(("arbitrary",)),
    )(do, z, z, z, states, wg.array, bg.array)


NORM_NAMES = ("norm_mix_pre", "norm_mix_post", "norm_mlp_pre", "norm_mlp_post")
VEC512_NAMES = ("conv_b", "lru_ba_f", "lru_bx_f", "lru_lambda_f", "lru_ba_b", "lru_bx_b", "lru_lambda_b", "gla_head_norm")
VEC256_NAMES = ("gla_bg_f", "gla_bg_b")
LRU_MAT_NAMES = ("lru_wa_f", "lru_wx_f", "lru_wa_b", "lru_wx_b")
DIRS = ("f", "b")


def _prepare_params(w, gathered, depth):
    row_names = NORM_NAMES + ("conv_b", "gla_head_norm")
    ins = ([w[nm] for nm in row_names] + [w["lru_ba_" + d] for d in DIRS] + [w["lru_bx_" + d] for d in DIRS]
           + [w["lru_lambda_" + d] for d in DIRS] + [w["gla_bg_" + d] for d in DIRS]
           + [w["lru_wa_" + d].reshape(depth, LRU_W, LRU_HD) for d in DIRS]
           + [w["lru_wx_" + d].reshape(depth, LRU_W, LRU_HD) for d in DIRS]
           + [gathered["conv_w"], gathered["gla_wg_f"], gathered["gla_wg_b"], gathered["meta_tokens"]])
    n_rows = len(row_names)

    def body(*refs):
        rows_in = refs[:n_rows]
        ba, bx, lam, bg, wa, wx = (refs[n_rows + 2 * t:n_rows + 2 * t + 2] for t in range(6))
        convw_g, wgf_g, wgb_g, meta_g = refs[n_rows + 12:n_rows + 16]
        outs = refs[n_rows + 16:]
        rows_out = outs[:n_rows]
        convw, wcat, bias, lam_o, wg, bg_o, meta = outs[n_rows:]
        for l in range(depth):
            for src, dst in zip(rows_in, rows_out):
                dst[l] = src[pl.ds(l, 1), :]
            convw[l] = jnp.zeros((8, LRU_W), F32)
            for j in range(N_DEV):
                convw[l, 0:4, j * 64:(j + 1) * 64] = convw_g[j, l]
            for d in range(2):
                wcat[l, d] = jnp.zeros((LRU_W, 2 * LRU_W), BF16)
                for hd in range(LRU_HEADS):
                    rs = slice(hd * LRU_HD, (hd + 1) * LRU_HD)
                    wcat[l, d, rs, hd * LRU_HD:(hd + 1) * LRU_HD] = wa[d][l, rs, :].astype(BF16)
                    wcat[l, d, rs, LRU_W + hd * LRU_HD:LRU_W + (hd + 1) * LRU_HD] = wx[d][l, rs, :].astype(BF16)
                bias[l, d, :, 0:LRU_W] = ba[d][pl.ds(l, 1), :]
                bias[l, d, :, LRU_W:2 * LRU_W] = bx[d][pl.ds(l, 1), :]
                lam_o[l, d] = lam[d][pl.ds(l, 1), :]
                bg_o[l, d] = bg[d][pl.ds(l, 1), :]
                wg[l, d] = jnp.zeros((LANES, GLA_QK), BF16)
                src = wgf_g if d == 0 else wgb_g
                for j in range(N_DEV):
                    wg[l, d, d * GLA_RANK:(d + 1) * GLA_RANK, j * 32:(j + 1) * 32] = src[j, l].astype(BF16)
        for j in range(N_DEV):
            meta[:, j * LANES:(j + 1) * LANES] = meta_g[j]

    out_shape = ([_sds((depth, 1, w[nm].shape[1]), F32) for nm in row_names]
                 + [_sds((depth, 8, LRU_W), F32), _sds((depth, 2, LRU_W, 2 * LRU_W), BF16), _sds((depth, 2, 1, 2 * LRU_W), F32),
                    _sds((depth, 2, 1, LRU_W), F32), _sds((depth, 2, LANES, GLA_QK), BF16), _sds((depth, 2, 1, GLA_QK), F32),
                    _sds((N_META, D_MODEL), F32)])
    outs = pl.pallas_call(
        body, name="prepare_params", in_specs=[VMEM_SPEC] * len(ins), out_specs=[VMEM_SPEC] * len(out_shape),
        out_shape=out_shape, compiler_params=_params(None, 32),
    )(*ins)
    prepared = dict(zip(row_names, outs[:n_rows]))
    prepared.update(zip(("conv_w", "wcat", "lru_bias", "lru_lam", "wg", "gla_bg", "meta_tokens"), outs[n_rows:]))
    return prepared


class _Outbox:
    def __init__(self, on_complete):
        self.pending, self.on_complete = {}, on_complete

    def put(self, key, array, src, landing_shape):
        self.pending[key] = dict(array=array, src=src, landing=landing_shape, groups=list(range(len(PEER_GROUPS))))

    def exchange(self, wanted=None):
        ex, tickets = _Exchange(), []
        for key, item in self.pending.items():
            groups = [g for g in item["groups"] if wanted is None or (key[0], g) in wanted]
            out = None
            for g in groups:
                landing = item["landing"] if out is None else out
                out = ex.add(item["array"], item["src"], landing, _slab, peers=PEER_GROUPS[g], local=(g == 0))
                item["groups"].remove(g)
            if groups:
                tickets.append((key, out))
        return ex, tickets

    def store(self, tickets, landed):
        for key, out in tickets:
            item = self.pending[key]
            item["landing"] = landed[out]
            if not item["groups"]:
                del self.pending[key]
                self.on_complete(key, landed[out])


def _install_weight(p):
    def install(key, g):
        nm, l = key
        if nm == "w_in":
            g = jnp.pad(jnp.concatenate([g[j] for j in range(N_DEV)], axis=1), ((0, 0), (0, Z_W - D_IN)))
        elif nm == "w_out":
            g = g.reshape(D_MODEL, D_MODEL)
        elif nm == "w_mlp_down":
            g = g.reshape(D_FF, D_MODEL)
        p.setdefault(nm, {})[l] = g
    return install


def _request_weight(gather, shards, nm, l):
    gather.put((nm, l), shards[nm], _layer_of(l), _sds((N_DEV,) + shards[nm].shape[1:], BF16))


def _layer_fwd(h, l, p, gather, shards, depth):
    lp = lambda name, *index: _LayerParam(p[name], l, *index)
    s = dict(h=h)

    def hosted(fn, wanted, *args):
        ex, tickets = gather.exchange(wanted)
        outs = fn(*args, ex)
        own = len(outs) - len(ex.landings)
        gather.store(tickets, outs[own:])
        return outs[:own]

    _request_weight(gather, shards, "w_mlp_up", l)
    _request_weight(gather, shards, "w_mlp_down", l)
    s["hn"], s["z"] = hosted(_norm_in_proj, [("w_mlp_up", 0), ("w_out", 0)], h, lp("norm_mix_pre"), p["w_in"][l])
    s["xc"] = _conv_fwd(s["z"], lp("conv_w"), lp("conv_b"))
    plan = {"f": ([("w_mlp_up", 1), ("w_out", 1)], [("w_mlp_up", 2), ("w_out", 2)]),
            "b": ([("w_mlp_down", 0)], [("w_mlp_down", 1)])}
    for d, name in enumerate(DIRS):
        s["h_" + name], = hosted(_lru_scan, plan[name][0], s["xc"], lp("wcat", d), lp("lru_bias", d), lp("lru_lam", d), d == 1)
        s["o_" + name], s["s_" + name] = hosted(_gla_scan, plan[name][1], s["z"], lp("wg", d), lp("gla_bg", d), d == 1)
    s["ymix"] = _mix_epilogue(s["h_f"], s["h_b"], s["o_f"], s["o_b"], s["z"], lp("gla_head_norm"))
    s["mix"], s["h_mid"] = hosted(_out_proj, [("w_mlp_down", 2)], s["ymix"], p["w_out"][l], h, lp("norm_mix_post"))
    if l + 1 < depth:
        _request_weight(gather, shards, "w_in", l + 1)
        _request_weight(gather, shards, "w_out", l + 1)
    s["hn2"], s["up"], s["ff"], h_out = hosted(
        _mlp_fwd, None, s["h_mid"], lp("norm_mlp_pre"), p["w_mlp_up"][l], p["w_mlp_down"][l], lp("norm_mlp_post"))
    return h_out, s


def _layer_bwd(dh_out, l, p, s, outbox):
    lp = lambda name, *index: _LayerParam(p[name], l, *index)
    g = {}

    def hosted(fn, wanted, *args):
        ex, tickets = outbox.exchange(wanted)
        outs = fn(*args, ex)
        own = len(outs) - len(ex.landings)
        outbox.store(tickets, outs[own:])
        return outs[:own]

    d_ff, dup, dh_mid, g["norm_mlp_post"], g["norm_mlp_pre"] = hosted(
        _mlp_bwd, None, dh_out, s["ff"], s["up"], s["h_mid"], lp("norm_mlp_pre"), p["w_mlp_up"][l], p["w_mlp_down"][l],
        lp("norm_mlp_post"))
    _send_grad(outbox, "w_mlp_down", l, _matmul_tn(s["up"], d_ff, "grad_w_down", a_map=_relu_squared)
               .reshape(N_DEV, D_FF // N_DEV, D_MODEL))
    _send_grad(outbox, "w_mlp_up", l, _matmul_tn(s["hn2"], dup, "grad_w_up", column_slabs=True))
    dmix, dymix, g["norm_mix_post"] = hosted(_out_proj_bwd, [("w_mlp_down", 0)], dh_mid, s["mix"], lp("norm_mix_post"),
                                             p["w_out"][l])
    grad_w_out = _matmul_tn(s["ymix"], dmix, "grad_w_out").reshape(N_DEV, D_MODEL // N_DEV, D_MODEL)
    dhs, dgate, do, dgout, g["gla_head_norm"] = _mix_epilogue_bwd(
        dymix, s["h_f"], s["h_b"], s["o_f"], s["o_b"], s["z"], lp("gla_head_norm"))
    plan = {"f": ([("w_mlp_down", 1)], [("w_mlp_down", 2), ("w_mlp_up", 0)]), "b": ([("w_mlp_up", 1)], [("w_mlp_up", 2)])}
    dqk, dv, dzg, dxc = {}, {}, {}, {}
    for d, name in enumerate(DIRS):
        dqk[name], dv[name], dzg[name], g["wg_" + name], g["gla_bg_" + name] = hosted(
            _gla_scan_bwd, plan[name][0], do, s["z"], s["s_" + name], lp("wg", d), lp("gla_bg", d), d == 1)
        dxc[name], g["wcat_" + name], g["lru_bias_" + name], g["lru_lambda_" + name] = hosted(
            _lru_scan_bwd, plan[name][1], dhs, s["xc"], s["h_" + name], lp("wcat", d), lp("lru_bias", d), lp("lru_lam", d),
            d == 1)
    _send_grad(outbox, "w_out", l, grad_w_out)
    dxbr, g["conv_w"], g["conv_b"] = _conv_bwd(dxc["f"], dxc["b"], s["z"], lp("conv_w"))
    dz, dh_in, g["norm_mix_pre"] = hosted(
        _in_proj_bwd, [("w_out", group) for group in range(len(PEER_GROUPS))],
        (dxbr, dgate, dqk["f"], dqk["b"], dv["f"], dv["b"], dgout, dzg["f"], dzg["b"]),
        p["w_in"][l], s["h"], lp("norm_mix_pre"), dh_mid)
    return dh_in, g, dz


def _send_grad(outbox, nm, l, slabs):
    outbox.put((nm, l), slabs, _slab, _sds(slabs.shape, slabs.dtype))


def _w_in_slabs(grad_w_in):
    shard = D_IN // N_DEV
    return jnp.stack([grad_w_in[:, j * shard:(j + 1) * shard] for j in range(N_DEV)])


def _folded_block(hd):
    return slice((hd // 2) * LRU_HD, (hd // 2 + 1) * LRU_HD), slice((hd % 2) * LRU_HD, (hd % 2 + 1) * LRU_HD)


def _pack_small_grads(grads, dh0, depth):
    per_layer = ("norm_mix_pre", "norm_mix_post", "norm_mlp_pre", "norm_mlp_post", "conv_b", "gla_head_norm",
                 "lru_bias_f", "lru_bias_b", "lru_lambda_f", "lru_lambda_b", "gla_bg_f", "gla_bg_b",
                 "wcat_f", "wcat_b", "conv_w", "wg_f", "wg_b")
    ins = [grads[l][nm] for l in range(depth) for nm in per_layer]
    k = len(per_layer)
    meta_rows = PAD_ROWS // N_META

    def body(*refs):
        g = [dict(zip(per_layer, refs[l * k:(l + 1) * k])) for l in range(depth)]
        dh0_ref = refs[depth * k]
        norms, v512, v256, mats, convw, wgf, wgb, meta = refs[depth * k + 1:]
        v256[...] = jnp.zeros_like(v256)
        for l in range(depth):
            for p_, nm in enumerate(NORM_NAMES):
                norms[pl.ds(2 * p_ + l, 1), :] = g[l][nm][...]
            rows512 = [g[l]["conv_b"][...], g[l]["lru_bias_f"][:, 0:LRU_W], g[l]["lru_bias_f"][:, LRU_W:2 * LRU_W],
                       g[l]["lru_lambda_f"][...], g[l]["lru_bias_b"][:, 0:LRU_W], g[l]["lru_bias_b"][:, LRU_W:2 * LRU_W],
                       g[l]["lru_lambda_b"][...], g[l]["gla_head_norm"][...]]
            for p_, row in enumerate(rows512):
                v512[pl.ds(2 * p_ + l, 1), :] = row
            for p_, nm in enumerate(("gla_bg_f", "gla_bg_b")):
                v256[pl.ds(2 * p_ + l, 1), :] = g[l][nm][...]
            for d, name in enumerate(DIRS):
                for hd in range(LRU_HEADS):
                    rs = slice(hd * LRU_HD, (hd + 1) * LRU_HD)
                    dst_rows, dst_cols = _folded_block(hd)
                    mats[2 * d, l, dst_rows, dst_cols] = g[l]["wcat_" + name][rs, hd * LRU_HD:(hd + 1) * LRU_HD].astype(BF16)
                    mats[2 * d + 1, l, dst_rows, dst_cols] = (
                        g[l]["wcat_" + name][rs, LRU_W + hd * LRU_HD:LRU_W + (hd + 1) * LRU_HD].astype(BF16))
            for j in range(N_DEV):
                convw[j, l] = g[l]["conv_w"][0:4, j * 64:(j + 1) * 64]
                wgf[j, l] = g[l]["wg_f"][0:GLA_RANK, j * 32:(j + 1) * 32]
                wgb[j, l] = g[l]["wg_b"][GLA_RANK:2 * GLA_RANK, j * 32:(j + 1) * 32]
        for j in range(N_DEV):
            meta[j] = dh0_ref[:, j * LANES:(j + 1) * LANES]

    out_shape = [_sds((8, D_MODEL), F32), _sds((16, LRU_W), F32), _sds((8, GLA_QK), F32),
                 _sds((4, depth, LRU_W // 2, 2 * LRU_HD), BF16),
                 _sds((N_DEV, depth, 4, 64), F32), _sds((N_DEV, depth, GLA_RANK, 32), F32), _sds((N_DEV, depth, GLA_RANK, 32), F32),
                 _sds((N_DEV, N_META, LANES), F32)]
    return pl.pallas_call(
        body, name="pack_small_grads", grid=(1,),
        in_specs=[VMEM_SPEC] * (depth * k) + [pl.BlockSpec((N_META, D_MODEL), lambda i: (meta_rows, 0))],
        out_specs=[VMEM_SPEC] * len(out_shape), out_shape=out_shape, compiler_params=_params(("arbitrary",), 32),
    )(*ins, dh0)


def _my_index():
    return 4 * lax.axis_index("x") + 2 * lax.axis_index("y") + lax.axis_index("c")


def _peer(k):
    x, y, c = lax.axis_index("x"), lax.axis_index("y"), lax.axis_index("c")
    px = x ^ ((k >> 2) & 1)
    py = y ^ ((k >> 1) & 1)
    pc = c ^ (k & 1)
    return (px, py, pc), 4 * px + 2 * py + pc


ALL_PEERS = tuple(range(1, N_DEV))
PEER_GROUPS = ((1, 2, 3), (4, 5), (6, 7))


class _Exchange:
    def __init__(self):
        self.inputs, self.landings, self.transfers = [], [], []

    def add(self, array, src, landing, dst, peers=ALL_PEERS, local=True):
        if isinstance(landing, int):
            out = landing
        else:
            out = len(self.landings)
            self.landings.append(landing)
        self.transfers.append((len(self.inputs), src, out, dst, tuple(peers), local))
        self.inputs.append(array)
        return out

    def _pairs(self):
        return [(t, k) for t, tr in enumerate(self.transfers) for k in tr[4]]

    def _locals(self):
        return [t for t, tr in enumerate(self.transfers) if tr[5]]

    def out_shapes(self):
        return [g if isinstance(g, jax.ShapeDtypeStruct) else _sds(g.shape, g.dtype) for g in self.landings]

    def continued(self):
        return [(b, g) for b, g in enumerate(self.landings) if not isinstance(g, jax.ShapeDtypeStruct)]

    def sem_shapes(self):
        return [pltpu.SemaphoreType.DMA((max(len(self._pairs()), 1),)), pltpu.SemaphoreType.DMA((max(len(self._pairs()), 1),)),
                pltpu.SemaphoreType.DMA((max(len(self._locals()), 1),))]

    def _local(self, ins, outs, sems):
        me = _my_index()
        copies = []
        for s, t in enumerate(self._locals()):
            a, src, b, dst, _, _ = self.transfers[t]
            copies.append(pltpu.make_async_copy(src(ins[a], me), dst(outs[b], me), sems[2].at[s]))
        return copies

    def _remote(self, ins, outs, sems, sending):
        copies = []
        for s, (t, k) in enumerate(self._pairs()):
            a, src, b, dst, _, _ = self.transfers[t]
            peer, peer_index = _peer(k)
            copies.append(pltpu.make_async_remote_copy(
                src_ref=src(ins[a], peer_index), dst_ref=dst(outs[b], _my_index() if sending else peer_index),
                send_sem=sems[0].at[s], recv_sem=sems[1].at[s], device_id=peer, device_id_type=MESH_ID))
        return copies

    def start(self, ins, outs, sems):
        for cp in self._local(ins, outs, sems) + self._remote(ins, outs, sems, True):
            cp.start()

    def wait(self, ins, outs, sems):
        for cp in self._remote(ins, outs, sems, False):
            cp.wait_recv()
        for cp in self._remote(ins, outs, sems, True):
            cp.wait_send()
        for cp in self._local(ins, outs, sems):
            cp.wait()

    def run(self, name):
        return _hosting_call(self, None, name=name, grid=(), in_specs=[], out_specs=[], out_shape=[], scratch_shapes=[],
                             compiler_params=pltpu.CompilerParams(has_side_effects=True))()


def _hosting_call(exchange, body, *, name, grid, in_specs, out_specs, out_shape, scratch_shapes, compiler_params):
    if exchange is None or not exchange.transfers:
        return pl.pallas_call(body, name=name, grid=grid, in_specs=in_specs, out_specs=out_specs, out_shape=out_shape,
                              scratch_shapes=scratch_shapes, compiler_params=compiler_params)
    n_in, n_out, n_scr = len(in_specs), len(out_specs), len(scratch_shapes)
    x_in, x_out = len(exchange.inputs), len(exchange.landings)
    continued = exchange.continued()

    def hosted(*refs):
        ins, x_ins = refs[:n_in], refs[n_in:n_in + x_in]
        o0 = n_in + x_in + len(continued)
        outs, x_outs = refs[o0:o0 + n_out], refs[o0 + n_out:o0 + n_out + x_out]
        s0 = o0 + n_out + x_out
        scratch, sems = refs[s0:s0 + n_scr], refs[s0 + n_scr:]
        if body is None:
            exchange.start(x_ins, x_outs, sems)
            exchange.wait(x_ins, x_outs, sems)
            return
        ids = [pl.program_id(a) for a in range(len(grid))]
        first = functools.reduce(jnp.logical_and, [i == 0 for i in ids])
        last = functools.reduce(jnp.logical_and, [i == g - 1 for i, g in zip(ids, grid)])

        @pl.when(first)
        def _():
            exchange.start(x_ins, x_outs, sems)

        body(*ins, *outs, *scratch)

        @pl.when(last)
        def _():
            exchange.wait(x_ins, x_outs, sems)

    aliases = {n_in + x_in + i: n_out + b for i, (b, _) in enumerate(continued)}
    kwargs = dict(grid=grid) if grid else {}
    call = pl.pallas_call(
        hosted, name=name, in_specs=list(in_specs) + [ANY_SPEC] * (x_in + len(continued)),
        out_specs=list(out_specs) + [ANY_SPEC] * x_out, out_shape=list(out_shape) + exchange.out_shapes(),
        scratch_shapes=list(scratch_shapes) + exchange.sem_shapes(), compiler_params=compiler_params,
        input_output_aliases=aliases, **kwargs)
    return lambda *operands: call(*operands, *exchange.inputs, *[g for _, g in continued])


def _whole(ref, j):
    return ref


def _slab(ref, j):
    return ref.at[j]


def _layer_of(l):
    return lambda ref, j: ref.at[l]


def _adamw(g, w, m, v):
    nm = ADAM_B1 * m + (1.0 - ADAM_B1) * g
    nv = ADAM_B2 * v + (1.0 - ADAM_B2) * jnp.square(g)
    m_hat = nm / (1.0 - ADAM_B1 ** ADAM_STEP)
    v_hat = nv / (1.0 - ADAM_B2 ** ADAM_STEP)
    return -ADAM_LR * (m_hat / (jnp.sqrt(v_hat) + ADAM_EPS) + ADAM_WD * w), nm, nv


def _sum_parts(p_ref):
    g = p_ref[0].astype(F32)
    for j in range(1, N_DEV):
        g = g + p_ref[j].astype(F32)
    return g


def _adamw_sharded(parts, w, m, v, name):
    shape = w.shape
    lead, (rows, cols) = shape[:-2], shape[-2:]
    tr = min(rows, ROW_BLOCK)
    assert rows % tr == 0
    steps = rows // tr
    nl = len(lead)
    spec = pl.BlockSpec((None,) * nl + (tr, cols), lambda *idx: idx + (0,))
    per_layer = isinstance(parts, (list, tuple))
    if per_layer:
        def part_spec(l):
            return pl.BlockSpec((N_DEV, tr, cols), lambda li, r: (0, jnp.where(li == l, r, jnp.where(li < l, 0, steps - 1)), 0))
        part_specs = [part_spec(l) for l in range(len(parts))]
    else:
        parts = [parts]
        part_specs = [pl.BlockSpec((N_DEV,) + (None,) * nl + (tr, cols), lambda *idx: (0,) + idx + (0,))]
    count = len(parts)

    def body(*refs):
        p_refs = refs[:count]
        w_ref, m_ref, v_ref, g_ref, d_ref, nm_ref, nv_ref = refs[count:]

        def update(p_ref):
            g = _sum_parts(p_ref)
            g_ref[...] = g
            d_ref[...], nm_ref[...], nv_ref[...] = _adamw(g, w_ref[...], m_ref[...], v_ref[...])

        if per_layer:
            for l in range(count):
                pl.when(pl.program_id(0) == l)(functools.partial(update, p_refs[l]))
        else:
            update(p_refs[0])

    return pl.pallas_call(
        body, name=name, grid=lead + (steps,),
        in_specs=part_specs + [spec, spec, spec], out_specs=[spec] * 4, out_shape=[_sds(shape, F32)] * 4,
        compiler_params=_params(("arbitrary",) * (nl + 1)),
    )(*parts, w, m, v)


def _adamw_replicated(gathered, w, m, v, depth):
    names = NORM_NAMES + VEC512_NAMES + VEC256_NAMES + LRU_MAT_NAMES
    count = len(names)

    def body(*refs):
        norms, v512, v256, mats = refs[:4]
        w_refs, m_refs, v_refs = (refs[4 + t * count:4 + (t + 1) * count] for t in range(3))
        outs = refs[4 + 3 * count:4 + 7 * count]
        sum_norms, sum_512, sum_256, unfolded = refs[4 + 7 * count:]
        sum_norms[...] = _sum_parts(norms)
        sum_512[...] = _sum_parts(v512)
        sum_256[...] = _sum_parts(v256)
        for n_, nm in enumerate(names):
            if nm in NORM_NAMES:
                g = sum_norms[pl.ds(depth * NORM_NAMES.index(nm), depth), :]
            elif nm in VEC512_NAMES:
                g = sum_512[pl.ds(depth * VEC512_NAMES.index(nm), depth), :]
            elif nm in VEC256_NAMES:
                g = sum_256[pl.ds(depth * VEC256_NAMES.index(nm), depth), :]
            else:
                p_ = LRU_MAT_NAMES.index(nm)
                folded = mats[0, p_].astype(F32)
                for j in range(1, N_DEV):
                    folded = folded + mats[j, p_].astype(F32)
                for hd in range(LRU_HEADS):
                    src_rows, src_cols = _folded_block(hd)
                    unfolded[:, hd * LRU_HD:(hd + 1) * LRU_HD, :] = folded[:, src_rows, src_cols]
                g = unfolded[...]
            delta, nm_, nv_ = _adamw(g, w_refs[n_][...], m_refs[n_][...], v_refs[n_][...])
            outs[n_][...] = g
            outs[count + n_][...] = delta
            outs[2 * count + n_][...] = nm_
            outs[3 * count + n_][...] = nv_

    shapes = [_sds(w[nm].shape, F32) for nm in names]
    ins = list(gathered) + [t[nm] for t in (w, m, v) for nm in names]
    outs = pl.pallas_call(
        body, name="adamw_replicated", in_specs=[VMEM_SPEC] * len(ins), out_specs=[VMEM_SPEC] * (4 * count),
        out_shape=shapes * 4,
        scratch_shapes=[pltpu.VMEM(gathered[0].shape[1:], F32), pltpu.VMEM(gathered[1].shape[1:], F32),
                        pltpu.VMEM(gathered[2].shape[1:], F32), pltpu.VMEM((depth, LRU_W, LRU_HD), F32)],
        compiler_params=_params(None, 48),
    )(*ins)
    return [dict(zip(names, outs[t * count:(t + 1) * count])) for t in range(4)]


WEIGHT_NAMES = ("meta_tokens", "norm_mix_pre", "norm_mix_post", "norm_mlp_pre", "norm_mlp_post", "w_in", "conv_w", "conv_b",
                "lru_wa_f", "lru_ba_f", "lru_wx_f", "lru_bx_f", "lru_lambda_f", "lru_wa_b", "lru_ba_b", "lru_wx_b",
                "lru_bx_b", "lru_lambda_b", "gla_wg_f", "gla_bg_f", "gla_wg_b", "gla_bg_b", "gla_head_norm", "w_out",
                "w_mlp_up", "w_mlp_down")
MATMUL_WEIGHTS = ("w_in", "w_out", "w_mlp_up", "w_mlp_down")
SMALL_SHARDED = ("conv_w", "gla_wg_f", "gla_wg_b", "meta_tokens")


def kernel(x, meta_tokens, norm_mix_pre, norm_mix_post, norm_mlp_pre, norm_mlp_post, w_in, conv_w, conv_b, lru_wa_f, lru_ba_f, lru_wx_f, lru_bx_f, lru_lambda_f, lru_wa_b, lru_ba_b, lru_wx_b, lru_bx_b, lru_lambda_b, gla_wg_f, gla_bg_f, gla_wg_b, gla_bg_b, gla_head_norm, w_out, w_mlp_up, w_mlp_down, loss_target, m_meta_tokens, m_norm_mix_pre, m_norm_mix_post, m_norm_mlp_pre, m_norm_mlp_post, m_w_in, m_conv_w, m_conv_b, m_lru_wa_f, m_lru_ba_f, m_lru_wx_f, m_lru_bx_f, m_lru_lambda_f, m_lru_wa_b, m_lru_ba_b, m_lru_wx_b, m_lru_bx_b, m_lru_lambda_b, m_gla_wg_f, m_gla_bg_f, m_gla_wg_b, m_gla_bg_b, m_gla_head_norm, m_w_out, m_w_mlp_up, m_w_mlp_down, v_meta_tokens, v_norm_mix_pre, v_norm_mix_post, v_norm_mlp_pre, v_norm_mlp_post, v_w_in, v_conv_w, v_conv_b, v_lru_wa_f, v_lru_ba_f, v_lru_wx_f, v_lru_bx_f, v_lru_lambda_f, v_lru_wa_b, v_lru_ba_b, v_lru_wx_b, v_lru_bx_b, v_lru_lambda_b, v_gla_wg_f, v_gla_bg_f, v_gla_wg_b, v_gla_bg_b, v_gla_head_norm, v_w_out, v_w_mlp_up, v_w_mlp_down):
    args = locals()
    w = {nm: args[nm] for nm in WEIGHT_NAMES}
    m = {nm: args["m_" + nm] for nm in WEIGHT_NAMES}
    v = {nm: args["v_" + nm] for nm in WEIGHT_NAMES}
    depth = w_in.shape[0]

    shards = {nm: w[nm].astype(BF16) for nm in MATMUL_WEIGHTS}
    p = {}
    gather = _Outbox(_install_weight(p))
    _request_weight(gather, shards, "w_in", 0)
    _request_weight(gather, shards, "w_out", 0)
    ex, tickets = gather.exchange([("w_in", group) for group in range(len(PEER_GROUPS))])
    first_small = len(ex.landings)
    for nm in SMALL_SHARDED:
        ex.add(w[nm], _whole, _sds((N_DEV,) + w[nm].shape, F32), _slab)
    landed = ex.run("all_gather")
    gather.store(tickets, landed)
    p.update(_prepare_params(w, dict(zip(SMALL_SHARDED, landed[first_small:])), depth))

    h = jnp.concatenate([jnp.zeros((PAD_ROWS, D_MODEL), F32), p["meta_tokens"], x[0]], axis=0)
    saved = []
    for l in range(depth):
        h, s = _layer_fwd(h, l, p, gather, shards, depth)
        saved.append(s)
    dh, loss_part = _loss_and_grad(h, loss_target[0])
    loss = lax.psum(loss_part[0, 0], ("x", "y", "c"))

    received = {}
    outbox = _Outbox(received.__setitem__)
    grads = [None] * depth
    for l in reversed(range(depth)):
        dh, grads[l], dz = _layer_bwd(dh, l, p, saved[l], outbox)
        if l > 0:
            _send_grad(outbox, "w_in", l, _w_in_slabs(_matmul_tn(saved[l]["hn"], dz, "grad_w_in")))
    grad_x = dh[PAD_ROWS + N_META:][None]

    small = _pack_small_grads(grads, dh, depth)
    rep_bufs, small_slabs = small[:4], small[4:]
    ex, tickets = outbox.exchange()
    first_small = len(ex.landings)
    for g in small_slabs:
        ex.add(g, _slab, _sds(g.shape, F32), _slab)
    for g in rep_bufs:
        ex.add(g, _whole, _sds((N_DEV,) + g.shape, g.dtype), _slab)
    grad_w_in, *landed = _matmul_tn(saved[0]["hn"], dz, "grad_w_in", exchange=ex)
    outbox.store(tickets, landed)
    small_received = landed[first_small:first_small + len(small_slabs)]
    rep_received = landed[first_small + len(small_slabs):]
    _send_grad(outbox, "w_in", 0, _w_in_slabs(grad_w_in))
    ex, tickets = outbox.exchange()
    outbox.store(tickets, ex.run("exchange_grads"))

    results = [{}, {}, {}, {}]
    for nm in MATMUL_WEIGHTS:
        parts = [received[(nm, l)] for l in range(depth)]
        for t, out in enumerate(_adamw_sharded(parts, w[nm], m[nm], v[nm], "adamw_" + nm)):
            results[t][nm] = out
    for nm, parts in zip(SMALL_SHARDED, small_received):
        for t, out in enumerate(_adamw_sharded(parts, w[nm], m[nm], v[nm], "adamw_" + nm)):
            results[t][nm] = out

    def kernel_side(tree):
        return {nm: tree[nm].reshape(depth, LRU_W, LRU_HD) if nm in LRU_MAT_NAMES else tree[nm]
                for nm in NORM_NAMES + VEC512_NAMES + VEC256_NAMES + LRU_MAT_NAMES}

    for t, tree in enumerate(_adamw_replicated(rep_received, kernel_side(w), kernel_side(m), kernel_side(v), depth)):
        for nm, out in tree.items():
            results[t][nm] = out.reshape(w[nm].shape)
    return (loss, grad_x, *[results[t][nm] for t in range(4) for nm in WEIGHT_NAMES])
```

```python
import functools

import jax
import jax.numpy as jnp
from jax import lax
from jax.experimental import pallas as pl
from jax.experimental.pallas import tpu as pltpu

F32 = jnp.float32
BF16 = jnp.bfloat16

N_DEV = 8
D_MODEL = 1024
N_META = 16
ROW_BLOCK = 256
PAD_ROWS = ROW_BLOCK - N_META
CHUNK = 128
LRU_W = 512
LRU_HEADS = 8
LRU_HD = 64
LRU_C = 8.0
GLA_HEADS = 4
GLA_DK = 64
GLA_DV = 128
GLA_QK = GLA_HEADS * GLA_DK
GLA_W = GLA_HEADS * GLA_DV
GLA_RANK = 16
GATE_NORM = 16.0
D_FF = 4096
D_IN = 2592
Z_W = 2688
ZG_COL_BLOCK = 2560 // 128
EPS = 1e-6
LANES = 128

ADAM_LR = 0.001
ADAM_B1 = 0.9
ADAM_B2 = 0.999
ADAM_EPS = 1e-08
ADAM_WD = 0.01
ADAM_STEP = 10

VMEM_SPEC = pl.BlockSpec(memory_space=pltpu.VMEM)
ANY_SPEC = pl.BlockSpec(memory_space=pl.ANY)
MESH_ID = pl.DeviceIdType.MESH


def _sds(shape, dtype):
    return jax.ShapeDtypeStruct(shape, dtype)


def _params(sem=None, vmem_mb=None):
    kw = {}
    if sem is not None:
        kw["dimension_semantics"] = sem
    if vmem_mb is not None:
        kw["vmem_limit_bytes"] = vmem_mb * 2 ** 20
    return pltpu.CompilerParams(**kw)


def _row_tile(n, cap=768):
    for t in (768, 512, 384, 256):
        if t <= cap and n % t == 0:
            return t
    raise ValueError(n)


def _col_tile(k):
    for t in (1024, 896, 768, 640, 512, 384, 256, 128):
        if k % t == 0:
            return t
    raise ValueError(k)


def _sigmoid(x):
    return 0.5 * jnp.tanh(0.5 * x) + 0.5


def _gelu_and_grad(x):
    c = 0.7978845608028654
    inner = c * (x + 0.044715 * x * x * x)
    t = jnp.tanh(inner)
    gelu = 0.5 * x * (1.0 + t)
    dgelu = 0.5 * (1.0 + t) + 0.5 * x * (1.0 - t * t) * c * (1.0 + 3.0 * 0.044715 * x * x)
    return gelu, dgelu


def _one_minus_square(a, log_a):
    return jnp.tanh(-log_a) * (1.0 + a * a)


def _rms_fwd(x, g):
    rs = lax.rsqrt(jnp.mean(x * x, axis=-1, keepdims=True) + EPS)
    return x * rs * g


def _rms_bwd(x, g, dy):
    rs = lax.rsqrt(jnp.mean(x * x, axis=-1, keepdims=True) + EPS)
    xh = x * rs
    dyg = dy * g
    dx = rs * (dyg - xh * jnp.mean(dyg * xh, axis=-1, keepdims=True))
    return dx, jnp.sum(dy * xh, axis=0, keepdims=True)


def _dot(a, b):
    return jnp.dot(a.astype(BF16), b.astype(BF16), preferred_element_type=F32)


def _dot_nt(a, b):
    return lax.dot_general(a.astype(BF16), b.astype(BF16), (((1,), (1,)), ((), ())), preferred_element_type=F32)


def _dot_tn(a, b):
    return lax.dot_general(a.astype(BF16), b.astype(BF16), (((0,), (0,)), ((), ())), preferred_element_type=F32)


class _LayerParam:
    def __init__(self, array, *index):
        self.array = array
        self.index = index

    @property
    def spec(self):
        lead = len(self.index)
        tail = self.array.shape[lead:]
        index = self.index
        return pl.BlockSpec((None,) * lead + tail, lambda *_: index + (0,) * len(tail))


def _row_ids(rows, block_index):
    return block_index * rows + lax.broadcasted_iota(jnp.int32, (rows, 1), 0)


def _accumulate(ref, value, first):
    @pl.when(first)
    def _():
        ref[...] = value

    @pl.when(jnp.logical_not(first))
    def _():
        ref[...] += value


def _norm_in_proj(h, g, w, exchange=None):
    n, d = h.shape
    zw = w.shape[1]
    tr = _row_tile(n)

    def body(h_ref, g_ref, w_ref, hn_ref, z_ref):
        hn = _rms_fwd(h_ref[...], g_ref[...]).astype(BF16)
        hn_ref[...] = hn
        z_ref[...] = jnp.dot(hn, w_ref[...], preferred_element_type=F32)

    return _hosting_call(
        exchange, body, name="norm_in_proj", grid=(n // tr,),
        in_specs=[pl.BlockSpec((tr, d), lambda i: (i, 0)), g.spec, VMEM_SPEC],
        out_specs=[pl.BlockSpec((tr, d), lambda i: (i, 0)), pl.BlockSpec((tr, zw), lambda i: (i, 0))],
        out_shape=[_sds((n, d), BF16), _sds((n, zw), F32)],
        scratch_shapes=[], compiler_params=_params(("arbitrary",), 48),
    )(h, g.array, w)


def _halo_specs(width, nb, col=0):
    per = ROW_BLOCK // 8
    prev = pl.BlockSpec((8, width), lambda i: (jnp.maximum(i * per - 1, 0), col))
    nxt = pl.BlockSpec((8, width), lambda i: (jnp.minimum((i + 1) * per, nb * per - 1), col))
    return prev, nxt


def _shift_down(x, prev8, d):
    n = x.shape[0]
    r = pltpu.roll(x, d, 0)
    p = pltpu.roll(prev8, d, 0)
    row8 = lax.broadcasted_iota(jnp.int32, (8, 1), 0)
    head = jnp.where(row8 < d, p, r[0:8])
    return jnp.concatenate([head, r[8:]], axis=0)


def _shift_up(x, next8, d):
    n = x.shape[0]
    r = pltpu.roll(x, n - d, 0)
    q = pltpu.roll(next8, 8 - d, 0)
    row8 = lax.broadcasted_iota(jnp.int32, (8, 1), 0)
    tail = jnp.where(row8 >= 8 - d, q, r[n - 8:])
    return jnp.concatenate([r[:n - 8], tail], axis=0)


def _conv_fwd(z, conv_w, conv_b):
    n = z.shape[0]
    nb = n // ROW_BLOCK
    prev_spec, next_spec = _halo_specs(LRU_W, nb)

    def body(cur_ref, prev_ref, next_ref, w_ref, b_ref, xc_ref):
        i = pl.program_id(0)
        cur = cur_ref[...]
        prev8 = prev_ref[...] * jnp.where(i > 0, 1.0, 0.0)
        next8 = next_ref[...] * jnp.where(i < nb - 1, 1.0, 0.0)
        w = [w_ref[pl.ds(k, 1), :] for k in range(4)]
        xc = (w[0] * _shift_down(cur, prev8, 2) + w[1] * _shift_down(cur, prev8, 1)
              + w[2] * cur + w[3] * _shift_up(cur, next8, 1) + b_ref[...])
        xc_ref[...] = xc

    return pl.pallas_call(
        body, name="conv_fwd", grid=(nb,),
        in_specs=[pl.BlockSpec((ROW_BLOCK, LRU_W), lambda i: (i, 0)), prev_spec, next_spec, conv_w.spec, conv_b.spec],
        out_specs=pl.BlockSpec((ROW_BLOCK, LRU_W), lambda i: (i, 0)),
        out_shape=_sds((n, LRU_W), F32),
        compiler_params=_params(("parallel",)),
    )(z, z, z, conv_w.array, conv_b.array)


def _conv_bwd(dxc_f, dxc_b, z, conv_w):
    n = z.shape[0]
    nb = n // ROW_BLOCK
    prev_spec, next_spec = _halo_specs(LRU_W, nb)
    row_spec = pl.BlockSpec((ROW_BLOCK, LRU_W), lambda i: (i, 0))

    def body(df_ref, dfp_ref, dfn_ref, db_ref, dbp_ref, dbn_ref, x_ref, xp_ref, xn_ref, w_ref,
             dx_ref, dw_ref, dbias_ref):
        i = pl.program_id(0)
        has_prev = jnp.where(i > 0, 1.0, 0.0)
        has_next = jnp.where(i < nb - 1, 1.0, 0.0)
        dxc = df_ref[...] + db_ref[...]
        dprev = (dfp_ref[...] + dbp_ref[...]) * has_prev
        dnext = (dfn_ref[...] + dbn_ref[...]) * has_next
        x = x_ref[...]
        xprev = xp_ref[...] * has_prev
        xnext = xn_ref[...] * has_next
        w = [w_ref[pl.ds(k, 1), :] for k in range(4)]
        dx_ref[...] = (w[0] * _shift_up(dxc, dnext, 2) + w[1] * _shift_up(dxc, dnext, 1)
                       + w[2] * dxc + w[3] * _shift_down(dxc, dprev, 1)).astype(BF16)
        dw = jnp.concatenate([
            jnp.sum(dxc * _shift_down(x, xprev, 2), axis=0, keepdims=True),
            jnp.sum(dxc * _shift_down(x, xprev, 1), axis=0, keepdims=True),
            jnp.sum(dxc * x, axis=0, keepdims=True),
            jnp.sum(dxc * _shift_up(x, xnext, 1), axis=0, keepdims=True),
            jnp.zeros((4, LRU_W), F32)], axis=0)
        _accumulate(dw_ref, dw, i == 0)
        _accumulate(dbias_ref, jnp.sum(dxc, axis=0, keepdims=True), i == 0)

    dx, dw, dbias = pl.pallas_call(
        body, name="conv_bwd", grid=(nb,),
        in_specs=[row_spec, prev_spec, next_spec, row_spec, prev_spec, next_spec, row_spec, prev_spec, next_spec,
                  conv_w.spec],
        out_specs=[row_spec, pl.BlockSpec((8, LRU_W), lambda i: (0, 0)), pl.BlockSpec((1, LRU_W), lambda i: (0, 0))],
        out_shape=[_sds((n, LRU_W), BF16), _sds((8, LRU_W), F32), _sds((1, LRU_W), F32)],
        compiler_params=_params(("arbitrary",)),
    )(dxc_f, dxc_f, dxc_f, dxc_b, dxc_b, dxc_b, z, z, z, conv_w.array)
    return dx, dw, dbias


def _mix_epilogue(h_f, h_b, o_f, o_b, z, head_norm):
    n = z.shape[0]
    tr = _row_tile(n)
    spec = pl.BlockSpec((tr, 512), lambda i: (i, 0))

    def body(hf_ref, hb_ref, of_ref, ob_ref, gate_ref, gout_ref, w_ref, y_ref):
        gelu, _ = _gelu_and_grad(gate_ref[...])
        y_ref[:, 0:LRU_W] = ((hf_ref[...] + hb_ref[...]) * gelu).astype(BF16)
        o = of_ref[...] + ob_ref[...]
        gout = gout_ref[...]
        silu = gout * _sigmoid(gout)
        w = w_ref[...]
        for hd in range(GLA_HEADS):
            cs = slice(hd * GLA_DV, (hd + 1) * GLA_DV)
            oh = o[:, cs]
            on = oh * lax.rsqrt(jnp.mean(oh * oh, axis=-1, keepdims=True) + EPS)
            y_ref[:, LRU_W + hd * GLA_DV:LRU_W + (hd + 1) * GLA_DV] = (on * w[:, cs] * silu[:, cs]).astype(BF16)

    return pl.pallas_call(
        body, name="mix_epilogue", grid=(n // tr,),
        in_specs=[spec, spec, spec, spec, pl.BlockSpec((tr, 512), lambda i: (i, 1)),
                  pl.BlockSpec((tr, 512), lambda i: (i, 4)), head_norm.spec],
        out_specs=pl.BlockSpec((tr, D_MODEL), lambda i: (i, 0)),
        out_shape=_sds((n, D_MODEL), BF16),
        compiler_params=_params(("parallel",), 40),
    )(h_f, h_b, o_f, o_b, z, z, head_norm.array)


def _mix_epilogue_bwd(dymix, h_f, h_b, o_f, o_b, z, head_norm):
    n = z.shape[0]
    tr = _row_tile(n)
    spec = pl.BlockSpec((tr, 512), lambda i: (i, 0))

    def body(dyl_ref, dyg_ref, hf_ref, hb_ref, of_ref, ob_ref, gate_ref, gout_ref, w_ref,
             dhs_ref, dgate_ref, do_ref, dgout_ref, dw_ref):
        i = pl.program_id(0)
        dyl = dyl_ref[...]
        gelu, dgelu = _gelu_and_grad(gate_ref[...])
        dhs_ref[...] = dyl * gelu
        dgate_ref[...] = (dyl * (hf_ref[...] + hb_ref[...]) * dgelu).astype(BF16)
        dyg = dyg_ref[...]
        o = of_ref[...] + ob_ref[...]
        gout = gout_ref[...]
        sg = _sigmoid(gout)
        silu = gout * sg
        dsilu = sg * (1.0 + gout * (1.0 - sg))
        w = w_ref[...]
        dws = []
        for hd in range(GLA_HEADS):
            cs = slice(hd * GLA_DV, (hd + 1) * GLA_DV)
            oh = o[:, cs]
            rs = lax.rsqrt(jnp.mean(oh * oh, axis=-1, keepdims=True) + EPS)
            on = oh * rs
            dy = dyg[:, cs]
            dgout_ref[:, cs] = (dy * on * w[:, cs] * dsilu[:, cs]).astype(BF16)
            dys = dy * silu[:, cs]
            dws.append(jnp.sum(dys * on, axis=0, keepdims=True))
            don = dys * w[:, cs]
            do_ref[:, cs] = (rs * (don - on * jnp.mean(don * on, axis=-1, keepdims=True))).astype(BF16)
        _accumulate(dw_ref, jnp.concatenate(dws, axis=1), i == 0)

    return pl.pallas_call(
        body, name="mix_epilogue_bwd", grid=(n // tr,),
        in_specs=[pl.BlockSpec((tr, 512), lambda i: (i, 0)), pl.BlockSpec((tr, 512), lambda i: (i, 1)),
                  spec, spec, spec, spec, pl.BlockSpec((tr, 512), lambda i: (i, 1)),
                  pl.BlockSpec((tr, 512), lambda i: (i, 4)), head_norm.spec],
        out_specs=[spec, spec, spec, spec, pl.BlockSpec((1, GLA_W), lambda i: (0, 0))],
        out_shape=[_sds((n, 512), F32)] + [_sds((n, 512), BF16)] * 3 + [_sds((1, GLA_W), F32)],
        compiler_params=_params(("arbitrary",), 48),
    )(dymix, dymix, h_f, h_b, o_f, o_b, z, z, head_norm.array)


def _out_proj(ymix, w_out, h, g, exchange=None):
    n, d = h.shape
    tr = _row_tile(n)
    spec = pl.BlockSpec((tr, d), lambda i: (i, 0))

    def body(y_ref, w_ref, h_ref, g_ref, mix_ref, hmid_ref):
        mix = jnp.dot(y_ref[...], w_ref[...], preferred_element_type=F32)
        mix_ref[...] = mix
        hmid_ref[...] = h_ref[...] + _rms_fwd(mix, g_ref[...])

    return _hosting_call(
        exchange, body, name="out_proj", grid=(n // tr,),
        in_specs=[spec, VMEM_SPEC, spec, g.spec],
        out_specs=[spec, spec],
        out_shape=[_sds((n, d), F32), _sds((n, d), F32)],
        scratch_shapes=[], compiler_params=_params(("arbitrary",), 44),
    )(ymix, w_out, h, g.array)


def _out_proj_bwd(dh_mid, mix, g, w_out, exchange=None):
    n, d = mix.shape
    tr = _row_tile(n)
    spec = pl.BlockSpec((tr, d), lambda i: (i, 0))

    def body(dh_ref, mix_ref, g_ref, w_ref, dmix_ref, dy_ref, dg_ref):
        i = pl.program_id(0)
        dmix, dg = _rms_bwd(mix_ref[...], g_ref[...], dh_ref[...])
        dmix = dmix.astype(BF16)
        dmix_ref[...] = dmix
        dy_ref[...] = _dot_nt(dmix, w_ref[...])
        _accumulate(dg_ref, dg, i == 0)

    return _hosting_call(
        exchange, body, name="out_proj_bwd", grid=(n // tr,),
        in_specs=[spec, spec, g.spec, VMEM_SPEC],
        out_specs=[spec, spec, pl.BlockSpec((1, d), lambda i: (0, 0))],
        out_shape=[_sds((n, d), BF16), _sds((n, d), F32), _sds((1, d), F32)],
        scratch_shapes=[], compiler_params=_params(("arbitrary",), 44),
    )(dh_mid, mix, g.array, w_out)


FF_SLAB = D_FF // N_DEV


def _relu_squared(up):
    return jnp.square(jnp.maximum(up.astype(F32), 0.0)).astype(BF16)


def _mlp_fwd(h_mid, g_pre, w_up, w_down, g_post, exchange=None):
    n, d = h_mid.shape
    tr = _row_tile(n, 384)
    spec = pl.BlockSpec((tr, d), lambda i: (i, 0))

    def body(h_ref, gpre_ref, wup_ref, wdn_ref, gpost_ref, hn_ref, up_ref, ff_ref, hout_ref):
        h = h_ref[...]
        hn = _rms_fwd(h, gpre_ref[...]).astype(BF16)
        hn_ref[...] = hn
        ff = jnp.zeros((tr, d), F32)
        for j in range(N_DEV):
            cs = slice(j * FF_SLAB, (j + 1) * FF_SLAB)
            up = jnp.dot(hn, wup_ref[j], preferred_element_type=F32).astype(BF16)
            up_ref[:, cs] = up
            ff = ff + jnp.dot(_relu_squared(up), wdn_ref[cs, :], preferred_element_type=F32)
        ff_ref[...] = ff
        hout_ref[...] = h + _rms_fwd(ff, gpost_ref[...])

    return _hosting_call(
        exchange, body, name="mlp_fwd", grid=(n // tr,),
        in_specs=[spec, g_pre.spec, VMEM_SPEC, VMEM_SPEC, g_post.spec],
        out_specs=[spec, pl.BlockSpec((tr, D_FF), lambda i: (i, 0)), spec, spec],
        out_shape=[_sds((n, d), BF16), _sds((n, D_FF), BF16), _sds((n, d), F32), _sds((n, d), F32)],
        scratch_shapes=[], compiler_params=_params(("arbitrary",), 52),
    )(h_mid, g_pre.array, w_up, w_down, g_post.array)


def _mlp_bwd(dh, ff, up, h_mid, g_pre, w_up, w_down, g_post, exchange=None):
    n, d = h_mid.shape
    tr = _row_tile(n, 384)
    spec = pl.BlockSpec((tr, d), lambda i: (i, 0))
    wide = pl.BlockSpec((tr, D_FF), lambda i: (i, 0))
    gspec = pl.BlockSpec((1, d), lambda i: (0, 0))

    def body(dh_ref, ff_ref, up_ref, h_ref, gpre_ref, wup_ref, wdn_ref, gpost_ref,
             dff_ref, dup_ref, dhmid_ref, dgpost_ref, dgpre_ref):
        i = pl.program_id(0)
        dh = dh_ref[...]
        dff, dgpost = _rms_bwd(ff_ref[...], gpost_ref[...], dh)
        dff = dff.astype(BF16)
        dff_ref[...] = dff
        dhn = jnp.zeros((tr, d), F32)
        for j in range(N_DEV):
            cs = slice(j * FF_SLAB, (j + 1) * FF_SLAB)
            relu = jnp.maximum(up_ref[:, cs].astype(F32), 0.0)
            dact = _dot_nt(dff, wdn_ref[cs, :])
            dup = (dact * 2.0 * relu).astype(BF16)
            dup_ref[:, cs] = dup
            dhn = dhn + _dot_nt(dup, wup_ref[j])
        dx, dgpre = _rms_bwd(h_ref[...], gpre_ref[...], dhn)
        dhmid_ref[...] = dh + dx
        _accumulate(dgpost_ref, dgpost, i == 0)
        _accumulate(dgpre_ref, dgpre, i == 0)

    return _hosting_call(
        exchange, body, name="mlp_bwd", grid=(n // tr,),
        in_specs=[spec, spec, wide, spec, g_pre.spec, VMEM_SPEC, VMEM_SPEC, g_post.spec],
        out_specs=[spec, wide, spec, gspec, gspec],
        out_shape=[_sds((n, d), BF16), _sds((n, D_FF), BF16), _sds((n, d), F32), _sds((1, d), F32), _sds((1, d), F32)],
        scratch_shapes=[], compiler_params=_params(("arbitrary",), 56),
    )(dh, ff, up, h_mid, g_pre.array, w_up, w_down, g_post.array)


def _in_proj_bwd(pieces, w_in, h, g, dh_mid, exchange=None):
    dxbr, dgate, dqk_f, dqk_b, dv_f, dv_b, dgout, dzg_f, dzg_b = pieces
    n, d = h.shape
    tr = _row_tile(n, 384)
    spec = pl.BlockSpec((tr, d), lambda i: (i, 0))
    s512 = pl.BlockSpec((tr, 512), lambda i: (i, 0))
    s128 = pl.BlockSpec((tr, LANES), lambda i: (i, 0))

    def body(a_ref, b_ref, cf_ref, cb_ref, df_ref, db_ref, e_ref, ff_ref, fb_ref, w_ref, h_ref, g_ref, dhm_ref,
             dz_ref, dh_ref, dg_ref):
        i = pl.program_id(0)
        real = (_row_ids(tr, i) >= PAD_ROWS).astype(F32)
        f32 = lambda ref: ref[...].astype(F32)
        dz = jnp.concatenate([f32(a_ref), f32(b_ref), f32(cf_ref) + f32(cb_ref), f32(df_ref) + f32(db_ref),
                              f32(e_ref), f32(ff_ref) + f32(fb_ref)], axis=1) * real
        dz = dz.astype(BF16)
        dz_ref[...] = dz
        dhn = _dot_nt(dz, w_ref[...])
        dx, dg = _rms_bwd(h_ref[...], g_ref[...], dhn)
        dh_ref[...] = (dhm_ref[...] + dx) * real
        _accumulate(dg_ref, dg, i == 0)

    return _hosting_call(
        exchange, body, name="in_proj_bwd", grid=(n // tr,),
        in_specs=[s512, s512, s512, s512, s512, s512, s512, s128, s128, VMEM_SPEC, spec, g.spec, spec],
        out_specs=[pl.BlockSpec((tr, Z_W), lambda i: (i, 0)), spec, pl.BlockSpec((1, d), lambda i: (0, 0))],
        out_shape=[_sds((n, Z_W), BF16), _sds((n, d), F32), _sds((1, d), F32)],
        scratch_shapes=[], compiler_params=_params(("arbitrary",), 48),
    )(dxbr, dgate, dqk_f, dqk_b, dv_f, dv_b, dgout, dzg_f, dzg_b, w_in, h, g.array, dh_mid)


def _matmul_tn(a, b, name, column_slabs=False, exchange=None, a_map=None):
    n, m = a.shape
    k = b.shape[1]
    tr = next(t for t in (2816, 1408, 768, 512, 256) if n % t == 0)
    tm, tk = _col_tile(m), _col_tile(k)
    steps = n // tr
    slab = k // N_DEV
    per_step = tk // slab if column_slabs else 1
    sub = next(t for t in (704, 768, 512, 256) if tr % t == 0)

    def body(a_ref, b_ref, o_ref, acc_ref, *mapped_ref):
        r = pl.program_id(2)
        if a_map is None:
            a_blk = a_ref[...]
        else:
            for c in range(tr // sub):
                rows = pl.ds(c * sub, sub)
                mapped_ref[0][rows, :] = a_map(a_ref[rows, :])
            a_blk = mapped_ref[0][...]
        _accumulate(acc_ref, _dot_tn(a_blk, b_ref[...]), r == 0)

        @pl.when(r == steps - 1)
        def _():
            if column_slabs:
                for j in range(per_step):
                    o_ref[j] = acc_ref[:, j * slab:(j + 1) * slab].astype(BF16)
            else:
                o_ref[...] = acc_ref[...].astype(BF16)

    if column_slabs:
        out_spec = pl.BlockSpec((per_step, tm, slab), lambda mi, ki, r: (ki, mi, 0))
        out_shape = _sds((N_DEV, m, slab), BF16)
    else:
        out_spec = pl.BlockSpec((tm, tk), lambda mi, ki, r: (mi, ki))
        out_shape = _sds((m, k), BF16)
    outs = _hosting_call(
        exchange, body, name=name, grid=(m // tm, k // tk, steps),
        in_specs=[pl.BlockSpec((tr, tm), lambda mi, ki, r: (r, mi)), pl.BlockSpec((tr, tk), lambda mi, ki, r: (r, ki))],
        out_specs=[out_spec], out_shape=[out_shape],
        scratch_shapes=[pltpu.VMEM((tm, tk), F32)] + ([] if a_map is None else [pltpu.VMEM((tr, tm), BF16)]),
        compiler_params=_params(("arbitrary", "arbitrary", "arbitrary"), 52),
    )(a, b)
    return outs[0] if exchange is None else outs


def _loss_and_grad(h_out, target):
    n, d = h_out.shape
    tr = ROW_BLOCK
    first = (PAD_ROWS + N_META) // tr

    def body(h_ref, t_ref, dh_ref, loss_ref):
        i = pl.program_id(0)
        real = jnp.where(i >= first, 1.0, 0.0)
        diff = (h_ref[...] - t_ref[...]) * real
        dh_ref[...] = diff * (1.0 / d)
        part = 0.5 * jnp.sum(jnp.mean(diff * diff, axis=-1, keepdims=True), axis=0, keepdims=True)
        _accumulate(loss_ref, jnp.broadcast_to(part, (1, LANES)), i == 0)

    return pl.pallas_call(
        body, name="loss_and_grad", grid=(n // tr,),
        in_specs=[pl.BlockSpec((tr, d), lambda i: (i, 0)), pl.BlockSpec((tr, d), lambda i: (jnp.maximum(i - first, 0), 0))],
        out_specs=[pl.BlockSpec((tr, d), lambda i: (i, 0)), pl.BlockSpec((1, LANES), lambda i: (0, 0))],
        out_shape=[_sds((n, d), F32), _sds((1, LANES), F32)],
        compiler_params=_params(("arbitrary",)),
    )(h_out, target)


def _scan_block(a, u, h_in, reverse):
    n = a.shape[0]
    row = lax.broadcasted_iota(jnp.int32, (n, 1), 0)
    d = 1
    while d < n:
        shift = n - d if reverse else d
        keep = (row < n - d) if reverse else (row >= d)
        a_s = pltpu.roll(a, shift, 0)
        u_s = pltpu.roll(u, shift, 0)
        u = jnp.where(keep, a * u_s + u, u)
        a = jnp.where(keep, a * a_s, a)
        d *= 2
    return a * h_in + u


def _lru_gates(xc, wcat_ref, bias_ref, lam_ref):
    nl = -lam_ref[...]
    nsp = -LRU_C * (jnp.maximum(nl, 0.0) + jnp.log(1.0 + jnp.exp(-jnp.abs(nl))))
    pre = _dot(xc, wcat_ref[...]) + bias_ref[...]
    r = _sigmoid(pre[:, :LRU_W])
    ig = _sigmoid(pre[:, LRU_W:])
    log_a = r * nsp
    a = jnp.exp(log_a)
    m2 = _one_minus_square(a, log_a)
    inv_m = lax.rsqrt(jnp.maximum(m2, 1e-30))
    return r, ig, a, m2 * inv_m, inv_m, nsp


def _lru_scan(xc, wcat, bias, lam, reverse, exchange=None):
    n = xc.shape[0]
    nb = n // ROW_BLOCK
    order = (lambda i: nb - 1 - i) if reverse else (lambda i: i)
    spec = pl.BlockSpec((ROW_BLOCK, LRU_W), lambda i: (order(i), 0))
    edge = 0 if reverse else ROW_BLOCK - 1

    def body(xc_ref, wcat_ref, bias_ref, lam_ref, h_ref, carry_ref):
        i = pl.program_id(0)

        @pl.when(i == 0)
        def _():
            carry_ref[...] = jnp.zeros_like(carry_ref)

        xc = xc_ref[...]
        r, ig, a, m, _, _ = _lru_gates(xc, wcat_ref, bias_ref, lam_ref)
        u = jnp.where(_row_ids(ROW_BLOCK, order(i)) >= PAD_ROWS, m * (ig * xc), 0.0)
        h_ref[...] = _scan_block(a, u, carry_ref[0:1, :], reverse)
        carry_ref[0:1, :] = h_ref[pl.ds(edge, 1), :]

    return _hosting_call(
        exchange, body, name="lru_scan_b" if reverse else "lru_scan_f", grid=(nb,),
        in_specs=[spec, wcat.spec, bias.spec, lam.spec],
        out_specs=[spec],
        out_shape=[_sds((n, LRU_W), F32)],
        scratch_shapes=[pltpu.VMEM((8, LRU_W), F32)],
        compiler_params=_params(("arbitrary",)),
    )(xc, wcat.array, bias.array, lam.array)


def _lru_scan_bwd(dhs, xc, h, wcat, bias, lam, reverse, exchange=None):
    n = xc.shape[0]
    nb = n // ROW_BLOCK
    per = ROW_BLOCK // 8
    order = (lambda i: i) if reverse else (lambda i: nb - 1 - i)
    spec = pl.BlockSpec((ROW_BLOCK, LRU_W), lambda i: (order(i), 0))
    if reverse:
        halo = pl.BlockSpec((8, LRU_W), lambda i: (jnp.minimum((order(i) + 1) * per, nb * per - 1), 0))
    else:
        halo = pl.BlockSpec((8, LRU_W), lambda i: (jnp.maximum(order(i) * per - 1, 0), 0))
    edge = ROW_BLOCK - 1 if reverse else 0

    def body(dhs_ref, xc_ref, h_ref, halo_ref, wcat_ref, bias_ref, lam_ref,
             dxc_ref, dw_ref, db_ref, dlam_ref, cdh_ref, ca_ref, tmp_ref):
        i = pl.program_id(0)
        ib = order(i)

        @pl.when(i == 0)
        def _():
            cdh_ref[...] = jnp.zeros_like(cdh_ref)
            ca_ref[...] = jnp.zeros_like(ca_ref)

        xc = xc_ref[...]
        r, ig, a, m, inv_m, nsp = _lru_gates(xc, wcat_ref, bias_ref, lam_ref)
        row = lax.broadcasted_iota(jnp.int32, (ROW_BLOCK, 1), 0)
        if reverse:
            coef = jnp.where(row == 0, ca_ref[0:1, :], pltpu.roll(a, 1, 0))
            h_nb = jnp.where(row == ROW_BLOCK - 1, halo_ref[0:1, :] * jnp.where(ib < nb - 1, 1.0, 0.0),
                             pltpu.roll(h_ref[...], ROW_BLOCK - 1, 0))
        else:
            coef = jnp.where(row == ROW_BLOCK - 1, ca_ref[0:1, :], pltpu.roll(a, ROW_BLOCK - 1, 0))
            h_nb = jnp.where(row == 0, halo_ref[7:8, :] * jnp.where(ib > 0, 1.0, 0.0), pltpu.roll(h_ref[...], 1, 0))
        dh = _scan_block(coef, dhs_ref[...], cdh_ref[0:1, :], not reverse)
        tmp_ref[...] = dh
        cdh_ref[0:1, :] = tmp_ref[pl.ds(edge, 1), :]
        tmp_ref[...] = a
        ca_ref[0:1, :] = tmp_ref[pl.ds(edge, 1), :]

        du = jnp.where(_row_ids(ROW_BLOCK, ib) >= PAD_ROWS, dh, 0.0)
        da = dh * h_nb
        dm = du * (ig * xc)
        di = du * (m * xc)
        dlog_a = da * a - dm * (a * a) * inv_m
        dr = dlog_a * nsp
        dpre = jnp.concatenate([dr * r * (1.0 - r), di * ig * (1.0 - ig)], axis=1)
        dxc_ref[...] = du * (m * ig) + _dot_nt(dpre, wcat_ref[...])
        _accumulate(dw_ref, _dot_tn(xc, dpre), i == 0)
        _accumulate(db_ref, jnp.sum(dpre, axis=0, keepdims=True), i == 0)
        _accumulate(dlam_ref, jnp.sum(dlog_a * r, axis=0, keepdims=True), i == 0)

        @pl.when(i == nb - 1)
        def _():
            dlam_ref[...] = dlam_ref[...] * (LRU_C * _sigmoid(-lam_ref[...]))

    return _hosting_call(
        exchange, body, name="lru_scan_bwd_b" if reverse else "lru_scan_bwd_f", grid=(nb,),
        in_specs=[spec, spec, spec, halo, wcat.spec, bias.spec, lam.spec],
        out_specs=[spec, pl.BlockSpec((LRU_W, 2 * LRU_W), lambda i: (0, 0)),
                   pl.BlockSpec((1, 2 * LRU_W), lambda i: (0, 0)), pl.BlockSpec((1, LRU_W), lambda i: (0, 0))],
        out_shape=[_sds((n, LRU_W), F32), _sds((LRU_W, 2 * LRU_W), F32), _sds((1, 2 * LRU_W), F32), _sds((1, LRU_W), F32)],
        scratch_shapes=[pltpu.VMEM((8, LRU_W), F32), pltpu.VMEM((8, LRU_W), F32), pltpu.VMEM((ROW_BLOCK, LRU_W), F32)],
        compiler_params=_params(("arbitrary",)),
    )(dhs, xc, h, h, wcat.array, bias.array, lam.array)


def _gla_rows(n):
    return 768 if n % 768 == 0 else ROW_BLOCK


def _gla_masks(reverse):
    t = lax.broadcasted_iota(jnp.int32, (CHUNK, CHUNK), 0)
    s = lax.broadcasted_iota(jnp.int32, (CHUNK, CHUNK), 1)
    if reverse:
        return (s >= t).astype(F32), s > t
    return (s <= t).astype(F32), s <= t


def _gla_gate(zg, wg_ref, bg_ref):
    pre = _dot(zg, wg_ref[...]) + bg_ref[...]
    g = (jnp.minimum(pre, 0.0) - jnp.log(1.0 + jnp.exp(-jnp.abs(pre)))) * (1.0 / GATE_NORM)
    return pre, g


def _gla_decays(gc, tri):
    b = jnp.dot(tri, gc, precision=lax.Precision.HIGHEST, preferred_element_type=F32)
    b_last = jnp.sum(gc, axis=0, keepdims=True)
    return jnp.exp(b), jnp.exp(-b), jnp.exp(b_last - b), jnp.exp(b_last)


def _gla_scan(z, wg, bg, reverse, exchange=None):
    n = z.shape[0]
    rb = _gla_rows(n)
    nb = n // rb
    cpb = rb // CHUNK
    order = (lambda i: nb - 1 - i) if reverse else (lambda i: i)
    chunks = range(cpb - 1, -1, -1) if reverse else range(cpb)

    def body(qk_ref, v_ref, zg_ref, wg_ref, bg_ref, o_ref, sall_ref, s_ref):
        i = pl.program_id(0)

        @pl.when(i == 0)
        def _():
            s_ref[...] = jnp.zeros_like(s_ref)

        tri, mask = _gla_masks(reverse)
        _, g = _gla_gate(zg_ref[...], wg_ref, bg_ref)
        heads = range(GLA_HEADS)
        ks = [slice(hd * GLA_DK, (hd + 1) * GLA_DK) for hd in heads]
        vs = [slice(hd * GLA_DV, (hd + 1) * GLA_DV) for hd in heads]
        qh, kb, v, el, p, intra, kv = {}, {}, {}, {}, {}, {}, {}
        for c in chunks:
            rows = slice(c * CHUNK, (c + 1) * CHUNK)
            eb, enb, ebl, el[c] = _gla_decays(g[rows], tri)
            qk = qk_ref[rows, :]
            q_all = (qk[:, :GLA_QK] * (GLA_DK ** -0.5) * eb).astype(BF16)
            k_all = (qk[:, GLA_QK:] * enb).astype(BF16)
            kb_all = (qk[:, GLA_QK:] * ebl).astype(BF16)
            v_all = v_ref[rows, :].astype(BF16)
            for hd in heads:
                qh[c, hd], kb[c, hd], v[c, hd] = q_all[:, ks[hd]], kb_all[:, ks[hd]], v_all[:, vs[hd]]
                p[c, hd] = _dot_nt(qh[c, hd], k_all[:, ks[hd]])
        for c in chunks:
            for hd in heads:
                intra[c, hd] = _dot(jnp.where(mask, p[c, hd], 0.0), v[c, hd])
                kv[c, hd] = _dot_tn(v[c, hd], kb[c, hd])
        state = [s_ref[:, ks[hd]] for hd in heads]
        for c in chunks:
            rows = slice(c * CHUNK, (c + 1) * CHUNK)
            for hd in heads:
                sall_ref[c, :, ks[hd]] = state[hd]
                o_ref[rows, vs[hd]] = intra[c, hd] + _dot_nt(qh[c, hd], state[hd])
                state[hd] = state[hd] * el[c][:, ks[hd]] + kv[c, hd]
        for hd in heads:
            s_ref[:, ks[hd]] = state[hd]

    return _hosting_call(
        exchange, body, name="gla_scan_b" if reverse else "gla_scan_f", grid=(nb,),
        in_specs=[pl.BlockSpec((rb, 512), lambda i: (order(i), 2)), pl.BlockSpec((rb, 512), lambda i: (order(i), 3)),
                  pl.BlockSpec((rb, LANES), lambda i: (order(i), ZG_COL_BLOCK)), wg.spec, bg.spec],
        out_specs=[pl.BlockSpec((rb, GLA_W), lambda i: (order(i), 0)),
                   pl.BlockSpec((cpb, GLA_DV, GLA_QK), lambda i: (order(i), 0, 0))],
        out_shape=[_sds((n, GLA_W), F32), _sds((n // CHUNK, GLA_DV, GLA_QK), F32)],
        scratch_shapes=[pltpu.VMEM((GLA_DV, GLA_QK), F32)],
        compiler_params=_params(("arbitrary",)),
    )(z, z, z, wg.array, bg.array)


def _gla_scan_bwd(do, z, states, wg, bg, reverse, exchange=None):
    n = z.shape[0]
    rb = _gla_rows(n)
    nb = n // rb
    cpb = rb // CHUNK
    order = (lambda i: i) if reverse else (lambda i: nb - 1 - i)
    chunks = range(cpb) if reverse else range(cpb - 1, -1, -1)

    def body(do_ref, qk_ref, v_ref, zg_ref, sall_ref, wg_ref, bg_ref,
             dqk_ref, dv_ref, dzg_ref, dwg_ref, dbg_ref, ds_ref):
        i = pl.program_id(0)

        @pl.when(i == 0)
        def _():
            ds_ref[...] = jnp.zeros_like(ds_ref)

        tri, mask = _gla_masks(reverse)
        tri_t, _ = _gla_masks(not reverse)
        zg = zg_ref[...]
        pre, g = _gla_gate(zg, wg_ref, bg_ref)
        heads = range(GLA_HEADS)
        ks = [slice(hd * GLA_DK, (hd + 1) * GLA_DK) for hd in heads]
        vs = [slice(hd * GLA_DV, (hd + 1) * GLA_DV) for hd in heads]
        dec, full, qh, kh, kb, v, dout, p, dp = {}, {}, {}, {}, {}, {}, {}, {}, {}
        for c in chunks:
            rows = slice(c * CHUNK, (c + 1) * CHUNK)
            dec[c] = _gla_decays(g[rows], tri)
            eb, enb, ebl, _ = dec[c]
            qk = qk_ref[rows, :]
            q_f = qk[:, :GLA_QK] * (GLA_DK ** -0.5) * eb
            k_f = qk[:, GLA_QK:] * enb
            kb_f = qk[:, GLA_QK:] * ebl
            full[c] = (q_f, k_f, kb_f)
            q_all, k_all, kb_all = q_f.astype(BF16), k_f.astype(BF16), kb_f.astype(BF16)
            v_all, do_all = v_ref[rows, :].astype(BF16), do_ref[rows, :].astype(BF16)
            for hd in heads:
                qh[c, hd], kh[c, hd], kb[c, hd] = q_all[:, ks[hd]], k_all[:, ks[hd]], kb_all[:, ks[hd]]
                v[c, hd], dout[c, hd] = v_all[:, vs[hd]], do_all[:, vs[hd]]
                p[c, hd] = _dot_nt(qh[c, hd], kh[c, hd])
                dp[c, hd] = _dot_nt(dout[c, hd], v[c, hd])
        dv_i, dqh, dkh, dsq, state = {}, {}, {}, {}, {}
        for c in chunks:
            for hd in heads:
                pm = jnp.where(mask, p[c, hd], 0.0).astype(BF16)
                dpm = jnp.where(mask, dp[c, hd], 0.0).astype(BF16)
                state[c, hd] = sall_ref[c, :, ks[hd]]
                dv_i[c, hd] = _dot_tn(pm, dout[c, hd])
                dqh[c, hd] = _dot(dpm, kh[c, hd]) + _dot(dout[c, hd], state[c, hd])
                dkh[c, hd] = _dot_tn(dpm, qh[c, hd])
                dsq[c, hd] = _dot_tn(dout[c, hd], qh[c, hd])
        dstate = [ds_ref[:, ks[hd]] for hd in heads]
        dkb, sds = {}, {}
        for c in chunks:
            rows = slice(c * CHUNK, (c + 1) * CHUNK)
            el = dec[c][3]
            for hd in heads:
                dv_ref[rows, vs[hd]] = (dv_i[c, hd] + _dot_nt(kb[c, hd], dstate[hd])).astype(BF16)
                dkb[c, hd] = _dot(v[c, hd], dstate[hd])
                sds[c, hd] = jnp.sum(state[c, hd] * dstate[hd], axis=0, keepdims=True)
                dstate[hd] = dstate[hd] * el[:, ks[hd]] + dsq[c, hd]
        for hd in heads:
            ds_ref[:, ks[hd]] = dstate[hd]
        dgs = [None] * cpb
        for c in chunks:
            rows = slice(c * CHUNK, (c + 1) * CHUNK)
            eb, enb, ebl, el = dec[c]
            q_f, k_f, kb_f = full[c]
            dqh_c = jnp.concatenate([dqh[c, hd] for hd in heads], axis=1)
            dkh_c = jnp.concatenate([dkh[c, hd] for hd in heads], axis=1)
            dkb_c = jnp.concatenate([dkb[c, hd] for hd in heads], axis=1)
            sds_c = jnp.concatenate([sds[c, hd] for hd in heads], axis=1)
            dqk_ref[rows, :] = jnp.concatenate([dqh_c * eb * (GLA_DK ** -0.5), dkh_c * enb + dkb_c * ebl], axis=1).astype(BF16)
            dkb_kb = dkb_c * kb_f
            db = dqh_c * q_f - dkh_c * k_f - dkb_kb
            db_last = el * sds_c + jnp.sum(dkb_kb, axis=0, keepdims=True)
            dgs[c] = jnp.dot(tri_t, db, precision=lax.Precision.HIGHEST, preferred_element_type=F32) + db_last
        dg = jnp.concatenate(dgs, axis=0)
        dpre = dg * _sigmoid(-pre) * (1.0 / GATE_NORM)
        dzg_ref[...] = _dot_nt(dpre, wg_ref[...]).astype(BF16)
        _accumulate(dwg_ref, _dot_tn(zg, dpre), i == 0)
        _accumulate(dbg_ref, jnp.sum(dpre, axis=0, keepdims=True), i == 0)

    return _hosting_call(
        exchange, body, name="gla_scan_bwd_b" if reverse else "gla_scan_bwd_f", grid=(nb,),
        in_specs=[pl.BlockSpec((rb, GLA_W), lambda i: (order(i), 0)),
                  pl.BlockSpec((rb, 512), lambda i: (order(i), 2)), pl.BlockSpec((rb, 512), lambda i: (order(i), 3)),
                  pl.BlockSpec((rb, LANES), lambda i: (order(i), ZG_COL_BLOCK)),
                  pl.BlockSpec((cpb, GLA_DV, GLA_QK), lambda i: (order(i), 0, 0)), wg.spec, bg.spec],
        out_specs=[pl.BlockSpec((rb, 512), lambda i: (order(i), 0)), pl.BlockSpec((rb, 512), lambda i: (order(i), 0)),
                   pl.BlockSpec((rb, LANES), lambda i: (order(i), 0)),
                   pl.BlockSpec((LANES, GLA_QK), lambda i: (0, 0)), pl.BlockSpec((1, GLA_QK), lambda i: (0, 0))],
        out_shape=[_sds((n, 512), BF16), _sds((n, 512), BF16), _sds((n, LANES), BF16), _sds((LANES, GLA_QK), F32),
                   _sds((1, GLA_QK), F32)],
        scratch_shapes=[pltpu.VMEM((GLA_DV, GLA_QK), F32)],
        compiler_params=_params(("arbitrary",)),
    )(do, z, z, z, states, wg.array, bg.array)


NORM_NAMES = ("norm_mix_pre", "norm_mix_post", "norm_mlp_pre", "norm_mlp_post")
VEC512_NAMES = ("conv_b", "lru_ba_f", "lru_bx_f", "lru_lambda_f", "lru_ba_b", "lru_bx_b", "lru_lambda_b", "gla_head_norm")
VEC256_NAMES = ("gla_bg_f", "gla_bg_b")
LRU_MAT_NAMES = ("lru_wa_f", "lru_wx_f", "lru_wa_b", "lru_wx_b")
DIRS = ("f", "b")


def _prepare_params(w, gathered, depth):
    row_names = NORM_NAMES + ("conv_b", "gla_head_norm")
    ins = ([w[nm] for nm in row_names] + [w["lru_ba_" + d] for d in DIRS] + [w["lru_bx_" + d] for d in DIRS]
           + [w["lru_lambda_" + d] for d in DIRS] + [w["gla_bg_" + d] for d in DIRS]
           + [w["lru_wa_" + d].reshape(depth, LRU_W, LRU_HD) for d in DIRS]
           + [w["lru_wx_" + d].reshape(depth, LRU_W, LRU_HD) for d in DIRS]
           + [gathered["conv_w"], gathered["gla_wg_f"], gathered["gla_wg_b"], gathered["meta_tokens"]])
    n_rows = len(row_names)

    def body(*refs):
        rows_in = refs[:n_rows]
        ba, bx, lam, bg, wa, wx = (refs[n_rows + 2 * t:n_rows + 2 * t + 2] for t in range(6))
        convw_g, wgf_g, wgb_g, meta_g = refs[n_rows + 12:n_rows + 16]
        outs = refs[n_rows + 16:]
        rows_out = outs[:n_rows]
        convw, wcat, bias, lam_o, wg, bg_o, meta = outs[n_rows:]
        for l in range(depth):
            for src, dst in zip(rows_in, rows_out):
                dst[l] = src[pl.ds(l, 1), :]
            convw[l] = jnp.zeros((8, LRU_W), F32)
            for j in range(N_DEV):
                convw[l, 0:4, j * 64:(j + 1) * 64] = convw_g[j, l]
            for d in range(2):
                wcat[l, d] = jnp.zeros((LRU_W, 2 * LRU_W), BF16)
                for hd in range(LRU_HEADS):
                    rs = slice(hd * LRU_HD, (hd + 1) * LRU_HD)
                    wcat[l, d, rs, hd * LRU_HD:(hd + 1) * LRU_HD] = wa[d][l, rs, :].astype(BF16)
                    wcat[l, d, rs, LRU_W + hd * LRU_HD:LRU_W + (hd + 1) * LRU_HD] = wx[d][l, rs, :].astype(BF16)
                bias[l, d, :, 0:LRU_W] = ba[d][pl.ds(l, 1), :]
                bias[l, d, :, LRU_W:2 * LRU_W] = bx[d][pl.ds(l, 1), :]
                lam_o[l, d] = lam[d][pl.ds(l, 1), :]
                bg_o[l, d] = bg[d][pl.ds(l, 1), :]
                wg[l, d] = jnp.zeros((LANES, GLA_QK), BF16)
                src = wgf_g if d == 0 else wgb_g
                for j in range(N_DEV):
                    wg[l, d, d * GLA_RANK:(d + 1) * GLA_RANK, j * 32:(j + 1) * 32] = src[j, l].astype(BF16)
        for j in range(N_DEV):
            meta[:, j * LANES:(j + 1) * LANES] = meta_g[j]

    out_shape = ([_sds((depth, 1, w[nm].shape[1]), F32) for nm in row_names]
                 + [_sds((depth, 8, LRU_W), F32), _sds((depth, 2, LRU_W, 2 * LRU_W), BF16), _sds((depth, 2, 1, 2 * LRU_W), F32),
                    _sds((depth, 2, 1, LRU_W), F32), _sds((depth, 2, LANES, GLA_QK), BF16), _sds((depth, 2, 1, GLA_QK), F32),
                    _sds((N_META, D_MODEL), F32)])
    outs = pl.pallas_call(
        body, name="prepare_params", in_specs=[VMEM_SPEC] * len(ins), out_specs=[VMEM_SPEC] * len(out_shape),
        out_shape=out_shape, compiler_params=_params(None, 32),
    )(*ins)
    prepared = dict(zip(row_names, outs[:n_rows]))
    prepared.update(zip(("conv_w", "wcat", "lru_bias", "lru_lam", "wg", "gla_bg", "meta_tokens"), outs[n_rows:]))
    return prepared


class _Outbox:
    def __init__(self, on_complete):
        self.pending, self.on_complete = {}, on_complete

    def put(self, key, array, src, landing_shape):
        self.pending[key] = dict(array=array, src=src, landing=landing_shape, groups=list(range(len(PEER_GROUPS))))

    def exchange(self, wanted=None):
        ex, tickets = _Exchange(), []
        for key, item in self.pending.items():
            groups = [g for g in item["groups"] if wanted is None or (key[0], g) in wanted]
            out = None
            for g in groups:
                landing = item["landing"] if out is None else out
                out = ex.add(item["array"], item["src"], landing, _slab, peers=PEER_GROUPS[g], local=(g == 0))
                item["groups"].remove(g)
            if groups:
                tickets.append((key, out))
        return ex, tickets

    def store(self, tickets, landed):
        for key, out in tickets:
            item = self.pending[key]
            item["landing"] = landed[out]
            if not item["groups"]:
                del self.pending[key]
                self.on_complete(key, landed[out])


def _install_weight(p):
    def install(key, g):
        nm, l = key
        if nm == "w_in":
            g = jnp.pad(jnp.concatenate([g[j] for j in range(N_DEV)], axis=1), ((0, 0), (0, Z_W - D_IN)))
        elif nm == "w_out":
            g = g.reshape(D_MODEL, D_MODEL)
        elif nm == "w_mlp_down":
            g = g.reshape(D_FF, D_MODEL)
        p.setdefault(nm, {})[l] = g
    return install


def _request_weight(gather, shards, nm, l):
    gather.put((nm, l), shards[nm], _layer_of(l), _sds((N_DEV,) + shards[nm].shape[1:], BF16))


def _layer_fwd(h, l, p, gather, shards, depth):
    lp = lambda name, *index: _LayerParam(p[name], l, *index)
    s = dict(h=h)

    def hosted(fn, wanted, *args):
        ex, tickets = gather.exchange(wanted)
        outs = fn(*args, ex)
        own = len(outs) - len(ex.landings)
        gather.store(tickets, outs[own:])
        return outs[:own]

    _request_weight(gather, shards, "w_mlp_up", l)
    _request_weight(gather, shards, "w_mlp_down", l)
    s["hn"], s["z"] = hosted(_norm_in_proj, [("w_mlp_up", 0), ("w_out", 0)], h, lp("norm_mix_pre"), p["w_in"][l])
    s["xc"] = _conv_fwd(s["z"], lp("conv_w"), lp("conv_b"))
    plan = {"f": ([("w_mlp_up", 1), ("w_out", 1)], [("w_mlp_up", 2), ("w_out", 2)]),
            "b": ([("w_mlp_down", 0)], [("w_mlp_down", 1)])}
    for d, name in enumerate(DIRS):
        s["h_" + name], = hosted(_lru_scan, plan[name][0], s["xc"], lp("wcat", d), lp("lru_bias", d), lp("lru_lam", d), d == 1)
        s["o_" + name], s["s_" + name] = hosted(_gla_scan, plan[name][1], s["z"], lp("wg", d), lp("gla_bg", d), d == 1)
    s["ymix"] = _mix_epilogue(s["h_f"], s["h_b"], s["o_f"], s["o_b"], s["z"], lp("gla_head_norm"))
    s["mix"], s["h_mid"] = hosted(_out_proj, [("w_mlp_down", 2)], s["ymix"], p["w_out"][l], h, lp("norm_mix_post"))
    if l + 1 < depth:
        _request_weight(gather, shards, "w_in", l + 1)
        _request_weight(gather, shards, "w_out", l + 1)
    s["hn2"], s["up"], s["ff"], h_out = hosted(
        _mlp_fwd, None, s["h_mid"], lp("norm_mlp_pre"), p["w_mlp_up"][l], p["w_mlp_down"][l], lp("norm_mlp_post"))
    return h_out, s


def _layer_bwd(dh_out, l, p, s, outbox):
    lp = lambda name, *index: _LayerParam(p[name], l, *index)
    g = {}

    def hosted(fn, wanted, *args):
        ex, tickets = outbox.exchange(wanted)
        outs = fn(*args, ex)
        own = len(outs) - len(ex.landings)
        outbox.store(tickets, outs[own:])
        return outs[:own]

    d_ff, dup, dh_mid, g["norm_mlp_post"], g["norm_mlp_pre"] = hosted(
        _mlp_bwd, None, dh_out, s["ff"], s["up"], s["h_mid"], lp("norm_mlp_pre"), p["w_mlp_up"][l], p["w_mlp_down"][l],
        lp("norm_mlp_post"))
    _send_grad(outbox, "w_mlp_down", l, _matmul_tn(s["up"], d_ff, "grad_w_down", a_map=_relu_squared)
               .reshape(N_DEV, D_FF // N_DEV, D_MODEL))
    _send_grad(outbox, "w_mlp_up", l, _matmul_tn(s["hn2"], dup, "grad_w_up", column_slabs=True))
    dmix, dymix, g["norm_mix_post"] = hosted(_out_proj_bwd, [("w_mlp_down", 0)], dh_mid, s["mix"], lp("norm_mix_post"),
                                             p["w_out"][l])
    grad_w_out = _matmul_tn(s["ymix"], dmix, "grad_w_out").reshape(N_DEV, D_MODEL // N_DEV, D_MODEL)
    dhs, dgate, do, dgout, g["gla_head_norm"] = _mix_epilogue_bwd(
        dymix, s["h_f"], s["h_b"], s["o_f"], s["o_b"], s["z"], lp("gla_head_norm"))
    plan = {"f": ([("w_mlp_down", 1)], [("w_mlp_down", 2), ("w_mlp_up", 0)]), "b": ([("w_mlp_up", 1)], [("w_mlp_up", 2)])}
    dqk, dv, dzg, dxc = {}, {}, {}, {}
    for d, name in enumerate(DIRS):
        dqk[name], dv[name], dzg[name], g["wg_" + name], g["gla_bg_" + name] = hosted(
            _gla_scan_bwd, plan[name][0], do, s["z"], s["s_" + name], lp("wg", d), lp("gla_bg", d), d == 1)
        dxc[name], g["wcat_" + name], g["lru_bias_" + name], g["lru_lambda_" + name] = hosted(
            _lru_scan_bwd, plan[name][1], dhs, s["xc"], s["h_" + name], lp("wcat", d), lp("lru_bias", d), lp("lru_lam", d),
            d == 1)
    _send_grad(outbox, "w_out", l, grad_w_out)
    dxbr, g["conv_w"], g["conv_b"] = _conv_bwd(dxc["f"], dxc["b"], s["z"], lp("conv_w"))
    dz, dh_in, g["norm_mix_pre"] = hosted(
        _in_proj_bwd, [("w_out", group) for group in range(len(PEER_GROUPS))],
        (dxbr, dgate, dqk["f"], dqk["b"], dv["f"], dv["b"], dgout, dzg["f"], dzg["b"]),
        p["w_in"][l], s["h"], lp("norm_mix_pre"), dh_mid)
    return dh_in, g, dz


def _send_grad(outbox, nm, l, slabs):
    outbox.put((nm, l), slabs, _slab, _sds(slabs.shape, slabs.dtype))


def _w_in_slabs(grad_w_in):
    shard = D_IN // N_DEV
    return jnp.stack([grad_w_in[:, j * shard:(j + 1) * shard] for j in range(N_DEV)])


def _folded_block(hd):
    return slice((hd // 2) * LRU_HD, (hd // 2 + 1) * LRU_HD), slice((hd % 2) * LRU_HD, (hd % 2 + 1) * LRU_HD)


def _pack_small_grads(grads, dh0, depth):
    per_layer = ("norm_mix_pre", "norm_mix_post", "norm_mlp_pre", "norm_mlp_post", "conv_b", "gla_head_norm",
                 "lru_bias_f", "lru_bias_b", "lru_lambda_f", "lru_lambda_b", "gla_bg_f", "gla_bg_b",
                 "wcat_f", "wcat_b", "conv_w", "wg_f", "wg_b")
    ins = [grads[l][nm] for l in range(depth) for nm in per_layer]
    k = len(per_layer)
    meta_rows = PAD_ROWS // N_META

    def body(*refs):
        g = [dict(zip(per_layer, refs[l * k:(l + 1) * k])) for l in range(depth)]
        dh0_ref = refs[depth * k]
        norms, v512, v256, mats, convw, wgf, wgb, meta = refs[depth * k + 1:]
        v256[...] = jnp.zeros_like(v256)
        for l in range(depth):
            for p_, nm in enumerate(NORM_NAMES):
                norms[pl.ds(2 * p_ + l, 1), :] = g[l][nm][...]
            rows512 = [g[l]["conv_b"][...], g[l]["lru_bias_f"][:, 0:LRU_W], g[l]["lru_bias_f"][:, LRU_W:2 * LRU_W],
                       g[l]["lru_lambda_f"][...], g[l]["lru_bias_b"][:, 0:LRU_W], g[l]["lru_bias_b"][:, LRU_W:2 * LRU_W],
                       g[l]["lru_lambda_b"][...], g[l]["gla_head_norm"][...]]
            for p_, row in enumerate(rows512):
                v512[pl.ds(2 * p_ + l, 1), :] = row
            for p_, nm in enumerate(("gla_bg_f", "gla_bg_b")):
                v256[pl.ds(2 * p_ + l, 1), :] = g[l][nm][...]
            for d, name in enumerate(DIRS):
                for hd in range(LRU_HEADS):
                    rs = slice(hd * LRU_HD, (hd + 1) * LRU_HD)
                    dst_rows, dst_cols = _folded_block(hd)
                    mats[2 * d, l, dst_rows, dst_cols] = g[l]["wcat_" + name][rs, hd * LRU_HD:(hd + 1) * LRU_HD].astype(BF16)
                    mats[2 * d + 1, l, dst_rows, dst_cols] = (
                        g[l]["wcat_" + name][rs, LRU_W + hd * LRU_HD:LRU_W + (hd + 1) * LRU_HD].astype(BF16))
            for j in range(N_DEV):
                convw[j, l] = g[l]["conv_w"][0:4, j * 64:(j + 1) * 64]
                wgf[j, l] = g[l]["wg_f"][0:GLA_RANK, j * 32:(j + 1) * 32]
                wgb[j, l] = g[l]["wg_b"][GLA_RANK:2 * GLA_RANK, j * 32:(j + 1) * 32]
        for j in range(N_DEV):
            meta[j] = dh0_ref[:, j * LANES:(j + 1) * LANES]

    out_shape = [_sds((8, D_MODEL), F32), _sds((16, LRU_W), F32), _sds((8, GLA_QK), F32),
                 _sds((4, depth, LRU_W // 2, 2 * LRU_HD), BF16),
                 _sds((N_DEV, depth, 4, 64), F32), _sds((N_DEV, depth, GLA_RANK, 32), F32), _sds((N_DEV, depth, GLA_RANK, 32), F32),
                 _sds((N_DEV, N_META, LANES), F32)]
    return pl.pallas_call(
        body, name="pack_small_grads", grid=(1,),
        in_specs=[VMEM_SPEC] * (depth * k) + [pl.BlockSpec((N_META, D_MODEL), lambda i: (meta_rows, 0))],
        out_specs=[VMEM_SPEC] * len(out_shape), out_shape=out_shape, compiler_params=_params(("arbitrary",), 32),
    )(*ins, dh0)


def _my_index():
    return 4 * lax.axis_index("x") + 2 * lax.axis_index("y") + lax.axis_index("c")


def _peer(k):
    x, y, c = lax.axis_index("x"), lax.axis_index("y"), lax.axis_index("c")
    px = x ^ ((k >> 2) & 1)
    py = y ^ ((k >> 1) & 1)
    pc = c ^ (k & 1)
    return (px, py, pc), 4 * px + 2 * py + pc


ALL_PEERS = tuple(range(1, N_DEV))
PEER_GROUPS = ((1, 2, 3), (4, 5), (6, 7))


class _Exchange:
    def __init__(self):
        self.inputs, self.landings, self.transfers = [], [], []

    def add(self, array, src, landing, dst, peers=ALL_PEERS, local=True):
        if isinstance(landing, int):
            out = landing
        else:
            out = len(self.landings)
            self.landings.append(landing)
        self.transfers.append((len(self.inputs), src, out, dst, tuple(peers), local))
        self.inputs.append(array)
        return out

    def _pairs(self):
        return [(t, k) for t, tr in enumerate(self.transfers) for k in tr[4]]

    def _locals(self):
        return [t for t, tr in enumerate(self.transfers) if tr[5]]

    def out_shapes(self):
        return [g if isinstance(g, jax.ShapeDtypeStruct) else _sds(g.shape, g.dtype) for g in self.landings]

    def continued(self):
        return [(b, g) for b, g in enumerate(self.landings) if not isinstance(g, jax.ShapeDtypeStruct)]

    def sem_shapes(self):
        return [pltpu.SemaphoreType.DMA((max(len(self._pairs()), 1),)), pltpu.SemaphoreType.DMA((max(len(self._pairs()), 1),)),
                pltpu.SemaphoreType.DMA((max(len(self._locals()), 1),))]

    def _local(self, ins, outs, sems):
        me = _my_index()
        copies = []
        for s, t in enumerate(self._locals()):
            a, src, b, dst, _, _ = self.transfers[t]
            copies.append(pltpu.make_async_copy(src(ins[a], me), dst(outs[b], me), sems[2].at[s]))
        return copies

    def _remote(self, ins, outs, sems, sending):
        copies = []
        for s, (t, k) in enumerate(self._pairs()):
            a, src, b, dst, _, _ = self.transfers[t]
            peer, peer_index = _peer(k)
            copies.append(pltpu.make_async_remote_copy(
                src_ref=src(ins[a], peer_index), dst_ref=dst(outs[b], _my_index() if sending else peer_index),
                send_sem=sems[0].at[s], recv_sem=sems[1].at[s], device_id=peer, device_id_type=MESH_ID))
        return copies

    def start(self, ins, outs, sems):
        for cp in self._local(ins, outs, sems) + self._remote(ins, outs, sems, True):
            cp.start()

    def wait(self, ins, outs, sems):
        for cp in self._remote(ins, outs, sems, False):
            cp.wait_recv()
        for cp in self._remote(ins, outs, sems, True):
            cp.wait_send()
        for cp in self._local(ins, outs, sems):
            cp.wait()

    def run(self, name):
        return _hosting_call(self, None, name=name, grid=(), in_specs=[], out_specs=[], out_shape=[], scratch_shapes=[],
                             compiler_params=pltpu.CompilerParams(has_side_effects=True))()


def _hosting_call(exchange, body, *, name, grid, in_specs, out_specs, out_shape, scratch_shapes, compiler_params):
    if exchange is None or not exchange.transfers:
        return pl.pallas_call(body, name=name, grid=grid, in_specs=in_specs, out_specs=out_specs, out_shape=out_shape,
                              scratch_shapes=scratch_shapes, compiler_params=compiler_params)
    n_in, n_out, n_scr = len(in_specs), len(out_specs), len(scratch_shapes)
    x_in, x_out = len(exchange.inputs), len(exchange.landings)
    continued = exchange.continued()

    def hosted(*refs):
        ins, x_ins = refs[:n_in], refs[n_in:n_in + x_in]
        o0 = n_in + x_in + len(continued)
        outs, x_outs = refs[o0:o0 + n_out], refs[o0 + n_out:o0 + n_out + x_out]
        s0 = o0 + n_out + x_out
        scratch, sems = refs[s0:s0 + n_scr], refs[s0 + n_scr:]
        if body is None:
            exchange.start(x_ins, x_outs, sems)
            exchange.wait(x_ins, x_outs, sems)
            return
        ids = [pl.program_id(a) for a in range(len(grid))]
        first = functools.reduce(jnp.logical_and, [i == 0 for i in ids])
        last = functools.reduce(jnp.logical_and, [i == g - 1 for i, g in zip(ids, grid)])

        @pl.when(first)
        def _():
            exchange.start(x_ins, x_outs, sems)

        body(*ins, *outs, *scratch)

        @pl.when(last)
        def _():
            exchange.wait(x_ins, x_outs, sems)

    aliases = {n_in + x_in + i: n_out + b for i, (b, _) in enumerate(continued)}
    kwargs = dict(grid=grid) if grid else {}
    call = pl.pallas_call(
        hosted, name=name, in_specs=list(in_specs) + [ANY_SPEC] * (x_in + len(continued)),
        out_specs=list(out_specs) + [ANY_SPEC] * x_out, out_shape=list(out_shape) + exchange.out_shapes(),
        scratch_shapes=list(scratch_shapes) + exchange.sem_shapes(), compiler_params=compiler_params,
        input_output_aliases=aliases, **kwargs)
    return lambda *operands: call(*operands, *exchange.inputs, *[g for _, g in continued])


def _whole(ref, j):
    return ref


def _slab(ref, j):
    return ref.at[j]


def _layer_of(l):
    return lambda ref, j: ref.at[l]


def _adamw(g, w, m, v):
    nm = ADAM_B1 * m + (1.0 - ADAM_B1) * g
    nv = ADAM_B2 * v + (1.0 - ADAM_B2) * jnp.square(g)
    m_hat = nm / (1.0 - ADAM_B1 ** ADAM_STEP)
    v_hat = nv / (1.0 - ADAM_B2 ** ADAM_STEP)
    return -ADAM_LR * (m_hat / (jnp.sqrt(v_hat) + ADAM_EPS) + ADAM_WD * w), nm, nv


def _sum_parts(p_ref):
    g = p_ref[0].astype(F32)
    for j in range(1, N_DEV):
        g = g + p_ref[j].astype(F32)
    return g


def _adamw_sharded(parts, w, m, v, name):
    shape = w.shape
    lead, (rows, cols) = shape[:-2], shape[-2:]
    tr = min(rows, ROW_BLOCK)
    assert rows % tr == 0
    steps = rows // tr
    nl = len(lead)
    spec = pl.BlockSpec((None,) * nl + (tr, cols), lambda *idx: idx + (0,))
    per_layer = isinstance(parts, (list, tuple))
    if per_layer:
        def part_spec(l):
            return pl.BlockSpec((N_DEV, tr, cols), lambda li, r: (0, jnp.where(li == l, r, jnp.where(li < l, 0, steps - 1)), 0))
        part_specs = [part_spec(l) for l in range(len(parts))]
    else:
        parts = [parts]
        part_specs = [pl.BlockSpec((N_DEV,) + (None,) * nl + (tr, cols), lambda *idx: (0,) + idx + (0,))]
    count = len(parts)

    def body(*refs):
        p_refs = refs[:count]
        w_ref, m_ref, v_ref, g_ref, d_ref, nm_ref, nv_ref = refs[count:]

        def update(p_ref):
            g = _sum_parts(p_ref)
            g_ref[...] = g
            d_ref[...], nm_ref[...], nv_ref[...] = _adamw(g, w_ref[...], m_ref[...], v_ref[...])

        if per_layer:
            for l in range(count):
                pl.when(pl.program_id(0) == l)(functools.partial(update, p_refs[l]))
        else:
            update(p_refs[0])

    return pl.pallas_call(
        body, name=name, grid=lead + (steps,),
        in_specs=part_specs + [spec, spec, spec], out_specs=[spec] * 4, out_shape=[_sds(shape, F32)] * 4,
        compiler_params=_params(("arbitrary",) * (nl + 1)),
    )(*parts, w, m, v)


def _adamw_replicated(gathered, w, m, v, depth):
    names = NORM_NAMES + VEC512_NAMES + VEC256_NAMES + LRU_MAT_NAMES
    count = len(names)

    def body(*refs):
        norms, v512, v256, mats = refs[:4]
        w_refs, m_refs, v_refs = (refs[4 + t * count:4 + (t + 1) * count] for t in range(3))
        outs = refs[4 + 3 * count:4 + 7 * count]
        sum_norms, sum_512, sum_256, unfolded = refs[4 + 7 * count:]
        sum_norms[...] = _sum_parts(norms)
        sum_512[...] = _sum_parts(v512)
        sum_256[...] = _sum_parts(v256)
        for n_, nm in enumerate(names):
            if nm in NORM_NAMES:
                g = sum_norms[pl.ds(depth * NORM_NAMES.index(nm), depth), :]
            elif nm in VEC512_NAMES:
                g = sum_512[pl.ds(depth * VEC512_NAMES.index(nm), depth), :]
            elif nm in VEC256_NAMES:
                g = sum_256[pl.ds(depth * VEC256_NAMES.index(nm), depth), :]
            else:
                p_ = LRU_MAT_NAMES.index(nm)
                folded = mats[0, p_].astype(F32)
                for j in range(1, N_DEV):
                    folded = folded + mats[j, p_].astype(F32)
                for hd in range(LRU_HEADS):
                    src_rows, src_cols = _folded_block(hd)
                    unfolded[:, hd * LRU_HD:(hd + 1) * LRU_HD, :] = folded[:, src_rows, src_cols]
                g = unfolded[...]
            delta, nm_, nv_ = _adamw(g, w_refs[n_][...], m_refs[n_][...], v_refs[n_][...])
            outs[n_][...] = g
            outs[count + n_][...] = delta
            outs[2 * count + n_][...] = nm_
            outs[3 * count + n_][...] = nv_

    shapes = [_sds(w[nm].shape, F32) for nm in names]
    ins = list(gathered) + [t[nm] for t in (w, m, v) for nm in names]
    outs = pl.pallas_call(
        body, name="adamw_replicated", in_specs=[VMEM_SPEC] * len(ins), out_specs=[VMEM_SPEC] * (4 * count),
        out_shape=shapes * 4,
        scratch_shapes=[pltpu.VMEM(gathered[0].shape[1:], F32), pltpu.VMEM(gathered[1].shape[1:], F32),
                        pltpu.VMEM(gathered[2].shape[1:], F32), pltpu.VMEM((depth, LRU_W, LRU_HD), F32)],
        compiler_params=_params(None, 48),
    )(*ins)
    return [dict(zip(names, outs[t * count:(t + 1) * count])) for t in range(4)]


WEIGHT_NAMES = ("meta_tokens", "norm_mix_pre", "norm_mix_post", "norm_mlp_pre", "norm_mlp_post", "w_in", "conv_w", "conv_b",
                "lru_wa_f", "lru_ba_f", "lru_wx_f", "lru_bx_f", "lru_lambda_f", "lru_wa_b", "lru_ba_b", "lru_wx_b",
                "lru_bx_b", "lru_lambda_b", "gla_wg_f", "gla_bg_f", "gla_wg_b", "gla_bg_b", "gla_head_norm", "w_out",
                "w_mlp_up", "w_mlp_down")
MATMUL_WEIGHTS = ("w_in", "w_out", "w_mlp_up", "w_mlp_down")
SMALL_SHARDED = ("conv_w", "gla_wg_f", "gla_wg_b", "meta_tokens")


def kernel(x, meta_tokens, norm_mix_pre, norm_mix_post, norm_mlp_pre, norm_mlp_post, w_in, conv_w, conv_b, lru_wa_f, lru_ba_f, lru_wx_f, lru_bx_f, lru_lambda_f, lru_wa_b, lru_ba_b, lru_wx_b, lru_bx_b, lru_lambda_b, gla_wg_f, gla_bg_f, gla_wg_b, gla_bg_b, gla_head_norm, w_out, w_mlp_up, w_mlp_down, loss_target, m_meta_tokens, m_norm_mix_pre, m_norm_mix_post, m_norm_mlp_pre, m_norm_mlp_post, m_w_in, m_conv_w, m_conv_b, m_lru_wa_f, m_lru_ba_f, m_lru_wx_f, m_lru_bx_f, m_lru_lambda_f, m_lru_wa_b, m_lru_ba_b, m_lru_wx_b, m_lru_bx_b, m_lru_lambda_b, m_gla_wg_f, m_gla_bg_f, m_gla_wg_b, m_gla_bg_b, m_gla_head_norm, m_w_out, m_w_mlp_up, m_w_mlp_down, v_meta_tokens, v_norm_mix_pre, v_norm_mix_post, v_norm_mlp_pre, v_norm_mlp_post, v_w_in, v_conv_w, v_conv_b, v_lru_wa_f, v_lru_ba_f, v_lru_wx_f, v_lru_bx_f, v_lru_lambda_f, v_lru_wa_b, v_lru_ba_b, v_lru_wx_b, v_lru_bx_b, v_lru_lambda_b, v_gla_wg_f, v_gla_bg_f, v_gla_wg_b, v_gla_bg_b, v_gla_head_norm, v_w_out, v_w_mlp_up, v_w_mlp_down):
    args = locals()
    w = {nm: args[nm] for nm in WEIGHT_NAMES}
    m = {nm: args["m_" + nm] for nm in WEIGHT_NAMES}
    v = {nm: args["v_" + nm] for nm in WEIGHT_NAMES}
    depth = w_in.shape[0]

    shards = {nm: w[nm].astype(BF16) for nm in MATMUL_WEIGHTS}
    p = {}
    gather = _Outbox(_install_weight(p))
    _request_weight(gather, shards, "w_in", 0)
    _request_weight(gather, shards, "w_out", 0)
    ex, tickets = gather.exchange([("w_in", group) for group in range(len(PEER_GROUPS))])
    first_small = len(ex.landings)
    for nm in SMALL_SHARDED:
        ex.add(w[nm], _whole, _sds((N_DEV,) + w[nm].shape, F32), _slab)
    landed = ex.run("all_gather")
    gather.store(tickets, landed)
    p.update(_prepare_params(w, dict(zip(SMALL_SHARDED, landed[first_small:])), depth))

    h = jnp.concatenate([jnp.zeros((PAD_ROWS, D_MODEL), F32), p["meta_tokens"], x[0]], axis=0)
    saved = []
    for l in range(depth):
        h, s = _layer_fwd(h, l, p, gather, shards, depth)
        saved.append(s)
    dh, loss_part = _loss_and_grad(h, loss_target[0])
    loss = lax.psum(loss_part[0, 0], ("x", "y", "c"))

    received = {}
    outbox = _Outbox(received.__setitem__)
    grads = [None] * depth
    for l in reversed(range(depth)):
        dh, grads[l], dz = _layer_bwd(dh, l, p, saved[l], outbox)
        if l > 0:
            _send_grad(outbox, "w_in", l, _w_in_slabs(_matmul_tn(saved[l]["hn"], dz, "grad_w_in")))
    grad_x = dh[PAD_ROWS + N_META:][None]

    small = _pack_small_grads(grads, dh, depth)
    rep_bufs, small_slabs = small[:4], small[4:]
    ex, tickets = outbox.exchange()
    first_small = len(ex.landings)
    for g in small_slabs:
        ex.add(g, _slab, _sds(g.shape, F32), _slab)
    for g in rep_bufs:
        ex.add(g, _whole, _sds((N_DEV,) + g.shape, g.dtype), _slab)
    grad_w_in, *landed = _matmul_tn(saved[0]["hn"], dz, "grad_w_in", exchange=ex)
    outbox.store(tickets, landed)
    small_received = landed[first_small:first_small + len(small_slabs)]
    rep_received = landed[first_small + len(small_slabs):]
    _send_grad(outbox, "w_in", 0, _w_in_slabs(grad_w_in))
    ex, tickets = outbox.exchange()
    outbox.store(tickets, ex.run("exchange_grads"))

    results = [{}, {}, {}, {}]
    for nm in MATMUL_WEIGHTS:
        parts = [received[(nm, l)] for l in range(depth)]
        for t, out in enumerate(_adamw_sharded(parts, w[nm], m[nm], v[nm], "adamw_" + nm)):
            results[t][nm] = out
    for nm, parts in zip(SMALL_SHARDED, small_received):
        for t, out in enumerate(_adamw_sharded(parts, w[nm], m[nm], v[nm], "adamw_" + nm)):
            results[t][nm] = out

    def kernel_side(tree):
        return {nm: tree[nm].reshape(depth, LRU_W, LRU_HD) if nm in LRU_MAT_NAMES else tree[nm]
                for nm in NORM_NAMES + VEC512_NAMES + VEC256_NAMES + LRU_MAT_NAMES}

    for t, tree in enumerate(_adamw_replicated(rep_received, kernel_side(w), kernel_side(m), kernel_side(v), depth)):
        for nm, out in tree.items():
            results[t][nm] = out.reshape(w[nm].shape)
    return (loss, grad_x, *[results[t][nm] for t in range(4) for nm in WEIGHT_NAMES])
```

```python
import functools

import jax
import jax.numpy as jnp
from jax import lax
from jax.experimental import pallas as pl
from jax.experimental.pallas import tpu as pltpu

F32 = jnp.float32
BF16 = jnp.bfloat16

N_DEV = 8
D_MODEL = 1024
N_META = 16
ROW_BLOCK = 256
PAD_ROWS = ROW_BLOCK - N_META
CHUNK = 128
LRU_W = 512
LRU_HEADS = 8
LRU_HD = 64
LRU_C = 8.0
GLA_HEADS = 4
GLA_DK = 64
GLA_DV = 128
GLA_QK = GLA_HEADS * GLA_DK
GLA_W = GLA_HEADS * GLA_DV
GLA_RANK = 16
GATE_NORM = 16.0
D_FF = 4096
D_IN = 2592
Z_W = 2688
ZG_COL_BLOCK = 2560 // 128
EPS = 1e-6
LANES = 128

ADAM_LR = 0.001
ADAM_B1 = 0.9
ADAM_B2 = 0.999
ADAM_EPS = 1e-08
ADAM_WD = 0.01
ADAM_STEP = 10

VMEM_SPEC = pl.BlockSpec(memory_space=pltpu.VMEM)
ANY_SPEC = pl.BlockSpec(memory_space=pl.ANY)
MESH_ID = pl.DeviceIdType.MESH


def _sds(shape, dtype):
    return jax.ShapeDtypeStruct(shape, dtype)


def _params(sem=None, vmem_mb=None):
    kw = {}
    if sem is not None:
        kw["dimension_semantics"] = sem
    if vmem_mb is not None:
        kw["vmem_limit_bytes"] = vmem_mb * 2 ** 20
    return pltpu.CompilerParams(**kw)


def _row_tile(n, cap=768):
    for t in (768, 512, 384, 256):
        if t <= cap and n % t == 0:
            return t
    raise ValueError(n)


def _col_tile(k):
    for t in (1024, 896, 768, 640, 512, 384, 256, 128):
        if k % t == 0:
            return t
    raise ValueError(k)


def _sigmoid(x):
    return 0.5 * jnp.tanh(0.5 * x) + 0.5


def _gelu_and_grad(x):
    c = 0.7978845608028654
    inner = c * (x + 0.044715 * x * x * x)
    t = jnp.tanh(inner)
    gelu = 0.5 * x * (1.0 + t)
    dgelu = 0.5 * (1.0 + t) + 0.5 * x * (1.0 - t * t) * c * (1.0 + 3.0 * 0.044715 * x * x)
    return gelu, dgelu


def _one_minus_square(a, log_a):
    return jnp.tanh(-log_a) * (1.0 + a * a)


def _rms_fwd(x, g):
    rs = lax.rsqrt(jnp.mean(x * x, axis=-1, keepdims=True) + EPS)
    return x * rs * g


def _rms_bwd(x, g, dy):
    rs = lax.rsqrt(jnp.mean(x * x, axis=-1, keepdims=True) + EPS)
    xh = x * rs
    dyg = dy * g
    dx = rs * (dyg - xh * jnp.mean(dyg * xh, axis=-1, keepdims=True))
    return dx, jnp.sum(dy * xh, axis=0, keepdims=True)


def _dot(a, b):
    return jnp.dot(a.astype(BF16), b.astype(BF16), preferred_element_type=F32)


def _dot_nt(a, b):
    return lax.dot_general(a.astype(BF16), b.astype(BF16), (((1,), (1,)), ((), ())), preferred_element_type=F32)


def _dot_tn(a, b):
    return lax.dot_general(a.astype(BF16), b.astype(BF16), (((0,), (0,)), ((), ())), preferred_element_type=F32)


class _LayerParam:
    def __init__(self, array, *index):
        self.array = array
        self.index = index

    @property
    def spec(self):
        lead = len(self.index)
        tail = self.array.shape[lead:]
        index = self.index
        return pl.BlockSpec((None,) * lead + tail, lambda *_: index + (0,) * len(tail))


def _row_ids(rows, block_index):
    return block_index * rows + lax.broadcasted_iota(jnp.int32, (rows, 1), 0)


def _accumulate(ref, value, first):
    @pl.when(first)
    def _():
        ref[...] = value

    @pl.when(jnp.logical_not(first))
    def _():
        ref[...] += value


def _norm_in_proj(h, g, w, exchange=None):
    n, d = h.shape
    zw = w.shape[1]
    tr = _row_tile(n)

    def body(h_ref, g_ref, w_ref, hn_ref, z_ref):
        hn = _rms_fwd(h_ref[...], g_ref[...]).astype(BF16)
        hn_ref[...] = hn
        z_ref[...] = jnp.dot(hn, w_ref[...], preferred_element_type=F32)

    return _hosting_call(
        exchange, body, name="norm_in_proj", grid=(n // tr,),
        in_specs=[pl.BlockSpec((tr, d), lambda i: (i, 0)), g.spec, VMEM_SPEC],
        out_specs=[pl.BlockSpec((tr, d), lambda i: (i, 0)), pl.BlockSpec((tr, zw), lambda i: (i, 0))],
        out_shape=[_sds((n, d), BF16), _sds((n, zw), F32)],
        scratch_shapes=[], compiler_params=_params(("arbitrary",), 48),
    )(h, g.array, w)


def _halo_specs(width, nb, rows, col=0):
    per = rows // 8
    prev = pl.BlockSpec((8, width), lambda i: (jnp.maximum(i * per - 1, 0), col))
    nxt = pl.BlockSpec((8, width), lambda i: (jnp.minimum((i + 1) * per, nb * per - 1), col))
    return prev, nxt


def _shift_down(x, prev8, d):
    n = x.shape[0]
    r = pltpu.roll(x, d, 0)
    p = pltpu.roll(prev8, d, 0)
    row8 = lax.broadcasted_iota(jnp.int32, (8, 1), 0)
    head = jnp.where(row8 < d, p, r[0:8])
    return jnp.concatenate([head, r[8:]], axis=0)


def _shift_up(x, next8, d):
    n = x.shape[0]
    r = pltpu.roll(x, n - d, 0)
    q = pltpu.roll(next8, 8 - d, 0)
    row8 = lax.broadcasted_iota(jnp.int32, (8, 1), 0)
    tail = jnp.where(row8 >= 8 - d, q, r[n - 8:])
    return jnp.concatenate([r[:n - 8], tail], axis=0)


def _conv_fwd(z, conv_w, conv_b):
    n = z.shape[0]
    tr = _row_tile(n)
    nb = n // tr
    prev_spec, next_spec = _halo_specs(LRU_W, nb, tr)

    def body(cur_ref, prev_ref, next_ref, w_ref, b_ref, xc_ref):
        i = pl.program_id(0)
        cur = cur_ref[...]
        prev8 = prev_ref[...] * jnp.where(i > 0, 1.0, 0.0)
        next8 = next_ref[...] * jnp.where(i < nb - 1, 1.0, 0.0)
        w = [w_ref[pl.ds(k, 1), :] for k in range(4)]
        xc = (w[0] * _shift_down(cur, prev8, 2) + w[1] * _shift_down(cur, prev8, 1)
              + w[2] * cur + w[3] * _shift_up(cur, next8, 1) + b_ref[...])
        xc_ref[...] = xc

    return pl.pallas_call(
        body, name="conv_fwd", grid=(nb,),
        in_specs=[pl.BlockSpec((tr, LRU_W), lambda i: (i, 0)), prev_spec, next_spec, conv_w.spec, conv_b.spec],
        out_specs=pl.BlockSpec((tr, LRU_W), lambda i: (i, 0)),
        out_shape=_sds((n, LRU_W), F32),
        compiler_params=_params(("parallel",), 32),
    )(z, z, z, conv_w.array, conv_b.array)


def _conv_bwd(dxc_f, dxc_b, z, conv_w):
    n = z.shape[0]
    tr = _row_tile(n)
    nb = n // tr
    prev_spec, next_spec = _halo_specs(LRU_W, nb, tr)
    row_spec = pl.BlockSpec((tr, LRU_W), lambda i: (i, 0))

    def body(df_ref, dfp_ref, dfn_ref, db_ref, dbp_ref, dbn_ref, x_ref, xp_ref, xn_ref, w_ref,
             dx_ref, dw_ref, dbias_ref):
        i = pl.program_id(0)
        has_prev = jnp.where(i > 0, 1.0, 0.0)
        has_next = jnp.where(i < nb - 1, 1.0, 0.0)
        dxc = df_ref[...] + db_ref[...]
        dprev = (dfp_ref[...] + dbp_ref[...]) * has_prev
        dnext = (dfn_ref[...] + dbn_ref[...]) * has_next
        x = x_ref[...]
        xprev = xp_ref[...] * has_prev
        xnext = xn_ref[...] * has_next
        w = [w_ref[pl.ds(k, 1), :] for k in range(4)]
        dx_ref[...] = (w[0] * _shift_up(dxc, dnext, 2) + w[1] * _shift_up(dxc, dnext, 1)
                       + w[2] * dxc + w[3] * _shift_down(dxc, dprev, 1)).astype(BF16)
        dw = jnp.concatenate([
            jnp.sum(dxc * _shift_down(x, xprev, 2), axis=0, keepdims=True),
            jnp.sum(dxc * _shift_down(x, xprev, 1), axis=0, keepdims=True),
            jnp.sum(dxc * x, axis=0, keepdims=True),
            jnp.sum(dxc * _shift_up(x, xnext, 1), axis=0, keepdims=True),
            jnp.zeros((4, LRU_W), F32)], axis=0)
        _accumulate(dw_ref, dw, i == 0)
        _accumulate(dbias_ref, jnp.sum(dxc, axis=0, keepdims=True), i == 0)

    dx, dw, dbias = pl.pallas_call(
        body, name="conv_bwd", grid=(nb,),
        in_specs=[row_spec, prev_spec, next_spec, row_spec, prev_spec, next_spec, row_spec, prev_spec, next_spec,
                  conv_w.spec],
        out_specs=[row_spec, pl.BlockSpec((8, LRU_W), lambda i: (0, 0)), pl.BlockSpec((1, LRU_W), lambda i: (0, 0))],
        out_shape=[_sds((n, LRU_W), BF16), _sds((8, LRU_W), F32), _sds((1, LRU_W), F32)],
        compiler_params=_params(("arbitrary",), 40),
    )(dxc_f, dxc_f, dxc_f, dxc_b, dxc_b, dxc_b, z, z, z, conv_w.array)
    return dx, dw, dbias


def _mix_epilogue(h_f, h_b, o_f, o_b, z, head_norm):
    n = z.shape[0]
    tr = _row_tile(n)
    spec = pl.BlockSpec((tr, 512), lambda i: (i, 0))

    def body(hf_ref, hb_ref, of_ref, ob_ref, gate_ref, gout_ref, w_ref, y_ref):
        gelu, _ = _gelu_and_grad(gate_ref[...])
        y_ref[:, 0:LRU_W] = ((hf_ref[...] + hb_ref[...]) * gelu).astype(BF16)
        o = of_ref[...] + ob_ref[...]
        gout = gout_ref[...]
        silu = gout * _sigmoid(gout)
        w = w_ref[...]
        for hd in range(GLA_HEADS):
            cs = slice(hd * GLA_DV, (hd + 1) * GLA_DV)
            oh = o[:, cs]
            on = oh * lax.rsqrt(jnp.mean(oh * oh, axis=-1, keepdims=True) + EPS)
            y_ref[:, LRU_W + hd * GLA_DV:LRU_W + (hd + 1) * GLA_DV] = (on * w[:, cs] * silu[:, cs]).astype(BF16)

    return pl.pallas_call(
        body, name="mix_epilogue", grid=(n // tr,),
        in_specs=[spec, spec, spec, spec, pl.BlockSpec((tr, 512), lambda i: (i, 1)),
                  pl.BlockSpec((tr, 512), lambda i: (i, 4)), head_norm.spec],
        out_specs=pl.BlockSpec((tr, D_MODEL), lambda i: (i, 0)),
        out_shape=_sds((n, D_MODEL), BF16),
        compiler_params=_params(("parallel",), 40),
    )(h_f, h_b, o_f, o_b, z, z, head_norm.array)


def _mix_epilogue_bwd(dymix, h_f, h_b, o_f, o_b, z, head_norm):
    n = z.shape[0]
    tr = _row_tile(n)
    spec = pl.BlockSpec((tr, 512), lambda i: (i, 0))

    def body(dyl_ref, dyg_ref, hf_ref, hb_ref, of_ref, ob_ref, gate_ref, gout_ref, w_ref,
             dhs_ref, dgate_ref, do_ref, dgout_ref, dw_ref):
        i = pl.program_id(0)
        dyl = dyl_ref[...]
        gelu, dgelu = _gelu_and_grad(gate_ref[...])
        dhs_ref[...] = dyl * gelu
        dgate_ref[...] = (dyl * (hf_ref[...] + hb_ref[...]) * dgelu).astype(BF16)
        dyg = dyg_ref[...]
        o = of_ref[...] + ob_ref[...]
        gout = gout_ref[...]
        sg = _sigmoid(gout)
        silu = gout * sg
        dsilu = sg * (1.0 + gout * (1.0 - sg))
        w = w_ref[...]
        dws = []
        for hd in range(GLA_HEADS):
            cs = slice(hd * GLA_DV, (hd + 1) * GLA_DV)
            oh = o[:, cs]
            rs = lax.rsqrt(jnp.mean(oh * oh, axis=-1, keepdims=True) + EPS)
            on = oh * rs
            dy = dyg[:, cs]
            dgout_ref[:, cs] = (dy * on * w[:, cs] * dsilu[:, cs]).astype(BF16)
            dys = dy * silu[:, cs]
            dws.append(jnp.sum(dys * on, axis=0, keepdims=True))
            don = dys * w[:, cs]
            do_ref[:, cs] = (rs * (don - on * jnp.mean(don * on, axis=-1, keepdims=True))).astype(BF16)
        _accumulate(dw_ref, jnp.concatenate(dws, axis=1), i == 0)

    return pl.pallas_call(
        body, name="mix_epilogue_bwd", grid=(n // tr,),
        in_specs=[pl.BlockSpec((tr, 512), lambda i: (i, 0)), pl.BlockSpec((tr, 512), lambda i: (i, 1)),
                  spec, spec, spec, spec, pl.BlockSpec((tr, 512), lambda i: (i, 1)),
                  pl.BlockSpec((tr, 512), lambda i: (i, 4)), head_norm.spec],
        out_specs=[spec, spec, spec, spec, pl.BlockSpec((1, GLA_W), lambda i: (0, 0))],
        out_shape=[_sds((n, 512), F32)] + [_sds((n, 512), BF16)] * 3 + [_sds((1, GLA_W), F32)],
        compiler_params=_params(("arbitrary",), 48),
    )(dymix, dymix, h_f, h_b, o_f, o_b, z, z, head_norm.array)


def _out_proj(ymix, w_out, h, g, exchange=None):
    n, d = h.shape
    tr = _row_tile(n)
    spec = pl.BlockSpec((tr, d), lambda i: (i, 0))

    def body(y_ref, w_ref, h_ref, g_ref, mix_ref, hmid_ref):
        mix = jnp.dot(y_ref[...], w_ref[...], preferred_element_type=F32)
        mix_ref[...] = mix
        hmid_ref[...] = h_ref[...] + _rms_fwd(mix, g_ref[...])

    return _hosting_call(
        exchange, body, name="out_proj", grid=(n // tr,),
        in_specs=[spec, VMEM_SPEC, spec, g.spec],
        out_specs=[spec, spec],
        out_shape=[_sds((n, d), F32), _sds((n, d), F32)],
        scratch_shapes=[], compiler_params=_params(("arbitrary",), 44),
    )(ymix, w_out, h, g.array)


def _out_proj_bwd(dh_mid, mix, g, w_out, exchange=None):
    n, d = mix.shape
    tr = _row_tile(n)
    spec = pl.BlockSpec((tr, d), lambda i: (i, 0))

    def body(dh_ref, mix_ref, g_ref, w_ref, dmix_ref, dy_ref, dg_ref):
        i = pl.program_id(0)
        dmix, dg = _rms_bwd(mix_ref[...], g_ref[...], dh_ref[...])
        dmix = dmix.astype(BF16)
        dmix_ref[...] = dmix
        dy_ref[...] = _dot_nt(dmix, w_ref[...])
        _accumulate(dg_ref, dg, i == 0)

    return _hosting_call(
        exchange, body, name="out_proj_bwd", grid=(n // tr,),
        in_specs=[spec, spec, g.spec, VMEM_SPEC],
        out_specs=[spec, spec, pl.BlockSpec((1, d), lambda i: (0, 0))],
        out_shape=[_sds((n, d), BF16), _sds((n, d), F32), _sds((1, d), F32)],
        scratch_shapes=[], compiler_params=_params(("arbitrary",), 44),
    )(dh_mid, mix, g.array, w_out)


FF_SLAB = D_FF // N_DEV


def _relu_squared(up):
    return jnp.square(jnp.maximum(up.astype(F32), 0.0)).astype(BF16)


def _mlp_fwd(h_mid, g_pre, w_up, w_down, g_post, exchange=None):
    n, d = h_mid.shape
    tr = _row_tile(n, 384)
    spec = pl.BlockSpec((tr, d), lambda i: (i, 0))

    def body(h_ref, gpre_ref, wup_ref, wdn_ref, gpost_ref, hn_ref, up_ref, ff_ref, hout_ref):
        h = h_ref[...]
        hn = _rms_fwd(h, gpre_ref[...]).astype(BF16)
        hn_ref[...] = hn
        ff = jnp.zeros((tr, d), F32)
        for j in range(N_DEV):
            cs = slice(j * FF_SLAB, (j + 1) * FF_SLAB)
            up = jnp.dot(hn, wup_ref[j], preferred_element_type=F32).astype(BF16)
            up_ref[:, cs] = up
            ff = ff + jnp.dot(_relu_squared(up), wdn_ref[cs, :], preferred_element_type=F32)
        ff_ref[...] = ff
        hout_ref[...] = h + _rms_fwd(ff, gpost_ref[...])

    return _hosting_call(
        exchange, body, name="mlp_fwd", grid=(n // tr,),
        in_specs=[spec, g_pre.spec, VMEM_SPEC, VMEM_SPEC, g_post.spec],
        out_specs=[spec, pl.BlockSpec((tr, D_FF), lambda i: (i, 0)), spec, spec],
        out_shape=[_sds((n, d), BF16), _sds((n, D_FF), BF16), _sds((n, d), F32), _sds((n, d), F32)],
        scratch_shapes=[], compiler_params=_params(("arbitrary",), 52),
    )(h_mid, g_pre.array, w_up, w_down, g_post.array)


def _mlp_bwd(dh, ff, up, h_mid, g_pre, w_up, w_down, g_post, exchange=None):
    n, d = h_mid.shape
    tr = _row_tile(n, 384)
    spec = pl.BlockSpec((tr, d), lambda i: (i, 0))
    wide = pl.BlockSpec((tr, D_FF), lambda i: (i, 0))
    gspec = pl.BlockSpec((1, d), lambda i: (0, 0))

    def body(dh_ref, ff_ref, up_ref, h_ref, gpre_ref, wup_ref, wdn_ref, gpost_ref,
             dff_ref, dup_ref, dhmid_ref, dgpost_ref, dgpre_ref):
        i = pl.program_id(0)
        dh = dh_ref[...]
        dff, dgpost = _rms_bwd(ff_ref[...], gpost_ref[...], dh)
        dff = dff.astype(BF16)
        dff_ref[...] = dff
        dhn = jnp.zeros((tr, d), F32)
        for j in range(N_DEV):
            cs = slice(j * FF_SLAB, (j + 1) * FF_SLAB)
            relu = jnp.maximum(up_ref[:, cs].astype(F32), 0.0)
            dact = _dot_nt(dff, wdn_ref[cs, :])
            dup = (dact * 2.0 * relu).astype(BF16)
            dup_ref[:, cs] = dup
            dhn = dhn + _dot_nt(dup, wup_ref[j])
        dx, dgpre = _rms_bwd(h_ref[...], gpre_ref[...], dhn)
        dhmid_ref[...] = dh + dx
        _accumulate(dgpost_ref, dgpost, i == 0)
        _accumulate(dgpre_ref, dgpre, i == 0)

    return _hosting_call(
        exchange, body, name="mlp_bwd", grid=(n // tr,),
        in_specs=[spec, spec, wide, spec, g_pre.spec, VMEM_SPEC, VMEM_SPEC, g_post.spec],
        out_specs=[spec, wide, spec, gspec, gspec],
        out_shape=[_sds((n, d), BF16), _sds((n, D_FF), BF16), _sds((n, d), F32), _sds((1, d), F32), _sds((1, d), F32)],
        scratch_shapes=[], compiler_params=_params(("arbitrary",), 56),
    )(dh, ff, up, h_mid, g_pre.array, w_up, w_down, g_post.array)


def _in_proj_bwd(pieces, w_in, h, g, dh_mid, exchange=None):
    dxbr, dgate, dqk_f, dqk_b, dv_f, dv_b, dgout, dzg_f, dzg_b = pieces
    n, d = h.shape
    tr = _row_tile(n, 384)
    spec = pl.BlockSpec((tr, d), lambda i: (i, 0))
    s512 = pl.BlockSpec((tr, 512), lambda i: (i, 0))
    s128 = pl.BlockSpec((tr, LANES), lambda i: (i, 0))

    def body(a_ref, b_ref, cf_ref, cb_ref, df_ref, db_ref, e_ref, ff_ref, fb_ref, w_ref, h_ref, g_ref, dhm_ref,
             dz_ref, dh_ref, dg_ref):
        i = pl.program_id(0)
        real = (_row_ids(tr, i) >= PAD_ROWS).astype(F32)
        f32 = lambda ref: ref[...].astype(F32)
        dz = jnp.concatenate([f32(a_ref), f32(b_ref), f32(cf_ref) + f32(cb_ref), f32(df_ref) + f32(db_ref),
                              f32(e_ref), f32(ff_ref) + f32(fb_ref)], axis=1) * real
        dz = dz.astype(BF16)
        dz_ref[...] = dz
        dhn = _dot_nt(dz, w_ref[...])
        dx, dg = _rms_bwd(h_ref[...], g_ref[...], dhn)
        dh_ref[...] = (dhm_ref[...] + dx) * real
        _accumulate(dg_ref, dg, i == 0)

    return _hosting_call(
        exchange, body, name="in_proj_bwd", grid=(n // tr,),
        in_specs=[s512, s512, s512, s512, s512, s512, s512, s128, s128, VMEM_SPEC, spec, g.spec, spec],
        out_specs=[pl.BlockSpec((tr, Z_W), lambda i: (i, 0)), spec, pl.BlockSpec((1, d), lambda i: (0, 0))],
        out_shape=[_sds((n, Z_W), BF16), _sds((n, d), F32), _sds((1, d), F32)],
        scratch_shapes=[], compiler_params=_params(("arbitrary",), 48),
    )(dxbr, dgate, dqk_f, dqk_b, dv_f, dv_b, dgout, dzg_f, dzg_b, w_in, h, g.array, dh_mid)


def _matmul_tn(a, b, name, column_slabs=False, exchange=None, a_map=None):
    n, m = a.shape
    k = b.shape[1]
    tr = next(t for t in (2816, 1408, 768, 512, 256) if n % t == 0)
    tm, tk = _col_tile(m), _col_tile(k)
    steps = n // tr
    slab = k // N_DEV
    per_step = tk // slab if column_slabs else 1
    sub = next(t for t in (704, 768, 512, 256) if tr % t == 0)

    def body(a_ref, b_ref, o_ref, acc_ref, *mapped_ref):
        r = pl.program_id(2)
        if a_map is None:
            a_blk = a_ref[...]
        else:
            for c in range(tr // sub):
                rows = pl.ds(c * sub, sub)
                mapped_ref[0][rows, :] = a_map(a_ref[rows, :])
            a_blk = mapped_ref[0][...]
        _accumulate(acc_ref, _dot_tn(a_blk, b_ref[...]), r == 0)

        @pl.when(r == steps - 1)
        def _():
            if column_slabs:
                for j in range(per_step):
                    o_ref[j] = acc_ref[:, j * slab:(j + 1) * slab].astype(BF16)
            else:
                o_ref[...] = acc_ref[...].astype(BF16)

    if column_slabs:
        out_spec = pl.BlockSpec((per_step, tm, slab), lambda mi, ki, r: (ki, mi, 0))
        out_shape = _sds((N_DEV, m, slab), BF16)
    else:
        out_spec = pl.BlockSpec((tm, tk), lambda mi, ki, r: (mi, ki))
        out_shape = _sds((m, k), BF16)
    outs = _hosting_call(
        exchange, body, name=name, grid=(m // tm, k // tk, steps),
        in_specs=[pl.BlockSpec((tr, tm), lambda mi, ki, r: (r, mi)), pl.BlockSpec((tr, tk), lambda mi, ki, r: (r, ki))],
        out_specs=[out_spec], out_shape=[out_shape],
        scratch_shapes=[pltpu.VMEM((tm, tk), F32)] + ([] if a_map is None else [pltpu.VMEM((tr, tm), BF16)]),
        compiler_params=_params(("arbitrary", "arbitrary", "arbitrary"), 52),
    )(a, b)
    return outs[0] if exchange is None else outs


def _loss_and_grad(h_out, target):
    n, d = h_out.shape
    tr = ROW_BLOCK
    first = (PAD_ROWS + N_META) // tr

    def body(h_ref, t_ref, dh_ref, loss_ref):
        i = pl.program_id(0)
        real = jnp.where(i >= first, 1.0, 0.0)
        diff = (h_ref[...] - t_ref[...]) * real
        dh_ref[...] = diff * (1.0 / d)
        part = 0.5 * jnp.sum(jnp.mean(diff * diff, axis=-1, keepdims=True), axis=0, keepdims=True)
        _accumulate(loss_ref, jnp.broadcast_to(part, (1, LANES)), i == 0)

    return pl.pallas_call(
        body, name="loss_and_grad", grid=(n // tr,),
        in_specs=[pl.BlockSpec((tr, d), lambda i: (i, 0)), pl.BlockSpec((tr, d), lambda i: (jnp.maximum(i - first, 0), 0))],
        out_specs=[pl.BlockSpec((tr, d), lambda i: (i, 0)), pl.BlockSpec((1, LANES), lambda i: (0, 0))],
        out_shape=[_sds((n, d), F32), _sds((1, LANES), F32)],
        compiler_params=_params(("arbitrary",)),
    )(h_out, target)


def _scan_block(a, u, h_in, reverse):
    n = a.shape[0]
    row = lax.broadcasted_iota(jnp.int32, (n, 1), 0)
    d = 1
    while d < n:
        shift = n - d if reverse else d
        keep = (row < n - d) if reverse else (row >= d)
        a_s = pltpu.roll(a, shift, 0)
        u_s = pltpu.roll(u, shift, 0)
        u = jnp.where(keep, a * u_s + u, u)
        a = jnp.where(keep, a * a_s, a)
        d *= 2
    return a * h_in + u


def _lru_gates(xc, wcat_ref, bias_ref, lam_ref):
    nl = -lam_ref[...]
    nsp = -LRU_C * (jnp.maximum(nl, 0.0) + jnp.log(1.0 + jnp.exp(-jnp.abs(nl))))
    pre = _dot(xc, wcat_ref[...]) + bias_ref[...]
    r = _sigmoid(pre[:, :LRU_W])
    ig = _sigmoid(pre[:, LRU_W:])
    log_a = r * nsp
    a = jnp.exp(log_a)
    m2 = _one_minus_square(a, log_a)
    inv_m = lax.rsqrt(jnp.maximum(m2, 1e-30))
    return r, ig, a, m2 * inv_m, inv_m, nsp


def _lru_scan(xc, wcat, bias, lam, reverse, exchange=None):
    n = xc.shape[0]
    nb = n // ROW_BLOCK
    order = (lambda i: nb - 1 - i) if reverse else (lambda i: i)
    spec = pl.BlockSpec((ROW_BLOCK, LRU_W), lambda i: (order(i), 0))
    edge = 0 if reverse else ROW_BLOCK - 1

    def body(xc_ref, wcat_ref, bias_ref, lam_ref, h_ref, carry_ref):
        i = pl.program_id(0)

        @pl.when(i == 0)
        def _():
            carry_ref[...] = jnp.zeros_like(carry_ref)

        xc = xc_ref[...]
        r, ig, a, m, _, _ = _lru_gates(xc, wcat_ref, bias_ref, lam_ref)
        u = jnp.where(_row_ids(ROW_BLOCK, order(i)) >= PAD_ROWS, m * (ig * xc), 0.0)
        h_ref[...] = _scan_block(a, u, carry_ref[0:1, :], reverse)
        carry_ref[0:1, :] = h_ref[pl.ds(edge, 1), :]

    return _hosting_call(
        exchange, body, name="lru_scan_b" if reverse else "lru_scan_f", grid=(nb,),
        in_specs=[spec, wcat.spec, bias.spec, lam.spec],
        out_specs=[spec],
        out_shape=[_sds((n, LRU_W), F32)],
        scratch_shapes=[pltpu.VMEM((8, LRU_W), F32)],
        compiler_params=_params(("arbitrary",)),
    )(xc, wcat.array, bias.array, lam.array)


def _lru_scan_bwd(dhs, xc, h, wcat, bias, lam, reverse, exchange=None):
    n = xc.shape[0]
    nb = n // ROW_BLOCK
    per = ROW_BLOCK // 8
    order = (lambda i: i) if reverse else (lambda i: nb - 1 - i)
    spec = pl.BlockSpec((ROW_BLOCK, LRU_W), lambda i: (order(i), 0))
    if reverse:
        halo = pl.BlockSpec((8, LRU_W), lambda i: (jnp.minimum((order(i) + 1) * per, nb * per - 1), 0))
    else:
        halo = pl.BlockSpec((8, LRU_W), lambda i: (jnp.maximum(order(i) * per - 1, 0), 0))
    edge = ROW_BLOCK - 1 if reverse else 0

    def body(dhs_ref, xc_ref, h_ref, halo_ref, wcat_ref, bias_ref, lam_ref,
             dxc_ref, dw_ref, db_ref, dlam_ref, cdh_ref, ca_ref, tmp_ref):
        i = pl.program_id(0)
        ib = order(i)

        @pl.when(i == 0)
        def _():
            cdh_ref[...] = jnp.zeros_like(cdh_ref)
            ca_ref[...] = jnp.zeros_like(ca_ref)

        xc = xc_ref[...]
        r, ig, a, m, inv_m, nsp = _lru_gates(xc, wcat_ref, bias_ref, lam_ref)
        row = lax.broadcasted_iota(jnp.int32, (ROW_BLOCK, 1), 0)
        if reverse:
            coef = jnp.where(row == 0, ca_ref[0:1, :], pltpu.roll(a, 1, 0))
            h_nb = jnp.where(row == ROW_BLOCK - 1, halo_ref[0:1, :] * jnp.where(ib < nb - 1, 1.0, 0.0),
                             pltpu.roll(h_ref[...], ROW_BLOCK - 1, 0))
        else:
            coef = jnp.where(row == ROW_BLOCK - 1, ca_ref[0:1, :], pltpu.roll(a, ROW_BLOCK - 1, 0))
            h_nb = jnp.where(row == 0, halo_ref[7:8, :] * jnp.where(ib > 0, 1.0, 0.0), pltpu.roll(h_ref[...], 1, 0))
        dh = _scan_block(coef, dhs_ref[...], cdh_ref[0:1, :], not reverse)
        tmp_ref[...] = dh
        cdh_ref[0:1, :] = tmp_ref[pl.ds(edge, 1), :]
        tmp_ref[...] = a
        ca_ref[0:1, :] = tmp_ref[pl.ds(edge, 1), :]

        du = jnp.where(_row_ids(ROW_BLOCK, ib) >= PAD_ROWS, dh, 0.0)
        da = dh * h_nb
        dm = du * (ig * xc)
        di = du * (m * xc)
        dlog_a = da * a - dm * (a * a) * inv_m
        dr = dlog_a * nsp
        dpre = jnp.concatenate([dr * r * (1.0 - r), di * ig * (1.0 - ig)], axis=1)
        dxc_ref[...] = du * (m * ig) + _dot_nt(dpre, wcat_ref[...])
        _accumulate(dw_ref, _dot_tn(xc, dpre), i == 0)
        _accumulate(db_ref, jnp.sum(dpre, axis=0, keepdims=True), i == 0)
        _accumulate(dlam_ref, jnp.sum(dlog_a * r, axis=0, keepdims=True), i == 0)

        @pl.when(i == nb - 1)
        def _():
            dlam_ref[...] = dlam_ref[...] * (LRU_C * _sigmoid(-lam_ref[...]))

    return _hosting_call(
        exchange, body, name="lru_scan_bwd_b" if reverse else "lru_scan_bwd_f", grid=(nb,),
        in_specs=[spec, spec, spec, halo, wcat.spec, bias.spec, lam.spec],
        out_specs=[spec, pl.BlockSpec((LRU_W, 2 * LRU_W), lambda i: (0, 0)),
                   pl.BlockSpec((1, 2 * LRU_W), lambda i: (0, 0)), pl.BlockSpec((1, LRU_W), lambda i: (0, 0))],
        out_shape=[_sds((n, LRU_W), F32), _sds((LRU_W, 2 * LRU_W), F32), _sds((1, 2 * LRU_W), F32), _sds((1, LRU_W), F32)],
        scratch_shapes=[pltpu.VMEM((8, LRU_W), F32), pltpu.VMEM((8, LRU_W), F32), pltpu.VMEM((ROW_BLOCK, LRU_W), F32)],
        compiler_params=_params(("arbitrary",)),
    )(dhs, xc, h, h, wcat.array, bias.array, lam.array)


def _gla_rows(n):
    return 768 if n % 768 == 0 else ROW_BLOCK


def _gla_masks(reverse):
    t = lax.broadcasted_iota(jnp.int32, (CHUNK, CHUNK), 0)
    s = lax.broadcasted_iota(jnp.int32, (CHUNK, CHUNK), 1)
    if reverse:
        return (s >= t).astype(F32), s > t
    return (s <= t).astype(F32), s <= t


def _gla_gate(zg, wg_ref, bg_ref):
    pre = _dot(zg, wg_ref[...]) + bg_ref[...]
    g = (jnp.minimum(pre, 0.0) - jnp.log(1.0 + jnp.exp(-jnp.abs(pre)))) * (1.0 / GATE_NORM)
    return pre, g


def _gla_decays(gc, tri):
    b = jnp.dot(tri, gc, precision=lax.Precision.HIGHEST, preferred_element_type=F32)
    b_last = jnp.sum(gc, axis=0, keepdims=True)
    return jnp.exp(b), jnp.exp(-b), jnp.exp(b_last - b), jnp.exp(b_last)


def _gla_scan(z, wg, bg, reverse, exchange=None):
    n = z.shape[0]
    rb = _gla_rows(n)
    nb = n // rb
    cpb = rb // CHUNK
    order = (lambda i: nb - 1 - i) if reverse else (lambda i: i)
    chunks = range(cpb - 1, -1, -1) if reverse else range(cpb)

    def body(qk_ref, v_ref, zg_ref, wg_ref, bg_ref, o_ref, sall_ref, s_ref):
        i = pl.program_id(0)

        @pl.when(i == 0)
        def _():
            s_ref[...] = jnp.zeros_like(s_ref)

        tri, mask = _gla_masks(reverse)
        _, g = _gla_gate(zg_ref[...], wg_ref, bg_ref)
        heads = range(GLA_HEADS)
        ks = [slice(hd * GLA_DK, (hd + 1) * GLA_DK) for hd in heads]
        vs = [slice(hd * GLA_DV, (hd + 1) * GLA_DV) for hd in heads]
        qh, kb, v, el, p, intra, kv = {}, {}, {}, {}, {}, {}, {}
        for c in chunks:
            rows = slice(c * CHUNK, (c + 1) * CHUNK)
            eb, enb, ebl, el[c] = _gla_decays(g[rows], tri)
            qk = qk_ref[rows, :]
            q_all = (qk[:, :GLA_QK] * (GLA_DK ** -0.5) * eb).astype(BF16)
            k_all = (qk[:, GLA_QK:] * enb).astype(BF16)
            kb_all = (qk[:, GLA_QK:] * ebl).astype(BF16)
            v_all = v_ref[rows, :].astype(BF16)
            for hd in heads:
                qh[c, hd], kb[c, hd], v[c, hd] = q_all[:, ks[hd]], kb_all[:, ks[hd]], v_all[:, vs[hd]]
                p[c, hd] = _dot_nt(qh[c, hd], k_all[:, ks[hd]])
        for c in chunks:
            for hd in heads:
                intra[c, hd] = _dot(jnp.where(mask, p[c, hd], 0.0), v[c, hd])
                kv[c, hd] = _dot_tn(v[c, hd], kb[c, hd])
        state = [s_ref[:, ks[hd]] for hd in heads]
        for c in chunks:
            rows = slice(c * CHUNK, (c + 1) * CHUNK)
            for hd in heads:
                sall_ref[c, :, ks[hd]] = state[hd]
                o_ref[rows, vs[hd]] = intra[c, hd] + _dot_nt(qh[c, hd], state[hd])
                state[hd] = state[hd] * el[c][:, ks[hd]] + kv[c, hd]
        for hd in heads:
            s_ref[:, ks[hd]] = state[hd]

    return _hosting_call(
        exchange, body, name="gla_scan_b" if reverse else "gla_scan_f", grid=(nb,),
        in_specs=[pl.BlockSpec((rb, 512), lambda i: (order(i), 2)), pl.BlockSpec((rb, 512), lambda i: (order(i), 3)),
                  pl.BlockSpec((rb, LANES), lambda i: (order(i), ZG_COL_BLOCK)), wg.spec, bg.spec],
        out_specs=[pl.BlockSpec((rb, GLA_W), lambda i: (order(i), 0)),
                   pl.BlockSpec((cpb, GLA_DV, GLA_QK), lambda i: (order(i), 0, 0))],
        out_shape=[_sds((n, GLA_W), F32), _sds((n // CHUNK, GLA_DV, GLA_QK), F32)],
        scratch_shapes=[pltpu.VMEM((GLA_DV, GLA_QK), F32)],
        compiler_params=_params(("arbitrary",)),
    )(z, z, z, wg.array, bg.array)


def _gla_scan_bwd(do, z, states, wg, bg, reverse, exchange=None):
    n = z.shape[0]
    rb = _gla_rows(n)
    nb = n // rb
    cpb = rb // CHUNK
    order = (lambda i: i) if reverse else (lambda i: nb - 1 - i)
    chunks = range(cpb) if reverse else range(cpb - 1, -1, -1)

    def body(do_ref, qk_ref, v_ref, zg_ref, sall_ref, wg_ref, bg_ref,
             dqk_ref, dv_ref, dzg_ref, dwg_ref, dbg_ref, ds_ref):
        i = pl.program_id(0)

        @pl.when(i == 0)
        def _():
            ds_ref[...] = jnp.zeros_like(ds_ref)

        tri, mask = _gla_masks(reverse)
        tri_t, _ = _gla_masks(not reverse)
        zg = zg_ref[...]
        pre, g = _gla_gate(zg, wg_ref, bg_ref)
        heads = range(GLA_HEADS)
        ks = [slice(hd * GLA_DK, (hd + 1) * GLA_DK) for hd in heads]
        vs = [slice(hd * GLA_DV, (hd + 1) * GLA_DV) for hd in heads]
        dec, full, qh, kh, kb, v, dout, p, dp = {}, {}, {}, {}, {}, {}, {}, {}, {}
        for c in chunks:
            rows = slice(c * CHUNK, (c + 1) * CHUNK)
            dec[c] = _gla_decays(g[rows], tri)
            eb, enb, ebl, _ = dec[c]
            qk = qk_ref[rows, :]
            q_f = qk[:, :GLA_QK] * (GLA_DK ** -0.5) * eb
            k_f = qk[:, GLA_QK:] * enb
            kb_f = qk[:, GLA_QK:] * ebl
            full[c] = (q_f, k_f, kb_f)
            q_all, k_all, kb_all = q_f.astype(BF16), k_f.astype(BF16), kb_f.astype(BF16)
            v_all, do_all = v_ref[rows, :].astype(BF16), do_ref[rows, :].astype(BF16)
            for hd in heads:
                qh[c, hd], kh[c, hd], kb[c, hd] = q_all[:, ks[hd]], k_all[:, ks[hd]], kb_all[:, ks[hd]]
                v[c, hd], dout[c, hd] = v_all[:, vs[hd]], do_all[:, vs[hd]]
                p[c, hd] = _dot_nt(qh[c, hd], kh[c, hd])
                dp[c, hd] = _dot_nt(dout[c, hd], v[c, hd])
        dv_i, dqh, dkh, dsq, state = {}, {}, {}, {}, {}
        for c in chunks:
            for hd in heads:
                pm = jnp.where(mask, p[c, hd], 0.0).astype(BF16)
                dpm = jnp.where(mask, dp[c, hd], 0.0).astype(BF16)
                state[c, hd] = sall_ref[c, :, ks[hd]]
                dv_i[c, hd] = _dot_tn(pm, dout[c, hd])
                dqh[c, hd] = _dot(dpm, kh[c, hd]) + _dot(dout[c, hd], state[c, hd])
                dkh[c, hd] = _dot_tn(dpm, qh[c, hd])
                dsq[c, hd] = _dot_tn(dout[c, hd], qh[c, hd])
        dstate = [ds_ref[:, ks[hd]] for hd in heads]
        dkb, sds = {}, {}
        for c in chunks:
            rows = slice(c * CHUNK, (c + 1) * CHUNK)
            el = dec[c][3]
            for hd in heads:
                dv_ref[rows, vs[hd]] = (dv_i[c, hd] + _dot_nt(kb[c, hd], dstate[hd])).astype(BF16)
                dkb[c, hd] = _dot(v[c, hd], dstate[hd])
                sds[c, hd] = jnp.sum(state[c, hd] * dstate[hd], axis=0, keepdims=True)
                dstate[hd] = dstate[hd] * el[:, ks[hd]] + dsq[c, hd]
        for hd in heads:
            ds_ref[:, ks[hd]] = dstate[hd]
        dgs = [None] * cpb
        for c in chunks:
            rows = slice(c * CHUNK, (c + 1) * CHUNK)
            eb, enb, ebl, el = dec[c]
            q_f, k_f, kb_f = full[c]
            dqh_c = jnp.concatenate([dqh[c, hd] for hd in heads], axis=1)
            dkh_c = jnp.concatenate([dkh[c, hd] for hd in heads], axis=1)
            dkb_c = jnp.concatenate([dkb[c, hd] for hd in heads], axis=1)
            sds_c = jnp.concatenate([sds[c, hd] for hd in heads], axis=1)
            dqk_ref[rows, :] = jnp.concatenate([dqh_c * eb * (GLA_DK ** -0.5), dkh_c * enb + dkb_c * ebl], axis=1).astype(BF16)
            dkb_kb = dkb_c * kb_f
            db = dqh_c * q_f - dkh_c * k_f - dkb_kb
            db_last = el * sds_c + jnp.sum(dkb_kb, axis=0, keepdims=True)
            dgs[c] = jnp.dot(tri_t, db, precision=lax.Precision.HIGHEST, preferred_element_type=F32) + db_last
        dg = jnp.concatenate(dgs, axis=0)
        dpre = dg * _sigmoid(-pre) * (1.0 / GATE_NORM)
        dzg_ref[...] = _dot_nt(dpre, wg_ref[...]).astype(BF16)
        _accumulate(dwg_ref, _dot_tn(zg, dpre), i == 0)
        _accumulate(dbg_ref, jnp.sum(dpre, axis=0, keepdims=True), i == 0)

    return _hosting_call(
        exchange, body, name="gla_scan_bwd_b" if reverse else "gla_scan_bwd_f", grid=(nb,),
        in_specs=[pl.BlockSpec((rb, GLA_W), lambda i: (order(i), 0)),
                  pl.BlockSpec((rb, 512), lambda i: (order(i), 2)), pl.BlockSpec((rb, 512), lambda i: (order(i), 3)),
                  pl.BlockSpec((rb, LANES), lambda i: (order(i), ZG_COL_BLOCK)),
                  pl.BlockSpec((cpb, GLA_DV, GLA_QK), lambda i: (order(i), 0, 0)), wg.spec, bg.spec],
        out_specs=[pl.BlockSpec((rb, 512), lambda i: (order(i), 0)), pl.BlockSpec((rb, 512), lambda i: (order(i), 0)),
                   pl.BlockSpec((rb, LANES), lambda i: (order(i), 0)),
                   pl.BlockSpec((LANES, GLA_QK), lambda i: (0, 0)), pl.BlockSpec((1, GLA_QK), lambda i: (0, 0))],
        out_shape=[_sds((n, 512), BF16), _sds((n, 512), BF16), _sds((n, LANES), BF16), _sds((LANES, GLA_QK), F32),
                   _sds((1, GLA_QK), F32)],
        scratch_shapes=[pltpu.VMEM((GLA_DV, GLA_QK), F32)],
        compiler_params=_params(("arbitrary",)),
    )(do, z, z, z, states, wg.array, bg.array)


NORM_NAMES = ("norm_mix_pre", "norm_mix_post", "norm_mlp_pre", "norm_mlp_post")
VEC512_NAMES = ("conv_b", "lru_ba_f", "lru_bx_f", "lru_lambda_f", "lru_ba_b", "lru_bx_b", "lru_lambda_b", "gla_head_norm")
VEC256_NAMES = ("gla_bg_f", "gla_bg_b")
LRU_MAT_NAMES = ("lru_wa_f", "lru_wx_f", "lru_wa_b", "lru_wx_b")
DIRS = ("f", "b")


def _prepare_params(w, gathered, depth):
    row_names = NORM_NAMES + ("conv_b", "gla_head_norm")
    ins = ([w[nm] for nm in row_names] + [w["lru_ba_" + d] for d in DIRS] + [w["lru_bx_" + d] for d in DIRS]
           + [w["lru_lambda_" + d] for d in DIRS] + [w["gla_bg_" + d] for d in DIRS]
           + [w["lru_wa_" + d].reshape(depth, LRU_W, LRU_HD) for d in DIRS]
           + [w["lru_wx_" + d].reshape(depth, LRU_W, LRU_HD) for d in DIRS]
           + [gathered["conv_w"], gathered["gla_wg_f"], gathered["gla_wg_b"], gathered["meta_tokens"]])
    n_rows = len(row_names)

    def body(*refs):
        rows_in = refs[:n_rows]
        ba, bx, lam, bg, wa, wx = (refs[n_rows + 2 * t:n_rows + 2 * t + 2] for t in range(6))
        convw_g, wgf_g, wgb_g, meta_g = refs[n_rows + 12:n_rows + 16]
        outs = refs[n_rows + 16:]
        rows_out = outs[:n_rows]
        convw, wcat, bias, lam_o, wg, bg_o, meta = outs[n_rows:]
        for l in range(depth):
            for src, dst in zip(rows_in, rows_out):
                dst[l] = src[pl.ds(l, 1), :]
            convw[l] = jnp.zeros((8, LRU_W), F32)
            for j in range(N_DEV):
                convw[l, 0:4, j * 64:(j + 1) * 64] = convw_g[j, l]
            for d in range(2):
                wcat[l, d] = jnp.zeros((LRU_W, 2 * LRU_W), BF16)
                for hd in range(LRU_HEADS):
                    rs = slice(hd * LRU_HD, (hd + 1) * LRU_HD)
                    wcat[l, d, rs, hd * LRU_HD:(hd + 1) * LRU_HD] = wa[d][l, rs, :].astype(BF16)
                    wcat[l, d, rs, LRU_W + hd * LRU_HD:LRU_W + (hd + 1) * LRU_HD] = wx[d][l, rs, :].astype(BF16)
                bias[l, d, :, 0:LRU_W] = ba[d][pl.ds(l, 1), :]
                bias[l, d, :, LRU_W:2 * LRU_W] = bx[d][pl.ds(l, 1), :]
                lam_o[l, d] = lam[d][pl.ds(l, 1), :]
                bg_o[l, d] = bg[d][pl.ds(l, 1), :]
                wg[l, d] = jnp.zeros((LANES, GLA_QK), BF16)
                src = wgf_g if d == 0 else wgb_g
                for j in range(N_DEV):
                    wg[l, d, d * GLA_RANK:(d + 1) * GLA_RANK, j * 32:(j + 1) * 32] = src[j, l].astype(BF16)
        for j in range(N_DEV):
            meta[:, j * LANES:(j + 1) * LANES] = meta_g[j]

    out_shape = ([_sds((depth, 1, w[nm].shape[1]), F32) for nm in row_names]
                 + [_sds((depth, 8, LRU_W), F32), _sds((depth, 2, LRU_W, 2 * LRU_W), BF16), _sds((depth, 2, 1, 2 * LRU_W), F32),
                    _sds((depth, 2, 1, LRU_W), F32), _sds((depth, 2, LANES, GLA_QK), BF16), _sds((depth, 2, 1, GLA_QK), F32),
                    _sds((N_META, D_MODEL), F32)])
    outs = pl.pallas_call(
        body, name="prepare_params", in_specs=[VMEM_SPEC] * len(ins), out_specs=[VMEM_SPEC] * len(out_shape),
        out_shape=out_shape, compiler_params=_params(None, 32),
    )(*ins)
    prepared = dict(zip(row_names, outs[:n_rows]))
    prepared.update(zip(("conv_w", "wcat", "lru_bias", "lru_lam", "wg", "gla_bg", "meta_tokens"), outs[n_rows:]))
    return prepared


class _Outbox:
    def __init__(self, on_complete):
        self.pending, self.on_complete = {}, on_complete

    def put(self, key, array, src, landing_shape):
        self.pending[key] = dict(array=array, src=src, landing=landing_shape, groups=list(range(len(PEER_GROUPS))))

    def exchange(self, wanted=None):
        ex, tickets = _Exchange(), []
        for key, item in self.pending.items():
            groups = [g for g in item["groups"] if wanted is None or (key[0], g) in wanted]
            out = None
            for g in groups:
                landing = item["landing"] if out is None else out
                out = ex.add(item["array"], item["src"], landing, _slab, peers=PEER_GROUPS[g], local=(g == 0))
                item["groups"].remove(g)
            if groups:
                tickets.append((key, out))
        return ex, tickets

    def store(self, tickets, landed):
        for key, out in tickets:
            item = self.pending[key]
            item["landing"] = landed[out]
            if not item["groups"]:
                del self.pending[key]
                self.on_complete(key, landed[out])


def _install_weight(p):
    def install(key, g):
        nm, l = key
        if nm == "w_in":
            g = jnp.pad(jnp.concatenate([g[j] for j in range(N_DEV)], axis=1), ((0, 0), (0, Z_W - D_IN)))
        elif nm == "w_out":
            g = g.reshape(D_MODEL, D_MODEL)
        elif nm == "w_mlp_down":
            g = g.reshape(D_FF, D_MODEL)
        p.setdefault(nm, {})[l] = g
    return install


def _request_weight(gather, shards, nm, l):
    gather.put((nm, l), shards[nm], _layer_of(l), _sds((N_DEV,) + shards[nm].shape[1:], BF16))


def _layer_fwd(h, l, p, gather, shards, depth):
    lp = lambda name, *index: _LayerParam(p[name], l, *index)
    s = dict(h=h)

    def hosted(fn, wanted, *args):
        ex, tickets = gather.exchange(wanted)
        outs = fn(*args, ex)
        own = len(outs) - len(ex.landings)
        gather.store(tickets, outs[own:])
        return outs[:own]

    _request_weight(gather, shards, "w_mlp_up", l)
    _request_weight(gather, shards, "w_mlp_down", l)
    s["hn"], s["z"] = hosted(_norm_in_proj, [("w_mlp_up", 0), ("w_out", 0)], h, lp("norm_mix_pre"), p["w_in"][l])
    s["xc"] = _conv_fwd(s["z"], lp("conv_w"), lp("conv_b"))
    plan = {"f": ([("w_mlp_up", 1), ("w_out", 1)], [("w_mlp_up", 2), ("w_out", 2)]),
            "b": ([("w_mlp_down", 0)], [("w_mlp_down", 1)])}
    for d, name in enumerate(DIRS):
        s["h_" + name], = hosted(_lru_scan, plan[name][0], s["xc"], lp("wcat", d), lp("lru_bias", d), lp("lru_lam", d), d == 1)
        s["o_" + name], s["s_" + name] = hosted(_gla_scan, plan[name][1], s["z"], lp("wg", d), lp("gla_bg", d), d == 1)
    s["ymix"] = _mix_epilogue(s["h_f"], s["h_b"], s["o_f"], s["o_b"], s["z"], lp("gla_head_norm"))
    s["mix"], s["h_mid"] = hosted(_out_proj, [("w_mlp_down", 2)], s["ymix"], p["w_out"][l], h, lp("norm_mix_post"))
    if l + 1 < depth:
        _request_weight(gather, shards, "w_in", l + 1)
        _request_weight(gather, shards, "w_out", l + 1)
    s["hn2"], s["up"], s["ff"], h_out = hosted(
        _mlp_fwd, None, s["h_mid"], lp("norm_mlp_pre"), p["w_mlp_up"][l], p["w_mlp_down"][l], lp("norm_mlp_post"))
    return h_out, s


def _layer_bwd(dh_out, l, p, s, outbox):
    lp = lambda name, *index: _LayerParam(p[name], l, *index)
    g = {}

    def hosted(fn, wanted, *args):
        ex, tickets = outbox.exchange(wanted)
        outs = fn(*args, ex)
        own = len(outs) - len(ex.landings)
        outbox.store(tickets, outs[own:])
        return outs[:own]

    d_ff, dup, dh_mid, g["norm_mlp_post"], g["norm_mlp_pre"] = hosted(
        _mlp_bwd, None, dh_out, s["ff"], s["up"], s["h_mid"], lp("norm_mlp_pre"), p["w_mlp_up"][l], p["w_mlp_down"][l],
        lp("norm_mlp_post"))
    _send_grad(outbox, "w_mlp_down", l, _matmul_tn(s["up"], d_ff, "grad_w_down", a_map=_relu_squared)
               .reshape(N_DEV, D_FF // N_DEV, D_MODEL))
    _send_grad(outbox, "w_mlp_up", l, _matmul_tn(s["hn2"], dup, "grad_w_up", column_slabs=True))
    dmix, dymix, g["norm_mix_post"] = hosted(_out_proj_bwd, [("w_mlp_down", 0)], dh_mid, s["mix"], lp("norm_mix_post"),
                                             p["w_out"][l])
    grad_w_out = _matmul_tn(s["ymix"], dmix, "grad_w_out").reshape(N_DEV, D_MODEL // N_DEV, D_MODEL)
    dhs, dgate, do, dgout, g["gla_head_norm"] = _mix_epilogue_bwd(
        dymix, s["h_f"], s["h_b"], s["o_f"], s["o_b"], s["z"], lp("gla_head_norm"))
    plan = {"f": ([("w_mlp_down", 1)], [("w_mlp_down", 2), ("w_mlp_up", 0)]), "b": ([("w_mlp_up", 1)], [("w_mlp_up", 2)])}
    dqk, dv, dzg, dxc = {}, {}, {}, {}
    for d, name in enumerate(DIRS):
        dqk[name], dv[name], dzg[name], g["wg_" + name], g["gla_bg_" + name] = hosted(
            _gla_scan_bwd, plan[name][0], do, s["z"], s["s_" + name], lp("wg", d), lp("gla_bg", d), d == 1)
        dxc[name], g["wcat_" + name], g["lru_bias_" + name], g["lru_lambda_" + name] = hosted(
            _lru_scan_bwd, plan[name][1], dhs, s["xc"], s["h_" + name], lp("wcat", d), lp("lru_bias", d), lp("lru_lam", d),
            d == 1)
    _send_grad(outbox, "w_out", l, grad_w_out)
    dxbr, g["conv_w"], g["conv_b"] = _conv_bwd(dxc["f"], dxc["b"], s["z"], lp("conv_w"))
    dz, dh_in, g["norm_mix_pre"] = hosted(
        _in_proj_bwd, [("w_out", group) for group in range(len(PEER_GROUPS))],
        (dxbr, dgate, dqk["f"], dqk["b"], dv["f"], dv["b"], dgout, dzg["f"], dzg["b"]),
        p["w_in"][l], s["h"], lp("norm_mix_pre"), dh_mid)
    return dh_in, g, dz


def _send_grad(outbox, nm, l, slabs):
    outbox.put((nm, l), slabs, _slab, _sds(slabs.shape, slabs.dtype))


def _w_in_slabs(grad_w_in):
    shard = D_IN // N_DEV
    return jnp.stack([grad_w_in[:, j * shard:(j + 1) * shard] for j in range(N_DEV)])


def _folded_block(hd):
    return slice((hd // 2) * LRU_HD, (hd // 2 + 1) * LRU_HD), slice((hd % 2) * LRU_HD, (hd % 2 + 1) * LRU_HD)


def _pack_small_grads(grads, dh0, depth):
    per_layer = ("norm_mix_pre", "norm_mix_post", "norm_mlp_pre", "norm_mlp_post", "conv_b", "gla_head_norm",
                 "lru_bias_f", "lru_bias_b", "lru_lambda_f", "lru_lambda_b", "gla_bg_f", "gla_bg_b",
                 "wcat_f", "wcat_b", "conv_w", "wg_f", "wg_b")
    ins = [grads[l][nm] for l in range(depth) for nm in per_layer]
    k = len(per_layer)
    meta_rows = PAD_ROWS // N_META

    def body(*refs):
        g = [dict(zip(per_layer, refs[l * k:(l + 1) * k])) for l in range(depth)]
        dh0_ref = refs[depth * k]
        norms, v512, v256, mats, convw, wgf, wgb, meta = refs[depth * k + 1:]
        v256[...] = jnp.zeros_like(v256)
        for l in range(depth):
            for p_, nm in enumerate(NORM_NAMES):
                norms[pl.ds(2 * p_ + l, 1), :] = g[l][nm][...]
            rows512 = [g[l]["conv_b"][...], g[l]["lru_bias_f"][:, 0:LRU_W], g[l]["lru_bias_f"][:, LRU_W:2 * LRU_W],
                       g[l]["lru_lambda_f"][...], g[l]["lru_bias_b"][:, 0:LRU_W], g[l]["lru_bias_b"][:, LRU_W:2 * LRU_W],
                       g[l]["lru_lambda_b"][...], g[l]["gla_head_norm"][...]]
            for p_, row in enumerate(rows512):
                v512[pl.ds(2 * p_ + l, 1), :] = row
            for p_, nm in enumerate(("gla_bg_f", "gla_bg_b")):
                v256[pl.ds(2 * p_ + l, 1), :] = g[l][nm][...]
            for d, name in enumerate(DIRS):
                for hd in range(LRU_HEADS):
                    rs = slice(hd * LRU_HD, (hd + 1) * LRU_HD)
                    dst_rows, dst_cols = _folded_block(hd)
                    mats[2 * d, l, dst_rows, dst_cols] = g[l]["wcat_" + name][rs, hd * LRU_HD:(hd + 1) * LRU_HD].astype(BF16)
                    mats[2 * d + 1, l, dst_rows, dst_cols] = (
                        g[l]["wcat_" + name][rs, LRU_W + hd * LRU_HD:LRU_W + (hd + 1) * LRU_HD].astype(BF16))
            for j in range(N_DEV):
                convw[j, l] = g[l]["conv_w"][0:4, j * 64:(j + 1) * 64]
                wgf[j, l] = g[l]["wg_f"][0:GLA_RANK, j * 32:(j + 1) * 32]
                wgb[j, l] = g[l]["wg_b"][GLA_RANK:2 * GLA_RANK, j * 32:(j + 1) * 32]
        for j in range(N_DEV):
            meta[j] = dh0_ref[:, j * LANES:(j + 1) * LANES]

    out_shape = [_sds((8, D_MODEL), F32), _sds((16, LRU_W), F32), _sds((8, GLA_QK), F32),
                 _sds((4, depth, LRU_W // 2, 2 * LRU_HD), BF16),
                 _sds((N_DEV, depth, 4, 64), F32), _sds((N_DEV, depth, GLA_RANK, 32), F32), _sds((N_DEV, depth, GLA_RANK, 32), F32),
                 _sds((N_DEV, N_META, LANES), F32)]
    return pl.pallas_call(
        body, name="pack_small_grads", grid=(1,),
        in_specs=[VMEM_SPEC] * (depth * k) + [pl.BlockSpec((N_META, D_MODEL), lambda i: (meta_rows, 0))],
        out_specs=[VMEM_SPEC] * len(out_shape), out_shape=out_shape, compiler_params=_params(("arbitrary",), 32),
    )(*ins, dh0)


def _my_index():
    return 4 * lax.axis_index("x") + 2 * lax.axis_index("y") + lax.axis_index("c")


def _peer(k):
    x, y, c = lax.axis_index("x"), lax.axis_index("y"), lax.axis_index("c")
    px = x ^ ((k >> 2) & 1)
    py = y ^ ((k >> 1) & 1)
    pc = c ^ (k & 1)
    return (px, py, pc), 4 * px + 2 * py + pc


ALL_PEERS = tuple(range(1, N_DEV))
PEER_GROUPS = ((1, 2, 3), (4, 5), (6, 7))


class _Exchange:
    def __init__(self):
        self.inputs, self.landings, self.transfers = [], [], []

    def add(self, array, src, landing, dst, peers=ALL_PEERS, local=True):
        if isinstance(landing, int):
            out = landing
        else:
            out = len(self.landings)
            self.landings.append(landing)
        self.transfers.append((len(self.inputs), src, out, dst, tuple(peers), local))
        self.inputs.append(array)
        return out

    def _pairs(self):
        return [(t, k) for t, tr in enumerate(self.transfers) for k in tr[4]]

    def _locals(self):
        return [t for t, tr in enumerate(self.transfers) if tr[5]]

    def out_shapes(self):
        return [g if isinstance(g, jax.ShapeDtypeStruct) else _sds(g.shape, g.dtype) for g in self.landings]

    def continued(self):
        return [(b, g) for b, g in enumerate(self.landings) if not isinstance(g, jax.ShapeDtypeStruct)]

    def sem_shapes(self):
        return [pltpu.SemaphoreType.DMA((max(len(self._pairs()), 1),)), pltpu.SemaphoreType.DMA((max(len(self._pairs()), 1),)),
                pltpu.SemaphoreType.DMA((max(len(self._locals()), 1),))]

    def _local(self, ins, outs, sems):
        me = _my_index()
        copies = []
        for s, t in enumerate(self._locals()):
            a, src, b, dst, _, _ = self.transfers[t]
            copies.append(pltpu.make_async_copy(src(ins[a], me), dst(outs[b], me), sems[2].at[s]))
        return copies

    def _remote(self, ins, outs, sems, sending):
        copies = []
        for s, (t, k) in enumerate(self._pairs()):
            a, src, b, dst, _, _ = self.transfers[t]
            peer, peer_index = _peer(k)
            copies.append(pltpu.make_async_remote_copy(
                src_ref=src(ins[a], peer_index), dst_ref=dst(outs[b], _my_index() if sending else peer_index),
                send_sem=sems[0].at[s], recv_sem=sems[1].at[s], device_id=peer, device_id_type=MESH_ID))
        return copies

    def start(self, ins, outs, sems):
        for cp in self._local(ins, outs, sems) + self._remote(ins, outs, sems, True):
            cp.start()

    def wait(self, ins, outs, sems):
        for cp in self._remote(ins, outs, sems, False):
            cp.wait_recv()
        for cp in self._remote(ins, outs, sems, True):
            cp.wait_send()
        for cp in self._local(ins, outs, sems):
            cp.wait()

    def run(self, name):
        return _hosting_call(self, None, name=name, grid=(), in_specs=[], out_specs=[], out_shape=[], scratch_shapes=[],
                             compiler_params=pltpu.CompilerParams(has_side_effects=True))()


def _hosting_call(exchange, body, *, name, grid, in_specs, out_specs, out_shape, scratch_shapes, compiler_params):
    if exchange is None or not exchange.transfers:
        return pl.pallas_call(body, name=name, grid=grid, in_specs=in_specs, out_specs=out_specs, out_shape=out_shape,
                              scratch_shapes=scratch_shapes, compiler_params=compiler_params)
    n_in, n_out, n_scr = len(in_specs), len(out_specs), len(scratch_shapes)
    x_in, x_out = len(exchange.inputs), len(exchange.landings)
    continued = exchange.continued()

    def hosted(*refs):
        ins, x_ins = refs[:n_in], refs[n_in:n_in + x_in]
        o0 = n_in + x_in + len(continued)
        outs, x_outs = refs[o0:o0 + n_out], refs[o0 + n_out:o0 + n_out + x_out]
        s0 = o0 + n_out + x_out
        scratch, sems = refs[s0:s0 + n_scr], refs[s0 + n_scr:]
        if body is None:
            exchange.start(x_ins, x_outs, sems)
            exchange.wait(x_ins, x_outs, sems)
            return
        ids = [pl.program_id(a) for a in range(len(grid))]
        first = functools.reduce(jnp.logical_and, [i == 0 for i in ids])
        last = functools.reduce(jnp.logical_and, [i == g - 1 for i, g in zip(ids, grid)])

        @pl.when(first)
        def _():
            exchange.start(x_ins, x_outs, sems)

        body(*ins, *outs, *scratch)

        @pl.when(last)
        def _():
            exchange.wait(x_ins, x_outs, sems)

    aliases = {n_in + x_in + i: n_out + b for i, (b, _) in enumerate(continued)}
    kwargs = dict(grid=grid) if grid else {}
    call = pl.pallas_call(
        hosted, name=name, in_specs=list(in_specs) + [ANY_SPEC] * (x_in + len(continued)),
        out_specs=list(out_specs) + [ANY_SPEC] * x_out, out_shape=list(out_shape) + exchange.out_shapes(),
        scratch_shapes=list(scratch_shapes) + exchange.sem_shapes(), compiler_params=compiler_params,
        input_output_aliases=aliases, **kwargs)
    return lambda *operands: call(*operands, *exchange.inputs, *[g for _, g in continued])


def _whole(ref, j):
    return ref


def _slab(ref, j):
    return ref.at[j]


def _layer_of(l):
    return lambda ref, j: ref.at[l]


def _adamw(g, w, m, v):
    nm = ADAM_B1 * m + (1.0 - ADAM_B1) * g
    nv = ADAM_B2 * v + (1.0 - ADAM_B2) * jnp.square(g)
    m_hat = nm / (1.0 - ADAM_B1 ** ADAM_STEP)
    v_hat = nv / (1.0 - ADAM_B2 ** ADAM_STEP)
    return -ADAM_LR * (m_hat / (jnp.sqrt(v_hat) + ADAM_EPS) + ADAM_WD * w), nm, nv


def _sum_parts(p_ref):
    g = p_ref[0].astype(F32)
    for j in range(1, N_DEV):
        g = g + p_ref[j].astype(F32)
    return g


def _adamw_sharded(parts, w, m, v, name):
    shape = w.shape
    lead, (rows, cols) = shape[:-2], shape[-2:]
    tr = min(rows, ROW_BLOCK)
    assert rows % tr == 0
    steps = rows // tr
    nl = len(lead)
    spec = pl.BlockSpec((None,) * nl + (tr, cols), lambda *idx: idx + (0,))
    per_layer = isinstance(parts, (list, tuple))
    if per_layer:
        def part_spec(l):
            return pl.BlockSpec((N_DEV, tr, cols), lambda li, r: (0, jnp.where(li == l, r, jnp.where(li < l, 0, steps - 1)), 0))
        part_specs = [part_spec(l) for l in range(len(parts))]
    else:
        parts = [parts]
        part_specs = [pl.BlockSpec((N_DEV,) + (None,) * nl + (tr, cols), lambda *idx: (0,) + idx + (0,))]
    count = len(parts)

    def body(*refs):
        p_refs = refs[:count]
        w_ref, m_ref, v_ref, g_ref, d_ref, nm_ref, nv_ref = refs[count:]

        def update(p_ref):
            g = _sum_parts(p_ref)
            g_ref[...] = g
            d_ref[...], nm_ref[...], nv_ref[...] = _adamw(g, w_ref[...], m_ref[...], v_ref[...])

        if per_layer:
            for l in range(count):
                pl.when(pl.program_id(0) == l)(functools.partial(update, p_refs[l]))
        else:
            update(p_refs[0])

    return pl.pallas_call(
        body, name=name, grid=lead + (steps,),
        in_specs=part_specs + [spec, spec, spec], out_specs=[spec] * 4, out_shape=[_sds(shape, F32)] * 4,
        compiler_params=_params(("arbitrary",) * (nl + 1)),
    )(*parts, w, m, v)


def _adamw_replicated(gathered, w, m, v, depth):
    names = NORM_NAMES + VEC512_NAMES + VEC256_NAMES + LRU_MAT_NAMES
    count = len(names)

    def body(*refs):
        norms, v512, v256, mats = refs[:4]
        w_refs, m_refs, v_refs = (refs[4 + t * count:4 + (t + 1) * count] for t in range(3))
        outs = refs[4 + 3 * count:4 + 7 * count]
        sum_norms, sum_512, sum_256, unfolded = refs[4 + 7 * count:]
        sum_norms[...] = _sum_parts(norms)
        sum_512[...] = _sum_parts(v512)
        sum_256[...] = _sum_parts(v256)
        for n_, nm in enumerate(names):
            if nm in NORM_NAMES:
                g = sum_norms[pl.ds(depth * NORM_NAMES.index(nm), depth), :]
            elif nm in VEC512_NAMES:
                g = sum_512[pl.ds(depth * VEC512_NAMES.index(nm), depth), :]
            elif nm in VEC256_NAMES:
                g = sum_256[pl.ds(depth * VEC256_NAMES.index(nm), depth), :]
            else:
                p_ = LRU_MAT_NAMES.index(nm)
                folded = mats[0, p_].astype(F32)
                for j in range(1, N_DEV):
                    folded = folded + mats[j, p_].astype(F32)
                for hd in range(LRU_HEADS):
                    src_rows, src_cols = _folded_block(hd)
                    unfolded[:, hd * LRU_HD:(hd + 1) * LRU_HD, :] = folded[:, src_rows, src_cols]
                g = unfolded[...]
            delta, nm_, nv_ = _adamw(g, w_refs[n_][...], m_refs[n_][...], v_refs[n_][...])
            outs[n_][...] = g
            outs[count + n_][...] = delta
            outs[2 * count + n_][...] = nm_
            outs[3 * count + n_][...] = nv_

    shapes = [_sds(w[nm].shape, F32) for nm in names]
    ins = list(gathered) + [t[nm] for t in (w, m, v) for nm in names]
    outs = pl.pallas_call(
        body, name="adamw_replicated", in_specs=[VMEM_SPEC] * len(ins), out_specs=[VMEM_SPEC] * (4 * count),
        out_shape=shapes * 4,
        scratch_shapes=[pltpu.VMEM(gathered[0].shape[1:], F32), pltpu.VMEM(gathered[1].shape[1:], F32),
                        pltpu.VMEM(gathered[2].shape[1:], F32), pltpu.VMEM((depth, LRU_W, LRU_HD), F32)],
        compiler_params=_params(None, 48),
    )(*ins)
    return [dict(zip(names, outs[t * count:(t + 1) * count])) for t in range(4)]


WEIGHT_NAMES = ("meta_tokens", "norm_mix_pre", "norm_mix_post", "norm_mlp_pre", "norm_mlp_post", "w_in", "conv_w", "conv_b",
                "lru_wa_f", "lru_ba_f", "lru_wx_f", "lru_bx_f", "lru_lambda_f", "lru_wa_b", "lru_ba_b", "lru_wx_b",
                "lru_bx_b", "lru_lambda_b", "gla_wg_f", "gla_bg_f", "gla_wg_b", "gla_bg_b", "gla_head_norm", "w_out",
                "w_mlp_up", "w_mlp_down")
MATMUL_WEIGHTS = ("w_in", "w_out", "w_mlp_up", "w_mlp_down")
SMALL_SHARDED = ("conv_w", "gla_wg_f", "gla_wg_b", "meta_tokens")


def kernel(x, meta_tokens, norm_mix_pre, norm_mix_post, norm_mlp_pre, norm_mlp_post, w_in, conv_w, conv_b, lru_wa_f, lru_ba_f, lru_wx_f, lru_bx_f, lru_lambda_f, lru_wa_b, lru_ba_b, lru_wx_b, lru_bx_b, lru_lambda_b, gla_wg_f, gla_bg_f, gla_wg_b, gla_bg_b, gla_head_norm, w_out, w_mlp_up, w_mlp_down, loss_target, m_meta_tokens, m_norm_mix_pre, m_norm_mix_post, m_norm_mlp_pre, m_norm_mlp_post, m_w_in, m_conv_w, m_conv_b, m_lru_wa_f, m_lru_ba_f, m_lru_wx_f, m_lru_bx_f, m_lru_lambda_f, m_lru_wa_b, m_lru_ba_b, m_lru_wx_b, m_lru_bx_b, m_lru_lambda_b, m_gla_wg_f, m_gla_bg_f, m_gla_wg_b, m_gla_bg_b, m_gla_head_norm, m_w_out, m_w_mlp_up, m_w_mlp_down, v_meta_tokens, v_norm_mix_pre, v_norm_mix_post, v_norm_mlp_pre, v_norm_mlp_post, v_w_in, v_conv_w, v_conv_b, v_lru_wa_f, v_lru_ba_f, v_lru_wx_f, v_lru_bx_f, v_lru_lambda_f, v_lru_wa_b, v_lru_ba_b, v_lru_wx_b, v_lru_bx_b, v_lru_lambda_b, v_gla_wg_f, v_gla_bg_f, v_gla_wg_b, v_gla_bg_b, v_gla_head_norm, v_w_out, v_w_mlp_up, v_w_mlp_down):
    args = locals()
    w = {nm: args[nm] for nm in WEIGHT_NAMES}
    m = {nm: args["m_" + nm] for nm in WEIGHT_NAMES}
    v = {nm: args["v_" + nm] for nm in WEIGHT_NAMES}
    depth = w_in.shape[0]

    shards = {nm: w[nm].astype(BF16) for nm in MATMUL_WEIGHTS}
    p = {}
    gather = _Outbox(_install_weight(p))
    _request_weight(gather, shards, "w_in", 0)
    _request_weight(gather, shards, "w_out", 0)
    ex, tickets = gather.exchange([("w_in", group) for group in range(len(PEER_GROUPS))])
    first_small = len(ex.landings)
    for nm in SMALL_SHARDED:
        ex.add(w[nm], _whole, _sds((N_DEV,) + w[nm].shape, F32), _slab)
    landed = ex.run("all_gather")
    gather.store(tickets, landed)
    p.update(_prepare_params(w, dict(zip(SMALL_SHARDED, landed[first_small:])), depth))

    h = jnp.concatenate([jnp.zeros((PAD_ROWS, D_MODEL), F32), p["meta_tokens"], x[0]], axis=0)
    saved = []
    for l in range(depth):
        h, s = _layer_fwd(h, l, p, gather, shards, depth)
        saved.append(s)
    dh, loss_part = _loss_and_grad(h, loss_target[0])
    loss = lax.psum(loss_part[0, 0], ("x", "y", "c"))

    received = {}
    outbox = _Outbox(received.__setitem__)
    grads = [None] * depth
    for l in reversed(range(depth)):
        dh, grads[l], dz = _layer_bwd(dh, l, p, saved[l], outbox)
        if l > 0:
            _send_grad(outbox, "w_in", l, _w_in_slabs(_matmul_tn(saved[l]["hn"], dz, "grad_w_in")))
    grad_x = dh[PAD_ROWS + N_META:][None]

    small = _pack_small_grads(grads, dh, depth)
    rep_bufs, small_slabs = small[:4], small[4:]
    ex, tickets = outbox.exchange()
    first_small = len(ex.landings)
    for g in small_slabs:
        ex.add(g, _slab, _sds(g.shape, F32), _slab)
    for g in rep_bufs:
        ex.add(g, _whole, _sds((N_DEV,) + g.shape, g.dtype), _slab)
    grad_w_in, *landed = _matmul_tn(saved[0]["hn"], dz, "grad_w_in", exchange=ex)
    outbox.store(tickets, landed)
    small_received = landed[first_small:first_small + len(small_slabs)]
    rep_received = landed[first_small + len(small_slabs):]
    _send_grad(outbox, "w_in", 0, _w_in_slabs(grad_w_in))
    ex, tickets = outbox.exchange()
    outbox.store(tickets, ex.run("exchange_grads"))

    results = [{}, {}, {}, {}]
    for nm in MATMUL_WEIGHTS:
        parts = [received[(nm, l)] for l in range(depth)]
        for t, out in enumerate(_adamw_sharded(parts, w[nm], m[nm], v[nm], "adamw_" + nm)):
            results[t][nm] = out
    for nm, parts in zip(SMALL_SHARDED, small_received):
        for t, out in enumerate(_adamw_sharded(parts, w[nm], m[nm], v[nm], "adamw_" + nm)):
            results[t][nm] = out

    def kernel_side(tree):
        return {nm: tree[nm].reshape(depth, LRU_W, LRU_HD) if nm in LRU_MAT_NAMES else tree[nm]
                for nm in NORM_NAMES + VEC512_NAMES + VEC256_NAMES + LRU_MAT_NAMES}

    for t, tree in enumerate(_adamw_replicated(rep_received, kernel_side(w), kernel_side(m), kernel_side(v), depth)):
        for nm, out in tree.items():
            results[t][nm] = out.reshape(w[nm].shape)
    return (loss, grad_x, *[results[t][nm] for t in range(4) for nm in WEIGHT_NAMES])
```

```python
import functools

import jax
import jax.numpy as jnp
from jax import lax
from jax.experimental import pallas as pl
from jax.experimental.pallas import tpu as pltpu

F32 = jnp.float32
BF16 = jnp.bfloat16

N_DEV = 8
D_MODEL = 1024
N_META = 16
ROW_BLOCK = 256
PAD_ROWS = ROW_BLOCK - N_META
CHUNK = 128
LRU_W = 512
LRU_HEADS = 8
LRU_HD = 64
LRU_C = 8.0
GLA_HEADS = 4
GLA_DK = 64
GLA_DV = 128
GLA_QK = GLA_HEADS * GLA_DK
GLA_W = GLA_HEADS * GLA_DV
GLA_RANK = 16
GATE_NORM = 16.0
D_FF = 4096
D_IN = 2592
Z_W = 2688
ZG_COL_BLOCK = 2560 // 128
EPS = 1e-6
LANES = 128

ADAM_LR = 0.001
ADAM_B1 = 0.9
ADAM_B2 = 0.999
ADAM_EPS = 1e-08
ADAM_WD = 0.01
ADAM_STEP = 10

VMEM_SPEC = pl.BlockSpec(memory_space=pltpu.VMEM)
ANY_SPEC = pl.BlockSpec(memory_space=pl.ANY)
MESH_ID = pl.DeviceIdType.MESH


def _sds(shape, dtype):
    return jax.ShapeDtypeStruct(shape, dtype)


def _params(sem=None, vmem_mb=None):
    kw = {}
    if sem is not None:
        kw["dimension_semantics"] = sem
    if vmem_mb is not None:
        kw["vmem_limit_bytes"] = vmem_mb * 2 ** 20
    return pltpu.CompilerParams(**kw)


def _row_tile(n, cap=768):
    for t in (768, 512, 384, 256):
        if t <= cap and n % t == 0:
            return t
    raise ValueError(n)


def _col_tile(k):
    for t in (1024, 896, 768, 640, 512, 384, 256, 128):
        if k % t == 0:
            return t
    raise ValueError(k)


def _sigmoid(x):
    return 0.5 * jnp.tanh(0.5 * x) + 0.5


def _gelu_and_grad(x):
    c = 0.7978845608028654
    inner = c * (x + 0.044715 * x * x * x)
    t = jnp.tanh(inner)
    gelu = 0.5 * x * (1.0 + t)
    dgelu = 0.5 * (1.0 + t) + 0.5 * x * (1.0 - t * t) * c * (1.0 + 3.0 * 0.044715 * x * x)
    return gelu, dgelu


def _one_minus_square(a, log_a):
    return jnp.tanh(-log_a) * (1.0 + a * a)


def _rms_fwd(x, g):
    rs = lax.rsqrt(jnp.mean(x * x, axis=-1, keepdims=True) + EPS)
    return x * rs * g


def _rms_bwd(x, g, dy):
    rs = lax.rsqrt(jnp.mean(x * x, axis=-1, keepdims=True) + EPS)
    xh = x * rs
    dyg = dy * g
    dx = rs * (dyg - xh * jnp.mean(dyg * xh, axis=-1, keepdims=True))
    return dx, jnp.sum(dy * xh, axis=0, keepdims=True)


def _dot(a, b):
    return jnp.dot(a.astype(BF16), b.astype(BF16), preferred_element_type=F32)


def _dot_nt(a, b):
    return lax.dot_general(a.astype(BF16), b.astype(BF16), (((1,), (1,)), ((), ())), preferred_element_type=F32)


def _dot_tn(a, b):
    return lax.dot_general(a.astype(BF16), b.astype(BF16), (((0,), (0,)), ((), ())), preferred_element_type=F32)


class _LayerParam:
    def __init__(self, array, *index):
        self.array = array
        self.index = index

    @property
    def spec(self):
        lead = len(self.index)
        tail = self.array.shape[lead:]
        index = self.index
        return pl.BlockSpec((None,) * lead + tail, lambda *_: index + (0,) * len(tail))


def _row_ids(rows, block_index):
    return block_index * rows + lax.broadcasted_iota(jnp.int32, (rows, 1), 0)


def _accumulate(ref, value, first):
    @pl.when(first)
    def _():
        ref[...] = value

    @pl.when(jnp.logical_not(first))
    def _():
        ref[...] += value


def _norm_in_proj(h, g, w, exchange=None):
    n, d = h.shape
    zw = w.shape[1]
    tr = _row_tile(n)

    def body(h_ref, g_ref, w_ref, hn_ref, z_ref):
        hn = _rms_fwd(h_ref[...], g_ref[...]).astype(BF16)
        hn_ref[...] = hn
        z_ref[...] = jnp.dot(hn, w_ref[...], preferred_element_type=F32)

    return _hosting_call(
        exchange, body, name="norm_in_proj", grid=(n // tr,),
        in_specs=[pl.BlockSpec((tr, d), lambda i: (i, 0)), g.spec, VMEM_SPEC],
        out_specs=[pl.BlockSpec((tr, d), lambda i: (i, 0)), pl.BlockSpec((tr, zw), lambda i: (i, 0))],
        out_shape=[_sds((n, d), BF16), _sds((n, zw), F32)],
        scratch_shapes=[], compiler_params=_params(("arbitrary",), 48),
    )(h, g.array, w)


def _halo_specs(width, nb, rows, col=0):
    per = rows // 8
    prev = pl.BlockSpec((8, width), lambda i: (jnp.maximum(i * per - 1, 0), col))
    nxt = pl.BlockSpec((8, width), lambda i: (jnp.minimum((i + 1) * per, nb * per - 1), col))
    return prev, nxt


def _shift_down(x, prev8, d):
    n = x.shape[0]
    r = pltpu.roll(x, d, 0)
    p = pltpu.roll(prev8, d, 0)
    row8 = lax.broadcasted_iota(jnp.int32, (8, 1), 0)
    head = jnp.where(row8 < d, p, r[0:8])
    return jnp.concatenate([head, r[8:]], axis=0)


def _shift_up(x, next8, d):
    n = x.shape[0]
    r = pltpu.roll(x, n - d, 0)
    q = pltpu.roll(next8, 8 - d, 0)
    row8 = lax.broadcasted_iota(jnp.int32, (8, 1), 0)
    tail = jnp.where(row8 >= 8 - d, q, r[n - 8:])
    return jnp.concatenate([r[:n - 8], tail], axis=0)


def _conv_fwd(z, conv_w, conv_b):
    n = z.shape[0]
    tr = _row_tile(n)
    nb = n // tr
    prev_spec, next_spec = _halo_specs(LRU_W, nb, tr)

    def body(cur_ref, prev_ref, next_ref, w_ref, b_ref, xc_ref):
        i = pl.program_id(0)
        cur = cur_ref[...]
        prev8 = prev_ref[...] * jnp.where(i > 0, 1.0, 0.0)
        next8 = next_ref[...] * jnp.where(i < nb - 1, 1.0, 0.0)
        w = [w_ref[pl.ds(k, 1), :] for k in range(4)]
        xc = (w[0] * _shift_down(cur, prev8, 2) + w[1] * _shift_down(cur, prev8, 1)
              + w[2] * cur + w[3] * _shift_up(cur, next8, 1) + b_ref[...])
        xc_ref[...] = xc

    return pl.pallas_call(
        body, name="conv_fwd", grid=(nb,),
        in_specs=[pl.BlockSpec((tr, LRU_W), lambda i: (i, 0)), prev_spec, next_spec, conv_w.spec, conv_b.spec],
        out_specs=pl.BlockSpec((tr, LRU_W), lambda i: (i, 0)),
        out_shape=_sds((n, LRU_W), F32),
        compiler_params=_params(("parallel",), 32),
    )(z, z, z, conv_w.array, conv_b.array)


def _conv_bwd(dxc_f, dxc_b, z, conv_w):
    n = z.shape[0]
    tr = _row_tile(n)
    nb = n // tr
    prev_spec, next_spec = _halo_specs(LRU_W, nb, tr)
    row_spec = pl.BlockSpec((tr, LRU_W), lambda i: (i, 0))

    def body(df_ref, dfp_ref, dfn_ref, db_ref, dbp_ref, dbn_ref, x_ref, xp_ref, xn_ref, w_ref,
             dx_ref, dw_ref, dbias_ref):
        i = pl.program_id(0)
        has_prev = jnp.where(i > 0, 1.0, 0.0)
        has_next = jnp.where(i < nb - 1, 1.0, 0.0)
        dxc = df_ref[...] + db_ref[...]
        dprev = (dfp_ref[...] + dbp_ref[...]) * has_prev
        dnext = (dfn_ref[...] + dbn_ref[...]) * has_next
        x = x_ref[...]
        xprev = xp_ref[...] * has_prev
        xnext = xn_ref[...] * has_next
        w = [w_ref[pl.ds(k, 1), :] for k in range(4)]
        dx_ref[...] = (w[0] * _shift_up(dxc, dnext, 2) + w[1] * _shift_up(dxc, dnext, 1)
                       + w[2] * dxc + w[3] * _shift_down(dxc, dprev, 1)).astype(BF16)
        dw = jnp.concatenate([
            jnp.sum(dxc * _shift_down(x, xprev, 2), axis=0, keepdims=True),
            jnp.sum(dxc * _shift_down(x, xprev, 1), axis=0, keepdims=True),
            jnp.sum(dxc * x, axis=0, keepdims=True),
            jnp.sum(dxc * _shift_up(x, xnext, 1), axis=0, keepdims=True),
            jnp.zeros((4, LRU_W), F32)], axis=0)
        _accumulate(dw_ref, dw, i == 0)
        _accumulate(dbias_ref, jnp.sum(dxc, axis=0, keepdims=True), i == 0)

    dx, dw, dbias = pl.pallas_call(
        body, name="conv_bwd", grid=(nb,),
        in_specs=[row_spec, prev_spec, next_spec, row_spec, prev_spec, next_spec, row_spec, prev_spec, next_spec,
                  conv_w.spec],
        out_specs=[row_spec, pl.BlockSpec((8, LRU_W), lambda i: (0, 0)), pl.BlockSpec((1, LRU_W), lambda i: (0, 0))],
        out_shape=[_sds((n, LRU_W), BF16), _sds((8, LRU_W), F32), _sds((1, LRU_W), F32)],
        compiler_params=_params(("arbitrary",), 40),
    )(dxc_f, dxc_f, dxc_f, dxc_b, dxc_b, dxc_b, z, z, z, conv_w.array)
    return dx, dw, dbias


def _mix_epilogue(h_f, h_b, o_f, o_b, z, head_norm):
    n = z.shape[0]
    tr = _row_tile(n)
    spec = pl.BlockSpec((tr, 512), lambda i: (i, 0))

    def body(hf_ref, hb_ref, of_ref, ob_ref, gate_ref, gout_ref, w_ref, y_ref):
        gelu, _ = _gelu_and_grad(gate_ref[...])
        y_ref[:, 0:LRU_W] = ((hf_ref[...] + hb_ref[...]) * gelu).astype(BF16)
        o = of_ref[...] + ob_ref[...]
        gout = gout_ref[...]
        silu = gout * _sigmoid(gout)
        w = w_ref[...]
        for hd in range(GLA_HEADS):
            cs = slice(hd * GLA_DV, (hd + 1) * GLA_DV)
            oh = o[:, cs]
            on = oh * lax.rsqrt(jnp.mean(oh * oh, axis=-1, keepdims=True) + EPS)
            y_ref[:, LRU_W + hd * GLA_DV:LRU_W + (hd + 1) * GLA_DV] = (on * w[:, cs] * silu[:, cs]).astype(BF16)

    return pl.pallas_call(
        body, name="mix_epilogue", grid=(n // tr,),
        in_specs=[spec, spec, spec, spec, pl.BlockSpec((tr, 512), lambda i: (i, 1)),
                  pl.BlockSpec((tr, 512), lambda i: (i, 4)), head_norm.spec],
        out_specs=pl.BlockSpec((tr, D_MODEL), lambda i: (i, 0)),
        out_shape=_sds((n, D_MODEL), BF16),
        compiler_params=_params(("parallel",), 40),
    )(h_f, h_b, o_f, o_b, z, z, head_norm.array)


def _mix_epilogue_bwd(dymix, h_f, h_b, o_f, o_b, z, head_norm):
    n = z.shape[0]
    tr = _row_tile(n)
    spec = pl.BlockSpec((tr, 512), lambda i: (i, 0))

    def body(dyl_ref, dyg_ref, hf_ref, hb_ref, of_ref, ob_ref, gate_ref, gout_ref, w_ref,
             dhs_ref, dgate_ref, do_ref, dgout_ref, dw_ref):
        i = pl.program_id(0)
        dyl = dyl_ref[...]
        gelu, dgelu = _gelu_and_grad(gate_ref[...])
        dhs_ref[...] = dyl * gelu
        dgate_ref[...] = (dyl * (hf_ref[...] + hb_ref[...]) * dgelu).astype(BF16)
        dyg = dyg_ref[...]
        o = of_ref[...] + ob_ref[...]
        gout = gout_ref[...]
        sg = _sigmoid(gout)
        silu = gout * sg
        dsilu = sg * (1.0 + gout * (1.0 - sg))
        w = w_ref[...]
        dws = []
        for hd in range(GLA_HEADS):
            cs = slice(hd * GLA_DV, (hd + 1) * GLA_DV)
            oh = o[:, cs]
            rs = lax.rsqrt(jnp.mean(oh * oh, axis=-1, keepdims=True) + EPS)
            on = oh * rs
            dy = dyg[:, cs]
            dgout_ref[:, cs] = (dy * on * w[:, cs] * dsilu[:, cs]).astype(BF16)
            dys = dy * silu[:, cs]
            dws.append(jnp.sum(dys * on, axis=0, keepdims=True))
            don = dys * w[:, cs]
            do_ref[:, cs] = (rs * (don - on * jnp.mean(don * on, axis=-1, keepdims=True))).astype(BF16)
        _accumulate(dw_ref, jnp.concatenate(dws, axis=1), i == 0)

    return pl.pallas_call(
        body, name="mix_epilogue_bwd", grid=(n // tr,),
        in_specs=[pl.BlockSpec((tr, 512), lambda i: (i, 0)), pl.BlockSpec((tr, 512), lambda i: (i, 1)),
                  spec, spec, spec, spec, pl.BlockSpec((tr, 512), lambda i: (i, 1)),
                  pl.BlockSpec((tr, 512), lambda i: (i, 4)), head_norm.spec],
        out_specs=[spec, spec, spec, spec, pl.BlockSpec((1, GLA_W), lambda i: (0, 0))],
        out_shape=[_sds((n, 512), F32)] + [_sds((n, 512), BF16)] * 3 + [_sds((1, GLA_W), F32)],
        compiler_params=_params(("arbitrary",), 48),
    )(dymix, dymix, h_f, h_b, o_f, o_b, z, z, head_norm.array)


def _out_proj(ymix, w_out, h, g, exchange=None):
    n, d = h.shape
    tr = _row_tile(n)
    spec = pl.BlockSpec((tr, d), lambda i: (i, 0))

    def body(y_ref, w_ref, h_ref, g_ref, mix_ref, hmid_ref):
        mix = jnp.dot(y_ref[...], w_ref[...], preferred_element_type=F32)
        mix_ref[...] = mix
        hmid_ref[...] = h_ref[...] + _rms_fwd(mix, g_ref[...])

    return _hosting_call(
        exchange, body, name="out_proj", grid=(n // tr,),
        in_specs=[spec, VMEM_SPEC, spec, g.spec],
        out_specs=[spec, spec],
        out_shape=[_sds((n, d), F32), _sds((n, d), F32)],
        scratch_shapes=[], compiler_params=_params(("arbitrary",), 44),
    )(ymix, w_out, h, g.array)


def _out_proj_bwd(dh_mid, mix, g, w_out, exchange=None):
    n, d = mix.shape
    tr = _row_tile(n)
    spec = pl.BlockSpec((tr, d), lambda i: (i, 0))

    def body(dh_ref, mix_ref, g_ref, w_ref, dmix_ref, dy_ref, dg_ref):
        i = pl.program_id(0)
        dmix, dg = _rms_bwd(mix_ref[...], g_ref[...], dh_ref[...])
        dmix = dmix.astype(BF16)
        dmix_ref[...] = dmix
        dy_ref[...] = _dot_nt(dmix, w_ref[...])
        _accumulate(dg_ref, dg, i == 0)

    return _hosting_call(
        exchange, body, name="out_proj_bwd", grid=(n // tr,),
        in_specs=[spec, spec, g.spec, VMEM_SPEC],
        out_specs=[spec, spec, pl.BlockSpec((1, d), lambda i: (0, 0))],
        out_shape=[_sds((n, d), BF16), _sds((n, d), F32), _sds((1, d), F32)],
        scratch_shapes=[], compiler_params=_params(("arbitrary",), 44),
    )(dh_mid, mix, g.array, w_out)


FF_SLAB = D_FF // N_DEV


def _relu_squared(up):
    return jnp.square(jnp.maximum(up.astype(F32), 0.0)).astype(BF16)


def _mlp_fwd(h_mid, g_pre, w_up, w_down, g_post, exchange=None):
    n, d = h_mid.shape
    tr = _row_tile(n, 384)
    spec = pl.BlockSpec((tr, d), lambda i: (i, 0))

    def body(h_ref, gpre_ref, wup_ref, wdn_ref, gpost_ref, hn_ref, up_ref, ff_ref, hout_ref):
        h = h_ref[...]
        hn = _rms_fwd(h, gpre_ref[...]).astype(BF16)
        hn_ref[...] = hn
        ff = jnp.zeros((tr, d), F32)
        for j in range(N_DEV):
            cs = slice(j * FF_SLAB, (j + 1) * FF_SLAB)
            up = jnp.dot(hn, wup_ref[j], preferred_element_type=F32).astype(BF16)
            up_ref[:, cs] = up
            ff = ff + jnp.dot(_relu_squared(up), wdn_ref[cs, :], preferred_element_type=F32)
        ff_ref[...] = ff
        hout_ref[...] = h + _rms_fwd(ff, gpost_ref[...])

    return _hosting_call(
        exchange, body, name="mlp_fwd", grid=(n // tr,),
        in_specs=[spec, g_pre.spec, VMEM_SPEC, VMEM_SPEC, g_post.spec],
        out_specs=[spec, pl.BlockSpec((tr, D_FF), lambda i: (i, 0)), spec, spec],
        out_shape=[_sds((n, d), BF16), _sds((n, D_FF), BF16), _sds((n, d), F32), _sds((n, d), F32)],
        scratch_shapes=[], compiler_params=_params(("arbitrary",), 52),
    )(h_mid, g_pre.array, w_up, w_down, g_post.array)


def _mlp_bwd(dh, ff, up, h_mid, g_pre, w_up, w_down, g_post, exchange=None):
    n, d = h_mid.shape
    tr = _row_tile(n, 384)
    spec = pl.BlockSpec((tr, d), lambda i: (i, 0))
    wide = pl.BlockSpec((tr, D_FF), lambda i: (i, 0))
    gspec = pl.BlockSpec((1, d), lambda i: (0, 0))

    def body(dh_ref, ff_ref, up_ref, h_ref, gpre_ref, wup_ref, wdn_ref, gpost_ref,
             dff_ref, dup_ref, dhmid_ref, dgpost_ref, dgpre_ref):
        i = pl.program_id(0)
        dh = dh_ref[...]
        dff, dgpost = _rms_bwd(ff_ref[...], gpost_ref[...], dh)
        dff = dff.astype(BF16)
        dff_ref[...] = dff
        dhn = jnp.zeros((tr, d), F32)
        for j in range(N_DEV):
            cs = slice(j * FF_SLAB, (j + 1) * FF_SLAB)
            relu = jnp.maximum(up_ref[:, cs].astype(F32), 0.0)
            dact = _dot_nt(dff, wdn_ref[cs, :])
            dup = (dact * 2.0 * relu).astype(BF16)
            dup_ref[:, cs] = dup
            dhn = dhn + _dot_nt(dup, wup_ref[j])
        dx, dgpre = _rms_bwd(h_ref[...], gpre_ref[...], dhn)
        dhmid_ref[...] = dh + dx
        _accumulate(dgpost_ref, dgpost, i == 0)
        _accumulate(dgpre_ref, dgpre, i == 0)

    return _hosting_call(
        exchange, body, name="mlp_bwd", grid=(n // tr,),
        in_specs=[spec, spec, wide, spec, g_pre.spec, VMEM_SPEC, VMEM_SPEC, g_post.spec],
        out_specs=[spec, wide, spec, gspec, gspec],
        out_shape=[_sds((n, d), BF16), _sds((n, D_FF), BF16), _sds((n, d), F32), _sds((1, d), F32), _sds((1, d), F32)],
        scratch_shapes=[], compiler_params=_params(("arbitrary",), 56),
    )(dh, ff, up, h_mid, g_pre.array, w_up, w_down, g_post.array)


def _in_proj_bwd(pieces, w_in, h, g, dh_mid, token_grad, exchange=None):
    dxbr, dgate, dqk_f, dqk_b, dv_f, dv_b, dgout, dzg_f, dzg_b = pieces
    n, d = h.shape
    tr = ROW_BLOCK if token_grad else _row_tile(n, 384)
    spec = pl.BlockSpec((tr, d), lambda i: (i, 0))
    s512 = pl.BlockSpec((tr, 512), lambda i: (i, 0))
    s128 = pl.BlockSpec((tr, LANES), lambda i: (i, 0))

    def body(a_ref, b_ref, cf_ref, cb_ref, df_ref, db_ref, e_ref, ff_ref, fb_ref, w_ref, h_ref, g_ref, dhm_ref,
             dz_ref, dh_ref, dg_ref, *gx_ref):
        i = pl.program_id(0)
        real = (_row_ids(tr, i) >= PAD_ROWS).astype(F32)
        f32 = lambda ref: ref[...].astype(F32)
        dz = jnp.concatenate([f32(a_ref), f32(b_ref), f32(cf_ref) + f32(cb_ref), f32(df_ref) + f32(db_ref),
                              f32(e_ref), f32(ff_ref) + f32(fb_ref)], axis=1) * real
        dz = dz.astype(BF16)
        dz_ref[...] = dz
        dhn = _dot_nt(dz, w_ref[...])
        dx, dg = _rms_bwd(h_ref[...], g_ref[...], dhn)
        dh = (dhm_ref[...] + dx) * real
        dh_ref[...] = dh
        if token_grad:
            gx_ref[0][...] = dh
        _accumulate(dg_ref, dg, i == 0)

    extra_specs = [pl.BlockSpec((tr, d), lambda i: (jnp.maximum(i - 1, 0), 0))] if token_grad else []
    extra_shapes = [_sds((n - ROW_BLOCK, d), F32)] if token_grad else []
    return _hosting_call(
        exchange, body, name="in_proj_bwd", grid=(n // tr,),
        in_specs=[s512, s512, s512, s512, s512, s512, s512, s128, s128, VMEM_SPEC, spec, g.spec, spec],
        out_specs=[pl.BlockSpec((tr, Z_W), lambda i: (i, 0)), spec, pl.BlockSpec((1, d), lambda i: (0, 0))] + extra_specs,
        out_shape=[_sds((n, Z_W), BF16), _sds((n, d), F32), _sds((1, d), F32)] + extra_shapes,
        scratch_shapes=[], compiler_params=_params(("arbitrary",), 48),
    )(dxbr, dgate, dqk_f, dqk_b, dv_f, dv_b, dgout, dzg_f, dzg_b, w_in, h, g.array, dh_mid)


def _matmul_tn(a, b, name, column_slabs=False, exchange=None, a_map=None):
    n, m = a.shape
    k = b.shape[1]
    tr = next(t for t in (2816, 1408, 768, 512, 256) if n % t == 0)
    tm, tk = _col_tile(m), _col_tile(k)
    steps = n // tr
    slab = k // N_DEV
    per_step = tk // slab if column_slabs else 1
    sub = next(t for t in (704, 768, 512, 256) if tr % t == 0)

    def body(a_ref, b_ref, o_ref, acc_ref, *mapped_ref):
        r = pl.program_id(2)
        if a_map is None:
            a_blk = a_ref[...]
        else:
            for c in range(tr // sub):
                rows = pl.ds(c * sub, sub)
                mapped_ref[0][rows, :] = a_map(a_ref[rows, :])
            a_blk = mapped_ref[0][...]
        _accumulate(acc_ref, _dot_tn(a_blk, b_ref[...]), r == 0)

        @pl.when(r == steps - 1)
        def _():
            if column_slabs:
                for j in range(per_step):
                    o_ref[j] = acc_ref[:, j * slab:(j + 1) * slab].astype(BF16)
            else:
                o_ref[...] = acc_ref[...].astype(BF16)

    if column_slabs:
        out_spec = pl.BlockSpec((per_step, tm, slab), lambda mi, ki, r: (ki, mi, 0))
        out_shape = _sds((N_DEV, m, slab), BF16)
    else:
        out_spec = pl.BlockSpec((tm, tk), lambda mi, ki, r: (mi, ki))
        out_shape = _sds((m, k), BF16)
    outs = _hosting_call(
        exchange, body, name=name, grid=(m // tm, k // tk, steps),
        in_specs=[pl.BlockSpec((tr, tm), lambda mi, ki, r: (r, mi)), pl.BlockSpec((tr, tk), lambda mi, ki, r: (r, ki))],
        out_specs=[out_spec], out_shape=[out_shape],
        scratch_shapes=[pltpu.VMEM((tm, tk), F32)] + ([] if a_map is None else [pltpu.VMEM((tr, tm), BF16)]),
        compiler_params=_params(("arbitrary", "arbitrary", "arbitrary"), 52),
    )(a, b)
    return outs[0] if exchange is None else outs


def _loss_and_grad(h_out, target):
    n, d = h_out.shape
    tr = ROW_BLOCK
    first = (PAD_ROWS + N_META) // tr

    def body(h_ref, t_ref, dh_ref, loss_ref):
        i = pl.program_id(0)
        real = jnp.where(i >= first, 1.0, 0.0)
        diff = (h_ref[...] - t_ref[...]) * real
        dh_ref[...] = diff * (1.0 / d)
        part = 0.5 * jnp.sum(jnp.mean(diff * diff, axis=-1, keepdims=True), axis=0, keepdims=True)
        _accumulate(loss_ref, jnp.broadcast_to(part, (1, LANES)), i == 0)

    return pl.pallas_call(
        body, name="loss_and_grad", grid=(n // tr,),
        in_specs=[pl.BlockSpec((tr, d), lambda i: (i, 0)), pl.BlockSpec((tr, d), lambda i: (jnp.maximum(i - first, 0), 0))],
        out_specs=[pl.BlockSpec((tr, d), lambda i: (i, 0)), pl.BlockSpec((1, LANES), lambda i: (0, 0))],
        out_shape=[_sds((n, d), F32), _sds((1, LANES), F32)],
        compiler_params=_params(("arbitrary",)),
    )(h_out, target)


def _scan_block(a, u, h_in, reverse):
    n = a.shape[0]
    row = lax.broadcasted_iota(jnp.int32, (n, 1), 0)
    d = 1
    while d < n:
        shift = n - d if reverse else d
        keep = (row < n - d) if reverse else (row >= d)
        a_s = pltpu.roll(a, shift, 0)
        u_s = pltpu.roll(u, shift, 0)
        u = jnp.where(keep, a * u_s + u, u)
        a = jnp.where(keep, a * a_s, a)
        d *= 2
    return a * h_in + u


def _lru_gates(xc, wcat_ref, bias_ref, lam_ref):
    nl = -lam_ref[...]
    nsp = -LRU_C * (jnp.maximum(nl, 0.0) + jnp.log(1.0 + jnp.exp(-jnp.abs(nl))))
    pre = _dot(xc, wcat_ref[...]) + bias_ref[...]
    r = _sigmoid(pre[:, :LRU_W])
    ig = _sigmoid(pre[:, LRU_W:])
    log_a = r * nsp
    a = jnp.exp(log_a)
    m2 = _one_minus_square(a, log_a)
    inv_m = lax.rsqrt(jnp.maximum(m2, 1e-30))
    return r, ig, a, m2 * inv_m, inv_m, nsp


def _lru_scan(xc, wcat, bias, lam, reverse, exchange=None):
    n = xc.shape[0]
    nb = n // ROW_BLOCK
    order = (lambda i: nb - 1 - i) if reverse else (lambda i: i)
    spec = pl.BlockSpec((ROW_BLOCK, LRU_W), lambda i: (order(i), 0))
    edge = 0 if reverse else ROW_BLOCK - 1

    def body(xc_ref, wcat_ref, bias_ref, lam_ref, h_ref, carry_ref):
        i = pl.program_id(0)

        @pl.when(i == 0)
        def _():
            carry_ref[...] = jnp.zeros_like(carry_ref)

        xc = xc_ref[...]
        r, ig, a, m, _, _ = _lru_gates(xc, wcat_ref, bias_ref, lam_ref)
        u = jnp.where(_row_ids(ROW_BLOCK, order(i)) >= PAD_ROWS, m * (ig * xc), 0.0)
        h_ref[...] = _scan_block(a, u, carry_ref[0:1, :], reverse)
        carry_ref[0:1, :] = h_ref[pl.ds(edge, 1), :]

    return _hosting_call(
        exchange, body, name="lru_scan_b" if reverse else "lru_scan_f", grid=(nb,),
        in_specs=[spec, wcat.spec, bias.spec, lam.spec],
        out_specs=[spec],
        out_shape=[_sds((n, LRU_W), F32)],
        scratch_shapes=[pltpu.VMEM((8, LRU_W), F32)],
        compiler_params=_params(("arbitrary",)),
    )(xc, wcat.array, bias.array, lam.array)


def _lru_scan_bwd(dhs, xc, h, wcat, bias, lam, reverse, exchange=None):
    n = xc.shape[0]
    nb = n // ROW_BLOCK
    per = ROW_BLOCK // 8
    order = (lambda i: i) if reverse else (lambda i: nb - 1 - i)
    spec = pl.BlockSpec((ROW_BLOCK, LRU_W), lambda i: (order(i), 0))
    if reverse:
        halo = pl.BlockSpec((8, LRU_W), lambda i: (jnp.minimum((order(i) + 1) * per, nb * per - 1), 0))
    else:
        halo = pl.BlockSpec((8, LRU_W), lambda i: (jnp.maximum(order(i) * per - 1, 0), 0))
    edge = ROW_BLOCK - 1 if reverse else 0

    def body(dhs_ref, xc_ref, h_ref, halo_ref, wcat_ref, bias_ref, lam_ref,
             dxc_ref, dw_ref, db_ref, dlam_ref, cdh_ref, ca_ref, tmp_ref):
        i = pl.program_id(0)
        ib = order(i)

        @pl.when(i == 0)
        def _():
            cdh_ref[...] = jnp.zeros_like(cdh_ref)
            ca_ref[...] = jnp.zeros_like(ca_ref)

        xc = xc_ref[...]
        r, ig, a, m, inv_m, nsp = _lru_gates(xc, wcat_ref, bias_ref, lam_ref)
        row = lax.broadcasted_iota(jnp.int32, (ROW_BLOCK, 1), 0)
        if reverse:
            coef = jnp.where(row == 0, ca_ref[0:1, :], pltpu.roll(a, 1, 0))
            h_nb = jnp.where(row == ROW_BLOCK - 1, halo_ref[0:1, :] * jnp.where(ib < nb - 1, 1.0, 0.0),
                             pltpu.roll(h_ref[...], ROW_BLOCK - 1, 0))
        else:
            coef = jnp.where(row == ROW_BLOCK - 1, ca_ref[0:1, :], pltpu.roll(a, ROW_BLOCK - 1, 0))
            h_nb = jnp.where(row == 0, halo_ref[7:8, :] * jnp.where(ib > 0, 1.0, 0.0), pltpu.roll(h_ref[...], 1, 0))
        dh = _scan_block(coef, dhs_ref[...], cdh_ref[0:1, :], not reverse)
        tmp_ref[...] = dh
        cdh_ref[0:1, :] = tmp_ref[pl.ds(edge, 1), :]
        tmp_ref[...] = a
        ca_ref[0:1, :] = tmp_ref[pl.ds(edge, 1), :]

        du = jnp.where(_row_ids(ROW_BLOCK, ib) >= PAD_ROWS, dh, 0.0)
        da = dh * h_nb
        dm = du * (ig * xc)
        di = du * (m * xc)
        dlog_a = da * a - dm * (a * a) * inv_m
        dr = dlog_a * nsp
        dpre = jnp.concatenate([dr * r * (1.0 - r), di * ig * (1.0 - ig)], axis=1)
        dxc_ref[...] = du * (m * ig) + _dot_nt(dpre, wcat_ref[...])
        _accumulate(dw_ref, _dot_tn(xc, dpre), i == 0)
        _accumulate(db_ref, jnp.sum(dpre, axis=0, keepdims=True), i == 0)
        _accumulate(dlam_ref, jnp.sum(dlog_a * r, axis=0, keepdims=True), i == 0)

        @pl.when(i == nb - 1)
        def _():
            dlam_ref[...] = dlam_ref[...] * (LRU_C * _sigmoid(-lam_ref[...]))

    return _hosting_call(
        exchange, body, name="lru_scan_bwd_b" if reverse else "lru_scan_bwd_f", grid=(nb,),
        in_specs=[spec, spec, spec, halo, wcat.spec, bias.spec, lam.spec],
        out_specs=[spec, pl.BlockSpec((LRU_W, 2 * LRU_W), lambda i: (0, 0)),
                   pl.BlockSpec((1, 2 * LRU_W), lambda i: (0, 0)), pl.BlockSpec((1, LRU_W), lambda i: (0, 0))],
        out_shape=[_sds((n, LRU_W), F32), _sds((LRU_W, 2 * LRU_W), F32), _sds((1, 2 * LRU_W), F32), _sds((1, LRU_W), F32)],
        scratch_shapes=[pltpu.VMEM((8, LRU_W), F32), pltpu.VMEM((8, LRU_W), F32), pltpu.VMEM((ROW_BLOCK, LRU_W), F32)],
        compiler_params=_params(("arbitrary",)),
    )(dhs, xc, h, h, wcat.array, bias.array, lam.array)


def _gla_rows(n):
    return 768 if n % 768 == 0 else ROW_BLOCK


def _gla_masks(reverse):
    t = lax.broadcasted_iota(jnp.int32, (CHUNK, CHUNK), 0)
    s = lax.broadcasted_iota(jnp.int32, (CHUNK, CHUNK), 1)
    if reverse:
        return (s >= t).astype(F32), s > t
    return (s <= t).astype(F32), s <= t


def _gla_gate(zg, wg_ref, bg_ref):
    pre = _dot(zg, wg_ref[...]) + bg_ref[...]
    g = (jnp.minimum(pre, 0.0) - jnp.log(1.0 + jnp.exp(-jnp.abs(pre)))) * (1.0 / GATE_NORM)
    return pre, g


def _gla_decays(gc, tri):
    b = jnp.dot(tri, gc, precision=lax.Precision.HIGHEST, preferred_element_type=F32)
    b_last = jnp.sum(gc, axis=0, keepdims=True)
    return jnp.exp(b), jnp.exp(-b), jnp.exp(b_last - b), jnp.exp(b_last)


def _gla_scan(z, wg, bg, reverse, exchange=None):
    n = z.shape[0]
    rb = _gla_rows(n)
    nb = n // rb
    cpb = rb // CHUNK
    order = (lambda i: nb - 1 - i) if reverse else (lambda i: i)
    chunks = range(cpb - 1, -1, -1) if reverse else range(cpb)

    def body(qk_ref, v_ref, zg_ref, wg_ref, bg_ref, o_ref, sall_ref, s_ref):
        i = pl.program_id(0)

        @pl.when(i == 0)
        def _():
            s_ref[...] = jnp.zeros_like(s_ref)

        tri, mask = _gla_masks(reverse)
        _, g = _gla_gate(zg_ref[...], wg_ref, bg_ref)
        heads = range(GLA_HEADS)
        ks = [slice(hd * GLA_DK, (hd + 1) * GLA_DK) for hd in heads]
        vs = [slice(hd * GLA_DV, (hd + 1) * GLA_DV) for hd in heads]
        qh, kb, v, el, p, intra, kv = {}, {}, {}, {}, {}, {}, {}
        for c in chunks:
            rows = slice(c * CHUNK, (c + 1) * CHUNK)
            eb, enb, ebl, el[c] = _gla_decays(g[rows], tri)
            qk = qk_ref[rows, :]
            q_all = (qk[:, :GLA_QK] * (GLA_DK ** -0.5) * eb).astype(BF16)
            k_all = (qk[:, GLA_QK:] * enb).astype(BF16)
            kb_all = (qk[:, GLA_QK:] * ebl).astype(BF16)
            v_all = v_ref[rows, :].astype(BF16)
            for hd in heads:
                qh[c, hd], kb[c, hd], v[c, hd] = q_all[:, ks[hd]], kb_all[:, ks[hd]], v_all[:, vs[hd]]
                p[c, hd] = _dot_nt(qh[c, hd], k_all[:, ks[hd]])
        for c in chunks:
            for hd in heads:
                intra[c, hd] = _dot(jnp.where(mask, p[c, hd], 0.0), v[c, hd])
                kv[c, hd] = _dot_tn(v[c, hd], kb[c, hd])
        state = [s_ref[:, ks[hd]] for hd in heads]
        for c in chunks:
            rows = slice(c * CHUNK, (c + 1) * CHUNK)
            for hd in heads:
                sall_ref[c, :, ks[hd]] = state[hd]
                o_ref[rows, vs[hd]] = intra[c, hd] + _dot_nt(qh[c, hd], state[hd])
                state[hd] = state[hd] * el[c][:, ks[hd]] + kv[c, hd]
        for hd in heads:
            s_ref[:, ks[hd]] = state[hd]

    return _hosting_call(
        exchange, body, name="gla_scan_b" if reverse else "gla_scan_f", grid=(nb,),
        in_specs=[pl.BlockSpec((rb, 512), lambda i: (order(i), 2)), pl.BlockSpec((rb, 512), lambda i: (order(i), 3)),
                  pl.BlockSpec((rb, LANES), lambda i: (order(i), ZG_COL_BLOCK)), wg.spec, bg.spec],
        out_specs=[pl.BlockSpec((rb, GLA_W), lambda i: (order(i), 0)),
                   pl.BlockSpec((cpb, GLA_DV, GLA_QK), lambda i: (order(i), 0, 0))],
        out_shape=[_sds((n, GLA_W), F32), _sds((n // CHUNK, GLA_DV, GLA_QK), F32)],
        scratch_shapes=[pltpu.VMEM((GLA_DV, GLA_QK), F32)],
        compiler_params=_params(("arbitrary",)),
    )(z, z, z, wg.array, bg.array)


def _gla_scan_bwd(do, z, states, wg, bg, reverse, exchange=None):
    n = z.shape[0]
    rb = _gla_rows(n)
    nb = n // rb
    cpb = rb // CHUNK
    order = (lambda i: i) if reverse else (lambda i: nb - 1 - i)
    chunks = range(cpb) if reverse else range(cpb - 1, -1, -1)

    def body(do_ref, qk_ref, v_ref, zg_ref, sall_ref, wg_ref, bg_ref,
             dqk_ref, dv_ref, dzg_ref, dwg_ref, dbg_ref, ds_ref):
        i = pl.program_id(0)

        @pl.when(i == 0)
        def _():
            ds_ref[...] = jnp.zeros_like(ds_ref)

        tri, mask = _gla_masks(reverse)
        tri_t, _ = _gla_masks(not reverse)
        zg = zg_ref[...]
        pre, g = _gla_gate(zg, wg_ref, bg_ref)
        heads = range(GLA_HEADS)
        ks = [slice(hd * GLA_DK, (hd + 1) * GLA_DK) for hd in heads]
        vs = [slice(hd * GLA_DV, (hd + 1) * GLA_DV) for hd in heads]
        dec, full, qh, kh, kb, v, dout, p, dp = {}, {}, {}, {}, {}, {}, {}, {}, {}
        for c in chunks:
            rows = slice(c * CHUNK, (c + 1) * CHUNK)
            dec[c] = _gla_decays(g[rows], tri)
            eb, enb, ebl, _ = dec[c]
            qk = qk_ref[rows, :]
            q_f = qk[:, :GLA_QK] * (GLA_DK ** -0.5) * eb
            k_f = qk[:, GLA_QK:] * enb
            kb_f = qk[:, GLA_QK:] * ebl
            full[c] = (q_f, k_f, kb_f)
            q_all, k_all, kb_all = q_f.astype(BF16), k_f.astype(BF16), kb_f.astype(BF16)
            v_all, do_all = v_ref[rows, :].astype(BF16), do_ref[rows, :].astype(BF16)
            for hd in heads:
                qh[c, hd], kh[c, hd], kb[c, hd] = q_all[:, ks[hd]], k_all[:, ks[hd]], kb_all[:, ks[hd]]
                v[c, hd], dout[c, hd] = v_all[:, vs[hd]], do_all[:, vs[hd]]
                p[c, hd] = _dot_nt(qh[c, hd], kh[c, hd])
                dp[c, hd] = _dot_nt(dout[c, hd], v[c, hd])
        dv_i, dqh, dkh, dsq, state = {}, {}, {}, {}, {}
        for c in chunks:
            for hd in heads:
                pm = jnp.where(mask, p[c, hd], 0.0).astype(BF16)
                dpm = jnp.where(mask, dp[c, hd], 0.0).astype(BF16)
                state[c, hd] = sall_ref[c, :, ks[hd]]
                dv_i[c, hd] = _dot_tn(pm, dout[c, hd])
                dqh[c, hd] = _dot(dpm, kh[c, hd]) + _dot(dout[c, hd], state[c, hd])
                dkh[c, hd] = _dot_tn(dpm, qh[c, hd])
                dsq[c, hd] = _dot_tn(dout[c, hd], qh[c, hd])
        dstate = [ds_ref[:, ks[hd]] for hd in heads]
        dkb, sds = {}, {}
        for c in chunks:
            rows = slice(c * CHUNK, (c + 1) * CHUNK)
            el = dec[c][3]
            for hd in heads:
                dv_ref[rows, vs[hd]] = (dv_i[c, hd] + _dot_nt(kb[c, hd], dstate[hd])).astype(BF16)
                dkb[c, hd] = _dot(v[c, hd], dstate[hd])
                sds[c, hd] = jnp.sum(state[c, hd] * dstate[hd], axis=0, keepdims=True)
                dstate[hd] = dstate[hd] * el[:, ks[hd]] + dsq[c, hd]
        for hd in heads:
            ds_ref[:, ks[hd]] = dstate[hd]
        dgs = [None] * cpb
        for c in chunks:
            rows = slice(c * CHUNK, (c + 1) * CHUNK)
            eb, enb, ebl, el = dec[c]
            q_f, k_f, kb_f = full[c]
            dqh_c = jnp.concatenate([dqh[c, hd] for hd in heads], axis=1)
            dkh_c = jnp.concatenate([dkh[c, hd] for hd in heads], axis=1)
            dkb_c = jnp.concatenate([dkb[c, hd] for hd in heads], axis=1)
            sds_c = jnp.concatenate([sds[c, hd] for hd in heads], axis=1)
            dqk_ref[rows, :] = jnp.concatenate([dqh_c * eb * (GLA_DK ** -0.5), dkh_c * enb + dkb_c * ebl], axis=1).astype(BF16)
            dkb_kb = dkb_c * kb_f
            db = dqh_c * q_f - dkh_c * k_f - dkb_kb
            db_last = el * sds_c + jnp.sum(dkb_kb, axis=0, keepdims=True)
            dgs[c] = jnp.dot(tri_t, db, precision=lax.Precision.HIGHEST, preferred_element_type=F32) + db_last
        dg = jnp.concatenate(dgs, axis=0)
        dpre = dg * _sigmoid(-pre) * (1.0 / GATE_NORM)
        dzg_ref[...] = _dot_nt(dpre, wg_ref[...]).astype(BF16)
        _accumulate(dwg_ref, _dot_tn(zg, dpre), i == 0)
        _accumulate(dbg_ref, jnp.sum(dpre, axis=0, keepdims=True), i == 0)

    return _hosting_call(
        exchange, body, name="gla_scan_bwd_b" if reverse else "gla_scan_bwd_f", grid=(nb,),
        in_specs=[pl.BlockSpec((rb, GLA_W), lambda i: (order(i), 0)),
                  pl.BlockSpec((rb, 512), lambda i: (order(i), 2)), pl.BlockSpec((rb, 512), lambda i: (order(i), 3)),
                  pl.BlockSpec((rb, LANES), lambda i: (order(i), ZG_COL_BLOCK)),
                  pl.BlockSpec((cpb, GLA_DV, GLA_QK), lambda i: (order(i), 0, 0)), wg.spec, bg.spec],
        out_specs=[pl.BlockSpec((rb, 512), lambda i: (order(i), 0)), pl.BlockSpec((rb, 512), lambda i: (order(i), 0)),
                   pl.BlockSpec((rb, LANES), lambda i: (order(i), 0)),
                   pl.BlockSpec((LANES, GLA_QK), lambda i: (0, 0)), pl.BlockSpec((1, GLA_QK), lambda i: (0, 0))],
        out_shape=[_sds((n, 512), BF16), _sds((n, 512), BF16), _sds((n, LANES), BF16), _sds((LANES, GLA_QK), F32),
                   _sds((1, GLA_QK), F32)],
        scratch_shapes=[pltpu.VMEM((GLA_DV, GLA_QK), F32)],
        compiler_params=_params(("arbitrary",)),
    )(do, z, z, z, states, wg.array, bg.array)


NORM_NAMES = ("norm_mix_pre", "norm_mix_post", "norm_mlp_pre", "norm_mlp_post")
VEC512_NAMES = ("conv_b", "lru_ba_f", "lru_bx_f", "lru_lambda_f", "lru_ba_b", "lru_bx_b", "lru_lambda_b", "gla_head_norm")
VEC256_NAMES = ("gla_bg_f", "gla_bg_b")
LRU_MAT_NAMES = ("lru_wa_f", "lru_wx_f", "lru_wa_b", "lru_wx_b")
DIRS = ("f", "b")


def _prepare_params(w, gathered, depth):
    row_names = NORM_NAMES + ("conv_b", "gla_head_norm")
    ins = ([w[nm] for nm in row_names] + [w["lru_ba_" + d] for d in DIRS] + [w["lru_bx_" + d] for d in DIRS]
           + [w["lru_lambda_" + d] for d in DIRS] + [w["gla_bg_" + d] for d in DIRS]
           + [w["lru_wa_" + d].reshape(depth, LRU_W, LRU_HD) for d in DIRS]
           + [w["lru_wx_" + d].reshape(depth, LRU_W, LRU_HD) for d in DIRS]
           + [gathered["conv_w"], gathered["gla_wg_f"], gathered["gla_wg_b"], gathered["meta_tokens"]])
    n_rows = len(row_names)

    def body(*refs):
        rows_in = refs[:n_rows]
        ba, bx, lam, bg, wa, wx = (refs[n_rows + 2 * t:n_rows + 2 * t + 2] for t in range(6))
        convw_g, wgf_g, wgb_g, meta_g = refs[n_rows + 12:n_rows + 16]
        outs = refs[n_rows + 16:]
        rows_out = outs[:n_rows]
        convw, wcat, bias, lam_o, wg, bg_o, meta = outs[n_rows:]
        for l in range(depth):
            for src, dst in zip(rows_in, rows_out):
                dst[l] = src[pl.ds(l, 1), :]
            convw[l] = jnp.zeros((8, LRU_W), F32)
            for j in range(N_DEV):
                convw[l, 0:4, j * 64:(j + 1) * 64] = convw_g[j, l]
            for d in range(2):
                wcat[l, d] = jnp.zeros((LRU_W, 2 * LRU_W), BF16)
                for hd in range(LRU_HEADS):
                    rs = slice(hd * LRU_HD, (hd + 1) * LRU_HD)
                    wcat[l, d, rs, hd * LRU_HD:(hd + 1) * LRU_HD] = wa[d][l, rs, :].astype(BF16)
                    wcat[l, d, rs, LRU_W + hd * LRU_HD:LRU_W + (hd + 1) * LRU_HD] = wx[d][l, rs, :].astype(BF16)
                bias[l, d, :, 0:LRU_W] = ba[d][pl.ds(l, 1), :]
                bias[l, d, :, LRU_W:2 * LRU_W] = bx[d][pl.ds(l, 1), :]
                lam_o[l, d] = lam[d][pl.ds(l, 1), :]
                bg_o[l, d] = bg[d][pl.ds(l, 1), :]
                wg[l, d] = jnp.zeros((LANES, GLA_QK), BF16)
                src = wgf_g if d == 0 else wgb_g
                for j in range(N_DEV):
                    wg[l, d, d * GLA_RANK:(d + 1) * GLA_RANK, j * 32:(j + 1) * 32] = src[j, l].astype(BF16)
        for j in range(N_DEV):
            meta[:, j * LANES:(j + 1) * LANES] = meta_g[j]

    out_shape = ([_sds((depth, 1, w[nm].shape[1]), F32) for nm in row_names]
                 + [_sds((depth, 8, LRU_W), F32), _sds((depth, 2, LRU_W, 2 * LRU_W), BF16), _sds((depth, 2, 1, 2 * LRU_W), F32),
                    _sds((depth, 2, 1, LRU_W), F32), _sds((depth, 2, LANES, GLA_QK), BF16), _sds((depth, 2, 1, GLA_QK), F32),
                    _sds((N_META, D_MODEL), F32)])
    outs = pl.pallas_call(
        body, name="prepare_params", in_specs=[VMEM_SPEC] * len(ins), out_specs=[VMEM_SPEC] * len(out_shape),
        out_shape=out_shape, compiler_params=_params(None, 32),
    )(*ins)
    prepared = dict(zip(row_names, outs[:n_rows]))
    prepared.update(zip(("conv_w", "wcat", "lru_bias", "lru_lam", "wg", "gla_bg", "meta_tokens"), outs[n_rows:]))
    return prepared


class _Outbox:
    def __init__(self, on_complete):
        self.pending, self.on_complete = {}, on_complete

    def put(self, key, array, src, landing_shape):
        self.pending[key] = dict(array=array, src=src, landing=landing_shape, groups=list(range(len(PEER_GROUPS))))

    def exchange(self, wanted=None):
        ex, tickets = _Exchange(), []
        for key, item in self.pending.items():
            groups = [g for g in item["groups"] if wanted is None or (key[0], g) in wanted]
            out = None
            for g in groups:
                landing = item["landing"] if out is None else out
                out = ex.add(item["array"], item["src"], landing, _slab, peers=PEER_GROUPS[g], local=(g == 0))
                item["groups"].remove(g)
            if groups:
                tickets.append((key, out))
        return ex, tickets

    def store(self, tickets, landed):
        for key, out in tickets:
            item = self.pending[key]
            item["landing"] = landed[out]
            if not item["groups"]:
                del self.pending[key]
                self.on_complete(key, landed[out])


def _install_weight(p):
    def install(key, g):
        nm, l = key
        if nm == "w_in":
            g = jnp.pad(jnp.concatenate([g[j] for j in range(N_DEV)], axis=1), ((0, 0), (0, Z_W - D_IN)))
        elif nm == "w_out":
            g = g.reshape(D_MODEL, D_MODEL)
        elif nm == "w_mlp_down":
            g = g.reshape(D_FF, D_MODEL)
        p.setdefault(nm, {})[l] = g
    return install


def _request_weight(gather, shards, nm, l):
    gather.put((nm, l), shards[nm], _layer_of(l), _sds((N_DEV,) + shards[nm].shape[1:], BF16))


def _layer_fwd(h, l, p, gather, shards, depth):
    lp = lambda name, *index: _LayerParam(p[name], l, *index)
    s = dict(h=h)

    def hosted(fn, wanted, *args):
        ex, tickets = gather.exchange(wanted)
        outs = fn(*args, ex)
        own = len(outs) - len(ex.landings)
        gather.store(tickets, outs[own:])
        return outs[:own]

    _request_weight(gather, shards, "w_mlp_up", l)
    _request_weight(gather, shards, "w_mlp_down", l)
    s["hn"], s["z"] = hosted(_norm_in_proj, [("w_mlp_up", 0), ("w_out", 0)], h, lp("norm_mix_pre"), p["w_in"][l])
    s["xc"] = _conv_fwd(s["z"], lp("conv_w"), lp("conv_b"))
    plan = {"f": ([("w_mlp_up", 1), ("w_out", 1)], [("w_mlp_up", 2), ("w_out", 2)]),
            "b": ([("w_mlp_down", 0)], [("w_mlp_down", 1)])}
    for d, name in enumerate(DIRS):
        s["h_" + name], = hosted(_lru_scan, plan[name][0], s["xc"], lp("wcat", d), lp("lru_bias", d), lp("lru_lam", d), d == 1)
        s["o_" + name], s["s_" + name] = hosted(_gla_scan, plan[name][1], s["z"], lp("wg", d), lp("gla_bg", d), d == 1)
    s["ymix"] = _mix_epilogue(s["h_f"], s["h_b"], s["o_f"], s["o_b"], s["z"], lp("gla_head_norm"))
    s["mix"], s["h_mid"] = hosted(_out_proj, [("w_mlp_down", 2)], s["ymix"], p["w_out"][l], h, lp("norm_mix_post"))
    if l + 1 < depth:
        _request_weight(gather, shards, "w_in", l + 1)
        _request_weight(gather, shards, "w_out", l + 1)
    s["hn2"], s["up"], s["ff"], h_out = hosted(
        _mlp_fwd, None, s["h_mid"], lp("norm_mlp_pre"), p["w_mlp_up"][l], p["w_mlp_down"][l], lp("norm_mlp_post"))
    return h_out, s


def _layer_bwd(dh_out, l, p, s, outbox):
    lp = lambda name, *index: _LayerParam(p[name], l, *index)
    g = {}

    def hosted(fn, wanted, *args):
        ex, tickets = outbox.exchange(wanted)
        outs = fn(*args, ex)
        own = len(outs) - len(ex.landings)
        outbox.store(tickets, outs[own:])
        return outs[:own]

    d_ff, dup, dh_mid, g["norm_mlp_post"], g["norm_mlp_pre"] = hosted(
        _mlp_bwd, None, dh_out, s["ff"], s["up"], s["h_mid"], lp("norm_mlp_pre"), p["w_mlp_up"][l], p["w_mlp_down"][l],
        lp("norm_mlp_post"))
    _send_grad(outbox, "w_mlp_down", l, _matmul_tn(s["up"], d_ff, "grad_w_down", a_map=_relu_squared)
               .reshape(N_DEV, D_FF // N_DEV, D_MODEL))
    _send_grad(outbox, "w_mlp_up", l, _matmul_tn(s["hn2"], dup, "grad_w_up", column_slabs=True))
    dmix, dymix, g["norm_mix_post"] = hosted(_out_proj_bwd, [("w_mlp_down", 0)], dh_mid, s["mix"], lp("norm_mix_post"),
                                             p["w_out"][l])
    grad_w_out = _matmul_tn(s["ymix"], dmix, "grad_w_out").reshape(N_DEV, D_MODEL // N_DEV, D_MODEL)
    dhs, dgate, do, dgout, g["gla_head_norm"] = _mix_epilogue_bwd(
        dymix, s["h_f"], s["h_b"], s["o_f"], s["o_b"], s["z"], lp("gla_head_norm"))
    plan = {"f": ([("w_mlp_down", 1)], [("w_mlp_down", 2), ("w_mlp_up", 0)]), "b": ([("w_mlp_up", 1)], [("w_mlp_up", 2)])}
    dqk, dv, dzg, dxc = {}, {}, {}, {}
    for d, name in enumerate(DIRS):
        dqk[name], dv[name], dzg[name], g["wg_" + name], g["gla_bg_" + name] = hosted(
            _gla_scan_bwd, plan[name][0], do, s["z"], s["s_" + name], lp("wg", d), lp("gla_bg", d), d == 1)
        dxc[name], g["wcat_" + name], g["lru_bias_" + name], g["lru_lambda_" + name] = hosted(
            _lru_scan_bwd, plan[name][1], dhs, s["xc"], s["h_" + name], lp("wcat", d), lp("lru_bias", d), lp("lru_lam", d),
            d == 1)
    _send_grad(outbox, "w_out", l, grad_w_out)
    dxbr, g["conv_w"], g["conv_b"] = _conv_bwd(dxc["f"], dxc["b"], s["z"], lp("conv_w"))
    dz, dh_in, g["norm_mix_pre"], *token_grad = hosted(
        _in_proj_bwd, [("w_out", group) for group in range(len(PEER_GROUPS))],
        (dxbr, dgate, dqk["f"], dqk["b"], dv["f"], dv["b"], dgout, dzg["f"], dzg["b"]),
        p["w_in"][l], s["h"], lp("norm_mix_pre"), dh_mid, l == 0)
    return dh_in, g, dz, token_grad


def _send_grad(outbox, nm, l, slabs):
    outbox.put((nm, l), slabs, _slab, _sds(slabs.shape, slabs.dtype))


def _w_in_slabs(grad_w_in):
    shard = D_IN // N_DEV
    return jnp.stack([grad_w_in[:, j * shard:(j + 1) * shard] for j in range(N_DEV)])


def _folded_block(hd):
    return slice((hd // 2) * LRU_HD, (hd // 2 + 1) * LRU_HD), slice((hd % 2) * LRU_HD, (hd % 2 + 1) * LRU_HD)


def _pack_small_grads(grads, dh0, depth):
    per_layer = ("norm_mix_pre", "norm_mix_post", "norm_mlp_pre", "norm_mlp_post", "conv_b", "gla_head_norm",
                 "lru_bias_f", "lru_bias_b", "lru_lambda_f", "lru_lambda_b", "gla_bg_f", "gla_bg_b",
                 "wcat_f", "wcat_b", "conv_w", "wg_f", "wg_b")
    ins = [grads[l][nm] for l in range(depth) for nm in per_layer]
    k = len(per_layer)
    meta_rows = PAD_ROWS // N_META

    def body(*refs):
        g = [dict(zip(per_layer, refs[l * k:(l + 1) * k])) for l in range(depth)]
        dh0_ref = refs[depth * k]
        norms, v512, v256, mats, convw, wgf, wgb, meta = refs[depth * k + 1:]
        v256[...] = jnp.zeros_like(v256)
        for l in range(depth):
            for p_, nm in enumerate(NORM_NAMES):
                norms[pl.ds(2 * p_ + l, 1), :] = g[l][nm][...]
            rows512 = [g[l]["conv_b"][...], g[l]["lru_bias_f"][:, 0:LRU_W], g[l]["lru_bias_f"][:, LRU_W:2 * LRU_W],
                       g[l]["lru_lambda_f"][...], g[l]["lru_bias_b"][:, 0:LRU_W], g[l]["lru_bias_b"][:, LRU_W:2 * LRU_W],
                       g[l]["lru_lambda_b"][...], g[l]["gla_head_norm"][...]]
            for p_, row in enumerate(rows512):
                v512[pl.ds(2 * p_ + l, 1), :] = row
            for p_, nm in enumerate(("gla_bg_f", "gla_bg_b")):
                v256[pl.ds(2 * p_ + l, 1), :] = g[l][nm][...]
            for d, name in enumerate(DIRS):
                for hd in range(LRU_HEADS):
                    rs = slice(hd * LRU_HD, (hd + 1) * LRU_HD)
                    dst_rows, dst_cols = _folded_block(hd)
                    mats[2 * d, l, dst_rows, dst_cols] = g[l]["wcat_" + name][rs, hd * LRU_HD:(hd + 1) * LRU_HD].astype(BF16)
                    mats[2 * d + 1, l, dst_rows, dst_cols] = (
                        g[l]["wcat_" + name][rs, LRU_W + hd * LRU_HD:LRU_W + (hd + 1) * LRU_HD].astype(BF16))
            for j in range(N_DEV):
                convw[j, l] = g[l]["conv_w"][0:4, j * 64:(j + 1) * 64]
                wgf[j, l] = g[l]["wg_f"][0:GLA_RANK, j * 32:(j + 1) * 32]
                wgb[j, l] = g[l]["wg_b"][GLA_RANK:2 * GLA_RANK, j * 32:(j + 1) * 32]
        for j in range(N_DEV):
            meta[j] = dh0_ref[:, j * LANES:(j + 1) * LANES]

    out_shape = [_sds((8, D_MODEL), F32), _sds((16, LRU_W), F32), _sds((8, GLA_QK), F32),
                 _sds((4, depth, LRU_W // 2, 2 * LRU_HD), BF16),
                 _sds((N_DEV, depth, 4, 64), F32), _sds((N_DEV, depth, GLA_RANK, 32), F32), _sds((N_DEV, depth, GLA_RANK, 32), F32),
                 _sds((N_DEV, N_META, LANES), F32)]
    return pl.pallas_call(
        body, name="pack_small_grads", grid=(1,),
        in_specs=[VMEM_SPEC] * (depth * k) + [pl.BlockSpec((N_META, D_MODEL), lambda i: (meta_rows, 0))],
        out_specs=[VMEM_SPEC] * len(out_shape), out_shape=out_shape, compiler_params=_params(("arbitrary",), 32),
    )(*ins, dh0)


def _my_index():
    return 4 * lax.axis_index("x") + 2 * lax.axis_index("y") + lax.axis_index("c")


def _peer(k):
    x, y, c = lax.axis_index("x"), lax.axis_index("y"), lax.axis_index("c")
    px = x ^ ((k >> 2) & 1)
    py = y ^ ((k >> 1) & 1)
    pc = c ^ (k & 1)
    return (px, py, pc), 4 * px + 2 * py + pc


ALL_PEERS = tuple(range(1, N_DEV))
PEER_GROUPS = ((1, 2, 3), (4, 5), (6, 7))


class _Exchange:
    def __init__(self):
        self.inputs, self.landings, self.transfers = [], [], []

    def add(self, array, src, landing, dst, peers=ALL_PEERS, local=True):
        if isinstance(landing, int):
            out = landing
        else:
            out = len(self.landings)
            self.landings.append(landing)
        self.transfers.append((len(self.inputs), src, out, dst, tuple(peers), local))
        self.inputs.append(array)
        return out

    def _pairs(self):
        return [(t, k) for t, tr in enumerate(self.transfers) for k in tr[4]]

    def _locals(self):
        return [t for t, tr in enumerate(self.transfers) if tr[5]]

    def out_shapes(self):
        return [g if isinstance(g, jax.ShapeDtypeStruct) else _sds(g.shape, g.dtype) for g in self.landings]

    def continued(self):
        return [(b, g) for b, g in enumerate(self.landings) if not isinstance(g, jax.ShapeDtypeStruct)]

    def sem_shapes(self):
        return [pltpu.SemaphoreType.DMA((max(len(self._pairs()), 1),)), pltpu.SemaphoreType.DMA((max(len(self._pairs()), 1),)),
                pltpu.SemaphoreType.DMA((max(len(self._locals()), 1),))]

    def _local(self, ins, outs, sems):
        me = _my_index()
        copies = []
        for s, t in enumerate(self._locals()):
            a, src, b, dst, _, _ = self.transfers[t]
            copies.append(pltpu.make_async_copy(src(ins[a], me), dst(outs[b], me), sems[2].at[s]))
        return copies

    def _remote(self, ins, outs, sems, sending):
        copies = []
        for s, (t, k) in enumerate(self._pairs()):
            a, src, b, dst, _, _ = self.transfers[t]
            peer, peer_index = _peer(k)
            copies.append(pltpu.make_async_remote_copy(
                src_ref=src(ins[a], peer_index), dst_ref=dst(outs[b], _my_index() if sending else peer_index),
                send_sem=sems[0].at[s], recv_sem=sems[1].at[s], device_id=peer, device_id_type=MESH_ID))
        return copies

    def start(self, ins, outs, sems):
        for cp in self._local(ins, outs, sems) + self._remote(ins, outs, sems, True):
            cp.start()

    def wait(self, ins, outs, sems):
        for cp in self._remote(ins, outs, sems, False):
            cp.wait_recv()
        for cp in self._remote(ins, outs, sems, True):
            cp.wait_send()
        for cp in self._local(ins, outs, sems):
            cp.wait()

    def run(self, name):
        return _hosting_call(self, None, name=name, grid=(), in_specs=[], out_specs=[], out_shape=[], scratch_shapes=[],
                             compiler_params=pltpu.CompilerParams(has_side_effects=True))()


def _hosting_call(exchange, body, *, name, grid, in_specs, out_specs, out_shape, scratch_shapes, compiler_params):
    if exchange is None or not exchange.transfers:
        return pl.pallas_call(body, name=name, grid=grid, in_specs=in_specs, out_specs=out_specs, out_shape=out_shape,
                              scratch_shapes=scratch_shapes, compiler_params=compiler_params)
    n_in, n_out, n_scr = len(in_specs), len(out_specs), len(scratch_shapes)
    x_in, x_out = len(exchange.inputs), len(exchange.landings)
    continued = exchange.continued()

    def hosted(*refs):
        ins, x_ins = refs[:n_in], refs[n_in:n_in + x_in]
        o0 = n_in + x_in + len(continued)
        outs, x_outs = refs[o0:o0 + n_out], refs[o0 + n_out:o0 + n_out + x_out]
        s0 = o0 + n_out + x_out
        scratch, sems = refs[s0:s0 + n_scr], refs[s0 + n_scr:]
        if body is None:
            exchange.start(x_ins, x_outs, sems)
            exchange.wait(x_ins, x_outs, sems)
            return
        ids = [pl.program_id(a) for a in range(len(grid))]
        first = functools.reduce(jnp.logical_and, [i == 0 for i in ids])
        last = functools.reduce(jnp.logical_and, [i == g - 1 for i, g in zip(ids, grid)])

        @pl.when(first)
        def _():
            exchange.start(x_ins, x_outs, sems)

        body(*ins, *outs, *scratch)

        @pl.when(last)
        def _():
            exchange.wait(x_ins, x_outs, sems)

    aliases = {n_in + x_in + i: n_out + b for i, (b, _) in enumerate(continued)}
    kwargs = dict(grid=grid) if grid else {}
    call = pl.pallas_call(
        hosted, name=name, in_specs=list(in_specs) + [ANY_SPEC] * (x_in + len(continued)),
        out_specs=list(out_specs) + [ANY_SPEC] * x_out, out_shape=list(out_shape) + exchange.out_shapes(),
        scratch_shapes=list(scratch_shapes) + exchange.sem_shapes(), compiler_params=compiler_params,
        input_output_aliases=aliases, **kwargs)
    return lambda *operands: call(*operands, *exchange.inputs, *[g for _, g in continued])


def _whole(ref, j):
    return ref


def _slab(ref, j):
    return ref.at[j]


def _layer_of(l):
    return lambda ref, j: ref.at[l]


def _adamw(g, w, m, v):
    nm = ADAM_B1 * m + (1.0 - ADAM_B1) * g
    nv = ADAM_B2 * v + (1.0 - ADAM_B2) * jnp.square(g)
    m_hat = nm / (1.0 - ADAM_B1 ** ADAM_STEP)
    v_hat = nv / (1.0 - ADAM_B2 ** ADAM_STEP)
    return -ADAM_LR * (m_hat / (jnp.sqrt(v_hat) + ADAM_EPS) + ADAM_WD * w), nm, nv


def _sum_parts(p_ref):
    g = p_ref[0].astype(F32)
    for j in range(1, N_DEV):
        g = g + p_ref[j].astype(F32)
    return g


def _adamw_sharded(parts, w, m, v, name):
    shape = w.shape
    lead, (rows, cols) = shape[:-2], shape[-2:]
    tr = min(rows, ROW_BLOCK)
    assert rows % tr == 0
    steps = rows // tr
    nl = len(lead)
    spec = pl.BlockSpec((None,) * nl + (tr, cols), lambda *idx: idx + (0,))
    per_layer = isinstance(parts, (list, tuple))
    if per_layer:
        def part_spec(l):
            return pl.BlockSpec((N_DEV, tr, cols), lambda li, r: (0, jnp.where(li == l, r, jnp.where(li < l, 0, steps - 1)), 0))
        part_specs = [part_spec(l) for l in range(len(parts))]
    else:
        parts = [parts]
        part_specs = [pl.BlockSpec((N_DEV,) + (None,) * nl + (tr, cols), lambda *idx: (0,) + idx + (0,))]
    count = len(parts)

    def body(*refs):
        p_refs = refs[:count]
        w_ref, m_ref, v_ref, g_ref, d_ref, nm_ref, nv_ref = refs[count:]

        def update(p_ref):
            g = _sum_parts(p_ref)
            g_ref[...] = g
            d_ref[...], nm_ref[...], nv_ref[...] = _adamw(g, w_ref[...], m_ref[...], v_ref[...])

        if per_layer:
            for l in range(count):
                pl.when(pl.program_id(0) == l)(functools.partial(update, p_refs[l]))
        else:
            update(p_refs[0])

    return pl.pallas_call(
        body, name=name, grid=lead + (steps,),
        in_specs=part_specs + [spec, spec, spec], out_specs=[spec] * 4, out_shape=[_sds(shape, F32)] * 4,
        compiler_params=_params(("arbitrary",) * (nl + 1)),
    )(*parts, w, m, v)


def _adamw_replicated(gathered, w, m, v, depth):
    names = NORM_NAMES + VEC512_NAMES + VEC256_NAMES + LRU_MAT_NAMES
    count = len(names)

    def body(*refs):
        norms, v512, v256, mats = refs[:4]
        w_refs, m_refs, v_refs = (refs[4 + t * count:4 + (t + 1) * count] for t in range(3))
        outs = refs[4 + 3 * count:4 + 7 * count]
        sum_norms, sum_512, sum_256, unfolded = refs[4 + 7 * count:]
        sum_norms[...] = _sum_parts(norms)
        sum_512[...] = _sum_parts(v512)
        sum_256[...] = _sum_parts(v256)
        for n_, nm in enumerate(names):
            if nm in NORM_NAMES:
                g = sum_norms[pl.ds(depth * NORM_NAMES.index(nm), depth), :]
            elif nm in VEC512_NAMES:
                g = sum_512[pl.ds(depth * VEC512_NAMES.index(nm), depth), :]
            elif nm in VEC256_NAMES:
                g = sum_256[pl.ds(depth * VEC256_NAMES.index(nm), depth), :]
            else:
                p_ = LRU_MAT_NAMES.index(nm)
                folded = mats[0, p_].astype(F32)
                for j in range(1, N_DEV):
                    folded = folded + mats[j, p_].astype(F32)
                for hd in range(LRU_HEADS):
                    src_rows, src_cols = _folded_block(hd)
                    unfolded[:, hd * LRU_HD:(hd + 1) * LRU_HD, :] = folded[:, src_rows, src_cols]
                g = unfolded[...]
            delta, nm_, nv_ = _adamw(g, w_refs[n_][...], m_refs[n_][...], v_refs[n_][...])
            outs[n_][...] = g
            outs[count + n_][...] = delta
            outs[2 * count + n_][...] = nm_
            outs[3 * count + n_][...] = nv_

    shapes = [_sds(w[nm].shape, F32) for nm in names]
    ins = list(gathered) + [t[nm] for t in (w, m, v) for nm in names]
    outs = pl.pallas_call(
        body, name="adamw_replicated", in_specs=[VMEM_SPEC] * len(ins), out_specs=[VMEM_SPEC] * (4 * count),
        out_shape=shapes * 4,
        scratch_shapes=[pltpu.VMEM(gathered[0].shape[1:], F32), pltpu.VMEM(gathered[1].shape[1:], F32),
                        pltpu.VMEM(gathered[2].shape[1:], F32), pltpu.VMEM((depth, LRU_W, LRU_HD), F32)],
        compiler_params=_params(None, 48),
    )(*ins)
    return [dict(zip(names, outs[t * count:(t + 1) * count])) for t in range(4)]


WEIGHT_NAMES = ("meta_tokens", "norm_mix_pre", "norm_mix_post", "norm_mlp_pre", "norm_mlp_post", "w_in", "conv_w", "conv_b",
                "lru_wa_f", "lru_ba_f", "lru_wx_f", "lru_bx_f", "lru_lambda_f", "lru_wa_b", "lru_ba_b", "lru_wx_b",
                "lru_bx_b", "lru_lambda_b", "gla_wg_f", "gla_bg_f", "gla_wg_b", "gla_bg_b", "gla_head_norm", "w_out",
                "w_mlp_up", "w_mlp_down")
MATMUL_WEIGHTS = ("w_in", "w_out", "w_mlp_up", "w_mlp_down")
SMALL_SHARDED = ("conv_w", "gla_wg_f", "gla_wg_b", "meta_tokens")


def kernel(x, meta_tokens, norm_mix_pre, norm_mix_post, norm_mlp_pre, norm_mlp_post, w_in, conv_w, conv_b, lru_wa_f, lru_ba_f, lru_wx_f, lru_bx_f, lru_lambda_f, lru_wa_b, lru_ba_b, lru_wx_b, lru_bx_b, lru_lambda_b, gla_wg_f, gla_bg_f, gla_wg_b, gla_bg_b, gla_head_norm, w_out, w_mlp_up, w_mlp_down, loss_target, m_meta_tokens, m_norm_mix_pre, m_norm_mix_post, m_norm_mlp_pre, m_norm_mlp_post, m_w_in, m_conv_w, m_conv_b, m_lru_wa_f, m_lru_ba_f, m_lru_wx_f, m_lru_bx_f, m_lru_lambda_f, m_lru_wa_b, m_lru_ba_b, m_lru_wx_b, m_lru_bx_b, m_lru_lambda_b, m_gla_wg_f, m_gla_bg_f, m_gla_wg_b, m_gla_bg_b, m_gla_head_norm, m_w_out, m_w_mlp_up, m_w_mlp_down, v_meta_tokens, v_norm_mix_pre, v_norm_mix_post, v_norm_mlp_pre, v_norm_mlp_post, v_w_in, v_conv_w, v_conv_b, v_lru_wa_f, v_lru_ba_f, v_lru_wx_f, v_lru_bx_f, v_lru_lambda_f, v_lru_wa_b, v_lru_ba_b, v_lru_wx_b, v_lru_bx_b, v_lru_lambda_b, v_gla_wg_f, v_gla_bg_f, v_gla_wg_b, v_gla_bg_b, v_gla_head_norm, v_w_out, v_w_mlp_up, v_w_mlp_down):
    args = locals()
    w = {nm: args[nm] for nm in WEIGHT_NAMES}
    m = {nm: args["m_" + nm] for nm in WEIGHT_NAMES}
    v = {nm: args["v_" + nm] for nm in WEIGHT_NAMES}
    depth = w_in.shape[0]

    shards = {nm: w[nm].astype(BF16) for nm in MATMUL_WEIGHTS}
    p = {}
    gather = _Outbox(_install_weight(p))
    _request_weight(gather, shards, "w_in", 0)
    _request_weight(gather, shards, "w_out", 0)
    ex, tickets = gather.exchange([("w_in", group) for group in range(len(PEER_GROUPS))])
    first_small = len(ex.landings)
    for nm in SMALL_SHARDED:
        ex.add(w[nm], _whole, _sds((N_DEV,) + w[nm].shape, F32), _slab)
    landed = ex.run("all_gather")
    gather.store(tickets, landed)
    p.update(_prepare_params(w, dict(zip(SMALL_SHARDED, landed[first_small:])), depth))

    h = jnp.concatenate([jnp.zeros((PAD_ROWS, D_MODEL), F32), p["meta_tokens"], x[0]], axis=0)
    saved = []
    for l in range(depth):
        h, s = _layer_fwd(h, l, p, gather, shards, depth)
        saved.append(s)
    dh, loss_part = _loss_and_grad(h, loss_target[0])
    loss = lax.psum(loss_part[0, 0], ("x", "y", "c"))

    received = {}
    outbox = _Outbox(received.__setitem__)
    grads = [None] * depth
    for l in reversed(range(depth)):
        dh, grads[l], dz, token_grad = _layer_bwd(dh, l, p, saved[l], outbox)
        if l > 0:
            _send_grad(outbox, "w_in", l, _w_in_slabs(_matmul_tn(saved[l]["hn"], dz, "grad_w_in")))
    grad_x = token_grad[0][None]

    small = _pack_small_grads(grads, dh, depth)
    rep_bufs, small_slabs = small[:4], small[4:]
    ex, tickets = outbox.exchange()
    first_small = len(ex.landings)
    for g in small_slabs:
        ex.add(g, _slab, _sds(g.shape, F32), _slab)
    for g in rep_bufs:
        ex.add(g, _whole, _sds((N_DEV,) + g.shape, g.dtype), _slab)
    grad_w_in, *landed = _matmul_tn(saved[0]["hn"], dz, "grad_w_in", exchange=ex)
    outbox.store(tickets, landed)
    small_received = landed[first_small:first_small + len(small_slabs)]
    rep_received = landed[first_small + len(small_slabs):]
    _send_grad(outbox, "w_in", 0, _w_in_slabs(grad_w_in))
    ex, tickets = outbox.exchange()
    outbox.store(tickets, ex.run("exchange_grads"))

    results = [{}, {}, {}, {}]
    for nm in MATMUL_WEIGHTS:
        parts = [received[(nm, l)] for l in range(depth)]
        for t, out in enumerate(_adamw_sharded(parts, w[nm], m[nm], v[nm], "adamw_" + nm)):
            results[t][nm] = out
    for nm, parts in zip(SMALL_SHARDED, small_received):
        for t, out in enumerate(_adamw_sharded(parts, w[nm], m[nm], v[nm], "adamw_" + nm)):
            results[t][nm] = out

    def kernel_side(tree):
        return {nm: tree[nm].reshape(depth, LRU_W, LRU_HD) if nm in LRU_MAT_NAMES else tree[nm]
                for nm in NORM_NAMES + VEC512_NAMES + VEC256_NAMES + LRU_MAT_NAMES}

    for t, tree in enumerate(_adamw_replicated(rep_received, kernel_side(w), kernel_side(m), kernel_side(v), depth)):
        for nm, out in tree.items():
            results[t][nm] = out.reshape(w[nm].shape)
    return (loss, grad_x, *[results[t][nm] for t in range(4) for nm in WEIGHT_NAMES])
```
